```python
import math
import jax, jax.numpy as jnp
from jax import lax
import numpy as np

D_MODEL = 1024
BATCH = 2
SEQ = 8192
DEPTH = 1
DEC_BATCH = 128
DEC_SEQ = 8
PAST_LEN = 8192
PAGE_SIZE = 128

HEAD_DIM = 64
MIX_WIDTH = D_MODEL
N_MLSTM_HEADS = (MIX_WIDTH // 2) // HEAD_DIM
N_ATTN_HEADS = (MIX_WIDTH // 2) // HEAD_DIM
N_KV_HEADS = max(1, N_ATTN_HEADS // 4)
GQA_GROUP = N_ATTN_HEADS // N_KV_HEADS
WINDOW = 128
ATTN_BLOCK = WINDOW
MLSTM_CHUNK = 64
N_EXPERTS = 32
TOP_K = 4
D_FF = D_MODEL
SWIGLU_LIMIT = 7.0
SWIGLU_ALPHA = 1.702
MOE_BLOCK = 128
RMS_EPS = 1e-5

DM = N_MLSTM_HEADS * HEAD_DIM
DA = N_ATTN_HEADS * HEAD_DIM
DKV = N_KV_HEADS * HEAD_DIM
IN_SIZES = (DM, DM, DM, DM, N_MLSTM_HEADS, N_MLSTM_HEADS, DA, DKV, DKV)
IN_SPLIT_POINTS = tuple(int(s) for s in np.cumsum(IN_SIZES)[:-1])
D_IN = int(sum(IN_SIZES))

kernel_name = 'hymba_mlstm_swa_sink_alibi_moe_step'


def rms_norm(x, g):
    xf = x.astype(jnp.float32)
    y = xf * lax.rsqrt(jnp.mean(xf * xf, axis=-1, keepdims=True) + RMS_EPS)
    return (y * g.astype(jnp.float32)).astype(x.dtype)


def alibi_slopes(n):
    return jnp.asarray(np.array([2.0 ** (-8.0 * (h + 1) / n) for h in range(n)], dtype=np.float32))


def mlstm_chunkwise(q, k, v, i_pre, logf, c0, n0, m0):
    B, T, H, Dh = q.shape
    L = math.gcd(T, MLSTM_CHUNK)
    NC = T // L

    def chunks(a):
        a = a.astype(jnp.float32).reshape((B, NC, L, H) + a.shape[3:])
        return jnp.moveaxis(a, (1, 3), (0, 2))

    causal = jnp.arange(L)[:, None] >= jnp.arange(L)[None, :]

    def step(carry, xs):
        c, n, m = carry
        qc, kc, vc, ic, fc = xs
        b = jnp.cumsum(fc, axis=-1)
        log_d = jnp.where(causal, b[..., :, None] - b[..., None, :] + ic[..., None, :], -jnp.inf)
        m_inter = m[..., None] + b
        m_t = jnp.maximum(m_inter, jnp.max(log_d, axis=-1))
        s = jnp.einsum('bhtd,bhsd->bhts', qc, kc) * jnp.exp(log_d - m_t[..., None])
        inter = jnp.exp(m_inter - m_t)
        num = jnp.einsum('bhts,bhsd->bhtd', s, vc) + inter[..., None] * jnp.einsum('bhtk,bhkv->bhtv', qc, c)
        den = jnp.sum(s, axis=-1) + inter * jnp.einsum('bhtk,bhk->bht', qc, n)
        h = num / jnp.maximum(jnp.abs(den), jnp.exp(-m_t))[..., None]
        m_new = m_t[..., -1]
        decay = jnp.exp(m + b[..., -1] - m_new)
        kw = kc * jnp.exp(b[..., -1:] - b + ic - m_new[..., None])[..., None]
        c_new = decay[..., None, None] * c + jnp.einsum('bhsk,bhsv->bhkv', kw, vc)
        n_new = decay[..., None] * n + jnp.sum(kw, axis=-2)
        return (c_new, n_new, m_new), h

    init = (c0.astype(jnp.float32), n0.astype(jnp.float32), m0.astype(jnp.float32))
    (c, n, m), h = lax.scan(step, init, (chunks(q), chunks(k), chunks(v), chunks(i_pre), chunks(logf)))
    h = jnp.moveaxis(h, (0, 2), (1, 3)).reshape(B, T, H, Dh)
    return h, c, n, m


def sink_probs(scores, valid, sinks):
    scores = jnp.where(valid, scores, -jnp.inf)
    sink = sinks.astype(jnp.float32).reshape(N_KV_HEADS, GQA_GROUP, 1, 1)
    mx = jnp.maximum(jnp.max(scores, axis=-1, keepdims=True), sink)
    p = jnp.exp(scores - mx)
    return p / (jnp.sum(p, axis=-1, keepdims=True) + jnp.exp(sink - mx))


def swa_prompt(q, k, v, sinks):
    B, T, Hq, Dh = q.shape
    nb = T // ATTN_BLOCK
    slopes = alibi_slopes(N_ATTN_HEADS).reshape(N_KV_HEADS, GQA_GROUP, 1, 1)
    qb = q.astype(jnp.float32).reshape(B, nb, ATTN_BLOCK, N_KV_HEADS, GQA_GROUP, Dh)
    pad = jnp.zeros((B, ATTN_BLOCK, N_KV_HEADS, Dh), jnp.float32)
    kp = jnp.concatenate([pad, k.astype(jnp.float32)], axis=1).reshape(B, nb + 1, ATTN_BLOCK, N_KV_HEADS, Dh)
    vp = jnp.concatenate([pad, v.astype(jnp.float32)], axis=1).reshape(B, nb + 1, ATTN_BLOCK, N_KV_HEADS, Dh)
    kb = jnp.concatenate([kp[:, :-1], kp[:, 1:]], axis=2)
    vb = jnp.concatenate([vp[:, :-1], vp[:, 1:]], axis=2)
    dist = jnp.arange(ATTN_BLOCK)[:, None] + ATTN_BLOCK - jnp.arange(2 * ATTN_BLOCK)[None, :]
    key_pos = (jnp.arange(nb)[:, None] - 1) * ATTN_BLOCK + jnp.arange(2 * ATTN_BLOCK)[None, :]
    valid = ((dist >= 0) & (dist < WINDOW))[None, :, :] & (key_pos >= 0)[:, None, :]
    scores = jnp.einsum('bnqhgd,bnkhd->bnhgqk', qb, kb) * (HEAD_DIM ** -0.5) - slopes * dist.astype(jnp.float32)
    p = sink_probs(scores, valid[None, :, None, None], sinks)
    o = jnp.einsum('bnhgqk,bnkhd->bnqhgd', p, vb)
    return o.reshape(B, T, Hq, Dh)


def swa_decode(q, k, v, k_buf, v_buf, sinks):
    Bd, Tn, Hq, Dh = q.shape
    W = k_buf.shape[1]
    slopes = alibi_slopes(N_ATTN_HEADS).reshape(N_KV_HEADS, GQA_GROUP, 1, 1)
    k_all = jnp.concatenate([k_buf.astype(k.dtype), k], axis=1)
    v_all = jnp.concatenate([v_buf.astype(v.dtype), v], axis=1)
    dist = jnp.arange(Tn)[:, None] + W - jnp.arange(W + Tn)[None, :]
    valid = (dist >= 0) & (dist < WINDOW)
    qg = q.astype(jnp.float32).reshape(Bd, Tn, N_KV_HEADS, GQA_GROUP, Dh)
    scores = jnp.einsum('bqhgd,bkhd->bhgqk', qg, k_all.astype(jnp.float32)) * (HEAD_DIM ** -0.5) - slopes * dist.astype(jnp.float32)
    p = sink_probs(scores, valid, sinks)
    o = jnp.einsum('bhgqk,bkhd->bqhgd', p, v_all.astype(jnp.float32))
    return o.reshape(Bd, Tn, Hq, Dh), k_all[:, -W:], v_all[:, -W:]


def moe_ffn(x, w_router, b_router, w_gate_up, b_gate_up, w_down, b_down):
    shp = x.shape
    xt = x.reshape(-1, D_MODEL)
    T = xt.shape[0]
    logits = (xt @ w_router).astype(jnp.float32) + b_router.astype(jnp.float32)
    top_val, top_idx = lax.top_k(logits, TOP_K)
    gates = jax.nn.softmax(top_val, axis=-1)
    n_assign = T * TOP_K
    flat_e = top_idx.reshape(-1).astype(jnp.int32)
    flat_tok = jnp.arange(n_assign, dtype=jnp.int32) // TOP_K
    flat_g = gates.reshape(-1)
    order = jnp.argsort(flat_e, stable=True)
    sorted_e = flat_e[order]
    counts = jnp.zeros((N_EXPERTS,), jnp.int32).at[flat_e].add(1)
    padded = (counts + MOE_BLOCK - 1) // MOE_BLOCK * MOE_BLOCK
    pad_end = jnp.cumsum(padded)
    pad_start = pad_end - padded
    start = jnp.cumsum(counts) - counts
    dest = pad_start[sorted_e] + jnp.arange(n_assign, dtype=jnp.int32) - start[sorted_e]
    n_blocks = -(-n_assign // MOE_BLOCK) + N_EXPERTS
    n_rows = n_blocks * MOE_BLOCK
    row_tok = jnp.full((n_rows,), T, jnp.int32).at[dest].set(flat_tok[order])
    row_gate = jnp.zeros((n_rows,), jnp.float32).at[dest].set(flat_g[order])
    block_start = jnp.arange(n_blocks, dtype=jnp.int32) * MOE_BLOCK
    block_exp = jnp.minimum(jnp.searchsorted(pad_end, block_start, side='right'), N_EXPERTS - 1).astype(jnp.int32)
    x_pad = jnp.concatenate([xt, jnp.zeros((1, D_MODEL), xt.dtype)], axis=0)

    def expert_block(args):
        tok, e = args
        xb = x_pad[tok]
        gu = xb @ w_gate_up[e] + b_gate_up[e]
        gate, up = jnp.split(gu, 2, axis=-1)
        gate = jnp.minimum(gate, SWIGLU_LIMIT)
        up = jnp.clip(up, -SWIGLU_LIMIT, SWIGLU_LIMIT)
        act = (up + 1.0) * (gate * jax.nn.sigmoid(gate * SWIGLU_ALPHA))
        return act @ w_down[e] + b_down[e]

    out = lax.map(expert_block, (row_tok.reshape(n_blocks, MOE_BLOCK), block_exp))
    out = out.reshape(n_rows, D_MODEL) * row_gate[:, None].astype(out.dtype)
    y = jnp.zeros((T + 1, D_MODEL), out.dtype).at[row_tok].add(out)[:T]
    return y.reshape(shp).astype(x.dtype)


def layer_forward(x, c0, n0, m0, k_buf, v_buf, g_mix, w_in, b_igate, b_fgate, g_head, attn_sinks,
                  w_out, g_ffn, w_router, b_router, w_gate_up, b_gate_up, w_down, b_down):
    B, T, _ = x.shape
    h = rms_norm(x, g_mix)
    q_m, k_m, v_m, o_m, i_m, f_m, q_a, k_a, v_a = jnp.split(h @ w_in, IN_SPLIT_POINTS, axis=-1)
    heads = lambda a, n: a.reshape(B, T, n, HEAD_DIM)
    i_pre = i_m.astype(jnp.float32) + b_igate.astype(jnp.float32)
    logf = jax.nn.log_sigmoid(f_m.astype(jnp.float32) + b_fgate.astype(jnp.float32))
    h_m, c, n, m = mlstm_chunkwise(heads(q_m, N_MLSTM_HEADS), heads(k_m, N_MLSTM_HEADS) * (HEAD_DIM ** -0.5),
                                   heads(v_m, N_MLSTM_HEADS), i_pre, logf, c0, n0, m0)
    h_m = rms_norm(h_m, g_head) * jax.nn.sigmoid(heads(o_m, N_MLSTM_HEADS).astype(jnp.float32))
    qa, ka, va = heads(q_a, N_ATTN_HEADS), heads(k_a, N_KV_HEADS), heads(v_a, N_KV_HEADS)
    if k_buf is None:
        h_a = swa_prompt(qa, ka, va, attn_sinks)
        w_keep = min(WINDOW, T)
        k_new, v_new = ka[:, -w_keep:], va[:, -w_keep:]
    else:
        h_a, k_new, v_new = swa_decode(qa, ka, va, k_buf, v_buf, attn_sinks)
    mix = jnp.concatenate([h_m.reshape(B, T, DM), h_a.reshape(B, T, DA)], axis=-1).astype(x.dtype)
    x = x + mix @ w_out
    x = x + moe_ffn(rms_norm(x, g_ffn), w_router, b_router, w_gate_up, b_gate_up, w_down, b_down)
    return x, (k_new, v_new, c, n, m)


def setup_inputs(seed: int = 0) -> dict:
    key = jax.random.key(seed)
    ks = iter(jax.random.split(key, 32))
    nrm = lambda shape, scale: scale * jax.random.normal(next(ks), shape, jnp.float32)
    w_buf = min(WINDOW, PAST_LEN)
    return {
        'x_prompt': nrm((BATCH, SEQ, D_MODEL), 1.0),
        'x_sample': nrm((DEC_BATCH, DEC_SEQ, D_MODEL), 1.0),
        'cache_swa_k': nrm((DEPTH, DEC_BATCH, w_buf, N_KV_HEADS, HEAD_DIM), 1.0),
        'cache_swa_v': nrm((DEPTH, DEC_BATCH, w_buf, N_KV_HEADS, HEAD_DIM), 1.0),
        'state_mlstm_c': nrm((DEPTH, DEC_BATCH, N_MLSTM_HEADS, HEAD_DIM, HEAD_DIM), 0.3),
        'state_mlstm_n': nrm((DEPTH, DEC_BATCH, N_MLSTM_HEADS, HEAD_DIM), 0.3),
        'state_mlstm_m': nrm((DEPTH, DEC_BATCH, N_MLSTM_HEADS), 1.0),
        'g_mix': 1.0 + nrm((DEPTH, D_MODEL), 0.02),
        'w_in': nrm((DEPTH, D_MODEL, D_IN), D_MODEL ** -0.5),
        'b_igate': nrm((DEPTH, N_MLSTM_HEADS), 0.1),
        'b_fgate': jnp.linspace(3.0, 6.0, N_MLSTM_HEADS, dtype=jnp.float32)[None, :] + nrm((DEPTH, N_MLSTM_HEADS), 0.1),
        'g_head': 1.0 + nrm((DEPTH, N_MLSTM_HEADS, HEAD_DIM), 0.02),
        'attn_sinks': nrm((DEPTH, N_ATTN_HEADS), 0.5),
        'w_out': nrm((DEPTH, DM + DA, D_MODEL), (DM + DA) ** -0.5),
        'g_ffn': 1.0 + nrm((DEPTH, D_MODEL), 0.02),
        'w_router': nrm((DEPTH, D_MODEL, N_EXPERTS), D_MODEL ** -0.5),
        'b_router': nrm((DEPTH, N_EXPERTS), 0.01),
        'w_gate_up': nrm((DEPTH, N_EXPERTS, D_MODEL, 2 * D_FF), D_MODEL ** -0.5),
        'b_gate_up': nrm((DEPTH, N_EXPERTS, 2 * D_FF), 0.02),
        'w_down': nrm((DEPTH, N_EXPERTS, D_FF, D_MODEL), D_FF ** -0.5),
        'b_down': nrm((DEPTH, N_EXPERTS, D_MODEL), 0.02),
        'g_final': 1.0 + nrm((D_MODEL,), 0.02),
    }


def reference(x_prompt, x_sample, cache_swa_k, cache_swa_v, state_mlstm_c, state_mlstm_n, state_mlstm_m,
              g_mix, w_in, b_igate, b_fgate, g_head, attn_sinks, w_out, g_ffn, w_router, b_router,
              w_gate_up, b_gate_up, w_down, b_down, g_final):
    B = x_prompt.shape[0]
    zc = jnp.zeros((B, N_MLSTM_HEADS, HEAD_DIM, HEAD_DIM), jnp.float32)
    zn = jnp.zeros((B, N_MLSTM_HEADS, HEAD_DIM), jnp.float32)
    zm = jnp.zeros((B, N_MLSTM_HEADS), jnp.float32)
    xp, xs = x_prompt, x_sample
    kp_l, vp_l, cp_l, np_l, mp_l = [], [], [], [], []
    ks_l, vs_l, cs_l, ns_l, ms_l = [], [], [], [], []
    for l in range(DEPTH):
        lw = (g_mix[l], w_in[l], b_igate[l], b_fgate[l], g_head[l], attn_sinks[l], w_out[l], g_ffn[l],
              w_router[l], b_router[l], w_gate_up[l], b_gate_up[l], w_down[l], b_down[l])
        xp, (kp, vp, cp, n_p, mp) = layer_forward(xp, zc, zn, zm, None, None, *lw)
        xs, (ks_, vs_, cs_, ns_, ms_) = layer_forward(xs, state_mlstm_c[l], state_mlstm_n[l], state_mlstm_m[l],
                                                     cache_swa_k[l], cache_swa_v[l], *lw)
        kp_l.append(kp); vp_l.append(vp); cp_l.append(cp); np_l.append(n_p); mp_l.append(mp)
        ks_l.append(ks_); vs_l.append(vs_); cs_l.append(cs_); ns_l.append(ns_); ms_l.append(ms_)
    y_prompt = rms_norm(xp, g_final)
    y_sample = rms_norm(xs, g_final)
    return (y_prompt, y_sample,
            jnp.stack(kp_l), jnp.stack(vp_l), jnp.stack(cp_l), jnp.stack(np_l), jnp.stack(mp_l),
            jnp.stack(ks_l), jnp.stack(vs_l), jnp.stack(cs_l), jnp.stack(ns_l), jnp.stack(ms_l))
```

```python
import functools
import math

import jax
import jax.numpy as jnp
import numpy as np
from jax import lax
from jax.experimental import pallas as pl
from jax.experimental.pallas import tpu as pltpu

F32 = jnp.float32
BF16 = jnp.bfloat16
HIGHEST = lax.Precision.HIGHEST

D_MODEL = 1024
HEAD_DIM = 64
N_HEADS = 8
N_PAIRS = N_HEADS // 2
N_KV = 2
GROUP = N_HEADS // N_KV
WINDOW = 128
N_EXPERTS = 32
TOP_K = 4
D_FF = 1024
SWIGLU_LIMIT = 7.0
SWIGLU_ALPHA = 1.702
RMS_EPS = 1e-5
DM = N_HEADS * HEAD_DIM
DKV = N_KV * HEAD_DIM
NEG = -1e30

LANES = 128
SUBLANES = 8
VMEM_LIMIT = 56 * 1024 * 1024

TM = 512
MLSTM_TL = 512
MLSTM_L = 128
DEC_G = 8
MOE_BM = 256
DISP_T = 512
COMB_T = 256


def _cparams(sem, vmem=None):
    return pltpu.CompilerParams(dimension_semantics=sem, vmem_limit_bytes=vmem)


def _rms(x, g):
    return x * lax.rsqrt(jnp.mean(x * x, axis=-1, keepdims=True) + RMS_EPS) * g


def _log_sigmoid(z):
    return jnp.minimum(z, 0.0) - jnp.log(1.0 + jnp.exp(-jnp.abs(z)))


def _sigmoid(z):
    return 1.0 / (1.0 + jnp.exp(-z))


def _inproj_kernel(n_ptiles, xp_ref, xs_ref, g_ref, w1_ref, wkt_ref, wg_ref, wgt_ref, bcol_ref, brow_ref,
                   qm_ref, vm_ref, om_ref, qa_ref, ka_ref, va_ref, kmt_ref, gcol_ref, grow_ref):
    i = pl.program_id(0)
    x = jnp.where(i < n_ptiles, xp_ref[...], xs_ref[...])
    h = _rms(x, g_ref[...]).astype(BF16)
    main = jnp.dot(h, w1_ref[...], preferred_element_type=F32)
    qm_ref[...] = main[:, 0:DM]
    vm_ref[...] = main[:, DM:2 * DM]
    om_ref[...] = main[:, 2 * DM:3 * DM]
    qa_ref[...] = main[:, 3 * DM:4 * DM]
    ka_ref[...] = main[:, 4 * DM:4 * DM + DKV]
    va_ref[...] = main[:, 4 * DM + DKV:4 * DM + 2 * DKV]
    kt = lax.dot_general(wkt_ref[...], h, (((1,), (1,)), ((), ())), preferred_element_type=F32)
    kmt_ref[...] = kt * (HEAD_DIM ** -0.5)
    zc = jnp.dot(h, wg_ref[...], preferred_element_type=F32) + bcol_ref[...]
    lane = lax.broadcasted_iota(jnp.int32, zc.shape, 1)
    gcol_ref[...] = jnp.where(lane < N_HEADS, zc, _log_sigmoid(zc))
    zr = lax.dot_general(wgt_ref[...], h, (((1,), (1,)), ((), ())), preferred_element_type=F32) + brow_ref[...]
    row = lax.broadcasted_iota(jnp.int32, zr.shape, 0)
    grow_ref[...] = jnp.where(row < N_HEADS, zr, _log_sigmoid(zr))


def _inproj(xp, xs, g_mix, w1, wkt, wg, wgt, bcol, brow):
    tp, ts = xp.shape[0], xs.shape[0]
    n_pt, n_st = tp // TM, ts // TM
    t_all = tp + ts
    n1 = w1.shape[1]
    tok = lambda w: pl.BlockSpec((TM, w), lambda i: (i, 0))
    full = lambda a: pl.BlockSpec(a.shape, lambda i: (0,) * a.ndim)
    out_shape = (
        jax.ShapeDtypeStruct((t_all, DM), F32), jax.ShapeDtypeStruct((t_all, DM), F32),
        jax.ShapeDtypeStruct((t_all, DM), F32), jax.ShapeDtypeStruct((t_all, DM), F32),
        jax.ShapeDtypeStruct((t_all, DKV), F32), jax.ShapeDtypeStruct((t_all, DKV), F32),
        jax.ShapeDtypeStruct((DM, t_all), F32),
        jax.ShapeDtypeStruct((t_all, LANES), F32), jax.ShapeDtypeStruct((2 * N_HEADS, t_all), F32),
    )
    return pl.pallas_call(
        functools.partial(_inproj_kernel, n_pt),
        grid=(n_pt + n_st,),
        in_specs=[
            pl.BlockSpec((TM, D_MODEL), lambda i: (jnp.minimum(i, n_pt - 1), 0)),
            pl.BlockSpec((TM, D_MODEL), lambda i: (jnp.maximum(i - n_pt, 0), 0)),
            full(g_mix), full(w1), full(wkt), full(wg), full(wgt), full(bcol), full(brow),
        ],
        out_specs=(tok(DM), tok(DM), tok(DM), tok(DM), tok(DKV), tok(DKV),
                   pl.BlockSpec((DM, TM), lambda i: (0, i)),
                   tok(LANES), pl.BlockSpec((2 * N_HEADS, TM), lambda i: (0, i))),
        out_shape=out_shape,
        compiler_params=_cparams(("arbitrary",), VMEM_LIMIT),
        name="inproj",
    )(xp, xs, g_mix, w1, wkt, wg, wgt, bcol, brow)


def _cumsum_rows(x, n):
    row = lax.broadcasted_iota(jnp.int32, x.shape, 0)
    sh = 1
    while sh < n:
        x = x + jnp.where(row >= sh, pltpu.roll(x, sh, axis=0), 0.0)
        sh *= 2
    return x


def _mlstm_prompt_kernel(qm_ref, kmt_ref, vm_ref, om_ref, gcol_ref, grow_ref, gh_ref,
                         hm_ref, cbd_ref, m_ref, cbd_s, m_s):
    j = pl.program_id(1)
    L = MLSTM_L

    @pl.when(j == 0)
    def _():
        cbd_s[...] = jnp.zeros_like(cbd_s)
        m_s[...] = jnp.zeros_like(m_s)

    ti = lax.broadcasted_iota(jnp.int32, (L, L), 0)
    si = lax.broadcasted_iota(jnp.int32, (L, L), 1)
    causal = ti >= si
    upper = (ti <= si).astype(F32)
    lane128 = lax.broadcasted_iota(jnp.int32, (L, LANES), 1)
    even128 = lane128 < HEAD_DIM
    lane256 = lax.broadcasted_iota(jnp.int32, (1, 2 * LANES), 1)
    cols_e = (lane256 < HEAD_DIM) | (lane256 == LANES)
    cols_o = ((lane256 >= HEAD_DIM) & (lane256 < LANES)) | (lane256 == LANES + 1)
    rowk = lax.broadcasted_iota(jnp.int32, (LANES, 1), 0)
    rows_e = rowk < HEAD_DIM
    bd_mask = (rows_e & cols_e) | ((~rows_e) & cols_o)
    ones_cols = ((lax.broadcasted_iota(jnp.int32, (L, LANES), 1) < 2)).astype(F32)
    bo_r = lax.broadcasted_iota(jnp.int32, (LANES, LANES), 0) // HEAD_DIM
    bo_c = lax.broadcasted_iota(jnp.int32, (LANES, LANES), 1) // HEAD_DIM
    block_ones = (bo_r == bo_c).astype(F32)

    for c in range(MLSTM_TL // L):
        sl = slice(c * L, (c + 1) * L)
        grow = grow_ref[:, sl]
        i_row = grow[0:N_HEADS]
        b_row = jnp.dot(grow[N_HEADS:2 * N_HEADS], upper, precision=HIGHEST,
                        preferred_element_type=F32)
        b_colall = _cumsum_rows(gcol_ref[sl, :], L)
        rowv_all = i_row - b_row
        for p in range(N_PAIRS):
            ls = slice(p * LANES, (p + 1) * LANES)
            q2 = qm_ref[sl, ls].astype(BF16)
            kt2 = kmt_ref[ls, sl]
            v2 = vm_ref[sl, ls]
            vext = jnp.concatenate([v2, ones_cols], axis=1)
            kt_e = jnp.where(rows_e, kt2, 0.0).astype(BF16)
            kt_o = jnp.where(rows_e, 0.0, kt2).astype(BF16)
            s2 = jnp.dot(q2, jnp.concatenate([kt_e, kt_o], axis=1), preferred_element_type=F32)
            cbd = cbd_s[p]
            rq = jnp.dot(q2, cbd.astype(BF16), preferred_element_type=F32)
            ps, mts, inters, wrows, decays, mnews = [], [], [], [], [], []
            for hh in range(2):
                h = 2 * p + hh
                bcol = b_colall[:, N_HEADS + h:N_HEADS + h + 1]
                logd = jnp.where(causal, bcol + rowv_all[h:h + 1, :], NEG)
                m_prev = m_s[h:h + 1, 0:1]
                m_inter = m_prev + bcol
                m_t = jnp.maximum(m_inter, jnp.max(logd, axis=1, keepdims=True))
                d = jnp.exp(logd - m_t)
                ps.append((s2[:, hh * L:(hh + 1) * L] * d).astype(BF16))
                mts.append(m_t)
                inters.append(jnp.exp(m_inter - m_t))
                m_new = m_t[L - 1:L, :]
                b_last = b_row[h:h + 1, L - 1:L]
                decays.append(jnp.exp(m_prev + b_last - m_new))
                wrows.append(jnp.exp(b_last - b_row[h:h + 1, :] + i_row[h:h + 1, :] - m_new))
                mnews.append(m_new)
            vstack = jnp.concatenate([jnp.where(cols_e, vext, 0.0), jnp.where(cols_o, vext, 0.0)],
                                     axis=0).astype(BF16)
            r = jnp.dot(jnp.concatenate(ps, axis=1), vstack, preferred_element_type=F32)
            nd = r + jnp.where(cols_e, inters[0], inters[1]) * rq
            num = nd[:, 0:LANES]
            den = jnp.where(even128, nd[:, LANES:LANES + 1], nd[:, LANES + 1:LANES + 2])
            mt2 = jnp.where(even128, mts[0], mts[1])
            hv = num / jnp.maximum(jnp.abs(den), jnp.exp(-mt2))
            ms = jnp.dot(hv * hv, block_ones, precision=HIGHEST, preferred_element_type=F32) * (1.0 / HEAD_DIM)
            y = hv * lax.rsqrt(ms + RMS_EPS) * gh_ref[:, ls] * _sigmoid(om_ref[sl, ls])
            hm_ref[sl, ls] = y.astype(hm_ref.dtype)
            w2 = jnp.where(rows_e, wrows[0], wrows[1])
            upd = jnp.dot((kt2 * w2).astype(BF16), vext.astype(BF16), preferred_element_type=F32)
            dec2 = jnp.where(rows_e, decays[0], decays[1])
            cbd_s[p] = dec2 * cbd + jnp.where(bd_mask, upd, 0.0)
            for hh in range(2):
                h = 2 * p + hh
                m_s[h:h + 1, :] = jnp.broadcast_to(mnews[hh], (1, LANES))

    @pl.when(j == pl.num_programs(1) - 1)
    def _():
        cbd_ref[0] = cbd_s[...]
        m_ref[0] = m_s[...]


def _mlstm_prompt(qm, kmt, vm, om, gcol, grow, gh, batch, seq):
    nt = seq // MLSTM_TL
    tokb = lambda w: pl.BlockSpec((MLSTM_TL, w), lambda b, j: (b * nt + j, 0))
    rowb = lambda r: pl.BlockSpec((r, MLSTM_TL), lambda b, j: (0, b * nt + j))
    return pl.pallas_call(
        _mlstm_prompt_kernel,
        grid=(batch, nt),
        in_specs=[tokb(DM), rowb(DM), tokb(DM), tokb(DM), tokb(LANES), rowb(2 * N_HEADS),
                  pl.BlockSpec((1, DM), lambda b, j: (0, 0))],
        out_specs=(tokb(DM),
                   pl.BlockSpec((1, N_PAIRS, LANES, 2 * LANES), lambda b, j: (b, 0, 0, 0)),
                   pl.BlockSpec((1, N_HEADS, LANES), lambda b, j: (b, 0, 0))),
        out_shape=(jax.ShapeDtypeStruct((batch * seq, DM), BF16),
                   jax.ShapeDtypeStruct((batch, N_PAIRS, LANES, 2 * LANES), F32),
                   jax.ShapeDtypeStruct((batch, N_HEADS, LANES), F32)),
        scratch_shapes=[pltpu.VMEM((N_PAIRS, LANES, 2 * LANES), F32), pltpu.VMEM((N_HEADS, LANES), F32)],
        compiler_params=_cparams(("arbitrary", "arbitrary"), VMEM_LIMIT),
        name="mlstm_prompt",
    )(qm, kmt, vm, om, gcol, grow, gh)


def _mlstm_sample_kernel(n_tok, q_ref, k_ref, kt_ref, v_ref, o_ref, grow_ref, gcol_ref, c0_ref, n0_ref, m0_ref,
                         gh_ref, h_ref, c_ref, n_ref, m_ref):
    L = n_tok
    ti = lax.broadcasted_iota(jnp.int32, (L, L), 0)
    si = lax.broadcasted_iota(jnp.int32, (L, L), 1)
    causal = ti >= si
    upper = (ti <= si).astype(F32)
    lower = causal.astype(F32)

    def body(g, carry):
        grow = grow_ref[g]
        gcol = gcol_ref[g]
        i_row = grow[0:N_HEADS]
        b_row = jnp.dot(grow[N_HEADS:], upper, precision=HIGHEST, preferred_element_type=F32)
        b_col = jnp.dot(lower, gcol[:, N_HEADS:], precision=HIGHEST, preferred_element_type=F32)
        i_col = gcol[:, 0:N_HEADS]
        m0 = m0_ref[pl.ds(g, 1), :]
        for h in range(N_HEADS):
            q = q_ref[g, h]
            k = k_ref[g, h]
            kt = kt_ref[g, h]
            v = v_ref[g, h]
            c = c0_ref[g, h]
            n = n0_ref[g, h:h + 1, :]
            m_prev = m0[:, h:h + 1]
            bc = b_col[:, h:h + 1]
            logd = jnp.where(causal, bc + (i_row[h:h + 1, :] - b_row[h:h + 1, :]), NEG)
            m_inter = m_prev + bc
            m_t = jnp.maximum(m_inter, jnp.max(logd, axis=1, keepdims=True))
            qb = q.astype(BF16)
            s = jnp.dot(qb, kt.astype(BF16), preferred_element_type=F32) * jnp.exp(logd - m_t)
            inter = jnp.exp(m_inter - m_t)
            num = (jnp.dot(s.astype(BF16), v.astype(BF16), preferred_element_type=F32)
                   + inter * jnp.dot(qb, c.astype(BF16), preferred_element_type=F32))
            den = (jnp.sum(s, axis=1, keepdims=True)
                   + inter * jnp.sum(q * n, axis=1, keepdims=True))
            hv = num / jnp.maximum(jnp.abs(den), jnp.exp(-m_t))
            y = _rms(hv, gh_ref[h:h + 1, :]) * _sigmoid(o_ref[g, h])
            h_ref[g, h] = y
            m_new = m_t[L - 1:L, :]
            b_last = b_row[h:h + 1, L - 1:L]
            decay = jnp.exp(m_prev + b_last - m_new)
            w_row = jnp.exp(b_last - b_row[h:h + 1, :] + i_row[h:h + 1, :] - m_new)
            w_col = jnp.exp(b_last - bc + i_col[:, h:h + 1] - m_new)
            c_ref[g, h] = decay * c + jnp.dot((kt * w_row).astype(BF16), v.astype(BF16),
                                              preferred_element_type=F32)
            n_ref[g, h:h + 1, :] = decay * n + jnp.sum(k * w_col, axis=0, keepdims=True)
            m_ref[pl.ds(g, 1), h:h + 1] = m_new
        return carry

    lax.fori_loop(0, DEC_G, body, 0)


def _mlstm_sample(q, k, kt, v, o, grow, gcol, c0, n0, m0, gh):
    nb, _, n_tok, _ = q.shape
    b4 = lambda a: pl.BlockSpec((DEC_G,) + a.shape[1:], lambda i: (i,) + (0,) * (a.ndim - 1))
    return pl.pallas_call(
        functools.partial(_mlstm_sample_kernel, n_tok),
        grid=(nb // DEC_G,),
        in_specs=[b4(q), b4(k), b4(kt), b4(v), b4(o), b4(grow), b4(gcol), b4(c0), b4(n0), b4(m0),
                  pl.BlockSpec(gh.shape, lambda i: (0, 0))],
        out_specs=(b4(q), b4(c0), b4(n0), b4(m0)),
        out_shape=(jax.ShapeDtypeStruct(q.shape, F32), jax.ShapeDtypeStruct(c0.shape, F32),
                   jax.ShapeDtypeStruct(n0.shape, F32), jax.ShapeDtypeStruct(m0.shape, F32)),
        compiler_params=_cparams(("arbitrary",), VMEM_LIMIT),
        name="mlstm_sample",
    )(q, k, kt, v, o, grow, gcol, c0, n0, m0, gh)


def _alibi_slope(h):
    return float(np.float32(2.0 ** (-8.0 * (h + 1) / N_HEADS)))


def _dup_halves(x):
    lane = lax.broadcasted_iota(jnp.int32, x.shape, 1)
    xr = pltpu.roll(x, HEAD_DIM, axis=1)
    lo = lane < HEAD_DIM
    return jnp.where(lo, x, xr), jnp.where(lo, xr, x)


def _stack_group_queries(q, g):
    lane = lax.broadcasted_iota(jnp.int32, (q.shape[0], LANES), 1)
    parts = []
    for hh in range(GROUP):
        h = GROUP * g + hh
        blk = q[:, (h // 2) * LANES:(h // 2 + 1) * LANES]
        keep = (lane < HEAD_DIM) if h % 2 == 0 else (lane >= HEAD_DIM)
        parts.append(jnp.where(keep, blk, 0.0))
    return jnp.concatenate(parts, axis=0).astype(BF16)


def _swa_prompt_kernel(sink_ref, q_ref, kp_ref, ko_ref, vp_ref, vo_ref, o_ref):
    j = pl.program_id(1)
    R = WINDOW
    q = q_ref[...]
    kd = _dup_halves(jnp.concatenate([kp_ref[...], ko_ref[...]], axis=0))
    vd = _dup_halves(jnp.concatenate([vp_ref[...], vo_ref[...]], axis=0))
    qi = lax.broadcasted_iota(jnp.int32, (R, 2 * R), 0)
    kj = lax.broadcasted_iota(jnp.int32, (R, 2 * R), 1)
    dist = qi + R - kj
    valid = (dist >= 0) & (dist < WINDOW) & ((kj >= R) | (j > 0))
    distf = dist.astype(F32)
    lane = lax.broadcasted_iota(jnp.int32, (R, LANES), 1)
    outs = []
    for g in range(N_KV):
        qs = _stack_group_queries(q, g)
        s = lax.dot_general(qs, kd[g].astype(BF16), (((1,), (1,)), ((), ())),
                            preferred_element_type=F32) * (HEAD_DIM ** -0.5)
        ps = []
        for hh in range(GROUP):
            h = GROUP * g + hh
            sink = sink_ref[h]
            sh = jnp.where(valid, s[hh * R:(hh + 1) * R] - _alibi_slope(h) * distf, NEG)
            mx = jnp.maximum(jnp.max(sh, axis=1, keepdims=True), sink)
            p = jnp.exp(sh - mx)
            p = p / (jnp.sum(p, axis=1, keepdims=True) + jnp.exp(sink - mx))
            ps.append(p.astype(BF16))
        o = jnp.dot(jnp.concatenate(ps, axis=0), vd[g].astype(BF16), preferred_element_type=F32)
        for pp in range(GROUP // 2):
            outs.append(jnp.where(lane < HEAD_DIM, o[(2 * pp) * R:(2 * pp + 1) * R],
                                  o[(2 * pp + 1) * R:(2 * pp + 2) * R]))
    o_ref[...] = jnp.concatenate(outs, axis=1).astype(o_ref.dtype)


def _swa_prompt(sinks, qa, ka, va, batch, seq):
    nb = seq // WINDOW
    own = lambda w: pl.BlockSpec((WINDOW, w), lambda b, j: (b * nb + j, 0))
    prev = lambda w: pl.BlockSpec((WINDOW, w), lambda b, j: (b * nb + jnp.maximum(j - 1, 0), 0))
    return pl.pallas_call(
        _swa_prompt_kernel,
        grid=(batch, nb),
        in_specs=[pl.BlockSpec(memory_space=pltpu.SMEM), own(DM), prev(DKV), own(DKV), prev(DKV), own(DKV)],
        out_specs=own(DM),
        out_shape=jax.ShapeDtypeStruct((batch * seq, DM), BF16),
        compiler_params=_cparams(("arbitrary", "arbitrary"), VMEM_LIMIT),
        name="swa_prompt",
    )(sinks, qa, ka, ka, va, va)


def _swa_decode_kernel(n_tok, sink_ref, q_ref, kn_ref, vn_ref, kc_ref, vc_ref, o_ref, ko_ref, vo_ref):
    W = WINDOW
    Tn = n_tok
    qt = lax.broadcasted_iota(jnp.int32, (Tn, W), 0)
    kj = lax.broadcasted_iota(jnp.int32, (Tn, W), 1)
    dist_c = qt + W - kj
    valid_c = dist_c < WINDOW
    qt2 = lax.broadcasted_iota(jnp.int32, (Tn, Tn), 0)
    kj2 = lax.broadcasted_iota(jnp.int32, (Tn, Tn), 1)
    dist_n = qt2 - kj2
    valid_n = dist_n >= 0
    dcf = dist_c.astype(F32)
    dnf = dist_n.astype(F32)
    lane = lax.broadcasted_iota(jnp.int32, (Tn, LANES), 1)

    def body(b, carry):
        rows = pl.ds(pl.multiple_of(b * Tn, Tn), Tn)
        q = q_ref[rows, :]
        kn = kn_ref[rows, :]
        vn = vn_ref[rows, :]
        kc = kc_ref[b]
        vc = vc_ref[b]
        ko_ref[b, 0:W - Tn, :] = kc[Tn:W]
        ko_ref[b, W - Tn:W, :] = kn
        vo_ref[b, 0:W - Tn, :] = vc[Tn:W]
        vo_ref[b, W - Tn:W, :] = vn
        kcd, knd = _dup_halves(kc), _dup_halves(kn)
        vcd, vnd = _dup_halves(vc), _dup_halves(vn)
        outs = []
        for g in range(N_KV):
            qs = _stack_group_queries(q, g)
            nt = (((1,), (1,)), ((), ()))
            sc = lax.dot_general(qs, kcd[g].astype(BF16), nt, preferred_element_type=F32) * (HEAD_DIM ** -0.5)
            sn = lax.dot_general(qs, knd[g].astype(BF16), nt, preferred_element_type=F32) * (HEAD_DIM ** -0.5)
            pcs, pns = [], []
            for hh in range(GROUP):
                h = GROUP * g + hh
                sink = sink_ref[h]
                slope = _alibi_slope(h)
                shc = jnp.where(valid_c, sc[hh * Tn:(hh + 1) * Tn] - slope * dcf, NEG)
                shn = jnp.where(valid_n, sn[hh * Tn:(hh + 1) * Tn] - slope * dnf, NEG)
                mx = jnp.maximum(jnp.maximum(jnp.max(shc, axis=1, keepdims=True),
                                             jnp.max(shn, axis=1, keepdims=True)), sink)
                pc = jnp.exp(shc - mx)
                pn = jnp.exp(shn - mx)
                inv = 1.0 / (jnp.sum(pc, axis=1, keepdims=True) + jnp.sum(pn, axis=1, keepdims=True)
                             + jnp.exp(sink - mx))
                pcs.append((pc * inv).astype(BF16))
                pns.append((pn * inv).astype(BF16))
            o = (jnp.dot(jnp.concatenate(pcs, axis=0), vcd[g].astype(BF16), preferred_element_type=F32)
                 + jnp.dot(jnp.concatenate(pns, axis=0), vnd[g].astype(BF16), preferred_element_type=F32))
            for pp in range(GROUP // 2):
                outs.append(jnp.where(lane < HEAD_DIM, o[(2 * pp) * Tn:(2 * pp + 1) * Tn],
                                      o[(2 * pp + 1) * Tn:(2 * pp + 2) * Tn]))
        o_ref[rows, :] = jnp.concatenate(outs, axis=1)
        return carry

    lax.fori_loop(0, DEC_G, body, 0)


def _swa_decode(sinks, qa, ka, va, kc, vc, row0, n_tok):
    nb = kc.shape[0]
    blk0 = row0 // (DEC_G * n_tok)
    tokb = lambda w: pl.BlockSpec((DEC_G * n_tok, w), lambda i: (blk0 + i, 0))
    cache = pl.BlockSpec((DEC_G, WINDOW, DKV), lambda i: (i, 0, 0))
    return pl.pallas_call(
        functools.partial(_swa_decode_kernel, n_tok),
        grid=(nb // DEC_G,),
        in_specs=[pl.BlockSpec(memory_space=pltpu.SMEM), tokb(DM), tokb(DKV), tokb(DKV), cache, cache],
        out_specs=(pl.BlockSpec((DEC_G * n_tok, DM), lambda i: (i, 0)), cache, cache),
        out_shape=(jax.ShapeDtypeStruct((nb * n_tok, DM), F32),
                   jax.ShapeDtypeStruct(kc.shape, F32), jax.ShapeDtypeStruct(vc.shape, F32)),
        compiler_params=_cparams(("arbitrary",), VMEM_LIMIT),
        name="swa_decode",
    )(sinks, qa, ka, va, kc, vc)


def _outproj_router_kernel(n_ptiles, xp_ref, xs_ref, hmp_ref, hap_ref, hms_ref, has_ref, wom_ref, woa_ref,
                           g_ref, wrt_ref, br_ref,
                           x1_ref, h2_ref, ir_ref, gcol_ref, cnt_ref, carry_s):
    i = pl.program_id(0)

    @pl.when(i == 0)
    def _():
        carry_s[...] = jnp.zeros_like(carry_s)

    is_p = i < n_ptiles
    x = jnp.where(is_p, xp_ref[...], xs_ref[...])
    hm = jnp.where(is_p, hmp_ref[...], hms_ref[...].astype(BF16))
    ha = jnp.where(is_p, hap_ref[...], has_ref[...].astype(BF16))
    x1 = (x + jnp.dot(hm, wom_ref[...], preferred_element_type=F32)
          + jnp.dot(ha, woa_ref[...], preferred_element_type=F32))
    x1_ref[...] = x1
    h2 = _rms(x1, g_ref[...])
    h2_ref[...] = h2
    logits = lax.dot_general(wrt_ref[...], h2, (((1,), (1,)), ((), ())), precision=HIGHEST,
                             preferred_element_type=F32) + br_ref[...]
    eidx = lax.broadcasted_iota(jnp.int32, logits.shape, 0).astype(F32)
    work = logits
    vals, idxs, hots = [], [], []
    for _ in range(TOP_K):
        mv = jnp.max(work, axis=0, keepdims=True)
        sel = jnp.min(jnp.where(work == mv, eidx, float(N_EXPERTS)), axis=0, keepdims=True)
        hot = eidx == sel
        vals.append(mv)
        idxs.append(sel.astype(jnp.int32))
        hots.append(hot)
        work = jnp.where(hot, -jnp.inf, work)
    es = [jnp.exp(v - vals[0]) for v in vals]
    tot = es[0] + es[1] + es[2] + es[3]
    gates = [e / tot for e in es]
    hot_all = (hots[0] | hots[1] | hots[2] | hots[3])
    tm = logits.shape[1]
    su = (lax.broadcasted_iota(jnp.int32, (tm, tm), 0) < lax.broadcasted_iota(jnp.int32, (tm, tm), 1))
    cum = jnp.dot(hot_all.astype(BF16), su.astype(BF16), preferred_element_type=F32)
    base = carry_s[:, 0:1] + cum
    ranks = [jnp.sum(jnp.where(hot, base, 0.0), axis=0, keepdims=True).astype(jnp.int32) for hot in hots]
    ir_ref[...] = jnp.concatenate(idxs + ranks, axis=0)
    g8 = jnp.concatenate(gates + [jnp.zeros((SUBLANES - TOP_K, tm), F32)], axis=0)
    gcol_ref[...] = jnp.transpose(g8)
    carry_s[...] = carry_s[...] + jnp.sum(hot_all.astype(F32), axis=1, keepdims=True)
    cnt_ref[...] = carry_s[...]


def _outproj_router(xp, xs, hmp, hap, hms, has, wom, woa, g_ffn, wrt, br):
    tp, ts = xp.shape[0], xs.shape[0]
    n_pt, n_st = tp // TM, ts // TM
    t_all = tp + ts
    pblk = lambda w: pl.BlockSpec((TM, w), lambda i: (jnp.minimum(i, n_pt - 1), 0))
    sblk = lambda w: pl.BlockSpec((TM, w), lambda i: (jnp.maximum(i - n_pt, 0), 0))
    full = lambda a: pl.BlockSpec(a.shape, lambda i: (0,) * a.ndim)
    return pl.pallas_call(
        functools.partial(_outproj_router_kernel, n_pt),
        grid=(n_pt + n_st,),
        in_specs=[pblk(D_MODEL), sblk(D_MODEL), pblk(DM), pblk(DM), sblk(DM), sblk(DM),
                  full(wom), full(woa), full(g_ffn), full(wrt), full(br)],
        out_specs=(pl.BlockSpec((TM, D_MODEL), lambda i: (i, 0)),
                   pl.BlockSpec((TM, D_MODEL), lambda i: (i, 0)),
                   pl.BlockSpec((2 * TOP_K, TM), lambda i: (0, i)),
                   pl.BlockSpec((TM, SUBLANES), lambda i: (i, 0)),
                   pl.BlockSpec((N_EXPERTS, LANES), lambda i: (0, 0))),
        out_shape=(jax.ShapeDtypeStruct((t_all, D_MODEL), F32),
                   jax.ShapeDtypeStruct((t_all, D_MODEL), F32),
                   jax.ShapeDtypeStruct((2 * TOP_K, t_all), jnp.int32),
                   jax.ShapeDtypeStruct((t_all, SUBLANES), F32),
                   jax.ShapeDtypeStruct((N_EXPERTS, LANES), F32)),
        scratch_shapes=[pltpu.VMEM((N_EXPERTS, LANES), F32)],
        compiler_params=_cparams(("arbitrary",), VMEM_LIMIT),
        name="outproj_router",
    )(xp, xs, hmp, hap, hms, has, wom, woa, g_ffn, wrt, br)


def _dispatch_kernel(zlo_ref, zhi_ref, dest_ref, h2_ref, zrow_ref, xs_ref, sem, zsem):
    i = pl.program_id(0)
    base = i * DISP_T

    def issue(t, carry):
        for k in range(TOP_K):
            pltpu.make_async_copy(h2_ref.at[pl.ds(base + t, 1)], xs_ref.at[pl.ds(dest_ref[k, t], 1)], sem).start()
        return carry

    lax.fori_loop(0, DISP_T, issue, 0)

    def drain(t, carry):
        for k in range(TOP_K):
            pltpu.make_async_copy(h2_ref.at[pl.ds(0, 1)], xs_ref.at[pl.ds(0, 1)], sem).wait()
        return carry

    lax.fori_loop(0, DISP_T, drain, 0)

    @pl.when(i == pl.num_programs(0) - 1)
    def _():
        def per_expert(e, carry):
            def zi(r, c):
                pltpu.make_async_copy(zrow_ref, xs_ref.at[pl.ds(r, 1)], zsem).start()
                return c

            def zw(r, c):
                pltpu.make_async_copy(zrow_ref, xs_ref.at[pl.ds(0, 1)], zsem).wait()
                return c

            lax.fori_loop(zlo_ref[e], zhi_ref[e], zi, 0)
            lax.fori_loop(zlo_ref[e], zhi_ref[e], zw, 0)
            return carry

        lax.fori_loop(0, N_EXPERTS, per_expert, 0)


def _dispatch(zlo, zhi, dest, h2, n_rows):
    t_all = h2.shape[0]
    zrow = jnp.zeros((1, D_MODEL), F32)
    grid_spec = pltpu.PrefetchScalarGridSpec(
        num_scalar_prefetch=2,
        grid=(t_all // DISP_T,),
        in_specs=[pl.BlockSpec((TOP_K, DISP_T), lambda i, *_: (0, i), memory_space=pltpu.SMEM),
                  pl.BlockSpec(memory_space=pl.ANY), pl.BlockSpec(memory_space=pl.ANY)],
        out_specs=pl.BlockSpec(memory_space=pl.ANY),
        scratch_shapes=[pltpu.SemaphoreType.DMA(()), pltpu.SemaphoreType.DMA(())],
    )
    return pl.pallas_call(
        _dispatch_kernel,
        grid_spec=grid_spec,
        out_shape=jax.ShapeDtypeStruct((n_rows, D_MODEL), F32),
        compiler_params=_cparams(("arbitrary",)),
        name="moe_dispatch",
    )(zlo, zhi, dest, h2, zrow)


def _expert_kernel(be_ref, na_ref, x_ref, wgu_ref, bgu_ref, wd_ref, bd_ref, o_ref, wgu_s, wd_s):
    i = pl.program_id(0)

    @pl.when(i < na_ref[0])
    def _():
        changed = (i == 0) | (be_ref[i] != be_ref[jnp.maximum(i - 1, 0)])

        @pl.when(changed)
        def _():
            wgu_s[...] = wgu_ref[...].astype(BF16)
            wd_s[...] = wd_ref[...].astype(BF16)

        x = x_ref[...].astype(BF16)
        gu = jnp.dot(x, wgu_s[...], preferred_element_type=F32) + bgu_ref[...]
        gate = jnp.minimum(gu[:, :D_FF], SWIGLU_LIMIT)
        up = jnp.clip(gu[:, D_FF:], -SWIGLU_LIMIT, SWIGLU_LIMIT)
        act = (up + 1.0) * (gate * _sigmoid(gate * SWIGLU_ALPHA))
        o_ref[...] = jnp.dot(act.astype(BF16), wd_s[...], preferred_element_type=F32) + bd_ref[...]


def _expert_ffn(block_exp, n_active, xs, wgu, bgu, wd, bd):
    n_rows = xs.shape[0]
    n_blocks = n_rows // MOE_BM
    rowblk = pl.BlockSpec((MOE_BM, D_MODEL), lambda i, be, na: (jnp.minimum(i, na[0] - 1), 0))
    grid_spec = pltpu.PrefetchScalarGridSpec(
        num_scalar_prefetch=2,
        grid=(n_blocks,),
        in_specs=[rowblk,
                  pl.BlockSpec((None, D_MODEL, 2 * D_FF), lambda i, be, na: (be[i], 0, 0)),
                  pl.BlockSpec((None, 1, 2 * D_FF), lambda i, be, na: (be[i], 0, 0)),
                  pl.BlockSpec((None, D_FF, D_MODEL), lambda i, be, na: (be[i], 0, 0)),
                  pl.BlockSpec((None, 1, D_MODEL), lambda i, be, na: (be[i], 0, 0))],
        out_specs=rowblk,
        scratch_shapes=[pltpu.VMEM((D_MODEL, 2 * D_FF), BF16), pltpu.VMEM((D_FF, D_MODEL), BF16)],
    )
    return pl.pallas_call(
        _expert_kernel,
        grid_spec=grid_spec,
        out_shape=jax.ShapeDtypeStruct((n_rows, D_MODEL), F32),
        compiler_params=_cparams(("arbitrary",), VMEM_LIMIT),
        name="moe_experts",
    )(block_exp, n_active, xs, wgu, bgu, wd, bd)


def _combine_kernel(n_ptiles, dest_ref, outs_ref, x1_ref, gcol_ref, gf_ref, yp_ref, ys_ref, buf, sem):
    i = pl.program_id(0)

    def issue(t, carry):
        for k in range(TOP_K):
            pltpu.make_async_copy(outs_ref.at[pl.ds(dest_ref[k, t], 1)], buf.at[k, pl.ds(t, 1)], sem).start()
        return carry

    lax.fori_loop(0, COMB_T, issue, 0)

    def drain(t, carry):
        for k in range(TOP_K):
            pltpu.make_async_copy(outs_ref.at[pl.ds(0, 1)], buf.at[k, pl.ds(0, 1)], sem).wait()
        return carry

    lax.fori_loop(0, COMB_T, drain, 0)

    acc = x1_ref[...]
    gc = gcol_ref[...]
    for k in range(TOP_K):
        acc = acc + gc[:, k:k + 1] * buf[k]
    y = _rms(acc, gf_ref[...])

    @pl.when(i < n_ptiles)
    def _():
        yp_ref[...] = y

    @pl.when(i >= n_ptiles)
    def _():
        ys_ref[...] = y


def _combine(dest, outs, x1, gcol, g_final, tp, ts):
    n_pt, n_st = tp // COMB_T, ts // COMB_T
    return pl.pallas_call(
        functools.partial(_combine_kernel, n_pt),
        grid=(n_pt + n_st,),
        in_specs=[pl.BlockSpec((TOP_K, COMB_T), lambda i: (0, i), memory_space=pltpu.SMEM),
                  pl.BlockSpec(memory_space=pl.ANY),
                  pl.BlockSpec((COMB_T, D_MODEL), lambda i: (i, 0)),
                  pl.BlockSpec((COMB_T, SUBLANES), lambda i: (i, 0)),
                  pl.BlockSpec((1, D_MODEL), lambda i: (0, 0))],
        out_specs=(pl.BlockSpec((COMB_T, D_MODEL), lambda i: (jnp.minimum(i, n_pt - 1), 0)),
                   pl.BlockSpec((COMB_T, D_MODEL), lambda i: (jnp.maximum(i - n_pt, 0), 0))),
        out_shape=(jax.ShapeDtypeStruct((tp, D_MODEL), F32), jax.ShapeDtypeStruct((ts, D_MODEL), F32)),
        scratch_shapes=[pltpu.VMEM((TOP_K, COMB_T, D_MODEL), F32), pltpu.SemaphoreType.DMA(())],
        compiler_params=_cparams(("arbitrary",), VMEM_LIMIT),
        name="moe_combine",
    )(dest, outs, x1, gcol, g_final)


def kernel(x_prompt, x_sample, cache_swa_k, cache_swa_v, state_mlstm_c, state_mlstm_n, state_mlstm_m,
           g_mix, w_in, b_igate, b_fgate, g_head, attn_sinks, w_out, g_ffn, w_router, b_router,
           w_gate_up, b_gate_up, w_down, b_down, g_final):
    assert w_in.shape[0] == 1, "single-layer problem"
    B, S, _ = x_prompt.shape
    Bd, Tn, _ = x_sample.shape
    tp, ts = B * S, Bd * Tn
    t_all = tp + ts
    xp = x_prompt.reshape(tp, D_MODEL)
    xs = x_sample.reshape(ts, D_MODEL)

    w = w_in[0]
    o = np.cumsum([0, DM, DM, DM, DM, N_HEADS, N_HEADS, DM, DKV, DKV])
    col = lambda a: w[:, int(o[a]):int(o[a + 1])]
    w1 = jnp.concatenate([col(0), col(2), col(3), col(6), col(7), col(8)], axis=1).astype(BF16)
    wkt = col(1).T.astype(BF16)
    wgates = jnp.concatenate([col(4), col(5)], axis=1)
    wg = jnp.pad(wgates, ((0, 0), (0, LANES - 2 * N_HEADS))).astype(BF16)
    wgt = wgates.T.astype(BF16)
    bg = jnp.concatenate([b_igate[0], b_fgate[0]]).astype(F32)
    bcol = jnp.pad(bg, (0, LANES - 2 * N_HEADS)).reshape(1, LANES)
    brow = bg.reshape(2 * N_HEADS, 1)

    qm, vm, om, qa, ka, va, kmt, gcol, grow = _inproj(xp, xs, g_mix[0].reshape(1, D_MODEL), w1, wkt, wg, wgt,
                                                      bcol, brow)

    gh = g_head[0].astype(F32)
    sinks = attn_sinks[0].astype(F32)

    hm_p, cbd, m_p = _mlstm_prompt(qm, kmt, vm, om, gcol, grow, gh.reshape(1, DM), B, S)
    ha_p = _swa_prompt(sinks, qa, ka, va, B, S)

    hd = lambda a: a[tp:].reshape(Bd, Tn, N_HEADS, HEAD_DIM).transpose(0, 2, 1, 3)
    kts = kmt[:, tp:].reshape(N_HEADS, HEAD_DIM, Bd, Tn)
    grow_s = grow[:, tp:].reshape(2 * N_HEADS, Bd, Tn).transpose(1, 0, 2)
    gcol_s = gcol[tp:, :2 * N_HEADS].reshape(Bd, Tn, 2 * N_HEADS)
    hm_s, c_s, n_s, m_s = _mlstm_sample(
        hd(qm), kts.transpose(2, 0, 3, 1), kts.transpose(2, 0, 1, 3), hd(vm), hd(om), grow_s, gcol_s,
        state_mlstm_c[0], state_mlstm_n[0], state_mlstm_m[0], gh)
    hm_s = hm_s.transpose(0, 2, 1, 3).reshape(ts, DM)
    ha_s, k_s, v_s = _swa_decode(sinks, qa, ka, va, cache_swa_k[0].reshape(Bd, WINDOW, DKV),
                                 cache_swa_v[0].reshape(Bd, WINDOW, DKV), tp, Tn)

    wo = w_out[0].astype(BF16)
    x1, h2, ir, gates_col, cnt = _outproj_router(
        xp, xs, hm_p, ha_p, hm_s, ha_s, wo[:DM], wo[DM:], g_ffn[0].reshape(1, D_MODEL),
        w_router[0].T, b_router[0].reshape(N_EXPERTS, 1))

    n_blocks = (t_all * TOP_K) // MOE_BM + N_EXPERTS
    n_rows = n_blocks * MOE_BM
    counts = cnt[:, 0].astype(jnp.int32)
    padded = (counts + MOE_BM - 1) // MOE_BM * MOE_BM
    pad_end = jnp.cumsum(padded)
    pad_start = pad_end - padded
    dest = pad_start[ir[:TOP_K]] + ir[TOP_K:]
    n_active = (pad_end[-1] // MOE_BM).astype(jnp.int32)
    blk = jnp.minimum(jnp.arange(n_blocks, dtype=jnp.int32), n_active - 1)
    block_exp = jnp.minimum(jnp.searchsorted(pad_end, blk * MOE_BM, side="right"), N_EXPERTS - 1).astype(jnp.int32)

    xs_sorted = _dispatch(pad_start + counts, pad_end, dest, h2, n_rows)
    outs = _expert_ffn(block_exp, n_active.reshape(1), xs_sorted, w_gate_up[0],
                       b_gate_up[0].reshape(N_EXPERTS, 1, 2 * D_FF), w_down[0],
                       b_down[0].reshape(N_EXPERTS, 1, D_MODEL))
    y_p, y_s = _combine(dest, outs, x1, gates_col, g_final.reshape(1, D_MODEL), tp, ts)

    kv_tail = lambda a: a[:tp].reshape(B, S, N_KV, HEAD_DIM)[:, S - WINDOW:][None]
    c_e = cbd[:, :, :HEAD_DIM, :HEAD_DIM]
    c_o = cbd[:, :, HEAD_DIM:, HEAD_DIM:LANES]
    c_p = jnp.stack([c_e, c_o], axis=2).reshape(B, N_HEADS, HEAD_DIM, HEAD_DIM)
    n_p = jnp.stack([cbd[:, :, :HEAD_DIM, LANES], cbd[:, :, HEAD_DIM:, LANES + 1]], axis=2).reshape(B, N_HEADS, HEAD_DIM)
    return (y_p.reshape(B, S, D_MODEL), y_s.reshape(Bd, Tn, D_MODEL),
            kv_tail(ka), kv_tail(va), c_p[None], n_p[None], m_p[:, :, 0][None],
            k_s.reshape(Bd, WINDOW, N_KV, HEAD_DIM)[None], v_s.reshape(Bd, WINDOW, N_KV, HEAD_DIM)[None],
            c_s[None], n_s[None], m_s[None])
```

```python
import functools
import math

import jax
import jax.numpy as jnp
import numpy as np
from jax import lax
from jax.experimental import pallas as pl
from jax.experimental.pallas import tpu as pltpu

F32 = jnp.float32
BF16 = jnp.bfloat16
HIGHEST = lax.Precision.HIGHEST

D_MODEL = 1024
HEAD_DIM = 64
N_HEADS = 8
N_PAIRS = N_HEADS // 2
N_KV = 2
GROUP = N_HEADS // N_KV
WINDOW = 128
N_EXPERTS = 32
TOP_K = 4
D_FF = 1024
SWIGLU_LIMIT = 7.0
SWIGLU_ALPHA = 1.702
RMS_EPS = 1e-5
DM = N_HEADS * HEAD_DIM
DKV = N_KV * HEAD_DIM
NEG = -1e30

LANES = 128
SUBLANES = 8
VMEM_LIMIT = 56 * 1024 * 1024

TM = 512
MLSTM_TL = 512
MLSTM_L = 128
DEC_G = 8
MOE_BM = 256
DISP_T = 512
COMB_T = 256


def _cparams(sem, vmem=None):
    return pltpu.CompilerParams(dimension_semantics=sem, vmem_limit_bytes=vmem)


def _rms(x, g):
    return x * lax.rsqrt(jnp.mean(x * x, axis=-1, keepdims=True) + RMS_EPS) * g


def _log_sigmoid(z):
    return jnp.minimum(z, 0.0) - jnp.log(1.0 + jnp.exp(-jnp.abs(z)))


def _sigmoid(z):
    return 1.0 / (1.0 + jnp.exp(-z))


def _inproj_kernel(n_ptiles, xp_ref, xs_ref, g_ref, w1_ref, wkt_ref, wg_ref, wgt_ref, bcol_ref, brow_ref,
                   qm_ref, vm_ref, om_ref, qa_ref, ka_ref, va_ref, kmt_ref, gcol_ref, grow_ref):
    i = pl.program_id(0)
    x = jnp.where(i < n_ptiles, xp_ref[...], xs_ref[...])
    h = _rms(x, g_ref[...]).astype(BF16)
    main = jnp.dot(h, w1_ref[...], preferred_element_type=F32)
    qm_ref[...] = main[:, 0:DM]
    vm_ref[...] = main[:, DM:2 * DM]
    om_ref[...] = main[:, 2 * DM:3 * DM]
    qa_ref[...] = main[:, 3 * DM:4 * DM]
    ka_ref[...] = main[:, 4 * DM:4 * DM + DKV]
    va_ref[...] = main[:, 4 * DM + DKV:4 * DM + 2 * DKV]
    kt = lax.dot_general(wkt_ref[...], h, (((1,), (1,)), ((), ())), preferred_element_type=F32)
    kmt_ref[...] = kt * (HEAD_DIM ** -0.5)
    zc = jnp.dot(h, wg_ref[...], preferred_element_type=F32) + bcol_ref[...]
    lane = lax.broadcasted_iota(jnp.int32, zc.shape, 1)
    gcol_ref[...] = jnp.where(lane < N_HEADS, zc, _log_sigmoid(zc))
    zr = lax.dot_general(wgt_ref[...], h, (((1,), (1,)), ((), ())), preferred_element_type=F32) + brow_ref[...]
    row = lax.broadcasted_iota(jnp.int32, zr.shape, 0)
    grow_ref[...] = jnp.where(row < N_HEADS, zr, _log_sigmoid(zr))


def _inproj(xp, xs, g_mix, w1, wkt, wg, wgt, bcol, brow):
    tp, ts = xp.shape[0], xs.shape[0]
    n_pt, n_st = tp // TM, ts // TM
    t_all = tp + ts
    n1 = w1.shape[1]
    tok = lambda w: pl.BlockSpec((TM, w), lambda i: (i, 0))
    full = lambda a: pl.BlockSpec(a.shape, lambda i: (0,) * a.ndim)
    out_shape = (
        jax.ShapeDtypeStruct((t_all, DM), F32), jax.ShapeDtypeStruct((t_all, DM), F32),
        jax.ShapeDtypeStruct((t_all, DM), F32), jax.ShapeDtypeStruct((t_all, DM), F32),
        jax.ShapeDtypeStruct((t_all, DKV), F32), jax.ShapeDtypeStruct((t_all, DKV), F32),
        jax.ShapeDtypeStruct((DM, t_all), F32),
        jax.ShapeDtypeStruct((t_all, LANES), F32), jax.ShapeDtypeStruct((2 * N_HEADS, t_all), F32),
    )
    return pl.pallas_call(
        functools.partial(_inproj_kernel, n_pt),
        grid=(n_pt + n_st,),
        in_specs=[
            pl.BlockSpec((TM, D_MODEL), lambda i: (jnp.minimum(i, n_pt - 1), 0)),
            pl.BlockSpec((TM, D_MODEL), lambda i: (jnp.maximum(i - n_pt, 0), 0)),
            full(g_mix), full(w1), full(wkt), full(wg), full(wgt), full(bcol), full(brow),
        ],
        out_specs=(tok(DM), tok(DM), tok(DM), tok(DM), tok(DKV), tok(DKV),
                   pl.BlockSpec((DM, TM), lambda i: (0, i)),
                   tok(LANES), pl.BlockSpec((2 * N_HEADS, TM), lambda i: (0, i))),
        out_shape=out_shape,
        compiler_params=_cparams(("arbitrary",), VMEM_LIMIT),
        name="inproj",
    )(xp, xs, g_mix, w1, wkt, wg, wgt, bcol, brow)


def _cumsum_rows(x, n):
    row = lax.broadcasted_iota(jnp.int32, x.shape, 0)
    sh = 1
    while sh < n:
        x = x + jnp.where(row >= sh, pltpu.roll(x, sh, axis=0), 0.0)
        sh *= 2
    return x


def _mlstm_prompt_kernel(qm_ref, kmt_ref, vm_ref, om_ref, gcol_ref, grow_ref, gh_ref,
                         hm_ref, cbd_ref, m_ref, cbd_s, m_s):
    j = pl.program_id(1)
    L = MLSTM_L

    @pl.when(j == 0)
    def _():
        cbd_s[...] = jnp.zeros_like(cbd_s)
        m_s[...] = jnp.zeros_like(m_s)

    ti = lax.broadcasted_iota(jnp.int32, (L, L), 0)
    si = lax.broadcasted_iota(jnp.int32, (L, L), 1)
    causal = ti >= si
    upper = (ti <= si).astype(F32)
    lane128 = lax.broadcasted_iota(jnp.int32, (L, LANES), 1)
    even128 = lane128 < HEAD_DIM
    lane256 = lax.broadcasted_iota(jnp.int32, (1, 2 * LANES), 1)
    cols_e = (lane256 < HEAD_DIM) | (lane256 == LANES)
    cols_o = ((lane256 >= HEAD_DIM) & (lane256 < LANES)) | (lane256 == LANES + 1)
    rowk = lax.broadcasted_iota(jnp.int32, (LANES, 1), 0)
    rows_e = rowk < HEAD_DIM
    bd_mask = (rows_e & cols_e) | ((~rows_e) & cols_o)
    ones_cols = ((lax.broadcasted_iota(jnp.int32, (L, LANES), 1) < 2)).astype(F32)
    bo_r = lax.broadcasted_iota(jnp.int32, (LANES, LANES), 0) // HEAD_DIM
    bo_c = lax.broadcasted_iota(jnp.int32, (LANES, LANES), 1) // HEAD_DIM
    block_ones = (bo_r == bo_c).astype(F32)

    for c in range(MLSTM_TL // L):
        sl = slice(c * L, (c + 1) * L)
        grow = grow_ref[:, sl]
        i_row = grow[0:N_HEADS]
        b_row = jnp.dot(grow[N_HEADS:2 * N_HEADS], upper, precision=HIGHEST,
                        preferred_element_type=F32)
        b_colall = _cumsum_rows(gcol_ref[sl, :], L)
        rowv_all = i_row - b_row
        for p in range(N_PAIRS):
            ls = slice(p * LANES, (p + 1) * LANES)
            q2 = qm_ref[sl, ls].astype(BF16)
            kt2 = kmt_ref[ls, sl]
            v2 = vm_ref[sl, ls]
            vext = jnp.concatenate([v2, ones_cols], axis=1)
            kt_e = jnp.where(rows_e, kt2, 0.0).astype(BF16)
            kt_o = jnp.where(rows_e, 0.0, kt2).astype(BF16)
            s2 = jnp.dot(q2, jnp.concatenate([kt_e, kt_o], axis=1), preferred_element_type=F32)
            cbd = cbd_s[p]
            rq = jnp.dot(q2, cbd.astype(BF16), preferred_element_type=F32)
            ps, mts, inters, wrows, decays, mnews = [], [], [], [], [], []
            for hh in range(2):
                h = 2 * p + hh
                bcol = b_colall[:, N_HEADS + h:N_HEADS + h + 1]
                logd = jnp.where(causal, bcol + rowv_all[h:h + 1, :], NEG)
                m_prev = m_s[h:h + 1, 0:1]
                m_inter = m_prev + bcol
                m_t = jnp.maximum(m_inter, jnp.max(logd, axis=1, keepdims=True))
                d = jnp.exp(logd - m_t)
                ps.append((s2[:, hh * L:(hh + 1) * L] * d).astype(BF16))
                mts.append(m_t)
                inters.append(jnp.exp(m_inter - m_t))
                m_new = m_t[L - 1:L, :]
                b_last = b_row[h:h + 1, L - 1:L]
                decays.append(jnp.exp(m_prev + b_last - m_new))
                wrows.append(jnp.exp(b_last - b_row[h:h + 1, :] + i_row[h:h + 1, :] - m_new))
                mnews.append(m_new)
            vstack = jnp.concatenate([jnp.where(cols_e, vext, 0.0), jnp.where(cols_o, vext, 0.0)],
                                     axis=0).astype(BF16)
            r = jnp.dot(jnp.concatenate(ps, axis=1), vstack, preferred_element_type=F32)
            nd = r + jnp.where(cols_e, inters[0], inters[1]) * rq
            num = nd[:, 0:LANES]
            den = jnp.where(even128, nd[:, LANES:LANES + 1], nd[:, LANES + 1:LANES + 2])
            mt2 = jnp.where(even128, mts[0], mts[1])
            hv = num / jnp.maximum(jnp.abs(den), jnp.exp(-mt2))
            ms = jnp.dot(hv * hv, block_ones, precision=HIGHEST, preferred_element_type=F32) * (1.0 / HEAD_DIM)
            y = hv * lax.rsqrt(ms + RMS_EPS) * gh_ref[:, ls] * _sigmoid(om_ref[sl, ls])
            hm_ref[sl, ls] = y.astype(hm_ref.dtype)
            w2 = jnp.where(rows_e, wrows[0], wrows[1])
            upd = jnp.dot((kt2 * w2).astype(BF16), vext.astype(BF16), preferred_element_type=F32)
            dec2 = jnp.where(rows_e, decays[0], decays[1])
            cbd_s[p] = dec2 * cbd + jnp.where(bd_mask, upd, 0.0)
            for hh in range(2):
                h = 2 * p + hh
                m_s[h:h + 1, :] = jnp.broadcast_to(mnews[hh], (1, LANES))

    @pl.when(j == pl.num_programs(1) - 1)
    def _():
        cbd_ref[0] = cbd_s[...]
        m_ref[0] = m_s[...]


def _mlstm_prompt(qm, kmt, vm, om, gcol, grow, gh, batch, seq):
    nt = seq // MLSTM_TL
    tokb = lambda w: pl.BlockSpec((MLSTM_TL, w), lambda b, j: (b * nt + j, 0))
    rowb = lambda r: pl.BlockSpec((r, MLSTM_TL), lambda b, j: (0, b * nt + j))
    return pl.pallas_call(
        _mlstm_prompt_kernel,
        grid=(batch, nt),
        in_specs=[tokb(DM), rowb(DM), tokb(DM), tokb(DM), tokb(LANES), rowb(2 * N_HEADS),
                  pl.BlockSpec((1, DM), lambda b, j: (0, 0))],
        out_specs=(tokb(DM),
                   pl.BlockSpec((1, N_PAIRS, LANES, 2 * LANES), lambda b, j: (b, 0, 0, 0)),
                   pl.BlockSpec((1, N_HEADS, LANES), lambda b, j: (b, 0, 0))),
        out_shape=(jax.ShapeDtypeStruct((batch * seq, DM), BF16),
                   jax.ShapeDtypeStruct((batch, N_PAIRS, LANES, 2 * LANES), F32),
                   jax.ShapeDtypeStruct((batch, N_HEADS, LANES), F32)),
        scratch_shapes=[pltpu.VMEM((N_PAIRS, LANES, 2 * LANES), F32), pltpu.VMEM((N_HEADS, LANES), F32)],
        compiler_params=_cparams(("arbitrary", "arbitrary"), VMEM_LIMIT),
        name="mlstm_prompt",
    )(qm, kmt, vm, om, gcol, grow, gh)


def _mlstm_sample_kernel(n_tok, q_ref, k_ref, kt_ref, v_ref, o_ref, grow_ref, gcol_ref, c0_ref, n0_ref, m0_ref,
                         gh_ref, h_ref, c_ref, n_ref, m_ref):
    L = n_tok
    ti = lax.broadcasted_iota(jnp.int32, (L, L), 0)
    si = lax.broadcasted_iota(jnp.int32, (L, L), 1)
    causal = ti >= si
    upper = (ti <= si).astype(F32)
    lower = causal.astype(F32)

    def body(g, carry):
        grow = grow_ref[g]
        gcol = gcol_ref[g]
        i_row = grow[0:N_HEADS]
        b_row = jnp.dot(grow[N_HEADS:], upper, precision=HIGHEST, preferred_element_type=F32)
        b_col = jnp.dot(lower, gcol[:, N_HEADS:], precision=HIGHEST, preferred_element_type=F32)
        i_col = gcol[:, 0:N_HEADS]
        m0 = m0_ref[pl.ds(g, 1), :]
        for h in range(N_HEADS):
            q = q_ref[g, h]
            k = k_ref[g, h]
            kt = kt_ref[g, h]
            v = v_ref[g, h]
            c = c0_ref[g, h]
            n = n0_ref[g, h:h + 1, :]
            m_prev = m0[:, h:h + 1]
            bc = b_col[:, h:h + 1]
            logd = jnp.where(causal, bc + (i_row[h:h + 1, :] - b_row[h:h + 1, :]), NEG)
            m_inter = m_prev + bc
            m_t = jnp.maximum(m_inter, jnp.max(logd, axis=1, keepdims=True))
            qb = q.astype(BF16)
            s = jnp.dot(qb, kt.astype(BF16), preferred_element_type=F32) * jnp.exp(logd - m_t)
            inter = jnp.exp(m_inter - m_t)
            num = (jnp.dot(s.astype(BF16), v.astype(BF16), preferred_element_type=F32)
                   + inter * jnp.dot(qb, c.astype(BF16), preferred_element_type=F32))
            den = (jnp.sum(s, axis=1, keepdims=True)
                   + inter * jnp.sum(q * n, axis=1, keepdims=True))
            hv = num / jnp.maximum(jnp.abs(den), jnp.exp(-m_t))
            y = _rms(hv, gh_ref[h:h + 1, :]) * _sigmoid(o_ref[g, h])
            h_ref[g, h] = y
            m_new = m_t[L - 1:L, :]
            b_last = b_row[h:h + 1, L - 1:L]
            decay = jnp.exp(m_prev + b_last - m_new)
            w_row = jnp.exp(b_last - b_row[h:h + 1, :] + i_row[h:h + 1, :] - m_new)
            w_col = jnp.exp(b_last - bc + i_col[:, h:h + 1] - m_new)
            c_ref[g, h] = decay * c + jnp.dot((kt * w_row).astype(BF16), v.astype(BF16),
                                              preferred_element_type=F32)
            n_ref[g, h:h + 1, :] = decay * n + jnp.sum(k * w_col, axis=0, keepdims=True)
            m_ref[pl.ds(g, 1), h:h + 1] = m_new
        return carry

    lax.fori_loop(0, DEC_G, body, 0)


def _mlstm_sample(q, k, kt, v, o, grow, gcol, c0, n0, m0, gh):
    nb, _, n_tok, _ = q.shape
    b4 = lambda a: pl.BlockSpec((DEC_G,) + a.shape[1:], lambda i: (i,) + (0,) * (a.ndim - 1))
    return pl.pallas_call(
        functools.partial(_mlstm_sample_kernel, n_tok),
        grid=(nb // DEC_G,),
        in_specs=[b4(q), b4(k), b4(kt), b4(v), b4(o), b4(grow), b4(gcol), b4(c0), b4(n0), b4(m0),
                  pl.BlockSpec(gh.shape, lambda i: (0, 0))],
        out_specs=(b4(q), b4(c0), b4(n0), b4(m0)),
        out_shape=(jax.ShapeDtypeStruct(q.shape, F32), jax.ShapeDtypeStruct(c0.shape, F32),
                   jax.ShapeDtypeStruct(n0.shape, F32), jax.ShapeDtypeStruct(m0.shape, F32)),
        compiler_params=_cparams(("arbitrary",), VMEM_LIMIT),
        name="mlstm_sample",
    )(q, k, kt, v, o, grow, gcol, c0, n0, m0, gh)


def _alibi_slope(h):
    return float(np.float32(2.0 ** (-8.0 * (h + 1) / N_HEADS)))


def _dup_halves(x):
    lane = lax.broadcasted_iota(jnp.int32, x.shape, 1)
    xr = pltpu.roll(x, HEAD_DIM, axis=1)
    lo = lane < HEAD_DIM
    return jnp.where(lo, x, xr), jnp.where(lo, xr, x)


def _stack_group_queries(q, g):
    lane = lax.broadcasted_iota(jnp.int32, (q.shape[0], LANES), 1)
    parts = []
    for hh in range(GROUP):
        h = GROUP * g + hh
        blk = q[:, (h // 2) * LANES:(h // 2 + 1) * LANES]
        keep = (lane < HEAD_DIM) if h % 2 == 0 else (lane >= HEAD_DIM)
        parts.append(jnp.where(keep, blk, 0.0))
    return jnp.concatenate(parts, axis=0).astype(BF16)


def _swa_prompt_kernel(sink_ref, q_ref, kp_ref, ko_ref, vp_ref, vo_ref, o_ref):
    j = pl.program_id(1)
    R = WINDOW
    q = q_ref[...]
    kd = _dup_halves(jnp.concatenate([kp_ref[...], ko_ref[...]], axis=0))
    vd = _dup_halves(jnp.concatenate([vp_ref[...], vo_ref[...]], axis=0))
    qi = lax.broadcasted_iota(jnp.int32, (R, 2 * R), 0)
    kj = lax.broadcasted_iota(jnp.int32, (R, 2 * R), 1)
    dist = qi + R - kj
    valid = (dist >= 0) & (dist < WINDOW) & ((kj >= R) | (j > 0))
    distf = dist.astype(F32)
    lane = lax.broadcasted_iota(jnp.int32, (R, LANES), 1)
    outs = []
    for g in range(N_KV):
        qs = _stack_group_queries(q, g)
        s = lax.dot_general(qs, kd[g].astype(BF16), (((1,), (1,)), ((), ())),
                            preferred_element_type=F32) * (HEAD_DIM ** -0.5)
        ps = []
        for hh in range(GROUP):
            h = GROUP * g + hh
            sink = sink_ref[h]
            sh = jnp.where(valid, s[hh * R:(hh + 1) * R] - _alibi_slope(h) * distf, NEG)
            mx = jnp.maximum(jnp.max(sh, axis=1, keepdims=True), sink)
            p = jnp.exp(sh - mx)
            p = p / (jnp.sum(p, axis=1, keepdims=True) + jnp.exp(sink - mx))
            ps.append(p.astype(BF16))
        o = jnp.dot(jnp.concatenate(ps, axis=0), vd[g].astype(BF16), preferred_element_type=F32)
        for pp in range(GROUP // 2):
            outs.append(jnp.where(lane < HEAD_DIM, o[(2 * pp) * R:(2 * pp + 1) * R],
                                  o[(2 * pp + 1) * R:(2 * pp + 2) * R]))
    o_ref[...] = jnp.concatenate(outs, axis=1).astype(o_ref.dtype)


def _swa_prompt(sinks, qa, ka, va, batch, seq):
    nb = seq // WINDOW
    own = lambda w: pl.BlockSpec((WINDOW, w), lambda b, j: (b * nb + j, 0))
    prev = lambda w: pl.BlockSpec((WINDOW, w), lambda b, j: (b * nb + jnp.maximum(j - 1, 0), 0))
    return pl.pallas_call(
        _swa_prompt_kernel,
        grid=(batch, nb),
        in_specs=[pl.BlockSpec(memory_space=pltpu.SMEM), own(DM), prev(DKV), own(DKV), prev(DKV), own(DKV)],
        out_specs=own(DM),
        out_shape=jax.ShapeDtypeStruct((batch * seq, DM), BF16),
        compiler_params=_cparams(("arbitrary", "arbitrary"), VMEM_LIMIT),
        name="swa_prompt",
    )(sinks, qa, ka, ka, va, va)


def _swa_decode_kernel(n_tok, sink_ref, q_ref, kn_ref, vn_ref, kc_ref, vc_ref, o_ref, ko_ref, vo_ref):
    W = WINDOW
    Tn = n_tok
    qt = lax.broadcasted_iota(jnp.int32, (Tn, W), 0)
    kj = lax.broadcasted_iota(jnp.int32, (Tn, W), 1)
    dist_c = qt + W - kj
    valid_c = dist_c < WINDOW
    qt2 = lax.broadcasted_iota(jnp.int32, (Tn, Tn), 0)
    kj2 = lax.broadcasted_iota(jnp.int32, (Tn, Tn), 1)
    dist_n = qt2 - kj2
    valid_n = dist_n >= 0
    dcf = dist_c.astype(F32)
    dnf = dist_n.astype(F32)
    lane = lax.broadcasted_iota(jnp.int32, (Tn, LANES), 1)

    def body(b, carry):
        rows = pl.ds(pl.multiple_of(b * Tn, Tn), Tn)
        q = q_ref[rows, :]
        kn = kn_ref[rows, :]
        vn = vn_ref[rows, :]
        kc = kc_ref[b]
        vc = vc_ref[b]
        ko_ref[b, 0:W - Tn, :] = kc[Tn:W]
        ko_ref[b, W - Tn:W, :] = kn
        vo_ref[b, 0:W - Tn, :] = vc[Tn:W]
        vo_ref[b, W - Tn:W, :] = vn
        kcd, knd = _dup_halves(kc), _dup_halves(kn)
        vcd, vnd = _dup_halves(vc), _dup_halves(vn)
        outs = []
        for g in range(N_KV):
            qs = _stack_group_queries(q, g)
            nt = (((1,), (1,)), ((), ()))
            sc = lax.dot_general(qs, kcd[g].astype(BF16), nt, preferred_element_type=F32) * (HEAD_DIM ** -0.5)
            sn = lax.dot_general(qs, knd[g].astype(BF16), nt, preferred_element_type=F32) * (HEAD_DIM ** -0.5)
            pcs, pns = [], []
            for hh in range(GROUP):
                h = GROUP * g + hh
                sink = sink_ref[h]
                slope = _alibi_slope(h)
                shc = jnp.where(valid_c, sc[hh * Tn:(hh + 1) * Tn] - slope * dcf, NEG)
                shn = jnp.where(valid_n, sn[hh * Tn:(hh + 1) * Tn] - slope * dnf, NEG)
                mx = jnp.maximum(jnp.maximum(jnp.max(shc, axis=1, keepdims=True),
                                             jnp.max(shn, axis=1, keepdims=True)), sink)
                pc = jnp.exp(shc - mx)
                pn = jnp.exp(shn - mx)
                inv = 1.0 / (jnp.sum(pc, axis=1, keepdims=True) + jnp.sum(pn, axis=1, keepdims=True)
                             + jnp.exp(sink - mx))
                pcs.append((pc * inv).astype(BF16))
                pns.append((pn * inv).astype(BF16))
            o = (jnp.dot(jnp.concatenate(pcs, axis=0), vcd[g].astype(BF16), preferred_element_type=F32)
                 + jnp.dot(jnp.concatenate(pns, axis=0), vnd[g].astype(BF16), preferred_element_type=F32))
            for pp in range(GROUP // 2):
                outs.append(jnp.where(lane < HEAD_DIM, o[(2 * pp) * Tn:(2 * pp + 1) * Tn],
                                      o[(2 * pp + 1) * Tn:(2 * pp + 2) * Tn]))
        o_ref[rows, :] = jnp.concatenate(outs, axis=1)
        return carry

    lax.fori_loop(0, DEC_G, body, 0)


def _swa_decode(sinks, qa, ka, va, kc, vc, row0, n_tok):
    nb = kc.shape[0]
    blk0 = row0 // (DEC_G * n_tok)
    tokb = lambda w: pl.BlockSpec((DEC_G * n_tok, w), lambda i: (blk0 + i, 0))
    cache = pl.BlockSpec((DEC_G, WINDOW, DKV), lambda i: (i, 0, 0))
    return pl.pallas_call(
        functools.partial(_swa_decode_kernel, n_tok),
        grid=(nb // DEC_G,),
        in_specs=[pl.BlockSpec(memory_space=pltpu.SMEM), tokb(DM), tokb(DKV), tokb(DKV), cache, cache],
        out_specs=(pl.BlockSpec((DEC_G * n_tok, DM), lambda i: (i, 0)), cache, cache),
        out_shape=(jax.ShapeDtypeStruct((nb * n_tok, DM), F32),
                   jax.ShapeDtypeStruct(kc.shape, F32), jax.ShapeDtypeStruct(vc.shape, F32)),
        compiler_params=_cparams(("arbitrary",), VMEM_LIMIT),
        name="swa_decode",
    )(sinks, qa, ka, va, kc, vc)


def _outproj_router_kernel(n_ptiles, xp_ref, xs_ref, hmp_ref, hap_ref, hms_ref, has_ref, wom_ref, woa_ref,
                           g_ref, wrt_ref, br_ref,
                           x1_ref, h2_ref, ir_ref, gcol_ref, cnt_ref, carry_s):
    i = pl.program_id(0)

    @pl.when(i == 0)
    def _():
        carry_s[...] = jnp.zeros_like(carry_s)

    is_p = i < n_ptiles
    x = jnp.where(is_p, xp_ref[...], xs_ref[...])
    hm = jnp.where(is_p, hmp_ref[...], hms_ref[...].astype(BF16))
    ha = jnp.where(is_p, hap_ref[...], has_ref[...].astype(BF16))
    x1 = (x + jnp.dot(hm, wom_ref[...], preferred_element_type=F32)
          + jnp.dot(ha, woa_ref[...], preferred_element_type=F32))
    x1_ref[...] = x1
    h2 = _rms(x1, g_ref[...])
    h2_ref[...] = h2
    logits = lax.dot_general(wrt_ref[...], h2, (((1,), (1,)), ((), ())), precision=HIGHEST,
                             preferred_element_type=F32) + br_ref[...]
    eidx = lax.broadcasted_iota(jnp.int32, logits.shape, 0).astype(F32)
    work = logits
    vals, idxs, hots = [], [], []
    for _ in range(TOP_K):
        mv = jnp.max(work, axis=0, keepdims=True)
        sel = jnp.min(jnp.where(work == mv, eidx, float(N_EXPERTS)), axis=0, keepdims=True)
        hot = eidx == sel
        vals.append(mv)
        idxs.append(sel.astype(jnp.int32))
        hots.append(hot)
        work = jnp.where(hot, -jnp.inf, work)
    es = [jnp.exp(v - vals[0]) for v in vals]
    tot = es[0] + es[1] + es[2] + es[3]
    gates = [e / tot for e in es]
    hot_all = (hots[0] | hots[1] | hots[2] | hots[3])
    tm = logits.shape[1]
    su = (lax.broadcasted_iota(jnp.int32, (tm, tm), 0) < lax.broadcasted_iota(jnp.int32, (tm, tm), 1))
    cum = jnp.dot(hot_all.astype(BF16), su.astype(BF16), preferred_element_type=F32)
    base = carry_s[:, 0:1] + cum
    ranks = [jnp.sum(jnp.where(hot, base, 0.0), axis=0, keepdims=True).astype(jnp.int32) for hot in hots]
    ir_ref[...] = jnp.concatenate(idxs + ranks, axis=0)
    g8 = jnp.concatenate(gates + [jnp.zeros((SUBLANES - TOP_K, tm), F32)], axis=0)
    gcol_ref[...] = jnp.transpose(g8)
    carry_s[...] = carry_s[...] + jnp.sum(hot_all.astype(F32), axis=1, keepdims=True)
    cnt_ref[...] = carry_s[...]


def _outproj_router(xp, xs, hmp, hap, hms, has, wom, woa, g_ffn, wrt, br):
    tp, ts = xp.shape[0], xs.shape[0]
    n_pt, n_st = tp // TM, ts // TM
    t_all = tp + ts
    pblk = lambda w: pl.BlockSpec((TM, w), lambda i: (jnp.minimum(i, n_pt - 1), 0))
    sblk = lambda w: pl.BlockSpec((TM, w), lambda i: (jnp.maximum(i - n_pt, 0), 0))
    full = lambda a: pl.BlockSpec(a.shape, lambda i: (0,) * a.ndim)
    return pl.pallas_call(
        functools.partial(_outproj_router_kernel, n_pt),
        grid=(n_pt + n_st,),
        in_specs=[pblk(D_MODEL), sblk(D_MODEL), pblk(DM), pblk(DM), sblk(DM), sblk(DM),
                  full(wom), full(woa), full(g_ffn), full(wrt), full(br)],
        out_specs=(pl.BlockSpec((TM, D_MODEL), lambda i: (i, 0)),
                   pl.BlockSpec((TM, D_MODEL), lambda i: (i, 0)),
                   pl.BlockSpec((2 * TOP_K, TM), lambda i: (0, i)),
                   pl.BlockSpec((TM, SUBLANES), lambda i: (i, 0)),
                   pl.BlockSpec((N_EXPERTS, LANES), lambda i: (0, 0))),
        out_shape=(jax.ShapeDtypeStruct((t_all, D_MODEL), F32),
                   jax.ShapeDtypeStruct((t_all, D_MODEL), F32),
                   jax.ShapeDtypeStruct((2 * TOP_K, t_all), jnp.int32),
                   jax.ShapeDtypeStruct((t_all, SUBLANES), F32),
                   jax.ShapeDtypeStruct((N_EXPERTS, LANES), F32)),
        scratch_shapes=[pltpu.VMEM((N_EXPERTS, LANES), F32)],
        compiler_params=_cparams(("arbitrary",), VMEM_LIMIT),
        name="outproj_router",
    )(xp, xs, hmp, hap, hms, has, wom, woa, g_ffn, wrt, br)


def _dispatch_kernel(zlo_ref, zhi_ref, dest_ref, h2_ref, zrow_ref, xs_ref, sem, zsem):
    i = pl.program_id(0)

    def issue(t, carry):
        for k in range(TOP_K):
            pltpu.make_async_copy(h2_ref.at[pl.ds(t, 1)], xs_ref.at[pl.ds(dest_ref[k, t], 1)], sem).start()
        return carry

    lax.fori_loop(0, DISP_T, issue, 0, unroll=8)

    for k in range(TOP_K):
        pltpu.make_async_copy(h2_ref, xs_ref.at[pl.ds(0, DISP_T)], sem).wait()

    @pl.when(i == pl.num_programs(0) - 1)
    def _():
        def per_expert(e, carry):
            def zi(r, c):
                pltpu.make_async_copy(zrow_ref, xs_ref.at[pl.ds(r, 1)], zsem).start()
                return c

            def zw(r, c):
                pltpu.make_async_copy(zrow_ref, xs_ref.at[pl.ds(0, 1)], zsem).wait()
                return c

            lax.fori_loop(zlo_ref[e], zhi_ref[e], zi, 0)
            lax.fori_loop(zlo_ref[e], zhi_ref[e], zw, 0)
            return carry

        lax.fori_loop(0, N_EXPERTS, per_expert, 0)


def _dispatch(zlo, zhi, dest, h2, n_rows):
    t_all = h2.shape[0]
    zrow = jnp.zeros((1, D_MODEL), F32)
    grid_spec = pltpu.PrefetchScalarGridSpec(
        num_scalar_prefetch=2,
        grid=(t_all // DISP_T,),
        in_specs=[pl.BlockSpec((TOP_K, DISP_T), lambda i, *_: (0, i), memory_space=pltpu.SMEM),
                  pl.BlockSpec((DISP_T, D_MODEL), lambda i, *_: (i, 0)),
                  pl.BlockSpec((1, D_MODEL), lambda i, *_: (0, 0))],
        out_specs=pl.BlockSpec(memory_space=pl.ANY),
        scratch_shapes=[pltpu.SemaphoreType.DMA(()), pltpu.SemaphoreType.DMA(())],
    )
    return pl.pallas_call(
        _dispatch_kernel,
        grid_spec=grid_spec,
        out_shape=jax.ShapeDtypeStruct((n_rows, D_MODEL), F32),
        compiler_params=_cparams(("arbitrary",)),
        name="moe_dispatch",
    )(zlo, zhi, dest, h2, zrow)


def _expert_kernel(be_ref, na_ref, x_ref, wgu_ref, bgu_ref, wd_ref, bd_ref, o_ref, wgu_s, wd_s):
    i = pl.program_id(0)

    @pl.when(i < na_ref[0])
    def _():
        changed = (i == 0) | (be_ref[i] != be_ref[jnp.maximum(i - 1, 0)])

        @pl.when(changed)
        def _():
            wgu_s[...] = wgu_ref[...].astype(BF16)
            wd_s[...] = wd_ref[...].astype(BF16)

        x = x_ref[...].astype(BF16)
        gu = jnp.dot(x, wgu_s[...], preferred_element_type=F32) + bgu_ref[...]
        gate = jnp.minimum(gu[:, :D_FF], SWIGLU_LIMIT)
        up = jnp.clip(gu[:, D_FF:], -SWIGLU_LIMIT, SWIGLU_LIMIT)
        act = (up + 1.0) * (gate * _sigmoid(gate * SWIGLU_ALPHA))
        o_ref[...] = jnp.dot(act.astype(BF16), wd_s[...], preferred_element_type=F32) + bd_ref[...]


def _expert_ffn(block_exp, n_active, xs, wgu, bgu, wd, bd):
    n_rows = xs.shape[0]
    n_blocks = n_rows // MOE_BM
    rowblk = pl.BlockSpec((MOE_BM, D_MODEL), lambda i, be, na: (jnp.minimum(i, na[0] - 1), 0))
    grid_spec = pltpu.PrefetchScalarGridSpec(
        num_scalar_prefetch=2,
        grid=(n_blocks,),
        in_specs=[rowblk,
                  pl.BlockSpec((None, D_MODEL, 2 * D_FF), lambda i, be, na: (be[i], 0, 0)),
                  pl.BlockSpec((None, 1, 2 * D_FF), lambda i, be, na: (be[i], 0, 0)),
                  pl.BlockSpec((None, D_FF, D_MODEL), lambda i, be, na: (be[i], 0, 0)),
                  pl.BlockSpec((None, 1, D_MODEL), lambda i, be, na: (be[i], 0, 0))],
        out_specs=rowblk,
        scratch_shapes=[pltpu.VMEM((D_MODEL, 2 * D_FF), BF16), pltpu.VMEM((D_FF, D_MODEL), BF16)],
    )
    return pl.pallas_call(
        _expert_kernel,
        grid_spec=grid_spec,
        out_shape=jax.ShapeDtypeStruct((n_rows, D_MODEL), F32),
        compiler_params=_cparams(("arbitrary",), VMEM_LIMIT),
        name="moe_experts",
    )(block_exp, n_active, xs, wgu, bgu, wd, bd)


def _combine_kernel(n_ptiles, dest_ref, outs_ref, x1_ref, gcol_ref, gf_ref, yp_ref, ys_ref, buf, sem):
    i = pl.program_id(0)

    def issue(t, carry):
        for k in range(TOP_K):
            pltpu.make_async_copy(outs_ref.at[pl.ds(dest_ref[k, t], 1)], buf.at[k, pl.ds(t, 1)], sem).start()
        return carry

    lax.fori_loop(0, COMB_T, issue, 0, unroll=8)

    for k in range(TOP_K):
        pltpu.make_async_copy(outs_ref.at[pl.ds(0, COMB_T)], buf.at[k], sem).wait()

    acc = x1_ref[...]
    gc = gcol_ref[...]
    for k in range(TOP_K):
        acc = acc + gc[:, k:k + 1] * buf[k]
    y = _rms(acc, gf_ref[...])

    @pl.when(i < n_ptiles)
    def _():
        yp_ref[...] = y

    @pl.when(i >= n_ptiles)
    def _():
        ys_ref[...] = y


def _combine(dest, outs, x1, gcol, g_final, tp, ts):
    n_pt, n_st = tp // COMB_T, ts // COMB_T
    return pl.pallas_call(
        functools.partial(_combine_kernel, n_pt),
        grid=(n_pt + n_st,),
        in_specs=[pl.BlockSpec((TOP_K, COMB_T), lambda i: (0, i), memory_space=pltpu.SMEM),
                  pl.BlockSpec(memory_space=pl.ANY),
                  pl.BlockSpec((COMB_T, D_MODEL), lambda i: (i, 0)),
                  pl.BlockSpec((COMB_T, SUBLANES), lambda i: (i, 0)),
                  pl.BlockSpec((1, D_MODEL), lambda i: (0, 0))],
        out_specs=(pl.BlockSpec((COMB_T, D_MODEL), lambda i: (jnp.minimum(i, n_pt - 1), 0)),
                   pl.BlockSpec((COMB_T, D_MODEL), lambda i: (jnp.maximum(i - n_pt, 0), 0))),
        out_shape=(jax.ShapeDtypeStruct((tp, D_MODEL), F32), jax.ShapeDtypeStruct((ts, D_MODEL), F32)),
        scratch_shapes=[pltpu.VMEM((TOP_K, COMB_T, D_MODEL), F32), pltpu.SemaphoreType.DMA(())],
        compiler_params=_cparams(("arbitrary",), VMEM_LIMIT),
        name="moe_combine",
    )(dest, outs, x1, gcol, g_final)


def kernel(x_prompt, x_sample, cache_swa_k, cache_swa_v, state_mlstm_c, state_mlstm_n, state_mlstm_m,
           g_mix, w_in, b_igate, b_fgate, g_head, attn_sinks, w_out, g_ffn, w_router, b_router,
           w_gate_up, b_gate_up, w_down, b_down, g_final):
    assert w_in.shape[0] == 1, "single-layer problem"
    B, S, _ = x_prompt.shape
    Bd, Tn, _ = x_sample.shape
    tp, ts = B * S, Bd * Tn
    t_all = tp + ts
    xp = x_prompt.reshape(tp, D_MODEL)
    xs = x_sample.reshape(ts, D_MODEL)

    w = w_in[0]
    o = np.cumsum([0, DM, DM, DM, DM, N_HEADS, N_HEADS, DM, DKV, DKV])
    col = lambda a: w[:, int(o[a]):int(o[a + 1])]
    w1 = jnp.concatenate([col(0), col(2), col(3), col(6), col(7), col(8)], axis=1).astype(BF16)
    wkt = col(1).T.astype(BF16)
    wgates = jnp.concatenate([col(4), col(5)], axis=1)
    wg = jnp.pad(wgates, ((0, 0), (0, LANES - 2 * N_HEADS))).astype(BF16)
    wgt = wgates.T.astype(BF16)
    bg = jnp.concatenate([b_igate[0], b_fgate[0]]).astype(F32)
    bcol = jnp.pad(bg, (0, LANES - 2 * N_HEADS)).reshape(1, LANES)
    brow = bg.reshape(2 * N_HEADS, 1)

    qm, vm, om, qa, ka, va, kmt, gcol, grow = _inproj(xp, xs, g_mix[0].reshape(1, D_MODEL), w1, wkt, wg, wgt,
                                                      bcol, brow)

    gh = g_head[0].astype(F32)
    sinks = attn_sinks[0].astype(F32)

    hm_p, cbd, m_p = _mlstm_prompt(qm, kmt, vm, om, gcol, grow, gh.reshape(1, DM), B, S)
    ha_p = _swa_prompt(sinks, qa, ka, va, B, S)

    hd = lambda a: a[tp:].reshape(Bd, Tn, N_HEADS, HEAD_DIM).transpose(0, 2, 1, 3)
    kts = kmt[:, tp:].reshape(N_HEADS, HEAD_DIM, Bd, Tn)
    grow_s = grow[:, tp:].reshape(2 * N_HEADS, Bd, Tn).transpose(1, 0, 2)
    gcol_s = gcol[tp:, :2 * N_HEADS].reshape(Bd, Tn, 2 * N_HEADS)
    hm_s, c_s, n_s, m_s = _mlstm_sample(
        hd(qm), kts.transpose(2, 0, 3, 1), kts.transpose(2, 0, 1, 3), hd(vm), hd(om), grow_s, gcol_s,
        state_mlstm_c[0], state_mlstm_n[0], state_mlstm_m[0], gh)
    hm_s = hm_s.transpose(0, 2, 1, 3).reshape(ts, DM)
    ha_s, k_s, v_s = _swa_decode(sinks, qa, ka, va, cache_swa_k[0].reshape(Bd, WINDOW, DKV),
                                 cache_swa_v[0].reshape(Bd, WINDOW, DKV), tp, Tn)

    wo = w_out[0].astype(BF16)
    x1, h2, ir, gates_col, cnt = _outproj_router(
        xp, xs, hm_p, ha_p, hm_s, ha_s, wo[:DM], wo[DM:], g_ffn[0].reshape(1, D_MODEL),
        w_router[0].T, b_router[0].reshape(N_EXPERTS, 1))

    n_blocks = (t_all * TOP_K) // MOE_BM + N_EXPERTS
    n_rows = n_blocks * MOE_BM
    counts = cnt[:, 0].astype(jnp.int32)
    padded = (counts + MOE_BM - 1) // MOE_BM * MOE_BM
    pad_end = jnp.cumsum(padded)
    pad_start = pad_end - padded
    eids = jnp.arange(N_EXPERTS, dtype=jnp.int32)[:, None, None]
    dest = jnp.sum(jnp.where(ir[None, :TOP_K] == eids, pad_start[:, None, None], 0), axis=0) + ir[TOP_K:]
    n_active = (pad_end[-1] // MOE_BM).astype(jnp.int32)
    blk = jnp.minimum(jnp.arange(n_blocks, dtype=jnp.int32), n_active - 1)
    block_exp = jnp.minimum(jnp.sum((pad_end[None, :] <= (blk * MOE_BM)[:, None]).astype(jnp.int32), axis=1),
                            N_EXPERTS - 1)

    xs_sorted = _dispatch(pad_start + counts, pad_end, dest, h2, n_rows)
    outs = _expert_ffn(block_exp, n_active.reshape(1), xs_sorted, w_gate_up[0],
                       b_gate_up[0].reshape(N_EXPERTS, 1, 2 * D_FF), w_down[0],
                       b_down[0].reshape(N_EXPERTS, 1, D_MODEL))
    y_p, y_s = _combine(dest, outs, x1, gates_col, g_final.reshape(1, D_MODEL), tp, ts)

    kv_tail = lambda a: a[:tp].reshape(B, S, N_KV, HEAD_DIM)[:, S - WINDOW:][None]
    c_e = cbd[:, :, :HEAD_DIM, :HEAD_DIM]
    c_o = cbd[:, :, HEAD_DIM:, HEAD_DIM:LANES]
    c_p = jnp.stack([c_e, c_o], axis=2).reshape(B, N_HEADS, HEAD_DIM, HEAD_DIM)
    n_p = jnp.stack([cbd[:, :, :HEAD_DIM, LANES], cbd[:, :, HEAD_DIM:, LANES + 1]], axis=2).reshape(B, N_HEADS, HEAD_DIM)
    return (y_p.reshape(B, S, D_MODEL), y_s.reshape(Bd, Tn, D_MODEL),
            kv_tail(ka), kv_tail(va), c_p[None], n_p[None], m_p[:, :, 0][None],
            k_s.reshape(Bd, WINDOW, N_KV, HEAD_DIM)[None], v_s.reshape(Bd, WINDOW, N_KV, HEAD_DIM)[None],
            c_s[None], n_s[None], m_s[None])
```

```python
import functools
import math

import jax
import jax.numpy as jnp
import numpy as np
from jax import lax
from jax.experimental import pallas as pl
from jax.experimental.pallas import tpu as pltpu

F32 = jnp.float32
BF16 = jnp.bfloat16
HIGHEST = lax.Precision.HIGHEST

D_MODEL = 1024
HEAD_DIM = 64
N_HEADS = 8
N_PAIRS = N_HEADS // 2
N_KV = 2
GROUP = N_HEADS // N_KV
WINDOW = 128
N_EXPERTS = 32
TOP_K = 4
D_FF = 1024
SWIGLU_LIMIT = 7.0
SWIGLU_ALPHA = 1.702
RMS_EPS = 1e-5
DM = N_HEADS * HEAD_DIM
DKV = N_KV * HEAD_DIM
NEG = -1e30

LANES = 128
SUBLANES = 8
VMEM_LIMIT = 56 * 1024 * 1024

TM = 512
MLSTM_TL = 512
MLSTM_L = 128
DEC_G = 8
MOE_BM = 256
GROUP_R = -(-(TOP_K * TM + N_EXPERTS * (SUBLANES - 1) + SUBLANES) // LANES) * LANES


def _cparams(sem, vmem=None):
    return pltpu.CompilerParams(dimension_semantics=sem, vmem_limit_bytes=vmem)


def _rms(x, g):
    return x * lax.rsqrt(jnp.mean(x * x, axis=-1, keepdims=True) + RMS_EPS) * g


def _log_sigmoid(z):
    return jnp.minimum(z, 0.0) - jnp.log(1.0 + jnp.exp(-jnp.abs(z)))


def _sigmoid(z):
    return 1.0 / (1.0 + jnp.exp(-z))


def _inproj_kernel(n_ptiles, xp_ref, xs_ref, g_ref, w1_ref, wkt_ref, wg_ref, wgt_ref, bcol_ref, brow_ref,
                   qm_ref, vm_ref, om_ref, qa_ref, ka_ref, va_ref, kmt_ref, gcol_ref, grow_ref):
    i = pl.program_id(0)
    x = jnp.where(i < n_ptiles, xp_ref[...], xs_ref[...])
    h = _rms(x, g_ref[...]).astype(BF16)
    main = jnp.dot(h, w1_ref[...], preferred_element_type=F32)
    qm_ref[...] = main[:, 0:DM]
    vm_ref[...] = main[:, DM:2 * DM]
    om_ref[...] = main[:, 2 * DM:3 * DM]
    qa_ref[...] = main[:, 3 * DM:4 * DM]
    ka_ref[...] = main[:, 4 * DM:4 * DM + DKV]
    va_ref[...] = main[:, 4 * DM + DKV:4 * DM + 2 * DKV]
    kt = lax.dot_general(wkt_ref[...], h, (((1,), (1,)), ((), ())), preferred_element_type=F32)
    kmt_ref[...] = kt * (HEAD_DIM ** -0.5)
    zc = jnp.dot(h, wg_ref[...], preferred_element_type=F32) + bcol_ref[...]
    lane = lax.broadcasted_iota(jnp.int32, zc.shape, 1)
    gcol_ref[...] = jnp.where(lane < N_HEADS, zc, _log_sigmoid(zc))
    zr = lax.dot_general(wgt_ref[...], h, (((1,), (1,)), ((), ())), preferred_element_type=F32) + brow_ref[...]
    row = lax.broadcasted_iota(jnp.int32, zr.shape, 0)
    grow_ref[...] = jnp.where(row < N_HEADS, zr, _log_sigmoid(zr))


def _inproj(xp, xs, g_mix, w1, wkt, wg, wgt, bcol, brow):
    tp, ts = xp.shape[0], xs.shape[0]
    n_pt, n_st = tp // TM, ts // TM
    t_all = tp + ts
    n1 = w1.shape[1]
    tok = lambda w: pl.BlockSpec((TM, w), lambda i: (i, 0))
    full = lambda a: pl.BlockSpec(a.shape, lambda i: (0,) * a.ndim)
    out_shape = (
        jax.ShapeDtypeStruct((t_all, DM), F32), jax.ShapeDtypeStruct((t_all, DM), F32),
        jax.ShapeDtypeStruct((t_all, DM), F32), jax.ShapeDtypeStruct((t_all, DM), F32),
        jax.ShapeDtypeStruct((t_all, DKV), F32), jax.ShapeDtypeStruct((t_all, DKV), F32),
        jax.ShapeDtypeStruct((DM, t_all), F32),
        jax.ShapeDtypeStruct((t_all, LANES), F32), jax.ShapeDtypeStruct((2 * N_HEADS, t_all), F32),
    )
    return pl.pallas_call(
        functools.partial(_inproj_kernel, n_pt),
        grid=(n_pt + n_st,),
        in_specs=[
            pl.BlockSpec((TM, D_MODEL), lambda i: (jnp.minimum(i, n_pt - 1), 0)),
            pl.BlockSpec((TM, D_MODEL), lambda i: (jnp.maximum(i - n_pt, 0), 0)),
            full(g_mix), full(w1), full(wkt), full(wg), full(wgt), full(bcol), full(brow),
        ],
        out_specs=(tok(DM), tok(DM), tok(DM), tok(DM), tok(DKV), tok(DKV),
                   pl.BlockSpec((DM, TM), lambda i: (0, i)),
                   tok(LANES), pl.BlockSpec((2 * N_HEADS, TM), lambda i: (0, i))),
        out_shape=out_shape,
        compiler_params=_cparams(("arbitrary",), VMEM_LIMIT),
        name="inproj",
    )(xp, xs, g_mix, w1, wkt, wg, wgt, bcol, brow)


def _cumsum_rows(x, n):
    row = lax.broadcasted_iota(jnp.int32, x.shape, 0)
    sh = 1
    while sh < n:
        x = x + jnp.where(row >= sh, pltpu.roll(x, sh, axis=0), 0.0)
        sh *= 2
    return x


def _mlstm_prompt_kernel(qm_ref, kmt_ref, vm_ref, om_ref, gcol_ref, grow_ref, gh_ref,
                         hm_ref, cbd_ref, m_ref, cbd_s, m_s):
    j = pl.program_id(1)
    L = MLSTM_L

    @pl.when(j == 0)
    def _():
        cbd_s[...] = jnp.zeros_like(cbd_s)
        m_s[...] = jnp.zeros_like(m_s)

    ti = lax.broadcasted_iota(jnp.int32, (L, L), 0)
    si = lax.broadcasted_iota(jnp.int32, (L, L), 1)
    causal = ti >= si
    upper = (ti <= si).astype(F32)
    lane128 = lax.broadcasted_iota(jnp.int32, (L, LANES), 1)
    even128 = lane128 < HEAD_DIM
    lane256 = lax.broadcasted_iota(jnp.int32, (1, 2 * LANES), 1)
    cols_e = (lane256 < HEAD_DIM) | (lane256 == LANES)
    cols_o = ((lane256 >= HEAD_DIM) & (lane256 < LANES)) | (lane256 == LANES + 1)
    rowk = lax.broadcasted_iota(jnp.int32, (LANES, 1), 0)
    rows_e = rowk < HEAD_DIM
    bd_mask = (rows_e & cols_e) | ((~rows_e) & cols_o)
    ones_cols = ((lax.broadcasted_iota(jnp.int32, (L, LANES), 1) < 2)).astype(F32)
    bo_r = lax.broadcasted_iota(jnp.int32, (LANES, LANES), 0) // HEAD_DIM
    bo_c = lax.broadcasted_iota(jnp.int32, (LANES, LANES), 1) // HEAD_DIM
    block_ones = (bo_r == bo_c).astype(F32)

    for c in range(MLSTM_TL // L):
        sl = slice(c * L, (c + 1) * L)
        grow = grow_ref[:, sl]
        i_row = grow[0:N_HEADS]
        b_row = jnp.dot(grow[N_HEADS:2 * N_HEADS], upper, precision=HIGHEST,
                        preferred_element_type=F32)
        b_colall = _cumsum_rows(gcol_ref[sl, :], L)
        rowv_all = i_row - b_row
        for p in range(N_PAIRS):
            ls = slice(p * LANES, (p + 1) * LANES)
            q2 = qm_ref[sl, ls].astype(BF16)
            kt2 = kmt_ref[ls, sl]
            v2 = vm_ref[sl, ls]
            vext = jnp.concatenate([v2, ones_cols], axis=1)
            kt_e = jnp.where(rows_e, kt2, 0.0).astype(BF16)
            kt_o = jnp.where(rows_e, 0.0, kt2).astype(BF16)
            s2 = jnp.dot(q2, jnp.concatenate([kt_e, kt_o], axis=1), preferred_element_type=F32)
            cbd = cbd_s[p]
            rq = jnp.dot(q2, cbd.astype(BF16), preferred_element_type=F32)
            ps, mts, inters, wrows, decays, mnews = [], [], [], [], [], []
            for hh in range(2):
                h = 2 * p + hh
                bcol = b_colall[:, N_HEADS + h:N_HEADS + h + 1]
                logd = jnp.where(causal, bcol + rowv_all[h:h + 1, :], NEG)
                m_prev = m_s[h:h + 1, 0:1]
                m_inter = m_prev + bcol
                m_t = jnp.maximum(m_inter, jnp.max(logd, axis=1, keepdims=True))
                d = jnp.exp(logd - m_t)
                ps.append((s2[:, hh * L:(hh + 1) * L] * d).astype(BF16))
                mts.append(m_t)
                inters.append(jnp.exp(m_inter - m_t))
                m_new = m_t[L - 1:L, :]
                b_last = b_row[h:h + 1, L - 1:L]
                decays.append(jnp.exp(m_prev + b_last - m_new))
                wrows.append(jnp.exp(b_last - b_row[h:h + 1, :] + i_row[h:h + 1, :] - m_new))
                mnews.append(m_new)
            vstack = jnp.concatenate([jnp.where(cols_e, vext, 0.0), jnp.where(cols_o, vext, 0.0)],
                                     axis=0).astype(BF16)
            r = jnp.dot(jnp.concatenate(ps, axis=1), vstack, preferred_element_type=F32)
            nd = r + jnp.where(cols_e, inters[0], inters[1]) * rq
            num = nd[:, 0:LANES]
            den = jnp.where(even128, nd[:, LANES:LANES + 1], nd[:, LANES + 1:LANES + 2])
            mt2 = jnp.where(even128, mts[0], mts[1])
            hv = num / jnp.maximum(jnp.abs(den), jnp.exp(-mt2))
            ms = jnp.dot(hv * hv, block_ones, precision=HIGHEST, preferred_element_type=F32) * (1.0 / HEAD_DIM)
            y = hv * lax.rsqrt(ms + RMS_EPS) * gh_ref[:, ls] * _sigmoid(om_ref[sl, ls])
            hm_ref[sl, ls] = y.astype(hm_ref.dtype)
            w2 = jnp.where(rows_e, wrows[0], wrows[1])
            upd = jnp.dot((kt2 * w2).astype(BF16), vext.astype(BF16), preferred_element_type=F32)
            dec2 = jnp.where(rows_e, decays[0], decays[1])
            cbd_s[p] = dec2 * cbd + jnp.where(bd_mask, upd, 0.0)
            for hh in range(2):
                h = 2 * p + hh
                m_s[h:h + 1, :] = jnp.broadcast_to(mnews[hh], (1, LANES))

    @pl.when(j == pl.num_programs(1) - 1)
    def _():
        cbd_ref[0] = cbd_s[...]
        m_ref[0] = m_s[...]


def _mlstm_prompt(qm, kmt, vm, om, gcol, grow, gh, batch, seq):
    nt = seq // MLSTM_TL
    tokb = lambda w: pl.BlockSpec((MLSTM_TL, w), lambda b, j: (b * nt + j, 0))
    rowb = lambda r: pl.BlockSpec((r, MLSTM_TL), lambda b, j: (0, b * nt + j))
    return pl.pallas_call(
        _mlstm_prompt_kernel,
        grid=(batch, nt),
        in_specs=[tokb(DM), rowb(DM), tokb(DM), tokb(DM), tokb(LANES), rowb(2 * N_HEADS),
                  pl.BlockSpec((1, DM), lambda b, j: (0, 0))],
        out_specs=(tokb(DM),
                   pl.BlockSpec((1, N_PAIRS, LANES, 2 * LANES), lambda b, j: (b, 0, 0, 0)),
                   pl.BlockSpec((1, N_HEADS, LANES), lambda b, j: (b, 0, 0))),
        out_shape=(jax.ShapeDtypeStruct((batch * seq, DM), BF16),
                   jax.ShapeDtypeStruct((batch, N_PAIRS, LANES, 2 * LANES), F32),
                   jax.ShapeDtypeStruct((batch, N_HEADS, LANES), F32)),
        scratch_shapes=[pltpu.VMEM((N_PAIRS, LANES, 2 * LANES), F32), pltpu.VMEM((N_HEADS, LANES), F32)],
        compiler_params=_cparams(("arbitrary", "arbitrary"), VMEM_LIMIT),
        name="mlstm_prompt",
    )(qm, kmt, vm, om, gcol, grow, gh)


def _mlstm_sample_kernel(n_tok, q_ref, k_ref, kt_ref, v_ref, o_ref, grow_ref, gcol_ref, c0_ref, n0_ref, m0_ref,
                         gh_ref, h_ref, c_ref, n_ref, m_ref):
    L = n_tok
    ti = lax.broadcasted_iota(jnp.int32, (L, L), 0)
    si = lax.broadcasted_iota(jnp.int32, (L, L), 1)
    causal = ti >= si
    upper = (ti <= si).astype(F32)
    lower = causal.astype(F32)

    def body(g, carry):
        grow = grow_ref[g]
        gcol = gcol_ref[g]
        i_row = grow[0:N_HEADS]
        b_row = jnp.dot(grow[N_HEADS:], upper, precision=HIGHEST, preferred_element_type=F32)
        b_col = jnp.dot(lower, gcol[:, N_HEADS:], precision=HIGHEST, preferred_element_type=F32)
        i_col = gcol[:, 0:N_HEADS]
        m0 = m0_ref[pl.ds(g, 1), :]
        for h in range(N_HEADS):
            q = q_ref[g, h]
            k = k_ref[g, h]
            kt = kt_ref[g, h]
            v = v_ref[g, h]
            c = c0_ref[g, h]
            n = n0_ref[g, h:h + 1, :]
            m_prev = m0[:, h:h + 1]
            bc = b_col[:, h:h + 1]
            logd = jnp.where(causal, bc + (i_row[h:h + 1, :] - b_row[h:h + 1, :]), NEG)
            m_inter = m_prev + bc
            m_t = jnp.maximum(m_inter, jnp.max(logd, axis=1, keepdims=True))
            qb = q.astype(BF16)
            s = jnp.dot(qb, kt.astype(BF16), preferred_element_type=F32) * jnp.exp(logd - m_t)
            inter = jnp.exp(m_inter - m_t)
            num = (jnp.dot(s.astype(BF16), v.astype(BF16), preferred_element_type=F32)
                   + inter * jnp.dot(qb, c.astype(BF16), preferred_element_type=F32))
            den = (jnp.sum(s, axis=1, keepdims=True)
                   + inter * jnp.sum(q * n, axis=1, keepdims=True))
            hv = num / jnp.maximum(jnp.abs(den), jnp.exp(-m_t))
            y = _rms(hv, gh_ref[h:h + 1, :]) * _sigmoid(o_ref[g, h])
            h_ref[g, h] = y
            m_new = m_t[L - 1:L, :]
            b_last = b_row[h:h + 1, L - 1:L]
            decay = jnp.exp(m_prev + b_last - m_new)
            w_row = jnp.exp(b_last - b_row[h:h + 1, :] + i_row[h:h + 1, :] - m_new)
            w_col = jnp.exp(b_last - bc + i_col[:, h:h + 1] - m_new)
            c_ref[g, h] = decay * c + jnp.dot((kt * w_row).astype(BF16), v.astype(BF16),
                                              preferred_element_type=F32)
            n_ref[g, h:h + 1, :] = decay * n + jnp.sum(k * w_col, axis=0, keepdims=True)
            m_ref[pl.ds(g, 1), h:h + 1] = m_new
        return carry

    lax.fori_loop(0, DEC_G, body, 0)


def _mlstm_sample(q, k, kt, v, o, grow, gcol, c0, n0, m0, gh):
    nb, _, n_tok, _ = q.shape
    b4 = lambda a: pl.BlockSpec((DEC_G,) + a.shape[1:], lambda i: (i,) + (0,) * (a.ndim - 1))
    return pl.pallas_call(
        functools.partial(_mlstm_sample_kernel, n_tok),
        grid=(nb // DEC_G,),
        in_specs=[b4(q), b4(k), b4(kt), b4(v), b4(o), b4(grow), b4(gcol), b4(c0), b4(n0), b4(m0),
                  pl.BlockSpec(gh.shape, lambda i: (0, 0))],
        out_specs=(b4(q), b4(c0), b4(n0), b4(m0)),
        out_shape=(jax.ShapeDtypeStruct(q.shape, F32), jax.ShapeDtypeStruct(c0.shape, F32),
                   jax.ShapeDtypeStruct(n0.shape, F32), jax.ShapeDtypeStruct(m0.shape, F32)),
        compiler_params=_cparams(("arbitrary",), VMEM_LIMIT),
        name="mlstm_sample",
    )(q, k, kt, v, o, grow, gcol, c0, n0, m0, gh)


def _alibi_slope(h):
    return float(np.float32(2.0 ** (-8.0 * (h + 1) / N_HEADS)))


def _dup_halves(x):
    lane = lax.broadcasted_iota(jnp.int32, x.shape, 1)
    xr = pltpu.roll(x, HEAD_DIM, axis=1)
    lo = lane < HEAD_DIM
    return jnp.where(lo, x, xr), jnp.where(lo, xr, x)


def _stack_group_queries(q, g):
    lane = lax.broadcasted_iota(jnp.int32, (q.shape[0], LANES), 1)
    parts = []
    for hh in range(GROUP):
        h = GROUP * g + hh
        blk = q[:, (h // 2) * LANES:(h // 2 + 1) * LANES]
        keep = (lane < HEAD_DIM) if h % 2 == 0 else (lane >= HEAD_DIM)
        parts.append(jnp.where(keep, blk, 0.0))
    return jnp.concatenate(parts, axis=0).astype(BF16)


def _swa_prompt_kernel(sink_ref, q_ref, kp_ref, ko_ref, vp_ref, vo_ref, o_ref):
    j = pl.program_id(1)
    R = WINDOW
    q = q_ref[...]
    kd = _dup_halves(jnp.concatenate([kp_ref[...], ko_ref[...]], axis=0))
    vd = _dup_halves(jnp.concatenate([vp_ref[...], vo_ref[...]], axis=0))
    qi = lax.broadcasted_iota(jnp.int32, (R, 2 * R), 0)
    kj = lax.broadcasted_iota(jnp.int32, (R, 2 * R), 1)
    dist = qi + R - kj
    valid = (dist >= 0) & (dist < WINDOW) & ((kj >= R) | (j > 0))
    distf = dist.astype(F32)
    lane = lax.broadcasted_iota(jnp.int32, (R, LANES), 1)
    outs = []
    for g in range(N_KV):
        qs = _stack_group_queries(q, g)
        s = lax.dot_general(qs, kd[g].astype(BF16), (((1,), (1,)), ((), ())),
                            preferred_element_type=F32) * (HEAD_DIM ** -0.5)
        ps = []
        for hh in range(GROUP):
            h = GROUP * g + hh
            sink = sink_ref[h]
            sh = jnp.where(valid, s[hh * R:(hh + 1) * R] - _alibi_slope(h) * distf, NEG)
            mx = jnp.maximum(jnp.max(sh, axis=1, keepdims=True), sink)
            p = jnp.exp(sh - mx)
            p = p / (jnp.sum(p, axis=1, keepdims=True) + jnp.exp(sink - mx))
            ps.append(p.astype(BF16))
        o = jnp.dot(jnp.concatenate(ps, axis=0), vd[g].astype(BF16), preferred_element_type=F32)
        for pp in range(GROUP // 2):
            outs.append(jnp.where(lane < HEAD_DIM, o[(2 * pp) * R:(2 * pp + 1) * R],
                                  o[(2 * pp + 1) * R:(2 * pp + 2) * R]))
    o_ref[...] = jnp.concatenate(outs, axis=1).astype(o_ref.dtype)


def _swa_prompt(sinks, qa, ka, va, batch, seq):
    nb = seq // WINDOW
    own = lambda w: pl.BlockSpec((WINDOW, w), lambda b, j: (b * nb + j, 0))
    prev = lambda w: pl.BlockSpec((WINDOW, w), lambda b, j: (b * nb + jnp.maximum(j - 1, 0), 0))
    return pl.pallas_call(
        _swa_prompt_kernel,
        grid=(batch, nb),
        in_specs=[pl.BlockSpec(memory_space=pltpu.SMEM), own(DM), prev(DKV), own(DKV), prev(DKV), own(DKV)],
        out_specs=own(DM),
        out_shape=jax.ShapeDtypeStruct((batch * seq, DM), BF16),
        compiler_params=_cparams(("arbitrary", "arbitrary"), VMEM_LIMIT),
        name="swa_prompt",
    )(sinks, qa, ka, ka, va, va)


def _swa_decode_kernel(n_tok, sink_ref, q_ref, kn_ref, vn_ref, kc_ref, vc_ref, o_ref, ko_ref, vo_ref):
    W = WINDOW
    Tn = n_tok
    qt = lax.broadcasted_iota(jnp.int32, (Tn, W), 0)
    kj = lax.broadcasted_iota(jnp.int32, (Tn, W), 1)
    dist_c = qt + W - kj
    valid_c = dist_c < WINDOW
    qt2 = lax.broadcasted_iota(jnp.int32, (Tn, Tn), 0)
    kj2 = lax.broadcasted_iota(jnp.int32, (Tn, Tn), 1)
    dist_n = qt2 - kj2
    valid_n = dist_n >= 0
    dcf = dist_c.astype(F32)
    dnf = dist_n.astype(F32)
    lane = lax.broadcasted_iota(jnp.int32, (Tn, LANES), 1)

    def body(b, carry):
        rows = pl.ds(pl.multiple_of(b * Tn, Tn), Tn)
        q = q_ref[rows, :]
        kn = kn_ref[rows, :]
        vn = vn_ref[rows, :]
        kc = kc_ref[b]
        vc = vc_ref[b]
        ko_ref[b, 0:W - Tn, :] = kc[Tn:W]
        ko_ref[b, W - Tn:W, :] = kn
        vo_ref[b, 0:W - Tn, :] = vc[Tn:W]
        vo_ref[b, W - Tn:W, :] = vn
        kcd, knd = _dup_halves(kc), _dup_halves(kn)
        vcd, vnd = _dup_halves(vc), _dup_halves(vn)
        outs = []
        for g in range(N_KV):
            qs = _stack_group_queries(q, g)
            nt = (((1,), (1,)), ((), ()))
            sc = lax.dot_general(qs, kcd[g].astype(BF16), nt, preferred_element_type=F32) * (HEAD_DIM ** -0.5)
            sn = lax.dot_general(qs, knd[g].astype(BF16), nt, preferred_element_type=F32) * (HEAD_DIM ** -0.5)
            pcs, pns = [], []
            for hh in range(GROUP):
                h = GROUP * g + hh
                sink = sink_ref[h]
                slope = _alibi_slope(h)
                shc = jnp.where(valid_c, sc[hh * Tn:(hh + 1) * Tn] - slope * dcf, NEG)
                shn = jnp.where(valid_n, sn[hh * Tn:(hh + 1) * Tn] - slope * dnf, NEG)
                mx = jnp.maximum(jnp.maximum(jnp.max(shc, axis=1, keepdims=True),
                                             jnp.max(shn, axis=1, keepdims=True)), sink)
                pc = jnp.exp(shc - mx)
                pn = jnp.exp(shn - mx)
                inv = 1.0 / (jnp.sum(pc, axis=1, keepdims=True) + jnp.sum(pn, axis=1, keepdims=True)
                             + jnp.exp(sink - mx))
                pcs.append((pc * inv).astype(BF16))
                pns.append((pn * inv).astype(BF16))
            o = (jnp.dot(jnp.concatenate(pcs, axis=0), vcd[g].astype(BF16), preferred_element_type=F32)
                 + jnp.dot(jnp.concatenate(pns, axis=0), vnd[g].astype(BF16), preferred_element_type=F32))
            for pp in range(GROUP // 2):
                outs.append(jnp.where(lane < HEAD_DIM, o[(2 * pp) * Tn:(2 * pp + 1) * Tn],
                                      o[(2 * pp + 1) * Tn:(2 * pp + 2) * Tn]))
        o_ref[rows, :] = jnp.concatenate(outs, axis=1)
        return carry

    lax.fori_loop(0, DEC_G, body, 0)


def _swa_decode(sinks, qa, ka, va, kc, vc, row0, n_tok):
    nb = kc.shape[0]
    blk0 = row0 // (DEC_G * n_tok)
    tokb = lambda w: pl.BlockSpec((DEC_G * n_tok, w), lambda i: (blk0 + i, 0))
    cache = pl.BlockSpec((DEC_G, WINDOW, DKV), lambda i: (i, 0, 0))
    return pl.pallas_call(
        functools.partial(_swa_decode_kernel, n_tok),
        grid=(nb // DEC_G,),
        in_specs=[pl.BlockSpec(memory_space=pltpu.SMEM), tokb(DM), tokb(DKV), tokb(DKV), cache, cache],
        out_specs=(pl.BlockSpec((DEC_G * n_tok, DM), lambda i: (i, 0)), cache, cache),
        out_shape=(jax.ShapeDtypeStruct((nb * n_tok, DM), F32),
                   jax.ShapeDtypeStruct(kc.shape, F32), jax.ShapeDtypeStruct(vc.shape, F32)),
        compiler_params=_cparams(("arbitrary",), VMEM_LIMIT),
        name="swa_decode",
    )(sinks, qa, ka, va, kc, vc)


def _outproj_router_kernel(n_ptiles, xp_ref, xs_ref, hmp_ref, hap_ref, hms_ref, has_ref, wom_ref, woa_ref,
                           g_ref, wrt_ref, br_ref,
                           x1_ref, xg_ref, meta_ref, cnt_ref):
    i = pl.program_id(0)
    is_p = i < n_ptiles
    x = jnp.where(is_p, xp_ref[...], xs_ref[...])
    hm = jnp.where(is_p, hmp_ref[...], hms_ref[...].astype(BF16))
    ha = jnp.where(is_p, hap_ref[...], has_ref[...].astype(BF16))
    x1 = (x + jnp.dot(hm, wom_ref[...], preferred_element_type=F32)
          + jnp.dot(ha, woa_ref[...], preferred_element_type=F32))
    x1_ref[...] = x1
    h2 = _rms(x1, g_ref[...])
    logits = lax.dot_general(wrt_ref[...], h2, (((1,), (1,)), ((), ())), precision=HIGHEST,
                             preferred_element_type=F32) + br_ref[...]
    eidx = lax.broadcasted_iota(jnp.int32, logits.shape, 0).astype(F32)
    work = logits
    vals, hots = [], []
    for _ in range(TOP_K):
        mv = jnp.max(work, axis=0, keepdims=True)
        sel = jnp.min(jnp.where(work == mv, eidx, float(N_EXPERTS)), axis=0, keepdims=True)
        hot = eidx == sel
        vals.append(mv)
        hots.append(hot)
        work = jnp.where(hot, -jnp.inf, work)
    es = [jnp.exp(v - vals[0]) for v in vals]
    tot = es[0] + es[1] + es[2] + es[3]
    gates = [e / tot for e in es]
    hot_all = jnp.where(hots[0] | hots[1] | hots[2] | hots[3], 1.0, 0.0)
    tm = logits.shape[1]
    su = (lax.broadcasted_iota(jnp.int32, (tm, tm), 0) < lax.broadcasted_iota(jnp.int32, (tm, tm), 1))
    cum = jnp.dot(hot_all.astype(BF16), su.astype(BF16), preferred_element_type=F32)
    cnt = jnp.sum(hot_all, axis=1, keepdims=True)
    cpad = (((cnt.astype(jnp.int32) + (SUBLANES - 1)) // SUBLANES) * SUBLANES).astype(F32)
    lower = (lax.broadcasted_iota(jnp.int32, (N_EXPERTS, N_EXPERTS), 0)
             > lax.broadcasted_iota(jnp.int32, (N_EXPERTS, N_EXPERTS), 1)).astype(F32)
    lstart = jnp.dot(lower, jnp.broadcast_to(cpad, (N_EXPERTS, LANES)), precision=HIGHEST,
                     preferred_element_type=F32)[:, 0:1]
    base = lstart + cum
    lpos = [jnp.sum(jnp.where(hot, base, 0.0), axis=0, keepdims=True) for hot in hots]
    lpi = [p.astype(jnp.int32) for p in lpos]
    r_iota = lax.broadcasted_iota(jnp.int32, (GROUP_R, tm), 0)
    sel01 = jnp.where(r_iota == lpi[0], 1.0, jnp.where(r_iota == lpi[1], 1.0, jnp.where(
        r_iota == lpi[2], 1.0, jnp.where(r_iota == lpi[3], 1.0, 0.0)))).astype(BF16)
    xg_ref[...] = jnp.dot(sel01, h2.astype(BF16), preferred_element_type=F32)
    meta_ref[...] = jnp.transpose(jnp.concatenate(gates + lpos, axis=0))
    cnt_ref[0] = jnp.broadcast_to(cnt, (N_EXPERTS, LANES))


def _outproj_router(xp, xs, hmp, hap, hms, has, wom, woa, g_ffn, wrt, br):
    tp, ts = xp.shape[0], xs.shape[0]
    n_pt, n_st = tp // TM, ts // TM
    t_all = tp + ts
    pblk = lambda w: pl.BlockSpec((TM, w), lambda i: (jnp.minimum(i, n_pt - 1), 0))
    sblk = lambda w: pl.BlockSpec((TM, w), lambda i: (jnp.maximum(i - n_pt, 0), 0))
    full = lambda a: pl.BlockSpec(a.shape, lambda i: (0,) * a.ndim)
    return pl.pallas_call(
        functools.partial(_outproj_router_kernel, n_pt),
        grid=(n_pt + n_st,),
        in_specs=[pblk(D_MODEL), sblk(D_MODEL), pblk(DM), pblk(DM), sblk(DM), sblk(DM),
                  full(wom), full(woa), full(g_ffn), full(wrt), full(br)],
        out_specs=(pl.BlockSpec((TM, D_MODEL), lambda i: (i, 0)),
                   pl.BlockSpec((GROUP_R, D_MODEL), lambda i: (i, 0)),
                   pl.BlockSpec((TM, 2 * TOP_K), lambda i: (i, 0)),
                   pl.BlockSpec((1, N_EXPERTS, LANES), lambda i: (i, 0, 0))),
        out_shape=(jax.ShapeDtypeStruct((t_all, D_MODEL), F32),
                   jax.ShapeDtypeStruct(((n_pt + n_st) * GROUP_R, D_MODEL), F32),
                   jax.ShapeDtypeStruct((t_all, 2 * TOP_K), F32),
                   jax.ShapeDtypeStruct((n_pt + n_st, N_EXPERTS, LANES), F32)),
        compiler_params=_cparams(("arbitrary",), VMEM_LIMIT),
        name="outproj_router",
    )(xp, xs, hmp, hap, hms, has, wom, woa, g_ffn, wrt, br)


def _expert_kernel(be_ref, na_ref, slot_ref, nxt_ref, ctab_ref, ctab_next_ref, xg_ref, wgu_ref, bgu_ref,
                   wd_ref, bd_ref, o_ref, xbuf, wgu_f, wd_f, wgu_s, wd_s, xsem, wsem):
    i = pl.program_id(0)
    na = na_ref[0]

    def x_copies(tab_ref, slot):
        return [pltpu.make_async_copy(xg_ref.at[pl.ds(pl.multiple_of(tab_ref[0, 0, c], SUBLANES), SUBLANES)],
                                      xbuf.at[slot, pl.ds(c * SUBLANES, SUBLANES)], xsem.at[slot])
                for c in range(MOE_BM // SUBLANES)]

    def w_copies(e, slot):
        return [pltpu.make_async_copy(wgu_ref.at[e], wgu_f.at[slot], wsem.at[slot]),
                pltpu.make_async_copy(wd_ref.at[e], wd_f.at[slot], wsem.at[slot])]

    @pl.when(i == 0)
    def _():
        for cp in x_copies(ctab_ref, 0) + w_copies(be_ref[0], 0):
            cp.start()

    @pl.when(i < na)
    def _():
        changed = (i == 0) | (be_ref[i] != be_ref[jnp.maximum(i - 1, 0)])
        wslot = slot_ref[i]
        xslot = lax.rem(i, 2)

        @pl.when(changed)
        def _():
            for cp in w_copies(be_ref[i], wslot):
                cp.wait()

            @pl.when(nxt_ref[i] >= 0)
            def _():
                for cp in w_copies(nxt_ref[i], 1 - wslot):
                    cp.start()

            wgu_s[...] = wgu_f[wslot].astype(BF16)
            wd_s[...] = wd_f[wslot].astype(BF16)

        @pl.when(i + 1 < na)
        def _():
            for cp in x_copies(ctab_next_ref, 1 - xslot):
                cp.start()

        for cp in x_copies(ctab_ref, xslot):
            cp.wait()

        x = xbuf[xslot].astype(BF16)
        gu = jnp.dot(x, wgu_s[...], preferred_element_type=F32) + bgu_ref[...]
        gate = jnp.minimum(gu[:, :D_FF], SWIGLU_LIMIT)
        up = jnp.clip(gu[:, D_FF:], -SWIGLU_LIMIT, SWIGLU_LIMIT)
        act = (up + 1.0) * (gate * _sigmoid(gate * SWIGLU_ALPHA))
        o_ref[...] = jnp.dot(act.astype(BF16), wd_s[...], preferred_element_type=F32) + bd_ref[...]


def _expert_ffn(block_exp, n_active, wslot, next_exp, ctab, xg, wgu, bgu, wd, bd):
    n_blocks = ctab.shape[0]
    n_rows = n_blocks * MOE_BM
    nch = MOE_BM // SUBLANES
    grid_spec = pltpu.PrefetchScalarGridSpec(
        num_scalar_prefetch=4,
        grid=(n_blocks,),
        in_specs=[pl.BlockSpec((1, 1, nch), lambda i, *_: (i, 0, 0), memory_space=pltpu.SMEM),
                  pl.BlockSpec((1, 1, nch), lambda i, *_: (jnp.minimum(i + 1, n_blocks - 1), 0, 0),
                               memory_space=pltpu.SMEM),
                  pl.BlockSpec(memory_space=pl.ANY),
                  pl.BlockSpec(memory_space=pl.ANY),
                  pl.BlockSpec((None, 1, 2 * D_FF), lambda i, be, *_: (be[i], 0, 0)),
                  pl.BlockSpec(memory_space=pl.ANY),
                  pl.BlockSpec((None, 1, D_MODEL), lambda i, be, *_: (be[i], 0, 0))],
        out_specs=pl.BlockSpec((MOE_BM, D_MODEL), lambda i, be, na, *_: (jnp.minimum(i, na[0] - 1), 0)),
        scratch_shapes=[pltpu.VMEM((2, MOE_BM, D_MODEL), F32),
                        pltpu.VMEM((2, D_MODEL, 2 * D_FF), F32), pltpu.VMEM((2, D_FF, D_MODEL), F32),
                        pltpu.VMEM((D_MODEL, 2 * D_FF), BF16), pltpu.VMEM((D_FF, D_MODEL), BF16),
                        pltpu.SemaphoreType.DMA((2,)), pltpu.SemaphoreType.DMA((2,))],
    )
    return pl.pallas_call(
        _expert_kernel,
        grid_spec=grid_spec,
        out_shape=jax.ShapeDtypeStruct((n_rows, D_MODEL), F32),
        compiler_params=_cparams(("arbitrary",), VMEM_LIMIT),
        name="moe_experts",
    )(block_exp, n_active, wslot, next_exp, ctab, ctab, xg, wgu, bgu, wd, bd)


def _combine_kernel(n_ptiles, ctab_ref, ctab_next_ref, outs_ref, x1_ref, meta_ref, gf_ref, yp_ref, ys_ref,
                    obuf, sem):
    i = pl.program_id(0)
    n = pl.num_programs(0)
    slot = lax.rem(i, 2)

    def copies(tab_ref, s):
        return [pltpu.make_async_copy(outs_ref.at[pl.ds(pl.multiple_of(tab_ref[0, 0, c], SUBLANES), SUBLANES)],
                                      obuf.at[s, pl.ds(c * SUBLANES, SUBLANES)], sem.at[s])
                for c in range(GROUP_R // SUBLANES)]

    @pl.when(i == 0)
    def _():
        for cp in copies(ctab_ref, 0):
            cp.start()

    @pl.when(i + 1 < n)
    def _():
        for cp in copies(ctab_next_ref, 1 - slot):
            cp.start()

    for cp in copies(ctab_ref, slot):
        cp.wait()

    meta = meta_ref[...]
    tm = meta.shape[0]
    r_iota = lax.broadcasted_iota(jnp.int32, (tm, GROUP_R), 1)
    lp = [meta[:, TOP_K + k:TOP_K + k + 1].astype(jnp.int32) for k in range(TOP_K)]
    gk = [meta[:, k:k + 1] for k in range(TOP_K)]
    gsel = jnp.where(r_iota == lp[0], gk[0], jnp.where(r_iota == lp[1], gk[1], jnp.where(
        r_iota == lp[2], gk[2], jnp.where(r_iota == lp[3], gk[3], 0.0))))
    sel01 = jnp.where(gsel != 0.0, 1.0, 0.0).astype(BF16)
    rg_row = jnp.sum(gsel, axis=0, keepdims=True)
    rg_col = jnp.transpose(jnp.broadcast_to(rg_row, (SUBLANES, GROUP_R)))[:, 0:1]
    og = obuf[slot] * rg_col
    og_hi = og.astype(BF16)
    og_lo = (og - og_hi.astype(F32)).astype(BF16)
    acc = (x1_ref[...] + jnp.dot(sel01, og_hi, preferred_element_type=F32)
           + jnp.dot(sel01, og_lo, preferred_element_type=F32))
    y = _rms(acc, gf_ref[...])

    @pl.when(i < n_ptiles)
    def _():
        yp_ref[...] = y

    @pl.when(i >= n_ptiles)
    def _():
        ys_ref[...] = y


def _combine(ctab, outs, x1, meta, g_final, tp, ts):
    n_pt, n_st = tp // TM, ts // TM
    n = n_pt + n_st
    nch = GROUP_R // SUBLANES
    return pl.pallas_call(
        functools.partial(_combine_kernel, n_pt),
        grid=(n,),
        in_specs=[pl.BlockSpec((1, 1, nch), lambda i: (i, 0, 0), memory_space=pltpu.SMEM),
                  pl.BlockSpec((1, 1, nch), lambda i: (jnp.minimum(i + 1, n - 1), 0, 0), memory_space=pltpu.SMEM),
                  pl.BlockSpec(memory_space=pl.ANY),
                  pl.BlockSpec((TM, D_MODEL), lambda i: (i, 0)),
                  pl.BlockSpec((TM, 2 * TOP_K), lambda i: (i, 0)),
                  pl.BlockSpec((1, D_MODEL), lambda i: (0, 0))],
        out_specs=(pl.BlockSpec((TM, D_MODEL), lambda i: (jnp.minimum(i, n_pt - 1), 0)),
                   pl.BlockSpec((TM, D_MODEL), lambda i: (jnp.maximum(i - n_pt, 0), 0))),
        out_shape=(jax.ShapeDtypeStruct((tp, D_MODEL), F32), jax.ShapeDtypeStruct((ts, D_MODEL), F32)),
        scratch_shapes=[pltpu.VMEM((2, GROUP_R, D_MODEL), F32), pltpu.SemaphoreType.DMA((2,))],
        compiler_params=_cparams(("arbitrary",), VMEM_LIMIT),
        name="moe_combine",
    )(ctab, ctab, outs, x1, meta, g_final)


def kernel(x_prompt, x_sample, cache_swa_k, cache_swa_v, state_mlstm_c, state_mlstm_n, state_mlstm_m,
           g_mix, w_in, b_igate, b_fgate, g_head, attn_sinks, w_out, g_ffn, w_router, b_router,
           w_gate_up, b_gate_up, w_down, b_down, g_final):
    assert w_in.shape[0] == 1, "single-layer problem"
    B, S, _ = x_prompt.shape
    Bd, Tn, _ = x_sample.shape
    tp, ts = B * S, Bd * Tn
    t_all = tp + ts
    xp = x_prompt.reshape(tp, D_MODEL)
    xs = x_sample.reshape(ts, D_MODEL)

    w = w_in[0]
    o = np.cumsum([0, DM, DM, DM, DM, N_HEADS, N_HEADS, DM, DKV, DKV])
    col = lambda a: w[:, int(o[a]):int(o[a + 1])]
    w1 = jnp.concatenate([col(0), col(2), col(3), col(6), col(7), col(8)], axis=1).astype(BF16)
    wkt = col(1).T.astype(BF16)
    wgates = jnp.concatenate([col(4), col(5)], axis=1)
    wg = jnp.pad(wgates, ((0, 0), (0, LANES - 2 * N_HEADS))).astype(BF16)
    wgt = wgates.T.astype(BF16)
    bg = jnp.concatenate([b_igate[0], b_fgate[0]]).astype(F32)
    bcol = jnp.pad(bg, (0, LANES - 2 * N_HEADS)).reshape(1, LANES)
    brow = bg.reshape(2 * N_HEADS, 1)

    qm, vm, om, qa, ka, va, kmt, gcol, grow = _inproj(xp, xs, g_mix[0].reshape(1, D_MODEL), w1, wkt, wg, wgt,
                                                      bcol, brow)

    gh = g_head[0].astype(F32)
    sinks = attn_sinks[0].astype(F32)

    hm_p, cbd, m_p = _mlstm_prompt(qm, kmt, vm, om, gcol, grow, gh.reshape(1, DM), B, S)
    ha_p = _swa_prompt(sinks, qa, ka, va, B, S)

    hd = lambda a: a[tp:].reshape(Bd, Tn, N_HEADS, HEAD_DIM).transpose(0, 2, 1, 3)
    kts = kmt[:, tp:].reshape(N_HEADS, HEAD_DIM, Bd, Tn)
    grow_s = grow[:, tp:].reshape(2 * N_HEADS, Bd, Tn).transpose(1, 0, 2)
    gcol_s = gcol[tp:, :2 * N_HEADS].reshape(Bd, Tn, 2 * N_HEADS)
    hm_s, c_s, n_s, m_s = _mlstm_sample(
        hd(qm), kts.transpose(2, 0, 3, 1), kts.transpose(2, 0, 1, 3), hd(vm), hd(om), grow_s, gcol_s,
        state_mlstm_c[0], state_mlstm_n[0], state_mlstm_m[0], gh)
    hm_s = hm_s.transpose(0, 2, 1, 3).reshape(ts, DM)
    ha_s, k_s, v_s = _swa_decode(sinks, qa, ka, va, cache_swa_k[0].reshape(Bd, WINDOW, DKV),
                                 cache_swa_v[0].reshape(Bd, WINDOW, DKV), tp, Tn)

    wo = w_out[0].astype(BF16)
    x1, xg, meta, cnt = _outproj_router(
        xp, xs, hm_p, ha_p, hm_s, ha_s, wo[:DM], wo[DM:], g_ffn[0].reshape(1, D_MODEL),
        w_router[0].T, b_router[0].reshape(N_EXPERTS, 1))

    i32 = jnp.int32
    n_tiles = t_all // TM
    max_rows = t_all * TOP_K + n_tiles * N_EXPERTS * (SUBLANES - 1) + N_EXPERTS * (MOE_BM - 1)
    n_blocks = -(-max_rows // MOE_BM)
    cpad = (cnt[:, :, 0].astype(i32) + (SUBLANES - 1)) // SUBLANES * SUBLANES
    lstart = jnp.cumsum(cpad, axis=1) - cpad
    goff = jnp.cumsum(cpad, axis=0) - cpad
    padded = (jnp.sum(cpad, axis=0) + MOE_BM - 1) // MOE_BM * MOE_BM
    pad_end = jnp.cumsum(padded)
    seg_begin = (pad_end - padded)[None, :] + goff
    n_active = (pad_end[-1] // MOE_BM).astype(i32)
    blk = jnp.minimum(jnp.arange(n_blocks, dtype=i32), n_active - 1)
    block_exp = jnp.minimum(jnp.sum((pad_end[None, :] <= (blk * MOE_BM)[:, None]).astype(i32), axis=1),
                            N_EXPERTS - 1)
    e_ids = jnp.arange(N_EXPERTS, dtype=i32)
    nonempty = padded > 0
    nxt_e = jnp.min(jnp.where((e_ids[None, :] > e_ids[:, None]) & nonempty[None, :], e_ids[None, :], N_EXPERTS),
                    axis=1)
    nxt_e = jnp.where(nxt_e == N_EXPERTS, -1, nxt_e)
    ord_e = jnp.cumsum(nonempty.astype(i32)) - 1
    be_hot = block_exp[:, None] == e_ids[None, :]
    next_exp = jnp.sum(jnp.where(be_hot, nxt_e[None, :], 0), axis=1).astype(i32)
    wslot = (jnp.sum(jnp.where(be_hot, ord_e[None, :], 0), axis=1) % 2).astype(i32)

    seg_src = jnp.arange(n_tiles, dtype=i32)[:, None] * GROUP_R + lstart
    sb, sl, ss = seg_begin.reshape(-1), cpad.reshape(-1), seg_src.reshape(-1)
    rc = jnp.arange(n_blocks * MOE_BM // SUBLANES, dtype=i32)[:, None] * SUBLANES
    inseg = (sb[None, :] <= rc) & (rc < (sb + sl)[None, :])
    ctab_e = jnp.where(jnp.any(inseg, axis=1), jnp.sum(jnp.where(inseg, (ss - sb)[None, :] + rc, 0), axis=1),
                       GROUP_R - SUBLANES)
    lr = jnp.arange(GROUP_R // SUBLANES, dtype=i32)[None, :, None] * SUBLANES
    inl = (lstart[:, None, :] <= lr) & (lr < (lstart + cpad)[:, None, :])
    ctab_c = jnp.sum(jnp.where(inl, (seg_begin - lstart)[:, None, :] + lr, 0), axis=2)

    outs = _expert_ffn(block_exp, n_active.reshape(1), wslot, next_exp,
                       ctab_e.astype(i32).reshape(n_blocks, 1, MOE_BM // SUBLANES), xg, w_gate_up[0],
                       b_gate_up[0].reshape(N_EXPERTS, 1, 2 * D_FF), w_down[0],
                       b_down[0].reshape(N_EXPERTS, 1, D_MODEL))
    y_p, y_s = _combine(ctab_c.astype(i32).reshape(n_tiles, 1, GROUP_R // SUBLANES), outs, x1, meta,
                        g_final.reshape(1, D_MODEL), tp, ts)

    kv_tail = lambda a: a[:tp].reshape(B, S, N_KV, HEAD_DIM)[:, S - WINDOW:][None]
    c_e = cbd[:, :, :HEAD_DIM, :HEAD_DIM]
    c_o = cbd[:, :, HEAD_DIM:, HEAD_DIM:LANES]
    c_p = jnp.stack([c_e, c_o], axis=2).reshape(B, N_HEADS, HEAD_DIM, HEAD_DIM)
    n_p = jnp.stack([cbd[:, :, :HEAD_DIM, LANES], cbd[:, :, HEAD_DIM:, LANES + 1]], axis=2).reshape(B, N_HEADS, HEAD_DIM)
    return (y_p.reshape(B, S, D_MODEL), y_s.reshape(Bd, Tn, D_MODEL),
            kv_tail(ka), kv_tail(va), c_p[None], n_p[None], m_p[:, :, 0][None],
            k_s.reshape(Bd, WINDOW, N_KV, HEAD_DIM)[None], v_s.reshape(Bd, WINDOW, N_KV, HEAD_DIM)[None],
            c_s[None], n_s[None], m_s[None])
```

```python
import functools
import math

import jax
import jax.numpy as jnp
import numpy as np
from jax import lax
from jax.experimental import pallas as pl
from jax.experimental.pallas import tpu as pltpu

F32 = jnp.float32
BF16 = jnp.bfloat16
HIGHEST = lax.Precision.HIGHEST

D_MODEL = 1024
HEAD_DIM = 64
N_HEADS = 8
N_PAIRS = N_HEADS // 2
N_KV = 2
GROUP = N_HEADS // N_KV
WINDOW = 128
N_EXPERTS = 32
TOP_K = 4
D_FF = 1024
SWIGLU_LIMIT = 7.0
SWIGLU_ALPHA = 1.702
RMS_EPS = 1e-5
DM = N_HEADS * HEAD_DIM
DKV = N_KV * HEAD_DIM
NEG = -1e30

LANES = 128
SUBLANES = 8
VMEM_LIMIT = 56 * 1024 * 1024

TM = 512
MLSTM_TL = 512
MLSTM_L = 128
DEC_G = 8
DEC_UNROLL = 4
MOE_BM = 256
GROUP_R = -(-(TOP_K * TM + N_EXPERTS * (SUBLANES - 1) + SUBLANES) // LANES) * LANES


def _cparams(sem, vmem=None):
    return pltpu.CompilerParams(dimension_semantics=sem, vmem_limit_bytes=vmem)


def _rms(x, g):
    return x * lax.rsqrt(jnp.mean(x * x, axis=-1, keepdims=True) + RMS_EPS) * g


def _log_sigmoid(z):
    return jnp.minimum(z, 0.0) - jnp.log(1.0 + jnp.exp(-jnp.abs(z)))


def _sigmoid(z):
    return 1.0 / (1.0 + jnp.exp(-z))


def _inproj_kernel(n_ptiles, xp_ref, xs_ref, g_ref, w1_ref, wkt_ref, wg_ref, wgt_ref, bcol_ref, brow_ref,
                   qm_ref, vm_ref, om_ref, qa_ref, ka_ref, va_ref, kmt_ref, gcol_ref, grow_ref):
    i = pl.program_id(0)
    x = jnp.where(i < n_ptiles, xp_ref[...], xs_ref[...])
    h = _rms(x, g_ref[...]).astype(BF16)
    main = jnp.dot(h, w1_ref[...], preferred_element_type=F32)
    qm_ref[...] = main[:, 0:DM]
    vm_ref[...] = main[:, DM:2 * DM]
    om_ref[...] = main[:, 2 * DM:3 * DM]
    qa_ref[...] = main[:, 3 * DM:4 * DM]
    ka_ref[...] = main[:, 4 * DM:4 * DM + DKV]
    va_ref[...] = main[:, 4 * DM + DKV:4 * DM + 2 * DKV]
    kt = lax.dot_general(wkt_ref[...], h, (((1,), (1,)), ((), ())), preferred_element_type=F32)
    kmt_ref[...] = kt * (HEAD_DIM ** -0.5)
    zc = jnp.dot(h, wg_ref[...], preferred_element_type=F32) + bcol_ref[...]
    lane = lax.broadcasted_iota(jnp.int32, zc.shape, 1)
    gcol_ref[...] = jnp.where(lane < N_HEADS, zc, _log_sigmoid(zc))
    zr = lax.dot_general(wgt_ref[...], h, (((1,), (1,)), ((), ())), preferred_element_type=F32) + brow_ref[...]
    row = lax.broadcasted_iota(jnp.int32, zr.shape, 0)
    grow_ref[...] = jnp.where(row < N_HEADS, zr, _log_sigmoid(zr))


def _inproj(xp, xs, g_mix, w1, wkt, wg, wgt, bcol, brow):
    tp, ts = xp.shape[0], xs.shape[0]
    n_pt, n_st = tp // TM, ts // TM
    t_all = tp + ts
    n1 = w1.shape[1]
    tok = lambda w: pl.BlockSpec((TM, w), lambda i: (i, 0))
    full = lambda a: pl.BlockSpec(a.shape, lambda i: (0,) * a.ndim)
    out_shape = (
        jax.ShapeDtypeStruct((t_all, DM), F32), jax.ShapeDtypeStruct((t_all, DM), F32),
        jax.ShapeDtypeStruct((t_all, DM), F32), jax.ShapeDtypeStruct((t_all, DM), F32),
        jax.ShapeDtypeStruct((t_all, DKV), F32), jax.ShapeDtypeStruct((t_all, DKV), F32),
        jax.ShapeDtypeStruct((DM, t_all), F32),
        jax.ShapeDtypeStruct((t_all, LANES), F32), jax.ShapeDtypeStruct((2 * N_HEADS, t_all), F32),
    )
    return pl.pallas_call(
        functools.partial(_inproj_kernel, n_pt),
        grid=(n_pt + n_st,),
        in_specs=[
            pl.BlockSpec((TM, D_MODEL), lambda i: (jnp.minimum(i, n_pt - 1), 0)),
            pl.BlockSpec((TM, D_MODEL), lambda i: (jnp.maximum(i - n_pt, 0), 0)),
            full(g_mix), full(w1), full(wkt), full(wg), full(wgt), full(bcol), full(brow),
        ],
        out_specs=(tok(DM), tok(DM), tok(DM), tok(DM), tok(DKV), tok(DKV),
                   pl.BlockSpec((DM, TM), lambda i: (0, i)),
                   tok(LANES), pl.BlockSpec((2 * N_HEADS, TM), lambda i: (0, i))),
        out_shape=out_shape,
        compiler_params=_cparams(("arbitrary",), VMEM_LIMIT),
        name="inproj",
    )(xp, xs, g_mix, w1, wkt, wg, wgt, bcol, brow)


def _cumsum_rows(x, n):
    row = lax.broadcasted_iota(jnp.int32, x.shape, 0)
    sh = 1
    while sh < n:
        x = x + jnp.where(row >= sh, pltpu.roll(x, sh, axis=0), 0.0)
        sh *= 2
    return x


def _mlstm_prompt_kernel(qm_ref, kmt_ref, vm_ref, om_ref, gcol_ref, grow_ref, gh_ref,
                         hm_ref, cbd_ref, m_ref, cbd_s, m_s):
    j = pl.program_id(1)
    L = MLSTM_L

    @pl.when(j == 0)
    def _():
        cbd_s[...] = jnp.zeros_like(cbd_s)
        m_s[...] = jnp.zeros_like(m_s)

    ti = lax.broadcasted_iota(jnp.int32, (L, L), 0)
    si = lax.broadcasted_iota(jnp.int32, (L, L), 1)
    causal = ti >= si
    upper = (ti <= si).astype(F32)
    lane128 = lax.broadcasted_iota(jnp.int32, (L, LANES), 1)
    even128 = lane128 < HEAD_DIM
    lane256 = lax.broadcasted_iota(jnp.int32, (1, 2 * LANES), 1)
    cols_e = (lane256 < HEAD_DIM) | (lane256 == LANES)
    cols_o = ((lane256 >= HEAD_DIM) & (lane256 < LANES)) | (lane256 == LANES + 1)
    rowk = lax.broadcasted_iota(jnp.int32, (LANES, 1), 0)
    rows_e = rowk < HEAD_DIM
    bd_mask = (rows_e & cols_e) | ((~rows_e) & cols_o)
    ones_cols = ((lax.broadcasted_iota(jnp.int32, (L, LANES), 1) < 2)).astype(F32)
    bo_r = lax.broadcasted_iota(jnp.int32, (LANES, LANES), 0) // HEAD_DIM
    bo_c = lax.broadcasted_iota(jnp.int32, (LANES, LANES), 1) // HEAD_DIM
    block_ones = (bo_r == bo_c).astype(F32)

    for c in range(MLSTM_TL // L):
        sl = slice(c * L, (c + 1) * L)
        grow = grow_ref[:, sl]
        i_row = grow[0:N_HEADS]
        b_row = jnp.dot(grow[N_HEADS:2 * N_HEADS], upper, precision=HIGHEST,
                        preferred_element_type=F32)
        b_colall = _cumsum_rows(gcol_ref[sl, :], L)
        rowv_all = i_row - b_row
        for p in range(N_PAIRS):
            ls = slice(p * LANES, (p + 1) * LANES)
            q2 = qm_ref[sl, ls].astype(BF16)
            kt2 = kmt_ref[ls, sl]
            v2 = vm_ref[sl, ls]
            vext = jnp.concatenate([v2, ones_cols], axis=1)
            kt_e = jnp.where(rows_e, kt2, 0.0).astype(BF16)
            kt_o = jnp.where(rows_e, 0.0, kt2).astype(BF16)
            s2 = jnp.dot(q2, jnp.concatenate([kt_e, kt_o], axis=1), preferred_element_type=F32)
            cbd = cbd_s[p]
            rq = jnp.dot(q2, cbd.astype(BF16), preferred_element_type=F32)
            ps, mts, inters, wrows, decays, mnews = [], [], [], [], [], []
            for hh in range(2):
                h = 2 * p + hh
                bcol = b_colall[:, N_HEADS + h:N_HEADS + h + 1]
                logd = jnp.where(causal, bcol + rowv_all[h:h + 1, :], NEG)
                m_prev = m_s[h:h + 1, 0:1]
                m_inter = m_prev + bcol
                m_t = jnp.maximum(m_inter, jnp.max(logd, axis=1, keepdims=True))
                d = jnp.exp(logd - m_t)
                ps.append((s2[:, hh * L:(hh + 1) * L] * d).astype(BF16))
                mts.append(m_t)
                inters.append(jnp.exp(m_inter - m_t))
                m_new = m_t[L - 1:L, :]
                b_last = b_row[h:h + 1, L - 1:L]
                decays.append(jnp.exp(m_prev + b_last - m_new))
                wrows.append(jnp.exp(b_last - b_row[h:h + 1, :] + i_row[h:h + 1, :] - m_new))
                mnews.append(m_new)
            vstack = jnp.concatenate([jnp.where(cols_e, vext, 0.0), jnp.where(cols_o, vext, 0.0)],
                                     axis=0).astype(BF16)
            r = jnp.dot(jnp.concatenate(ps, axis=1), vstack, preferred_element_type=F32)
            nd = r + jnp.where(cols_e, inters[0], inters[1]) * rq
            num = nd[:, 0:LANES]
            den = jnp.where(even128, nd[:, LANES:LANES + 1], nd[:, LANES + 1:LANES + 2])
            mt2 = jnp.where(even128, mts[0], mts[1])
            hv = num / jnp.maximum(jnp.abs(den), jnp.exp(-mt2))
            ms = jnp.dot(hv * hv, block_ones, precision=HIGHEST, preferred_element_type=F32) * (1.0 / HEAD_DIM)
            y = hv * lax.rsqrt(ms + RMS_EPS) * gh_ref[:, ls] * _sigmoid(om_ref[sl, ls])
            hm_ref[sl, ls] = y.astype(hm_ref.dtype)
            w2 = jnp.where(rows_e, wrows[0], wrows[1])
            upd = jnp.dot((kt2 * w2).astype(BF16), vext.astype(BF16), preferred_element_type=F32)
            dec2 = jnp.where(rows_e, decays[0], decays[1])
            cbd_s[p] = dec2 * cbd + jnp.where(bd_mask, upd, 0.0)
            for hh in range(2):
                h = 2 * p + hh
                m_s[h:h + 1, :] = jnp.broadcast_to(mnews[hh], (1, LANES))

    @pl.when(j == pl.num_programs(1) - 1)
    def _():
        cbd_ref[0] = cbd_s[...]
        m_ref[0] = m_s[...]


def _mlstm_prompt(qm, kmt, vm, om, gcol, grow, gh, batch, seq):
    nt = seq // MLSTM_TL
    tokb = lambda w: pl.BlockSpec((MLSTM_TL, w), lambda b, j: (b * nt + j, 0))
    rowb = lambda r: pl.BlockSpec((r, MLSTM_TL), lambda b, j: (0, b * nt + j))
    return pl.pallas_call(
        _mlstm_prompt_kernel,
        grid=(batch, nt),
        in_specs=[tokb(DM), rowb(DM), tokb(DM), tokb(DM), tokb(LANES), rowb(2 * N_HEADS),
                  pl.BlockSpec((1, DM), lambda b, j: (0, 0))],
        out_specs=(tokb(DM),
                   pl.BlockSpec((1, N_PAIRS, LANES, 2 * LANES), lambda b, j: (b, 0, 0, 0)),
                   pl.BlockSpec((1, N_HEADS, LANES), lambda b, j: (b, 0, 0))),
        out_shape=(jax.ShapeDtypeStruct((batch * seq, DM), BF16),
                   jax.ShapeDtypeStruct((batch, N_PAIRS, LANES, 2 * LANES), F32),
                   jax.ShapeDtypeStruct((batch, N_HEADS, LANES), F32)),
        scratch_shapes=[pltpu.VMEM((N_PAIRS, LANES, 2 * LANES), F32), pltpu.VMEM((N_HEADS, LANES), F32)],
        compiler_params=_cparams(("arbitrary", "arbitrary"), VMEM_LIMIT),
        name="mlstm_prompt",
    )(qm, kmt, vm, om, gcol, grow, gh)


def _mlstm_sample_kernel(n_tok, qm_ref, kmt_ref, vm_ref, om_ref, gcol_ref, grow_ref, c0_ref, n0_ref, m0_ref,
                         m0t_ref, gh_ref, hm_ref, c_ref, n_ref, mt_ref):
    L = LANES
    NB = L // n_tok
    ti = lax.broadcasted_iota(jnp.int32, (L, L), 0)
    si = lax.broadcasted_iota(jnp.int32, (L, L), 1)
    same = (ti // n_tok) == (si // n_tok)
    causal = same & (ti >= si)
    useg = jnp.where(same & (ti <= si), 1.0, 0.0)
    slast = jnp.where(same & (ti % n_tok == n_tok - 1), 1.0, 0.0)
    expand = jnp.where(ti // n_tok == si, 1.0, 0.0)
    expand_t = jnp.where(ti == si // n_tok, 1.0, 0.0)
    pick = jnp.where((ti // n_tok == si) & (ti % n_tok == n_tok - 1), 1.0, 0.0)
    hdot = lambda a, b: jnp.dot(a, b, precision=HIGHEST, preferred_element_type=F32)

    lane128 = lax.broadcasted_iota(jnp.int32, (L, LANES), 1)
    even128 = lane128 < HEAD_DIM
    lane256 = lax.broadcasted_iota(jnp.int32, (1, 2 * LANES), 1)
    cols_e = (lane256 < HEAD_DIM) | (lane256 == LANES)
    cols_o = ((lane256 >= HEAD_DIM) & (lane256 < LANES)) | (lane256 == LANES + 1)
    rows_e = lax.broadcasted_iota(jnp.int32, (LANES, 1), 0) < HEAD_DIM
    ones_cols = jnp.where(lane128 < 2, 1.0, 0.0)
    bo_r = lax.broadcasted_iota(jnp.int32, (LANES, LANES), 0) // HEAD_DIM
    bo_c = lax.broadcasted_iota(jnp.int32, (LANES, LANES), 1) // HEAD_DIM
    block_ones = jnp.where(bo_r == bo_c, 1.0, 0.0)
    W = NB * LANES
    rb = lax.broadcasted_iota(jnp.int32, (L, W), 0)
    cb = lax.broadcasted_iota(jnp.int32, (L, W), 1)
    own_block = (rb // n_tok) == (cb // LANES)
    bd_tiled = (rb // HEAD_DIM) == ((cb % LANES) // HEAD_DIM)

    grow = grow_ref[...]
    i_row = grow[0:N_HEADS]
    b_row = hdot(grow[N_HEADS:2 * N_HEADS], useg)
    b_last = hdot(b_row, slast)
    a_row = b_last - b_row + i_row
    pos = lax.broadcasted_iota(jnp.int32, a_row.shape, 1) % n_tok
    pm = a_row
    sh = 1
    while sh < n_tok:
        pm = jnp.where(pos >= sh, jnp.maximum(pm, pltpu.roll(pm, sh, axis=1)), pm)
        sh *= 2
    m_carry = hdot(jnp.concatenate([m0t_ref[0], b_row], axis=1), jnp.concatenate([expand_t, slast], axis=0))
    m_new_row = jnp.maximum(m_carry, hdot(pm, slast))
    decay_row = jnp.exp(m_carry - m_new_row)
    w_row = jnp.exp(a_row - m_new_row)
    mt_ref[0] = hdot(m_new_row, pick)
    decay_bh = hdot(decay_row, pick)
    decay_hb = jnp.transpose(decay_bh)[0:NB]

    bc_all = gcol_ref[...]
    rowpos = lax.broadcasted_iota(jnp.int32, bc_all.shape, 0) % n_tok
    sh = 1
    while sh < n_tok:
        bc_all = bc_all + jnp.where(rowpos >= sh, pltpu.roll(bc_all, sh, axis=0), 0.0)
        sh *= 2
    pad_rows = lambda a: jnp.concatenate([a, jnp.zeros((L - NB, a.shape[1]), F32)], axis=0)
    m0_col = hdot(expand, pad_rows(m0_ref[...]))
    rowv_all = i_row - b_row

    for p in range(N_PAIRS):
        ls = slice(p * LANES, (p + 1) * LANES)
        q2f = qm_ref[:, ls]
        q2 = q2f.astype(BF16)
        kt2 = kmt_ref[ls, :]
        v2 = vm_ref[:, ls]
        vext = jnp.concatenate([v2, ones_cols], axis=1)
        kt_e = jnp.where(rows_e, kt2, 0.0).astype(BF16)
        kt_o = jnp.where(rows_e, 0.0, kt2).astype(BF16)
        s2 = jnp.dot(q2, jnp.concatenate([kt_e, kt_o], axis=1), preferred_element_type=F32)
        ps, mts, inters = [], [], []
        for hh in range(2):
            h = 2 * p + hh
            bcol = bc_all[:, N_HEADS + h:N_HEADS + h + 1]
            logd = jnp.where(causal, bcol + rowv_all[h:h + 1, :], NEG)
            m_inter = m0_col[:, h:h + 1] + bcol
            m_t = jnp.maximum(m_inter, jnp.max(logd, axis=1, keepdims=True))
            ps.append((s2[:, hh * L:(hh + 1) * L] * jnp.exp(logd - m_t)).astype(BF16))
            mts.append(m_t)
            inters.append(jnp.exp(m_inter - m_t))
        vstack = jnp.concatenate([jnp.where(cols_e, vext, 0.0), jnp.where(cols_o, vext, 0.0)],
                                 axis=0).astype(BF16)
        r = jnp.dot(jnp.concatenate(ps, axis=1), vstack, preferred_element_type=F32)
        zero = jnp.zeros((HEAD_DIM, HEAD_DIM), F32)
        cstack = jnp.concatenate(
            [jnp.concatenate([jnp.concatenate([c0_ref[b, 2 * p], zero], axis=1),
                              jnp.concatenate([zero, c0_ref[b, 2 * p + 1]], axis=1)], axis=0)
             for b in range(NB)], axis=1)
        rq_all = jnp.where(own_block, jnp.dot(q2, cstack.astype(BF16), preferred_element_type=F32), 0.0)
        rq = rq_all[:, 0:LANES]
        for b in range(1, NB):
            rq = rq + rq_all[:, b * LANES:(b + 1) * LANES]
        n_rows = hdot(expand, pad_rows(n0_ref[:, ls]))
        qn = hdot(q2f * n_rows, block_ones)
        inter2 = jnp.where(even128, inters[0], inters[1])
        num = r[:, 0:LANES] + inter2 * rq
        den = jnp.where(even128, r[:, LANES:LANES + 1], r[:, LANES + 1:LANES + 2]) + inter2 * qn
        mt2 = jnp.where(even128, mts[0], mts[1])
        hv = num / jnp.maximum(jnp.abs(den), jnp.exp(-mt2))
        ms = hdot(hv * hv, block_ones) * (1.0 / HEAD_DIM)
        y = hv * lax.rsqrt(ms + RMS_EPS) * gh_ref[:, ls] * _sigmoid(om_ref[:, ls])
        hm_ref[:, ls] = y.astype(hm_ref.dtype)
        kw = kt2 * jnp.where(rows_e, w_row[2 * p:2 * p + 1, :], w_row[2 * p + 1:2 * p + 2, :])
        vbd = jnp.where(own_block, jnp.concatenate([v2] * NB, axis=1), 0.0).astype(BF16)
        upd = jnp.dot(kw.astype(BF16), vbd, preferred_element_type=F32)
        upd = jnp.where(bd_tiled, upd, 0.0)
        for b in range(NB):
            bs = slice(b * LANES, (b + 1) * LANES)
            dec_b = jnp.where(rows_e, decay_hb[b:b + 1, 2 * p:2 * p + 1], decay_hb[b:b + 1, 2 * p + 1:2 * p + 2])
            cnew = dec_b * cstack[:, bs] + upd[:, bs]
            c_ref[b, 2 * p] = cnew[0:HEAD_DIM, 0:HEAD_DIM]
            c_ref[b, 2 * p + 1] = cnew[HEAD_DIM:LANES, HEAD_DIM:LANES]
        nsum = jnp.transpose(hdot(kw, expand))[0:NB]
        dec_n = jnp.where(lax.broadcasted_iota(jnp.int32, (NB, LANES), 1) < HEAD_DIM,
                          decay_hb[:, 2 * p:2 * p + 1], decay_hb[:, 2 * p + 1:2 * p + 2])
        n_ref[:, ls] = dec_n * n0_ref[:, ls] + nsum


def _mlstm_sample(qm, kmt, vm, om, gcol, grow, c0, n0, m0, gh, row0, n_tok):
    nb = c0.shape[0]
    g_nb = LANES // n_tok
    n_g = nb // g_nb
    blk0 = row0 // LANES
    tokb = lambda w: pl.BlockSpec((LANES, w), lambda i: (blk0 + i, 0))
    rowb = lambda r: pl.BlockSpec((r, LANES), lambda i: (0, blk0 + i))
    m0t = jnp.pad(m0.reshape(n_g, g_nb, N_HEADS).transpose(0, 2, 1),
                  ((0, 0), (0, 0), (0, LANES - g_nb)))
    return pl.pallas_call(
        functools.partial(_mlstm_sample_kernel, n_tok),
        grid=(n_g,),
        in_specs=[tokb(DM), rowb(DM), tokb(DM), tokb(DM), tokb(LANES), rowb(2 * N_HEADS),
                  pl.BlockSpec((g_nb, N_HEADS, HEAD_DIM, HEAD_DIM), lambda i: (i, 0, 0, 0)),
                  pl.BlockSpec((g_nb, DM), lambda i: (i, 0)),
                  pl.BlockSpec((g_nb, N_HEADS), lambda i: (i, 0)),
                  pl.BlockSpec((1, N_HEADS, LANES), lambda i: (i, 0, 0)),
                  pl.BlockSpec((1, DM), lambda i: (0, 0))],
        out_specs=(pl.BlockSpec((LANES, DM), lambda i: (i, 0)),
                   pl.BlockSpec((g_nb, N_HEADS, HEAD_DIM, HEAD_DIM), lambda i: (i, 0, 0, 0)),
                   pl.BlockSpec((g_nb, DM), lambda i: (i, 0)),
                   pl.BlockSpec((1, N_HEADS, LANES), lambda i: (i, 0, 0))),
        out_shape=(jax.ShapeDtypeStruct((nb * n_tok, DM), BF16), jax.ShapeDtypeStruct(c0.shape, F32),
                   jax.ShapeDtypeStruct((nb, DM), F32), jax.ShapeDtypeStruct((n_g, N_HEADS, LANES), F32)),
        compiler_params=_cparams(("arbitrary",), VMEM_LIMIT),
        name="mlstm_sample",
    )(qm, kmt, vm, om, gcol, grow, c0, n0.reshape(nb, DM), m0, m0t, gh)


def _alibi_slope(h):
    return float(np.float32(2.0 ** (-8.0 * (h + 1) / N_HEADS)))


def _dup_halves(x):
    lane = lax.broadcasted_iota(jnp.int32, x.shape, 1)
    xr = pltpu.roll(x, HEAD_DIM, axis=1)
    lo = lane < HEAD_DIM
    return jnp.where(lo, x, xr), jnp.where(lo, xr, x)


def _stack_group_queries(q, g):
    lane = lax.broadcasted_iota(jnp.int32, (q.shape[0], LANES), 1)
    parts = []
    for hh in range(GROUP):
        h = GROUP * g + hh
        blk = q[:, (h // 2) * LANES:(h // 2 + 1) * LANES]
        keep = (lane < HEAD_DIM) if h % 2 == 0 else (lane >= HEAD_DIM)
        parts.append(jnp.where(keep, blk, 0.0))
    return jnp.concatenate(parts, axis=0).astype(BF16)


def _swa_prompt_kernel(sink_ref, q_ref, kp_ref, ko_ref, vp_ref, vo_ref, o_ref):
    j = pl.program_id(1)
    R = WINDOW
    q = q_ref[...]
    kd = _dup_halves(jnp.concatenate([kp_ref[...], ko_ref[...]], axis=0))
    vd = _dup_halves(jnp.concatenate([vp_ref[...], vo_ref[...]], axis=0))
    qi = lax.broadcasted_iota(jnp.int32, (R, 2 * R), 0)
    kj = lax.broadcasted_iota(jnp.int32, (R, 2 * R), 1)
    dist = qi + R - kj
    valid = (dist >= 0) & (dist < WINDOW) & ((kj >= R) | (j > 0))
    distf = dist.astype(F32)
    lane = lax.broadcasted_iota(jnp.int32, (R, LANES), 1)
    outs = []
    for g in range(N_KV):
        qs = _stack_group_queries(q, g)
        s = lax.dot_general(qs, kd[g].astype(BF16), (((1,), (1,)), ((), ())),
                            preferred_element_type=F32) * (HEAD_DIM ** -0.5)
        ps = []
        for hh in range(GROUP):
            h = GROUP * g + hh
            sink = sink_ref[h]
            sh = jnp.where(valid, s[hh * R:(hh + 1) * R] - _alibi_slope(h) * distf, NEG)
            mx = jnp.maximum(jnp.max(sh, axis=1, keepdims=True), sink)
            p = jnp.exp(sh - mx)
            p = p / (jnp.sum(p, axis=1, keepdims=True) + jnp.exp(sink - mx))
            ps.append(p.astype(BF16))
        o = jnp.dot(jnp.concatenate(ps, axis=0), vd[g].astype(BF16), preferred_element_type=F32)
        for pp in range(GROUP // 2):
            outs.append(jnp.where(lane < HEAD_DIM, o[(2 * pp) * R:(2 * pp + 1) * R],
                                  o[(2 * pp + 1) * R:(2 * pp + 2) * R]))
    o_ref[...] = jnp.concatenate(outs, axis=1).astype(o_ref.dtype)


def _swa_prompt(sinks, qa, ka, va, batch, seq):
    nb = seq // WINDOW
    own = lambda w: pl.BlockSpec((WINDOW, w), lambda b, j: (b * nb + j, 0))
    prev = lambda w: pl.BlockSpec((WINDOW, w), lambda b, j: (b * nb + jnp.maximum(j - 1, 0), 0))
    return pl.pallas_call(
        _swa_prompt_kernel,
        grid=(batch, nb),
        in_specs=[pl.BlockSpec(memory_space=pltpu.SMEM), own(DM), prev(DKV), own(DKV), prev(DKV), own(DKV)],
        out_specs=own(DM),
        out_shape=jax.ShapeDtypeStruct((batch * seq, DM), BF16),
        compiler_params=_cparams(("arbitrary", "arbitrary"), VMEM_LIMIT),
        name="swa_prompt",
    )(sinks, qa, ka, ka, va, va)


def _swa_decode_kernel(n_tok, sink_ref, q_ref, kn_ref, vn_ref, kc_ref, vc_ref, o_ref, ko_ref, vo_ref):
    W = WINDOW
    Tn = n_tok
    qt = lax.broadcasted_iota(jnp.int32, (Tn, W), 0)
    kj = lax.broadcasted_iota(jnp.int32, (Tn, W), 1)
    dist_c = qt + W - kj
    valid_c = dist_c < WINDOW
    qt2 = lax.broadcasted_iota(jnp.int32, (Tn, Tn), 0)
    kj2 = lax.broadcasted_iota(jnp.int32, (Tn, Tn), 1)
    dist_n = qt2 - kj2
    valid_n = dist_n >= 0
    dcf = dist_c.astype(F32)
    dnf = dist_n.astype(F32)
    lane = lax.broadcasted_iota(jnp.int32, (Tn, LANES), 1)

    def body(b, carry):
        rows = pl.ds(pl.multiple_of(b * Tn, Tn), Tn)
        q = q_ref[rows, :]
        kn = kn_ref[rows, :]
        vn = vn_ref[rows, :]
        kc = kc_ref[b]
        vc = vc_ref[b]
        ko_ref[b, 0:W - Tn, :] = kc[Tn:W]
        ko_ref[b, W - Tn:W, :] = kn
        vo_ref[b, 0:W - Tn, :] = vc[Tn:W]
        vo_ref[b, W - Tn:W, :] = vn
        kcd, knd = _dup_halves(kc), _dup_halves(kn)
        vcd, vnd = _dup_halves(vc), _dup_halves(vn)
        outs = []
        for g in range(N_KV):
            qs = _stack_group_queries(q, g)
            nt = (((1,), (1,)), ((), ()))
            sc = lax.dot_general(qs, kcd[g].astype(BF16), nt, preferred_element_type=F32) * (HEAD_DIM ** -0.5)
            sn = lax.dot_general(qs, knd[g].astype(BF16), nt, preferred_element_type=F32) * (HEAD_DIM ** -0.5)
            pcs, pns = [], []
            for hh in range(GROUP):
                h = GROUP * g + hh
                sink = sink_ref[h]
                slope = _alibi_slope(h)
                shc = jnp.where(valid_c, sc[hh * Tn:(hh + 1) * Tn] - slope * dcf, NEG)
                shn = jnp.where(valid_n, sn[hh * Tn:(hh + 1) * Tn] - slope * dnf, NEG)
                mx = jnp.maximum(jnp.maximum(jnp.max(shc, axis=1, keepdims=True),
                                             jnp.max(shn, axis=1, keepdims=True)), sink)
                pc = jnp.exp(shc - mx)
                pn = jnp.exp(shn - mx)
                inv = 1.0 / (jnp.sum(pc, axis=1, keepdims=True) + jnp.sum(pn, axis=1, keepdims=True)
                             + jnp.exp(sink - mx))
                pcs.append((pc * inv).astype(BF16))
                pns.append((pn * inv).astype(BF16))
            o = (jnp.dot(jnp.concatenate(pcs, axis=0), vcd[g].astype(BF16), preferred_element_type=F32)
                 + jnp.dot(jnp.concatenate(pns, axis=0), vnd[g].astype(BF16), preferred_element_type=F32))
            for pp in range(GROUP // 2):
                outs.append(jnp.where(lane < HEAD_DIM, o[(2 * pp) * Tn:(2 * pp + 1) * Tn],
                                      o[(2 * pp + 1) * Tn:(2 * pp + 2) * Tn]))
        o_ref[rows, :] = jnp.concatenate(outs, axis=1)
        return carry

    lax.fori_loop(0, DEC_G, body, 0, unroll=DEC_UNROLL)


def _swa_decode(sinks, qa, ka, va, kc, vc, row0, n_tok):
    nb = kc.shape[0]
    blk0 = row0 // (DEC_G * n_tok)
    tokb = lambda w: pl.BlockSpec((DEC_G * n_tok, w), lambda i: (blk0 + i, 0))
    cache = pl.BlockSpec((DEC_G, WINDOW, DKV), lambda i: (i, 0, 0))
    return pl.pallas_call(
        functools.partial(_swa_decode_kernel, n_tok),
        grid=(nb // DEC_G,),
        in_specs=[pl.BlockSpec(memory_space=pltpu.SMEM), tokb(DM), tokb(DKV), tokb(DKV), cache, cache],
        out_specs=(pl.BlockSpec((DEC_G * n_tok, DM), lambda i: (i, 0)), cache, cache),
        out_shape=(jax.ShapeDtypeStruct((nb * n_tok, DM), F32),
                   jax.ShapeDtypeStruct(kc.shape, F32), jax.ShapeDtypeStruct(vc.shape, F32)),
        compiler_params=_cparams(("arbitrary",), VMEM_LIMIT),
        name="swa_decode",
    )(sinks, qa, ka, va, kc, vc)


def _outproj_router_kernel(n_ptiles, xp_ref, xs_ref, hmp_ref, hap_ref, hms_ref, has_ref, wom_ref, woa_ref,
                           g_ref, wrt_ref, br_ref,
                           x1_ref, xg_ref, meta_ref, cnt_ref):
    i = pl.program_id(0)
    is_p = i < n_ptiles
    x = jnp.where(is_p, xp_ref[...], xs_ref[...])
    hm = jnp.where(is_p, hmp_ref[...], hms_ref[...].astype(BF16))
    ha = jnp.where(is_p, hap_ref[...], has_ref[...].astype(BF16))
    x1 = (x + jnp.dot(hm, wom_ref[...], preferred_element_type=F32)
          + jnp.dot(ha, woa_ref[...], preferred_element_type=F32))
    x1_ref[...] = x1
    h2 = _rms(x1, g_ref[...])
    logits = lax.dot_general(wrt_ref[...], h2, (((1,), (1,)), ((), ())), precision=HIGHEST,
                             preferred_element_type=F32) + br_ref[...]
    eidx = lax.broadcasted_iota(jnp.int32, logits.shape, 0).astype(F32)
    work = logits
    vals, hots = [], []
    for _ in range(TOP_K):
        mv = jnp.max(work, axis=0, keepdims=True)
        sel = jnp.min(jnp.where(work == mv, eidx, float(N_EXPERTS)), axis=0, keepdims=True)
        hot = eidx == sel
        vals.append(mv)
        hots.append(hot)
        work = jnp.where(hot, -jnp.inf, work)
    es = [jnp.exp(v - vals[0]) for v in vals]
    tot = es[0] + es[1] + es[2] + es[3]
    gates = [e / tot for e in es]
    hot_all = jnp.where(hots[0] | hots[1] | hots[2] | hots[3], 1.0, 0.0)
    tm = logits.shape[1]
    su = (lax.broadcasted_iota(jnp.int32, (tm, tm), 0) < lax.broadcasted_iota(jnp.int32, (tm, tm), 1))
    cum = jnp.dot(hot_all.astype(BF16), su.astype(BF16), preferred_element_type=F32)
    cnt = jnp.sum(hot_all, axis=1, keepdims=True)
    cpad = (((cnt.astype(jnp.int32) + (SUBLANES - 1)) // SUBLANES) * SUBLANES).astype(F32)
    lower = (lax.broadcasted_iota(jnp.int32, (N_EXPERTS, N_EXPERTS), 0)
             > lax.broadcasted_iota(jnp.int32, (N_EXPERTS, N_EXPERTS), 1)).astype(F32)
    lstart = jnp.dot(lower, jnp.broadcast_to(cpad, (N_EXPERTS, LANES)), precision=HIGHEST,
                     preferred_element_type=F32)[:, 0:1]
    base = lstart + cum
    lpos = [jnp.sum(jnp.where(hot, base, 0.0), axis=0, keepdims=True) for hot in hots]
    lpi = [p.astype(jnp.int32) for p in lpos]
    r_iota = lax.broadcasted_iota(jnp.int32, (GROUP_R, tm), 0)
    sel01 = jnp.where(r_iota == lpi[0], 1.0, jnp.where(r_iota == lpi[1], 1.0, jnp.where(
        r_iota == lpi[2], 1.0, jnp.where(r_iota == lpi[3], 1.0, 0.0)))).astype(BF16)
    xg_ref[...] = jnp.dot(sel01, h2.astype(BF16), preferred_element_type=F32)
    meta_ref[...] = jnp.transpose(jnp.concatenate(gates + lpos, axis=0))
    cnt_ref[0] = jnp.broadcast_to(cnt, (N_EXPERTS, LANES))


def _outproj_router(xp, xs, hmp, hap, hms, has, wom, woa, g_ffn, wrt, br):
    tp, ts = xp.shape[0], xs.shape[0]
    n_pt, n_st = tp // TM, ts // TM
    t_all = tp + ts
    pblk = lambda w: pl.BlockSpec((TM, w), lambda i: (jnp.minimum(i, n_pt - 1), 0))
    sblk = lambda w: pl.BlockSpec((TM, w), lambda i: (jnp.maximum(i - n_pt, 0), 0))
    full = lambda a: pl.BlockSpec(a.shape, lambda i: (0,) * a.ndim)
    return pl.pallas_call(
        functools.partial(_outproj_router_kernel, n_pt),
        grid=(n_pt + n_st,),
        in_specs=[pblk(D_MODEL), sblk(D_MODEL), pblk(DM), pblk(DM), sblk(DM), sblk(DM),
                  full(wom), full(woa), full(g_ffn), full(wrt), full(br)],
        out_specs=(pl.BlockSpec((TM, D_MODEL), lambda i: (i, 0)),
                   pl.BlockSpec((GROUP_R, D_MODEL), lambda i: (i, 0)),
                   pl.BlockSpec((TM, 2 * TOP_K), lambda i: (i, 0)),
                   pl.BlockSpec((1, N_EXPERTS, LANES), lambda i: (i, 0, 0))),
        out_shape=(jax.ShapeDtypeStruct((t_all, D_MODEL), F32),
                   jax.ShapeDtypeStruct(((n_pt + n_st) * GROUP_R, D_MODEL), F32),
                   jax.ShapeDtypeStruct((t_all, 2 * TOP_K), F32),
                   jax.ShapeDtypeStruct((n_pt + n_st, N_EXPERTS, LANES), F32)),
        compiler_params=_cparams(("arbitrary",), VMEM_LIMIT),
        name="outproj_router",
    )(xp, xs, hmp, hap, hms, has, wom, woa, g_ffn, wrt, br)


def _expert_kernel(be_ref, na_ref, slot_ref, nxt_ref, ctab_ref, ctab_next_ref, xg_ref, wgu_ref, bgu_ref,
                   wd_ref, bd_ref, o_ref, xbuf, wgu_f, wd_f, wgu_s, wd_s, xsem, wsem):
    i = pl.program_id(0)
    na = na_ref[0]

    def x_copies(tab_ref, slot):
        return [pltpu.make_async_copy(xg_ref.at[pl.ds(pl.multiple_of(tab_ref[0, 0, c], SUBLANES), SUBLANES)],
                                      xbuf.at[slot, pl.ds(c * SUBLANES, SUBLANES)], xsem.at[slot])
                for c in range(MOE_BM // SUBLANES)]

    def w_copies(e, slot):
        return [pltpu.make_async_copy(wgu_ref.at[e], wgu_f.at[slot], wsem.at[slot]),
                pltpu.make_async_copy(wd_ref.at[e], wd_f.at[slot], wsem.at[slot])]

    @pl.when(i == 0)
    def _():
        for cp in x_copies(ctab_ref, 0) + w_copies(be_ref[0], 0):
            cp.start()

    @pl.when(i < na)
    def _():
        changed = (i == 0) | (be_ref[i] != be_ref[jnp.maximum(i - 1, 0)])
        wslot = slot_ref[i]
        xslot = lax.rem(i, 2)

        @pl.when(i + 1 < na)
        def _():
            for cp in x_copies(ctab_next_ref, 1 - xslot):
                cp.start()

        @pl.when(changed)
        def _():
            for cp in w_copies(be_ref[i], wslot):
                cp.wait()

            @pl.when(nxt_ref[i] >= 0)
            def _():
                for cp in w_copies(nxt_ref[i], 1 - wslot):
                    cp.start(priority=1)

            wgu_s[...] = wgu_f[wslot].astype(BF16)
            wd_s[...] = wd_f[wslot].astype(BF16)

        for cp in x_copies(ctab_ref, xslot):
            cp.wait()

        x = xbuf[xslot].astype(BF16)
        gu = jnp.dot(x, wgu_s[...], preferred_element_type=F32) + bgu_ref[...]
        gate = jnp.minimum(gu[:, :D_FF], SWIGLU_LIMIT)
        up = jnp.clip(gu[:, D_FF:], -SWIGLU_LIMIT, SWIGLU_LIMIT)
        act = (up + 1.0) * (gate * _sigmoid(gate * SWIGLU_ALPHA))
        o_ref[...] = jnp.dot(act.astype(BF16), wd_s[...], preferred_element_type=F32) + bd_ref[...]


def _expert_ffn(block_exp, n_active, wslot, next_exp, ctab, xg, wgu, bgu, wd, bd):
    n_blocks = ctab.shape[0]
    n_rows = n_blocks * MOE_BM
    nch = MOE_BM // SUBLANES
    grid_spec = pltpu.PrefetchScalarGridSpec(
        num_scalar_prefetch=4,
        grid=(n_blocks,),
        in_specs=[pl.BlockSpec((1, 1, nch), lambda i, *_: (i, 0, 0), memory_space=pltpu.SMEM),
                  pl.BlockSpec((1, 1, nch), lambda i, *_: (jnp.minimum(i + 1, n_blocks - 1), 0, 0),
                               memory_space=pltpu.SMEM),
                  pl.BlockSpec(memory_space=pl.ANY),
                  pl.BlockSpec(memory_space=pl.ANY),
                  pl.BlockSpec((None, 1, 2 * D_FF), lambda i, be, *_: (be[i], 0, 0)),
                  pl.BlockSpec(memory_space=pl.ANY),
                  pl.BlockSpec((None, 1, D_MODEL), lambda i, be, *_: (be[i], 0, 0))],
        out_specs=pl.BlockSpec((MOE_BM, D_MODEL), lambda i, be, na, *_: (jnp.minimum(i, na[0] - 1), 0)),
        scratch_shapes=[pltpu.VMEM((2, MOE_BM, D_MODEL), F32),
                        pltpu.VMEM((2, D_MODEL, 2 * D_FF), F32), pltpu.VMEM((2, D_FF, D_MODEL), F32),
                        pltpu.VMEM((D_MODEL, 2 * D_FF), BF16), pltpu.VMEM((D_FF, D_MODEL), BF16),
                        pltpu.SemaphoreType.DMA((2,)), pltpu.SemaphoreType.DMA((2,))],
    )
    return pl.pallas_call(
        _expert_kernel,
        grid_spec=grid_spec,
        out_shape=jax.ShapeDtypeStruct((n_rows, D_MODEL), F32),
        compiler_params=_cparams(("arbitrary",), VMEM_LIMIT),
        name="moe_experts",
    )(block_exp, n_active, wslot, next_exp, ctab, ctab, xg, wgu, bgu, wd, bd)


def _combine_kernel(n_ptiles, ctab_ref, ctab_next_ref, outs_ref, x1_ref, meta_ref, gf_ref, yp_ref, ys_ref,
                    obuf, sem):
    i = pl.program_id(0)
    n = pl.num_programs(0)
    slot = lax.rem(i, 2)

    def copies(tab_ref, s):
        return [pltpu.make_async_copy(outs_ref.at[pl.ds(pl.multiple_of(tab_ref[0, 0, c], SUBLANES), SUBLANES)],
                                      obuf.at[s, pl.ds(c * SUBLANES, SUBLANES)], sem.at[s])
                for c in range(GROUP_R // SUBLANES)]

    @pl.when(i == 0)
    def _():
        for cp in copies(ctab_ref, 0):
            cp.start()

    @pl.when(i + 1 < n)
    def _():
        for cp in copies(ctab_next_ref, 1 - slot):
            cp.start()

    for cp in copies(ctab_ref, slot):
        cp.wait()

    meta = meta_ref[...]
    tm = meta.shape[0]
    r_iota = lax.broadcasted_iota(jnp.int32, (tm, GROUP_R), 1)
    lp = [meta[:, TOP_K + k:TOP_K + k + 1].astype(jnp.int32) for k in range(TOP_K)]
    gk = [meta[:, k:k + 1] for k in range(TOP_K)]
    gsel = jnp.where(r_iota == lp[0], gk[0], jnp.where(r_iota == lp[1], gk[1], jnp.where(
        r_iota == lp[2], gk[2], jnp.where(r_iota == lp[3], gk[3], 0.0))))
    sel01 = jnp.where(gsel != 0.0, 1.0, 0.0).astype(BF16)
    rg_row = jnp.sum(gsel, axis=0, keepdims=True)
    rg_col = jnp.transpose(jnp.broadcast_to(rg_row, (SUBLANES, GROUP_R)))[:, 0:1]
    og = obuf[slot] * rg_col
    og_hi = og.astype(BF16)
    og_lo = (og - og_hi.astype(F32)).astype(BF16)
    acc = (x1_ref[...] + jnp.dot(sel01, og_hi, preferred_element_type=F32)
           + jnp.dot(sel01, og_lo, preferred_element_type=F32))
    y = _rms(acc, gf_ref[...])

    @pl.when(i < n_ptiles)
    def _():
        yp_ref[...] = y

    @pl.when(i >= n_ptiles)
    def _():
        ys_ref[...] = y


def _combine(ctab, outs, x1, meta, g_final, tp, ts):
    n_pt, n_st = tp // TM, ts // TM
    n = n_pt + n_st
    nch = GROUP_R // SUBLANES
    return pl.pallas_call(
        functools.partial(_combine_kernel, n_pt),
        grid=(n,),
        in_specs=[pl.BlockSpec((1, 1, nch), lambda i: (i, 0, 0), memory_space=pltpu.SMEM),
                  pl.BlockSpec((1, 1, nch), lambda i: (jnp.minimum(i + 1, n - 1), 0, 0), memory_space=pltpu.SMEM),
                  pl.BlockSpec(memory_space=pl.ANY),
                  pl.BlockSpec((TM, D_MODEL), lambda i: (i, 0)),
                  pl.BlockSpec((TM, 2 * TOP_K), lambda i: (i, 0)),
                  pl.BlockSpec((1, D_MODEL), lambda i: (0, 0))],
        out_specs=(pl.BlockSpec((TM, D_MODEL), lambda i: (jnp.minimum(i, n_pt - 1), 0)),
                   pl.BlockSpec((TM, D_MODEL), lambda i: (jnp.maximum(i - n_pt, 0), 0))),
        out_shape=(jax.ShapeDtypeStruct((tp, D_MODEL), F32), jax.ShapeDtypeStruct((ts, D_MODEL), F32)),
        scratch_shapes=[pltpu.VMEM((2, GROUP_R, D_MODEL), F32), pltpu.SemaphoreType.DMA((2,))],
        compiler_params=_cparams(("arbitrary",), VMEM_LIMIT),
        name="moe_combine",
    )(ctab, ctab, outs, x1, meta, g_final)


def kernel(x_prompt, x_sample, cache_swa_k, cache_swa_v, state_mlstm_c, state_mlstm_n, state_mlstm_m,
           g_mix, w_in, b_igate, b_fgate, g_head, attn_sinks, w_out, g_ffn, w_router, b_router,
           w_gate_up, b_gate_up, w_down, b_down, g_final):
    assert w_in.shape[0] == 1, "single-layer problem"
    B, S, _ = x_prompt.shape
    Bd, Tn, _ = x_sample.shape
    tp, ts = B * S, Bd * Tn
    t_all = tp + ts
    xp = x_prompt.reshape(tp, D_MODEL)
    xs = x_sample.reshape(ts, D_MODEL)

    w = w_in[0]
    o = np.cumsum([0, DM, DM, DM, DM, N_HEADS, N_HEADS, DM, DKV, DKV])
    col = lambda a: w[:, int(o[a]):int(o[a + 1])]
    w1 = jnp.concatenate([col(0), col(2), col(3), col(6), col(7), col(8)], axis=1).astype(BF16)
    wkt = col(1).T.astype(BF16)
    wgates = jnp.concatenate([col(4), col(5)], axis=1)
    wg = jnp.pad(wgates, ((0, 0), (0, LANES - 2 * N_HEADS))).astype(BF16)
    wgt = wgates.T.astype(BF16)
    bg = jnp.concatenate([b_igate[0], b_fgate[0]]).astype(F32)
    bcol = jnp.pad(bg, (0, LANES - 2 * N_HEADS)).reshape(1, LANES)
    brow = bg.reshape(2 * N_HEADS, 1)

    qm, vm, om, qa, ka, va, kmt, gcol, grow = _inproj(xp, xs, g_mix[0].reshape(1, D_MODEL), w1, wkt, wg, wgt,
                                                      bcol, brow)

    gh = g_head[0].astype(F32)
    sinks = attn_sinks[0].astype(F32)

    hm_p, cbd, m_p = _mlstm_prompt(qm, kmt, vm, om, gcol, grow, gh.reshape(1, DM), B, S)
    ha_p = _swa_prompt(sinks, qa, ka, va, B, S)

    hm_s, c_s, n_s, mt_s = _mlstm_sample(qm, kmt, vm, om, gcol, grow, state_mlstm_c[0], state_mlstm_n[0],
                                         state_mlstm_m[0], gh.reshape(1, DM), tp, Tn)
    n_s = n_s.reshape(Bd, N_HEADS, HEAD_DIM)
    m_s = mt_s[:, :, :LANES // Tn].transpose(0, 2, 1).reshape(Bd, N_HEADS)
    ha_s, k_s, v_s = _swa_decode(sinks, qa, ka, va, cache_swa_k[0].reshape(Bd, WINDOW, DKV),
                                 cache_swa_v[0].reshape(Bd, WINDOW, DKV), tp, Tn)

    wo = w_out[0].astype(BF16)
    x1, xg, meta, cnt = _outproj_router(
        xp, xs, hm_p, ha_p, hm_s, ha_s, wo[:DM], wo[DM:], g_ffn[0].reshape(1, D_MODEL),
        w_router[0].T, b_router[0].reshape(N_EXPERTS, 1))

    i32 = jnp.int32
    n_tiles = t_all // TM
    max_rows = t_all * TOP_K + n_tiles * N_EXPERTS * (SUBLANES - 1) + N_EXPERTS * (MOE_BM - 1)
    n_blocks = -(-max_rows // MOE_BM)
    cpad = (cnt[:, :, 0].astype(i32) + (SUBLANES - 1)) // SUBLANES * SUBLANES
    lstart = jnp.cumsum(cpad, axis=1) - cpad
    goff = jnp.cumsum(cpad, axis=0) - cpad
    padded = (jnp.sum(cpad, axis=0) + MOE_BM - 1) // MOE_BM * MOE_BM
    pad_end = jnp.cumsum(padded)
    seg_begin = (pad_end - padded)[None, :] + goff
    n_active = (pad_end[-1] // MOE_BM).astype(i32)
    blk = jnp.minimum(jnp.arange(n_blocks, dtype=i32), n_active - 1)
    block_exp = jnp.minimum(jnp.sum((pad_end[None, :] <= (blk * MOE_BM)[:, None]).astype(i32), axis=1),
                            N_EXPERTS - 1)
    e_ids = jnp.arange(N_EXPERTS, dtype=i32)
    nonempty = padded > 0
    nxt_e = jnp.min(jnp.where((e_ids[None, :] > e_ids[:, None]) & nonempty[None, :], e_ids[None, :], N_EXPERTS),
                    axis=1)
    nxt_e = jnp.where(nxt_e == N_EXPERTS, -1, nxt_e)
    ord_e = jnp.cumsum(nonempty.astype(i32)) - 1
    be_hot = block_exp[:, None] == e_ids[None, :]
    next_exp = jnp.sum(jnp.where(be_hot, nxt_e[None, :], 0), axis=1).astype(i32)
    wslot = (jnp.sum(jnp.where(be_hot, ord_e[None, :], 0), axis=1) % 2).astype(i32)

    seg_src = jnp.arange(n_tiles, dtype=i32)[:, None] * GROUP_R + lstart
    sb, sl, ss = seg_begin.reshape(-1), cpad.reshape(-1), seg_src.reshape(-1)
    rc = jnp.arange(n_blocks * MOE_BM // SUBLANES, dtype=i32)[:, None] * SUBLANES
    inseg = (sb[None, :] <= rc) & (rc < (sb + sl)[None, :])
    ctab_e = jnp.where(jnp.any(inseg, axis=1), jnp.sum(jnp.where(inseg, (ss - sb)[None, :] + rc, 0), axis=1),
                       GROUP_R - SUBLANES)
    lr = jnp.arange(GROUP_R // SUBLANES, dtype=i32)[None, :, None] * SUBLANES
    inl = (lstart[:, None, :] <= lr) & (lr < (lstart + cpad)[:, None, :])
    ctab_c = jnp.sum(jnp.where(inl, (seg_begin - lstart)[:, None, :] + lr, 0), axis=2)

    outs = _expert_ffn(block_exp, n_active.reshape(1), wslot, next_exp,
                       ctab_e.astype(i32).reshape(n_blocks, 1, MOE_BM // SUBLANES), xg, w_gate_up[0],
                       b_gate_up[0].reshape(N_EXPERTS, 1, 2 * D_FF), w_down[0],
                       b_down[0].reshape(N_EXPERTS, 1, D_MODEL))
    y_p, y_s = _combine(ctab_c.astype(i32).reshape(n_tiles, 1, GROUP_R // SUBLANES), outs, x1, meta,
                        g_final.reshape(1, D_MODEL), tp, ts)

    kv_tail = lambda a: a[:tp].reshape(B, S, N_KV, HEAD_DIM)[:, S - WINDOW:][None]
    c_e = cbd[:, :, :HEAD_DIM, :HEAD_DIM]
    c_o = cbd[:, :, HEAD_DIM:, HEAD_DIM:LANES]
    c_p = jnp.stack([c_e, c_o], axis=2).reshape(B, N_HEADS, HEAD_DIM, HEAD_DIM)
    n_p = jnp.stack([cbd[:, :, :HEAD_DIM, LANES], cbd[:, :, HEAD_DIM:, LANES + 1]], axis=2).reshape(B, N_HEADS, HEAD_DIM)
    return (y_p.reshape(B, S, D_MODEL), y_s.reshape(Bd, Tn, D_MODEL),
            kv_tail(ka), kv_tail(va), c_p[None], n_p[None], m_p[:, :, 0][None],
            k_s.reshape(Bd, WINDOW, N_KV, HEAD_DIM)[None], v_s.reshape(Bd, WINDOW, N_KV, HEAD_DIM)[None],
            c_s[None], n_s[None], m_s[None])
```

```python
import functools
import math

import jax
import jax.numpy as jnp
import numpy as np
from jax import lax
from jax.experimental import pallas as pl
from jax.experimental.pallas import tpu as pltpu

F32 = jnp.float32
BF16 = jnp.bfloat16
HIGHEST = lax.Precision.HIGHEST

D_MODEL = 1024
HEAD_DIM = 64
N_HEADS = 8
N_PAIRS = N_HEADS // 2
N_KV = 2
GROUP = N_HEADS // N_KV
WINDOW = 128
N_EXPERTS = 32
TOP_K = 4
D_FF = 1024
SWIGLU_LIMIT = 7.0
SWIGLU_ALPHA = 1.702
RMS_EPS = 1e-5
DM = N_HEADS * HEAD_DIM
DKV = N_KV * HEAD_DIM
NEG = -1e30

LANES = 128
SUBLANES = 8
VMEM_LIMIT = 56 * 1024 * 1024

TM = 512
MLSTM_TL = 512
MLSTM_L = 128
DEC_G = 8
DEC_UNROLL = 4
MOE_BM = 256
FF_TILE = 256
GROUP_R = -(-(TOP_K * TM + N_EXPERTS * (SUBLANES - 1) + SUBLANES) // LANES) * LANES


def _cparams(sem, vmem=None):
    return pltpu.CompilerParams(dimension_semantics=sem, vmem_limit_bytes=vmem)


def _rms(x, g):
    return x * lax.rsqrt(jnp.mean(x * x, axis=-1, keepdims=True) + RMS_EPS) * g


def _log_sigmoid(z):
    return jnp.minimum(z, 0.0) - jnp.log(1.0 + jnp.exp(-jnp.abs(z)))


def _sigmoid(z):
    return 1.0 / (1.0 + jnp.exp(-z))


def _inproj_kernel(n_ptiles, xp_ref, xs_ref, g_ref, w1_ref, wkt_ref, wg_ref, wgt_ref, bcol_ref, brow_ref,
                   qm_ref, vm_ref, om_ref, qa_ref, ka_ref, va_ref, kmt_ref, gcol_ref, grow_ref):
    i = pl.program_id(0)
    x = jnp.where(i < n_ptiles, xp_ref[...], xs_ref[...])
    h = _rms(x, g_ref[...]).astype(BF16)
    main = jnp.dot(h, w1_ref[...], preferred_element_type=F32)
    qm_ref[...] = main[:, 0:DM]
    vm_ref[...] = main[:, DM:2 * DM]
    om_ref[...] = main[:, 2 * DM:3 * DM]
    qa_ref[...] = main[:, 3 * DM:4 * DM]
    ka_ref[...] = main[:, 4 * DM:4 * DM + DKV]
    va_ref[...] = main[:, 4 * DM + DKV:4 * DM + 2 * DKV]
    kt = lax.dot_general(wkt_ref[...], h, (((1,), (1,)), ((), ())), preferred_element_type=F32)
    kmt_ref[...] = kt * (HEAD_DIM ** -0.5)
    zc = jnp.dot(h, wg_ref[...], preferred_element_type=F32) + bcol_ref[...]
    lane = lax.broadcasted_iota(jnp.int32, zc.shape, 1)
    gcol_ref[...] = jnp.where(lane < N_HEADS, zc, _log_sigmoid(zc))
    zr = lax.dot_general(wgt_ref[...], h, (((1,), (1,)), ((), ())), preferred_element_type=F32) + brow_ref[...]
    row = lax.broadcasted_iota(jnp.int32, zr.shape, 0)
    grow_ref[...] = jnp.where(row < N_HEADS, zr, _log_sigmoid(zr))


def _inproj(xp, xs, g_mix, w1, wkt, wg, wgt, bcol, brow):
    tp, ts = xp.shape[0], xs.shape[0]
    n_pt, n_st = tp // TM, ts // TM
    t_all = tp + ts
    n1 = w1.shape[1]
    tok = lambda w: pl.BlockSpec((TM, w), lambda i: (i, 0))
    full = lambda a: pl.BlockSpec(a.shape, lambda i: (0,) * a.ndim)
    out_shape = (
        jax.ShapeDtypeStruct((t_all, DM), F32), jax.ShapeDtypeStruct((t_all, DM), F32),
        jax.ShapeDtypeStruct((t_all, DM), F32), jax.ShapeDtypeStruct((t_all, DM), F32),
        jax.ShapeDtypeStruct((t_all, DKV), F32), jax.ShapeDtypeStruct((t_all, DKV), F32),
        jax.ShapeDtypeStruct((DM, t_all), F32),
        jax.ShapeDtypeStruct((t_all, LANES), F32), jax.ShapeDtypeStruct((2 * N_HEADS, t_all), F32),
    )
    return pl.pallas_call(
        functools.partial(_inproj_kernel, n_pt),
        grid=(n_pt + n_st,),
        in_specs=[
            pl.BlockSpec((TM, D_MODEL), lambda i: (jnp.minimum(i, n_pt - 1), 0)),
            pl.BlockSpec((TM, D_MODEL), lambda i: (jnp.maximum(i - n_pt, 0), 0)),
            full(g_mix), full(w1), full(wkt), full(wg), full(wgt), full(bcol), full(brow),
        ],
        out_specs=(tok(DM), tok(DM), tok(DM), tok(DM), tok(DKV), tok(DKV),
                   pl.BlockSpec((DM, TM), lambda i: (0, i)),
                   tok(LANES), pl.BlockSpec((2 * N_HEADS, TM), lambda i: (0, i))),
        out_shape=out_shape,
        compiler_params=_cparams(("arbitrary",), VMEM_LIMIT),
        name="inproj",
    )(xp, xs, g_mix, w1, wkt, wg, wgt, bcol, brow)


def _cumsum_rows(x, n):
    row = lax.broadcasted_iota(jnp.int32, x.shape, 0)
    sh = 1
    while sh < n:
        x = x + jnp.where(row >= sh, pltpu.roll(x, sh, axis=0), 0.0)
        sh *= 2
    return x


def _mlstm_prompt_kernel(qm_ref, kmt_ref, vm_ref, om_ref, gcol_ref, grow_ref, gh_ref,
                         hm_ref, cbd_ref, m_ref, cbd_s, m_s):
    j = pl.program_id(1)
    L = MLSTM_L

    @pl.when(j == 0)
    def _():
        cbd_s[...] = jnp.zeros_like(cbd_s)
        m_s[...] = jnp.zeros_like(m_s)

    ti = lax.broadcasted_iota(jnp.int32, (L, L), 0)
    si = lax.broadcasted_iota(jnp.int32, (L, L), 1)
    causal = ti >= si
    upper = (ti <= si).astype(F32)
    lane128 = lax.broadcasted_iota(jnp.int32, (L, LANES), 1)
    even128 = lane128 < HEAD_DIM
    lane256 = lax.broadcasted_iota(jnp.int32, (1, 2 * LANES), 1)
    cols_e = (lane256 < HEAD_DIM) | (lane256 == LANES)
    cols_o = ((lane256 >= HEAD_DIM) & (lane256 < LANES)) | (lane256 == LANES + 1)
    rowk = lax.broadcasted_iota(jnp.int32, (LANES, 1), 0)
    rows_e = rowk < HEAD_DIM
    bd_mask = (rows_e & cols_e) | ((~rows_e) & cols_o)
    ones_cols = ((lax.broadcasted_iota(jnp.int32, (L, LANES), 1) < 2)).astype(F32)
    bo_r = lax.broadcasted_iota(jnp.int32, (LANES, LANES), 0) // HEAD_DIM
    bo_c = lax.broadcasted_iota(jnp.int32, (LANES, LANES), 1) // HEAD_DIM
    block_ones = (bo_r == bo_c).astype(F32)

    for c in range(MLSTM_TL // L):
        sl = slice(c * L, (c + 1) * L)
        grow = grow_ref[:, sl]
        i_row = grow[0:N_HEADS]
        b_row = jnp.dot(grow[N_HEADS:2 * N_HEADS], upper, precision=HIGHEST,
                        preferred_element_type=F32)
        b_colall = _cumsum_rows(gcol_ref[sl, :], L)
        rowv_all = i_row - b_row
        for p in range(N_PAIRS):
            ls = slice(p * LANES, (p + 1) * LANES)
            q2 = qm_ref[sl, ls].astype(BF16)
            kt2 = kmt_ref[ls, sl]
            v2 = vm_ref[sl, ls]
            vext = jnp.concatenate([v2, ones_cols], axis=1)
            kt_e = jnp.where(rows_e, kt2, 0.0).astype(BF16)
            kt_o = jnp.where(rows_e, 0.0, kt2).astype(BF16)
            s2 = jnp.dot(q2, jnp.concatenate([kt_e, kt_o], axis=1), preferred_element_type=F32)
            cbd = cbd_s[p]
            rq = jnp.dot(q2, cbd.astype(BF16), preferred_element_type=F32)
            ps, mts, inters, wrows, decays, mnews = [], [], [], [], [], []
            for hh in range(2):
                h = 2 * p + hh
                bcol = b_colall[:, N_HEADS + h:N_HEADS + h + 1]
                logd = jnp.where(causal, bcol + rowv_all[h:h + 1, :], NEG)
                m_prev = m_s[h:h + 1, 0:1]
                m_inter = m_prev + bcol
                m_t = jnp.maximum(m_inter, jnp.max(logd, axis=1, keepdims=True))
                d = jnp.exp(logd - m_t)
                ps.append((s2[:, hh * L:(hh + 1) * L] * d).astype(BF16))
                mts.append(m_t)
                inters.append(jnp.exp(m_inter - m_t))
                m_new = m_t[L - 1:L, :]
                b_last = b_row[h:h + 1, L - 1:L]
                decays.append(jnp.exp(m_prev + b_last - m_new))
                wrows.append(jnp.exp(b_last - b_row[h:h + 1, :] + i_row[h:h + 1, :] - m_new))
                mnews.append(m_new)
            vstack = jnp.concatenate([jnp.where(cols_e, vext, 0.0), jnp.where(cols_o, vext, 0.0)],
                                     axis=0).astype(BF16)
            r = jnp.dot(jnp.concatenate(ps, axis=1), vstack, preferred_element_type=F32)
            nd = r + jnp.where(cols_e, inters[0], inters[1]) * rq
            num = nd[:, 0:LANES]
            den = jnp.where(even128, nd[:, LANES:LANES + 1], nd[:, LANES + 1:LANES + 2])
            mt2 = jnp.where(even128, mts[0], mts[1])
            hv = num / jnp.maximum(jnp.abs(den), jnp.exp(-mt2))
            ms = jnp.dot(hv * hv, block_ones, precision=HIGHEST, preferred_element_type=F32) * (1.0 / HEAD_DIM)
            y = hv * lax.rsqrt(ms + RMS_EPS) * gh_ref[:, ls] * _sigmoid(om_ref[sl, ls])
            hm_ref[sl, ls] = y.astype(hm_ref.dtype)
            w2 = jnp.where(rows_e, wrows[0], wrows[1])
            upd = jnp.dot((kt2 * w2).astype(BF16), vext.astype(BF16), preferred_element_type=F32)
            dec2 = jnp.where(rows_e, decays[0], decays[1])
            cbd_s[p] = dec2 * cbd + jnp.where(bd_mask, upd, 0.0)
            for hh in range(2):
                h = 2 * p + hh
                m_s[h:h + 1, :] = jnp.broadcast_to(mnews[hh], (1, LANES))

    @pl.when(j == pl.num_programs(1) - 1)
    def _():
        cbd_ref[0] = cbd_s[...]
        m_ref[0] = m_s[...]


def _mlstm_prompt(qm, kmt, vm, om, gcol, grow, gh, batch, seq):
    nt = seq // MLSTM_TL
    tokb = lambda w: pl.BlockSpec((MLSTM_TL, w), lambda b, j: (b * nt + j, 0))
    rowb = lambda r: pl.BlockSpec((r, MLSTM_TL), lambda b, j: (0, b * nt + j))
    return pl.pallas_call(
        _mlstm_prompt_kernel,
        grid=(batch, nt),
        in_specs=[tokb(DM), rowb(DM), tokb(DM), tokb(DM), tokb(LANES), rowb(2 * N_HEADS),
                  pl.BlockSpec((1, DM), lambda b, j: (0, 0))],
        out_specs=(tokb(DM),
                   pl.BlockSpec((1, N_PAIRS, LANES, 2 * LANES), lambda b, j: (b, 0, 0, 0)),
                   pl.BlockSpec((1, N_HEADS, LANES), lambda b, j: (b, 0, 0))),
        out_shape=(jax.ShapeDtypeStruct((batch * seq, DM), BF16),
                   jax.ShapeDtypeStruct((batch, N_PAIRS, LANES, 2 * LANES), F32),
                   jax.ShapeDtypeStruct((batch, N_HEADS, LANES), F32)),
        scratch_shapes=[pltpu.VMEM((N_PAIRS, LANES, 2 * LANES), F32), pltpu.VMEM((N_HEADS, LANES), F32)],
        compiler_params=_cparams(("arbitrary", "arbitrary"), VMEM_LIMIT),
        name="mlstm_prompt",
    )(qm, kmt, vm, om, gcol, grow, gh)


def _mlstm_sample_kernel(n_tok, qm_ref, kmt_ref, vm_ref, om_ref, gcol_ref, grow_ref, c0_ref, n0_ref, m0_ref,
                         m0t_ref, gh_ref, hm_ref, c_ref, n_ref, mt_ref):
    L = LANES
    NB = L // n_tok
    ti = lax.broadcasted_iota(jnp.int32, (L, L), 0)
    si = lax.broadcasted_iota(jnp.int32, (L, L), 1)
    same = (ti // n_tok) == (si // n_tok)
    causal = same & (ti >= si)
    useg = jnp.where(same & (ti <= si), 1.0, 0.0)
    slast = jnp.where(same & (ti % n_tok == n_tok - 1), 1.0, 0.0)
    expand = jnp.where(ti // n_tok == si, 1.0, 0.0)
    expand_t = jnp.where(ti == si // n_tok, 1.0, 0.0)
    pick = jnp.where((ti // n_tok == si) & (ti % n_tok == n_tok - 1), 1.0, 0.0)
    hdot = lambda a, b: jnp.dot(a, b, precision=HIGHEST, preferred_element_type=F32)

    lane128 = lax.broadcasted_iota(jnp.int32, (L, LANES), 1)
    even128 = lane128 < HEAD_DIM
    lane256 = lax.broadcasted_iota(jnp.int32, (1, 2 * LANES), 1)
    cols_e = (lane256 < HEAD_DIM) | (lane256 == LANES)
    cols_o = ((lane256 >= HEAD_DIM) & (lane256 < LANES)) | (lane256 == LANES + 1)
    rows_e = lax.broadcasted_iota(jnp.int32, (LANES, 1), 0) < HEAD_DIM
    ones_cols = jnp.where(lane128 < 2, 1.0, 0.0)
    bo_r = lax.broadcasted_iota(jnp.int32, (LANES, LANES), 0) // HEAD_DIM
    bo_c = lax.broadcasted_iota(jnp.int32, (LANES, LANES), 1) // HEAD_DIM
    block_ones = jnp.where(bo_r == bo_c, 1.0, 0.0)
    W = NB * LANES
    rb = lax.broadcasted_iota(jnp.int32, (L, W), 0)
    cb = lax.broadcasted_iota(jnp.int32, (L, W), 1)
    own_block = (rb // n_tok) == (cb // LANES)
    bd_tiled = (rb // HEAD_DIM) == ((cb % LANES) // HEAD_DIM)

    grow = grow_ref[...]
    i_row = grow[0:N_HEADS]
    b_row = hdot(grow[N_HEADS:2 * N_HEADS], useg)
    b_last = hdot(b_row, slast)
    a_row = b_last - b_row + i_row
    pos = lax.broadcasted_iota(jnp.int32, a_row.shape, 1) % n_tok
    pm = a_row
    sh = 1
    while sh < n_tok:
        pm = jnp.where(pos >= sh, jnp.maximum(pm, pltpu.roll(pm, sh, axis=1)), pm)
        sh *= 2
    m_carry = hdot(jnp.concatenate([m0t_ref[0], b_row], axis=1), jnp.concatenate([expand_t, slast], axis=0))
    m_new_row = jnp.maximum(m_carry, hdot(pm, slast))
    decay_row = jnp.exp(m_carry - m_new_row)
    w_row = jnp.exp(a_row - m_new_row)
    mt_ref[0] = hdot(m_new_row, pick)
    decay_bh = hdot(decay_row, pick)
    decay_hb = jnp.transpose(decay_bh)[0:NB]

    bc_all = gcol_ref[...]
    rowpos = lax.broadcasted_iota(jnp.int32, bc_all.shape, 0) % n_tok
    sh = 1
    while sh < n_tok:
        bc_all = bc_all + jnp.where(rowpos >= sh, pltpu.roll(bc_all, sh, axis=0), 0.0)
        sh *= 2
    pad_rows = lambda a: jnp.concatenate([a, jnp.zeros((L - NB, a.shape[1]), F32)], axis=0)
    m0_col = hdot(expand, pad_rows(m0_ref[...]))
    rowv_all = i_row - b_row

    for p in range(N_PAIRS):
        ls = slice(p * LANES, (p + 1) * LANES)
        q2f = qm_ref[:, ls]
        q2 = q2f.astype(BF16)
        kt2 = kmt_ref[ls, :]
        v2 = vm_ref[:, ls]
        vext = jnp.concatenate([v2, ones_cols], axis=1)
        kt_e = jnp.where(rows_e, kt2, 0.0).astype(BF16)
        kt_o = jnp.where(rows_e, 0.0, kt2).astype(BF16)
        s2 = jnp.dot(q2, jnp.concatenate([kt_e, kt_o], axis=1), preferred_element_type=F32)
        ps, mts, inters = [], [], []
        for hh in range(2):
            h = 2 * p + hh
            bcol = bc_all[:, N_HEADS + h:N_HEADS + h + 1]
            logd = jnp.where(causal, bcol + rowv_all[h:h + 1, :], NEG)
            m_inter = m0_col[:, h:h + 1] + bcol
            m_t = jnp.maximum(m_inter, jnp.max(logd, axis=1, keepdims=True))
            ps.append((s2[:, hh * L:(hh + 1) * L] * jnp.exp(logd - m_t)).astype(BF16))
            mts.append(m_t)
            inters.append(jnp.exp(m_inter - m_t))
        vstack = jnp.concatenate([jnp.where(cols_e, vext, 0.0), jnp.where(cols_o, vext, 0.0)],
                                 axis=0).astype(BF16)
        r = jnp.dot(jnp.concatenate(ps, axis=1), vstack, preferred_element_type=F32)
        zero = jnp.zeros((HEAD_DIM, HEAD_DIM), F32)
        cstack = jnp.concatenate(
            [jnp.concatenate([jnp.concatenate([c0_ref[b, 2 * p], zero], axis=1),
                              jnp.concatenate([zero, c0_ref[b, 2 * p + 1]], axis=1)], axis=0)
             for b in range(NB)], axis=1)
        rq_all = jnp.where(own_block, jnp.dot(q2, cstack.astype(BF16), preferred_element_type=F32), 0.0)
        rq = rq_all[:, 0:LANES]
        for b in range(1, NB):
            rq = rq + rq_all[:, b * LANES:(b + 1) * LANES]
        n_rows = hdot(expand, pad_rows(n0_ref[:, ls]))
        qn = hdot(q2f * n_rows, block_ones)
        inter2 = jnp.where(even128, inters[0], inters[1])
        num = r[:, 0:LANES] + inter2 * rq
        den = jnp.where(even128, r[:, LANES:LANES + 1], r[:, LANES + 1:LANES + 2]) + inter2 * qn
        mt2 = jnp.where(even128, mts[0], mts[1])
        hv = num / jnp.maximum(jnp.abs(den), jnp.exp(-mt2))
        ms = hdot(hv * hv, block_ones) * (1.0 / HEAD_DIM)
        y = hv * lax.rsqrt(ms + RMS_EPS) * gh_ref[:, ls] * _sigmoid(om_ref[:, ls])
        hm_ref[:, ls] = y.astype(hm_ref.dtype)
        kw = kt2 * jnp.where(rows_e, w_row[2 * p:2 * p + 1, :], w_row[2 * p + 1:2 * p + 2, :])
        vbd = jnp.where(own_block, jnp.concatenate([v2] * NB, axis=1), 0.0).astype(BF16)
        upd = jnp.dot(kw.astype(BF16), vbd, preferred_element_type=F32)
        upd = jnp.where(bd_tiled, upd, 0.0)
        for b in range(NB):
            bs = slice(b * LANES, (b + 1) * LANES)
            dec_b = jnp.where(rows_e, decay_hb[b:b + 1, 2 * p:2 * p + 1], decay_hb[b:b + 1, 2 * p + 1:2 * p + 2])
            cnew = dec_b * cstack[:, bs] + upd[:, bs]
            c_ref[b, 2 * p] = cnew[0:HEAD_DIM, 0:HEAD_DIM]
            c_ref[b, 2 * p + 1] = cnew[HEAD_DIM:LANES, HEAD_DIM:LANES]
        nsum = jnp.transpose(hdot(kw, expand))[0:NB]
        dec_n = jnp.where(lax.broadcasted_iota(jnp.int32, (NB, LANES), 1) < HEAD_DIM,
                          decay_hb[:, 2 * p:2 * p + 1], decay_hb[:, 2 * p + 1:2 * p + 2])
        n_ref[:, ls] = dec_n * n0_ref[:, ls] + nsum


def _mlstm_sample(qm, kmt, vm, om, gcol, grow, c0, n0, m0, gh, row0, n_tok):
    nb = c0.shape[0]
    g_nb = LANES // n_tok
    n_g = nb // g_nb
    blk0 = row0 // LANES
    tokb = lambda w: pl.BlockSpec((LANES, w), lambda i: (blk0 + i, 0))
    rowb = lambda r: pl.BlockSpec((r, LANES), lambda i: (0, blk0 + i))
    m0t = jnp.pad(m0.reshape(n_g, g_nb, N_HEADS).transpose(0, 2, 1),
                  ((0, 0), (0, 0), (0, LANES - g_nb)))
    return pl.pallas_call(
        functools.partial(_mlstm_sample_kernel, n_tok),
        grid=(n_g,),
        in_specs=[tokb(DM), rowb(DM), tokb(DM), tokb(DM), tokb(LANES), rowb(2 * N_HEADS),
                  pl.BlockSpec((g_nb, N_HEADS, HEAD_DIM, HEAD_DIM), lambda i: (i, 0, 0, 0)),
                  pl.BlockSpec((g_nb, DM), lambda i: (i, 0)),
                  pl.BlockSpec((g_nb, N_HEADS), lambda i: (i, 0)),
                  pl.BlockSpec((1, N_HEADS, LANES), lambda i: (i, 0, 0)),
                  pl.BlockSpec((1, DM), lambda i: (0, 0))],
        out_specs=(pl.BlockSpec((LANES, DM), lambda i: (i, 0)),
                   pl.BlockSpec((g_nb, N_HEADS, HEAD_DIM, HEAD_DIM), lambda i: (i, 0, 0, 0)),
                   pl.BlockSpec((g_nb, DM), lambda i: (i, 0)),
                   pl.BlockSpec((1, N_HEADS, LANES), lambda i: (i, 0, 0))),
        out_shape=(jax.ShapeDtypeStruct((nb * n_tok, DM), BF16), jax.ShapeDtypeStruct(c0.shape, F32),
                   jax.ShapeDtypeStruct((nb, DM), F32), jax.ShapeDtypeStruct((n_g, N_HEADS, LANES), F32)),
        compiler_params=_cparams(("arbitrary",), VMEM_LIMIT),
        name="mlstm_sample",
    )(qm, kmt, vm, om, gcol, grow, c0, n0.reshape(nb, DM), m0, m0t, gh)


def _alibi_slope(h):
    return float(np.float32(2.0 ** (-8.0 * (h + 1) / N_HEADS)))


def _dup_halves(x):
    lane = lax.broadcasted_iota(jnp.int32, x.shape, 1)
    xr = pltpu.roll(x, HEAD_DIM, axis=1)
    lo = lane < HEAD_DIM
    return jnp.where(lo, x, xr), jnp.where(lo, xr, x)


def _stack_group_queries(q, g):
    lane = lax.broadcasted_iota(jnp.int32, (q.shape[0], LANES), 1)
    parts = []
    for hh in range(GROUP):
        h = GROUP * g + hh
        blk = q[:, (h // 2) * LANES:(h // 2 + 1) * LANES]
        keep = (lane < HEAD_DIM) if h % 2 == 0 else (lane >= HEAD_DIM)
        parts.append(jnp.where(keep, blk, 0.0))
    return jnp.concatenate(parts, axis=0).astype(BF16)


def _swa_prompt_kernel(sink_ref, q_ref, kp_ref, ko_ref, vp_ref, vo_ref, o_ref):
    j = pl.program_id(1)
    R = WINDOW
    q = q_ref[...]
    kd = _dup_halves(jnp.concatenate([kp_ref[...], ko_ref[...]], axis=0))
    vd = _dup_halves(jnp.concatenate([vp_ref[...], vo_ref[...]], axis=0))
    qi = lax.broadcasted_iota(jnp.int32, (R, 2 * R), 0)
    kj = lax.broadcasted_iota(jnp.int32, (R, 2 * R), 1)
    dist = qi + R - kj
    valid = (dist >= 0) & (dist < WINDOW) & ((kj >= R) | (j > 0))
    distf = dist.astype(F32)
    lane = lax.broadcasted_iota(jnp.int32, (R, LANES), 1)
    outs = []
    for g in range(N_KV):
        qs = _stack_group_queries(q, g)
        s = lax.dot_general(qs, kd[g].astype(BF16), (((1,), (1,)), ((), ())),
                            preferred_element_type=F32) * (HEAD_DIM ** -0.5)
        ps = []
        for hh in range(GROUP):
            h = GROUP * g + hh
            sink = sink_ref[h]
            sh = jnp.where(valid, s[hh * R:(hh + 1) * R] - _alibi_slope(h) * distf, NEG)
            mx = jnp.maximum(jnp.max(sh, axis=1, keepdims=True), sink)
            p = jnp.exp(sh - mx)
            p = p / (jnp.sum(p, axis=1, keepdims=True) + jnp.exp(sink - mx))
            ps.append(p.astype(BF16))
        o = jnp.dot(jnp.concatenate(ps, axis=0), vd[g].astype(BF16), preferred_element_type=F32)
        for pp in range(GROUP // 2):
            outs.append(jnp.where(lane < HEAD_DIM, o[(2 * pp) * R:(2 * pp + 1) * R],
                                  o[(2 * pp + 1) * R:(2 * pp + 2) * R]))
    o_ref[...] = jnp.concatenate(outs, axis=1).astype(o_ref.dtype)


def _swa_prompt(sinks, qa, ka, va, batch, seq):
    nb = seq // WINDOW
    own = lambda w: pl.BlockSpec((WINDOW, w), lambda b, j: (b * nb + j, 0))
    prev = lambda w: pl.BlockSpec((WINDOW, w), lambda b, j: (b * nb + jnp.maximum(j - 1, 0), 0))
    return pl.pallas_call(
        _swa_prompt_kernel,
        grid=(batch, nb),
        in_specs=[pl.BlockSpec(memory_space=pltpu.SMEM), own(DM), prev(DKV), own(DKV), prev(DKV), own(DKV)],
        out_specs=own(DM),
        out_shape=jax.ShapeDtypeStruct((batch * seq, DM), BF16),
        compiler_params=_cparams(("arbitrary", "arbitrary"), VMEM_LIMIT),
        name="swa_prompt",
    )(sinks, qa, ka, ka, va, va)


def _swa_decode_kernel(n_tok, sink_ref, q_ref, kn_ref, vn_ref, kc_ref, vc_ref, o_ref, ko_ref, vo_ref):
    W = WINDOW
    Tn = n_tok
    qt = lax.broadcasted_iota(jnp.int32, (Tn, W), 0)
    kj = lax.broadcasted_iota(jnp.int32, (Tn, W), 1)
    dist_c = qt + W - kj
    valid_c = dist_c < WINDOW
    qt2 = lax.broadcasted_iota(jnp.int32, (Tn, Tn), 0)
    kj2 = lax.broadcasted_iota(jnp.int32, (Tn, Tn), 1)
    dist_n = qt2 - kj2
    valid_n = dist_n >= 0
    dcf = dist_c.astype(F32)
    dnf = dist_n.astype(F32)
    lane = lax.broadcasted_iota(jnp.int32, (Tn, LANES), 1)

    def body(b, carry):
        rows = pl.ds(pl.multiple_of(b * Tn, Tn), Tn)
        q = q_ref[rows, :]
        kn = kn_ref[rows, :]
        vn = vn_ref[rows, :]
        kc = kc_ref[b]
        vc = vc_ref[b]
        ko_ref[b, 0:W - Tn, :] = kc[Tn:W]
        ko_ref[b, W - Tn:W, :] = kn
        vo_ref[b, 0:W - Tn, :] = vc[Tn:W]
        vo_ref[b, W - Tn:W, :] = vn
        kcd, knd = _dup_halves(kc), _dup_halves(kn)
        vcd, vnd = _dup_halves(vc), _dup_halves(vn)
        outs = []
        for g in range(N_KV):
            qs = _stack_group_queries(q, g)
            nt = (((1,), (1,)), ((), ()))
            sc = lax.dot_general(qs, kcd[g].astype(BF16), nt, preferred_element_type=F32) * (HEAD_DIM ** -0.5)
            sn = lax.dot_general(qs, knd[g].astype(BF16), nt, preferred_element_type=F32) * (HEAD_DIM ** -0.5)
            pcs, pns = [], []
            for hh in range(GROUP):
                h = GROUP * g + hh
                sink = sink_ref[h]
                slope = _alibi_slope(h)
                shc = jnp.where(valid_c, sc[hh * Tn:(hh + 1) * Tn] - slope * dcf, NEG)
                shn = jnp.where(valid_n, sn[hh * Tn:(hh + 1) * Tn] - slope * dnf, NEG)
                mx = jnp.maximum(jnp.maximum(jnp.max(shc, axis=1, keepdims=True),
                                             jnp.max(shn, axis=1, keepdims=True)), sink)
                pc = jnp.exp(shc - mx)
                pn = jnp.exp(shn - mx)
                inv = 1.0 / (jnp.sum(pc, axis=1, keepdims=True) + jnp.sum(pn, axis=1, keepdims=True)
                             + jnp.exp(sink - mx))
                pcs.append((pc * inv).astype(BF16))
                pns.append((pn * inv).astype(BF16))
            o = (jnp.dot(jnp.concatenate(pcs, axis=0), vcd[g].astype(BF16), preferred_element_type=F32)
                 + jnp.dot(jnp.concatenate(pns, axis=0), vnd[g].astype(BF16), preferred_element_type=F32))
            for pp in range(GROUP // 2):
                outs.append(jnp.where(lane < HEAD_DIM, o[(2 * pp) * Tn:(2 * pp + 1) * Tn],
                                      o[(2 * pp + 1) * Tn:(2 * pp + 2) * Tn]))
        o_ref[rows, :] = jnp.concatenate(outs, axis=1)
        return carry

    lax.fori_loop(0, DEC_G, body, 0, unroll=DEC_UNROLL)


def _swa_decode(sinks, qa, ka, va, kc, vc, row0, n_tok):
    nb = kc.shape[0]
    blk0 = row0 // (DEC_G * n_tok)
    tokb = lambda w: pl.BlockSpec((DEC_G * n_tok, w), lambda i: (blk0 + i, 0))
    cache = pl.BlockSpec((DEC_G, WINDOW, DKV), lambda i: (i, 0, 0))
    return pl.pallas_call(
        functools.partial(_swa_decode_kernel, n_tok),
        grid=(nb // DEC_G,),
        in_specs=[pl.BlockSpec(memory_space=pltpu.SMEM), tokb(DM), tokb(DKV), tokb(DKV), cache, cache],
        out_specs=(pl.BlockSpec((DEC_G * n_tok, DM), lambda i: (i, 0)), cache, cache),
        out_shape=(jax.ShapeDtypeStruct((nb * n_tok, DM), F32),
                   jax.ShapeDtypeStruct(kc.shape, F32), jax.ShapeDtypeStruct(vc.shape, F32)),
        compiler_params=_cparams(("arbitrary",), VMEM_LIMIT),
        name="swa_decode",
    )(sinks, qa, ka, va, kc, vc)


def _outproj_router_kernel(n_ptiles, xp_ref, xs_ref, hmp_ref, hap_ref, hms_ref, has_ref, wom_ref, woa_ref,
                           g_ref, wrt_ref, br_ref,
                           x1_ref, xg_ref, meta_ref, cnt_ref):
    i = pl.program_id(0)
    is_p = i < n_ptiles
    x = jnp.where(is_p, xp_ref[...], xs_ref[...])
    hm = jnp.where(is_p, hmp_ref[...], hms_ref[...].astype(BF16))
    ha = jnp.where(is_p, hap_ref[...], has_ref[...].astype(BF16))
    x1 = (x + jnp.dot(hm, wom_ref[...], preferred_element_type=F32)
          + jnp.dot(ha, woa_ref[...], preferred_element_type=F32))
    x1_ref[...] = x1
    h2 = _rms(x1, g_ref[...])
    logits = lax.dot_general(wrt_ref[...], h2, (((1,), (1,)), ((), ())), precision=HIGHEST,
                             preferred_element_type=F32) + br_ref[...]
    eidx = lax.broadcasted_iota(jnp.int32, logits.shape, 0).astype(F32)
    work = logits
    vals, hots = [], []
    for _ in range(TOP_K):
        mv = jnp.max(work, axis=0, keepdims=True)
        sel = jnp.min(jnp.where(work == mv, eidx, float(N_EXPERTS)), axis=0, keepdims=True)
        hot = eidx == sel
        vals.append(mv)
        hots.append(hot)
        work = jnp.where(hot, -jnp.inf, work)
    es = [jnp.exp(v - vals[0]) for v in vals]
    tot = es[0] + es[1] + es[2] + es[3]
    gates = [e / tot for e in es]
    hot_all = jnp.where(hots[0] | hots[1] | hots[2] | hots[3], 1.0, 0.0)
    tm = logits.shape[1]
    su = (lax.broadcasted_iota(jnp.int32, (tm, tm), 0) < lax.broadcasted_iota(jnp.int32, (tm, tm), 1))
    cum = jnp.dot(hot_all.astype(BF16), su.astype(BF16), preferred_element_type=F32)
    cnt = jnp.sum(hot_all, axis=1, keepdims=True)
    cpad = (((cnt.astype(jnp.int32) + (SUBLANES - 1)) // SUBLANES) * SUBLANES).astype(F32)
    lower = (lax.broadcasted_iota(jnp.int32, (N_EXPERTS, N_EXPERTS), 0)
             > lax.broadcasted_iota(jnp.int32, (N_EXPERTS, N_EXPERTS), 1)).astype(F32)
    lstart = jnp.dot(lower, jnp.broadcast_to(cpad, (N_EXPERTS, LANES)), precision=HIGHEST,
                     preferred_element_type=F32)[:, 0:1]
    base = lstart + cum
    lpos = [jnp.sum(jnp.where(hot, base, 0.0), axis=0, keepdims=True) for hot in hots]
    lpi = [p.astype(jnp.int32) for p in lpos]
    r_iota = lax.broadcasted_iota(jnp.int32, (GROUP_R, tm), 0)
    sel01 = jnp.where(r_iota == lpi[0], 1.0, jnp.where(r_iota == lpi[1], 1.0, jnp.where(
        r_iota == lpi[2], 1.0, jnp.where(r_iota == lpi[3], 1.0, 0.0)))).astype(BF16)
    xg_ref[...] = jnp.dot(sel01, h2.astype(BF16), preferred_element_type=F32)
    meta_ref[...] = jnp.transpose(jnp.concatenate(gates + lpos, axis=0))
    cnt_ref[0] = jnp.broadcast_to(cnt, (N_EXPERTS, LANES))


def _outproj_router(xp, xs, hmp, hap, hms, has, wom, woa, g_ffn, wrt, br):
    tp, ts = xp.shape[0], xs.shape[0]
    n_pt, n_st = tp // TM, ts // TM
    t_all = tp + ts
    pblk = lambda w: pl.BlockSpec((TM, w), lambda i: (jnp.minimum(i, n_pt - 1), 0))
    sblk = lambda w: pl.BlockSpec((TM, w), lambda i: (jnp.maximum(i - n_pt, 0), 0))
    full = lambda a: pl.BlockSpec(a.shape, lambda i: (0,) * a.ndim)
    return pl.pallas_call(
        functools.partial(_outproj_router_kernel, n_pt),
        grid=(n_pt + n_st,),
        in_specs=[pblk(D_MODEL), sblk(D_MODEL), pblk(DM), pblk(DM), sblk(DM), sblk(DM),
                  full(wom), full(woa), full(g_ffn), full(wrt), full(br)],
        out_specs=(pl.BlockSpec((TM, D_MODEL), lambda i: (i, 0)),
                   pl.BlockSpec((GROUP_R, D_MODEL), lambda i: (i, 0)),
                   pl.BlockSpec((TM, 2 * TOP_K), lambda i: (i, 0)),
                   pl.BlockSpec((1, N_EXPERTS, LANES), lambda i: (i, 0, 0))),
        out_shape=(jax.ShapeDtypeStruct((t_all, D_MODEL), F32),
                   jax.ShapeDtypeStruct(((n_pt + n_st) * GROUP_R, D_MODEL), F32),
                   jax.ShapeDtypeStruct((t_all, 2 * TOP_K), F32),
                   jax.ShapeDtypeStruct((n_pt + n_st, N_EXPERTS, LANES), F32)),
        compiler_params=_cparams(("arbitrary",), VMEM_LIMIT),
        name="outproj_router",
    )(xp, xs, hmp, hap, hms, has, wom, woa, g_ffn, wrt, br)


def _expert_kernel(be_ref, na_ref, slot_ref, nxt_ref, ctab_ref, ctab_next_ref, xg_ref, wgu_ref, bgu_ref,
                   wd_ref, bd_ref, o_ref, xbuf, wgu_f, wd_f, wgu_s, wd_s, xsem, wsem):
    i = pl.program_id(0)
    na = na_ref[0]

    def x_copies(tab_ref, slot):
        return [pltpu.make_async_copy(xg_ref.at[pl.ds(pl.multiple_of(tab_ref[0, 0, c], SUBLANES), SUBLANES)],
                                      xbuf.at[slot, pl.ds(c * SUBLANES, SUBLANES)], xsem.at[slot])
                for c in range(MOE_BM // SUBLANES)]

    def w_copies(e, slot):
        return [pltpu.make_async_copy(wgu_ref.at[e], wgu_f.at[slot], wsem.at[slot]),
                pltpu.make_async_copy(wd_ref.at[e], wd_f.at[slot], wsem.at[slot])]

    @pl.when(i == 0)
    def _():
        for cp in x_copies(ctab_ref, 0) + w_copies(be_ref[0], 0):
            cp.start()

    @pl.when(i < na)
    def _():
        changed = (i == 0) | (be_ref[i] != be_ref[jnp.maximum(i - 1, 0)])
        wslot = slot_ref[i]
        xslot = lax.rem(i, 2)

        @pl.when(i + 1 < na)
        def _():
            for cp in x_copies(ctab_next_ref, 1 - xslot):
                cp.start()

        @pl.when(changed)
        def _():
            for cp in w_copies(be_ref[i], wslot):
                cp.wait()

            @pl.when(nxt_ref[i] >= 0)
            def _():
                for cp in w_copies(nxt_ref[i], 1 - wslot):
                    cp.start(priority=1)

            wgu_s[...] = wgu_f[wslot].astype(BF16)
            wd_s[...] = wd_f[wslot].astype(BF16)

        for cp in x_copies(ctab_ref, xslot):
            cp.wait()

        x = xbuf[xslot].astype(BF16)
        acts = []
        for jt in range(D_FF // FF_TILE):
            gs = slice(jt * FF_TILE, (jt + 1) * FF_TILE)
            us = slice(D_FF + jt * FF_TILE, D_FF + (jt + 1) * FF_TILE)
            gate = jnp.dot(x, wgu_s[:, gs], preferred_element_type=F32) + bgu_ref[:, gs]
            up = jnp.dot(x, wgu_s[:, us], preferred_element_type=F32) + bgu_ref[:, us]
            gate = jnp.minimum(gate, SWIGLU_LIMIT)
            up = jnp.clip(up, -SWIGLU_LIMIT, SWIGLU_LIMIT)
            acts.append(((up + 1.0) * (gate * _sigmoid(gate * SWIGLU_ALPHA))).astype(BF16))
        o_ref[...] = jnp.dot(jnp.concatenate(acts, axis=1), wd_s[...], preferred_element_type=F32) + bd_ref[...]


def _expert_ffn(block_exp, n_active, wslot, next_exp, ctab, xg, wgu, bgu, wd, bd):
    n_blocks = ctab.shape[0]
    n_rows = n_blocks * MOE_BM
    nch = MOE_BM // SUBLANES
    grid_spec = pltpu.PrefetchScalarGridSpec(
        num_scalar_prefetch=4,
        grid=(n_blocks,),
        in_specs=[pl.BlockSpec((1, 1, nch), lambda i, *_: (i, 0, 0), memory_space=pltpu.SMEM),
                  pl.BlockSpec((1, 1, nch), lambda i, *_: (jnp.minimum(i + 1, n_blocks - 1), 0, 0),
                               memory_space=pltpu.SMEM),
                  pl.BlockSpec(memory_space=pl.ANY),
                  pl.BlockSpec(memory_space=pl.ANY),
                  pl.BlockSpec((None, 1, 2 * D_FF), lambda i, be, *_: (be[i], 0, 0)),
                  pl.BlockSpec(memory_space=pl.ANY),
                  pl.BlockSpec((None, 1, D_MODEL), lambda i, be, *_: (be[i], 0, 0))],
        out_specs=pl.BlockSpec((MOE_BM, D_MODEL), lambda i, be, na, *_: (jnp.minimum(i, na[0] - 1), 0)),
        scratch_shapes=[pltpu.VMEM((2, MOE_BM, D_MODEL), F32),
                        pltpu.VMEM((2, D_MODEL, 2 * D_FF), F32), pltpu.VMEM((2, D_FF, D_MODEL), F32),
                        pltpu.VMEM((D_MODEL, 2 * D_FF), BF16), pltpu.VMEM((D_FF, D_MODEL), BF16),
                        pltpu.SemaphoreType.DMA((2,)), pltpu.SemaphoreType.DMA((2,))],
    )
    return pl.pallas_call(
        _expert_kernel,
        grid_spec=grid_spec,
        out_shape=jax.ShapeDtypeStruct((n_rows, D_MODEL), F32),
        compiler_params=_cparams(("arbitrary",), VMEM_LIMIT),
        name="moe_experts",
    )(block_exp, n_active, wslot, next_exp, ctab, ctab, xg, wgu, bgu, wd, bd)


def _combine_kernel(n_ptiles, ctab_ref, ctab_next_ref, outs_ref, x1_ref, meta_ref, gf_ref, yp_ref, ys_ref,
                    obuf, sem):
    i = pl.program_id(0)
    n = pl.num_programs(0)
    slot = lax.rem(i, 2)

    def copies(tab_ref, s):
        return [pltpu.make_async_copy(outs_ref.at[pl.ds(pl.multiple_of(tab_ref[0, 0, c], SUBLANES), SUBLANES)],
                                      obuf.at[s, pl.ds(c * SUBLANES, SUBLANES)], sem.at[s])
                for c in range(GROUP_R // SUBLANES)]

    @pl.when(i == 0)
    def _():
        for cp in copies(ctab_ref, 0):
            cp.start()

    @pl.when(i + 1 < n)
    def _():
        for cp in copies(ctab_next_ref, 1 - slot):
            cp.start()

    for cp in copies(ctab_ref, slot):
        cp.wait()

    meta = meta_ref[...]
    tm = meta.shape[0]
    r_iota = lax.broadcasted_iota(jnp.int32, (tm, GROUP_R), 1)
    lp = [meta[:, TOP_K + k:TOP_K + k + 1].astype(jnp.int32) for k in range(TOP_K)]
    gk = [meta[:, k:k + 1] for k in range(TOP_K)]
    gsel = jnp.where(r_iota == lp[0], gk[0], jnp.where(r_iota == lp[1], gk[1], jnp.where(
        r_iota == lp[2], gk[2], jnp.where(r_iota == lp[3], gk[3], 0.0))))
    sel01 = jnp.where(gsel != 0.0, 1.0, 0.0).astype(BF16)
    rg_row = jnp.sum(gsel, axis=0, keepdims=True)
    rg_col = jnp.transpose(jnp.broadcast_to(rg_row, (SUBLANES, GROUP_R)))[:, 0:1]
    og = (obuf[slot] * rg_col).astype(BF16)
    acc = x1_ref[...] + jnp.dot(sel01, og, preferred_element_type=F32)
    y = _rms(acc, gf_ref[...])

    @pl.when(i < n_ptiles)
    def _():
        yp_ref[...] = y

    @pl.when(i >= n_ptiles)
    def _():
        ys_ref[...] = y


def _combine(ctab, outs, x1, meta, g_final, tp, ts):
    n_pt, n_st = tp // TM, ts // TM
    n = n_pt + n_st
    nch = GROUP_R // SUBLANES
    return pl.pallas_call(
        functools.partial(_combine_kernel, n_pt),
        grid=(n,),
        in_specs=[pl.BlockSpec((1, 1, nch), lambda i: (i, 0, 0), memory_space=pltpu.SMEM),
                  pl.BlockSpec((1, 1, nch), lambda i: (jnp.minimum(i + 1, n - 1), 0, 0), memory_space=pltpu.SMEM),
                  pl.BlockSpec(memory_space=pl.ANY),
                  pl.BlockSpec((TM, D_MODEL), lambda i: (i, 0)),
                  pl.BlockSpec((TM, 2 * TOP_K), lambda i: (i, 0)),
                  pl.BlockSpec((1, D_MODEL), lambda i: (0, 0))],
        out_specs=(pl.BlockSpec((TM, D_MODEL), lambda i: (jnp.minimum(i, n_pt - 1), 0)),
                   pl.BlockSpec((TM, D_MODEL), lambda i: (jnp.maximum(i - n_pt, 0), 0))),
        out_shape=(jax.ShapeDtypeStruct((tp, D_MODEL), F32), jax.ShapeDtypeStruct((ts, D_MODEL), F32)),
        scratch_shapes=[pltpu.VMEM((2, GROUP_R, D_MODEL), F32), pltpu.SemaphoreType.DMA((2,))],
        compiler_params=_cparams(("arbitrary",), VMEM_LIMIT),
        name="moe_combine",
    )(ctab, ctab, outs, x1, meta, g_final)


def kernel(x_prompt, x_sample, cache_swa_k, cache_swa_v, state_mlstm_c, state_mlstm_n, state_mlstm_m,
           g_mix, w_in, b_igate, b_fgate, g_head, attn_sinks, w_out, g_ffn, w_router, b_router,
           w_gate_up, b_gate_up, w_down, b_down, g_final):
    assert w_in.shape[0] == 1, "single-layer problem"
    B, S, _ = x_prompt.shape
    Bd, Tn, _ = x_sample.shape
    tp, ts = B * S, Bd * Tn
    t_all = tp + ts
    xp = x_prompt.reshape(tp, D_MODEL)
    xs = x_sample.reshape(ts, D_MODEL)

    w = w_in[0]
    o = np.cumsum([0, DM, DM, DM, DM, N_HEADS, N_HEADS, DM, DKV, DKV])
    col = lambda a: w[:, int(o[a]):int(o[a + 1])]
    w1 = jnp.concatenate([col(0), col(2), col(3), col(6), col(7), col(8)], axis=1).astype(BF16)
    wkt = col(1).T.astype(BF16)
    wgates = jnp.concatenate([col(4), col(5)], axis=1)
    wg = jnp.pad(wgates, ((0, 0), (0, LANES - 2 * N_HEADS))).astype(BF16)
    wgt = wgates.T.astype(BF16)
    bg = jnp.concatenate([b_igate[0], b_fgate[0]]).astype(F32)
    bcol = jnp.pad(bg, (0, LANES - 2 * N_HEADS)).reshape(1, LANES)
    brow = bg.reshape(2 * N_HEADS, 1)

    qm, vm, om, qa, ka, va, kmt, gcol, grow = _inproj(xp, xs, g_mix[0].reshape(1, D_MODEL), w1, wkt, wg, wgt,
                                                      bcol, brow)

    gh = g_head[0].astype(F32)
    sinks = attn_sinks[0].astype(F32)

    hm_p, cbd, m_p = _mlstm_prompt(qm, kmt, vm, om, gcol, grow, gh.reshape(1, DM), B, S)
    ha_p = _swa_prompt(sinks, qa, ka, va, B, S)

    hm_s, c_s, n_s, mt_s = _mlstm_sample(qm, kmt, vm, om, gcol, grow, state_mlstm_c[0], state_mlstm_n[0],
                                         state_mlstm_m[0], gh.reshape(1, DM), tp, Tn)
    n_s = n_s.reshape(Bd, N_HEADS, HEAD_DIM)
    m_s = mt_s[:, :, :LANES // Tn].transpose(0, 2, 1).reshape(Bd, N_HEADS)
    ha_s, k_s, v_s = _swa_decode(sinks, qa, ka, va, cache_swa_k[0].reshape(Bd, WINDOW, DKV),
                                 cache_swa_v[0].reshape(Bd, WINDOW, DKV), tp, Tn)

    wo = w_out[0].astype(BF16)
    x1, xg, meta, cnt = _outproj_router(
        xp, xs, hm_p, ha_p, hm_s, ha_s, wo[:DM], wo[DM:], g_ffn[0].reshape(1, D_MODEL),
        w_router[0].T, b_router[0].reshape(N_EXPERTS, 1))

    i32 = jnp.int32
    n_tiles = t_all // TM
    max_rows = t_all * TOP_K + n_tiles * N_EXPERTS * (SUBLANES - 1) + N_EXPERTS * (MOE_BM - 1)
    n_blocks = -(-max_rows // MOE_BM)
    cpad = (cnt[:, :, 0].astype(i32) + (SUBLANES - 1)) // SUBLANES * SUBLANES
    lstart = jnp.cumsum(cpad, axis=1) - cpad
    goff = jnp.cumsum(cpad, axis=0) - cpad
    padded = (jnp.sum(cpad, axis=0) + MOE_BM - 1) // MOE_BM * MOE_BM
    pad_end = jnp.cumsum(padded)
    seg_begin = (pad_end - padded)[None, :] + goff
    n_active = (pad_end[-1] // MOE_BM).astype(i32)
    blk = jnp.minimum(jnp.arange(n_blocks, dtype=i32), n_active - 1)
    block_exp = jnp.minimum(jnp.sum((pad_end[None, :] <= (blk * MOE_BM)[:, None]).astype(i32), axis=1),
                            N_EXPERTS - 1)
    e_ids = jnp.arange(N_EXPERTS, dtype=i32)
    nonempty = padded > 0
    nxt_e = jnp.min(jnp.where((e_ids[None, :] > e_ids[:, None]) & nonempty[None, :], e_ids[None, :], N_EXPERTS),
                    axis=1)
    nxt_e = jnp.where(nxt_e == N_EXPERTS, -1, nxt_e)
    ord_e = jnp.cumsum(nonempty.astype(i32)) - 1
    be_hot = block_exp[:, None] == e_ids[None, :]
    next_exp = jnp.sum(jnp.where(be_hot, nxt_e[None, :], 0), axis=1).astype(i32)
    wslot = (jnp.sum(jnp.where(be_hot, ord_e[None, :], 0), axis=1) % 2).astype(i32)

    seg_src = jnp.arange(n_tiles, dtype=i32)[:, None] * GROUP_R + lstart
    sb, sl, ss = seg_begin.reshape(-1), cpad.reshape(-1), seg_src.reshape(-1)
    rc = jnp.arange(n_blocks * MOE_BM // SUBLANES, dtype=i32)[:, None] * SUBLANES
    inseg = (sb[None, :] <= rc) & (rc < (sb + sl)[None, :])
    ctab_e = jnp.where(jnp.any(inseg, axis=1), jnp.sum(jnp.where(inseg, (ss - sb)[None, :] + rc, 0), axis=1),
                       GROUP_R - SUBLANES)
    lr = jnp.arange(GROUP_R // SUBLANES, dtype=i32)[None, :, None] * SUBLANES
    inl = (lstart[:, None, :] <= lr) & (lr < (lstart + cpad)[:, None, :])
    ctab_c = jnp.sum(jnp.where(inl, (seg_begin - lstart)[:, None, :] + lr, 0), axis=2)

    outs = _expert_ffn(block_exp, n_active.reshape(1), wslot, next_exp,
                       ctab_e.astype(i32).reshape(n_blocks, 1, MOE_BM // SUBLANES), xg, w_gate_up[0],
                       b_gate_up[0].reshape(N_EXPERTS, 1, 2 * D_FF), w_down[0],
                       b_down[0].reshape(N_EXPERTS, 1, D_MODEL))
    y_p, y_s = _combine(ctab_c.astype(i32).reshape(n_tiles, 1, GROUP_R // SUBLANES), outs, x1, meta,
                        g_final.reshape(1, D_MODEL), tp, ts)

    kv_tail = lambda a: a[:tp].reshape(B, S, N_KV, HEAD_DIM)[:, S - WINDOW:][None]
    c_e = cbd[:, :, :HEAD_DIM, :HEAD_DIM]
    c_o = cbd[:, :, HEAD_DIM:, HEAD_DIM:LANES]
    c_p = jnp.stack([c_e, c_o], axis=2).reshape(B, N_HEADS, HEAD_DIM, HEAD_DIM)
    n_p = jnp.stack([cbd[:, :, :HEAD_DIM, LANES], cbd[:, :, HEAD_DIM:, LANES + 1]], axis=2).reshape(B, N_HEADS, HEAD_DIM)
    return (y_p.reshape(B, S, D_MODEL), y_s.reshape(Bd, Tn, D_MODEL),
            kv_tail(ka), kv_tail(va), c_p[None], n_p[None], m_p[:, :, 0][None],
            k_s.reshape(Bd, WINDOW, N_KV, HEAD_DIM)[None], v_s.reshape(Bd, WINDOW, N_KV, HEAD_DIM)[None],
            c_s[None], n_s[None], m_s[None])
```

```python
import functools
import math

import jax
import jax.numpy as jnp
import numpy as np
from jax import lax
from jax.experimental import pallas as pl
from jax.experimental.pallas import tpu as pltpu

F32 = jnp.float32
BF16 = jnp.bfloat16
HIGHEST = lax.Precision.HIGHEST

D_MODEL = 1024
HEAD_DIM = 64
N_HEADS = 8
N_PAIRS = N_HEADS // 2
N_KV = 2
GROUP = N_HEADS // N_KV
WINDOW = 128
N_EXPERTS = 32
TOP_K = 4
D_FF = 1024
SWIGLU_LIMIT = 7.0
SWIGLU_ALPHA = 1.702
RMS_EPS = 1e-5
DM = N_HEADS * HEAD_DIM
DKV = N_KV * HEAD_DIM
NEG = -1e30

LANES = 128
SUBLANES = 8
VMEM_LIMIT = 56 * 1024 * 1024

TM = 512
MLSTM_TL = 512
MLSTM_L = 128
DEC_G = 8
DEC_UNROLL = 4
MOE_BM = 256
GROUP_R = -(-(TOP_K * TM + N_EXPERTS * (SUBLANES - 1) + SUBLANES) // LANES) * LANES


def _cparams(sem, vmem=None):
    return pltpu.CompilerParams(dimension_semantics=sem, vmem_limit_bytes=vmem)


def _rms(x, g):
    return x * lax.rsqrt(jnp.mean(x * x, axis=-1, keepdims=True) + RMS_EPS) * g


def _log_sigmoid(z):
    return jnp.minimum(z, 0.0) - jnp.log(1.0 + jnp.exp(-jnp.abs(z)))


def _sigmoid(z):
    return 1.0 / (1.0 + jnp.exp(-z))


def _inproj_kernel(n_ptiles, xp_ref, xs_ref, g_ref, w1_ref, wkt_ref, wg_ref, wgt_ref, bcol_ref, brow_ref,
                   qm_ref, vm_ref, om_ref, qa_ref, ka_ref, va_ref, kmt_ref, gcol_ref, grow_ref):
    i = pl.program_id(0)
    x = jnp.where(i < n_ptiles, xp_ref[...], xs_ref[...])
    h = _rms(x, g_ref[...]).astype(BF16)
    main = jnp.dot(h, w1_ref[...], preferred_element_type=F32)
    qm_ref[...] = main[:, 0:DM]
    vm_ref[...] = main[:, DM:2 * DM]
    om_ref[...] = main[:, 2 * DM:3 * DM]
    qa_ref[...] = main[:, 3 * DM:4 * DM]
    ka_ref[...] = main[:, 4 * DM:4 * DM + DKV]
    va_ref[...] = main[:, 4 * DM + DKV:4 * DM + 2 * DKV]
    kt = lax.dot_general(wkt_ref[...], h, (((1,), (1,)), ((), ())), preferred_element_type=F32)
    kmt_ref[...] = kt * (HEAD_DIM ** -0.5)
    zc = jnp.dot(h, wg_ref[...], preferred_element_type=F32) + bcol_ref[...]
    lane = lax.broadcasted_iota(jnp.int32, zc.shape, 1)
    gcol_ref[...] = jnp.where(lane < N_HEADS, zc, _log_sigmoid(zc))
    zr = lax.dot_general(wgt_ref[...], h, (((1,), (1,)), ((), ())), preferred_element_type=F32) + brow_ref[...]
    row = lax.broadcasted_iota(jnp.int32, zr.shape, 0)
    grow_ref[...] = jnp.where(row < N_HEADS, zr, _log_sigmoid(zr))


def _inproj(xp, xs, g_mix, w1, wkt, wg, wgt, bcol, brow):
    tp, ts = xp.shape[0], xs.shape[0]
    n_pt, n_st = tp // TM, ts // TM
    t_all = tp + ts
    n1 = w1.shape[1]
    tok = lambda w: pl.BlockSpec((TM, w), lambda i: (i, 0))
    full = lambda a: pl.BlockSpec(a.shape, lambda i: (0,) * a.ndim)
    out_shape = (
        jax.ShapeDtypeStruct((t_all, DM), F32), jax.ShapeDtypeStruct((t_all, DM), F32),
        jax.ShapeDtypeStruct((t_all, DM), F32), jax.ShapeDtypeStruct((t_all, DM), F32),
        jax.ShapeDtypeStruct((t_all, DKV), F32), jax.ShapeDtypeStruct((t_all, DKV), F32),
        jax.ShapeDtypeStruct((DM, t_all), F32),
        jax.ShapeDtypeStruct((t_all, LANES), F32), jax.ShapeDtypeStruct((2 * N_HEADS, t_all), F32),
    )
    return pl.pallas_call(
        functools.partial(_inproj_kernel, n_pt),
        grid=(n_pt + n_st,),
        in_specs=[
            pl.BlockSpec((TM, D_MODEL), lambda i: (jnp.minimum(i, n_pt - 1), 0)),
            pl.BlockSpec((TM, D_MODEL), lambda i: (jnp.maximum(i - n_pt, 0), 0)),
            full(g_mix), full(w1), full(wkt), full(wg), full(wgt), full(bcol), full(brow),
        ],
        out_specs=(tok(DM), tok(DM), tok(DM), tok(DM), tok(DKV), tok(DKV),
                   pl.BlockSpec((DM, TM), lambda i: (0, i)),
                   tok(LANES), pl.BlockSpec((2 * N_HEADS, TM), lambda i: (0, i))),
        out_shape=out_shape,
        compiler_params=_cparams(("arbitrary",), VMEM_LIMIT),
        name="inproj",
    )(xp, xs, g_mix, w1, wkt, wg, wgt, bcol, brow)


def _cumsum_rows(x, n):
    row = lax.broadcasted_iota(jnp.int32, x.shape, 0)
    sh = 1
    while sh < n:
        x = x + jnp.where(row >= sh, pltpu.roll(x, sh, axis=0), 0.0)
        sh *= 2
    return x


def _mlstm_prompt_kernel(qm_ref, kmt_ref, vm_ref, om_ref, gcol_ref, grow_ref, gh_ref,
                         hm_ref, cbd_ref, m_ref, cbd_s, m_s):
    j = pl.program_id(1)
    L = MLSTM_L

    @pl.when(j == 0)
    def _():
        cbd_s[...] = jnp.zeros_like(cbd_s)
        m_s[...] = jnp.zeros_like(m_s)

    ti = lax.broadcasted_iota(jnp.int32, (L, L), 0)
    si = lax.broadcasted_iota(jnp.int32, (L, L), 1)
    causal = ti >= si
    upper = (ti <= si).astype(F32)
    lane128 = lax.broadcasted_iota(jnp.int32, (L, LANES), 1)
    even128 = lane128 < HEAD_DIM
    lane256 = lax.broadcasted_iota(jnp.int32, (1, 2 * LANES), 1)
    cols_e = (lane256 < HEAD_DIM) | (lane256 == LANES)
    cols_o = ((lane256 >= HEAD_DIM) & (lane256 < LANES)) | (lane256 == LANES + 1)
    rowk = lax.broadcasted_iota(jnp.int32, (LANES, 1), 0)
    rows_e = rowk < HEAD_DIM
    bd_mask = (rows_e & cols_e) | ((~rows_e) & cols_o)
    ones_cols = ((lax.broadcasted_iota(jnp.int32, (L, LANES), 1) < 2)).astype(F32)
    bo_r = lax.broadcasted_iota(jnp.int32, (LANES, LANES), 0) // HEAD_DIM
    bo_c = lax.broadcasted_iota(jnp.int32, (LANES, LANES), 1) // HEAD_DIM
    block_ones = (bo_r == bo_c).astype(F32)

    for c in range(MLSTM_TL // L):
        sl = slice(c * L, (c + 1) * L)
        grow = grow_ref[:, sl]
        i_row = grow[0:N_HEADS]
        b_row = jnp.dot(grow[N_HEADS:2 * N_HEADS], upper, precision=HIGHEST,
                        preferred_element_type=F32)
        b_colall = _cumsum_rows(gcol_ref[sl, :], L)
        rowv_all = i_row - b_row
        for p in range(N_PAIRS):
            ls = slice(p * LANES, (p + 1) * LANES)
            q2 = qm_ref[sl, ls].astype(BF16)
            kt2 = kmt_ref[ls, sl]
            v2 = vm_ref[sl, ls]
            vext = jnp.concatenate([v2, ones_cols], axis=1)
            kt_e = jnp.where(rows_e, kt2, 0.0).astype(BF16)
            kt_o = jnp.where(rows_e, 0.0, kt2).astype(BF16)
            s2 = jnp.dot(q2, jnp.concatenate([kt_e, kt_o], axis=1), preferred_element_type=F32)
            cbd = cbd_s[p]
            rq = jnp.dot(q2, cbd.astype(BF16), preferred_element_type=F32)
            ps, mts, inters, wrows, decays, mnews = [], [], [], [], [], []
            for hh in range(2):
                h = 2 * p + hh
                bcol = b_colall[:, N_HEADS + h:N_HEADS + h + 1]
                logd = jnp.where(causal, bcol + rowv_all[h:h + 1, :], NEG)
                m_prev = m_s[h:h + 1, 0:1]
                m_inter = m_prev + bcol
                m_t = jnp.maximum(m_inter, jnp.max(logd, axis=1, keepdims=True))
                d = jnp.exp(logd - m_t)
                ps.append((s2[:, hh * L:(hh + 1) * L] * d).astype(BF16))
                mts.append(m_t)
                inters.append(jnp.exp(m_inter - m_t))
                m_new = m_t[L - 1:L, :]
                b_last = b_row[h:h + 1, L - 1:L]
                decays.append(jnp.exp(m_prev + b_last - m_new))
                wrows.append(jnp.exp(b_last - b_row[h:h + 1, :] + i_row[h:h + 1, :] - m_new))
                mnews.append(m_new)
            vstack = jnp.concatenate([jnp.where(cols_e, vext, 0.0), jnp.where(cols_o, vext, 0.0)],
                                     axis=0).astype(BF16)
            r = jnp.dot(jnp.concatenate(ps, axis=1), vstack, preferred_element_type=F32)
            nd = r + jnp.where(cols_e, inters[0], inters[1]) * rq
            num = nd[:, 0:LANES]
            den = jnp.where(even128, nd[:, LANES:LANES + 1], nd[:, LANES + 1:LANES + 2])
            mt2 = jnp.where(even128, mts[0], mts[1])
            hv = num / jnp.maximum(jnp.abs(den), jnp.exp(-mt2))
            ms = jnp.dot(hv * hv, block_ones, precision=HIGHEST, preferred_element_type=F32) * (1.0 / HEAD_DIM)
            y = hv * lax.rsqrt(ms + RMS_EPS) * gh_ref[:, ls] * _sigmoid(om_ref[sl, ls])
            hm_ref[sl, ls] = y.astype(hm_ref.dtype)
            w2 = jnp.where(rows_e, wrows[0], wrows[1])
            upd = jnp.dot((kt2 * w2).astype(BF16), vext.astype(BF16), preferred_element_type=F32)
            dec2 = jnp.where(rows_e, decays[0], decays[1])
            cbd_s[p] = dec2 * cbd + jnp.where(bd_mask, upd, 0.0)
            for hh in range(2):
                h = 2 * p + hh
                m_s[h:h + 1, :] = jnp.broadcast_to(mnews[hh], (1, LANES))

    @pl.when(j == pl.num_programs(1) - 1)
    def _():
        cbd_ref[0] = cbd_s[...]
        m_ref[0] = m_s[...]


def _mlstm_prompt(qm, kmt, vm, om, gcol, grow, gh, batch, seq):
    nt = seq // MLSTM_TL
    tokb = lambda w: pl.BlockSpec((MLSTM_TL, w), lambda b, j: (b * nt + j, 0))
    rowb = lambda r: pl.BlockSpec((r, MLSTM_TL), lambda b, j: (0, b * nt + j))
    return pl.pallas_call(
        _mlstm_prompt_kernel,
        grid=(batch, nt),
        in_specs=[tokb(DM), rowb(DM), tokb(DM), tokb(DM), tokb(LANES), rowb(2 * N_HEADS),
                  pl.BlockSpec((1, DM), lambda b, j: (0, 0))],
        out_specs=(tokb(DM),
                   pl.BlockSpec((1, N_PAIRS, LANES, 2 * LANES), lambda b, j: (b, 0, 0, 0)),
                   pl.BlockSpec((1, N_HEADS, LANES), lambda b, j: (b, 0, 0))),
        out_shape=(jax.ShapeDtypeStruct((batch * seq, DM), BF16),
                   jax.ShapeDtypeStruct((batch, N_PAIRS, LANES, 2 * LANES), F32),
                   jax.ShapeDtypeStruct((batch, N_HEADS, LANES), F32)),
        scratch_shapes=[pltpu.VMEM((N_PAIRS, LANES, 2 * LANES), F32), pltpu.VMEM((N_HEADS, LANES), F32)],
        compiler_params=_cparams(("arbitrary", "arbitrary"), VMEM_LIMIT),
        name="mlstm_prompt",
    )(qm, kmt, vm, om, gcol, grow, gh)


def _mlstm_sample_kernel(n_tok, qm_ref, kmt_ref, vm_ref, om_ref, gcol_ref, grow_ref, c0_ref, n0_ref, m0_ref,
                         m0t_ref, gh_ref, hm_ref, c_ref, n_ref, mt_ref):
    L = LANES
    NB = L // n_tok
    ti = lax.broadcasted_iota(jnp.int32, (L, L), 0)
    si = lax.broadcasted_iota(jnp.int32, (L, L), 1)
    same = (ti // n_tok) == (si // n_tok)
    causal = same & (ti >= si)
    useg = jnp.where(same & (ti <= si), 1.0, 0.0)
    slast = jnp.where(same & (ti % n_tok == n_tok - 1), 1.0, 0.0)
    expand = jnp.where(ti // n_tok == si, 1.0, 0.0)
    expand_t = jnp.where(ti == si // n_tok, 1.0, 0.0)
    pick = jnp.where((ti // n_tok == si) & (ti % n_tok == n_tok - 1), 1.0, 0.0)
    hdot = lambda a, b: jnp.dot(a, b, precision=HIGHEST, preferred_element_type=F32)

    lane128 = lax.broadcasted_iota(jnp.int32, (L, LANES), 1)
    even128 = lane128 < HEAD_DIM
    lane256 = lax.broadcasted_iota(jnp.int32, (1, 2 * LANES), 1)
    cols_e = (lane256 < HEAD_DIM) | (lane256 == LANES)
    cols_o = ((lane256 >= HEAD_DIM) & (lane256 < LANES)) | (lane256 == LANES + 1)
    rows_e = lax.broadcasted_iota(jnp.int32, (LANES, 1), 0) < HEAD_DIM
    ones_cols = jnp.where(lane128 < 2, 1.0, 0.0)
    bo_r = lax.broadcasted_iota(jnp.int32, (LANES, LANES), 0) // HEAD_DIM
    bo_c = lax.broadcasted_iota(jnp.int32, (LANES, LANES), 1) // HEAD_DIM
    block_ones = jnp.where(bo_r == bo_c, 1.0, 0.0)
    W = NB * LANES
    rb = lax.broadcasted_iota(jnp.int32, (L, W), 0)
    cb = lax.broadcasted_iota(jnp.int32, (L, W), 1)
    own_block = (rb // n_tok) == (cb // LANES)
    bd_tiled = (rb // HEAD_DIM) == ((cb % LANES) // HEAD_DIM)

    grow = grow_ref[...]
    i_row = grow[0:N_HEADS]
    b_row = hdot(grow[N_HEADS:2 * N_HEADS], useg)
    b_last = hdot(b_row, slast)
    a_row = b_last - b_row + i_row
    pos = lax.broadcasted_iota(jnp.int32, a_row.shape, 1) % n_tok
    pm = a_row
    sh = 1
    while sh < n_tok:
        pm = jnp.where(pos >= sh, jnp.maximum(pm, pltpu.roll(pm, sh, axis=1)), pm)
        sh *= 2
    m_carry = hdot(jnp.concatenate([m0t_ref[0], b_row], axis=1), jnp.concatenate([expand_t, slast], axis=0))
    m_new_row = jnp.maximum(m_carry, hdot(pm, slast))
    decay_row = jnp.exp(m_carry - m_new_row)
    w_row = jnp.exp(a_row - m_new_row)
    mt_ref[0] = hdot(m_new_row, pick)
    decay_bh = hdot(decay_row, pick)
    decay_hb = jnp.transpose(decay_bh)[0:NB]

    bc_all = gcol_ref[...]
    rowpos = lax.broadcasted_iota(jnp.int32, bc_all.shape, 0) % n_tok
    sh = 1
    while sh < n_tok:
        bc_all = bc_all + jnp.where(rowpos >= sh, pltpu.roll(bc_all, sh, axis=0), 0.0)
        sh *= 2
    pad_rows = lambda a: jnp.concatenate([a, jnp.zeros((L - NB, a.shape[1]), F32)], axis=0)
    m0_col = hdot(expand, pad_rows(m0_ref[...]))
    rowv_all = i_row - b_row

    for p in range(N_PAIRS):
        ls = slice(p * LANES, (p + 1) * LANES)
        q2f = qm_ref[:, ls]
        q2 = q2f.astype(BF16)
        kt2 = kmt_ref[ls, :]
        v2 = vm_ref[:, ls]
        vext = jnp.concatenate([v2, ones_cols], axis=1)
        kt_e = jnp.where(rows_e, kt2, 0.0).astype(BF16)
        kt_o = jnp.where(rows_e, 0.0, kt2).astype(BF16)
        s2 = jnp.dot(q2, jnp.concatenate([kt_e, kt_o], axis=1), preferred_element_type=F32)
        ps, mts, inters = [], [], []
        for hh in range(2):
            h = 2 * p + hh
            bcol = bc_all[:, N_HEADS + h:N_HEADS + h + 1]
            logd = jnp.where(causal, bcol + rowv_all[h:h + 1, :], NEG)
            m_inter = m0_col[:, h:h + 1] + bcol
            m_t = jnp.maximum(m_inter, jnp.max(logd, axis=1, keepdims=True))
            ps.append((s2[:, hh * L:(hh + 1) * L] * jnp.exp(logd - m_t)).astype(BF16))
            mts.append(m_t)
            inters.append(jnp.exp(m_inter - m_t))
        vstack = jnp.concatenate([jnp.where(cols_e, vext, 0.0), jnp.where(cols_o, vext, 0.0)],
                                 axis=0).astype(BF16)
        r = jnp.dot(jnp.concatenate(ps, axis=1), vstack, preferred_element_type=F32)
        zero = jnp.zeros((HEAD_DIM, HEAD_DIM), F32)
        cstack = jnp.concatenate(
            [jnp.concatenate([jnp.concatenate([c0_ref[b, 2 * p], zero], axis=1),
                              jnp.concatenate([zero, c0_ref[b, 2 * p + 1]], axis=1)], axis=0)
             for b in range(NB)], axis=1)
        rq_all = jnp.where(own_block, jnp.dot(q2, cstack.astype(BF16), preferred_element_type=F32), 0.0)
        rq = rq_all[:, 0:LANES]
        for b in range(1, NB):
            rq = rq + rq_all[:, b * LANES:(b + 1) * LANES]
        n_rows = hdot(expand, pad_rows(n0_ref[:, ls]))
        qn = hdot(q2f * n_rows, block_ones)
        inter2 = jnp.where(even128, inters[0], inters[1])
        num = r[:, 0:LANES] + inter2 * rq
        den = jnp.where(even128, r[:, LANES:LANES + 1], r[:, LANES + 1:LANES + 2]) + inter2 * qn
        mt2 = jnp.where(even128, mts[0], mts[1])
        hv = num / jnp.maximum(jnp.abs(den), jnp.exp(-mt2))
        ms = hdot(hv * hv, block_ones) * (1.0 / HEAD_DIM)
        y = hv * lax.rsqrt(ms + RMS_EPS) * gh_ref[:, ls] * _sigmoid(om_ref[:, ls])
        hm_ref[:, ls] = y.astype(hm_ref.dtype)
        kw = kt2 * jnp.where(rows_e, w_row[2 * p:2 * p + 1, :], w_row[2 * p + 1:2 * p + 2, :])
        vbd = jnp.where(own_block, jnp.concatenate([v2] * NB, axis=1), 0.0).astype(BF16)
        upd = jnp.dot(kw.astype(BF16), vbd, preferred_element_type=F32)
        upd = jnp.where(bd_tiled, upd, 0.0)
        for b in range(NB):
            bs = slice(b * LANES, (b + 1) * LANES)
            dec_b = jnp.where(rows_e, decay_hb[b:b + 1, 2 * p:2 * p + 1], decay_hb[b:b + 1, 2 * p + 1:2 * p + 2])
            cnew = dec_b * cstack[:, bs] + upd[:, bs]
            c_ref[b, 2 * p] = cnew[0:HEAD_DIM, 0:HEAD_DIM]
            c_ref[b, 2 * p + 1] = cnew[HEAD_DIM:LANES, HEAD_DIM:LANES]
        nsum = jnp.transpose(hdot(kw, expand))[0:NB]
        dec_n = jnp.where(lax.broadcasted_iota(jnp.int32, (NB, LANES), 1) < HEAD_DIM,
                          decay_hb[:, 2 * p:2 * p + 1], decay_hb[:, 2 * p + 1:2 * p + 2])
        n_ref[:, ls] = dec_n * n0_ref[:, ls] + nsum


def _mlstm_sample(qm, kmt, vm, om, gcol, grow, c0, n0, m0, gh, row0, n_tok):
    nb = c0.shape[0]
    g_nb = LANES // n_tok
    n_g = nb // g_nb
    blk0 = row0 // LANES
    tokb = lambda w: pl.BlockSpec((LANES, w), lambda i: (blk0 + i, 0))
    rowb = lambda r: pl.BlockSpec((r, LANES), lambda i: (0, blk0 + i))
    m0t = jnp.pad(m0.reshape(n_g, g_nb, N_HEADS).transpose(0, 2, 1),
                  ((0, 0), (0, 0), (0, LANES - g_nb)))
    return pl.pallas_call(
        functools.partial(_mlstm_sample_kernel, n_tok),
        grid=(n_g,),
        in_specs=[tokb(DM), rowb(DM), tokb(DM), tokb(DM), tokb(LANES), rowb(2 * N_HEADS),
                  pl.BlockSpec((g_nb, N_HEADS, HEAD_DIM, HEAD_DIM), lambda i: (i, 0, 0, 0)),
                  pl.BlockSpec((g_nb, DM), lambda i: (i, 0)),
                  pl.BlockSpec((g_nb, N_HEADS), lambda i: (i, 0)),
                  pl.BlockSpec((1, N_HEADS, LANES), lambda i: (i, 0, 0)),
                  pl.BlockSpec((1, DM), lambda i: (0, 0))],
        out_specs=(pl.BlockSpec((LANES, DM), lambda i: (i, 0)),
                   pl.BlockSpec((g_nb, N_HEADS, HEAD_DIM, HEAD_DIM), lambda i: (i, 0, 0, 0)),
                   pl.BlockSpec((g_nb, DM), lambda i: (i, 0)),
                   pl.BlockSpec((1, N_HEADS, LANES), lambda i: (i, 0, 0))),
        out_shape=(jax.ShapeDtypeStruct((nb * n_tok, DM), BF16), jax.ShapeDtypeStruct(c0.shape, F32),
                   jax.ShapeDtypeStruct((nb, DM), F32), jax.ShapeDtypeStruct((n_g, N_HEADS, LANES), F32)),
        compiler_params=_cparams(("arbitrary",), VMEM_LIMIT),
        name="mlstm_sample",
    )(qm, kmt, vm, om, gcol, grow, c0, n0.reshape(nb, DM), m0, m0t, gh)


def _alibi_slope(h):
    return float(np.float32(2.0 ** (-8.0 * (h + 1) / N_HEADS)))


def _dup_halves(x):
    lane = lax.broadcasted_iota(jnp.int32, x.shape, 1)
    xr = pltpu.roll(x, HEAD_DIM, axis=1)
    lo = lane < HEAD_DIM
    return jnp.where(lo, x, xr), jnp.where(lo, xr, x)


def _stack_group_queries(q, g):
    lane = lax.broadcasted_iota(jnp.int32, (q.shape[0], LANES), 1)
    parts = []
    for hh in range(GROUP):
        h = GROUP * g + hh
        blk = q[:, (h // 2) * LANES:(h // 2 + 1) * LANES]
        keep = (lane < HEAD_DIM) if h % 2 == 0 else (lane >= HEAD_DIM)
        parts.append(jnp.where(keep, blk, 0.0))
    return jnp.concatenate(parts, axis=0).astype(BF16)


def _swa_prompt_kernel(sink_ref, q_ref, kp_ref, ko_ref, vp_ref, vo_ref, o_ref):
    j = pl.program_id(1)
    R = WINDOW
    q = q_ref[...]
    kd = _dup_halves(jnp.concatenate([kp_ref[...], ko_ref[...]], axis=0))
    vd = _dup_halves(jnp.concatenate([vp_ref[...], vo_ref[...]], axis=0))
    qi = lax.broadcasted_iota(jnp.int32, (R, 2 * R), 0)
    kj = lax.broadcasted_iota(jnp.int32, (R, 2 * R), 1)
    dist = qi + R - kj
    valid = (dist >= 0) & (dist < WINDOW) & ((kj >= R) | (j > 0))
    distf = dist.astype(F32)
    lane = lax.broadcasted_iota(jnp.int32, (R, LANES), 1)
    outs = []
    for g in range(N_KV):
        qs = _stack_group_queries(q, g)
        s = lax.dot_general(qs, kd[g].astype(BF16), (((1,), (1,)), ((), ())),
                            preferred_element_type=F32) * (HEAD_DIM ** -0.5)
        ps = []
        for hh in range(GROUP):
            h = GROUP * g + hh
            sink = sink_ref[h]
            sh = jnp.where(valid, s[hh * R:(hh + 1) * R] - _alibi_slope(h) * distf, NEG)
            mx = jnp.maximum(jnp.max(sh, axis=1, keepdims=True), sink)
            p = jnp.exp(sh - mx)
            p = p / (jnp.sum(p, axis=1, keepdims=True) + jnp.exp(sink - mx))
            ps.append(p.astype(BF16))
        o = jnp.dot(jnp.concatenate(ps, axis=0), vd[g].astype(BF16), preferred_element_type=F32)
        for pp in range(GROUP // 2):
            outs.append(jnp.where(lane < HEAD_DIM, o[(2 * pp) * R:(2 * pp + 1) * R],
                                  o[(2 * pp + 1) * R:(2 * pp + 2) * R]))
    o_ref[...] = jnp.concatenate(outs, axis=1).astype(o_ref.dtype)


def _swa_prompt(sinks, qa, ka, va, batch, seq):
    nb = seq // WINDOW
    own = lambda w: pl.BlockSpec((WINDOW, w), lambda b, j: (b * nb + j, 0))
    prev = lambda w: pl.BlockSpec((WINDOW, w), lambda b, j: (b * nb + jnp.maximum(j - 1, 0), 0))
    return pl.pallas_call(
        _swa_prompt_kernel,
        grid=(batch, nb),
        in_specs=[pl.BlockSpec(memory_space=pltpu.SMEM), own(DM), prev(DKV), own(DKV), prev(DKV), own(DKV)],
        out_specs=own(DM),
        out_shape=jax.ShapeDtypeStruct((batch * seq, DM), BF16),
        compiler_params=_cparams(("arbitrary", "arbitrary"), VMEM_LIMIT),
        name="swa_prompt",
    )(sinks, qa, ka, ka, va, va)


def _swa_decode_kernel(n_tok, sink_ref, q_ref, kn_ref, vn_ref, kc_ref, vc_ref, o_ref, ko_ref, vo_ref):
    W = WINDOW
    Tn = n_tok
    qt = lax.broadcasted_iota(jnp.int32, (Tn, W), 0)
    kj = lax.broadcasted_iota(jnp.int32, (Tn, W), 1)
    dist_c = qt + W - kj
    valid_c = dist_c < WINDOW
    qt2 = lax.broadcasted_iota(jnp.int32, (Tn, Tn), 0)
    kj2 = lax.broadcasted_iota(jnp.int32, (Tn, Tn), 1)
    dist_n = qt2 - kj2
    valid_n = dist_n >= 0
    dcf = dist_c.astype(F32)
    dnf = dist_n.astype(F32)
    lane = lax.broadcasted_iota(jnp.int32, (Tn, LANES), 1)

    def body(b, carry):
        rows = pl.ds(pl.multiple_of(b * Tn, Tn), Tn)
        q = q_ref[rows, :]
        kn = kn_ref[rows, :]
        vn = vn_ref[rows, :]
        kc = kc_ref[b]
        vc = vc_ref[b]
        ko_ref[b, 0:W - Tn, :] = kc[Tn:W]
        ko_ref[b, W - Tn:W, :] = kn
        vo_ref[b, 0:W - Tn, :] = vc[Tn:W]
        vo_ref[b, W - Tn:W, :] = vn
        kcd, knd = _dup_halves(kc), _dup_halves(kn)
        vcd, vnd = _dup_halves(vc), _dup_halves(vn)
        outs = []
        for g in range(N_KV):
            qs = _stack_group_queries(q, g)
            nt = (((1,), (1,)), ((), ()))
            sc = lax.dot_general(qs, kcd[g].astype(BF16), nt, preferred_element_type=F32) * (HEAD_DIM ** -0.5)
            sn = lax.dot_general(qs, knd[g].astype(BF16), nt, preferred_element_type=F32) * (HEAD_DIM ** -0.5)
            pcs, pns = [], []
            for hh in range(GROUP):
                h = GROUP * g + hh
                sink = sink_ref[h]
                slope = _alibi_slope(h)
                shc = jnp.where(valid_c, sc[hh * Tn:(hh + 1) * Tn] - slope * dcf, NEG)
                shn = jnp.where(valid_n, sn[hh * Tn:(hh + 1) * Tn] - slope * dnf, NEG)
                mx = jnp.maximum(jnp.maximum(jnp.max(shc, axis=1, keepdims=True),
                                             jnp.max(shn, axis=1, keepdims=True)), sink)
                pc = jnp.exp(shc - mx)
                pn = jnp.exp(shn - mx)
                inv = 1.0 / (jnp.sum(pc, axis=1, keepdims=True) + jnp.sum(pn, axis=1, keepdims=True)
                             + jnp.exp(sink - mx))
                pcs.append((pc * inv).astype(BF16))
                pns.append((pn * inv).astype(BF16))
            o = (jnp.dot(jnp.concatenate(pcs, axis=0), vcd[g].astype(BF16), preferred_element_type=F32)
                 + jnp.dot(jnp.concatenate(pns, axis=0), vnd[g].astype(BF16), preferred_element_type=F32))
            for pp in range(GROUP // 2):
                outs.append(jnp.where(lane < HEAD_DIM, o[(2 * pp) * Tn:(2 * pp + 1) * Tn],
                                      o[(2 * pp + 1) * Tn:(2 * pp + 2) * Tn]))
        o_ref[rows, :] = jnp.concatenate(outs, axis=1)
        return carry

    lax.fori_loop(0, DEC_G, body, 0, unroll=DEC_UNROLL)


def _swa_decode(sinks, qa, ka, va, kc, vc, row0, n_tok):
    nb = kc.shape[0]
    blk0 = row0 // (DEC_G * n_tok)
    tokb = lambda w: pl.BlockSpec((DEC_G * n_tok, w), lambda i: (blk0 + i, 0))
    cache = pl.BlockSpec((DEC_G, WINDOW, DKV), lambda i: (i, 0, 0))
    return pl.pallas_call(
        functools.partial(_swa_decode_kernel, n_tok),
        grid=(nb // DEC_G,),
        in_specs=[pl.BlockSpec(memory_space=pltpu.SMEM), tokb(DM), tokb(DKV), tokb(DKV), cache, cache],
        out_specs=(pl.BlockSpec((DEC_G * n_tok, DM), lambda i: (i, 0)), cache, cache),
        out_shape=(jax.ShapeDtypeStruct((nb * n_tok, DM), F32),
                   jax.ShapeDtypeStruct(kc.shape, F32), jax.ShapeDtypeStruct(vc.shape, F32)),
        compiler_params=_cparams(("arbitrary",), VMEM_LIMIT),
        name="swa_decode",
    )(sinks, qa, ka, va, kc, vc)


def _outproj_router_kernel(n_ptiles, xp_ref, xs_ref, hmp_ref, hap_ref, hms_ref, has_ref, wom_ref, woa_ref,
                           g_ref, wrt_ref, br_ref,
                           x1_ref, xg_ref, meta_ref, cnt_ref):
    i = pl.program_id(0)
    is_p = i < n_ptiles
    x = jnp.where(is_p, xp_ref[...], xs_ref[...])
    hm = jnp.where(is_p, hmp_ref[...], hms_ref[...].astype(BF16))
    ha = jnp.where(is_p, hap_ref[...], has_ref[...].astype(BF16))
    x1 = (x + jnp.dot(hm, wom_ref[...], preferred_element_type=F32)
          + jnp.dot(ha, woa_ref[...], preferred_element_type=F32))
    x1_ref[...] = x1
    h2 = _rms(x1, g_ref[...])
    logits = lax.dot_general(wrt_ref[...], h2, (((1,), (1,)), ((), ())), precision=HIGHEST,
                             preferred_element_type=F32) + br_ref[...]
    eidx = lax.broadcasted_iota(jnp.int32, logits.shape, 0).astype(F32)
    work = logits
    vals, hots = [], []
    for _ in range(TOP_K):
        mv = jnp.max(work, axis=0, keepdims=True)
        sel = jnp.min(jnp.where(work == mv, eidx, float(N_EXPERTS)), axis=0, keepdims=True)
        hot = eidx == sel
        vals.append(mv)
        hots.append(hot)
        work = jnp.where(hot, -jnp.inf, work)
    es = [jnp.exp(v - vals[0]) for v in vals]
    tot = es[0] + es[1] + es[2] + es[3]
    gates = [e / tot for e in es]
    hot_all = jnp.where(hots[0] | hots[1] | hots[2] | hots[3], 1.0, 0.0)
    tm = logits.shape[1]
    su = (lax.broadcasted_iota(jnp.int32, (tm, tm), 0) < lax.broadcasted_iota(jnp.int32, (tm, tm), 1))
    cum = jnp.dot(hot_all.astype(BF16), su.astype(BF16), preferred_element_type=F32)
    cnt = jnp.sum(hot_all, axis=1, keepdims=True)
    cpad = (((cnt.astype(jnp.int32) + (SUBLANES - 1)) // SUBLANES) * SUBLANES).astype(F32)
    lower = (lax.broadcasted_iota(jnp.int32, (N_EXPERTS, N_EXPERTS), 0)
             > lax.broadcasted_iota(jnp.int32, (N_EXPERTS, N_EXPERTS), 1)).astype(F32)
    lstart = jnp.dot(lower, jnp.broadcast_to(cpad, (N_EXPERTS, LANES)), precision=HIGHEST,
                     preferred_element_type=F32)[:, 0:1]
    base = lstart + cum
    lpos = [jnp.sum(jnp.where(hot, base, 0.0), axis=0, keepdims=True) for hot in hots]
    lpi = [p.astype(jnp.int32) for p in lpos]
    r_iota = lax.broadcasted_iota(jnp.int32, (GROUP_R, tm), 0)
    sel01 = jnp.where(r_iota == lpi[0], 1.0, jnp.where(r_iota == lpi[1], 1.0, jnp.where(
        r_iota == lpi[2], 1.0, jnp.where(r_iota == lpi[3], 1.0, 0.0)))).astype(BF16)
    xg_ref[...] = jnp.dot(sel01, h2.astype(BF16), preferred_element_type=F32)
    meta_ref[...] = jnp.transpose(jnp.concatenate(gates + lpos, axis=0))
    cnt_ref[0] = jnp.broadcast_to(cnt, (N_EXPERTS, LANES))


def _outproj_router(xp, xs, hmp, hap, hms, has, wom, woa, g_ffn, wrt, br):
    tp, ts = xp.shape[0], xs.shape[0]
    n_pt, n_st = tp // TM, ts // TM
    t_all = tp + ts
    pblk = lambda w: pl.BlockSpec((TM, w), lambda i: (jnp.minimum(i, n_pt - 1), 0))
    sblk = lambda w: pl.BlockSpec((TM, w), lambda i: (jnp.maximum(i - n_pt, 0), 0))
    full = lambda a: pl.BlockSpec(a.shape, lambda i: (0,) * a.ndim)
    return pl.pallas_call(
        functools.partial(_outproj_router_kernel, n_pt),
        grid=(n_pt + n_st,),
        in_specs=[pblk(D_MODEL), sblk(D_MODEL), pblk(DM), pblk(DM), sblk(DM), sblk(DM),
                  full(wom), full(woa), full(g_ffn), full(wrt), full(br)],
        out_specs=(pl.BlockSpec((TM, D_MODEL), lambda i: (i, 0)),
                   pl.BlockSpec((GROUP_R, D_MODEL), lambda i: (i, 0)),
                   pl.BlockSpec((TM, 2 * TOP_K), lambda i: (i, 0)),
                   pl.BlockSpec((1, N_EXPERTS, LANES), lambda i: (i, 0, 0))),
        out_shape=(jax.ShapeDtypeStruct((t_all, D_MODEL), F32),
                   jax.ShapeDtypeStruct(((n_pt + n_st) * GROUP_R, D_MODEL), F32),
                   jax.ShapeDtypeStruct((t_all, 2 * TOP_K), F32),
                   jax.ShapeDtypeStruct((n_pt + n_st, N_EXPERTS, LANES), F32)),
        compiler_params=_cparams(("arbitrary",), VMEM_LIMIT),
        name="outproj_router",
    )(xp, xs, hmp, hap, hms, has, wom, woa, g_ffn, wrt, br)


def _expert_kernel(be_ref, na_ref, slot_ref, nxt_ref, ctab_ref, ctab_next_ref, xg_ref, wgu_ref, bgu_ref,
                   wd_ref, bd_ref, o_ref, xbuf, wgu_f, wd_f, wgu_s, wd_s, xsem, wsem):
    i = pl.program_id(0)
    na = na_ref[0]

    def x_copies(tab_ref, slot):
        return [pltpu.make_async_copy(xg_ref.at[pl.ds(pl.multiple_of(tab_ref[0, 0, c], SUBLANES), SUBLANES)],
                                      xbuf.at[slot, pl.ds(c * SUBLANES, SUBLANES)], xsem.at[slot])
                for c in range(MOE_BM // SUBLANES)]

    def w_copies(e, slot):
        return [pltpu.make_async_copy(wgu_ref.at[e], wgu_f.at[slot], wsem.at[slot]),
                pltpu.make_async_copy(wd_ref.at[e], wd_f.at[slot], wsem.at[slot])]

    @pl.when(i == 0)
    def _():
        for cp in x_copies(ctab_ref, 0) + w_copies(be_ref[0], 0):
            cp.start()

    @pl.when(i < na)
    def _():
        changed = (i == 0) | (be_ref[i] != be_ref[jnp.maximum(i - 1, 0)])
        wslot = slot_ref[i]
        xslot = lax.rem(i, 2)

        @pl.when(i + 1 < na)
        def _():
            for cp in x_copies(ctab_next_ref, 1 - xslot):
                cp.start()

        @pl.when(changed)
        def _():
            for cp in w_copies(be_ref[i], wslot):
                cp.wait()

            @pl.when(nxt_ref[i] >= 0)
            def _():
                for cp in w_copies(nxt_ref[i], 1 - wslot):
                    cp.start(priority=1)

            wgu_s[...] = wgu_f[wslot].astype(BF16)
            wd_s[...] = wd_f[wslot].astype(BF16)

        for cp in x_copies(ctab_ref, xslot):
            cp.wait()

        x = xbuf[xslot].astype(BF16)
        gu = jnp.dot(x, wgu_s[...], preferred_element_type=F32) + bgu_ref[...]
        gate = jnp.minimum(gu[:, :D_FF], SWIGLU_LIMIT)
        up = jnp.clip(gu[:, D_FF:], -SWIGLU_LIMIT, SWIGLU_LIMIT)
        act = (up + 1.0) * (gate * _sigmoid(gate * SWIGLU_ALPHA))
        o_ref[...] = jnp.dot(act.astype(BF16), wd_s[...], preferred_element_type=F32) + bd_ref[...]


def _expert_ffn(block_exp, n_active, wslot, next_exp, ctab, xg, wgu, bgu, wd, bd):
    n_blocks = ctab.shape[0]
    n_rows = n_blocks * MOE_BM
    nch = MOE_BM // SUBLANES
    grid_spec = pltpu.PrefetchScalarGridSpec(
        num_scalar_prefetch=4,
        grid=(n_blocks,),
        in_specs=[pl.BlockSpec((1, 1, nch), lambda i, *_: (i, 0, 0), memory_space=pltpu.SMEM),
                  pl.BlockSpec((1, 1, nch), lambda i, *_: (jnp.minimum(i + 1, n_blocks - 1), 0, 0),
                               memory_space=pltpu.SMEM),
                  pl.BlockSpec(memory_space=pl.ANY),
                  pl.BlockSpec(memory_space=pl.ANY),
                  pl.BlockSpec((None, 1, 2 * D_FF), lambda i, be, *_: (be[i], 0, 0)),
                  pl.BlockSpec(memory_space=pl.ANY),
                  pl.BlockSpec((None, 1, D_MODEL), lambda i, be, *_: (be[i], 0, 0))],
        out_specs=pl.BlockSpec((MOE_BM, D_MODEL), lambda i, be, na, *_: (jnp.minimum(i, na[0] - 1), 0)),
        scratch_shapes=[pltpu.VMEM((2, MOE_BM, D_MODEL), F32),
                        pltpu.VMEM((2, D_MODEL, 2 * D_FF), F32), pltpu.VMEM((2, D_FF, D_MODEL), F32),
                        pltpu.VMEM((D_MODEL, 2 * D_FF), BF16), pltpu.VMEM((D_FF, D_MODEL), BF16),
                        pltpu.SemaphoreType.DMA((2,)), pltpu.SemaphoreType.DMA((2,))],
    )
    return pl.pallas_call(
        _expert_kernel,
        grid_spec=grid_spec,
        out_shape=jax.ShapeDtypeStruct((n_rows, D_MODEL), F32),
        compiler_params=_cparams(("arbitrary",), VMEM_LIMIT),
        name="moe_experts",
    )(block_exp, n_active, wslot, next_exp, ctab, ctab, xg, wgu, bgu, wd, bd)


def _combine_kernel(n_ptiles, ctab_ref, ctab_next_ref, outs_ref, x1_ref, meta_ref, gf_ref, yp_ref, ys_ref,
                    obuf, sem):
    i = pl.program_id(0)
    n = pl.num_programs(0)
    slot = lax.rem(i, 2)

    def copies(tab_ref, s):
        return [pltpu.make_async_copy(outs_ref.at[pl.ds(pl.multiple_of(tab_ref[0, 0, c], SUBLANES), SUBLANES)],
                                      obuf.at[s, pl.ds(c * SUBLANES, SUBLANES)], sem.at[s])
                for c in range(GROUP_R // SUBLANES)]

    @pl.when(i == 0)
    def _():
        for cp in copies(ctab_ref, 0):
            cp.start()

    @pl.when(i + 1 < n)
    def _():
        for cp in copies(ctab_next_ref, 1 - slot):
            cp.start()

    for cp in copies(ctab_ref, slot):
        cp.wait()

    meta = meta_ref[...]
    tm = meta.shape[0]
    r_iota = lax.broadcasted_iota(jnp.int32, (tm, GROUP_R), 1)
    lp = [meta[:, TOP_K + k:TOP_K + k + 1].astype(jnp.int32) for k in range(TOP_K)]
    gk = [meta[:, k:k + 1] for k in range(TOP_K)]
    gsel = jnp.where(r_iota == lp[0], gk[0], jnp.where(r_iota == lp[1], gk[1], jnp.where(
        r_iota == lp[2], gk[2], jnp.where(r_iota == lp[3], gk[3], 0.0))))
    sel01 = jnp.where(gsel != 0.0, 1.0, 0.0).astype(BF16)
    rg_row = jnp.sum(gsel, axis=0, keepdims=True)
    rg_col = jnp.transpose(jnp.broadcast_to(rg_row, (SUBLANES, GROUP_R)))[:, 0:1]
    og = (obuf[slot] * rg_col).astype(BF16)
    acc = x1_ref[...] + jnp.dot(sel01, og, preferred_element_type=F32)
    y = _rms(acc, gf_ref[...])

    @pl.when(i < n_ptiles)
    def _():
        yp_ref[...] = y

    @pl.when(i >= n_ptiles)
    def _():
        ys_ref[...] = y


def _combine(ctab, outs, x1, meta, g_final, tp, ts):
    n_pt, n_st = tp // TM, ts // TM
    n = n_pt + n_st
    nch = GROUP_R // SUBLANES
    return pl.pallas_call(
        functools.partial(_combine_kernel, n_pt),
        grid=(n,),
        in_specs=[pl.BlockSpec((1, 1, nch), lambda i: (i, 0, 0), memory_space=pltpu.SMEM),
                  pl.BlockSpec((1, 1, nch), lambda i: (jnp.minimum(i + 1, n - 1), 0, 0), memory_space=pltpu.SMEM),
                  pl.BlockSpec(memory_space=pl.ANY),
                  pl.BlockSpec((TM, D_MODEL), lambda i: (i, 0)),
                  pl.BlockSpec((TM, 2 * TOP_K), lambda i: (i, 0)),
                  pl.BlockSpec((1, D_MODEL), lambda i: (0, 0))],
        out_specs=(pl.BlockSpec((TM, D_MODEL), lambda i: (jnp.minimum(i, n_pt - 1), 0)),
                   pl.BlockSpec((TM, D_MODEL), lambda i: (jnp.maximum(i - n_pt, 0), 0))),
        out_shape=(jax.ShapeDtypeStruct((tp, D_MODEL), F32), jax.ShapeDtypeStruct((ts, D_MODEL), F32)),
        scratch_shapes=[pltpu.VMEM((2, GROUP_R, D_MODEL), F32), pltpu.SemaphoreType.DMA((2,))],
        compiler_params=_cparams(("arbitrary",), VMEM_LIMIT),
        name="moe_combine",
    )(ctab, ctab, outs, x1, meta, g_final)


def kernel(x_prompt, x_sample, cache_swa_k, cache_swa_v, state_mlstm_c, state_mlstm_n, state_mlstm_m,
           g_mix, w_in, b_igate, b_fgate, g_head, attn_sinks, w_out, g_ffn, w_router, b_router,
           w_gate_up, b_gate_up, w_down, b_down, g_final):
    assert w_in.shape[0] == 1, "single-layer problem"
    B, S, _ = x_prompt.shape
    Bd, Tn, _ = x_sample.shape
    tp, ts = B * S, Bd * Tn
    t_all = tp + ts
    xp = x_prompt.reshape(tp, D_MODEL)
    xs = x_sample.reshape(ts, D_MODEL)

    w = w_in[0]
    o = np.cumsum([0, DM, DM, DM, DM, N_HEADS, N_HEADS, DM, DKV, DKV])
    col = lambda a: w[:, int(o[a]):int(o[a + 1])]
    w1 = jnp.concatenate([col(0), col(2), col(3), col(6), col(7), col(8)], axis=1).astype(BF16)
    wkt = col(1).T.astype(BF16)
    wgates = jnp.concatenate([col(4), col(5)], axis=1)
    wg = jnp.pad(wgates, ((0, 0), (0, LANES - 2 * N_HEADS))).astype(BF16)
    wgt = wgates.T.astype(BF16)
    bg = jnp.concatenate([b_igate[0], b_fgate[0]]).astype(F32)
    bcol = jnp.pad(bg, (0, LANES - 2 * N_HEADS)).reshape(1, LANES)
    brow = bg.reshape(2 * N_HEADS, 1)

    qm, vm, om, qa, ka, va, kmt, gcol, grow = _inproj(xp, xs, g_mix[0].reshape(1, D_MODEL), w1, wkt, wg, wgt,
                                                      bcol, brow)

    gh = g_head[0].astype(F32)
    sinks = attn_sinks[0].astype(F32)

    hm_p, cbd, m_p = _mlstm_prompt(qm, kmt, vm, om, gcol, grow, gh.reshape(1, DM), B, S)
    ha_p = _swa_prompt(sinks, qa, ka, va, B, S)

    hm_s, c_s, n_s, mt_s = _mlstm_sample(qm, kmt, vm, om, gcol, grow, state_mlstm_c[0], state_mlstm_n[0],
                                         state_mlstm_m[0], gh.reshape(1, DM), tp, Tn)
    n_s = n_s.reshape(Bd, N_HEADS, HEAD_DIM)
    m_s = mt_s[:, :, :LANES // Tn].transpose(0, 2, 1).reshape(Bd, N_HEADS)
    ha_s, k_s, v_s = _swa_decode(sinks, qa, ka, va, cache_swa_k[0].reshape(Bd, WINDOW, DKV),
                                 cache_swa_v[0].reshape(Bd, WINDOW, DKV), tp, Tn)

    wo = w_out[0].astype(BF16)
    x1, xg, meta, cnt = _outproj_router(
        xp, xs, hm_p, ha_p, hm_s, ha_s, wo[:DM], wo[DM:], g_ffn[0].reshape(1, D_MODEL),
        w_router[0].T, b_router[0].reshape(N_EXPERTS, 1))

    i32 = jnp.int32
    n_tiles = t_all // TM
    max_rows = t_all * TOP_K + n_tiles * N_EXPERTS * (SUBLANES - 1) + N_EXPERTS * (MOE_BM - 1)
    n_blocks = -(-max_rows // MOE_BM)
    cpad = (cnt[:, :, 0].astype(i32) + (SUBLANES - 1)) // SUBLANES * SUBLANES
    lstart = jnp.cumsum(cpad, axis=1) - cpad
    goff = jnp.cumsum(cpad, axis=0) - cpad
    padded = (jnp.sum(cpad, axis=0) + MOE_BM - 1) // MOE_BM * MOE_BM
    pad_end = jnp.cumsum(padded)
    seg_begin = (pad_end - padded)[None, :] + goff
    n_active = (pad_end[-1] // MOE_BM).astype(i32)
    blk = jnp.minimum(jnp.arange(n_blocks, dtype=i32), n_active - 1)
    block_exp = jnp.minimum(jnp.sum((pad_end[None, :] <= (blk * MOE_BM)[:, None]).astype(i32), axis=1),
                            N_EXPERTS - 1)
    e_ids = jnp.arange(N_EXPERTS, dtype=i32)
    nonempty = padded > 0
    nxt_e = jnp.min(jnp.where((e_ids[None, :] > e_ids[:, None]) & nonempty[None, :], e_ids[None, :], N_EXPERTS),
                    axis=1)
    nxt_e = jnp.where(nxt_e == N_EXPERTS, -1, nxt_e)
    ord_e = jnp.cumsum(nonempty.astype(i32)) - 1
    be_hot = block_exp[:, None] == e_ids[None, :]
    next_exp = jnp.sum(jnp.where(be_hot, nxt_e[None, :], 0), axis=1).astype(i32)
    wslot = (jnp.sum(jnp.where(be_hot, ord_e[None, :], 0), axis=1) % 2).astype(i32)

    seg_src = jnp.arange(n_tiles, dtype=i32)[:, None] * GROUP_R + lstart
    sb, sl, ss = seg_begin.reshape(-1), cpad.reshape(-1), seg_src.reshape(-1)
    rc = jnp.arange(n_blocks * MOE_BM // SUBLANES, dtype=i32)[:, None] * SUBLANES
    inseg = (sb[None, :] <= rc) & (rc < (sb + sl)[None, :])
    ctab_e = jnp.where(jnp.any(inseg, axis=1), jnp.sum(jnp.where(inseg, (ss - sb)[None, :] + rc, 0), axis=1),
                       GROUP_R - SUBLANES)
    lr = jnp.arange(GROUP_R // SUBLANES, dtype=i32)[None, :, None] * SUBLANES
    inl = (lstart[:, None, :] <= lr) & (lr < (lstart + cpad)[:, None, :])
    ctab_c = jnp.sum(jnp.where(inl, (seg_begin - lstart)[:, None, :] + lr, 0), axis=2)

    outs = _expert_ffn(block_exp, n_active.reshape(1), wslot, next_exp,
                       ctab_e.astype(i32).reshape(n_blocks, 1, MOE_BM // SUBLANES), xg, w_gate_up[0],
                       b_gate_up[0].reshape(N_EXPERTS, 1, 2 * D_FF), w_down[0],
                       b_down[0].reshape(N_EXPERTS, 1, D_MODEL))
    y_p, y_s = _combine(ctab_c.astype(i32).reshape(n_tiles, 1, GROUP_R // SUBLANES), outs, x1, meta,
                        g_final.reshape(1, D_MODEL), tp, ts)

    kv_tail = lambda a: a[:tp].reshape(B, S, N_KV, HEAD_DIM)[:, S - WINDOW:][None]
    c_e = cbd[:, :, :HEAD_DIM, :HEAD_DIM]
    c_o = cbd[:, :, HEAD_DIM:, HEAD_DIM:LANES]
    c_p = jnp.stack([c_e, c_o], axis=2).reshape(B, N_HEADS, HEAD_DIM, HEAD_DIM)
    n_p = jnp.stack([cbd[:, :, :HEAD_DIM, LANES], cbd[:, :, HEAD_DIM:, LANES + 1]], axis=2).reshape(B, N_HEADS, HEAD_DIM)
    return (y_p.reshape(B, S, D_MODEL), y_s.reshape(Bd, Tn, D_MODEL),
            kv_tail(ka), kv_tail(va), c_p[None], n_p[None], m_p[:, :, 0][None],
            k_s.reshape(Bd, WINDOW, N_KV, HEAD_DIM)[None], v_s.reshape(Bd, WINDOW, N_KV, HEAD_DIM)[None],
            c_s[None], n_s[None], m_s[None])
```

```python
import functools
import math

import jax
import jax.numpy as jnp
import numpy as np
from jax import lax
from jax.experimental import pallas as pl
from jax.experimental.pallas import tpu as pltpu

F32 = jnp.float32
BF16 = jnp.bfloat16
HIGHEST = lax.Precision.HIGHEST

D_MODEL = 1024
HEAD_DIM = 64
N_HEADS = 8
N_PAIRS = N_HEADS // 2
N_KV = 2
GROUP = N_HEADS // N_KV
WINDOW = 128
N_EXPERTS = 32
TOP_K = 4
D_FF = 1024
SWIGLU_LIMIT = 7.0
SWIGLU_ALPHA = 1.702
RMS_EPS = 1e-5
DM = N_HEADS * HEAD_DIM
DKV = N_KV * HEAD_DIM
NEG = -1e30

LANES = 128
SUBLANES = 8
VMEM_LIMIT = 56 * 1024 * 1024

TM = 512
MLSTM_TL = 512
MLSTM_L = 128
SWA_BLOCKS = 4
DEC_G = 8
DEC_UNROLL = 4
MOE_BM = 256
GROUP_R = -(-(TOP_K * TM + N_EXPERTS * (SUBLANES - 1) + SUBLANES) // LANES) * LANES


def _cparams(sem, vmem=None):
    return pltpu.CompilerParams(dimension_semantics=sem, vmem_limit_bytes=vmem)


def _rms(x, g):
    return x * lax.rsqrt(jnp.mean(x * x, axis=-1, keepdims=True) + RMS_EPS) * g


def _log_sigmoid(z):
    return jnp.minimum(z, 0.0) - jnp.log(1.0 + jnp.exp(-jnp.abs(z)))


def _sigmoid(z):
    return 1.0 / (1.0 + jnp.exp(-z))


def _inproj_kernel(n_ptiles, xp_ref, xs_ref, g_ref, w1_ref, wkt_ref, wg_ref, wgt_ref, bcol_ref, brow_ref,
                   qm_ref, vm_ref, om_ref, qa_ref, ka_ref, va_ref, kmt_ref, gcol_ref, grow_ref):
    i = pl.program_id(0)
    x = jnp.where(i < n_ptiles, xp_ref[...], xs_ref[...])
    h = _rms(x, g_ref[...]).astype(BF16)
    main = jnp.dot(h, w1_ref[...], preferred_element_type=F32)
    qm_ref[...] = main[:, 0:DM]
    vm_ref[...] = main[:, DM:2 * DM]
    om_ref[...] = main[:, 2 * DM:3 * DM]
    qa_ref[...] = main[:, 3 * DM:4 * DM]
    ka_ref[...] = main[:, 4 * DM:4 * DM + DKV]
    va_ref[...] = main[:, 4 * DM + DKV:4 * DM + 2 * DKV]
    kt = lax.dot_general(wkt_ref[...], h, (((1,), (1,)), ((), ())), preferred_element_type=F32)
    kmt_ref[...] = kt * (HEAD_DIM ** -0.5)
    zc = jnp.dot(h, wg_ref[...], preferred_element_type=F32) + bcol_ref[...]
    lane = lax.broadcasted_iota(jnp.int32, zc.shape, 1)
    gcol_ref[...] = jnp.where(lane < N_HEADS, zc, _log_sigmoid(zc))
    zr = lax.dot_general(wgt_ref[...], h, (((1,), (1,)), ((), ())), preferred_element_type=F32) + brow_ref[...]
    row = lax.broadcasted_iota(jnp.int32, zr.shape, 0)
    grow_ref[...] = jnp.where(row < N_HEADS, zr, _log_sigmoid(zr))


def _inproj(xp, xs, g_mix, w1, wkt, wg, wgt, bcol, brow):
    tp, ts = xp.shape[0], xs.shape[0]
    n_pt, n_st = tp // TM, ts // TM
    t_all = tp + ts
    n1 = w1.shape[1]
    tok = lambda w: pl.BlockSpec((TM, w), lambda i: (i, 0))
    full = lambda a: pl.BlockSpec(a.shape, lambda i: (0,) * a.ndim)
    out_shape = (
        jax.ShapeDtypeStruct((t_all, DM), F32), jax.ShapeDtypeStruct((t_all, DM), F32),
        jax.ShapeDtypeStruct((t_all, DM), F32), jax.ShapeDtypeStruct((t_all, DM), F32),
        jax.ShapeDtypeStruct((t_all, DKV), F32), jax.ShapeDtypeStruct((t_all, DKV), F32),
        jax.ShapeDtypeStruct((DM, t_all), F32),
        jax.ShapeDtypeStruct((t_all, LANES), F32), jax.ShapeDtypeStruct((2 * N_HEADS, t_all), F32),
    )
    return pl.pallas_call(
        functools.partial(_inproj_kernel, n_pt),
        grid=(n_pt + n_st,),
        in_specs=[
            pl.BlockSpec((TM, D_MODEL), lambda i: (jnp.minimum(i, n_pt - 1), 0)),
            pl.BlockSpec((TM, D_MODEL), lambda i: (jnp.maximum(i - n_pt, 0), 0)),
            full(g_mix), full(w1), full(wkt), full(wg), full(wgt), full(bcol), full(brow),
        ],
        out_specs=(tok(DM), tok(DM), tok(DM), tok(DM), tok(DKV), tok(DKV),
                   pl.BlockSpec((DM, TM), lambda i: (0, i)),
                   tok(LANES), pl.BlockSpec((2 * N_HEADS, TM), lambda i: (0, i))),
        out_shape=out_shape,
        compiler_params=_cparams(("arbitrary",), VMEM_LIMIT),
        name="inproj",
    )(xp, xs, g_mix, w1, wkt, wg, wgt, bcol, brow)


def _cumsum_rows(x, n):
    row = lax.broadcasted_iota(jnp.int32, x.shape, 0)
    sh = 1
    while sh < n:
        x = x + jnp.where(row >= sh, pltpu.roll(x, sh, axis=0), 0.0)
        sh *= 2
    return x


def _mlstm_prompt_kernel(qm_ref, kmt_ref, vm_ref, om_ref, gcol_ref, grow_ref, gh_ref,
                         hm_ref, cbd_ref, m_ref, cbd_s, m_s):
    j = pl.program_id(1)
    L = MLSTM_L

    @pl.when(j == 0)
    def _():
        cbd_s[...] = jnp.zeros_like(cbd_s)
        m_s[...] = jnp.zeros_like(m_s)

    ti = lax.broadcasted_iota(jnp.int32, (L, L), 0)
    si = lax.broadcasted_iota(jnp.int32, (L, L), 1)
    causal = ti >= si
    upper = (ti <= si).astype(F32)
    lane128 = lax.broadcasted_iota(jnp.int32, (L, LANES), 1)
    even128 = lane128 < HEAD_DIM
    lane256 = lax.broadcasted_iota(jnp.int32, (1, 2 * LANES), 1)
    cols_e = (lane256 < HEAD_DIM) | (lane256 == LANES)
    cols_o = ((lane256 >= HEAD_DIM) & (lane256 < LANES)) | (lane256 == LANES + 1)
    rowk = lax.broadcasted_iota(jnp.int32, (LANES, 1), 0)
    rows_e = rowk < HEAD_DIM
    bd_mask = (rows_e & cols_e) | ((~rows_e) & cols_o)
    ones_cols = ((lax.broadcasted_iota(jnp.int32, (L, LANES), 1) < 2)).astype(F32)
    bo_r = lax.broadcasted_iota(jnp.int32, (LANES, LANES), 0) // HEAD_DIM
    bo_c = lax.broadcasted_iota(jnp.int32, (LANES, LANES), 1) // HEAD_DIM
    block_ones = (bo_r == bo_c).astype(F32)

    for c in range(MLSTM_TL // L):
        sl = slice(c * L, (c + 1) * L)
        grow = grow_ref[:, sl]
        i_row = grow[0:N_HEADS]
        b_row = jnp.dot(grow[N_HEADS:2 * N_HEADS], upper, precision=HIGHEST,
                        preferred_element_type=F32)
        b_colall = _cumsum_rows(gcol_ref[sl, :], L)
        rowv_all = i_row - b_row
        for p in range(N_PAIRS):
            ls = slice(p * LANES, (p + 1) * LANES)
            q2 = qm_ref[sl, ls].astype(BF16)
            kt2 = kmt_ref[ls, sl]
            v2 = vm_ref[sl, ls]
            vext = jnp.concatenate([v2, ones_cols], axis=1)
            kt_e = jnp.where(rows_e, kt2, 0.0).astype(BF16)
            kt_o = jnp.where(rows_e, 0.0, kt2).astype(BF16)
            s2 = jnp.dot(q2, jnp.concatenate([kt_e, kt_o], axis=1), preferred_element_type=F32)
            cbd = cbd_s[p]
            rq = jnp.dot(q2, cbd.astype(BF16), preferred_element_type=F32)
            ps, mts, inters, wrows, decays, mnews = [], [], [], [], [], []
            for hh in range(2):
                h = 2 * p + hh
                bcol = b_colall[:, N_HEADS + h:N_HEADS + h + 1]
                logd = jnp.where(causal, bcol + rowv_all[h:h + 1, :], NEG)
                m_prev = m_s[h:h + 1, 0:1]
                m_inter = m_prev + bcol
                m_t = jnp.maximum(m_inter, jnp.max(logd, axis=1, keepdims=True))
                d = jnp.exp(logd - m_t)
                ps.append((s2[:, hh * L:(hh + 1) * L] * d).astype(BF16))
                mts.append(m_t)
                inters.append(jnp.exp(m_inter - m_t))
                m_new = m_t[L - 1:L, :]
                b_last = b_row[h:h + 1, L - 1:L]
                decays.append(jnp.exp(m_prev + b_last - m_new))
                wrows.append(jnp.exp(b_last - b_row[h:h + 1, :] + i_row[h:h + 1, :] - m_new))
                mnews.append(m_new)
            vstack = jnp.concatenate([jnp.where(cols_e, vext, 0.0), jnp.where(cols_o, vext, 0.0)],
                                     axis=0).astype(BF16)
            r = jnp.dot(jnp.concatenate(ps, axis=1), vstack, preferred_element_type=F32)
            nd = r + jnp.where(cols_e, inters[0], inters[1]) * rq
            num = nd[:, 0:LANES]
            den = jnp.where(even128, nd[:, LANES:LANES + 1], nd[:, LANES + 1:LANES + 2])
            mt2 = jnp.where(even128, mts[0], mts[1])
            hv = num / jnp.maximum(jnp.abs(den), jnp.exp(-mt2))
            ms = jnp.dot(hv * hv, block_ones, precision=HIGHEST, preferred_element_type=F32) * (1.0 / HEAD_DIM)
            y = hv * lax.rsqrt(ms + RMS_EPS) * gh_ref[:, ls] * _sigmoid(om_ref[sl, ls])
            hm_ref[sl, ls] = y.astype(hm_ref.dtype)
            w2 = jnp.where(rows_e, wrows[0], wrows[1])
            upd = jnp.dot((kt2 * w2).astype(BF16), vext.astype(BF16), preferred_element_type=F32)
            dec2 = jnp.where(rows_e, decays[0], decays[1])
            cbd_s[p] = dec2 * cbd + jnp.where(bd_mask, upd, 0.0)
            for hh in range(2):
                h = 2 * p + hh
                m_s[h:h + 1, :] = jnp.broadcast_to(mnews[hh], (1, LANES))

    @pl.when(j == pl.num_programs(1) - 1)
    def _():
        cbd_ref[0] = cbd_s[...]
        m_ref[0] = m_s[...]


def _mlstm_prompt(qm, kmt, vm, om, gcol, grow, gh, batch, seq):
    nt = seq // MLSTM_TL
    tokb = lambda w: pl.BlockSpec((MLSTM_TL, w), lambda b, j: (b * nt + j, 0))
    rowb = lambda r: pl.BlockSpec((r, MLSTM_TL), lambda b, j: (0, b * nt + j))
    return pl.pallas_call(
        _mlstm_prompt_kernel,
        grid=(batch, nt),
        in_specs=[tokb(DM), rowb(DM), tokb(DM), tokb(DM), tokb(LANES), rowb(2 * N_HEADS),
                  pl.BlockSpec((1, DM), lambda b, j: (0, 0))],
        out_specs=(tokb(DM),
                   pl.BlockSpec((1, N_PAIRS, LANES, 2 * LANES), lambda b, j: (b, 0, 0, 0)),
                   pl.BlockSpec((1, N_HEADS, LANES), lambda b, j: (b, 0, 0))),
        out_shape=(jax.ShapeDtypeStruct((batch * seq, DM), BF16),
                   jax.ShapeDtypeStruct((batch, N_PAIRS, LANES, 2 * LANES), F32),
                   jax.ShapeDtypeStruct((batch, N_HEADS, LANES), F32)),
        scratch_shapes=[pltpu.VMEM((N_PAIRS, LANES, 2 * LANES), F32), pltpu.VMEM((N_HEADS, LANES), F32)],
        compiler_params=_cparams(("arbitrary", "arbitrary"), VMEM_LIMIT),
        name="mlstm_prompt",
    )(qm, kmt, vm, om, gcol, grow, gh)


def _mlstm_sample_kernel(n_tok, qm_ref, kmt_ref, vm_ref, om_ref, gcol_ref, grow_ref, c0_ref, n0_ref, m0_ref,
                         m0t_ref, gh_ref, hm_ref, c_ref, n_ref, mt_ref):
    L = LANES
    NB = L // n_tok
    ti = lax.broadcasted_iota(jnp.int32, (L, L), 0)
    si = lax.broadcasted_iota(jnp.int32, (L, L), 1)
    same = (ti // n_tok) == (si // n_tok)
    causal = same & (ti >= si)
    useg = jnp.where(same & (ti <= si), 1.0, 0.0)
    slast = jnp.where(same & (ti % n_tok == n_tok - 1), 1.0, 0.0)
    expand = jnp.where(ti // n_tok == si, 1.0, 0.0)
    expand_t = jnp.where(ti == si // n_tok, 1.0, 0.0)
    pick = jnp.where((ti // n_tok == si) & (ti % n_tok == n_tok - 1), 1.0, 0.0)
    hdot = lambda a, b: jnp.dot(a, b, precision=HIGHEST, preferred_element_type=F32)

    lane128 = lax.broadcasted_iota(jnp.int32, (L, LANES), 1)
    even128 = lane128 < HEAD_DIM
    lane256 = lax.broadcasted_iota(jnp.int32, (1, 2 * LANES), 1)
    cols_e = (lane256 < HEAD_DIM) | (lane256 == LANES)
    cols_o = ((lane256 >= HEAD_DIM) & (lane256 < LANES)) | (lane256 == LANES + 1)
    rows_e = lax.broadcasted_iota(jnp.int32, (LANES, 1), 0) < HEAD_DIM
    ones_cols = jnp.where(lane128 < 2, 1.0, 0.0)
    bo_r = lax.broadcasted_iota(jnp.int32, (LANES, LANES), 0) // HEAD_DIM
    bo_c = lax.broadcasted_iota(jnp.int32, (LANES, LANES), 1) // HEAD_DIM
    block_ones = jnp.where(bo_r == bo_c, 1.0, 0.0)
    W = NB * LANES
    rb = lax.broadcasted_iota(jnp.int32, (L, W), 0)
    cb = lax.broadcasted_iota(jnp.int32, (L, W), 1)
    own_block = (rb // n_tok) == (cb // LANES)
    bd_tiled = (rb // HEAD_DIM) == ((cb % LANES) // HEAD_DIM)

    grow = grow_ref[...]
    i_row = grow[0:N_HEADS]
    b_row = hdot(grow[N_HEADS:2 * N_HEADS], useg)
    b_last = hdot(b_row, slast)
    a_row = b_last - b_row + i_row
    pos = lax.broadcasted_iota(jnp.int32, a_row.shape, 1) % n_tok
    pm = a_row
    sh = 1
    while sh < n_tok:
        pm = jnp.where(pos >= sh, jnp.maximum(pm, pltpu.roll(pm, sh, axis=1)), pm)
        sh *= 2
    m_carry = hdot(jnp.concatenate([m0t_ref[0], b_row], axis=1), jnp.concatenate([expand_t, slast], axis=0))
    m_new_row = jnp.maximum(m_carry, hdot(pm, slast))
    decay_row = jnp.exp(m_carry - m_new_row)
    w_row = jnp.exp(a_row - m_new_row)
    mt_ref[0] = hdot(m_new_row, pick)
    decay_bh = hdot(decay_row, pick)
    decay_hb = jnp.transpose(decay_bh)[0:NB]

    bc_all = gcol_ref[...]
    rowpos = lax.broadcasted_iota(jnp.int32, bc_all.shape, 0) % n_tok
    sh = 1
    while sh < n_tok:
        bc_all = bc_all + jnp.where(rowpos >= sh, pltpu.roll(bc_all, sh, axis=0), 0.0)
        sh *= 2
    pad_rows = lambda a: jnp.concatenate([a, jnp.zeros((L - NB, a.shape[1]), F32)], axis=0)
    m0_col = hdot(expand, pad_rows(m0_ref[...]))
    rowv_all = i_row - b_row

    for p in range(N_PAIRS):
        ls = slice(p * LANES, (p + 1) * LANES)
        q2f = qm_ref[:, ls]
        q2 = q2f.astype(BF16)
        kt2 = kmt_ref[ls, :]
        v2 = vm_ref[:, ls]
        vext = jnp.concatenate([v2, ones_cols], axis=1)
        kt_e = jnp.where(rows_e, kt2, 0.0).astype(BF16)
        kt_o = jnp.where(rows_e, 0.0, kt2).astype(BF16)
        s2 = jnp.dot(q2, jnp.concatenate([kt_e, kt_o], axis=1), preferred_element_type=F32)
        ps, mts, inters = [], [], []
        for hh in range(2):
            h = 2 * p + hh
            bcol = bc_all[:, N_HEADS + h:N_HEADS + h + 1]
            logd = jnp.where(causal, bcol + rowv_all[h:h + 1, :], NEG)
            m_inter = m0_col[:, h:h + 1] + bcol
            m_t = jnp.maximum(m_inter, jnp.max(logd, axis=1, keepdims=True))
            ps.append((s2[:, hh * L:(hh + 1) * L] * jnp.exp(logd - m_t)).astype(BF16))
            mts.append(m_t)
            inters.append(jnp.exp(m_inter - m_t))
        vstack = jnp.concatenate([jnp.where(cols_e, vext, 0.0), jnp.where(cols_o, vext, 0.0)],
                                 axis=0).astype(BF16)
        r = jnp.dot(jnp.concatenate(ps, axis=1), vstack, preferred_element_type=F32)
        zero = jnp.zeros((HEAD_DIM, HEAD_DIM), F32)
        cstack = jnp.concatenate(
            [jnp.concatenate([jnp.concatenate([c0_ref[b, 2 * p], zero], axis=1),
                              jnp.concatenate([zero, c0_ref[b, 2 * p + 1]], axis=1)], axis=0)
             for b in range(NB)], axis=1)
        rq_all = jnp.where(own_block, jnp.dot(q2, cstack.astype(BF16), preferred_element_type=F32), 0.0)
        rq = rq_all[:, 0:LANES]
        for b in range(1, NB):
            rq = rq + rq_all[:, b * LANES:(b + 1) * LANES]
        n_rows = hdot(expand, pad_rows(n0_ref[:, ls]))
        qn = hdot(q2f * n_rows, block_ones)
        inter2 = jnp.where(even128, inters[0], inters[1])
        num = r[:, 0:LANES] + inter2 * rq
        den = jnp.where(even128, r[:, LANES:LANES + 1], r[:, LANES + 1:LANES + 2]) + inter2 * qn
        mt2 = jnp.where(even128, mts[0], mts[1])
        hv = num / jnp.maximum(jnp.abs(den), jnp.exp(-mt2))
        ms = hdot(hv * hv, block_ones) * (1.0 / HEAD_DIM)
        y = hv * lax.rsqrt(ms + RMS_EPS) * gh_ref[:, ls] * _sigmoid(om_ref[:, ls])
        hm_ref[:, ls] = y.astype(hm_ref.dtype)
        kw = kt2 * jnp.where(rows_e, w_row[2 * p:2 * p + 1, :], w_row[2 * p + 1:2 * p + 2, :])
        vbd = jnp.where(own_block, jnp.concatenate([v2] * NB, axis=1), 0.0).astype(BF16)
        upd = jnp.dot(kw.astype(BF16), vbd, preferred_element_type=F32)
        upd = jnp.where(bd_tiled, upd, 0.0)
        for b in range(NB):
            bs = slice(b * LANES, (b + 1) * LANES)
            dec_b = jnp.where(rows_e, decay_hb[b:b + 1, 2 * p:2 * p + 1], decay_hb[b:b + 1, 2 * p + 1:2 * p + 2])
            cnew = dec_b * cstack[:, bs] + upd[:, bs]
            c_ref[b, 2 * p] = cnew[0:HEAD_DIM, 0:HEAD_DIM]
            c_ref[b, 2 * p + 1] = cnew[HEAD_DIM:LANES, HEAD_DIM:LANES]
        nsum = jnp.transpose(hdot(kw, expand))[0:NB]
        dec_n = jnp.where(lax.broadcasted_iota(jnp.int32, (NB, LANES), 1) < HEAD_DIM,
                          decay_hb[:, 2 * p:2 * p + 1], decay_hb[:, 2 * p + 1:2 * p + 2])
        n_ref[:, ls] = dec_n * n0_ref[:, ls] + nsum


def _mlstm_sample(qm, kmt, vm, om, gcol, grow, c0, n0, m0, gh, row0, n_tok):
    nb = c0.shape[0]
    g_nb = LANES // n_tok
    n_g = nb // g_nb
    blk0 = row0 // LANES
    tokb = lambda w: pl.BlockSpec((LANES, w), lambda i: (blk0 + i, 0))
    rowb = lambda r: pl.BlockSpec((r, LANES), lambda i: (0, blk0 + i))
    m0t = jnp.pad(m0.reshape(n_g, g_nb, N_HEADS).transpose(0, 2, 1),
                  ((0, 0), (0, 0), (0, LANES - g_nb)))
    return pl.pallas_call(
        functools.partial(_mlstm_sample_kernel, n_tok),
        grid=(n_g,),
        in_specs=[tokb(DM), rowb(DM), tokb(DM), tokb(DM), tokb(LANES), rowb(2 * N_HEADS),
                  pl.BlockSpec((g_nb, N_HEADS, HEAD_DIM, HEAD_DIM), lambda i: (i, 0, 0, 0)),
                  pl.BlockSpec((g_nb, DM), lambda i: (i, 0)),
                  pl.BlockSpec((g_nb, N_HEADS), lambda i: (i, 0)),
                  pl.BlockSpec((1, N_HEADS, LANES), lambda i: (i, 0, 0)),
                  pl.BlockSpec((1, DM), lambda i: (0, 0))],
        out_specs=(pl.BlockSpec((LANES, DM), lambda i: (i, 0)),
                   pl.BlockSpec((g_nb, N_HEADS, HEAD_DIM, HEAD_DIM), lambda i: (i, 0, 0, 0)),
                   pl.BlockSpec((g_nb, DM), lambda i: (i, 0)),
                   pl.BlockSpec((1, N_HEADS, LANES), lambda i: (i, 0, 0))),
        out_shape=(jax.ShapeDtypeStruct((nb * n_tok, DM), BF16), jax.ShapeDtypeStruct(c0.shape, F32),
                   jax.ShapeDtypeStruct((nb, DM), F32), jax.ShapeDtypeStruct((n_g, N_HEADS, LANES), F32)),
        compiler_params=_cparams(("arbitrary",), VMEM_LIMIT),
        name="mlstm_sample",
    )(qm, kmt, vm, om, gcol, grow, c0, n0.reshape(nb, DM), m0, m0t, gh)


def _alibi_slope(h):
    return float(np.float32(2.0 ** (-8.0 * (h + 1) / N_HEADS)))


def _dup_halves(x):
    lane = lax.broadcasted_iota(jnp.int32, x.shape, 1)
    xr = pltpu.roll(x, HEAD_DIM, axis=1)
    lo = lane < HEAD_DIM
    return jnp.where(lo, x, xr), jnp.where(lo, xr, x)


def _stack_group_queries(q, g):
    lane = lax.broadcasted_iota(jnp.int32, (q.shape[0], LANES), 1)
    parts = []
    for hh in range(GROUP):
        h = GROUP * g + hh
        blk = q[:, (h // 2) * LANES:(h // 2 + 1) * LANES]
        keep = (lane < HEAD_DIM) if h % 2 == 0 else (lane >= HEAD_DIM)
        parts.append(jnp.where(keep, blk, 0.0))
    return jnp.concatenate(parts, axis=0).astype(BF16)


def _swa_prompt_kernel(sink_ref, q_ref, kp_ref, ko_ref, vp_ref, vo_ref, o_ref):
    j = pl.program_id(1)
    R = WINDOW
    qi = lax.broadcasted_iota(jnp.int32, (R, R), 0)
    kj = lax.broadcasted_iota(jnp.int32, (R, R), 1)
    own = kj <= qi
    distf = jnp.where(own, qi - kj, qi - kj + R).astype(F32)
    lane = lax.broadcasted_iota(jnp.int32, (R, LANES), 1)
    for u in range(SWA_BLOCKS):
        rs = slice(u * R, (u + 1) * R)
        q = q_ref[rs, :]
        k_prev = kp_ref[...] if u == 0 else ko_ref[(u - 1) * R:u * R, :]
        v_prev = vp_ref[...] if u == 0 else vo_ref[(u - 1) * R:u * R, :]
        kd = _dup_halves(jnp.concatenate([k_prev, ko_ref[rs, :]], axis=0))
        vd = _dup_halves(jnp.concatenate([v_prev, vo_ref[rs, :]], axis=0))
        valid = (own | (j > 0)) if u == 0 else None
        outs = []
        for g in range(N_KV):
            qs = _stack_group_queries(q, g)
            s = lax.dot_general(qs, kd[g].astype(BF16), (((1,), (1,)), ((), ())),
                                preferred_element_type=F32) * (HEAD_DIM ** -0.5)
            ps = []
            for hh in range(GROUP):
                h = GROUP * g + hh
                sink = sink_ref[h]
                rows = slice(hh * R, (hh + 1) * R)
                sh = jnp.where(own, s[rows, R:], s[rows, :R]) - _alibi_slope(h) * distf
                if valid is not None:
                    sh = jnp.where(valid, sh, NEG)
                mx = jnp.maximum(jnp.max(sh, axis=1, keepdims=True), sink)
                p = jnp.exp(sh - mx)
                p = p / (jnp.sum(p, axis=1, keepdims=True) + jnp.exp(sink - mx))
                ps.append(jnp.concatenate([jnp.where(own, 0.0, p), jnp.where(own, p, 0.0)], axis=1).astype(BF16))
            o = jnp.dot(jnp.concatenate(ps, axis=0), vd[g].astype(BF16), preferred_element_type=F32)
            for pp in range(GROUP // 2):
                outs.append(jnp.where(lane < HEAD_DIM, o[(2 * pp) * R:(2 * pp + 1) * R],
                                      o[(2 * pp + 1) * R:(2 * pp + 2) * R]))
        o_ref[rs, :] = jnp.concatenate(outs, axis=1).astype(o_ref.dtype)


def _swa_prompt(sinks, qa, ka, va, batch, seq):
    nb = seq // (WINDOW * SWA_BLOCKS)
    own = lambda w: pl.BlockSpec((WINDOW * SWA_BLOCKS, w), lambda b, j: (b * nb + j, 0))
    prev = lambda w: pl.BlockSpec(
        (WINDOW, w), lambda b, j: (b * nb * SWA_BLOCKS + jnp.maximum(j * SWA_BLOCKS - 1, 0), 0))
    return pl.pallas_call(
        _swa_prompt_kernel,
        grid=(batch, nb),
        in_specs=[pl.BlockSpec(memory_space=pltpu.SMEM), own(DM), prev(DKV), own(DKV), prev(DKV), own(DKV)],
        out_specs=own(DM),
        out_shape=jax.ShapeDtypeStruct((batch * seq, DM), BF16),
        compiler_params=_cparams(("arbitrary", "arbitrary"), VMEM_LIMIT),
        name="swa_prompt",
    )(sinks, qa, ka, ka, va, va)


def _swa_decode_kernel(n_tok, sink_ref, q_ref, kn_ref, vn_ref, kc_ref, vc_ref, o_ref, ko_ref, vo_ref):
    W = WINDOW
    Tn = n_tok
    qt = lax.broadcasted_iota(jnp.int32, (Tn, W), 0)
    kj = lax.broadcasted_iota(jnp.int32, (Tn, W), 1)
    dist_c = qt + W - kj
    valid_c = dist_c < WINDOW
    qt2 = lax.broadcasted_iota(jnp.int32, (Tn, Tn), 0)
    kj2 = lax.broadcasted_iota(jnp.int32, (Tn, Tn), 1)
    dist_n = qt2 - kj2
    valid_n = dist_n >= 0
    dcf = dist_c.astype(F32)
    dnf = dist_n.astype(F32)
    lane = lax.broadcasted_iota(jnp.int32, (Tn, LANES), 1)

    def body(b, carry):
        rows = pl.ds(pl.multiple_of(b * Tn, Tn), Tn)
        q = q_ref[rows, :]
        kn = kn_ref[rows, :]
        vn = vn_ref[rows, :]
        kc = kc_ref[b]
        vc = vc_ref[b]
        ko_ref[b, 0:W - Tn, :] = kc[Tn:W]
        ko_ref[b, W - Tn:W, :] = kn
        vo_ref[b, 0:W - Tn, :] = vc[Tn:W]
        vo_ref[b, W - Tn:W, :] = vn
        kcd, knd = _dup_halves(kc), _dup_halves(kn)
        vcd, vnd = _dup_halves(vc), _dup_halves(vn)
        outs = []
        for g in range(N_KV):
            qs = _stack_group_queries(q, g)
            nt = (((1,), (1,)), ((), ()))
            sc = lax.dot_general(qs, kcd[g].astype(BF16), nt, preferred_element_type=F32) * (HEAD_DIM ** -0.5)
            sn = lax.dot_general(qs, knd[g].astype(BF16), nt, preferred_element_type=F32) * (HEAD_DIM ** -0.5)
            pcs, pns = [], []
            for hh in range(GROUP):
                h = GROUP * g + hh
                sink = sink_ref[h]
                slope = _alibi_slope(h)
                shc = jnp.where(valid_c, sc[hh * Tn:(hh + 1) * Tn] - slope * dcf, NEG)
                shn = jnp.where(valid_n, sn[hh * Tn:(hh + 1) * Tn] - slope * dnf, NEG)
                mx = jnp.maximum(jnp.maximum(jnp.max(shc, axis=1, keepdims=True),
                                             jnp.max(shn, axis=1, keepdims=True)), sink)
                pc = jnp.exp(shc - mx)
                pn = jnp.exp(shn - mx)
                inv = 1.0 / (jnp.sum(pc, axis=1, keepdims=True) + jnp.sum(pn, axis=1, keepdims=True)
                             + jnp.exp(sink - mx))
                pcs.append((pc * inv).astype(BF16))
                pns.append((pn * inv).astype(BF16))
            o = (jnp.dot(jnp.concatenate(pcs, axis=0), vcd[g].astype(BF16), preferred_element_type=F32)
                 + jnp.dot(jnp.concatenate(pns, axis=0), vnd[g].astype(BF16), preferred_element_type=F32))
            for pp in range(GROUP // 2):
                outs.append(jnp.where(lane < HEAD_DIM, o[(2 * pp) * Tn:(2 * pp + 1) * Tn],
                                      o[(2 * pp + 1) * Tn:(2 * pp + 2) * Tn]))
        o_ref[rows, :] = jnp.concatenate(outs, axis=1)
        return carry

    lax.fori_loop(0, DEC_G, body, 0, unroll=DEC_UNROLL)


def _swa_decode(sinks, qa, ka, va, kc, vc, row0, n_tok):
    nb = kc.shape[0]
    blk0 = row0 // (DEC_G * n_tok)
    tokb = lambda w: pl.BlockSpec((DEC_G * n_tok, w), lambda i: (blk0 + i, 0))
    cache = pl.BlockSpec((DEC_G, WINDOW, DKV), lambda i: (i, 0, 0))
    return pl.pallas_call(
        functools.partial(_swa_decode_kernel, n_tok),
        grid=(nb // DEC_G,),
        in_specs=[pl.BlockSpec(memory_space=pltpu.SMEM), tokb(DM), tokb(DKV), tokb(DKV), cache, cache],
        out_specs=(pl.BlockSpec((DEC_G * n_tok, DM), lambda i: (i, 0)), cache, cache),
        out_shape=(jax.ShapeDtypeStruct((nb * n_tok, DM), F32),
                   jax.ShapeDtypeStruct(kc.shape, F32), jax.ShapeDtypeStruct(vc.shape, F32)),
        compiler_params=_cparams(("arbitrary",), VMEM_LIMIT),
        name="swa_decode",
    )(sinks, qa, ka, va, kc, vc)


def _outproj_router_kernel(n_ptiles, xp_ref, xs_ref, hmp_ref, hap_ref, hms_ref, has_ref, wom_ref, woa_ref,
                           g_ref, wrt_ref, br_ref,
                           x1_ref, xg_ref, meta_ref, cnt_ref):
    i = pl.program_id(0)
    is_p = i < n_ptiles
    x = jnp.where(is_p, xp_ref[...], xs_ref[...])
    hm = jnp.where(is_p, hmp_ref[...], hms_ref[...].astype(BF16))
    ha = jnp.where(is_p, hap_ref[...], has_ref[...].astype(BF16))
    x1 = (x + jnp.dot(hm, wom_ref[...], preferred_element_type=F32)
          + jnp.dot(ha, woa_ref[...], preferred_element_type=F32))
    x1_ref[...] = x1
    h2 = _rms(x1, g_ref[...])
    logits = lax.dot_general(wrt_ref[...], h2, (((1,), (1,)), ((), ())), precision=HIGHEST,
                             preferred_element_type=F32) + br_ref[...]
    eidx = lax.broadcasted_iota(jnp.int32, logits.shape, 0).astype(F32)
    work = logits
    vals, hots = [], []
    for _ in range(TOP_K):
        mv = jnp.max(work, axis=0, keepdims=True)
        sel = jnp.min(jnp.where(work == mv, eidx, float(N_EXPERTS)), axis=0, keepdims=True)
        hot = eidx == sel
        vals.append(mv)
        hots.append(hot)
        work = jnp.where(hot, -jnp.inf, work)
    es = [jnp.exp(v - vals[0]) for v in vals]
    tot = es[0] + es[1] + es[2] + es[3]
    gates = [e / tot for e in es]
    hot_all = jnp.where(hots[0] | hots[1] | hots[2] | hots[3], 1.0, 0.0)
    tm = logits.shape[1]
    su = (lax.broadcasted_iota(jnp.int32, (tm, tm), 0) < lax.broadcasted_iota(jnp.int32, (tm, tm), 1))
    cum = jnp.dot(hot_all.astype(BF16), su.astype(BF16), preferred_element_type=F32)
    cnt = jnp.sum(hot_all, axis=1, keepdims=True)
    cpad = (((cnt.astype(jnp.int32) + (SUBLANES - 1)) // SUBLANES) * SUBLANES).astype(F32)
    lower = (lax.broadcasted_iota(jnp.int32, (N_EXPERTS, N_EXPERTS), 0)
             > lax.broadcasted_iota(jnp.int32, (N_EXPERTS, N_EXPERTS), 1)).astype(F32)
    lstart = jnp.dot(lower, jnp.broadcast_to(cpad, (N_EXPERTS, LANES)), precision=HIGHEST,
                     preferred_element_type=F32)[:, 0:1]
    base = lstart + cum
    lpos = [jnp.sum(jnp.where(hot, base, 0.0), axis=0, keepdims=True) for hot in hots]
    lpi = [p.astype(jnp.int32) for p in lpos]
    r_iota = lax.broadcasted_iota(jnp.int32, (GROUP_R, tm), 0)
    sel01 = jnp.where(r_iota == lpi[0], 1.0, jnp.where(r_iota == lpi[1], 1.0, jnp.where(
        r_iota == lpi[2], 1.0, jnp.where(r_iota == lpi[3], 1.0, 0.0)))).astype(BF16)
    xg_ref[...] = jnp.dot(sel01, h2.astype(BF16), preferred_element_type=F32)
    meta_ref[...] = jnp.transpose(jnp.concatenate(gates + lpos, axis=0))
    cnt_ref[0] = jnp.broadcast_to(cnt, (N_EXPERTS, LANES))


def _outproj_router(xp, xs, hmp, hap, hms, has, wom, woa, g_ffn, wrt, br):
    tp, ts = xp.shape[0], xs.shape[0]
    n_pt, n_st = tp // TM, ts // TM
    t_all = tp + ts
    pblk = lambda w: pl.BlockSpec((TM, w), lambda i: (jnp.minimum(i, n_pt - 1), 0))
    sblk = lambda w: pl.BlockSpec((TM, w), lambda i: (jnp.maximum(i - n_pt, 0), 0))
    full = lambda a: pl.BlockSpec(a.shape, lambda i: (0,) * a.ndim)
    return pl.pallas_call(
        functools.partial(_outproj_router_kernel, n_pt),
        grid=(n_pt + n_st,),
        in_specs=[pblk(D_MODEL), sblk(D_MODEL), pblk(DM), pblk(DM), sblk(DM), sblk(DM),
                  full(wom), full(woa), full(g_ffn), full(wrt), full(br)],
        out_specs=(pl.BlockSpec((TM, D_MODEL), lambda i: (i, 0)),
                   pl.BlockSpec((GROUP_R, D_MODEL), lambda i: (i, 0)),
                   pl.BlockSpec((TM, 2 * TOP_K), lambda i: (i, 0)),
                   pl.BlockSpec((1, N_EXPERTS, LANES), lambda i: (i, 0, 0))),
        out_shape=(jax.ShapeDtypeStruct((t_all, D_MODEL), F32),
                   jax.ShapeDtypeStruct(((n_pt + n_st) * GROUP_R, D_MODEL), F32),
                   jax.ShapeDtypeStruct((t_all, 2 * TOP_K), F32),
                   jax.ShapeDtypeStruct((n_pt + n_st, N_EXPERTS, LANES), F32)),
        compiler_params=_cparams(("arbitrary",), VMEM_LIMIT),
        name="outproj_router",
    )(xp, xs, hmp, hap, hms, has, wom, woa, g_ffn, wrt, br)


def _expert_kernel(be_ref, na_ref, slot_ref, nxt_ref, ctab_ref, ctab_next_ref, xg_ref, wgu_ref, bgu_ref,
                   wd_ref, bd_ref, o_ref, xbuf, wgu_f, wd_f, wgu_s, wd_s, xsem, wsem):
    i = pl.program_id(0)
    na = na_ref[0]

    def x_copies(tab_ref, slot):
        return [pltpu.make_async_copy(xg_ref.at[pl.ds(pl.multiple_of(tab_ref[0, 0, c], SUBLANES), SUBLANES)],
                                      xbuf.at[slot, pl.ds(c * SUBLANES, SUBLANES)], xsem.at[slot])
                for c in range(MOE_BM // SUBLANES)]

    def w_copies(e, slot):
        return [pltpu.make_async_copy(wgu_ref.at[e], wgu_f.at[slot], wsem.at[slot]),
                pltpu.make_async_copy(wd_ref.at[e], wd_f.at[slot], wsem.at[slot])]

    @pl.when(i == 0)
    def _():
        for cp in x_copies(ctab_ref, 0) + w_copies(be_ref[0], 0):
            cp.start()

    @pl.when(i < na)
    def _():
        changed = (i == 0) | (be_ref[i] != be_ref[jnp.maximum(i - 1, 0)])
        wslot = slot_ref[i]
        xslot = lax.rem(i, 2)

        @pl.when(i + 1 < na)
        def _():
            for cp in x_copies(ctab_next_ref, 1 - xslot):
                cp.start()

        @pl.when(changed)
        def _():
            for cp in w_copies(be_ref[i], wslot):
                cp.wait()

            @pl.when(nxt_ref[i] >= 0)
            def _():
                for cp in w_copies(nxt_ref[i], 1 - wslot):
                    cp.start(priority=1)

            wgu_s[...] = wgu_f[wslot].astype(BF16)
            wd_s[...] = wd_f[wslot].astype(BF16)

        for cp in x_copies(ctab_ref, xslot):
            cp.wait()

        x = xbuf[xslot].astype(BF16)
        gu = jnp.dot(x, wgu_s[...], preferred_element_type=F32) + bgu_ref[...]
        gate = jnp.minimum(gu[:, :D_FF], SWIGLU_LIMIT)
        up = jnp.clip(gu[:, D_FF:], -SWIGLU_LIMIT, SWIGLU_LIMIT)
        act = (up + 1.0) * (gate * _sigmoid(gate * SWIGLU_ALPHA))
        o_ref[...] = jnp.dot(act.astype(BF16), wd_s[...], preferred_element_type=F32) + bd_ref[...]


def _expert_ffn(block_exp, n_active, wslot, next_exp, ctab, xg, wgu, bgu, wd, bd):
    n_blocks = ctab.shape[0]
    n_rows = n_blocks * MOE_BM
    nch = MOE_BM // SUBLANES
    grid_spec = pltpu.PrefetchScalarGridSpec(
        num_scalar_prefetch=4,
        grid=(n_blocks,),
        in_specs=[pl.BlockSpec((1, 1, nch), lambda i, *_: (i, 0, 0), memory_space=pltpu.SMEM),
                  pl.BlockSpec((1, 1, nch), lambda i, *_: (jnp.minimum(i + 1, n_blocks - 1), 0, 0),
                               memory_space=pltpu.SMEM),
                  pl.BlockSpec(memory_space=pl.ANY),
                  pl.BlockSpec(memory_space=pl.ANY),
                  pl.BlockSpec((None, 1, 2 * D_FF), lambda i, be, *_: (be[i], 0, 0)),
                  pl.BlockSpec(memory_space=pl.ANY),
                  pl.BlockSpec((None, 1, D_MODEL), lambda i, be, *_: (be[i], 0, 0))],
        out_specs=pl.BlockSpec((MOE_BM, D_MODEL), lambda i, be, na, *_: (jnp.minimum(i, na[0] - 1), 0)),
        scratch_shapes=[pltpu.VMEM((2, MOE_BM, D_MODEL), F32),
                        pltpu.VMEM((2, D_MODEL, 2 * D_FF), F32), pltpu.VMEM((2, D_FF, D_MODEL), F32),
                        pltpu.VMEM((D_MODEL, 2 * D_FF), BF16), pltpu.VMEM((D_FF, D_MODEL), BF16),
                        pltpu.SemaphoreType.DMA((2,)), pltpu.SemaphoreType.DMA((2,))],
    )
    return pl.pallas_call(
        _expert_kernel,
        grid_spec=grid_spec,
        out_shape=jax.ShapeDtypeStruct((n_rows, D_MODEL), F32),
        compiler_params=_cparams(("arbitrary",), VMEM_LIMIT),
        name="moe_experts",
    )(block_exp, n_active, wslot, next_exp, ctab, ctab, xg, wgu, bgu, wd, bd)


def _combine_kernel(n_ptiles, ctab_ref, ctab_next_ref, outs_ref, x1_ref, meta_ref, gf_ref, yp_ref, ys_ref,
                    obuf, sem):
    i = pl.program_id(0)
    n = pl.num_programs(0)
    slot = lax.rem(i, 2)

    def copies(tab_ref, s):
        return [pltpu.make_async_copy(outs_ref.at[pl.ds(pl.multiple_of(tab_ref[0, 0, c], SUBLANES), SUBLANES)],
                                      obuf.at[s, pl.ds(c * SUBLANES, SUBLANES)], sem.at[s])
                for c in range(GROUP_R // SUBLANES)]

    @pl.when(i == 0)
    def _():
        for cp in copies(ctab_ref, 0):
            cp.start()

    @pl.when(i + 1 < n)
    def _():
        for cp in copies(ctab_next_ref, 1 - slot):
            cp.start()

    for cp in copies(ctab_ref, slot):
        cp.wait()

    meta = meta_ref[...]
    tm = meta.shape[0]
    r_iota = lax.broadcasted_iota(jnp.int32, (tm, GROUP_R), 1)
    lp = [meta[:, TOP_K + k:TOP_K + k + 1].astype(jnp.int32) for k in range(TOP_K)]
    gk = [meta[:, k:k + 1] for k in range(TOP_K)]
    gsel = jnp.where(r_iota == lp[0], gk[0], jnp.where(r_iota == lp[1], gk[1], jnp.where(
        r_iota == lp[2], gk[2], jnp.where(r_iota == lp[3], gk[3], 0.0))))
    sel01 = jnp.where(gsel != 0.0, 1.0, 0.0).astype(BF16)
    rg_row = jnp.sum(gsel, axis=0, keepdims=True)
    rg_col = jnp.transpose(jnp.broadcast_to(rg_row, (SUBLANES, GROUP_R)))[:, 0:1]
    og = (obuf[slot] * rg_col).astype(BF16)
    acc = x1_ref[...] + jnp.dot(sel01, og, preferred_element_type=F32)
    y = _rms(acc, gf_ref[...])

    @pl.when(i < n_ptiles)
    def _():
        yp_ref[...] = y

    @pl.when(i >= n_ptiles)
    def _():
        ys_ref[...] = y


def _combine(ctab, outs, x1, meta, g_final, tp, ts):
    n_pt, n_st = tp // TM, ts // TM
    n = n_pt + n_st
    nch = GROUP_R // SUBLANES
    return pl.pallas_call(
        functools.partial(_combine_kernel, n_pt),
        grid=(n,),
        in_specs=[pl.BlockSpec((1, 1, nch), lambda i: (i, 0, 0), memory_space=pltpu.SMEM),
                  pl.BlockSpec((1, 1, nch), lambda i: (jnp.minimum(i + 1, n - 1), 0, 0), memory_space=pltpu.SMEM),
                  pl.BlockSpec(memory_space=pl.ANY),
                  pl.BlockSpec((TM, D_MODEL), lambda i: (i, 0)),
                  pl.BlockSpec((TM, 2 * TOP_K), lambda i: (i, 0)),
                  pl.BlockSpec((1, D_MODEL), lambda i: (0, 0))],
        out_specs=(pl.BlockSpec((TM, D_MODEL), lambda i: (jnp.minimum(i, n_pt - 1), 0)),
                   pl.BlockSpec((TM, D_MODEL), lambda i: (jnp.maximum(i - n_pt, 0), 0))),
        out_shape=(jax.ShapeDtypeStruct((tp, D_MODEL), F32), jax.ShapeDtypeStruct((ts, D_MODEL), F32)),
        scratch_shapes=[pltpu.VMEM((2, GROUP_R, D_MODEL), F32), pltpu.SemaphoreType.DMA((2,))],
        compiler_params=_cparams(("arbitrary",), VMEM_LIMIT),
        name="moe_combine",
    )(ctab, ctab, outs, x1, meta, g_final)


def kernel(x_prompt, x_sample, cache_swa_k, cache_swa_v, state_mlstm_c, state_mlstm_n, state_mlstm_m,
           g_mix, w_in, b_igate, b_fgate, g_head, attn_sinks, w_out, g_ffn, w_router, b_router,
           w_gate_up, b_gate_up, w_down, b_down, g_final):
    assert w_in.shape[0] == 1, "single-layer problem"
    B, S, _ = x_prompt.shape
    Bd, Tn, _ = x_sample.shape
    tp, ts = B * S, Bd * Tn
    t_all = tp + ts
    xp = x_prompt.reshape(tp, D_MODEL)
    xs = x_sample.reshape(ts, D_MODEL)

    w = w_in[0]
    o = np.cumsum([0, DM, DM, DM, DM, N_HEADS, N_HEADS, DM, DKV, DKV])
    col = lambda a: w[:, int(o[a]):int(o[a + 1])]
    w1 = jnp.concatenate([col(0), col(2), col(3), col(6), col(7), col(8)], axis=1).astype(BF16)
    wkt = col(1).T.astype(BF16)
    wgates = jnp.concatenate([col(4), col(5)], axis=1)
    wg = jnp.pad(wgates, ((0, 0), (0, LANES - 2 * N_HEADS))).astype(BF16)
    wgt = wgates.T.astype(BF16)
    bg = jnp.concatenate([b_igate[0], b_fgate[0]]).astype(F32)
    bcol = jnp.pad(bg, (0, LANES - 2 * N_HEADS)).reshape(1, LANES)
    brow = bg.reshape(2 * N_HEADS, 1)

    qm, vm, om, qa, ka, va, kmt, gcol, grow = _inproj(xp, xs, g_mix[0].reshape(1, D_MODEL), w1, wkt, wg, wgt,
                                                      bcol, brow)

    gh = g_head[0].astype(F32)
    sinks = attn_sinks[0].astype(F32)

    hm_p, cbd, m_p = _mlstm_prompt(qm, kmt, vm, om, gcol, grow, gh.reshape(1, DM), B, S)
    ha_p = _swa_prompt(sinks, qa, ka, va, B, S)

    hm_s, c_s, n_s, mt_s = _mlstm_sample(qm, kmt, vm, om, gcol, grow, state_mlstm_c[0], state_mlstm_n[0],
                                         state_mlstm_m[0], gh.reshape(1, DM), tp, Tn)
    n_s = n_s.reshape(Bd, N_HEADS, HEAD_DIM)
    m_s = mt_s[:, :, :LANES // Tn].transpose(0, 2, 1).reshape(Bd, N_HEADS)
    ha_s, k_s, v_s = _swa_decode(sinks, qa, ka, va, cache_swa_k[0].reshape(Bd, WINDOW, DKV),
                                 cache_swa_v[0].reshape(Bd, WINDOW, DKV), tp, Tn)

    wo = w_out[0].astype(BF16)
    x1, xg, meta, cnt = _outproj_router(
        xp, xs, hm_p, ha_p, hm_s, ha_s, wo[:DM], wo[DM:], g_ffn[0].reshape(1, D_MODEL),
        w_router[0].T, b_router[0].reshape(N_EXPERTS, 1))

    i32 = jnp.int32
    n_tiles = t_all // TM
    max_rows = t_all * TOP_K + n_tiles * N_EXPERTS * (SUBLANES - 1) + N_EXPERTS * (MOE_BM - 1)
    n_blocks = -(-max_rows // MOE_BM)
    cpad = (cnt[:, :, 0].astype(i32) + (SUBLANES - 1)) // SUBLANES * SUBLANES
    lstart = jnp.cumsum(cpad, axis=1) - cpad
    goff = jnp.cumsum(cpad, axis=0) - cpad
    padded = (jnp.sum(cpad, axis=0) + MOE_BM - 1) // MOE_BM * MOE_BM
    pad_end = jnp.cumsum(padded)
    seg_begin = (pad_end - padded)[None, :] + goff
    n_active = (pad_end[-1] // MOE_BM).astype(i32)
    blk = jnp.minimum(jnp.arange(n_blocks, dtype=i32), n_active - 1)
    block_exp = jnp.minimum(jnp.sum((pad_end[None, :] <= (blk * MOE_BM)[:, None]).astype(i32), axis=1),
                            N_EXPERTS - 1)
    e_ids = jnp.arange(N_EXPERTS, dtype=i32)
    nonempty = padded > 0
    nxt_e = jnp.min(jnp.where((e_ids[None, :] > e_ids[:, None]) & nonempty[None, :], e_ids[None, :], N_EXPERTS),
                    axis=1)
    nxt_e = jnp.where(nxt_e == N_EXPERTS, -1, nxt_e)
    ord_e = jnp.cumsum(nonempty.astype(i32)) - 1
    be_hot = block_exp[:, None] == e_ids[None, :]
    next_exp = jnp.sum(jnp.where(be_hot, nxt_e[None, :], 0), axis=1).astype(i32)
    wslot = (jnp.sum(jnp.where(be_hot, ord_e[None, :], 0), axis=1) % 2).astype(i32)

    seg_src = jnp.arange(n_tiles, dtype=i32)[:, None] * GROUP_R + lstart
    sb, sl, ss = seg_begin.reshape(-1), cpad.reshape(-1), seg_src.reshape(-1)
    rc = jnp.arange(n_blocks * MOE_BM // SUBLANES, dtype=i32)[:, None] * SUBLANES
    inseg = (sb[None, :] <= rc) & (rc < (sb + sl)[None, :])
    ctab_e = jnp.where(jnp.any(inseg, axis=1), jnp.sum(jnp.where(inseg, (ss - sb)[None, :] + rc, 0), axis=1),
                       GROUP_R - SUBLANES)
    lr = jnp.arange(GROUP_R // SUBLANES, dtype=i32)[None, :, None] * SUBLANES
    inl = (lstart[:, None, :] <= lr) & (lr < (lstart + cpad)[:, None, :])
    ctab_c = jnp.sum(jnp.where(inl, (seg_begin - lstart)[:, None, :] + lr, 0), axis=2)

    outs = _expert_ffn(block_exp, n_active.reshape(1), wslot, next_exp,
                       ctab_e.astype(i32).reshape(n_blocks, 1, MOE_BM // SUBLANES), xg, w_gate_up[0],
                       b_gate_up[0].reshape(N_EXPERTS, 1, 2 * D_FF), w_down[0],
                       b_down[0].reshape(N_EXPERTS, 1, D_MODEL))
    y_p, y_s = _combine(ctab_c.astype(i32).reshape(n_tiles, 1, GROUP_R // SUBLANES), outs, x1, meta,
                        g_final.reshape(1, D_MODEL), tp, ts)

    kv_tail = lambda a: a[:tp].reshape(B, S, N_KV, HEAD_DIM)[:, S - WINDOW:][None]
    c_e = cbd[:, :, :HEAD_DIM, :HEAD_DIM]
    c_o = cbd[:, :, HEAD_DIM:, HEAD_DIM:LANES]
    c_p = jnp.stack([c_e, c_o], axis=2).reshape(B, N_HEADS, HEAD_DIM, HEAD_DIM)
    n_p = jnp.stack([cbd[:, :, :HEAD_DIM, LANES], cbd[:, :, HEAD_DIM:, LANES + 1]], axis=2).reshape(B, N_HEADS, HEAD_DIM)
    return (y_p.reshape(B, S, D_MODEL), y_s.reshape(Bd, Tn, D_MODEL),
            kv_tail(ka), kv_tail(va), c_p[None], n_p[None], m_p[:, :, 0][None],
            k_s.reshape(Bd, WINDOW, N_KV, HEAD_DIM)[None], v_s.reshape(Bd, WINDOW, N_KV, HEAD_DIM)[None],
            c_s[None], n_s[None], m_s[None])
```

```python
import functools
import math

import jax
import jax.numpy as jnp
import numpy as np
from jax import lax
from jax.experimental import pallas as pl
from jax.experimental.pallas import tpu as pltpu

F32 = jnp.float32
BF16 = jnp.bfloat16
HIGHEST = lax.Precision.HIGHEST

D_MODEL = 1024
HEAD_DIM = 64
N_HEADS = 8
N_PAIRS = N_HEADS // 2
N_KV = 2
GROUP = N_HEADS // N_KV
WINDOW = 128
N_EXPERTS = 32
TOP_K = 4
D_FF = 1024
SWIGLU_LIMIT = 7.0
SWIGLU_ALPHA = 1.702
RMS_EPS = 1e-5
DM = N_HEADS * HEAD_DIM
DKV = N_KV * HEAD_DIM
NEG = -1e30

LANES = 128
SUBLANES = 8
VMEM_LIMIT = 56 * 1024 * 1024

TM = 512
MLSTM_TL = 512
MLSTM_L = 128
SWA_BLOCKS = 4
DEC_G = 8
DEC_UNROLL = 4
MOE_BM = 256
GROUP_R = -(-(TOP_K * TM + N_EXPERTS * (SUBLANES - 1) + SUBLANES) // LANES) * LANES


def _cparams(sem, vmem=None):
    return pltpu.CompilerParams(dimension_semantics=sem, vmem_limit_bytes=vmem)


def _rms(x, g):
    return x * lax.rsqrt(jnp.mean(x * x, axis=-1, keepdims=True) + RMS_EPS) * g


def _log_sigmoid(z):
    return jnp.minimum(z, 0.0) - jnp.log(1.0 + jnp.exp(-jnp.abs(z)))


def _sigmoid(z):
    return 1.0 / (1.0 + jnp.exp(-z))


def _inproj_kernel(n_ptiles, xp_ref, xs_ref, g_ref, w1_ref, wt_ref, wg_ref, wgt_ref, bcol_ref, brow_ref,
                   km_ref, qa_ref, ka_ref, va_ref, qmt_ref, vmt_ref, omt_ref, gcol_ref, grow_ref):
    i = pl.program_id(0)
    x = jnp.where(i < n_ptiles, xp_ref[...], xs_ref[...])
    h = _rms(x, g_ref[...]).astype(BF16)
    main = jnp.dot(h, w1_ref[...], preferred_element_type=F32)
    km_ref[...] = main[:, 0:DM] * (HEAD_DIM ** -0.5)
    qa_ref[...] = main[:, DM:2 * DM]
    ka_ref[...] = main[:, 2 * DM:2 * DM + DKV]
    va_ref[...] = main[:, 2 * DM + DKV:2 * DM + 2 * DKV]
    t = lax.dot_general(wt_ref[...], h, (((1,), (1,)), ((), ())), preferred_element_type=F32)
    qmt_ref[...] = t[0:DM]
    vmt_ref[...] = t[DM:2 * DM]
    omt_ref[...] = t[2 * DM:3 * DM]
    zc = jnp.dot(h, wg_ref[...], preferred_element_type=F32) + bcol_ref[...]
    lane = lax.broadcasted_iota(jnp.int32, zc.shape, 1)
    gcol_ref[...] = jnp.where(lane < N_HEADS, zc, _log_sigmoid(zc))
    zr = lax.dot_general(wgt_ref[...], h, (((1,), (1,)), ((), ())), preferred_element_type=F32) + brow_ref[...]
    row = lax.broadcasted_iota(jnp.int32, zr.shape, 0)
    grow_ref[...] = jnp.where(row < N_HEADS, zr, _log_sigmoid(zr))


def _inproj(xp, xs, g_mix, w1, wkt, wg, wgt, bcol, brow):
    tp, ts = xp.shape[0], xs.shape[0]
    n_pt, n_st = tp // TM, ts // TM
    t_all = tp + ts
    tok = lambda w: pl.BlockSpec((TM, w), lambda i: (i, 0))
    tr = lambda r: pl.BlockSpec((r, TM), lambda i: (0, i))
    full = lambda a: pl.BlockSpec(a.shape, lambda i: (0,) * a.ndim)
    out_shape = (
        jax.ShapeDtypeStruct((t_all, DM), F32), jax.ShapeDtypeStruct((t_all, DM), F32),
        jax.ShapeDtypeStruct((t_all, DKV), F32), jax.ShapeDtypeStruct((t_all, DKV), F32),
        jax.ShapeDtypeStruct((DM, t_all), F32), jax.ShapeDtypeStruct((DM, t_all), F32),
        jax.ShapeDtypeStruct((DM, t_all), F32),
        jax.ShapeDtypeStruct((t_all, LANES), F32), jax.ShapeDtypeStruct((2 * N_HEADS, t_all), F32),
    )
    return pl.pallas_call(
        functools.partial(_inproj_kernel, n_pt),
        grid=(n_pt + n_st,),
        in_specs=[
            pl.BlockSpec((TM, D_MODEL), lambda i: (jnp.minimum(i, n_pt - 1), 0)),
            pl.BlockSpec((TM, D_MODEL), lambda i: (jnp.maximum(i - n_pt, 0), 0)),
            full(g_mix), full(w1), full(wkt), full(wg), full(wgt), full(bcol), full(brow),
        ],
        out_specs=(tok(DM), tok(DM), tok(DKV), tok(DKV), tr(DM), tr(DM), tr(DM),
                   tok(LANES), tr(2 * N_HEADS)),
        out_shape=out_shape,
        compiler_params=_cparams(("arbitrary",), VMEM_LIMIT),
        name="inproj",
    )(xp, xs, g_mix, w1, wkt, wg, wgt, bcol, brow)


CT_ROWS = LANES + 2 * SUBLANES


def _cumsum_rows(x, n):
    row = lax.broadcasted_iota(jnp.int32, x.shape, 0)
    sh = 1
    while sh < n:
        x = x + jnp.where(row >= sh, pltpu.roll(x, sh, axis=0), 0.0)
        sh *= 2
    return x


def _mlstm_prompt_kernel(km_ref, qmt_ref, vmt_ref, omt_ref, gcol_ref, grow_ref, ghr_ref,
                         hm_ref, ct_ref, m_ref, ct_s, m_s):
    j = pl.program_id(1)
    L = MLSTM_L
    assert L == LANES

    @pl.when(j == 0)
    def _():
        ct_s[...] = jnp.zeros_like(ct_s)
        m_s[...] = jnp.zeros_like(m_s)

    si = lax.broadcasted_iota(jnp.int32, (L, L), 0)
    ti = lax.broadcasted_iota(jnp.int32, (L, L), 1)
    causal_t = si <= ti
    upper = jnp.where(causal_t, 1.0, 0.0)
    rows_c = lax.broadcasted_iota(jnp.int32, (CT_ROWS, 1), 0)
    rmask_e = (rows_c < HEAD_DIM) | (rows_c == LANES)
    rmask_o = ((rows_c >= HEAD_DIM) & (rows_c < LANES)) | (rows_c == LANES + 1)
    rows_e = lax.broadcasted_iota(jnp.int32, (LANES, 1), 0) < HEAD_DIM
    cols_e = lax.broadcasted_iota(jnp.int32, (1, LANES), 1) < HEAD_DIM
    bdt_mask = (rmask_e & cols_e) | (rmask_o & (~cols_e))
    ones_rows = jnp.where(lax.broadcasted_iota(jnp.int32, (CT_ROWS - LANES, L), 0) < 2, 1.0, 0.0)

    for c in range(MLSTM_TL // L):
        sl = slice(c * L, (c + 1) * L)
        grow = grow_ref[:, sl]
        i_row = grow[0:N_HEADS]
        b_row = jnp.dot(grow[N_HEADS:2 * N_HEADS], upper, precision=HIGHEST,
                        preferred_element_type=F32)
        gc = gcol_ref[sl, :]
        bc_all = _cumsum_rows(gc, L)
        for p in range(N_PAIRS):
            ls = slice(p * LANES, (p + 1) * LANES)
            k2 = km_ref[sl, ls]
            qt2 = qmt_ref[ls, sl]
            qt_e = jnp.where(rows_e, qt2, 0.0).astype(BF16)
            qt_o = jnp.where(rows_e, 0.0, qt2).astype(BF16)
            st2 = jnp.dot(k2.astype(BF16), jnp.concatenate([qt_e, qt_o], axis=1),
                          preferred_element_type=F32)
            ct = ct_s[p]
            rqt = jnp.dot(ct.astype(BF16), qt2.astype(BF16), preferred_element_type=F32)
            pts, mts, inters, wreps, decays, mnews = [], [], [], [], [], []
            for hh in range(2):
                h = 2 * p + hh
                cvec = jnp.broadcast_to(gc[:, h:h + 1] - bc_all[:, N_HEADS + h:N_HEADS + h + 1], (L, L))
                brow = b_row[h:h + 1, :]
                logdt = jnp.where(causal_t, cvec + brow, NEG)
                m_prev = m_s[h:h + 1, 0:1]
                m_inter = m_prev + brow
                m_t = jnp.maximum(m_inter, jnp.max(logdt, axis=0, keepdims=True))
                pts.append((st2[:, hh * L:(hh + 1) * L] * jnp.exp(logdt - m_t)).astype(BF16))
                mts.append(m_t)
                inters.append(jnp.exp(m_inter - m_t))
                m_new = m_t[:, L - 1:L]
                b_last = brow[:, L - 1:L]
                decays.append(jnp.exp(m_prev + b_last - m_new))
                wreps.append(jnp.exp(cvec + (b_last - m_new)))
                mnews.append(m_new)
            vext = jnp.concatenate([vmt_ref[ls, sl], ones_rows], axis=0)
            lhs = jnp.concatenate([jnp.where(rmask_e, vext, 0.0), jnp.where(rmask_o, vext, 0.0)],
                                  axis=1).astype(BF16)
            rt = jnp.dot(lhs, jnp.concatenate(pts, axis=0), preferred_element_type=F32)
            ndt = rt + jnp.where(rmask_e, inters[0], inters[1]) * rqt
            den = jnp.where(rows_e, ndt[LANES:LANES + 1], ndt[LANES + 1:LANES + 2])
            mt2 = jnp.where(rows_e, mts[0], mts[1])
            hvt = ndt[0:LANES] / jnp.maximum(jnp.abs(den), jnp.exp(-mt2))
            sq = hvt * hvt
            ms = jnp.where(rows_e, jnp.sum(sq[0:HEAD_DIM], axis=0, keepdims=True),
                           jnp.sum(sq[HEAD_DIM:LANES], axis=0, keepdims=True)) * (1.0 / HEAD_DIM)
            yt = hvt * lax.rsqrt(ms + RMS_EPS) * ghr_ref[ls, :] * _sigmoid(omt_ref[ls, sl])
            hm_ref[sl, ls] = jnp.transpose(yt).astype(hm_ref.dtype)
            kw = (k2 * jnp.where(cols_e, wreps[0], wreps[1])).astype(BF16)
            upd = jnp.dot(vext.astype(BF16), kw, preferred_element_type=F32)
            ct_s[p] = jnp.where(rmask_e, decays[0], decays[1]) * ct + jnp.where(bdt_mask, upd, 0.0)
            for hh in range(2):
                h = 2 * p + hh
                m_s[h:h + 1, :] = jnp.broadcast_to(mnews[hh], (1, LANES))

    @pl.when(j == pl.num_programs(1) - 1)
    def _():
        ct_ref[0] = ct_s[...]
        m_ref[0] = m_s[...]


def _mlstm_prompt(km, qmt, vmt, omt, gcol, grow, ghr, batch, seq):
    nt = seq // MLSTM_TL
    tokb = lambda w: pl.BlockSpec((MLSTM_TL, w), lambda b, j: (b * nt + j, 0))
    rowb = lambda r: pl.BlockSpec((r, MLSTM_TL), lambda b, j: (0, b * nt + j))
    return pl.pallas_call(
        _mlstm_prompt_kernel,
        grid=(batch, nt),
        in_specs=[tokb(DM), rowb(DM), rowb(DM), rowb(DM), tokb(LANES), rowb(2 * N_HEADS),
                  pl.BlockSpec((DM, LANES), lambda b, j: (0, 0))],
        out_specs=(tokb(DM),
                   pl.BlockSpec((1, N_PAIRS, CT_ROWS, LANES), lambda b, j: (b, 0, 0, 0)),
                   pl.BlockSpec((1, N_HEADS, LANES), lambda b, j: (b, 0, 0))),
        out_shape=(jax.ShapeDtypeStruct((batch * seq, DM), BF16),
                   jax.ShapeDtypeStruct((batch, N_PAIRS, CT_ROWS, LANES), F32),
                   jax.ShapeDtypeStruct((batch, N_HEADS, LANES), F32)),
        scratch_shapes=[pltpu.VMEM((N_PAIRS, CT_ROWS, LANES), F32), pltpu.VMEM((N_HEADS, LANES), F32)],
        compiler_params=_cparams(("arbitrary", "arbitrary"), VMEM_LIMIT),
        name="mlstm_prompt",
    )(km, qmt, vmt, omt, gcol, grow, ghr)


def _mlstm_sample_kernel(n_tok, km_ref, qmt_ref, vmt_ref, omt_ref, gcol_ref, grow_ref, c0_ref, n0_ref, m0_ref,
                         m0t_ref, gh_ref, hm_ref, c_ref, n_ref, mt_ref):
    L = LANES
    NB = L // n_tok
    ti = lax.broadcasted_iota(jnp.int32, (L, L), 0)
    si = lax.broadcasted_iota(jnp.int32, (L, L), 1)
    same = (ti // n_tok) == (si // n_tok)
    causal = same & (ti >= si)
    useg = jnp.where(same & (ti <= si), 1.0, 0.0)
    slast = jnp.where(same & (ti % n_tok == n_tok - 1), 1.0, 0.0)
    expand = jnp.where(ti // n_tok == si, 1.0, 0.0)
    expand_t = jnp.where(ti == si // n_tok, 1.0, 0.0)
    pick = jnp.where((ti // n_tok == si) & (ti % n_tok == n_tok - 1), 1.0, 0.0)
    hdot = lambda a, b: jnp.dot(a, b, precision=HIGHEST, preferred_element_type=F32)

    lane128 = lax.broadcasted_iota(jnp.int32, (L, LANES), 1)
    even128 = lane128 < HEAD_DIM
    lane256 = lax.broadcasted_iota(jnp.int32, (1, 2 * LANES), 1)
    cols_e = (lane256 < HEAD_DIM) | (lane256 == LANES)
    cols_o = ((lane256 >= HEAD_DIM) & (lane256 < LANES)) | (lane256 == LANES + 1)
    rows_e = lax.broadcasted_iota(jnp.int32, (LANES, 1), 0) < HEAD_DIM
    ones_cols = jnp.where(lane128 < 2, 1.0, 0.0)
    bo_r = lax.broadcasted_iota(jnp.int32, (LANES, LANES), 0) // HEAD_DIM
    bo_c = lax.broadcasted_iota(jnp.int32, (LANES, LANES), 1) // HEAD_DIM
    block_ones = jnp.where(bo_r == bo_c, 1.0, 0.0)
    W = NB * LANES
    rb = lax.broadcasted_iota(jnp.int32, (L, W), 0)
    cb = lax.broadcasted_iota(jnp.int32, (L, W), 1)
    own_block = (rb // n_tok) == (cb // LANES)
    bd_tiled = (rb // HEAD_DIM) == ((cb % LANES) // HEAD_DIM)

    grow = grow_ref[...]
    i_row = grow[0:N_HEADS]
    b_row = hdot(grow[N_HEADS:2 * N_HEADS], useg)
    b_last = hdot(b_row, slast)
    a_row = b_last - b_row + i_row
    pos = lax.broadcasted_iota(jnp.int32, a_row.shape, 1) % n_tok
    pm = a_row
    sh = 1
    while sh < n_tok:
        pm = jnp.where(pos >= sh, jnp.maximum(pm, pltpu.roll(pm, sh, axis=1)), pm)
        sh *= 2
    m_carry = hdot(jnp.concatenate([m0t_ref[0], b_row], axis=1), jnp.concatenate([expand_t, slast], axis=0))
    m_new_row = jnp.maximum(m_carry, hdot(pm, slast))
    decay_row = jnp.exp(m_carry - m_new_row)
    w_row = jnp.exp(a_row - m_new_row)
    mt_ref[0] = hdot(m_new_row, pick)
    decay_bh = hdot(decay_row, pick)
    decay_hb = jnp.transpose(decay_bh)[0:NB]

    bc_all = gcol_ref[...]
    rowpos = lax.broadcasted_iota(jnp.int32, bc_all.shape, 0) % n_tok
    sh = 1
    while sh < n_tok:
        bc_all = bc_all + jnp.where(rowpos >= sh, pltpu.roll(bc_all, sh, axis=0), 0.0)
        sh *= 2
    pad_rows = lambda a: jnp.concatenate([a, jnp.zeros((L - NB, a.shape[1]), F32)], axis=0)
    m0_col = hdot(expand, pad_rows(m0_ref[...]))
    rowv_all = i_row - b_row

    for p in range(N_PAIRS):
        ls = slice(p * LANES, (p + 1) * LANES)
        q2f = jnp.transpose(qmt_ref[ls, :])
        q2 = q2f.astype(BF16)
        kt2 = jnp.transpose(km_ref[:, ls])
        v2 = jnp.transpose(vmt_ref[ls, :])
        vext = jnp.concatenate([v2, ones_cols], axis=1)
        kt_e = jnp.where(rows_e, kt2, 0.0).astype(BF16)
        kt_o = jnp.where(rows_e, 0.0, kt2).astype(BF16)
        s2 = jnp.dot(q2, jnp.concatenate([kt_e, kt_o], axis=1), preferred_element_type=F32)
        ps, mts, inters = [], [], []
        for hh in range(2):
            h = 2 * p + hh
            bcol = bc_all[:, N_HEADS + h:N_HEADS + h + 1]
            logd = jnp.where(causal, bcol + rowv_all[h:h + 1, :], NEG)
            m_inter = m0_col[:, h:h + 1] + bcol
            m_t = jnp.maximum(m_inter, jnp.max(logd, axis=1, keepdims=True))
            ps.append((s2[:, hh * L:(hh + 1) * L] * jnp.exp(logd - m_t)).astype(BF16))
            mts.append(m_t)
            inters.append(jnp.exp(m_inter - m_t))
        vstack = jnp.concatenate([jnp.where(cols_e, vext, 0.0), jnp.where(cols_o, vext, 0.0)],
                                 axis=0).astype(BF16)
        r = jnp.dot(jnp.concatenate(ps, axis=1), vstack, preferred_element_type=F32)
        zero = jnp.zeros((HEAD_DIM, HEAD_DIM), F32)
        cstack = jnp.concatenate(
            [jnp.concatenate([jnp.concatenate([c0_ref[b, 2 * p], zero], axis=1),
                              jnp.concatenate([zero, c0_ref[b, 2 * p + 1]], axis=1)], axis=0)
             for b in range(NB)], axis=1)
        rq_all = jnp.where(own_block, jnp.dot(q2, cstack.astype(BF16), preferred_element_type=F32), 0.0)
        rq = rq_all[:, 0:LANES]
        for b in range(1, NB):
            rq = rq + rq_all[:, b * LANES:(b + 1) * LANES]
        n_rows = hdot(expand, pad_rows(n0_ref[:, ls]))
        qn = hdot(q2f * n_rows, block_ones)
        inter2 = jnp.where(even128, inters[0], inters[1])
        num = r[:, 0:LANES] + inter2 * rq
        den = jnp.where(even128, r[:, LANES:LANES + 1], r[:, LANES + 1:LANES + 2]) + inter2 * qn
        mt2 = jnp.where(even128, mts[0], mts[1])
        hv = num / jnp.maximum(jnp.abs(den), jnp.exp(-mt2))
        ms = hdot(hv * hv, block_ones) * (1.0 / HEAD_DIM)
        y = hv * lax.rsqrt(ms + RMS_EPS) * gh_ref[:, ls] * _sigmoid(jnp.transpose(omt_ref[ls, :]))
        hm_ref[:, ls] = y.astype(hm_ref.dtype)
        kw = kt2 * jnp.where(rows_e, w_row[2 * p:2 * p + 1, :], w_row[2 * p + 1:2 * p + 2, :])
        vbd = jnp.where(own_block, jnp.concatenate([v2] * NB, axis=1), 0.0).astype(BF16)
        upd = jnp.dot(kw.astype(BF16), vbd, preferred_element_type=F32)
        upd = jnp.where(bd_tiled, upd, 0.0)
        for b in range(NB):
            bs = slice(b * LANES, (b + 1) * LANES)
            dec_b = jnp.where(rows_e, decay_hb[b:b + 1, 2 * p:2 * p + 1], decay_hb[b:b + 1, 2 * p + 1:2 * p + 2])
            cnew = dec_b * cstack[:, bs] + upd[:, bs]
            c_ref[b, 2 * p] = cnew[0:HEAD_DIM, 0:HEAD_DIM]
            c_ref[b, 2 * p + 1] = cnew[HEAD_DIM:LANES, HEAD_DIM:LANES]
        nsum = jnp.transpose(hdot(kw, expand))[0:NB]
        dec_n = jnp.where(lax.broadcasted_iota(jnp.int32, (NB, LANES), 1) < HEAD_DIM,
                          decay_hb[:, 2 * p:2 * p + 1], decay_hb[:, 2 * p + 1:2 * p + 2])
        n_ref[:, ls] = dec_n * n0_ref[:, ls] + nsum


def _mlstm_sample(km, qmt, vmt, omt, gcol, grow, c0, n0, m0, gh, row0, n_tok):
    nb = c0.shape[0]
    g_nb = LANES // n_tok
    n_g = nb // g_nb
    blk0 = row0 // LANES
    tokb = lambda w: pl.BlockSpec((LANES, w), lambda i: (blk0 + i, 0))
    rowb = lambda r: pl.BlockSpec((r, LANES), lambda i: (0, blk0 + i))
    m0t = jnp.pad(m0.reshape(n_g, g_nb, N_HEADS).transpose(0, 2, 1),
                  ((0, 0), (0, 0), (0, LANES - g_nb)))
    return pl.pallas_call(
        functools.partial(_mlstm_sample_kernel, n_tok),
        grid=(n_g,),
        in_specs=[tokb(DM), rowb(DM), rowb(DM), rowb(DM), tokb(LANES), rowb(2 * N_HEADS),
                  pl.BlockSpec((g_nb, N_HEADS, HEAD_DIM, HEAD_DIM), lambda i: (i, 0, 0, 0)),
                  pl.BlockSpec((g_nb, DM), lambda i: (i, 0)),
                  pl.BlockSpec((g_nb, N_HEADS), lambda i: (i, 0)),
                  pl.BlockSpec((1, N_HEADS, LANES), lambda i: (i, 0, 0)),
                  pl.BlockSpec((1, DM), lambda i: (0, 0))],
        out_specs=(pl.BlockSpec((LANES, DM), lambda i: (i, 0)),
                   pl.BlockSpec((g_nb, N_HEADS, HEAD_DIM, HEAD_DIM), lambda i: (i, 0, 0, 0)),
                   pl.BlockSpec((g_nb, DM), lambda i: (i, 0)),
                   pl.BlockSpec((1, N_HEADS, LANES), lambda i: (i, 0, 0))),
        out_shape=(jax.ShapeDtypeStruct((nb * n_tok, DM), BF16), jax.ShapeDtypeStruct(c0.shape, F32),
                   jax.ShapeDtypeStruct((nb, DM), F32), jax.ShapeDtypeStruct((n_g, N_HEADS, LANES), F32)),
        compiler_params=_cparams(("arbitrary",), VMEM_LIMIT),
        name="mlstm_sample",
    )(km, qmt, vmt, omt, gcol, grow, c0, n0.reshape(nb, DM), m0, m0t, gh)


def _alibi_slope(h):
    return float(np.float32(2.0 ** (-8.0 * (h + 1) / N_HEADS)))


def _dup_halves(x):
    lane = lax.broadcasted_iota(jnp.int32, x.shape, 1)
    xr = pltpu.roll(x, HEAD_DIM, axis=1)
    lo = lane < HEAD_DIM
    return jnp.where(lo, x, xr), jnp.where(lo, xr, x)


def _stack_group_queries(q, g):
    lane = lax.broadcasted_iota(jnp.int32, (q.shape[0], LANES), 1)
    parts = []
    for hh in range(GROUP):
        h = GROUP * g + hh
        blk = q[:, (h // 2) * LANES:(h // 2 + 1) * LANES]
        keep = (lane < HEAD_DIM) if h % 2 == 0 else (lane >= HEAD_DIM)
        parts.append(jnp.where(keep, blk, 0.0))
    return jnp.concatenate(parts, axis=0).astype(BF16)


def _swa_prompt_kernel(sink_ref, q_ref, kp_ref, ko_ref, vp_ref, vo_ref, o_ref):
    j = pl.program_id(1)
    R = WINDOW
    qi = lax.broadcasted_iota(jnp.int32, (R, R), 0)
    kj = lax.broadcasted_iota(jnp.int32, (R, R), 1)
    own = kj <= qi
    distf = jnp.where(own, qi - kj, qi - kj + R).astype(F32)
    lane = lax.broadcasted_iota(jnp.int32, (R, LANES), 1)
    for u in range(SWA_BLOCKS):
        rs = slice(u * R, (u + 1) * R)
        q = q_ref[rs, :]
        k_prev = kp_ref[...] if u == 0 else ko_ref[(u - 1) * R:u * R, :]
        v_prev = vp_ref[...] if u == 0 else vo_ref[(u - 1) * R:u * R, :]
        kd = _dup_halves(jnp.concatenate([k_prev, ko_ref[rs, :]], axis=0))
        vd = _dup_halves(jnp.concatenate([v_prev, vo_ref[rs, :]], axis=0))
        valid = (own | (j > 0)) if u == 0 else None
        outs = []
        for g in range(N_KV):
            qs = _stack_group_queries(q, g)
            s = lax.dot_general(qs, kd[g].astype(BF16), (((1,), (1,)), ((), ())),
                                preferred_element_type=F32) * (HEAD_DIM ** -0.5)
            ps = []
            for hh in range(GROUP):
                h = GROUP * g + hh
                sink = sink_ref[h]
                rows = slice(hh * R, (hh + 1) * R)
                sh = jnp.where(own, s[rows, R:], s[rows, :R]) - _alibi_slope(h) * distf
                if valid is not None:
                    sh = jnp.where(valid, sh, NEG)
                mx = jnp.maximum(jnp.max(sh, axis=1, keepdims=True), sink)
                p = jnp.exp(sh - mx)
                p = p / (jnp.sum(p, axis=1, keepdims=True) + jnp.exp(sink - mx))
                ps.append(jnp.concatenate([jnp.where(own, 0.0, p), jnp.where(own, p, 0.0)], axis=1).astype(BF16))
            o = jnp.dot(jnp.concatenate(ps, axis=0), vd[g].astype(BF16), preferred_element_type=F32)
            for pp in range(GROUP // 2):
                outs.append(jnp.where(lane < HEAD_DIM, o[(2 * pp) * R:(2 * pp + 1) * R],
                                      o[(2 * pp + 1) * R:(2 * pp + 2) * R]))
        o_ref[rs, :] = jnp.concatenate(outs, axis=1).astype(o_ref.dtype)


def _swa_prompt(sinks, qa, ka, va, batch, seq):
    nb = seq // (WINDOW * SWA_BLOCKS)
    own = lambda w: pl.BlockSpec((WINDOW * SWA_BLOCKS, w), lambda b, j: (b * nb + j, 0))
    prev = lambda w: pl.BlockSpec(
        (WINDOW, w), lambda b, j: (b * nb * SWA_BLOCKS + jnp.maximum(j * SWA_BLOCKS - 1, 0), 0))
    return pl.pallas_call(
        _swa_prompt_kernel,
        grid=(batch, nb),
        in_specs=[pl.BlockSpec(memory_space=pltpu.SMEM), own(DM), prev(DKV), own(DKV), prev(DKV), own(DKV)],
        out_specs=own(DM),
        out_shape=jax.ShapeDtypeStruct((batch * seq, DM), BF16),
        compiler_params=_cparams(("arbitrary", "arbitrary"), VMEM_LIMIT),
        name="swa_prompt",
    )(sinks, qa, ka, ka, va, va)


def _swa_decode_kernel(n_tok, sink_ref, q_ref, kn_ref, vn_ref, kc_ref, vc_ref, o_ref, ko_ref, vo_ref):
    W = WINDOW
    Tn = n_tok
    qt = lax.broadcasted_iota(jnp.int32, (Tn, W), 0)
    kj = lax.broadcasted_iota(jnp.int32, (Tn, W), 1)
    dist_c = qt + W - kj
    valid_c = dist_c < WINDOW
    qt2 = lax.broadcasted_iota(jnp.int32, (Tn, Tn), 0)
    kj2 = lax.broadcasted_iota(jnp.int32, (Tn, Tn), 1)
    dist_n = qt2 - kj2
    valid_n = dist_n >= 0
    dcf = dist_c.astype(F32)
    dnf = dist_n.astype(F32)
    lane = lax.broadcasted_iota(jnp.int32, (Tn, LANES), 1)

    def body(b, carry):
        rows = pl.ds(pl.multiple_of(b * Tn, Tn), Tn)
        q = q_ref[rows, :]
        kn = kn_ref[rows, :]
        vn = vn_ref[rows, :]
        kc = kc_ref[b]
        vc = vc_ref[b]
        ko_ref[b, 0:W - Tn, :] = kc[Tn:W]
        ko_ref[b, W - Tn:W, :] = kn
        vo_ref[b, 0:W - Tn, :] = vc[Tn:W]
        vo_ref[b, W - Tn:W, :] = vn
        kcd, knd = _dup_halves(kc), _dup_halves(kn)
        vcd, vnd = _dup_halves(vc), _dup_halves(vn)
        outs = []
        for g in range(N_KV):
            qs = _stack_group_queries(q, g)
            nt = (((1,), (1,)), ((), ()))
            sc = lax.dot_general(qs, kcd[g].astype(BF16), nt, preferred_element_type=F32) * (HEAD_DIM ** -0.5)
            sn = lax.dot_general(qs, knd[g].astype(BF16), nt, preferred_element_type=F32) * (HEAD_DIM ** -0.5)
            pcs, pns = [], []
            for hh in range(GROUP):
                h = GROUP * g + hh
                sink = sink_ref[h]
                slope = _alibi_slope(h)
                shc = jnp.where(valid_c, sc[hh * Tn:(hh + 1) * Tn] - slope * dcf, NEG)
                shn = jnp.where(valid_n, sn[hh * Tn:(hh + 1) * Tn] - slope * dnf, NEG)
                mx = jnp.maximum(jnp.maximum(jnp.max(shc, axis=1, keepdims=True),
                                             jnp.max(shn, axis=1, keepdims=True)), sink)
                pc = jnp.exp(shc - mx)
                pn = jnp.exp(shn - mx)
                inv = 1.0 / (jnp.sum(pc, axis=1, keepdims=True) + jnp.sum(pn, axis=1, keepdims=True)
                             + jnp.exp(sink - mx))
                pcs.append((pc * inv).astype(BF16))
                pns.append((pn * inv).astype(BF16))
            o = (jnp.dot(jnp.concatenate(pcs, axis=0), vcd[g].astype(BF16), preferred_element_type=F32)
                 + jnp.dot(jnp.concatenate(pns, axis=0), vnd[g].astype(BF16), preferred_element_type=F32))
            for pp in range(GROUP // 2):
                outs.append(jnp.where(lane < HEAD_DIM, o[(2 * pp) * Tn:(2 * pp + 1) * Tn],
                                      o[(2 * pp + 1) * Tn:(2 * pp + 2) * Tn]))
        o_ref[rows, :] = jnp.concatenate(outs, axis=1)
        return carry

    lax.fori_loop(0, DEC_G, body, 0, unroll=DEC_UNROLL)


def _swa_decode(sinks, qa, ka, va, kc, vc, row0, n_tok):
    nb = kc.shape[0]
    blk0 = row0 // (DEC_G * n_tok)
    tokb = lambda w: pl.BlockSpec((DEC_G * n_tok, w), lambda i: (blk0 + i, 0))
    cache = pl.BlockSpec((DEC_G, WINDOW, DKV), lambda i: (i, 0, 0))
    return pl.pallas_call(
        functools.partial(_swa_decode_kernel, n_tok),
        grid=(nb // DEC_G,),
        in_specs=[pl.BlockSpec(memory_space=pltpu.SMEM), tokb(DM), tokb(DKV), tokb(DKV), cache, cache],
        out_specs=(pl.BlockSpec((DEC_G * n_tok, DM), lambda i: (i, 0)), cache, cache),
        out_shape=(jax.ShapeDtypeStruct((nb * n_tok, DM), F32),
                   jax.ShapeDtypeStruct(kc.shape, F32), jax.ShapeDtypeStruct(vc.shape, F32)),
        compiler_params=_cparams(("arbitrary",), VMEM_LIMIT),
        name="swa_decode",
    )(sinks, qa, ka, va, kc, vc)


def _outproj_router_kernel(n_ptiles, xp_ref, xs_ref, hmp_ref, hap_ref, hms_ref, has_ref, wom_ref, woa_ref,
                           g_ref, wrt_ref, br_ref,
                           x1_ref, xg_ref, meta_ref, cnt_ref):
    i = pl.program_id(0)
    is_p = i < n_ptiles
    x = jnp.where(is_p, xp_ref[...], xs_ref[...])
    hm = jnp.where(is_p, hmp_ref[...], hms_ref[...].astype(BF16))
    ha = jnp.where(is_p, hap_ref[...], has_ref[...].astype(BF16))
    x1 = (x + jnp.dot(hm, wom_ref[...], preferred_element_type=F32)
          + jnp.dot(ha, woa_ref[...], preferred_element_type=F32))
    x1_ref[...] = x1
    h2 = _rms(x1, g_ref[...])
    logits = lax.dot_general(wrt_ref[...], h2, (((1,), (1,)), ((), ())), precision=HIGHEST,
                             preferred_element_type=F32) + br_ref[...]
    eidx = lax.broadcasted_iota(jnp.int32, logits.shape, 0).astype(F32)
    work = logits
    vals, hots = [], []
    for _ in range(TOP_K):
        mv = jnp.max(work, axis=0, keepdims=True)
        sel = jnp.min(jnp.where(work == mv, eidx, float(N_EXPERTS)), axis=0, keepdims=True)
        hot = eidx == sel
        vals.append(mv)
        hots.append(hot)
        work = jnp.where(hot, -jnp.inf, work)
    es = [jnp.exp(v - vals[0]) for v in vals]
    tot = es[0] + es[1] + es[2] + es[3]
    gates = [e / tot for e in es]
    hot_all = jnp.where(hots[0] | hots[1] | hots[2] | hots[3], 1.0, 0.0)
    tm = logits.shape[1]
    su = (lax.broadcasted_iota(jnp.int32, (tm, tm), 0) < lax.broadcasted_iota(jnp.int32, (tm, tm), 1))
    cum = jnp.dot(hot_all.astype(BF16), su.astype(BF16), preferred_element_type=F32)
    cnt = jnp.sum(hot_all, axis=1, keepdims=True)
    cpad = (((cnt.astype(jnp.int32) + (SUBLANES - 1)) // SUBLANES) * SUBLANES).astype(F32)
    lower = (lax.broadcasted_iota(jnp.int32, (N_EXPERTS, N_EXPERTS), 0)
             > lax.broadcasted_iota(jnp.int32, (N_EXPERTS, N_EXPERTS), 1)).astype(F32)
    lstart = jnp.dot(lower, jnp.broadcast_to(cpad, (N_EXPERTS, LANES)), precision=HIGHEST,
                     preferred_element_type=F32)[:, 0:1]
    base = lstart + cum
    lpos = [jnp.sum(jnp.where(hot, base, 0.0), axis=0, keepdims=True) for hot in hots]
    lpi = [p.astype(jnp.int32) for p in lpos]
    r_iota = lax.broadcasted_iota(jnp.int32, (GROUP_R, tm), 0)
    sel01 = jnp.where(r_iota == lpi[0], 1.0, jnp.where(r_iota == lpi[1], 1.0, jnp.where(
        r_iota == lpi[2], 1.0, jnp.where(r_iota == lpi[3], 1.0, 0.0)))).astype(BF16)
    xg_ref[...] = jnp.dot(sel01, h2.astype(BF16), preferred_element_type=F32)
    meta_ref[...] = jnp.transpose(jnp.concatenate(gates + lpos, axis=0))
    cnt_ref[0] = jnp.broadcast_to(cnt, (N_EXPERTS, LANES))


def _outproj_router(xp, xs, hmp, hap, hms, has, wom, woa, g_ffn, wrt, br):
    tp, ts = xp.shape[0], xs.shape[0]
    n_pt, n_st = tp // TM, ts // TM
    t_all = tp + ts
    pblk = lambda w: pl.BlockSpec((TM, w), lambda i: (jnp.minimum(i, n_pt - 1), 0))
    sblk = lambda w: pl.BlockSpec((TM, w), lambda i: (jnp.maximum(i - n_pt, 0), 0))
    full = lambda a: pl.BlockSpec(a.shape, lambda i: (0,) * a.ndim)
    return pl.pallas_call(
        functools.partial(_outproj_router_kernel, n_pt),
        grid=(n_pt + n_st,),
        in_specs=[pblk(D_MODEL), sblk(D_MODEL), pblk(DM), pblk(DM), sblk(DM), sblk(DM),
                  full(wom), full(woa), full(g_ffn), full(wrt), full(br)],
        out_specs=(pl.BlockSpec((TM, D_MODEL), lambda i: (i, 0)),
                   pl.BlockSpec((GROUP_R, D_MODEL), lambda i: (i, 0)),
                   pl.BlockSpec((TM, 2 * TOP_K), lambda i: (i, 0)),
                   pl.BlockSpec((1, N_EXPERTS, LANES), lambda i: (i, 0, 0))),
        out_shape=(jax.ShapeDtypeStruct((t_all, D_MODEL), F32),
                   jax.ShapeDtypeStruct(((n_pt + n_st) * GROUP_R, D_MODEL), F32),
                   jax.ShapeDtypeStruct((t_all, 2 * TOP_K), F32),
                   jax.ShapeDtypeStruct((n_pt + n_st, N_EXPERTS, LANES), F32)),
        compiler_params=_cparams(("arbitrary",), VMEM_LIMIT),
        name="outproj_router",
    )(xp, xs, hmp, hap, hms, has, wom, woa, g_ffn, wrt, br)


def _expert_kernel(be_ref, na_ref, slot_ref, nxt_ref, ctab_ref, ctab_next_ref, xg_ref, wgu_ref, bgu_ref,
                   wd_ref, bd_ref, o_ref, xbuf, wgu_f, wd_f, wgu_s, wd_s, xsem, wsem):
    i = pl.program_id(0)
    na = na_ref[0]

    def x_copies(tab_ref, slot):
        return [pltpu.make_async_copy(xg_ref.at[pl.ds(pl.multiple_of(tab_ref[0, 0, c], SUBLANES), SUBLANES)],
                                      xbuf.at[slot, pl.ds(c * SUBLANES, SUBLANES)], xsem.at[slot])
                for c in range(MOE_BM // SUBLANES)]

    def w_copies(e, slot):
        return [pltpu.make_async_copy(wgu_ref.at[e], wgu_f.at[slot], wsem.at[slot]),
                pltpu.make_async_copy(wd_ref.at[e], wd_f.at[slot], wsem.at[slot])]

    @pl.when(i == 0)
    def _():
        for cp in x_copies(ctab_ref, 0) + w_copies(be_ref[0], 0):
            cp.start()

    @pl.when(i < na)
    def _():
        changed = (i == 0) | (be_ref[i] != be_ref[jnp.maximum(i - 1, 0)])
        wslot = slot_ref[i]
        xslot = lax.rem(i, 2)

        @pl.when(i + 1 < na)
        def _():
            for cp in x_copies(ctab_next_ref, 1 - xslot):
                cp.start()

        @pl.when(changed)
        def _():
            for cp in w_copies(be_ref[i], wslot):
                cp.wait()

            @pl.when(nxt_ref[i] >= 0)
            def _():
                for cp in w_copies(nxt_ref[i], 1 - wslot):
                    cp.start(priority=1)

            wgu_s[...] = wgu_f[wslot].astype(BF16)
            wd_s[...] = wd_f[wslot].astype(BF16)

        for cp in x_copies(ctab_ref, xslot):
            cp.wait()

        x = xbuf[xslot].astype(BF16)
        gu = jnp.dot(x, wgu_s[...], preferred_element_type=F32) + bgu_ref[...]
        gate = jnp.minimum(gu[:, :D_FF], SWIGLU_LIMIT)
        up = jnp.clip(gu[:, D_FF:], -SWIGLU_LIMIT, SWIGLU_LIMIT)
        act = (up + 1.0) * (gate * _sigmoid(gate * SWIGLU_ALPHA))
        o_ref[...] = jnp.dot(act.astype(BF16), wd_s[...], preferred_element_type=F32) + bd_ref[...]


def _expert_ffn(block_exp, n_active, wslot, next_exp, ctab, xg, wgu, bgu, wd, bd):
    n_blocks = ctab.shape[0]
    n_rows = n_blocks * MOE_BM
    nch = MOE_BM // SUBLANES
    grid_spec = pltpu.PrefetchScalarGridSpec(
        num_scalar_prefetch=4,
        grid=(n_blocks,),
        in_specs=[pl.BlockSpec((1, 1, nch), lambda i, *_: (i, 0, 0), memory_space=pltpu.SMEM),
                  pl.BlockSpec((1, 1, nch), lambda i, *_: (jnp.minimum(i + 1, n_blocks - 1), 0, 0),
                               memory_space=pltpu.SMEM),
                  pl.BlockSpec(memory_space=pl.ANY),
                  pl.BlockSpec(memory_space=pl.ANY),
                  pl.BlockSpec((None, 1, 2 * D_FF), lambda i, be, *_: (be[i], 0, 0)),
                  pl.BlockSpec(memory_space=pl.ANY),
                  pl.BlockSpec((None, 1, D_MODEL), lambda i, be, *_: (be[i], 0, 0))],
        out_specs=pl.BlockSpec((MOE_BM, D_MODEL), lambda i, be, na, *_: (jnp.minimum(i, na[0] - 1), 0)),
        scratch_shapes=[pltpu.VMEM((2, MOE_BM, D_MODEL), F32),
                        pltpu.VMEM((2, D_MODEL, 2 * D_FF), F32), pltpu.VMEM((2, D_FF, D_MODEL), F32),
                        pltpu.VMEM((D_MODEL, 2 * D_FF), BF16), pltpu.VMEM((D_FF, D_MODEL), BF16),
                        pltpu.SemaphoreType.DMA((2,)), pltpu.SemaphoreType.DMA((2,))],
    )
    return pl.pallas_call(
        _expert_kernel,
        grid_spec=grid_spec,
        out_shape=jax.ShapeDtypeStruct((n_rows, D_MODEL), F32),
        compiler_params=_cparams(("arbitrary",), VMEM_LIMIT),
        name="moe_experts",
    )(block_exp, n_active, wslot, next_exp, ctab, ctab, xg, wgu, bgu, wd, bd)


def _combine_kernel(n_ptiles, ctab_ref, ctab_next_ref, outs_ref, x1_ref, meta_ref, gf_ref, yp_ref, ys_ref,
                    obuf, sem):
    i = pl.program_id(0)
    n = pl.num_programs(0)
    slot = lax.rem(i, 2)

    def copies(tab_ref, s):
        return [pltpu.make_async_copy(outs_ref.at[pl.ds(pl.multiple_of(tab_ref[0, 0, c], SUBLANES), SUBLANES)],
                                      obuf.at[s, pl.ds(c * SUBLANES, SUBLANES)], sem.at[s])
                for c in range(GROUP_R // SUBLANES)]

    @pl.when(i == 0)
    def _():
        for cp in copies(ctab_ref, 0):
            cp.start()

    @pl.when(i + 1 < n)
    def _():
        for cp in copies(ctab_next_ref, 1 - slot):
            cp.start()

    for cp in copies(ctab_ref, slot):
        cp.wait()

    meta = meta_ref[...]
    tm = meta.shape[0]
    r_iota = lax.broadcasted_iota(jnp.int32, (tm, GROUP_R), 1)
    lp = [meta[:, TOP_K + k:TOP_K + k + 1].astype(jnp.int32) for k in range(TOP_K)]
    gk = [meta[:, k:k + 1] for k in range(TOP_K)]
    gsel = jnp.where(r_iota == lp[0], gk[0], jnp.where(r_iota == lp[1], gk[1], jnp.where(
        r_iota == lp[2], gk[2], jnp.where(r_iota == lp[3], gk[3], 0.0))))
    sel01 = jnp.where(gsel != 0.0, 1.0, 0.0).astype(BF16)
    rg_row = jnp.sum(gsel, axis=0, keepdims=True)
    rg_col = jnp.transpose(jnp.broadcast_to(rg_row, (SUBLANES, GROUP_R)))[:, 0:1]
    og = (obuf[slot] * rg_col).astype(BF16)
    acc = x1_ref[...] + jnp.dot(sel01, og, preferred_element_type=F32)
    y = _rms(acc, gf_ref[...])

    @pl.when(i < n_ptiles)
    def _():
        yp_ref[...] = y

    @pl.when(i >= n_ptiles)
    def _():
        ys_ref[...] = y


def _combine(ctab, outs, x1, meta, g_final, tp, ts):
    n_pt, n_st = tp // TM, ts // TM
    n = n_pt + n_st
    nch = GROUP_R // SUBLANES
    return pl.pallas_call(
        functools.partial(_combine_kernel, n_pt),
        grid=(n,),
        in_specs=[pl.BlockSpec((1, 1, nch), lambda i: (i, 0, 0), memory_space=pltpu.SMEM),
                  pl.BlockSpec((1, 1, nch), lambda i: (jnp.minimum(i + 1, n - 1), 0, 0), memory_space=pltpu.SMEM),
                  pl.BlockSpec(memory_space=pl.ANY),
                  pl.BlockSpec((TM, D_MODEL), lambda i: (i, 0)),
                  pl.BlockSpec((TM, 2 * TOP_K), lambda i: (i, 0)),
                  pl.BlockSpec((1, D_MODEL), lambda i: (0, 0))],
        out_specs=(pl.BlockSpec((TM, D_MODEL), lambda i: (jnp.minimum(i, n_pt - 1), 0)),
                   pl.BlockSpec((TM, D_MODEL), lambda i: (jnp.maximum(i - n_pt, 0), 0))),
        out_shape=(jax.ShapeDtypeStruct((tp, D_MODEL), F32), jax.ShapeDtypeStruct((ts, D_MODEL), F32)),
        scratch_shapes=[pltpu.VMEM((2, GROUP_R, D_MODEL), F32), pltpu.SemaphoreType.DMA((2,))],
        compiler_params=_cparams(("arbitrary",), VMEM_LIMIT),
        name="moe_combine",
    )(ctab, ctab, outs, x1, meta, g_final)


def kernel(x_prompt, x_sample, cache_swa_k, cache_swa_v, state_mlstm_c, state_mlstm_n, state_mlstm_m,
           g_mix, w_in, b_igate, b_fgate, g_head, attn_sinks, w_out, g_ffn, w_router, b_router,
           w_gate_up, b_gate_up, w_down, b_down, g_final):
    assert w_in.shape[0] == 1, "single-layer problem"
    B, S, _ = x_prompt.shape
    Bd, Tn, _ = x_sample.shape
    tp, ts = B * S, Bd * Tn
    t_all = tp + ts
    xp = x_prompt.reshape(tp, D_MODEL)
    xs = x_sample.reshape(ts, D_MODEL)

    w = w_in[0]
    o = np.cumsum([0, DM, DM, DM, DM, N_HEADS, N_HEADS, DM, DKV, DKV])
    col = lambda a: w[:, int(o[a]):int(o[a + 1])]
    w1 = jnp.concatenate([col(1), col(6), col(7), col(8)], axis=1).astype(BF16)
    wkt = jnp.concatenate([col(0), col(2), col(3)], axis=1).T.astype(BF16)
    wgates = jnp.concatenate([col(4), col(5)], axis=1)
    wg = jnp.pad(wgates, ((0, 0), (0, LANES - 2 * N_HEADS))).astype(BF16)
    wgt = wgates.T.astype(BF16)
    bg = jnp.concatenate([b_igate[0], b_fgate[0]]).astype(F32)
    bcol = jnp.pad(bg, (0, LANES - 2 * N_HEADS)).reshape(1, LANES)
    brow = bg.reshape(2 * N_HEADS, 1)

    km, qa, ka, va, qmt, vmt, omt, gcol, grow = _inproj(xp, xs, g_mix[0].reshape(1, D_MODEL), w1, wkt, wg, wgt,
                                                        bcol, brow)

    gh = g_head[0].astype(F32)
    sinks = attn_sinks[0].astype(F32)

    hm_p, ctp, m_p = _mlstm_prompt(km, qmt, vmt, omt, gcol, grow,
                                   jnp.broadcast_to(gh.reshape(DM, 1), (DM, LANES)), B, S)
    ha_p = _swa_prompt(sinks, qa, ka, va, B, S)

    hm_s, c_s, n_s, mt_s = _mlstm_sample(km, qmt, vmt, omt, gcol, grow, state_mlstm_c[0], state_mlstm_n[0],
                                         state_mlstm_m[0], gh.reshape(1, DM), tp, Tn)
    n_s = n_s.reshape(Bd, N_HEADS, HEAD_DIM)
    m_s = mt_s[:, :, :LANES // Tn].transpose(0, 2, 1).reshape(Bd, N_HEADS)
    ha_s, k_s, v_s = _swa_decode(sinks, qa, ka, va, cache_swa_k[0].reshape(Bd, WINDOW, DKV),
                                 cache_swa_v[0].reshape(Bd, WINDOW, DKV), tp, Tn)

    wo = w_out[0].astype(BF16)
    x1, xg, meta, cnt = _outproj_router(
        xp, xs, hm_p, ha_p, hm_s, ha_s, wo[:DM], wo[DM:], g_ffn[0].reshape(1, D_MODEL),
        w_router[0].T, b_router[0].reshape(N_EXPERTS, 1))

    i32 = jnp.int32
    n_tiles = t_all // TM
    max_rows = t_all * TOP_K + n_tiles * N_EXPERTS * (SUBLANES - 1) + N_EXPERTS * (MOE_BM - 1)
    n_blocks = -(-max_rows // MOE_BM)
    cpad = (cnt[:, :, 0].astype(i32) + (SUBLANES - 1)) // SUBLANES * SUBLANES
    lstart = jnp.cumsum(cpad, axis=1) - cpad
    goff = jnp.cumsum(cpad, axis=0) - cpad
    padded = (jnp.sum(cpad, axis=0) + MOE_BM - 1) // MOE_BM * MOE_BM
    pad_end = jnp.cumsum(padded)
    seg_begin = (pad_end - padded)[None, :] + goff
    n_active = (pad_end[-1] // MOE_BM).astype(i32)
    blk = jnp.minimum(jnp.arange(n_blocks, dtype=i32), n_active - 1)
    block_exp = jnp.minimum(jnp.sum((pad_end[None, :] <= (blk * MOE_BM)[:, None]).astype(i32), axis=1),
                            N_EXPERTS - 1)
    e_ids = jnp.arange(N_EXPERTS, dtype=i32)
    nonempty = padded > 0
    nxt_e = jnp.min(jnp.where((e_ids[None, :] > e_ids[:, None]) & nonempty[None, :], e_ids[None, :], N_EXPERTS),
                    axis=1)
    nxt_e = jnp.where(nxt_e == N_EXPERTS, -1, nxt_e)
    ord_e = jnp.cumsum(nonempty.astype(i32)) - 1
    be_hot = block_exp[:, None] == e_ids[None, :]
    next_exp = jnp.sum(jnp.where(be_hot, nxt_e[None, :], 0), axis=1).astype(i32)
    wslot = (jnp.sum(jnp.where(be_hot, ord_e[None, :], 0), axis=1) % 2).astype(i32)

    seg_src = jnp.arange(n_tiles, dtype=i32)[:, None] * GROUP_R + lstart
    sb, sl, ss = seg_begin.reshape(-1), cpad.reshape(-1), seg_src.reshape(-1)
    rc = jnp.arange(n_blocks * MOE_BM // SUBLANES, dtype=i32)[:, None] * SUBLANES
    inseg = (sb[None, :] <= rc) & (rc < (sb + sl)[None, :])
    ctab_e = jnp.where(jnp.any(inseg, axis=1), jnp.sum(jnp.where(inseg, (ss - sb)[None, :] + rc, 0), axis=1),
                       GROUP_R - SUBLANES)
    lr = jnp.arange(GROUP_R // SUBLANES, dtype=i32)[None, :, None] * SUBLANES
    inl = (lstart[:, None, :] <= lr) & (lr < (lstart + cpad)[:, None, :])
    ctab_c = jnp.sum(jnp.where(inl, (seg_begin - lstart)[:, None, :] + lr, 0), axis=2)

    outs = _expert_ffn(block_exp, n_active.reshape(1), wslot, next_exp,
                       ctab_e.astype(i32).reshape(n_blocks, 1, MOE_BM // SUBLANES), xg, w_gate_up[0],
                       b_gate_up[0].reshape(N_EXPERTS, 1, 2 * D_FF), w_down[0],
                       b_down[0].reshape(N_EXPERTS, 1, D_MODEL))
    y_p, y_s = _combine(ctab_c.astype(i32).reshape(n_tiles, 1, GROUP_R // SUBLANES), outs, x1, meta,
                        g_final.reshape(1, D_MODEL), tp, ts)

    kv_tail = lambda a: a[:tp].reshape(B, S, N_KV, HEAD_DIM)[:, S - WINDOW:][None]
    c_e = ctp[:, :, :HEAD_DIM, :HEAD_DIM]
    c_o = ctp[:, :, HEAD_DIM:LANES, HEAD_DIM:]
    c_p = jnp.swapaxes(jnp.stack([c_e, c_o], axis=2), -1, -2).reshape(B, N_HEADS, HEAD_DIM, HEAD_DIM)
    n_p = (ctp[:, :, LANES:LANES + 2, :HEAD_DIM] + ctp[:, :, LANES:LANES + 2, HEAD_DIM:]).reshape(B, N_HEADS, HEAD_DIM)
    return (y_p.reshape(B, S, D_MODEL), y_s.reshape(Bd, Tn, D_MODEL),
            kv_tail(ka), kv_tail(va), c_p[None], n_p[None], m_p[:, :, 0][None],
            k_s.reshape(Bd, WINDOW, N_KV, HEAD_DIM)[None], v_s.reshape(Bd, WINDOW, N_KV, HEAD_DIM)[None],
            c_s[None], n_s[None], m_s[None])
```

```python
import functools
import math

import jax
import jax.numpy as jnp
import numpy as np
from jax import lax
from jax.experimental import pallas as pl
from jax.experimental.pallas import tpu as pltpu

F32 = jnp.float32
BF16 = jnp.bfloat16
HIGHEST = lax.Precision.HIGHEST

D_MODEL = 1024
HEAD_DIM = 64
N_HEADS = 8
N_PAIRS = N_HEADS // 2
N_KV = 2
GROUP = N_HEADS // N_KV
WINDOW = 128
N_EXPERTS = 32
TOP_K = 4
D_FF = 1024
SWIGLU_LIMIT = 7.0
SWIGLU_ALPHA = 1.702
RMS_EPS = 1e-5
DM = N_HEADS * HEAD_DIM
DKV = N_KV * HEAD_DIM
NEG = -1e30

LANES = 128
SUBLANES = 8
VMEM_LIMIT = 56 * 1024 * 1024

TM = 512
MLSTM_TL = 512
MLSTM_L = 128
SWA_BLOCKS = 4
MOE_BM = 256
GROUP_R = -(-(TOP_K * TM + N_EXPERTS * (SUBLANES - 1) + SUBLANES) // LANES) * LANES


def _cparams(sem, vmem=None):
    return pltpu.CompilerParams(dimension_semantics=sem, vmem_limit_bytes=vmem)


def _rms(x, g):
    return x * lax.rsqrt(jnp.mean(x * x, axis=-1, keepdims=True) + RMS_EPS) * g


def _log_sigmoid(z):
    return jnp.minimum(z, 0.0) - jnp.log(1.0 + jnp.exp(-jnp.abs(z)))


def _sigmoid(z):
    return 1.0 / (1.0 + jnp.exp(-z))


def _inproj_kernel(n_ptiles, xp_ref, xs_ref, g_ref, w1_ref, wt_ref, wg_ref, wgt_ref, bcol_ref, brow_ref,
                   km_ref, qa_ref, ka_ref, va_ref, qmt_ref, vmt_ref, omt_ref, gcol_ref, grow_ref):
    i = pl.program_id(0)
    x = jnp.where(i < n_ptiles, xp_ref[...], xs_ref[...])
    h = _rms(x, g_ref[...]).astype(BF16)
    main = jnp.dot(h, w1_ref[...], preferred_element_type=F32)
    km_ref[...] = main[:, 0:DM] * (HEAD_DIM ** -0.5)
    qa_ref[...] = main[:, DM:2 * DM]
    ka_ref[...] = main[:, 2 * DM:2 * DM + DKV]
    va_ref[...] = main[:, 2 * DM + DKV:2 * DM + 2 * DKV]
    t = lax.dot_general(wt_ref[...], h, (((1,), (1,)), ((), ())), preferred_element_type=F32)
    qmt_ref[...] = t[0:DM]
    vmt_ref[...] = t[DM:2 * DM]
    omt_ref[...] = t[2 * DM:3 * DM]
    zc = jnp.dot(h, wg_ref[...], preferred_element_type=F32) + bcol_ref[...]
    lane = lax.broadcasted_iota(jnp.int32, zc.shape, 1)
    gcol_ref[...] = jnp.where(lane < N_HEADS, zc, _log_sigmoid(zc))
    zr = lax.dot_general(wgt_ref[...], h, (((1,), (1,)), ((), ())), preferred_element_type=F32) + brow_ref[...]
    row = lax.broadcasted_iota(jnp.int32, zr.shape, 0)
    grow_ref[...] = jnp.where(row < N_HEADS, zr, _log_sigmoid(zr))


def _inproj(xp, xs, g_mix, w1, wkt, wg, wgt, bcol, brow):
    tp, ts = xp.shape[0], xs.shape[0]
    n_pt, n_st = tp // TM, ts // TM
    t_all = tp + ts
    tok = lambda w: pl.BlockSpec((TM, w), lambda i: (i, 0))
    tr = lambda r: pl.BlockSpec((r, TM), lambda i: (0, i))
    full = lambda a: pl.BlockSpec(a.shape, lambda i: (0,) * a.ndim)
    out_shape = (
        jax.ShapeDtypeStruct((t_all, DM), F32), jax.ShapeDtypeStruct((t_all, DM), F32),
        jax.ShapeDtypeStruct((t_all, DKV), F32), jax.ShapeDtypeStruct((t_all, DKV), F32),
        jax.ShapeDtypeStruct((DM, t_all), F32), jax.ShapeDtypeStruct((DM, t_all), F32),
        jax.ShapeDtypeStruct((DM, t_all), F32),
        jax.ShapeDtypeStruct((t_all, LANES), F32), jax.ShapeDtypeStruct((2 * N_HEADS, t_all), F32),
    )
    return pl.pallas_call(
        functools.partial(_inproj_kernel, n_pt),
        grid=(n_pt + n_st,),
        in_specs=[
            pl.BlockSpec((TM, D_MODEL), lambda i: (jnp.minimum(i, n_pt - 1), 0)),
            pl.BlockSpec((TM, D_MODEL), lambda i: (jnp.maximum(i - n_pt, 0), 0)),
            full(g_mix), full(w1), full(wkt), full(wg), full(wgt), full(bcol), full(brow),
        ],
        out_specs=(tok(DM), tok(DM), tok(DKV), tok(DKV), tr(DM), tr(DM), tr(DM),
                   tok(LANES), tr(2 * N_HEADS)),
        out_shape=out_shape,
        compiler_params=_cparams(("arbitrary",), VMEM_LIMIT),
        name="inproj",
    )(xp, xs, g_mix, w1, wkt, wg, wgt, bcol, brow)


CT_ROWS = LANES + 2 * SUBLANES


def _cumsum_rows(x, n):
    row = lax.broadcasted_iota(jnp.int32, x.shape, 0)
    sh = 1
    while sh < n:
        x = x + jnp.where(row >= sh, pltpu.roll(x, sh, axis=0), 0.0)
        sh *= 2
    return x


def _mlstm_prompt_kernel(km_ref, qmt_ref, vmt_ref, omt_ref, gcol_ref, grow_ref, ghr_ref,
                         hm_ref, ct_ref, m_ref, ct_s, m_s):
    j = pl.program_id(1)
    L = MLSTM_L
    assert L == LANES

    @pl.when(j == 0)
    def _():
        ct_s[...] = jnp.zeros_like(ct_s)
        m_s[...] = jnp.zeros_like(m_s)

    si = lax.broadcasted_iota(jnp.int32, (L, L), 0)
    ti = lax.broadcasted_iota(jnp.int32, (L, L), 1)
    causal_t = si <= ti
    upper = jnp.where(causal_t, 1.0, 0.0)
    rows_c = lax.broadcasted_iota(jnp.int32, (CT_ROWS, 1), 0)
    rmask_e = (rows_c < HEAD_DIM) | (rows_c == LANES)
    rmask_o = ((rows_c >= HEAD_DIM) & (rows_c < LANES)) | (rows_c == LANES + 1)
    rows_e = lax.broadcasted_iota(jnp.int32, (LANES, 1), 0) < HEAD_DIM
    cols_e = lax.broadcasted_iota(jnp.int32, (1, LANES), 1) < HEAD_DIM
    bdt_mask = (rmask_e & cols_e) | (rmask_o & (~cols_e))
    ones_rows = jnp.where(lax.broadcasted_iota(jnp.int32, (CT_ROWS - LANES, L), 0) < 2, 1.0, 0.0)

    for c in range(MLSTM_TL // L):
        sl = slice(c * L, (c + 1) * L)
        grow = grow_ref[:, sl]
        i_row = grow[0:N_HEADS]
        b_row = jnp.dot(grow[N_HEADS:2 * N_HEADS], upper, precision=HIGHEST,
                        preferred_element_type=F32)
        gc = gcol_ref[sl, :]
        bc_all = _cumsum_rows(gc, L)
        for p in range(N_PAIRS):
            ls = slice(p * LANES, (p + 1) * LANES)
            k2 = km_ref[sl, ls]
            qt2 = qmt_ref[ls, sl]
            qt_e = jnp.where(rows_e, qt2, 0.0).astype(BF16)
            qt_o = jnp.where(rows_e, 0.0, qt2).astype(BF16)
            st2 = jnp.dot(k2.astype(BF16), jnp.concatenate([qt_e, qt_o], axis=1),
                          preferred_element_type=F32)
            ct = ct_s[p]
            rqt = jnp.dot(ct.astype(BF16), qt2.astype(BF16), preferred_element_type=F32)
            pts, mts, inters, wreps, decays, mnews = [], [], [], [], [], []
            for hh in range(2):
                h = 2 * p + hh
                cvec = jnp.broadcast_to(gc[:, h:h + 1] - bc_all[:, N_HEADS + h:N_HEADS + h + 1], (L, L))
                brow = b_row[h:h + 1, :]
                logdt = jnp.where(causal_t, cvec + brow, NEG)
                m_prev = m_s[h:h + 1, 0:1]
                m_inter = m_prev + brow
                m_t = jnp.maximum(m_inter, jnp.max(logdt, axis=0, keepdims=True))
                pts.append((st2[:, hh * L:(hh + 1) * L] * jnp.exp(logdt - m_t)).astype(BF16))
                mts.append(m_t)
                inters.append(jnp.exp(m_inter - m_t))
                m_new = m_t[:, L - 1:L]
                b_last = brow[:, L - 1:L]
                decays.append(jnp.exp(m_prev + b_last - m_new))
                wreps.append(jnp.exp(cvec + (b_last - m_new)))
                mnews.append(m_new)
            vext = jnp.concatenate([vmt_ref[ls, sl], ones_rows], axis=0)
            lhs = jnp.concatenate([jnp.where(rmask_e, vext, 0.0), jnp.where(rmask_o, vext, 0.0)],
                                  axis=1).astype(BF16)
            rt = jnp.dot(lhs, jnp.concatenate(pts, axis=0), preferred_element_type=F32)
            ndt = rt + jnp.where(rmask_e, inters[0], inters[1]) * rqt
            den = jnp.where(rows_e, ndt[LANES:LANES + 1], ndt[LANES + 1:LANES + 2])
            mt2 = jnp.where(rows_e, mts[0], mts[1])
            hvt = ndt[0:LANES] / jnp.maximum(jnp.abs(den), jnp.exp(-mt2))
            sq = hvt * hvt
            ms = jnp.where(rows_e, jnp.sum(sq[0:HEAD_DIM], axis=0, keepdims=True),
                           jnp.sum(sq[HEAD_DIM:LANES], axis=0, keepdims=True)) * (1.0 / HEAD_DIM)
            yt = hvt * lax.rsqrt(ms + RMS_EPS) * ghr_ref[ls, :] * _sigmoid(omt_ref[ls, sl])
            hm_ref[sl, ls] = jnp.transpose(yt).astype(hm_ref.dtype)
            kw = (k2 * jnp.where(cols_e, wreps[0], wreps[1])).astype(BF16)
            upd = jnp.dot(vext.astype(BF16), kw, preferred_element_type=F32)
            ct_s[p] = jnp.where(rmask_e, decays[0], decays[1]) * ct + jnp.where(bdt_mask, upd, 0.0)
            for hh in range(2):
                h = 2 * p + hh
                m_s[h:h + 1, :] = jnp.broadcast_to(mnews[hh], (1, LANES))

    @pl.when(j == pl.num_programs(1) - 1)
    def _():
        ct_ref[0] = ct_s[...]
        m_ref[0] = m_s[...]


def _mlstm_prompt(km, qmt, vmt, omt, gcol, grow, ghr, batch, seq):
    nt = seq // MLSTM_TL
    tokb = lambda w: pl.BlockSpec((MLSTM_TL, w), lambda b, j: (b * nt + j, 0))
    rowb = lambda r: pl.BlockSpec((r, MLSTM_TL), lambda b, j: (0, b * nt + j))
    return pl.pallas_call(
        _mlstm_prompt_kernel,
        grid=(batch, nt),
        in_specs=[tokb(DM), rowb(DM), rowb(DM), rowb(DM), tokb(LANES), rowb(2 * N_HEADS),
                  pl.BlockSpec((DM, LANES), lambda b, j: (0, 0))],
        out_specs=(tokb(DM),
                   pl.BlockSpec((1, N_PAIRS, CT_ROWS, LANES), lambda b, j: (b, 0, 0, 0)),
                   pl.BlockSpec((1, N_HEADS, LANES), lambda b, j: (b, 0, 0))),
        out_shape=(jax.ShapeDtypeStruct((batch * seq, DM), BF16),
                   jax.ShapeDtypeStruct((batch, N_PAIRS, CT_ROWS, LANES), F32),
                   jax.ShapeDtypeStruct((batch, N_HEADS, LANES), F32)),
        scratch_shapes=[pltpu.VMEM((N_PAIRS, CT_ROWS, LANES), F32), pltpu.VMEM((N_HEADS, LANES), F32)],
        compiler_params=_cparams(("arbitrary", "arbitrary"), VMEM_LIMIT),
        name="mlstm_prompt",
    )(km, qmt, vmt, omt, gcol, grow, ghr)


def _mlstm_sample_kernel(n_tok, km_ref, qmt_ref, vmt_ref, omt_ref, gcol_ref, grow_ref, c0_ref, n0_ref, m0_ref,
                         m0t_ref, gh_ref, hm_ref, c_ref, n_ref, mt_ref):
    L = LANES
    NB = L // n_tok
    ti = lax.broadcasted_iota(jnp.int32, (L, L), 0)
    si = lax.broadcasted_iota(jnp.int32, (L, L), 1)
    same = (ti // n_tok) == (si // n_tok)
    causal = same & (ti >= si)
    useg = jnp.where(same & (ti <= si), 1.0, 0.0)
    slast = jnp.where(same & (ti % n_tok == n_tok - 1), 1.0, 0.0)
    expand = jnp.where(ti // n_tok == si, 1.0, 0.0)
    expand_t = jnp.where(ti == si // n_tok, 1.0, 0.0)
    pick = jnp.where((ti // n_tok == si) & (ti % n_tok == n_tok - 1), 1.0, 0.0)
    hdot = lambda a, b: jnp.dot(a, b, precision=HIGHEST, preferred_element_type=F32)

    lane128 = lax.broadcasted_iota(jnp.int32, (L, LANES), 1)
    even128 = lane128 < HEAD_DIM
    lane256 = lax.broadcasted_iota(jnp.int32, (1, 2 * LANES), 1)
    cols_e = (lane256 < HEAD_DIM) | (lane256 == LANES)
    cols_o = ((lane256 >= HEAD_DIM) & (lane256 < LANES)) | (lane256 == LANES + 1)
    rows_e = lax.broadcasted_iota(jnp.int32, (LANES, 1), 0) < HEAD_DIM
    ones_cols = jnp.where(lane128 < 2, 1.0, 0.0)
    bo_r = lax.broadcasted_iota(jnp.int32, (LANES, LANES), 0) // HEAD_DIM
    bo_c = lax.broadcasted_iota(jnp.int32, (LANES, LANES), 1) // HEAD_DIM
    block_ones = jnp.where(bo_r == bo_c, 1.0, 0.0)
    W = NB * LANES
    rb = lax.broadcasted_iota(jnp.int32, (L, W), 0)
    cb = lax.broadcasted_iota(jnp.int32, (L, W), 1)
    own_block = (rb // n_tok) == (cb // LANES)
    bd_tiled = (rb // HEAD_DIM) == ((cb % LANES) // HEAD_DIM)

    grow = grow_ref[...]
    i_row = grow[0:N_HEADS]
    b_row = hdot(grow[N_HEADS:2 * N_HEADS], useg)
    b_last = hdot(b_row, slast)
    a_row = b_last - b_row + i_row
    pos = lax.broadcasted_iota(jnp.int32, a_row.shape, 1) % n_tok
    pm = a_row
    sh = 1
    while sh < n_tok:
        pm = jnp.where(pos >= sh, jnp.maximum(pm, pltpu.roll(pm, sh, axis=1)), pm)
        sh *= 2
    m_carry = hdot(jnp.concatenate([m0t_ref[0], b_row], axis=1), jnp.concatenate([expand_t, slast], axis=0))
    m_new_row = jnp.maximum(m_carry, hdot(pm, slast))
    decay_row = jnp.exp(m_carry - m_new_row)
    w_row = jnp.exp(a_row - m_new_row)
    mt_ref[0] = hdot(m_new_row, pick)
    decay_bh = hdot(decay_row, pick)
    decay_hb = jnp.transpose(decay_bh)[0:NB]

    bc_all = gcol_ref[...]
    rowpos = lax.broadcasted_iota(jnp.int32, bc_all.shape, 0) % n_tok
    sh = 1
    while sh < n_tok:
        bc_all = bc_all + jnp.where(rowpos >= sh, pltpu.roll(bc_all, sh, axis=0), 0.0)
        sh *= 2
    pad_rows = lambda a: jnp.concatenate([a, jnp.zeros((L - NB, a.shape[1]), F32)], axis=0)
    m0_col = hdot(expand, pad_rows(m0_ref[...]))
    rowv_all = i_row - b_row

    for p in range(N_PAIRS):
        ls = slice(p * LANES, (p + 1) * LANES)
        q2f = jnp.transpose(qmt_ref[ls, :])
        q2 = q2f.astype(BF16)
        kt2 = jnp.transpose(km_ref[:, ls])
        v2 = jnp.transpose(vmt_ref[ls, :])
        vext = jnp.concatenate([v2, ones_cols], axis=1)
        kt_e = jnp.where(rows_e, kt2, 0.0).astype(BF16)
        kt_o = jnp.where(rows_e, 0.0, kt2).astype(BF16)
        s2 = jnp.dot(q2, jnp.concatenate([kt_e, kt_o], axis=1), preferred_element_type=F32)
        ps, mts, inters = [], [], []
        for hh in range(2):
            h = 2 * p + hh
            bcol = bc_all[:, N_HEADS + h:N_HEADS + h + 1]
            logd = jnp.where(causal, bcol + rowv_all[h:h + 1, :], NEG)
            m_inter = m0_col[:, h:h + 1] + bcol
            m_t = jnp.maximum(m_inter, jnp.max(logd, axis=1, keepdims=True))
            ps.append((s2[:, hh * L:(hh + 1) * L] * jnp.exp(logd - m_t)).astype(BF16))
            mts.append(m_t)
            inters.append(jnp.exp(m_inter - m_t))
        vstack = jnp.concatenate([jnp.where(cols_e, vext, 0.0), jnp.where(cols_o, vext, 0.0)],
                                 axis=0).astype(BF16)
        r = jnp.dot(jnp.concatenate(ps, axis=1), vstack, preferred_element_type=F32)
        zero = jnp.zeros((HEAD_DIM, HEAD_DIM), F32)
        cstack = jnp.concatenate(
            [jnp.concatenate([jnp.concatenate([c0_ref[b, 2 * p], zero], axis=1),
                              jnp.concatenate([zero, c0_ref[b, 2 * p + 1]], axis=1)], axis=0)
             for b in range(NB)], axis=1)
        rq_all = jnp.where(own_block, jnp.dot(q2, cstack.astype(BF16), preferred_element_type=F32), 0.0)
        rq = rq_all[:, 0:LANES]
        for b in range(1, NB):
            rq = rq + rq_all[:, b * LANES:(b + 1) * LANES]
        n_rows = hdot(expand, pad_rows(n0_ref[:, ls]))
        qn = hdot(q2f * n_rows, block_ones)
        inter2 = jnp.where(even128, inters[0], inters[1])
        num = r[:, 0:LANES] + inter2 * rq
        den = jnp.where(even128, r[:, LANES:LANES + 1], r[:, LANES + 1:LANES + 2]) + inter2 * qn
        mt2 = jnp.where(even128, mts[0], mts[1])
        hv = num / jnp.maximum(jnp.abs(den), jnp.exp(-mt2))
        ms = hdot(hv * hv, block_ones) * (1.0 / HEAD_DIM)
        y = hv * lax.rsqrt(ms + RMS_EPS) * gh_ref[:, ls] * _sigmoid(jnp.transpose(omt_ref[ls, :]))
        hm_ref[:, ls] = y.astype(hm_ref.dtype)
        kw = kt2 * jnp.where(rows_e, w_row[2 * p:2 * p + 1, :], w_row[2 * p + 1:2 * p + 2, :])
        vbd = jnp.where(own_block, jnp.concatenate([v2] * NB, axis=1), 0.0).astype(BF16)
        upd = jnp.dot(kw.astype(BF16), vbd, preferred_element_type=F32)
        upd = jnp.where(bd_tiled, upd, 0.0)
        for b in range(NB):
            bs = slice(b * LANES, (b + 1) * LANES)
            dec_b = jnp.where(rows_e, decay_hb[b:b + 1, 2 * p:2 * p + 1], decay_hb[b:b + 1, 2 * p + 1:2 * p + 2])
            cnew = dec_b * cstack[:, bs] + upd[:, bs]
            c_ref[b, 2 * p] = cnew[0:HEAD_DIM, 0:HEAD_DIM]
            c_ref[b, 2 * p + 1] = cnew[HEAD_DIM:LANES, HEAD_DIM:LANES]
        nsum = jnp.transpose(hdot(kw, expand))[0:NB]
        dec_n = jnp.where(lax.broadcasted_iota(jnp.int32, (NB, LANES), 1) < HEAD_DIM,
                          decay_hb[:, 2 * p:2 * p + 1], decay_hb[:, 2 * p + 1:2 * p + 2])
        n_ref[:, ls] = dec_n * n0_ref[:, ls] + nsum


def _mlstm_sample(km, qmt, vmt, omt, gcol, grow, c0, n0, m0, gh, row0, n_tok):
    nb = c0.shape[0]
    g_nb = LANES // n_tok
    n_g = nb // g_nb
    blk0 = row0 // LANES
    tokb = lambda w: pl.BlockSpec((LANES, w), lambda i: (blk0 + i, 0))
    rowb = lambda r: pl.BlockSpec((r, LANES), lambda i: (0, blk0 + i))
    m0t = jnp.pad(m0.reshape(n_g, g_nb, N_HEADS).transpose(0, 2, 1),
                  ((0, 0), (0, 0), (0, LANES - g_nb)))
    return pl.pallas_call(
        functools.partial(_mlstm_sample_kernel, n_tok),
        grid=(n_g,),
        in_specs=[tokb(DM), rowb(DM), rowb(DM), rowb(DM), tokb(LANES), rowb(2 * N_HEADS),
                  pl.BlockSpec((g_nb, N_HEADS, HEAD_DIM, HEAD_DIM), lambda i: (i, 0, 0, 0)),
                  pl.BlockSpec((g_nb, DM), lambda i: (i, 0)),
                  pl.BlockSpec((g_nb, N_HEADS), lambda i: (i, 0)),
                  pl.BlockSpec((1, N_HEADS, LANES), lambda i: (i, 0, 0)),
                  pl.BlockSpec((1, DM), lambda i: (0, 0))],
        out_specs=(pl.BlockSpec((LANES, DM), lambda i: (i, 0)),
                   pl.BlockSpec((g_nb, N_HEADS, HEAD_DIM, HEAD_DIM), lambda i: (i, 0, 0, 0)),
                   pl.BlockSpec((g_nb, DM), lambda i: (i, 0)),
                   pl.BlockSpec((1, N_HEADS, LANES), lambda i: (i, 0, 0))),
        out_shape=(jax.ShapeDtypeStruct((nb * n_tok, DM), BF16), jax.ShapeDtypeStruct(c0.shape, F32),
                   jax.ShapeDtypeStruct((nb, DM), F32), jax.ShapeDtypeStruct((n_g, N_HEADS, LANES), F32)),
        compiler_params=_cparams(("arbitrary",), VMEM_LIMIT),
        name="mlstm_sample",
    )(km, qmt, vmt, omt, gcol, grow, c0, n0.reshape(nb, DM), m0, m0t, gh)


def _alibi_slope(h):
    return float(np.float32(2.0 ** (-8.0 * (h + 1) / N_HEADS)))


def _dup_halves(x):
    lane = lax.broadcasted_iota(jnp.int32, x.shape, 1)
    xr = pltpu.roll(x, HEAD_DIM, axis=1)
    lo = lane < HEAD_DIM
    return jnp.where(lo, x, xr), jnp.where(lo, xr, x)


def _stack_group_queries(q, g):
    lane = lax.broadcasted_iota(jnp.int32, (q.shape[0], LANES), 1)
    parts = []
    for hh in range(GROUP):
        h = GROUP * g + hh
        blk = q[:, (h // 2) * LANES:(h // 2 + 1) * LANES]
        keep = (lane < HEAD_DIM) if h % 2 == 0 else (lane >= HEAD_DIM)
        parts.append(jnp.where(keep, blk, 0.0))
    return jnp.concatenate(parts, axis=0).astype(BF16)


def _swa_prompt_kernel(sink_ref, q_ref, kp_ref, ko_ref, vp_ref, vo_ref, o_ref):
    j = pl.program_id(1)
    R = WINDOW
    qi = lax.broadcasted_iota(jnp.int32, (R, R), 0)
    kj = lax.broadcasted_iota(jnp.int32, (R, R), 1)
    own = kj <= qi
    distf = jnp.where(own, qi - kj, qi - kj + R).astype(F32)
    lane = lax.broadcasted_iota(jnp.int32, (R, LANES), 1)
    for u in range(SWA_BLOCKS):
        rs = slice(u * R, (u + 1) * R)
        q = q_ref[rs, :]
        k_prev = kp_ref[...] if u == 0 else ko_ref[(u - 1) * R:u * R, :]
        v_prev = vp_ref[...] if u == 0 else vo_ref[(u - 1) * R:u * R, :]
        kd = _dup_halves(jnp.concatenate([k_prev, ko_ref[rs, :]], axis=0))
        vd = _dup_halves(jnp.concatenate([v_prev, vo_ref[rs, :]], axis=0))
        valid = (own | (j > 0)) if u == 0 else None
        outs = []
        for g in range(N_KV):
            qs = _stack_group_queries(q, g)
            s = lax.dot_general(qs, kd[g].astype(BF16), (((1,), (1,)), ((), ())),
                                preferred_element_type=F32) * (HEAD_DIM ** -0.5)
            ps = []
            for hh in range(GROUP):
                h = GROUP * g + hh
                sink = sink_ref[h]
                rows = slice(hh * R, (hh + 1) * R)
                sh = jnp.where(own, s[rows, R:], s[rows, :R]) - _alibi_slope(h) * distf
                if valid is not None:
                    sh = jnp.where(valid, sh, NEG)
                mx = jnp.maximum(jnp.max(sh, axis=1, keepdims=True), sink)
                p = jnp.exp(sh - mx)
                p = p / (jnp.sum(p, axis=1, keepdims=True) + jnp.exp(sink - mx))
                ps.append(jnp.concatenate([jnp.where(own, 0.0, p), jnp.where(own, p, 0.0)], axis=1).astype(BF16))
            o = jnp.dot(jnp.concatenate(ps, axis=0), vd[g].astype(BF16), preferred_element_type=F32)
            for pp in range(GROUP // 2):
                outs.append(jnp.where(lane < HEAD_DIM, o[(2 * pp) * R:(2 * pp + 1) * R],
                                      o[(2 * pp + 1) * R:(2 * pp + 2) * R]))
        o_ref[rs, :] = jnp.concatenate(outs, axis=1).astype(o_ref.dtype)


def _swa_prompt(sinks, qa, ka, va, batch, seq):
    nb = seq // (WINDOW * SWA_BLOCKS)
    own = lambda w: pl.BlockSpec((WINDOW * SWA_BLOCKS, w), lambda b, j: (b * nb + j, 0))
    prev = lambda w: pl.BlockSpec(
        (WINDOW, w), lambda b, j: (b * nb * SWA_BLOCKS + jnp.maximum(j * SWA_BLOCKS - 1, 0), 0))
    return pl.pallas_call(
        _swa_prompt_kernel,
        grid=(batch, nb),
        in_specs=[pl.BlockSpec(memory_space=pltpu.SMEM), own(DM), prev(DKV), own(DKV), prev(DKV), own(DKV)],
        out_specs=own(DM),
        out_shape=jax.ShapeDtypeStruct((batch * seq, DM), BF16),
        compiler_params=_cparams(("arbitrary", "arbitrary"), VMEM_LIMIT),
        name="swa_prompt",
    )(sinks, qa, ka, ka, va, va)


def _swa_decode_kernel(n_tok, sink_ref, q_ref, kn_ref, vn_ref, kc_ref, vc_ref, o_ref, ko_ref, vo_ref):
    W = WINDOW
    L = LANES
    NB = L // n_tok
    for b in range(NB):
        ko_ref[b, 0:W - n_tok, :] = kc_ref[b, n_tok:W, :]
        ko_ref[b, W - n_tok:W, :] = kn_ref[b * n_tok:(b + 1) * n_tok, :]
        vo_ref[b, 0:W - n_tok, :] = vc_ref[b, n_tok:W, :]
        vo_ref[b, W - n_tok:W, :] = vn_ref[b * n_tok:(b + 1) * n_tok, :]
    q = q_ref[...]
    knd, vnd = _dup_halves(kn_ref[...]), _dup_halves(vn_ref[...])
    kcd = _dup_halves(kc_ref[...].reshape(NB * W, DKV))
    vcd = _dup_halves(vc_ref[...].reshape(NB * W, DKV))
    ri = lax.broadcasted_iota(jnp.int32, (L, L), 0)
    ci = lax.broadcasted_iota(jnp.int32, (L, L), 1)
    t_q = ri % n_tok
    dist_n = t_q - ci % n_tok
    valid_n = ((ri // n_tok) == (ci // n_tok)) & (dist_n >= 0)
    dist_c = t_q + W - ci
    valid_c = dist_c < WINDOW
    dnf = dist_n.astype(F32)
    dcf = dist_c.astype(F32)
    own1 = (lax.broadcasted_iota(jnp.int32, (L, NB * W), 0) // n_tok
            == lax.broadcasted_iota(jnp.int32, (L, NB * W), 1) // W)
    own4 = ((lax.broadcasted_iota(jnp.int32, (GROUP * L, NB * W), 0) % L) // n_tok
            == lax.broadcasted_iota(jnp.int32, (GROUP * L, NB * W), 1) // W)
    lane = lax.broadcasted_iota(jnp.int32, (L, LANES), 1)
    nt = (((1,), (1,)), ((), ()))
    outs = []
    for g in range(N_KV):
        qs = _stack_group_queries(q, g)
        sn = lax.dot_general(qs, knd[g].astype(BF16), nt, preferred_element_type=F32) * (HEAD_DIM ** -0.5)
        sc_all = lax.dot_general(qs, kcd[g].astype(BF16), nt, preferred_element_type=F32) * (HEAD_DIM ** -0.5)
        pcs, pns = [], []
        for hh in range(GROUP):
            h = GROUP * g + hh
            sink = sink_ref[h]
            slope = _alibi_slope(h)
            rows = slice(hh * L, (hh + 1) * L)
            blk = jnp.where(own1, sc_all[rows], 0.0)
            sc = blk[:, 0:W]
            for b in range(1, NB):
                sc = sc + blk[:, b * W:(b + 1) * W]
            shc = jnp.where(valid_c, sc - slope * dcf, NEG)
            shn = jnp.where(valid_n, sn[rows] - slope * dnf, NEG)
            mx = jnp.maximum(jnp.maximum(jnp.max(shc, axis=1, keepdims=True),
                                         jnp.max(shn, axis=1, keepdims=True)), sink)
            pc = jnp.exp(shc - mx)
            pn = jnp.exp(shn - mx)
            inv = 1.0 / (jnp.sum(pc, axis=1, keepdims=True) + jnp.sum(pn, axis=1, keepdims=True)
                         + jnp.exp(sink - mx))
            pcs.append(pc * inv)
            pns.append((pn * inv).astype(BF16))
        pc4 = jnp.concatenate(pcs, axis=0)
        p_bd = jnp.where(own4, jnp.concatenate([pc4] * NB, axis=1), 0.0).astype(BF16)
        o = (jnp.dot(p_bd, vcd[g].astype(BF16), preferred_element_type=F32)
             + jnp.dot(jnp.concatenate(pns, axis=0), vnd[g].astype(BF16), preferred_element_type=F32))
        for pp in range(GROUP // 2):
            outs.append(jnp.where(lane < HEAD_DIM, o[(2 * pp) * L:(2 * pp + 1) * L],
                                  o[(2 * pp + 1) * L:(2 * pp + 2) * L]))
    o_ref[...] = jnp.concatenate(outs, axis=1)


def _swa_decode(sinks, qa, ka, va, kc, vc, row0, n_tok):
    nb = kc.shape[0]
    g_nb = LANES // n_tok
    blk0 = row0 // LANES
    tokb = lambda w: pl.BlockSpec((LANES, w), lambda i: (blk0 + i, 0))
    cache = pl.BlockSpec((g_nb, WINDOW, DKV), lambda i: (i, 0, 0))
    return pl.pallas_call(
        functools.partial(_swa_decode_kernel, n_tok),
        grid=(nb // g_nb,),
        in_specs=[pl.BlockSpec(memory_space=pltpu.SMEM), tokb(DM), tokb(DKV), tokb(DKV), cache, cache],
        out_specs=(pl.BlockSpec((LANES, DM), lambda i: (i, 0)), cache, cache),
        out_shape=(jax.ShapeDtypeStruct((nb * n_tok, DM), F32),
                   jax.ShapeDtypeStruct(kc.shape, F32), jax.ShapeDtypeStruct(vc.shape, F32)),
        compiler_params=_cparams(("arbitrary",), VMEM_LIMIT),
        name="swa_decode",
    )(sinks, qa, ka, va, kc, vc)


def _outproj_router_kernel(n_ptiles, xp_ref, xs_ref, hmp_ref, hap_ref, hms_ref, has_ref, wom_ref, woa_ref,
                           g_ref, wrt_ref, br_ref,
                           x1_ref, xg_ref, meta_ref, cnt_ref):
    i = pl.program_id(0)
    is_p = i < n_ptiles
    x = jnp.where(is_p, xp_ref[...], xs_ref[...])
    hm = jnp.where(is_p, hmp_ref[...], hms_ref[...].astype(BF16))
    ha = jnp.where(is_p, hap_ref[...], has_ref[...].astype(BF16))
    x1 = (x + jnp.dot(hm, wom_ref[...], preferred_element_type=F32)
          + jnp.dot(ha, woa_ref[...], preferred_element_type=F32))
    x1_ref[...] = x1
    h2 = _rms(x1, g_ref[...])
    logits = lax.dot_general(wrt_ref[...], h2, (((1,), (1,)), ((), ())), precision=HIGHEST,
                             preferred_element_type=F32) + br_ref[...]
    eidx = lax.broadcasted_iota(jnp.int32, logits.shape, 0).astype(F32)
    work = logits
    vals, hots = [], []
    for _ in range(TOP_K):
        mv = jnp.max(work, axis=0, keepdims=True)
        sel = jnp.min(jnp.where(work == mv, eidx, float(N_EXPERTS)), axis=0, keepdims=True)
        hot = eidx == sel
        vals.append(mv)
        hots.append(hot)
        work = jnp.where(hot, -jnp.inf, work)
    es = [jnp.exp(v - vals[0]) for v in vals]
    tot = es[0] + es[1] + es[2] + es[3]
    gates = [e / tot for e in es]
    hot_all = jnp.where(hots[0] | hots[1] | hots[2] | hots[3], 1.0, 0.0)
    tm = logits.shape[1]
    su = (lax.broadcasted_iota(jnp.int32, (tm, tm), 0) < lax.broadcasted_iota(jnp.int32, (tm, tm), 1))
    cum = jnp.dot(hot_all.astype(BF16), su.astype(BF16), preferred_element_type=F32)
    cnt = jnp.sum(hot_all, axis=1, keepdims=True)
    cpad = (((cnt.astype(jnp.int32) + (SUBLANES - 1)) // SUBLANES) * SUBLANES).astype(F32)
    lower = (lax.broadcasted_iota(jnp.int32, (N_EXPERTS, N_EXPERTS), 0)
             > lax.broadcasted_iota(jnp.int32, (N_EXPERTS, N_EXPERTS), 1)).astype(F32)
    lstart = jnp.dot(lower, jnp.broadcast_to(cpad, (N_EXPERTS, LANES)), precision=HIGHEST,
                     preferred_element_type=F32)[:, 0:1]
    base = lstart + cum
    lpos = [jnp.sum(jnp.where(hot, base, 0.0), axis=0, keepdims=True) for hot in hots]
    lpi = [p.astype(jnp.int32) for p in lpos]
    r_iota = lax.broadcasted_iota(jnp.int32, (GROUP_R, tm), 0)
    sel01 = jnp.where(r_iota == lpi[0], 1.0, jnp.where(r_iota == lpi[1], 1.0, jnp.where(
        r_iota == lpi[2], 1.0, jnp.where(r_iota == lpi[3], 1.0, 0.0)))).astype(BF16)
    xg_ref[...] = jnp.dot(sel01, h2.astype(BF16), preferred_element_type=F32)
    meta_ref[...] = jnp.transpose(jnp.concatenate(gates + lpos, axis=0))
    cnt_ref[0] = jnp.broadcast_to(cnt, (N_EXPERTS, LANES))


def _outproj_router(xp, xs, hmp, hap, hms, has, wom, woa, g_ffn, wrt, br):
    tp, ts = xp.shape[0], xs.shape[0]
    n_pt, n_st = tp // TM, ts // TM
    t_all = tp + ts
    pblk = lambda w: pl.BlockSpec((TM, w), lambda i: (jnp.minimum(i, n_pt - 1), 0))
    sblk = lambda w: pl.BlockSpec((TM, w), lambda i: (jnp.maximum(i - n_pt, 0), 0))
    full = lambda a: pl.BlockSpec(a.shape, lambda i: (0,) * a.ndim)
    return pl.pallas_call(
        functools.partial(_outproj_router_kernel, n_pt),
        grid=(n_pt + n_st,),
        in_specs=[pblk(D_MODEL), sblk(D_MODEL), pblk(DM), pblk(DM), sblk(DM), sblk(DM),
                  full(wom), full(woa), full(g_ffn), full(wrt), full(br)],
        out_specs=(pl.BlockSpec((TM, D_MODEL), lambda i: (i, 0)),
                   pl.BlockSpec((GROUP_R, D_MODEL), lambda i: (i, 0)),
                   pl.BlockSpec((TM, 2 * TOP_K), lambda i: (i, 0)),
                   pl.BlockSpec((1, N_EXPERTS, LANES), lambda i: (i, 0, 0))),
        out_shape=(jax.ShapeDtypeStruct((t_all, D_MODEL), F32),
                   jax.ShapeDtypeStruct(((n_pt + n_st) * GROUP_R, D_MODEL), F32),
                   jax.ShapeDtypeStruct((t_all, 2 * TOP_K), F32),
                   jax.ShapeDtypeStruct((n_pt + n_st, N_EXPERTS, LANES), F32)),
        compiler_params=_cparams(("arbitrary",), VMEM_LIMIT),
        name="outproj_router",
    )(xp, xs, hmp, hap, hms, has, wom, woa, g_ffn, wrt, br)


def _expert_kernel(be_ref, na_ref, slot_ref, nxt_ref, ctab_ref, ctab_next_ref, xg_ref, wgu_ref, bgu_ref,
                   wd_ref, bd_ref, o_ref, xbuf, wgu_f, wd_f, wgu_s, wd_s, xsem, wsem):
    i = pl.program_id(0)
    na = na_ref[0]

    def x_copies(tab_ref, slot):
        return [pltpu.make_async_copy(xg_ref.at[pl.ds(pl.multiple_of(tab_ref[0, 0, c], SUBLANES), SUBLANES)],
                                      xbuf.at[slot, pl.ds(c * SUBLANES, SUBLANES)], xsem.at[slot])
                for c in range(MOE_BM // SUBLANES)]

    def w_copies(e, slot):
        return [pltpu.make_async_copy(wgu_ref.at[e], wgu_f.at[slot], wsem.at[slot]),
                pltpu.make_async_copy(wd_ref.at[e], wd_f.at[slot], wsem.at[slot])]

    @pl.when(i == 0)
    def _():
        for cp in x_copies(ctab_ref, 0) + w_copies(be_ref[0], 0):
            cp.start()

    @pl.when(i < na)
    def _():
        changed = (i == 0) | (be_ref[i] != be_ref[jnp.maximum(i - 1, 0)])
        wslot = slot_ref[i]
        xslot = lax.rem(i, 2)

        @pl.when(i + 1 < na)
        def _():
            for cp in x_copies(ctab_next_ref, 1 - xslot):
                cp.start()

        @pl.when(changed)
        def _():
            for cp in w_copies(be_ref[i], wslot):
                cp.wait()

            @pl.when(nxt_ref[i] >= 0)
            def _():
                for cp in w_copies(nxt_ref[i], 1 - wslot):
                    cp.start(priority=1)

            wgu_s[...] = wgu_f[wslot].astype(BF16)
            wd_s[...] = wd_f[wslot].astype(BF16)

        for cp in x_copies(ctab_ref, xslot):
            cp.wait()

        x = xbuf[xslot].astype(BF16)
        gu = jnp.dot(x, wgu_s[...], preferred_element_type=F32) + bgu_ref[...]
        gate = jnp.minimum(gu[:, :D_FF], SWIGLU_LIMIT)
        up = jnp.clip(gu[:, D_FF:], -SWIGLU_LIMIT, SWIGLU_LIMIT)
        act = (up + 1.0) * (gate * _sigmoid(gate * SWIGLU_ALPHA))
        o_ref[...] = jnp.dot(act.astype(BF16), wd_s[...], preferred_element_type=F32) + bd_ref[...]


def _expert_ffn(block_exp, n_active, wslot, next_exp, ctab, xg, wgu, bgu, wd, bd):
    n_blocks = ctab.shape[0]
    n_rows = n_blocks * MOE_BM
    nch = MOE_BM // SUBLANES
    grid_spec = pltpu.PrefetchScalarGridSpec(
        num_scalar_prefetch=4,
        grid=(n_blocks,),
        in_specs=[pl.BlockSpec((1, 1, nch), lambda i, *_: (i, 0, 0), memory_space=pltpu.SMEM),
                  pl.BlockSpec((1, 1, nch), lambda i, *_: (jnp.minimum(i + 1, n_blocks - 1), 0, 0),
                               memory_space=pltpu.SMEM),
                  pl.BlockSpec(memory_space=pl.ANY),
                  pl.BlockSpec(memory_space=pl.ANY),
                  pl.BlockSpec((None, 1, 2 * D_FF), lambda i, be, *_: (be[i], 0, 0)),
                  pl.BlockSpec(memory_space=pl.ANY),
                  pl.BlockSpec((None, 1, D_MODEL), lambda i, be, *_: (be[i], 0, 0))],
        out_specs=pl.BlockSpec((MOE_BM, D_MODEL), lambda i, be, na, *_: (jnp.minimum(i, na[0] - 1), 0)),
        scratch_shapes=[pltpu.VMEM((2, MOE_BM, D_MODEL), F32),
                        pltpu.VMEM((2, D_MODEL, 2 * D_FF), F32), pltpu.VMEM((2, D_FF, D_MODEL), F32),
                        pltpu.VMEM((D_MODEL, 2 * D_FF), BF16), pltpu.VMEM((D_FF, D_MODEL), BF16),
                        pltpu.SemaphoreType.DMA((2,)), pltpu.SemaphoreType.DMA((2,))],
    )
    return pl.pallas_call(
        _expert_kernel,
        grid_spec=grid_spec,
        out_shape=jax.ShapeDtypeStruct((n_rows, D_MODEL), F32),
        compiler_params=_cparams(("arbitrary",), VMEM_LIMIT),
        name="moe_experts",
    )(block_exp, n_active, wslot, next_exp, ctab, ctab, xg, wgu, bgu, wd, bd)


def _combine_kernel(n_ptiles, ctab_ref, ctab_next_ref, outs_ref, x1_ref, meta_ref, gf_ref, yp_ref, ys_ref,
                    obuf, sem):
    i = pl.program_id(0)
    n = pl.num_programs(0)
    slot = lax.rem(i, 2)

    def copies(tab_ref, s):
        return [pltpu.make_async_copy(outs_ref.at[pl.ds(pl.multiple_of(tab_ref[0, 0, c], SUBLANES), SUBLANES)],
                                      obuf.at[s, pl.ds(c * SUBLANES, SUBLANES)], sem.at[s])
                for c in range(GROUP_R // SUBLANES)]

    @pl.when(i == 0)
    def _():
        for cp in copies(ctab_ref, 0):
            cp.start()

    @pl.when(i + 1 < n)
    def _():
        for cp in copies(ctab_next_ref, 1 - slot):
            cp.start()

    for cp in copies(ctab_ref, slot):
        cp.wait()

    meta = meta_ref[...]
    tm = meta.shape[0]
    r_iota = lax.broadcasted_iota(jnp.int32, (tm, GROUP_R), 1)
    lp = [meta[:, TOP_K + k:TOP_K + k + 1].astype(jnp.int32) for k in range(TOP_K)]
    gk = [meta[:, k:k + 1] for k in range(TOP_K)]
    gsel = jnp.where(r_iota == lp[0], gk[0], jnp.where(r_iota == lp[1], gk[1], jnp.where(
        r_iota == lp[2], gk[2], jnp.where(r_iota == lp[3], gk[3], 0.0))))
    sel01 = jnp.where(gsel != 0.0, 1.0, 0.0).astype(BF16)
    rg_row = jnp.sum(gsel, axis=0, keepdims=True)
    rg_col = jnp.transpose(jnp.broadcast_to(rg_row, (SUBLANES, GROUP_R)))[:, 0:1]
    og = (obuf[slot] * rg_col).astype(BF16)
    acc = x1_ref[...] + jnp.dot(sel01, og, preferred_element_type=F32)
    y = _rms(acc, gf_ref[...])

    @pl.when(i < n_ptiles)
    def _():
        yp_ref[...] = y

    @pl.when(i >= n_ptiles)
    def _():
        ys_ref[...] = y


def _combine(ctab, outs, x1, meta, g_final, tp, ts):
    n_pt, n_st = tp // TM, ts // TM
    n = n_pt + n_st
    nch = GROUP_R // SUBLANES
    return pl.pallas_call(
        functools.partial(_combine_kernel, n_pt),
        grid=(n,),
        in_specs=[pl.BlockSpec((1, 1, nch), lambda i: (i, 0, 0), memory_space=pltpu.SMEM),
                  pl.BlockSpec((1, 1, nch), lambda i: (jnp.minimum(i + 1, n - 1), 0, 0), memory_space=pltpu.SMEM),
                  pl.BlockSpec(memory_space=pl.ANY),
                  pl.BlockSpec((TM, D_MODEL), lambda i: (i, 0)),
                  pl.BlockSpec((TM, 2 * TOP_K), lambda i: (i, 0)),
                  pl.BlockSpec((1, D_MODEL), lambda i: (0, 0))],
        out_specs=(pl.BlockSpec((TM, D_MODEL), lambda i: (jnp.minimum(i, n_pt - 1), 0)),
                   pl.BlockSpec((TM, D_MODEL), lambda i: (jnp.maximum(i - n_pt, 0), 0))),
        out_shape=(jax.ShapeDtypeStruct((tp, D_MODEL), F32), jax.ShapeDtypeStruct((ts, D_MODEL), F32)),
        scratch_shapes=[pltpu.VMEM((2, GROUP_R, D_MODEL), F32), pltpu.SemaphoreType.DMA((2,))],
        compiler_params=_cparams(("arbitrary",), VMEM_LIMIT),
        name="moe_combine",
    )(ctab, ctab, outs, x1, meta, g_final)


def kernel(x_prompt, x_sample, cache_swa_k, cache_swa_v, state_mlstm_c, state_mlstm_n, state_mlstm_m,
           g_mix, w_in, b_igate, b_fgate, g_head, attn_sinks, w_out, g_ffn, w_router, b_router,
           w_gate_up, b_gate_up, w_down, b_down, g_final):
    assert w_in.shape[0] == 1, "single-layer problem"
    B, S, _ = x_prompt.shape
    Bd, Tn, _ = x_sample.shape
    tp, ts = B * S, Bd * Tn
    t_all = tp + ts
    xp = x_prompt.reshape(tp, D_MODEL)
    xs = x_sample.reshape(ts, D_MODEL)

    w = w_in[0]
    o = np.cumsum([0, DM, DM, DM, DM, N_HEADS, N_HEADS, DM, DKV, DKV])
    col = lambda a: w[:, int(o[a]):int(o[a + 1])]
    w1 = jnp.concatenate([col(1), col(6), col(7), col(8)], axis=1).astype(BF16)
    wkt = jnp.concatenate([col(0), col(2), col(3)], axis=1).T.astype(BF16)
    wgates = jnp.concatenate([col(4), col(5)], axis=1)
    wg = jnp.pad(wgates, ((0, 0), (0, LANES - 2 * N_HEADS))).astype(BF16)
    wgt = wgates.T.astype(BF16)
    bg = jnp.concatenate([b_igate[0], b_fgate[0]]).astype(F32)
    bcol = jnp.pad(bg, (0, LANES - 2 * N_HEADS)).reshape(1, LANES)
    brow = bg.reshape(2 * N_HEADS, 1)

    km, qa, ka, va, qmt, vmt, omt, gcol, grow = _inproj(xp, xs, g_mix[0].reshape(1, D_MODEL), w1, wkt, wg, wgt,
                                                        bcol, brow)

    gh = g_head[0].astype(F32)
    sinks = attn_sinks[0].astype(F32)

    hm_p, ctp, m_p = _mlstm_prompt(km, qmt, vmt, omt, gcol, grow,
                                   jnp.broadcast_to(gh.reshape(DM, 1), (DM, LANES)), B, S)
    ha_p = _swa_prompt(sinks, qa, ka, va, B, S)

    hm_s, c_s, n_s, mt_s = _mlstm_sample(km, qmt, vmt, omt, gcol, grow, state_mlstm_c[0], state_mlstm_n[0],
                                         state_mlstm_m[0], gh.reshape(1, DM), tp, Tn)
    n_s = n_s.reshape(Bd, N_HEADS, HEAD_DIM)
    m_s = mt_s[:, :, :LANES // Tn].transpose(0, 2, 1).reshape(Bd, N_HEADS)
    ha_s, k_s, v_s = _swa_decode(sinks, qa, ka, va, cache_swa_k[0].reshape(Bd, WINDOW, DKV),
                                 cache_swa_v[0].reshape(Bd, WINDOW, DKV), tp, Tn)

    wo = w_out[0].astype(BF16)
    x1, xg, meta, cnt = _outproj_router(
        xp, xs, hm_p, ha_p, hm_s, ha_s, wo[:DM], wo[DM:], g_ffn[0].reshape(1, D_MODEL),
        w_router[0].T, b_router[0].reshape(N_EXPERTS, 1))

    i32 = jnp.int32
    n_tiles = t_all // TM
    max_rows = t_all * TOP_K + n_tiles * N_EXPERTS * (SUBLANES - 1) + N_EXPERTS * (MOE_BM - 1)
    n_blocks = -(-max_rows // MOE_BM)
    cpad = (cnt[:, :, 0].astype(i32) + (SUBLANES - 1)) // SUBLANES * SUBLANES
    lstart = jnp.cumsum(cpad, axis=1) - cpad
    goff = jnp.cumsum(cpad, axis=0) - cpad
    padded = (jnp.sum(cpad, axis=0) + MOE_BM - 1) // MOE_BM * MOE_BM
    pad_end = jnp.cumsum(padded)
    seg_begin = (pad_end - padded)[None, :] + goff
    n_active = (pad_end[-1] // MOE_BM).astype(i32)
    blk = jnp.minimum(jnp.arange(n_blocks, dtype=i32), n_active - 1)
    block_exp = jnp.minimum(jnp.sum((pad_end[None, :] <= (blk * MOE_BM)[:, None]).astype(i32), axis=1),
                            N_EXPERTS - 1)
    e_ids = jnp.arange(N_EXPERTS, dtype=i32)
    nonempty = padded > 0
    nxt_e = jnp.min(jnp.where((e_ids[None, :] > e_ids[:, None]) & nonempty[None, :], e_ids[None, :], N_EXPERTS),
                    axis=1)
    nxt_e = jnp.where(nxt_e == N_EXPERTS, -1, nxt_e)
    ord_e = jnp.cumsum(nonempty.astype(i32)) - 1
    be_hot = block_exp[:, None] == e_ids[None, :]
    next_exp = jnp.sum(jnp.where(be_hot, nxt_e[None, :], 0), axis=1).astype(i32)
    wslot = (jnp.sum(jnp.where(be_hot, ord_e[None, :], 0), axis=1) % 2).astype(i32)

    seg_src = jnp.arange(n_tiles, dtype=i32)[:, None] * GROUP_R + lstart
    sb, sl, ss = seg_begin.reshape(-1), cpad.reshape(-1), seg_src.reshape(-1)
    rc = jnp.arange(n_blocks * MOE_BM // SUBLANES, dtype=i32)[:, None] * SUBLANES
    inseg = (sb[None, :] <= rc) & (rc < (sb + sl)[None, :])
    ctab_e = jnp.where(jnp.any(inseg, axis=1), jnp.sum(jnp.where(inseg, (ss - sb)[None, :] + rc, 0), axis=1),
                       GROUP_R - SUBLANES)
    lr = jnp.arange(GROUP_R // SUBLANES, dtype=i32)[None, :, None] * SUBLANES
    inl = (lstart[:, None, :] <= lr) & (lr < (lstart + cpad)[:, None, :])
    ctab_c = jnp.sum(jnp.where(inl, (seg_begin - lstart)[:, None, :] + lr, 0), axis=2)

    outs = _expert_ffn(block_exp, n_active.reshape(1), wslot, next_exp,
                       ctab_e.astype(i32).reshape(n_blocks, 1, MOE_BM // SUBLANES), xg, w_gate_up[0],
                       b_gate_up[0].reshape(N_EXPERTS, 1, 2 * D_FF), w_down[0],
                       b_down[0].reshape(N_EXPERTS, 1, D_MODEL))
    y_p, y_s = _combine(ctab_c.astype(i32).reshape(n_tiles, 1, GROUP_R // SUBLANES), outs, x1, meta,
                        g_final.reshape(1, D_MODEL), tp, ts)

    kv_tail = lambda a: jnp.concatenate([a[(b + 1) * S - WINDOW:(b + 1) * S] for b in range(B)], axis=0).reshape(
        1, B, WINDOW, N_KV, HEAD_DIM)
    c_e = ctp[:, :, :HEAD_DIM, :HEAD_DIM]
    c_o = ctp[:, :, HEAD_DIM:LANES, HEAD_DIM:]
    c_p = jnp.swapaxes(jnp.stack([c_e, c_o], axis=2), -1, -2).reshape(B, N_HEADS, HEAD_DIM, HEAD_DIM)
    n_p = (ctp[:, :, LANES:LANES + 2, :HEAD_DIM] + ctp[:, :, LANES:LANES + 2, HEAD_DIM:]).reshape(B, N_HEADS, HEAD_DIM)
    return (y_p.reshape(B, S, D_MODEL), y_s.reshape(Bd, Tn, D_MODEL),
            kv_tail(ka), kv_tail(va), c_p[None], n_p[None], m_p[:, :, 0][None],
            k_s.reshape(Bd, WINDOW, N_KV, HEAD_DIM)[None], v_s.reshape(Bd, WINDOW, N_KV, HEAD_DIM)[None],
            c_s[None], n_s[None], m_s[None])
```

```python
import functools
import math

import jax
import jax.numpy as jnp
import numpy as np
from jax import lax
from jax.experimental import pallas as pl
from jax.experimental.pallas import tpu as pltpu

F32 = jnp.float32
BF16 = jnp.bfloat16
HIGHEST = lax.Precision.HIGHEST

D_MODEL = 1024
HEAD_DIM = 64
N_HEADS = 8
N_PAIRS = N_HEADS // 2
N_KV = 2
GROUP = N_HEADS // N_KV
WINDOW = 128
N_EXPERTS = 32
TOP_K = 4
D_FF = 1024
SWIGLU_LIMIT = 7.0
SWIGLU_ALPHA = 1.702
RMS_EPS = 1e-5
DM = N_HEADS * HEAD_DIM
DKV = N_KV * HEAD_DIM
NEG = -1e30

LANES = 128
SUBLANES = 8
VMEM_LIMIT = 56 * 1024 * 1024

TM = 512
ROUTE_T = 512
MLSTM_TL = 512
MLSTM_L = 128
SWA_BLOCKS = 8
MOE_BM = 256
GROUP_R = -(-(TOP_K * ROUTE_T + N_EXPERTS * (SUBLANES - 1) + SUBLANES) // LANES) * LANES


def _cparams(sem, vmem=None):
    return pltpu.CompilerParams(dimension_semantics=sem, vmem_limit_bytes=vmem)


def _rms(x, g):
    return x * lax.rsqrt(jnp.mean(x * x, axis=-1, keepdims=True) + RMS_EPS) * g


def _log_sigmoid(z):
    return jnp.minimum(z, 0.0) - jnp.log(1.0 + jnp.exp(-jnp.abs(z)))


def _sigmoid(z):
    return 1.0 / (1.0 + jnp.exp(-z))


def _inproj_kernel(n_ptiles, xp_ref, xs_ref, g_ref, w1_ref, wt_ref, bcol_ref, brow_ref,
                   km_ref, qa_ref, ka_ref, va_ref, qmt_ref, vmt_ref, omt_ref, gcol_ref, grow_ref):
    i = pl.program_id(0)
    x = jnp.where(i < n_ptiles, xp_ref[...], xs_ref[...])
    h = _rms(x, g_ref[...]).astype(BF16)
    main = jnp.dot(h, w1_ref[...], preferred_element_type=F32)
    km_ref[...] = main[:, 0:DM] * (HEAD_DIM ** -0.5)
    qa_ref[...] = main[:, DM:2 * DM]
    ka_ref[...] = main[:, 2 * DM:2 * DM + DKV]
    va_ref[...] = main[:, 2 * DM + DKV:2 * DM + 2 * DKV]
    t = lax.dot_general(wt_ref[...], h, (((1,), (1,)), ((), ())), preferred_element_type=F32)
    qmt_ref[...] = t[0:DM]
    vmt_ref[...] = t[DM:2 * DM]
    omt_ref[...] = t[2 * DM:3 * DM]
    zc = main[:, 2 * DM + 2 * DKV:] + bcol_ref[...]
    lane = lax.broadcasted_iota(jnp.int32, zc.shape, 1)
    gcol_ref[...] = jnp.where(lane < N_HEADS, zc, _log_sigmoid(zc))
    zr = t[3 * DM:] + brow_ref[...]
    row = lax.broadcasted_iota(jnp.int32, zr.shape, 0)
    grow_ref[...] = jnp.where(row < N_HEADS, zr, _log_sigmoid(zr))


def _inproj(xp, xs, g_mix, w1, wkt, bcol, brow):
    tp, ts = xp.shape[0], xs.shape[0]
    n_pt, n_st = tp // TM, ts // TM
    t_all = tp + ts
    tok = lambda w: pl.BlockSpec((TM, w), lambda i: (i, 0))
    tr = lambda r: pl.BlockSpec((r, TM), lambda i: (0, i))
    full = lambda a: pl.BlockSpec(a.shape, lambda i: (0,) * a.ndim)
    out_shape = (
        jax.ShapeDtypeStruct((t_all, DM), F32), jax.ShapeDtypeStruct((t_all, DM), F32),
        jax.ShapeDtypeStruct((t_all, DKV), F32), jax.ShapeDtypeStruct((t_all, DKV), F32),
        jax.ShapeDtypeStruct((DM, t_all), F32), jax.ShapeDtypeStruct((DM, t_all), F32),
        jax.ShapeDtypeStruct((DM, t_all), F32),
        jax.ShapeDtypeStruct((t_all, LANES), F32), jax.ShapeDtypeStruct((2 * N_HEADS, t_all), F32),
    )
    return pl.pallas_call(
        functools.partial(_inproj_kernel, n_pt),
        grid=(n_pt + n_st,),
        in_specs=[
            pl.BlockSpec((TM, D_MODEL), lambda i: (jnp.minimum(i, n_pt - 1), 0)),
            pl.BlockSpec((TM, D_MODEL), lambda i: (jnp.maximum(i - n_pt, 0), 0)),
            full(g_mix), full(w1), full(wkt), full(bcol), full(brow),
        ],
        out_specs=(tok(DM), tok(DM), tok(DKV), tok(DKV), tr(DM), tr(DM), tr(DM),
                   tok(LANES), tr(2 * N_HEADS)),
        out_shape=out_shape,
        compiler_params=_cparams(("arbitrary",), VMEM_LIMIT),
        name="inproj",
    )(xp, xs, g_mix, w1, wkt, bcol, brow)


CT_ROWS = LANES + 2 * SUBLANES


def _cumsum_rows(x, n):
    row = lax.broadcasted_iota(jnp.int32, x.shape, 0)
    sh = 1
    while sh < n:
        x = x + jnp.where(row >= sh, pltpu.roll(x, sh, axis=0), 0.0)
        sh *= 2
    return x


def _mlstm_prompt_kernel(km_ref, qmt_ref, vmt_ref, omt_ref, gcol_ref, grow_ref, ghr_ref,
                         hm_ref, ct_ref, m_ref, ct_s, m_s):
    j = pl.program_id(1)
    L = MLSTM_L
    assert L == LANES

    @pl.when(j == 0)
    def _():
        ct_s[...] = jnp.zeros_like(ct_s)
        m_s[...] = jnp.zeros_like(m_s)

    si = lax.broadcasted_iota(jnp.int32, (L, L), 0)
    ti = lax.broadcasted_iota(jnp.int32, (L, L), 1)
    causal_t = si <= ti
    upper = jnp.where(causal_t, 1.0, 0.0)
    rows_c = lax.broadcasted_iota(jnp.int32, (CT_ROWS, 1), 0)
    rmask_e = (rows_c < HEAD_DIM) | (rows_c == LANES)
    rmask_o = ((rows_c >= HEAD_DIM) & (rows_c < LANES)) | (rows_c == LANES + 1)
    rows_e = lax.broadcasted_iota(jnp.int32, (LANES, 1), 0) < HEAD_DIM
    cols_e = lax.broadcasted_iota(jnp.int32, (1, LANES), 1) < HEAD_DIM
    bdt_mask = (rmask_e & cols_e) | (rmask_o & (~cols_e))
    ones_rows = jnp.where(lax.broadcasted_iota(jnp.int32, (CT_ROWS - LANES, L), 0) < 2, 1.0, 0.0)

    for c in range(MLSTM_TL // L):
        sl = slice(c * L, (c + 1) * L)
        grow = grow_ref[:, sl]
        i_row = grow[0:N_HEADS]
        b_row = jnp.dot(grow[N_HEADS:2 * N_HEADS], upper, precision=HIGHEST,
                        preferred_element_type=F32)
        gc = gcol_ref[sl, :]
        bc_all = _cumsum_rows(gc, L)
        for p in range(N_PAIRS):
            ls = slice(p * LANES, (p + 1) * LANES)
            k2 = km_ref[sl, ls]
            qt2 = qmt_ref[ls, sl]
            qt_e = jnp.where(rows_e, qt2, 0.0).astype(BF16)
            qt_o = jnp.where(rows_e, 0.0, qt2).astype(BF16)
            st2 = jnp.dot(k2.astype(BF16), jnp.concatenate([qt_e, qt_o], axis=1),
                          preferred_element_type=F32)
            ct = ct_s[p]
            rqt = jnp.dot(ct.astype(BF16), qt2.astype(BF16), preferred_element_type=F32)
            pts, mts, inters, wreps, decays, mnews = [], [], [], [], [], []
            for hh in range(2):
                h = 2 * p + hh
                cvec = jnp.broadcast_to(gc[:, h:h + 1] - bc_all[:, N_HEADS + h:N_HEADS + h + 1], (L, L))
                brow = b_row[h:h + 1, :]
                logdt = jnp.where(causal_t, cvec + brow, NEG)
                m_prev = m_s[h:h + 1, 0:1]
                m_inter = m_prev + brow
                m_t = jnp.maximum(m_inter, jnp.max(logdt, axis=0, keepdims=True))
                pts.append((st2[:, hh * L:(hh + 1) * L] * jnp.exp(logdt - m_t)).astype(BF16))
                mts.append(m_t)
                inters.append(jnp.exp(m_inter - m_t))
                m_new = m_t[:, L - 1:L]
                b_last = brow[:, L - 1:L]
                decays.append(jnp.exp(m_prev + b_last - m_new))
                wreps.append(jnp.exp(cvec + (b_last - m_new)))
                mnews.append(m_new)
            vext = jnp.concatenate([vmt_ref[ls, sl], ones_rows], axis=0)
            lhs = jnp.concatenate([jnp.where(rmask_e, vext, 0.0), jnp.where(rmask_o, vext, 0.0)],
                                  axis=1).astype(BF16)
            rt = jnp.dot(lhs, jnp.concatenate(pts, axis=0), preferred_element_type=F32)
            ndt = rt + jnp.where(rmask_e, inters[0], inters[1]) * rqt
            den = jnp.where(rows_e, ndt[LANES:LANES + 1], ndt[LANES + 1:LANES + 2])
            mt2 = jnp.where(rows_e, mts[0], mts[1])
            hvt = ndt[0:LANES] / jnp.maximum(jnp.abs(den), jnp.exp(-mt2))
            sq = hvt * hvt
            ms = jnp.where(rows_e, jnp.sum(sq[0:HEAD_DIM], axis=0, keepdims=True),
                           jnp.sum(sq[HEAD_DIM:LANES], axis=0, keepdims=True)) * (1.0 / HEAD_DIM)
            yt = hvt * lax.rsqrt(ms + RMS_EPS) * ghr_ref[ls, :] * _sigmoid(omt_ref[ls, sl])
            hm_ref[sl, ls] = jnp.transpose(yt).astype(hm_ref.dtype)
            kw = (k2 * jnp.where(cols_e, wreps[0], wreps[1])).astype(BF16)
            upd = jnp.dot(vext.astype(BF16), kw, preferred_element_type=F32)
            ct_s[p] = jnp.where(rmask_e, decays[0], decays[1]) * ct + jnp.where(bdt_mask, upd, 0.0)
            for hh in range(2):
                h = 2 * p + hh
                m_s[h:h + 1, :] = jnp.broadcast_to(mnews[hh], (1, LANES))

    @pl.when(j == pl.num_programs(1) - 1)
    def _():
        ct_ref[0] = ct_s[...]
        m_ref[0] = m_s[...]


def _mlstm_prompt(km, qmt, vmt, omt, gcol, grow, ghr, batch, seq):
    nt = seq // MLSTM_TL
    tokb = lambda w: pl.BlockSpec((MLSTM_TL, w), lambda b, j: (b * nt + j, 0))
    rowb = lambda r: pl.BlockSpec((r, MLSTM_TL), lambda b, j: (0, b * nt + j))
    return pl.pallas_call(
        _mlstm_prompt_kernel,
        grid=(batch, nt),
        in_specs=[tokb(DM), rowb(DM), rowb(DM), rowb(DM), tokb(LANES), rowb(2 * N_HEADS),
                  pl.BlockSpec((DM, LANES), lambda b, j: (0, 0))],
        out_specs=(tokb(DM),
                   pl.BlockSpec((1, N_PAIRS, CT_ROWS, LANES), lambda b, j: (b, 0, 0, 0)),
                   pl.BlockSpec((1, N_HEADS, LANES), lambda b, j: (b, 0, 0))),
        out_shape=(jax.ShapeDtypeStruct((batch * seq, DM), BF16),
                   jax.ShapeDtypeStruct((batch, N_PAIRS, CT_ROWS, LANES), F32),
                   jax.ShapeDtypeStruct((batch, N_HEADS, LANES), F32)),
        scratch_shapes=[pltpu.VMEM((N_PAIRS, CT_ROWS, LANES), F32), pltpu.VMEM((N_HEADS, LANES), F32)],
        compiler_params=_cparams(("arbitrary", "arbitrary"), VMEM_LIMIT),
        name="mlstm_prompt",
    )(km, qmt, vmt, omt, gcol, grow, ghr)


def _mlstm_sample_kernel(n_tok, km_ref, qmt_ref, vmt_ref, omt_ref, gcol_ref, grow_ref, c0_ref, n0_ref, m0_ref,
                         m0t_ref, gh_ref, hm_ref, c_ref, n_ref, mt_ref):
    L = LANES
    NB = L // n_tok
    ti = lax.broadcasted_iota(jnp.int32, (L, L), 0)
    si = lax.broadcasted_iota(jnp.int32, (L, L), 1)
    same = (ti // n_tok) == (si // n_tok)
    causal = same & (ti >= si)
    useg = jnp.where(same & (ti <= si), 1.0, 0.0)
    slast = jnp.where(same & (ti % n_tok == n_tok - 1), 1.0, 0.0)
    expand = jnp.where(ti // n_tok == si, 1.0, 0.0)
    expand_t = jnp.where(ti == si // n_tok, 1.0, 0.0)
    pick = jnp.where((ti // n_tok == si) & (ti % n_tok == n_tok - 1), 1.0, 0.0)
    hdot = lambda a, b: jnp.dot(a, b, precision=HIGHEST, preferred_element_type=F32)

    lane128 = lax.broadcasted_iota(jnp.int32, (L, LANES), 1)
    even128 = lane128 < HEAD_DIM
    lane256 = lax.broadcasted_iota(jnp.int32, (1, 2 * LANES), 1)
    cols_e = (lane256 < HEAD_DIM) | (lane256 == LANES)
    cols_o = ((lane256 >= HEAD_DIM) & (lane256 < LANES)) | (lane256 == LANES + 1)
    rows_e = lax.broadcasted_iota(jnp.int32, (LANES, 1), 0) < HEAD_DIM
    ones_cols = jnp.where(lane128 < 2, 1.0, 0.0)
    bo_r = lax.broadcasted_iota(jnp.int32, (LANES, LANES), 0) // HEAD_DIM
    bo_c = lax.broadcasted_iota(jnp.int32, (LANES, LANES), 1) // HEAD_DIM
    block_ones = jnp.where(bo_r == bo_c, 1.0, 0.0)
    W = NB * LANES
    rb = lax.broadcasted_iota(jnp.int32, (L, W), 0)
    cb = lax.broadcasted_iota(jnp.int32, (L, W), 1)
    own_block = (rb // n_tok) == (cb // LANES)
    bd_tiled = (rb // HEAD_DIM) == ((cb % LANES) // HEAD_DIM)

    grow = grow_ref[...]
    i_row = grow[0:N_HEADS]
    b_row = hdot(grow[N_HEADS:2 * N_HEADS], useg)
    b_last = hdot(b_row, slast)
    a_row = b_last - b_row + i_row
    pos = lax.broadcasted_iota(jnp.int32, a_row.shape, 1) % n_tok
    pm = a_row
    sh = 1
    while sh < n_tok:
        pm = jnp.where(pos >= sh, jnp.maximum(pm, pltpu.roll(pm, sh, axis=1)), pm)
        sh *= 2
    m_carry = hdot(jnp.concatenate([m0t_ref[0], b_row], axis=1), jnp.concatenate([expand_t, slast], axis=0))
    m_new_row = jnp.maximum(m_carry, hdot(pm, slast))
    decay_row = jnp.exp(m_carry - m_new_row)
    w_row = jnp.exp(a_row - m_new_row)
    mt_ref[0] = hdot(m_new_row, pick)
    decay_bh = hdot(decay_row, pick)
    decay_hb = jnp.transpose(decay_bh)[0:NB]

    bc_all = gcol_ref[...]
    rowpos = lax.broadcasted_iota(jnp.int32, bc_all.shape, 0) % n_tok
    sh = 1
    while sh < n_tok:
        bc_all = bc_all + jnp.where(rowpos >= sh, pltpu.roll(bc_all, sh, axis=0), 0.0)
        sh *= 2
    pad_rows = lambda a: jnp.concatenate([a, jnp.zeros((L - NB, a.shape[1]), F32)], axis=0)
    m0_col = hdot(expand, pad_rows(m0_ref[...]))
    rowv_all = i_row - b_row

    for p in range(N_PAIRS):
        ls = slice(p * LANES, (p + 1) * LANES)
        q2f = jnp.transpose(qmt_ref[ls, :])
        q2 = q2f.astype(BF16)
        kt2 = jnp.transpose(km_ref[:, ls])
        v2 = jnp.transpose(vmt_ref[ls, :])
        vext = jnp.concatenate([v2, ones_cols], axis=1)
        kt_e = jnp.where(rows_e, kt2, 0.0).astype(BF16)
        kt_o = jnp.where(rows_e, 0.0, kt2).astype(BF16)
        s2 = jnp.dot(q2, jnp.concatenate([kt_e, kt_o], axis=1), preferred_element_type=F32)
        ps, mts, inters = [], [], []
        for hh in range(2):
            h = 2 * p + hh
            bcol = bc_all[:, N_HEADS + h:N_HEADS + h + 1]
            logd = jnp.where(causal, bcol + rowv_all[h:h + 1, :], NEG)
            m_inter = m0_col[:, h:h + 1] + bcol
            m_t = jnp.maximum(m_inter, jnp.max(logd, axis=1, keepdims=True))
            ps.append((s2[:, hh * L:(hh + 1) * L] * jnp.exp(logd - m_t)).astype(BF16))
            mts.append(m_t)
            inters.append(jnp.exp(m_inter - m_t))
        vstack = jnp.concatenate([jnp.where(cols_e, vext, 0.0), jnp.where(cols_o, vext, 0.0)],
                                 axis=0).astype(BF16)
        r = jnp.dot(jnp.concatenate(ps, axis=1), vstack, preferred_element_type=F32)
        zero = jnp.zeros((HEAD_DIM, HEAD_DIM), F32)
        cstack = jnp.concatenate(
            [jnp.concatenate([jnp.concatenate([c0_ref[b, 2 * p], zero], axis=1),
                              jnp.concatenate([zero, c0_ref[b, 2 * p + 1]], axis=1)], axis=0)
             for b in range(NB)], axis=1)
        rq_all = jnp.where(own_block, jnp.dot(q2, cstack.astype(BF16), preferred_element_type=F32), 0.0)
        rq = rq_all[:, 0:LANES]
        for b in range(1, NB):
            rq = rq + rq_all[:, b * LANES:(b + 1) * LANES]
        n_rows = hdot(expand, pad_rows(n0_ref[:, ls]))
        qn = hdot(q2f * n_rows, block_ones)
        inter2 = jnp.where(even128, inters[0], inters[1])
        num = r[:, 0:LANES] + inter2 * rq
        den = jnp.where(even128, r[:, LANES:LANES + 1], r[:, LANES + 1:LANES + 2]) + inter2 * qn
        mt2 = jnp.where(even128, mts[0], mts[1])
        hv = num / jnp.maximum(jnp.abs(den), jnp.exp(-mt2))
        ms = hdot(hv * hv, block_ones) * (1.0 / HEAD_DIM)
        y = hv * lax.rsqrt(ms + RMS_EPS) * gh_ref[:, ls] * _sigmoid(jnp.transpose(omt_ref[ls, :]))
        hm_ref[:, ls] = y.astype(hm_ref.dtype)
        kw = kt2 * jnp.where(rows_e, w_row[2 * p:2 * p + 1, :], w_row[2 * p + 1:2 * p + 2, :])
        vbd = jnp.where(own_block, jnp.concatenate([v2] * NB, axis=1), 0.0).astype(BF16)
        upd = jnp.dot(kw.astype(BF16), vbd, preferred_element_type=F32)
        upd = jnp.where(bd_tiled, upd, 0.0)
        for b in range(NB):
            bs = slice(b * LANES, (b + 1) * LANES)
            dec_b = jnp.where(rows_e, decay_hb[b:b + 1, 2 * p:2 * p + 1], decay_hb[b:b + 1, 2 * p + 1:2 * p + 2])
            cnew = dec_b * cstack[:, bs] + upd[:, bs]
            c_ref[b, 2 * p] = cnew[0:HEAD_DIM, 0:HEAD_DIM]
            c_ref[b, 2 * p + 1] = cnew[HEAD_DIM:LANES, HEAD_DIM:LANES]
        nsum = jnp.transpose(hdot(kw, expand))[0:NB]
        dec_n = jnp.where(lax.broadcasted_iota(jnp.int32, (NB, LANES), 1) < HEAD_DIM,
                          decay_hb[:, 2 * p:2 * p + 1], decay_hb[:, 2 * p + 1:2 * p + 2])
        n_ref[:, ls] = dec_n * n0_ref[:, ls] + nsum


def _mlstm_sample(km, qmt, vmt, omt, gcol, grow, c0, n0, m0, gh, row0, n_tok):
    nb = c0.shape[0]
    g_nb = LANES // n_tok
    n_g = nb // g_nb
    blk0 = row0 // LANES
    tokb = lambda w: pl.BlockSpec((LANES, w), lambda i: (blk0 + i, 0))
    rowb = lambda r: pl.BlockSpec((r, LANES), lambda i: (0, blk0 + i))
    m0t = jnp.pad(m0.reshape(n_g, g_nb, N_HEADS).transpose(0, 2, 1),
                  ((0, 0), (0, 0), (0, LANES - g_nb)))
    return pl.pallas_call(
        functools.partial(_mlstm_sample_kernel, n_tok),
        grid=(n_g,),
        in_specs=[tokb(DM), rowb(DM), rowb(DM), rowb(DM), tokb(LANES), rowb(2 * N_HEADS),
                  pl.BlockSpec((g_nb, N_HEADS, HEAD_DIM, HEAD_DIM), lambda i: (i, 0, 0, 0)),
                  pl.BlockSpec((g_nb, DM), lambda i: (i, 0)),
                  pl.BlockSpec((g_nb, N_HEADS), lambda i: (i, 0)),
                  pl.BlockSpec((1, N_HEADS, LANES), lambda i: (i, 0, 0)),
                  pl.BlockSpec((1, DM), lambda i: (0, 0))],
        out_specs=(pl.BlockSpec((LANES, DM), lambda i: (i, 0)),
                   pl.BlockSpec((g_nb, N_HEADS, HEAD_DIM, HEAD_DIM), lambda i: (i, 0, 0, 0)),
                   pl.BlockSpec((g_nb, DM), lambda i: (i, 0)),
                   pl.BlockSpec((1, N_HEADS, LANES), lambda i: (i, 0, 0))),
        out_shape=(jax.ShapeDtypeStruct((nb * n_tok, DM), BF16), jax.ShapeDtypeStruct(c0.shape, F32),
                   jax.ShapeDtypeStruct((nb, DM), F32), jax.ShapeDtypeStruct((n_g, N_HEADS, LANES), F32)),
        compiler_params=_cparams(("arbitrary",), VMEM_LIMIT),
        name="mlstm_sample",
    )(km, qmt, vmt, omt, gcol, grow, c0, n0.reshape(nb, DM), m0, m0t, gh)


def _alibi_slope(h):
    return float(np.float32(2.0 ** (-8.0 * (h + 1) / N_HEADS)))


def _dup_halves(x):
    lane = lax.broadcasted_iota(jnp.int32, x.shape, 1)
    xr = pltpu.roll(x, HEAD_DIM, axis=1)
    lo = lane < HEAD_DIM
    return jnp.where(lo, x, xr), jnp.where(lo, xr, x)


def _stack_group_queries(q, g):
    lane = lax.broadcasted_iota(jnp.int32, (q.shape[0], LANES), 1)
    parts = []
    for hh in range(GROUP):
        h = GROUP * g + hh
        blk = q[:, (h // 2) * LANES:(h // 2 + 1) * LANES]
        keep = (lane < HEAD_DIM) if h % 2 == 0 else (lane >= HEAD_DIM)
        parts.append(jnp.where(keep, blk, 0.0))
    return jnp.concatenate(parts, axis=0).astype(BF16)


def _swa_prompt_kernel(sink_ref, q_ref, kp_ref, ko_ref, vp_ref, vo_ref, o_ref):
    j = pl.program_id(1)
    R = WINDOW
    qi = lax.broadcasted_iota(jnp.int32, (R, R), 0)
    kj = lax.broadcasted_iota(jnp.int32, (R, R), 1)
    own = kj <= qi
    distf = jnp.where(own, qi - kj, qi - kj + R).astype(F32)
    lane = lax.broadcasted_iota(jnp.int32, (R, LANES), 1)
    for u in range(SWA_BLOCKS):
        rs = slice(u * R, (u + 1) * R)
        q = q_ref[rs, :]
        k_prev = kp_ref[...] if u == 0 else ko_ref[(u - 1) * R:u * R, :]
        v_prev = vp_ref[...] if u == 0 else vo_ref[(u - 1) * R:u * R, :]
        kd = _dup_halves(jnp.concatenate([k_prev, ko_ref[rs, :]], axis=0))
        vd = _dup_halves(jnp.concatenate([v_prev, vo_ref[rs, :]], axis=0))
        valid = (own | (j > 0)) if u == 0 else None
        outs = []
        for g in range(N_KV):
            qs = _stack_group_queries(q, g)
            s = lax.dot_general(qs, kd[g].astype(BF16), (((1,), (1,)), ((), ())),
                                preferred_element_type=F32) * (HEAD_DIM ** -0.5)
            ps = []
            for hh in range(GROUP):
                h = GROUP * g + hh
                sink = sink_ref[h]
                rows = slice(hh * R, (hh + 1) * R)
                sh = jnp.where(own, s[rows, R:], s[rows, :R]) - _alibi_slope(h) * distf
                if valid is not None:
                    sh = jnp.where(valid, sh, NEG)
                mx = jnp.maximum(jnp.max(sh, axis=1, keepdims=True), sink)
                p = jnp.exp(sh - mx)
                p = p / (jnp.sum(p, axis=1, keepdims=True) + jnp.exp(sink - mx))
                ps.append(jnp.concatenate([jnp.where(own, 0.0, p), jnp.where(own, p, 0.0)], axis=1).astype(BF16))
            o = jnp.dot(jnp.concatenate(ps, axis=0), vd[g].astype(BF16), preferred_element_type=F32)
            for pp in range(GROUP // 2):
                outs.append(jnp.where(lane < HEAD_DIM, o[(2 * pp) * R:(2 * pp + 1) * R],
                                      o[(2 * pp + 1) * R:(2 * pp + 2) * R]))
        o_ref[rs, :] = jnp.concatenate(outs, axis=1).astype(o_ref.dtype)


def _swa_prompt(sinks, qa, ka, va, batch, seq):
    nb = seq // (WINDOW * SWA_BLOCKS)
    own = lambda w: pl.BlockSpec((WINDOW * SWA_BLOCKS, w), lambda b, j: (b * nb + j, 0))
    prev = lambda w: pl.BlockSpec(
        (WINDOW, w), lambda b, j: (b * nb * SWA_BLOCKS + jnp.maximum(j * SWA_BLOCKS - 1, 0), 0))
    return pl.pallas_call(
        _swa_prompt_kernel,
        grid=(batch, nb),
        in_specs=[pl.BlockSpec(memory_space=pltpu.SMEM), own(DM), prev(DKV), own(DKV), prev(DKV), own(DKV)],
        out_specs=own(DM),
        out_shape=jax.ShapeDtypeStruct((batch * seq, DM), BF16),
        compiler_params=_cparams(("arbitrary", "arbitrary"), VMEM_LIMIT),
        name="swa_prompt",
    )(sinks, qa, ka, ka, va, va)


def _swa_decode_kernel(n_tok, sink_ref, q_ref, kn_ref, vn_ref, kc_ref, vc_ref, o_ref, ko_ref, vo_ref):
    W = WINDOW
    L = LANES
    NB = L // n_tok
    for b in range(NB):
        ko_ref[b, 0:W - n_tok, :] = kc_ref[b, n_tok:W, :]
        ko_ref[b, W - n_tok:W, :] = kn_ref[b * n_tok:(b + 1) * n_tok, :]
        vo_ref[b, 0:W - n_tok, :] = vc_ref[b, n_tok:W, :]
        vo_ref[b, W - n_tok:W, :] = vn_ref[b * n_tok:(b + 1) * n_tok, :]
    q = q_ref[...]
    knd, vnd = _dup_halves(kn_ref[...]), _dup_halves(vn_ref[...])
    kcd = _dup_halves(kc_ref[...].reshape(NB * W, DKV))
    vcd = _dup_halves(vc_ref[...].reshape(NB * W, DKV))
    ri = lax.broadcasted_iota(jnp.int32, (L, L), 0)
    ci = lax.broadcasted_iota(jnp.int32, (L, L), 1)
    t_q = ri % n_tok
    dist_n = t_q - ci % n_tok
    valid_n = ((ri // n_tok) == (ci // n_tok)) & (dist_n >= 0)
    dist_c = t_q + W - ci
    valid_c = dist_c < WINDOW
    dnf = dist_n.astype(F32)
    dcf = dist_c.astype(F32)
    own1 = (lax.broadcasted_iota(jnp.int32, (L, NB * W), 0) // n_tok
            == lax.broadcasted_iota(jnp.int32, (L, NB * W), 1) // W)
    own4 = ((lax.broadcasted_iota(jnp.int32, (GROUP * L, NB * W), 0) % L) // n_tok
            == lax.broadcasted_iota(jnp.int32, (GROUP * L, NB * W), 1) // W)
    lane = lax.broadcasted_iota(jnp.int32, (L, LANES), 1)
    nt = (((1,), (1,)), ((), ()))
    outs = []
    for g in range(N_KV):
        qs = _stack_group_queries(q, g)
        sn = lax.dot_general(qs, knd[g].astype(BF16), nt, preferred_element_type=F32) * (HEAD_DIM ** -0.5)
        sc_all = lax.dot_general(qs, kcd[g].astype(BF16), nt, preferred_element_type=F32) * (HEAD_DIM ** -0.5)
        pcs, pns = [], []
        for hh in range(GROUP):
            h = GROUP * g + hh
            sink = sink_ref[h]
            slope = _alibi_slope(h)
            rows = slice(hh * L, (hh + 1) * L)
            blk = jnp.where(own1, sc_all[rows], 0.0)
            sc = blk[:, 0:W]
            for b in range(1, NB):
                sc = sc + blk[:, b * W:(b + 1) * W]
            shc = jnp.where(valid_c, sc - slope * dcf, NEG)
            shn = jnp.where(valid_n, sn[rows] - slope * dnf, NEG)
            mx = jnp.maximum(jnp.maximum(jnp.max(shc, axis=1, keepdims=True),
                                         jnp.max(shn, axis=1, keepdims=True)), sink)
            pc = jnp.exp(shc - mx)
            pn = jnp.exp(shn - mx)
            inv = 1.0 / (jnp.sum(pc, axis=1, keepdims=True) + jnp.sum(pn, axis=1, keepdims=True)
                         + jnp.exp(sink - mx))
            pcs.append(pc * inv)
            pns.append((pn * inv).astype(BF16))
        pc4 = jnp.concatenate(pcs, axis=0)
        p_bd = jnp.where(own4, jnp.concatenate([pc4] * NB, axis=1), 0.0).astype(BF16)
        o = (jnp.dot(p_bd, vcd[g].astype(BF16), preferred_element_type=F32)
             + jnp.dot(jnp.concatenate(pns, axis=0), vnd[g].astype(BF16), preferred_element_type=F32))
        for pp in range(GROUP // 2):
            outs.append(jnp.where(lane < HEAD_DIM, o[(2 * pp) * L:(2 * pp + 1) * L],
                                  o[(2 * pp + 1) * L:(2 * pp + 2) * L]))
    o_ref[...] = jnp.concatenate(outs, axis=1)


def _swa_decode(sinks, qa, ka, va, kc, vc, row0, n_tok):
    nb = kc.shape[0]
    g_nb = LANES // n_tok
    blk0 = row0 // LANES
    tokb = lambda w: pl.BlockSpec((LANES, w), lambda i: (blk0 + i, 0))
    cache = pl.BlockSpec((g_nb, WINDOW, DKV), lambda i: (i, 0, 0))
    return pl.pallas_call(
        functools.partial(_swa_decode_kernel, n_tok),
        grid=(nb // g_nb,),
        in_specs=[pl.BlockSpec(memory_space=pltpu.SMEM), tokb(DM), tokb(DKV), tokb(DKV), cache, cache],
        out_specs=(pl.BlockSpec((LANES, DM), lambda i: (i, 0)), cache, cache),
        out_shape=(jax.ShapeDtypeStruct((nb * n_tok, DM), F32),
                   jax.ShapeDtypeStruct(kc.shape, F32), jax.ShapeDtypeStruct(vc.shape, F32)),
        compiler_params=_cparams(("arbitrary",), VMEM_LIMIT),
        name="swa_decode",
    )(sinks, qa, ka, va, kc, vc)


def _outproj_router_kernel(n_ptiles, xp_ref, xs_ref, hmp_ref, hap_ref, hms_ref, has_ref, wom_ref, woa_ref,
                           g_ref, wrt_ref, br_ref,
                           x1_ref, xg_ref, meta_ref, cnt_ref):
    i = pl.program_id(0)
    is_p = i < n_ptiles
    x = jnp.where(is_p, xp_ref[...], xs_ref[...])
    hm = jnp.where(is_p, hmp_ref[...], hms_ref[...].astype(BF16))
    ha = jnp.where(is_p, hap_ref[...], has_ref[...].astype(BF16))
    x1 = (x + jnp.dot(hm, wom_ref[...], preferred_element_type=F32)
          + jnp.dot(ha, woa_ref[...], preferred_element_type=F32))
    x1_ref[...] = x1
    h2 = _rms(x1, g_ref[...])
    logits = lax.dot_general(wrt_ref[...], h2, (((1,), (1,)), ((), ())), precision=HIGHEST,
                             preferred_element_type=F32) + br_ref[...]
    eidx = lax.broadcasted_iota(jnp.int32, logits.shape, 0).astype(F32)
    work = logits
    vals, hots = [], []
    for _ in range(TOP_K):
        mv = jnp.max(work, axis=0, keepdims=True)
        sel = jnp.min(jnp.where(work == mv, eidx, float(N_EXPERTS)), axis=0, keepdims=True)
        hot = eidx == sel
        vals.append(mv)
        hots.append(hot)
        work = jnp.where(hot, -jnp.inf, work)
    es = [jnp.exp(v - vals[0]) for v in vals]
    tot = es[0] + es[1] + es[2] + es[3]
    gates = [e / tot for e in es]
    hot_all = jnp.where(hots[0] | hots[1] | hots[2] | hots[3], 1.0, 0.0)
    tm = logits.shape[1]
    su = (lax.broadcasted_iota(jnp.int32, (tm, tm), 0) < lax.broadcasted_iota(jnp.int32, (tm, tm), 1))
    cum = jnp.dot(hot_all.astype(BF16), su.astype(BF16), preferred_element_type=F32)
    cnt = jnp.sum(hot_all, axis=1, keepdims=True)
    cpad = (((cnt.astype(jnp.int32) + (SUBLANES - 1)) // SUBLANES) * SUBLANES).astype(F32)
    lower = (lax.broadcasted_iota(jnp.int32, (N_EXPERTS, N_EXPERTS), 0)
             > lax.broadcasted_iota(jnp.int32, (N_EXPERTS, N_EXPERTS), 1)).astype(F32)
    lstart = jnp.dot(lower, jnp.broadcast_to(cpad, (N_EXPERTS, LANES)), precision=HIGHEST,
                     preferred_element_type=F32)[:, 0:1]
    base = lstart + cum
    lpos = [jnp.sum(jnp.where(hot, base, 0.0), axis=0, keepdims=True) for hot in hots]
    lpi = [p.astype(jnp.int32) for p in lpos]
    r_iota = lax.broadcasted_iota(jnp.int32, (GROUP_R, tm), 0)
    sel01 = jnp.where(r_iota == lpi[0], 1.0, jnp.where(r_iota == lpi[1], 1.0, jnp.where(
        r_iota == lpi[2], 1.0, jnp.where(r_iota == lpi[3], 1.0, 0.0)))).astype(BF16)
    xg_ref[...] = jnp.dot(sel01, h2.astype(BF16), preferred_element_type=F32)
    meta_ref[...] = jnp.transpose(jnp.concatenate(gates + lpos, axis=0))
    cnt_ref[0] = jnp.broadcast_to(cnt, (N_EXPERTS, LANES))


def _outproj_router(xp, xs, hmp, hap, hms, has, wom, woa, g_ffn, wrt, br):
    tp, ts = xp.shape[0], xs.shape[0]
    n_pt, n_st = tp // ROUTE_T, ts // ROUTE_T
    t_all = tp + ts
    pblk = lambda w: pl.BlockSpec((ROUTE_T, w), lambda i: (jnp.minimum(i, n_pt - 1), 0))
    sblk = lambda w: pl.BlockSpec((ROUTE_T, w), lambda i: (jnp.maximum(i - n_pt, 0), 0))
    full = lambda a: pl.BlockSpec(a.shape, lambda i: (0,) * a.ndim)
    return pl.pallas_call(
        functools.partial(_outproj_router_kernel, n_pt),
        grid=(n_pt + n_st,),
        in_specs=[pblk(D_MODEL), sblk(D_MODEL), pblk(DM), pblk(DM), sblk(DM), sblk(DM),
                  full(wom), full(woa), full(g_ffn), full(wrt), full(br)],
        out_specs=(pl.BlockSpec((ROUTE_T, D_MODEL), lambda i: (i, 0)),
                   pl.BlockSpec((GROUP_R, D_MODEL), lambda i: (i, 0)),
                   pl.BlockSpec((ROUTE_T, 2 * TOP_K), lambda i: (i, 0)),
                   pl.BlockSpec((1, N_EXPERTS, LANES), lambda i: (i, 0, 0))),
        out_shape=(jax.ShapeDtypeStruct((t_all, D_MODEL), F32),
                   jax.ShapeDtypeStruct(((n_pt + n_st) * GROUP_R, D_MODEL), F32),
                   jax.ShapeDtypeStruct((t_all, 2 * TOP_K), F32),
                   jax.ShapeDtypeStruct((n_pt + n_st, N_EXPERTS, LANES), F32)),
        compiler_params=_cparams(("arbitrary",), VMEM_LIMIT),
        name="outproj_router",
    )(xp, xs, hmp, hap, hms, has, wom, woa, g_ffn, wrt, br)


def _expert_kernel(be_ref, na_ref, slot_ref, nxt_ref, ctab_ref, ctab_next_ref, xg_ref, wgu_ref, bgu_ref,
                   wd_ref, bd_ref, o_ref, xbuf, wgu_f, wd_f, wgu_s, wd_s, xsem, wsem):
    i = pl.program_id(0)
    na = na_ref[0]

    def x_copies(tab_ref, slot):
        return [pltpu.make_async_copy(xg_ref.at[pl.ds(pl.multiple_of(tab_ref[0, 0, c], SUBLANES), SUBLANES)],
                                      xbuf.at[slot, pl.ds(c * SUBLANES, SUBLANES)], xsem.at[slot])
                for c in range(MOE_BM // SUBLANES)]

    def w_copies(e, slot):
        return [pltpu.make_async_copy(wgu_ref.at[e], wgu_f.at[slot], wsem.at[slot]),
                pltpu.make_async_copy(wd_ref.at[e], wd_f.at[slot], wsem.at[slot])]

    @pl.when(i == 0)
    def _():
        for cp in x_copies(ctab_ref, 0) + w_copies(be_ref[0], 0):
            cp.start()

    @pl.when(i < na)
    def _():
        changed = (i == 0) | (be_ref[i] != be_ref[jnp.maximum(i - 1, 0)])
        wslot = slot_ref[i]
        xslot = lax.rem(i, 2)

        @pl.when(i + 1 < na)
        def _():
            for cp in x_copies(ctab_next_ref, 1 - xslot):
                cp.start()

        @pl.when(changed)
        def _():
            for cp in w_copies(be_ref[i], wslot):
                cp.wait()

            @pl.when(nxt_ref[i] >= 0)
            def _():
                for cp in w_copies(nxt_ref[i], 1 - wslot):
                    cp.start(priority=1)

            wgu_s[...] = wgu_f[wslot].astype(BF16)
            wd_s[...] = wd_f[wslot].astype(BF16)

        for cp in x_copies(ctab_ref, xslot):
            cp.wait()

        x = xbuf[xslot].astype(BF16)
        gu = jnp.dot(x, wgu_s[...], preferred_element_type=F32) + bgu_ref[...]
        gate = jnp.minimum(gu[:, :D_FF], SWIGLU_LIMIT)
        up = jnp.clip(gu[:, D_FF:], -SWIGLU_LIMIT, SWIGLU_LIMIT)
        act = (up + 1.0) * (gate * _sigmoid(gate * SWIGLU_ALPHA))
        o_ref[...] = jnp.dot(act.astype(BF16), wd_s[...], preferred_element_type=F32) + bd_ref[...]


def _expert_ffn(block_exp, n_active, wslot, next_exp, ctab, xg, wgu, bgu, wd, bd):
    n_blocks = ctab.shape[0]
    n_rows = n_blocks * MOE_BM
    nch = MOE_BM // SUBLANES
    grid_spec = pltpu.PrefetchScalarGridSpec(
        num_scalar_prefetch=4,
        grid=(n_blocks,),
        in_specs=[pl.BlockSpec((1, 1, nch), lambda i, *_: (i, 0, 0), memory_space=pltpu.SMEM),
                  pl.BlockSpec((1, 1, nch), lambda i, *_: (jnp.minimum(i + 1, n_blocks - 1), 0, 0),
                               memory_space=pltpu.SMEM),
                  pl.BlockSpec(memory_space=pl.ANY),
                  pl.BlockSpec(memory_space=pl.ANY),
                  pl.BlockSpec((None, 1, 2 * D_FF), lambda i, be, *_: (be[i], 0, 0)),
                  pl.BlockSpec(memory_space=pl.ANY),
                  pl.BlockSpec((None, 1, D_MODEL), lambda i, be, *_: (be[i], 0, 0))],
        out_specs=pl.BlockSpec((MOE_BM, D_MODEL), lambda i, be, na, *_: (jnp.minimum(i, na[0] - 1), 0)),
        scratch_shapes=[pltpu.VMEM((2, MOE_BM, D_MODEL), F32),
                        pltpu.VMEM((2, D_MODEL, 2 * D_FF), F32), pltpu.VMEM((2, D_FF, D_MODEL), F32),
                        pltpu.VMEM((D_MODEL, 2 * D_FF), BF16), pltpu.VMEM((D_FF, D_MODEL), BF16),
                        pltpu.SemaphoreType.DMA((2,)), pltpu.SemaphoreType.DMA((2,))],
    )
    return pl.pallas_call(
        _expert_kernel,
        grid_spec=grid_spec,
        out_shape=jax.ShapeDtypeStruct((n_rows, D_MODEL), F32),
        compiler_params=_cparams(("arbitrary",), VMEM_LIMIT),
        name="moe_experts",
    )(block_exp, n_active, wslot, next_exp, ctab, ctab, xg, wgu, bgu, wd, bd)


def _combine_kernel(n_ptiles, ctab_ref, ctab_next_ref, outs_ref, x1_ref, meta_ref, gf_ref, yp_ref, ys_ref,
                    obuf, sem):
    i = pl.program_id(0)
    n = pl.num_programs(0)
    slot = lax.rem(i, 2)

    def copies(tab_ref, s):
        return [pltpu.make_async_copy(outs_ref.at[pl.ds(pl.multiple_of(tab_ref[0, 0, c], SUBLANES), SUBLANES)],
                                      obuf.at[s, pl.ds(c * SUBLANES, SUBLANES)], sem.at[s])
                for c in range(GROUP_R // SUBLANES)]

    @pl.when(i == 0)
    def _():
        for cp in copies(ctab_ref, 0):
            cp.start()

    @pl.when(i + 1 < n)
    def _():
        for cp in copies(ctab_next_ref, 1 - slot):
            cp.start()

    for cp in copies(ctab_ref, slot):
        cp.wait()

    meta = meta_ref[...]
    tm = meta.shape[0]
    r_iota = lax.broadcasted_iota(jnp.int32, (tm, GROUP_R), 1)
    lp = [meta[:, TOP_K + k:TOP_K + k + 1].astype(jnp.int32) for k in range(TOP_K)]
    gk = [meta[:, k:k + 1] for k in range(TOP_K)]
    gsel = jnp.where(r_iota == lp[0], gk[0], jnp.where(r_iota == lp[1], gk[1], jnp.where(
        r_iota == lp[2], gk[2], jnp.where(r_iota == lp[3], gk[3], 0.0))))
    sel01 = jnp.where(gsel != 0.0, 1.0, 0.0).astype(BF16)
    rg_row = jnp.sum(gsel, axis=0, keepdims=True)
    rg_col = jnp.transpose(jnp.broadcast_to(rg_row, (SUBLANES, GROUP_R)))[:, 0:1]
    og = (obuf[slot] * rg_col).astype(BF16)
    acc = x1_ref[...] + jnp.dot(sel01, og, preferred_element_type=F32)
    y = _rms(acc, gf_ref[...])

    @pl.when(i < n_ptiles)
    def _():
        yp_ref[...] = y

    @pl.when(i >= n_ptiles)
    def _():
        ys_ref[...] = y


def _combine(ctab, outs, x1, meta, g_final, tp, ts):
    n_pt, n_st = tp // ROUTE_T, ts // ROUTE_T
    n = n_pt + n_st
    nch = GROUP_R // SUBLANES
    return pl.pallas_call(
        functools.partial(_combine_kernel, n_pt),
        grid=(n,),
        in_specs=[pl.BlockSpec((1, 1, nch), lambda i: (i, 0, 0), memory_space=pltpu.SMEM),
                  pl.BlockSpec((1, 1, nch), lambda i: (jnp.minimum(i + 1, n - 1), 0, 0), memory_space=pltpu.SMEM),
                  pl.BlockSpec(memory_space=pl.ANY),
                  pl.BlockSpec((ROUTE_T, D_MODEL), lambda i: (i, 0)),
                  pl.BlockSpec((ROUTE_T, 2 * TOP_K), lambda i: (i, 0)),
                  pl.BlockSpec((1, D_MODEL), lambda i: (0, 0))],
        out_specs=(pl.BlockSpec((ROUTE_T, D_MODEL), lambda i: (jnp.minimum(i, n_pt - 1), 0)),
                   pl.BlockSpec((ROUTE_T, D_MODEL), lambda i: (jnp.maximum(i - n_pt, 0), 0))),
        out_shape=(jax.ShapeDtypeStruct((tp, D_MODEL), F32), jax.ShapeDtypeStruct((ts, D_MODEL), F32)),
        scratch_shapes=[pltpu.VMEM((2, GROUP_R, D_MODEL), F32), pltpu.SemaphoreType.DMA((2,))],
        compiler_params=_cparams(("arbitrary",), VMEM_LIMIT),
        name="moe_combine",
    )(ctab, ctab, outs, x1, meta, g_final)


def kernel(x_prompt, x_sample, cache_swa_k, cache_swa_v, state_mlstm_c, state_mlstm_n, state_mlstm_m,
           g_mix, w_in, b_igate, b_fgate, g_head, attn_sinks, w_out, g_ffn, w_router, b_router,
           w_gate_up, b_gate_up, w_down, b_down, g_final):
    assert w_in.shape[0] == 1, "single-layer problem"
    B, S, _ = x_prompt.shape
    Bd, Tn, _ = x_sample.shape
    tp, ts = B * S, Bd * Tn
    t_all = tp + ts
    xp = x_prompt.reshape(tp, D_MODEL)
    xs = x_sample.reshape(ts, D_MODEL)

    w = w_in[0]
    o = np.cumsum([0, DM, DM, DM, DM, N_HEADS, N_HEADS, DM, DKV, DKV])
    col = lambda a: w[:, int(o[a]):int(o[a + 1])]
    wgates = jnp.concatenate([col(4), col(5)], axis=1)
    w1 = jnp.concatenate([col(1), col(6), col(7), col(8), jnp.pad(wgates, ((0, 0), (0, LANES - 2 * N_HEADS)))],
                         axis=1).astype(BF16)
    wkt = jnp.concatenate([col(0), col(2), col(3), wgates], axis=1).T.astype(BF16)
    bg = jnp.concatenate([b_igate[0], b_fgate[0]]).astype(F32)
    bcol = jnp.pad(bg, (0, LANES - 2 * N_HEADS)).reshape(1, LANES)
    brow = bg.reshape(2 * N_HEADS, 1)

    km, qa, ka, va, qmt, vmt, omt, gcol, grow = _inproj(xp, xs, g_mix[0].reshape(1, D_MODEL), w1, wkt, bcol, brow)

    gh = g_head[0].astype(F32)
    sinks = attn_sinks[0].astype(F32)

    hm_p, ctp, m_p = _mlstm_prompt(km, qmt, vmt, omt, gcol, grow,
                                   jnp.broadcast_to(gh.reshape(DM, 1), (DM, LANES)), B, S)
    ha_p = _swa_prompt(sinks, qa, ka, va, B, S)

    hm_s, c_s, n_s, mt_s = _mlstm_sample(km, qmt, vmt, omt, gcol, grow, state_mlstm_c[0], state_mlstm_n[0],
                                         state_mlstm_m[0], gh.reshape(1, DM), tp, Tn)
    n_s = n_s.reshape(Bd, N_HEADS, HEAD_DIM)
    m_s = mt_s[:, :, :LANES // Tn].transpose(0, 2, 1).reshape(Bd, N_HEADS)
    ha_s, k_s, v_s = _swa_decode(sinks, qa, ka, va, cache_swa_k[0].reshape(Bd, WINDOW, DKV),
                                 cache_swa_v[0].reshape(Bd, WINDOW, DKV), tp, Tn)

    wo = w_out[0].astype(BF16)
    x1, xg, meta, cnt = _outproj_router(
        xp, xs, hm_p, ha_p, hm_s, ha_s, wo[:DM], wo[DM:], g_ffn[0].reshape(1, D_MODEL),
        w_router[0].T, b_router[0].reshape(N_EXPERTS, 1))

    i32 = jnp.int32
    n_tiles = t_all // ROUTE_T
    max_rows = t_all * TOP_K + n_tiles * N_EXPERTS * (SUBLANES - 1) + N_EXPERTS * (MOE_BM - 1)
    n_blocks = -(-max_rows // MOE_BM)
    cpad = (cnt[:, :, 0].astype(i32) + (SUBLANES - 1)) // SUBLANES * SUBLANES
    lstart = jnp.cumsum(cpad, axis=1) - cpad
    goff = jnp.cumsum(cpad, axis=0) - cpad
    padded = (jnp.sum(cpad, axis=0) + MOE_BM - 1) // MOE_BM * MOE_BM
    pad_end = jnp.cumsum(padded)
    seg_begin = (pad_end - padded)[None, :] + goff
    n_active = (pad_end[-1] // MOE_BM).astype(i32)
    blk = jnp.minimum(jnp.arange(n_blocks, dtype=i32), n_active - 1)
    block_exp = jnp.minimum(jnp.sum((pad_end[None, :] <= (blk * MOE_BM)[:, None]).astype(i32), axis=1),
                            N_EXPERTS - 1)
    e_ids = jnp.arange(N_EXPERTS, dtype=i32)
    nonempty = padded > 0
    nxt_e = jnp.min(jnp.where((e_ids[None, :] > e_ids[:, None]) & nonempty[None, :], e_ids[None, :], N_EXPERTS),
                    axis=1)
    nxt_e = jnp.where(nxt_e == N_EXPERTS, -1, nxt_e)
    ord_e = jnp.cumsum(nonempty.astype(i32)) - 1
    be_hot = block_exp[:, None] == e_ids[None, :]
    next_exp = jnp.sum(jnp.where(be_hot, nxt_e[None, :], 0), axis=1).astype(i32)
    wslot = (jnp.sum(jnp.where(be_hot, ord_e[None, :], 0), axis=1) % 2).astype(i32)

    seg_src = jnp.arange(n_tiles, dtype=i32)[:, None] * GROUP_R + lstart
    sb, sl, ss = seg_begin.reshape(-1), cpad.reshape(-1), seg_src.reshape(-1)
    rc = jnp.arange(n_blocks * MOE_BM // SUBLANES, dtype=i32)[:, None] * SUBLANES
    inseg = (sb[None, :] <= rc) & (rc < (sb + sl)[None, :])
    ctab_e = jnp.where(jnp.any(inseg, axis=1), jnp.sum(jnp.where(inseg, (ss - sb)[None, :] + rc, 0), axis=1),
                       GROUP_R - SUBLANES)
    lr = jnp.arange(GROUP_R // SUBLANES, dtype=i32)[None, :, None] * SUBLANES
    inl = (lstart[:, None, :] <= lr) & (lr < (lstart + cpad)[:, None, :])
    ctab_c = jnp.sum(jnp.where(inl, (seg_begin - lstart)[:, None, :] + lr, 0), axis=2)

    outs = _expert_ffn(block_exp, n_active.reshape(1), wslot, next_exp,
                       ctab_e.astype(i32).reshape(n_blocks, 1, MOE_BM // SUBLANES), xg, w_gate_up[0],
                       b_gate_up[0].reshape(N_EXPERTS, 1, 2 * D_FF), w_down[0],
                       b_down[0].reshape(N_EXPERTS, 1, D_MODEL))
    y_p, y_s = _combine(ctab_c.astype(i32).reshape(n_tiles, 1, GROUP_R // SUBLANES), outs, x1, meta,
                        g_final.reshape(1, D_MODEL), tp, ts)

    kv_tail = lambda a: jnp.concatenate([a[(b + 1) * S - WINDOW:(b + 1) * S] for b in range(B)], axis=0).reshape(
        1, B, WINDOW, N_KV, HEAD_DIM)
    c_e = ctp[:, :, :HEAD_DIM, :HEAD_DIM]
    c_o = ctp[:, :, HEAD_DIM:LANES, HEAD_DIM:]
    c_p = jnp.swapaxes(jnp.stack([c_e, c_o], axis=2), -1, -2).reshape(B, N_HEADS, HEAD_DIM, HEAD_DIM)
    n_p = (ctp[:, :, LANES:LANES + 2, :HEAD_DIM] + ctp[:, :, LANES:LANES + 2, HEAD_DIM:]).reshape(B, N_HEADS, HEAD_DIM)
    return (y_p.reshape(B, S, D_MODEL), y_s.reshape(Bd, Tn, D_MODEL),
            kv_tail(ka), kv_tail(va), c_p[None], n_p[None], m_p[:, :, 0][None],
            k_s.reshape(Bd, WINDOW, N_KV, HEAD_DIM)[None], v_s.reshape(Bd, WINDOW, N_KV, HEAD_DIM)[None],
            c_s[None], n_s[None], m_s[None])
```

```python
import functools

import jax
import jax.numpy as jnp
import numpy as np
from jax import lax
from jax.experimental import pallas as pl
from jax.experimental.pallas import tpu as pltpu

F32 = jnp.float32
BF16 = jnp.bfloat16
HIGHEST = lax.Precision.HIGHEST

D_MODEL = 1024
HEAD_DIM = 64
N_HEADS = 8
N_PAIRS = N_HEADS // 2
N_KV = 2
GROUP = N_HEADS // N_KV
WINDOW = 128
N_EXPERTS = 32
TOP_K = 4
D_FF = 1024
SWIGLU_LIMIT = 7.0
SWIGLU_ALPHA = 1.702
RMS_EPS = 1e-5
DM = N_HEADS * HEAD_DIM
DKV = N_KV * HEAD_DIM
NEG = -1e30

LANES = 128
SUBLANES = 8
VMEM_LIMIT = 56 * 1024 * 1024

TM = 512
ROUTE_T = 512
MLSTM_TL = 512
MLSTM_L = 128
SWA_BLOCKS = 8
MOE_BM = 256
GROUP_R = -(-(TOP_K * ROUTE_T + N_EXPERTS * (SUBLANES - 1) + SUBLANES) // LANES) * LANES


def _cparams(sem, vmem=None):
    return pltpu.CompilerParams(dimension_semantics=sem, vmem_limit_bytes=vmem)


def _rms(x, g):
    return x * lax.rsqrt(jnp.mean(x * x, axis=-1, keepdims=True) + RMS_EPS) * g


def _log_sigmoid(z):
    return jnp.minimum(z, 0.0) - jnp.log(1.0 + jnp.exp(-jnp.abs(z)))


def _sigmoid(z):
    return 1.0 / (1.0 + jnp.exp(-z))


def _inproj_kernel(n_ptiles, xp_ref, xs_ref, g_ref, w1_ref, wt_ref, bcol_ref, brow_ref,
                   km_ref, qa_ref, ka_ref, va_ref, qmt_ref, vmt_ref, omt_ref, gcol_ref, grow_ref):
    i = pl.program_id(0)
    x = jnp.where(i < n_ptiles, xp_ref[...], xs_ref[...])
    h = _rms(x, g_ref[...]).astype(BF16)
    main = jnp.dot(h, w1_ref[...], preferred_element_type=F32)
    km_ref[...] = main[:, 0:DM] * (HEAD_DIM ** -0.5)
    qa_ref[...] = main[:, DM:2 * DM]
    ka_ref[...] = main[:, 2 * DM:2 * DM + DKV]
    va_ref[...] = main[:, 2 * DM + DKV:2 * DM + 2 * DKV]
    t = lax.dot_general(wt_ref[...], h, (((1,), (1,)), ((), ())), preferred_element_type=F32)
    qmt_ref[...] = t[0:DM]
    vmt_ref[...] = t[DM:2 * DM]
    omt_ref[...] = t[2 * DM:3 * DM]
    zc = main[:, 2 * DM + 2 * DKV:] + bcol_ref[...]
    lane = lax.broadcasted_iota(jnp.int32, zc.shape, 1)
    gcol_ref[...] = jnp.where(lane < N_HEADS, zc, _log_sigmoid(zc))
    zr = t[3 * DM:] + brow_ref[...]
    row = lax.broadcasted_iota(jnp.int32, zr.shape, 0)
    grow_ref[...] = jnp.where(row < N_HEADS, zr, _log_sigmoid(zr))


def _inproj(xp, xs, g_mix, w1, wkt, bcol, brow):
    tp, ts = xp.shape[0], xs.shape[0]
    n_pt, n_st = tp // TM, ts // TM
    t_all = tp + ts
    tok = lambda w: pl.BlockSpec((TM, w), lambda i: (i, 0))
    tr = lambda r: pl.BlockSpec((r, TM), lambda i: (0, i))
    full = lambda a: pl.BlockSpec(a.shape, lambda i: (0,) * a.ndim)
    out_shape = (
        jax.ShapeDtypeStruct((t_all, DM), F32), jax.ShapeDtypeStruct((t_all, DM), F32),
        jax.ShapeDtypeStruct((t_all, DKV), F32), jax.ShapeDtypeStruct((t_all, DKV), F32),
        jax.ShapeDtypeStruct((DM, t_all), F32), jax.ShapeDtypeStruct((DM, t_all), F32),
        jax.ShapeDtypeStruct((DM, t_all), F32),
        jax.ShapeDtypeStruct((t_all, LANES), F32), jax.ShapeDtypeStruct((2 * N_HEADS, t_all), F32),
    )
    return pl.pallas_call(
        functools.partial(_inproj_kernel, n_pt),
        grid=(n_pt + n_st,),
        in_specs=[
            pl.BlockSpec((TM, D_MODEL), lambda i: (jnp.minimum(i, n_pt - 1), 0)),
            pl.BlockSpec((TM, D_MODEL), lambda i: (jnp.maximum(i - n_pt, 0), 0)),
            full(g_mix), full(w1), full(wkt), full(bcol), full(brow),
        ],
        out_specs=(tok(DM), tok(DM), tok(DKV), tok(DKV), tr(DM), tr(DM), tr(DM),
                   tok(LANES), tr(2 * N_HEADS)),
        out_shape=out_shape,
        compiler_params=_cparams(("arbitrary",), VMEM_LIMIT),
        name="inproj",
    )(xp, xs, g_mix, w1, wkt, bcol, brow)


CT_ROWS = LANES + 2 * SUBLANES


def _cumsum_rows(x, n):
    row = lax.broadcasted_iota(jnp.int32, x.shape, 0)
    sh = 1
    while sh < n:
        x = x + jnp.where(row >= sh, pltpu.roll(x, sh, axis=0), 0.0)
        sh *= 2
    return x


def _mlstm_prompt_kernel(km_ref, qmt_ref, vmt_ref, omt_ref, gcol_ref, grow_ref, ghr_ref,
                         hm_ref, ct_ref, m_ref, ct_s, m_s):
    j = pl.program_id(1)
    L = MLSTM_L
    assert L == LANES

    @pl.when(j == 0)
    def _():
        ct_s[...] = jnp.zeros_like(ct_s)
        m_s[...] = jnp.zeros_like(m_s)

    si = lax.broadcasted_iota(jnp.int32, (L, L), 0)
    ti = lax.broadcasted_iota(jnp.int32, (L, L), 1)
    causal_t = si <= ti
    upper = jnp.where(causal_t, 1.0, 0.0)
    rows_c = lax.broadcasted_iota(jnp.int32, (CT_ROWS, 1), 0)
    rmask_e = (rows_c < HEAD_DIM) | (rows_c == LANES)
    rmask_o = ((rows_c >= HEAD_DIM) & (rows_c < LANES)) | (rows_c == LANES + 1)
    rows_e = lax.broadcasted_iota(jnp.int32, (LANES, 1), 0) < HEAD_DIM
    cols_e = lax.broadcasted_iota(jnp.int32, (1, LANES), 1) < HEAD_DIM
    bdt_mask = (rmask_e & cols_e) | (rmask_o & (~cols_e))
    ones_rows = jnp.where(lax.broadcasted_iota(jnp.int32, (CT_ROWS - LANES, L), 0) < 2, 1.0, 0.0)

    for c in range(MLSTM_TL // L):
        sl = slice(c * L, (c + 1) * L)
        grow = grow_ref[:, sl]
        i_row = grow[0:N_HEADS]
        b_row = jnp.dot(grow[N_HEADS:2 * N_HEADS], upper, precision=HIGHEST,
                        preferred_element_type=F32)
        gc = gcol_ref[sl, :]
        bc_all = _cumsum_rows(gc, L)
        for p in range(N_PAIRS):
            ls = slice(p * LANES, (p + 1) * LANES)
            k2 = km_ref[sl, ls]
            qt2 = qmt_ref[ls, sl]
            qt_e = jnp.where(rows_e, qt2, 0.0).astype(BF16)
            qt_o = jnp.where(rows_e, 0.0, qt2).astype(BF16)
            st2 = jnp.dot(k2.astype(BF16), jnp.concatenate([qt_e, qt_o], axis=1),
                          preferred_element_type=F32)
            ct = ct_s[p]
            rqt = jnp.dot(ct.astype(BF16), qt2.astype(BF16), preferred_element_type=F32)
            pts, mts, inters, wreps, decays, mnews = [], [], [], [], [], []
            for hh in range(2):
                h = 2 * p + hh
                cvec = jnp.broadcast_to(gc[:, h:h + 1] - bc_all[:, N_HEADS + h:N_HEADS + h + 1], (L, L))
                brow = b_row[h:h + 1, :]
                logdt = jnp.where(causal_t, cvec + brow, NEG)
                m_prev = m_s[h:h + 1, 0:1]
                m_inter = m_prev + brow
                m_t = jnp.maximum(m_inter, jnp.max(logdt, axis=0, keepdims=True))
                pts.append((st2[:, hh * L:(hh + 1) * L] * jnp.exp(logdt - m_t)).astype(BF16))
                mts.append(m_t)
                inters.append(jnp.exp(m_inter - m_t))
                m_new = m_t[:, L - 1:L]
                b_last = brow[:, L - 1:L]
                decays.append(jnp.exp(m_prev + b_last - m_new))
                wreps.append(jnp.exp(cvec + (b_last - m_new)))
                mnews.append(m_new)
            vext = jnp.concatenate([vmt_ref[ls, sl], ones_rows], axis=0)
            lhs = jnp.concatenate([jnp.where(rmask_e, vext, 0.0), jnp.where(rmask_o, vext, 0.0)],
                                  axis=1).astype(BF16)
            rt = jnp.dot(lhs, jnp.concatenate(pts, axis=0), preferred_element_type=F32)
            ndt = rt + jnp.where(rmask_e, inters[0], inters[1]) * rqt
            den = jnp.where(rows_e, ndt[LANES:LANES + 1], ndt[LANES + 1:LANES + 2])
            mt2 = jnp.where(rows_e, mts[0], mts[1])
            hvt = ndt[0:LANES] / jnp.maximum(jnp.abs(den), jnp.exp(-mt2))
            sq = hvt * hvt
            ms = jnp.where(rows_e, jnp.sum(sq[0:HEAD_DIM], axis=0, keepdims=True),
                           jnp.sum(sq[HEAD_DIM:LANES], axis=0, keepdims=True)) * (1.0 / HEAD_DIM)
            yt = hvt * lax.rsqrt(ms + RMS_EPS) * ghr_ref[ls, :] * _sigmoid(omt_ref[ls, sl])
            hm_ref[sl, ls] = jnp.transpose(yt).astype(hm_ref.dtype)
            kw = (k2 * jnp.where(cols_e, wreps[0], wreps[1])).astype(BF16)
            upd = jnp.dot(vext.astype(BF16), kw, preferred_element_type=F32)
            ct_s[p] = jnp.where(rmask_e, decays[0], decays[1]) * ct + jnp.where(bdt_mask, upd, 0.0)
            for hh in range(2):
                h = 2 * p + hh
                m_s[h:h + 1, :] = jnp.broadcast_to(mnews[hh], (1, LANES))

    @pl.when(j == pl.num_programs(1) - 1)
    def _():
        ct_ref[0] = ct_s[...]
        m_ref[0] = m_s[...]


def _mlstm_prompt(km, qmt, vmt, omt, gcol, grow, ghr, batch, seq):
    nt = seq // MLSTM_TL
    tokb = lambda w: pl.BlockSpec((MLSTM_TL, w), lambda b, j: (b * nt + j, 0))
    rowb = lambda r: pl.BlockSpec((r, MLSTM_TL), lambda b, j: (0, b * nt + j))
    return pl.pallas_call(
        _mlstm_prompt_kernel,
        grid=(batch, nt),
        in_specs=[tokb(DM), rowb(DM), rowb(DM), rowb(DM), tokb(LANES), rowb(2 * N_HEADS),
                  pl.BlockSpec((DM, LANES), lambda b, j: (0, 0))],
        out_specs=(tokb(DM),
                   pl.BlockSpec((1, N_PAIRS, CT_ROWS, LANES), lambda b, j: (b, 0, 0, 0)),
                   pl.BlockSpec((1, N_HEADS, LANES), lambda b, j: (b, 0, 0))),
        out_shape=(jax.ShapeDtypeStruct((batch * seq, DM), BF16),
                   jax.ShapeDtypeStruct((batch, N_PAIRS, CT_ROWS, LANES), F32),
                   jax.ShapeDtypeStruct((batch, N_HEADS, LANES), F32)),
        scratch_shapes=[pltpu.VMEM((N_PAIRS, CT_ROWS, LANES), F32), pltpu.VMEM((N_HEADS, LANES), F32)],
        compiler_params=_cparams(("arbitrary", "arbitrary"), VMEM_LIMIT),
        name="mlstm_prompt",
    )(km, qmt, vmt, omt, gcol, grow, ghr)


def _mlstm_sample_kernel(n_tok, km_ref, qmt_ref, vmt_ref, omt_ref, gcol_ref, grow_ref, c0_ref, n0_ref, m0_ref,
                         m0t_ref, gh_ref, hm_ref, c_ref, n_ref, mt_ref):
    L = LANES
    NB = L // n_tok
    ti = lax.broadcasted_iota(jnp.int32, (L, L), 0)
    si = lax.broadcasted_iota(jnp.int32, (L, L), 1)
    same = (ti // n_tok) == (si // n_tok)
    causal = same & (ti >= si)
    useg = jnp.where(same & (ti <= si), 1.0, 0.0)
    slast = jnp.where(same & (ti % n_tok == n_tok - 1), 1.0, 0.0)
    expand = jnp.where(ti // n_tok == si, 1.0, 0.0)
    expand_t = jnp.where(ti == si // n_tok, 1.0, 0.0)
    pick = jnp.where((ti // n_tok == si) & (ti % n_tok == n_tok - 1), 1.0, 0.0)
    hdot = lambda a, b: jnp.dot(a, b, precision=HIGHEST, preferred_element_type=F32)

    lane128 = lax.broadcasted_iota(jnp.int32, (L, LANES), 1)
    even128 = lane128 < HEAD_DIM
    lane256 = lax.broadcasted_iota(jnp.int32, (1, 2 * LANES), 1)
    cols_e = (lane256 < HEAD_DIM) | (lane256 == LANES)
    cols_o = ((lane256 >= HEAD_DIM) & (lane256 < LANES)) | (lane256 == LANES + 1)
    rows_e = lax.broadcasted_iota(jnp.int32, (LANES, 1), 0) < HEAD_DIM
    ones_cols = jnp.where(lane128 < 2, 1.0, 0.0)
    bo_r = lax.broadcasted_iota(jnp.int32, (LANES, LANES), 0) // HEAD_DIM
    bo_c = lax.broadcasted_iota(jnp.int32, (LANES, LANES), 1) // HEAD_DIM
    block_ones = jnp.where(bo_r == bo_c, 1.0, 0.0)
    W = NB * LANES
    rb = lax.broadcasted_iota(jnp.int32, (L, W), 0)
    cb = lax.broadcasted_iota(jnp.int32, (L, W), 1)
    own_block = (rb // n_tok) == (cb // LANES)
    bd_tiled = (rb // HEAD_DIM) == ((cb % LANES) // HEAD_DIM)

    grow = grow_ref[...]
    i_row = grow[0:N_HEADS]
    b_row = hdot(grow[N_HEADS:2 * N_HEADS], useg)
    b_last = hdot(b_row, slast)
    a_row = b_last - b_row + i_row
    pos = lax.broadcasted_iota(jnp.int32, a_row.shape, 1) % n_tok
    pm = a_row
    sh = 1
    while sh < n_tok:
        pm = jnp.where(pos >= sh, jnp.maximum(pm, pltpu.roll(pm, sh, axis=1)), pm)
        sh *= 2
    m_carry = hdot(jnp.concatenate([m0t_ref[0], b_row], axis=1), jnp.concatenate([expand_t, slast], axis=0))
    m_new_row = jnp.maximum(m_carry, hdot(pm, slast))
    decay_row = jnp.exp(m_carry - m_new_row)
    w_row = jnp.exp(a_row - m_new_row)
    mt_ref[0] = hdot(m_new_row, pick)
    decay_bh = hdot(decay_row, pick)
    decay_hb = jnp.transpose(decay_bh)[0:NB]

    bc_all = gcol_ref[...]
    rowpos = lax.broadcasted_iota(jnp.int32, bc_all.shape, 0) % n_tok
    sh = 1
    while sh < n_tok:
        bc_all = bc_all + jnp.where(rowpos >= sh, pltpu.roll(bc_all, sh, axis=0), 0.0)
        sh *= 2
    pad_rows = lambda a: jnp.concatenate([a, jnp.zeros((L - NB, a.shape[1]), F32)], axis=0)
    m0_col = hdot(expand, pad_rows(m0_ref[...]))
    rowv_all = i_row - b_row

    for p in range(N_PAIRS):
        ls = slice(p * LANES, (p + 1) * LANES)
        q2f = jnp.transpose(qmt_ref[ls, :])
        q2 = q2f.astype(BF16)
        kt2 = jnp.transpose(km_ref[:, ls])
        v2 = jnp.transpose(vmt_ref[ls, :])
        vext = jnp.concatenate([v2, ones_cols], axis=1)
        kt_e = jnp.where(rows_e, kt2, 0.0).astype(BF16)
        kt_o = jnp.where(rows_e, 0.0, kt2).astype(BF16)
        s2 = jnp.dot(q2, jnp.concatenate([kt_e, kt_o], axis=1), preferred_element_type=F32)
        ps, mts, inters = [], [], []
        for hh in range(2):
            h = 2 * p + hh
            bcol = bc_all[:, N_HEADS + h:N_HEADS + h + 1]
            logd = jnp.where(causal, bcol + rowv_all[h:h + 1, :], NEG)
            m_inter = m0_col[:, h:h + 1] + bcol
            m_t = jnp.maximum(m_inter, jnp.max(logd, axis=1, keepdims=True))
            ps.append((s2[:, hh * L:(hh + 1) * L] * jnp.exp(logd - m_t)).astype(BF16))
            mts.append(m_t)
            inters.append(jnp.exp(m_inter - m_t))
        vstack = jnp.concatenate([jnp.where(cols_e, vext, 0.0), jnp.where(cols_o, vext, 0.0)],
                                 axis=0).astype(BF16)
        r = jnp.dot(jnp.concatenate(ps, axis=1), vstack, preferred_element_type=F32)
        zero = jnp.zeros((HEAD_DIM, HEAD_DIM), F32)
        cstack = jnp.concatenate(
            [jnp.concatenate([jnp.concatenate([c0_ref[b, 2 * p], zero], axis=1),
                              jnp.concatenate([zero, c0_ref[b, 2 * p + 1]], axis=1)], axis=0)
             for b in range(NB)], axis=1)
        rq_all = jnp.where(own_block, jnp.dot(q2, cstack.astype(BF16), preferred_element_type=F32), 0.0)
        rq = rq_all[:, 0:LANES]
        for b in range(1, NB):
            rq = rq + rq_all[:, b * LANES:(b + 1) * LANES]
        n_rows = hdot(expand, pad_rows(n0_ref[:, ls]))
        qn = hdot(q2f * n_rows, block_ones)
        inter2 = jnp.where(even128, inters[0], inters[1])
        num = r[:, 0:LANES] + inter2 * rq
        den = jnp.where(even128, r[:, LANES:LANES + 1], r[:, LANES + 1:LANES + 2]) + inter2 * qn
        mt2 = jnp.where(even128, mts[0], mts[1])
        hv = num / jnp.maximum(jnp.abs(den), jnp.exp(-mt2))
        ms = hdot(hv * hv, block_ones) * (1.0 / HEAD_DIM)
        y = hv * lax.rsqrt(ms + RMS_EPS) * gh_ref[:, ls] * _sigmoid(jnp.transpose(omt_ref[ls, :]))
        hm_ref[:, ls] = y.astype(hm_ref.dtype)
        kw = kt2 * jnp.where(rows_e, w_row[2 * p:2 * p + 1, :], w_row[2 * p + 1:2 * p + 2, :])
        vbd = jnp.where(own_block, jnp.concatenate([v2] * NB, axis=1), 0.0).astype(BF16)
        upd = jnp.dot(kw.astype(BF16), vbd, preferred_element_type=F32)
        upd = jnp.where(bd_tiled, upd, 0.0)
        for b in range(NB):
            bs = slice(b * LANES, (b + 1) * LANES)
            dec_b = jnp.where(rows_e, decay_hb[b:b + 1, 2 * p:2 * p + 1], decay_hb[b:b + 1, 2 * p + 1:2 * p + 2])
            cnew = dec_b * cstack[:, bs] + upd[:, bs]
            c_ref[b, 2 * p] = cnew[0:HEAD_DIM, 0:HEAD_DIM]
            c_ref[b, 2 * p + 1] = cnew[HEAD_DIM:LANES, HEAD_DIM:LANES]
        nsum = jnp.transpose(hdot(kw, expand))[0:NB]
        dec_n = jnp.where(lax.broadcasted_iota(jnp.int32, (NB, LANES), 1) < HEAD_DIM,
                          decay_hb[:, 2 * p:2 * p + 1], decay_hb[:, 2 * p + 1:2 * p + 2])
        n_ref[:, ls] = dec_n * n0_ref[:, ls] + nsum


def _mlstm_sample(km, qmt, vmt, omt, gcol, grow, c0, n0, m0, gh, row0, n_tok):
    nb = c0.shape[0]
    g_nb = LANES // n_tok
    n_g = nb // g_nb
    blk0 = row0 // LANES
    tokb = lambda w: pl.BlockSpec((LANES, w), lambda i: (blk0 + i, 0))
    rowb = lambda r: pl.BlockSpec((r, LANES), lambda i: (0, blk0 + i))
    m0t = jnp.pad(m0.reshape(n_g, g_nb, N_HEADS).transpose(0, 2, 1),
                  ((0, 0), (0, 0), (0, LANES - g_nb)))
    return pl.pallas_call(
        functools.partial(_mlstm_sample_kernel, n_tok),
        grid=(n_g,),
        in_specs=[tokb(DM), rowb(DM), rowb(DM), rowb(DM), tokb(LANES), rowb(2 * N_HEADS),
                  pl.BlockSpec((g_nb, N_HEADS, HEAD_DIM, HEAD_DIM), lambda i: (i, 0, 0, 0)),
                  pl.BlockSpec((g_nb, DM), lambda i: (i, 0)),
                  pl.BlockSpec((g_nb, N_HEADS), lambda i: (i, 0)),
                  pl.BlockSpec((1, N_HEADS, LANES), lambda i: (i, 0, 0)),
                  pl.BlockSpec((1, DM), lambda i: (0, 0))],
        out_specs=(pl.BlockSpec((LANES, DM), lambda i: (i, 0)),
                   pl.BlockSpec((g_nb, N_HEADS, HEAD_DIM, HEAD_DIM), lambda i: (i, 0, 0, 0)),
                   pl.BlockSpec((g_nb, DM), lambda i: (i, 0)),
                   pl.BlockSpec((1, N_HEADS, LANES), lambda i: (i, 0, 0))),
        out_shape=(jax.ShapeDtypeStruct((nb * n_tok, DM), BF16), jax.ShapeDtypeStruct(c0.shape, F32),
                   jax.ShapeDtypeStruct((nb, DM), F32), jax.ShapeDtypeStruct((n_g, N_HEADS, LANES), F32)),
        compiler_params=_cparams(("arbitrary",), VMEM_LIMIT),
        name="mlstm_sample",
    )(km, qmt, vmt, omt, gcol, grow, c0, n0.reshape(nb, DM), m0, m0t, gh)


def _alibi_slope(h):
    return float(np.float32(2.0 ** (-8.0 * (h + 1) / N_HEADS)))


def _dup_halves(x):
    lane = lax.broadcasted_iota(jnp.int32, x.shape, 1)
    xr = pltpu.roll(x, HEAD_DIM, axis=1)
    lo = lane < HEAD_DIM
    return jnp.where(lo, x, xr), jnp.where(lo, xr, x)


def _stack_group_queries(q, g):
    lane = lax.broadcasted_iota(jnp.int32, (q.shape[0], LANES), 1)
    parts = []
    for hh in range(GROUP):
        h = GROUP * g + hh
        blk = q[:, (h // 2) * LANES:(h // 2 + 1) * LANES]
        keep = (lane < HEAD_DIM) if h % 2 == 0 else (lane >= HEAD_DIM)
        parts.append(jnp.where(keep, blk, 0.0))
    return jnp.concatenate(parts, axis=0).astype(BF16)


def _swa_prompt_kernel(sink_ref, q_ref, kp_ref, ko_ref, vp_ref, vo_ref, o_ref):
    j = pl.program_id(1)
    R = WINDOW
    qi = lax.broadcasted_iota(jnp.int32, (R, R), 0)
    kj = lax.broadcasted_iota(jnp.int32, (R, R), 1)
    own = kj <= qi
    distf = jnp.where(own, qi - kj, qi - kj + R).astype(F32)
    lane = lax.broadcasted_iota(jnp.int32, (R, LANES), 1)
    for u in range(SWA_BLOCKS):
        rs = slice(u * R, (u + 1) * R)
        q = q_ref[rs, :]
        k_prev = kp_ref[...] if u == 0 else ko_ref[(u - 1) * R:u * R, :]
        v_prev = vp_ref[...] if u == 0 else vo_ref[(u - 1) * R:u * R, :]
        kd = _dup_halves(jnp.concatenate([k_prev, ko_ref[rs, :]], axis=0))
        vd = _dup_halves(jnp.concatenate([v_prev, vo_ref[rs, :]], axis=0))
        valid = (own | (j > 0)) if u == 0 else None
        outs = []
        for g in range(N_KV):
            qs = _stack_group_queries(q, g)
            s = lax.dot_general(qs, kd[g].astype(BF16), (((1,), (1,)), ((), ())),
                                preferred_element_type=F32) * (HEAD_DIM ** -0.5)
            ps = []
            for hh in range(GROUP):
                h = GROUP * g + hh
                sink = sink_ref[h]
                rows = slice(hh * R, (hh + 1) * R)
                sh = jnp.where(own, s[rows, R:], s[rows, :R]) - _alibi_slope(h) * distf
                if valid is not None:
                    sh = jnp.where(valid, sh, NEG)
                mx = jnp.maximum(jnp.max(sh, axis=1, keepdims=True), sink)
                p = jnp.exp(sh - mx)
                p = p / (jnp.sum(p, axis=1, keepdims=True) + jnp.exp(sink - mx))
                ps.append(jnp.concatenate([jnp.where(own, 0.0, p), jnp.where(own, p, 0.0)], axis=1).astype(BF16))
            o = jnp.dot(jnp.concatenate(ps, axis=0), vd[g].astype(BF16), preferred_element_type=F32)
            for pp in range(GROUP // 2):
                outs.append(jnp.where(lane < HEAD_DIM, o[(2 * pp) * R:(2 * pp + 1) * R],
                                      o[(2 * pp + 1) * R:(2 * pp + 2) * R]))
        o_ref[rs, :] = jnp.concatenate(outs, axis=1).astype(o_ref.dtype)


def _swa_prompt(sinks, qa, ka, va, batch, seq):
    nb = seq // (WINDOW * SWA_BLOCKS)
    own = lambda w: pl.BlockSpec((WINDOW * SWA_BLOCKS, w), lambda b, j: (b * nb + j, 0))
    prev = lambda w: pl.BlockSpec(
        (WINDOW, w), lambda b, j: (b * nb * SWA_BLOCKS + jnp.maximum(j * SWA_BLOCKS - 1, 0), 0))
    return pl.pallas_call(
        _swa_prompt_kernel,
        grid=(batch, nb),
        in_specs=[pl.BlockSpec(memory_space=pltpu.SMEM), own(DM), prev(DKV), own(DKV), prev(DKV), own(DKV)],
        out_specs=own(DM),
        out_shape=jax.ShapeDtypeStruct((batch * seq, DM), BF16),
        compiler_params=_cparams(("arbitrary", "arbitrary"), VMEM_LIMIT),
        name="swa_prompt",
    )(sinks, qa, ka, ka, va, va)


def _swa_decode_kernel(n_tok, sink_ref, q_ref, kn_ref, vn_ref, kc_ref, vc_ref, o_ref, ko_ref, vo_ref):
    W = WINDOW
    L = LANES
    NB = L // n_tok
    for b in range(NB):
        ko_ref[b, 0:W - n_tok, :] = kc_ref[b, n_tok:W, :]
        ko_ref[b, W - n_tok:W, :] = kn_ref[b * n_tok:(b + 1) * n_tok, :]
        vo_ref[b, 0:W - n_tok, :] = vc_ref[b, n_tok:W, :]
        vo_ref[b, W - n_tok:W, :] = vn_ref[b * n_tok:(b + 1) * n_tok, :]
    q = q_ref[...]
    knd, vnd = _dup_halves(kn_ref[...]), _dup_halves(vn_ref[...])
    kcd = _dup_halves(kc_ref[...].reshape(NB * W, DKV))
    vcd = _dup_halves(vc_ref[...].reshape(NB * W, DKV))
    ri = lax.broadcasted_iota(jnp.int32, (L, L), 0)
    ci = lax.broadcasted_iota(jnp.int32, (L, L), 1)
    t_q = ri % n_tok
    dist_n = t_q - ci % n_tok
    valid_n = ((ri // n_tok) == (ci // n_tok)) & (dist_n >= 0)
    dist_c = t_q + W - ci
    valid_c = dist_c < WINDOW
    dnf = dist_n.astype(F32)
    dcf = dist_c.astype(F32)
    own1 = (lax.broadcasted_iota(jnp.int32, (L, NB * W), 0) // n_tok
            == lax.broadcasted_iota(jnp.int32, (L, NB * W), 1) // W)
    own4 = ((lax.broadcasted_iota(jnp.int32, (GROUP * L, NB * W), 0) % L) // n_tok
            == lax.broadcasted_iota(jnp.int32, (GROUP * L, NB * W), 1) // W)
    lane = lax.broadcasted_iota(jnp.int32, (L, LANES), 1)
    nt = (((1,), (1,)), ((), ()))
    outs = []
    for g in range(N_KV):
        qs = _stack_group_queries(q, g)
        sn = lax.dot_general(qs, knd[g].astype(BF16), nt, preferred_element_type=F32) * (HEAD_DIM ** -0.5)
        sc_all = lax.dot_general(qs, kcd[g].astype(BF16), nt, preferred_element_type=F32) * (HEAD_DIM ** -0.5)
        pcs, pns = [], []
        for hh in range(GROUP):
            h = GROUP * g + hh
            sink = sink_ref[h]
            slope = _alibi_slope(h)
            rows = slice(hh * L, (hh + 1) * L)
            blk = jnp.where(own1, sc_all[rows], 0.0)
            sc = blk[:, 0:W]
            for b in range(1, NB):
                sc = sc + blk[:, b * W:(b + 1) * W]
            shc = jnp.where(valid_c, sc - slope * dcf, NEG)
            shn = jnp.where(valid_n, sn[rows] - slope * dnf, NEG)
            mx = jnp.maximum(jnp.maximum(jnp.max(shc, axis=1, keepdims=True),
                                         jnp.max(shn, axis=1, keepdims=True)), sink)
            pc = jnp.exp(shc - mx)
            pn = jnp.exp(shn - mx)
            inv = 1.0 / (jnp.sum(pc, axis=1, keepdims=True) + jnp.sum(pn, axis=1, keepdims=True)
                         + jnp.exp(sink - mx))
            pcs.append(pc * inv)
            pns.append((pn * inv).astype(BF16))
        pc4 = jnp.concatenate(pcs, axis=0)
        p_bd = jnp.where(own4, jnp.concatenate([pc4] * NB, axis=1), 0.0).astype(BF16)
        o = (jnp.dot(p_bd, vcd[g].astype(BF16), preferred_element_type=F32)
             + jnp.dot(jnp.concatenate(pns, axis=0), vnd[g].astype(BF16), preferred_element_type=F32))
        for pp in range(GROUP // 2):
            outs.append(jnp.where(lane < HEAD_DIM, o[(2 * pp) * L:(2 * pp + 1) * L],
                                  o[(2 * pp + 1) * L:(2 * pp + 2) * L]))
    o_ref[...] = jnp.concatenate(outs, axis=1)


def _swa_decode(sinks, qa, ka, va, kc, vc, row0, n_tok):
    nb = kc.shape[0]
    g_nb = LANES // n_tok
    blk0 = row0 // LANES
    tokb = lambda w: pl.BlockSpec((LANES, w), lambda i: (blk0 + i, 0))
    cache = pl.BlockSpec((g_nb, WINDOW, DKV), lambda i: (i, 0, 0))
    return pl.pallas_call(
        functools.partial(_swa_decode_kernel, n_tok),
        grid=(nb // g_nb,),
        in_specs=[pl.BlockSpec(memory_space=pltpu.SMEM), tokb(DM), tokb(DKV), tokb(DKV), cache, cache],
        out_specs=(pl.BlockSpec((LANES, DM), lambda i: (i, 0)), cache, cache),
        out_shape=(jax.ShapeDtypeStruct((nb * n_tok, DM), F32),
                   jax.ShapeDtypeStruct(kc.shape, F32), jax.ShapeDtypeStruct(vc.shape, F32)),
        compiler_params=_cparams(("arbitrary",), VMEM_LIMIT),
        name="swa_decode",
    )(sinks, qa, ka, va, kc, vc)


def _outproj_router_kernel(n_ptiles, xp_ref, xs_ref, hmp_ref, hap_ref, hms_ref, has_ref, wom_ref, woa_ref,
                           g_ref, wrt_ref, br_ref,
                           x1_ref, xg_ref, meta_ref, cnt_ref):
    i = pl.program_id(0)
    is_p = i < n_ptiles
    x = jnp.where(is_p, xp_ref[...], xs_ref[...])
    hm = jnp.where(is_p, hmp_ref[...], hms_ref[...].astype(BF16))
    ha = jnp.where(is_p, hap_ref[...], has_ref[...].astype(BF16))
    x1 = (x + jnp.dot(hm, wom_ref[...], preferred_element_type=F32)
          + jnp.dot(ha, woa_ref[...], preferred_element_type=F32))
    x1_ref[...] = x1
    h2 = _rms(x1, g_ref[...])
    logits = lax.dot_general(wrt_ref[...], h2, (((1,), (1,)), ((), ())), precision=HIGHEST,
                             preferred_element_type=F32) + br_ref[...]
    eidx = lax.broadcasted_iota(jnp.int32, logits.shape, 0).astype(F32)
    work = logits
    vals, hots = [], []
    for _ in range(TOP_K):
        mv = jnp.max(work, axis=0, keepdims=True)
        sel = jnp.min(jnp.where(work == mv, eidx, float(N_EXPERTS)), axis=0, keepdims=True)
        hot = eidx == sel
        vals.append(mv)
        hots.append(hot)
        work = jnp.where(hot, -jnp.inf, work)
    es = [jnp.exp(v - vals[0]) for v in vals]
    tot = es[0] + es[1] + es[2] + es[3]
    gates = [e / tot for e in es]
    hot_all = jnp.where(hots[0] | hots[1] | hots[2] | hots[3], 1.0, 0.0)
    tm = logits.shape[1]
    su = (lax.broadcasted_iota(jnp.int32, (tm, tm), 0) < lax.broadcasted_iota(jnp.int32, (tm, tm), 1))
    cum = jnp.dot(hot_all.astype(BF16), su.astype(BF16), preferred_element_type=F32)
    cnt = jnp.sum(hot_all, axis=1, keepdims=True)
    cpad = (((cnt.astype(jnp.int32) + (SUBLANES - 1)) // SUBLANES) * SUBLANES).astype(F32)
    lower = (lax.broadcasted_iota(jnp.int32, (N_EXPERTS, N_EXPERTS), 0)
             > lax.broadcasted_iota(jnp.int32, (N_EXPERTS, N_EXPERTS), 1)).astype(F32)
    lstart = jnp.dot(lower, jnp.broadcast_to(cpad, (N_EXPERTS, LANES)), precision=HIGHEST,
                     preferred_element_type=F32)[:, 0:1]
    base = lstart + cum
    lpos = [jnp.sum(jnp.where(hot, base, 0.0), axis=0, keepdims=True) for hot in hots]
    lpi = [p.astype(jnp.int32) for p in lpos]
    r_iota = lax.broadcasted_iota(jnp.int32, (GROUP_R, tm), 0)
    sel01 = jnp.where(r_iota == lpi[0], 1.0, jnp.where(r_iota == lpi[1], 1.0, jnp.where(
        r_iota == lpi[2], 1.0, jnp.where(r_iota == lpi[3], 1.0, 0.0)))).astype(BF16)
    xg_ref[...] = jnp.dot(sel01, h2.astype(BF16), preferred_element_type=F32)
    meta_ref[...] = jnp.transpose(jnp.concatenate(gates + lpos, axis=0))
    cnt_ref[0] = jnp.broadcast_to(cnt, (N_EXPERTS, LANES))


def _outproj_router(xp, xs, hmp, hap, hms, has, wom, woa, g_ffn, wrt, br):
    tp, ts = xp.shape[0], xs.shape[0]
    n_pt, n_st = tp // ROUTE_T, ts // ROUTE_T
    t_all = tp + ts
    pblk = lambda w: pl.BlockSpec((ROUTE_T, w), lambda i: (jnp.minimum(i, n_pt - 1), 0))
    sblk = lambda w: pl.BlockSpec((ROUTE_T, w), lambda i: (jnp.maximum(i - n_pt, 0), 0))
    full = lambda a: pl.BlockSpec(a.shape, lambda i: (0,) * a.ndim)
    return pl.pallas_call(
        functools.partial(_outproj_router_kernel, n_pt),
        grid=(n_pt + n_st,),
        in_specs=[pblk(D_MODEL), sblk(D_MODEL), pblk(DM), pblk(DM), sblk(DM), sblk(DM),
                  full(wom), full(woa), full(g_ffn), full(wrt), full(br)],
        out_specs=(pl.BlockSpec((ROUTE_T, D_MODEL), lambda i: (i, 0)),
                   pl.BlockSpec((GROUP_R, D_MODEL), lambda i: (i, 0)),
                   pl.BlockSpec((ROUTE_T, 2 * TOP_K), lambda i: (i, 0)),
                   pl.BlockSpec((1, N_EXPERTS, LANES), lambda i: (i, 0, 0))),
        out_shape=(jax.ShapeDtypeStruct((t_all, D_MODEL), F32),
                   jax.ShapeDtypeStruct(((n_pt + n_st) * GROUP_R, D_MODEL), F32),
                   jax.ShapeDtypeStruct((t_all, 2 * TOP_K), F32),
                   jax.ShapeDtypeStruct((n_pt + n_st, N_EXPERTS, LANES), F32)),
        compiler_params=_cparams(("arbitrary",), VMEM_LIMIT),
        name="outproj_router",
    )(xp, xs, hmp, hap, hms, has, wom, woa, g_ffn, wrt, br)


def _expert_kernel(be_ref, na_ref, slot_ref, nxt_ref, ctab_ref, ctab_next_ref, xg_ref, wgu_ref, bgu_ref,
                   wd_ref, bd_ref, o_ref, xbuf, wgu_f, wd_f, wgu_s, wd_s, xsem, wsem):
    i = pl.program_id(0)
    na = na_ref[0]

    def x_copies(tab_ref, slot):
        return [pltpu.make_async_copy(xg_ref.at[pl.ds(pl.multiple_of(tab_ref[0, 0, c], SUBLANES), SUBLANES)],
                                      xbuf.at[slot, pl.ds(c * SUBLANES, SUBLANES)], xsem.at[slot])
                for c in range(MOE_BM // SUBLANES)]

    def w_copies(e, slot):
        return [pltpu.make_async_copy(wgu_ref.at[e], wgu_f.at[slot], wsem.at[slot]),
                pltpu.make_async_copy(wd_ref.at[e], wd_f.at[slot], wsem.at[slot])]

    @pl.when(i == 0)
    def _():
        for cp in x_copies(ctab_ref, 0) + w_copies(be_ref[0], 0):
            cp.start()

    @pl.when(i < na)
    def _():
        changed = (i == 0) | (be_ref[i] != be_ref[jnp.maximum(i - 1, 0)])
        wslot = slot_ref[i]
        xslot = lax.rem(i, 2)

        @pl.when(i + 1 < na)
        def _():
            for cp in x_copies(ctab_next_ref, 1 - xslot):
                cp.start()

        @pl.when(changed)
        def _():
            for cp in w_copies(be_ref[i], wslot):
                cp.wait()

            @pl.when(nxt_ref[i] >= 0)
            def _():
                for cp in w_copies(nxt_ref[i], 1 - wslot):
                    cp.start(priority=1)

            wgu_s[...] = wgu_f[wslot].astype(BF16)
            wd_s[...] = wd_f[wslot].astype(BF16)

        for cp in x_copies(ctab_ref, xslot):
            cp.wait()

        x = xbuf[xslot].astype(BF16)
        gu = jnp.dot(x, wgu_s[...], preferred_element_type=F32) + bgu_ref[...]
        gate = jnp.minimum(gu[:, :D_FF], SWIGLU_LIMIT)
        up = jnp.clip(gu[:, D_FF:], -SWIGLU_LIMIT, SWIGLU_LIMIT)
        act = (up + 1.0) * (gate * _sigmoid(gate * SWIGLU_ALPHA))
        o_ref[...] = jnp.dot(act.astype(BF16), wd_s[...], preferred_element_type=F32) + bd_ref[...]


def _expert_ffn(block_exp, n_active, wslot, next_exp, ctab, xg, wgu, bgu, wd, bd):
    n_blocks = ctab.shape[0]
    n_rows = n_blocks * MOE_BM
    nch = MOE_BM // SUBLANES
    grid_spec = pltpu.PrefetchScalarGridSpec(
        num_scalar_prefetch=4,
        grid=(n_blocks,),
        in_specs=[pl.BlockSpec((1, 1, nch), lambda i, *_: (i, 0, 0), memory_space=pltpu.SMEM),
                  pl.BlockSpec((1, 1, nch), lambda i, *_: (jnp.minimum(i + 1, n_blocks - 1), 0, 0),
                               memory_space=pltpu.SMEM),
                  pl.BlockSpec(memory_space=pl.ANY),
                  pl.BlockSpec(memory_space=pl.ANY),
                  pl.BlockSpec((None, 1, 2 * D_FF), lambda i, be, *_: (be[i], 0, 0)),
                  pl.BlockSpec(memory_space=pl.ANY),
                  pl.BlockSpec((None, 1, D_MODEL), lambda i, be, *_: (be[i], 0, 0))],
        out_specs=pl.BlockSpec((MOE_BM, D_MODEL), lambda i, be, na, *_: (jnp.minimum(i, na[0] - 1), 0)),
        scratch_shapes=[pltpu.VMEM((2, MOE_BM, D_MODEL), F32),
                        pltpu.VMEM((2, D_MODEL, 2 * D_FF), F32), pltpu.VMEM((2, D_FF, D_MODEL), F32),
                        pltpu.VMEM((D_MODEL, 2 * D_FF), BF16), pltpu.VMEM((D_FF, D_MODEL), BF16),
                        pltpu.SemaphoreType.DMA((2,)), pltpu.SemaphoreType.DMA((2,))],
    )
    return pl.pallas_call(
        _expert_kernel,
        grid_spec=grid_spec,
        out_shape=jax.ShapeDtypeStruct((n_rows, D_MODEL), F32),
        compiler_params=_cparams(("arbitrary",), VMEM_LIMIT),
        name="moe_experts",
    )(block_exp, n_active, wslot, next_exp, ctab, ctab, xg, wgu, bgu, wd, bd)


def _combine_kernel(n_ptiles, ctab_ref, ctab_next_ref, outs_ref, x1_ref, meta_ref, gf_ref, yp_ref, ys_ref,
                    obuf, sem):
    i = pl.program_id(0)
    n = pl.num_programs(0)
    slot = lax.rem(i, 2)

    def copies(tab_ref, s):
        return [pltpu.make_async_copy(outs_ref.at[pl.ds(pl.multiple_of(tab_ref[0, 0, c], SUBLANES), SUBLANES)],
                                      obuf.at[s, pl.ds(c * SUBLANES, SUBLANES)], sem.at[s])
                for c in range(GROUP_R // SUBLANES)]

    @pl.when(i == 0)
    def _():
        for cp in copies(ctab_ref, 0):
            cp.start()

    @pl.when(i + 1 < n)
    def _():
        for cp in copies(ctab_next_ref, 1 - slot):
            cp.start()

    for cp in copies(ctab_ref, slot):
        cp.wait()

    meta = meta_ref[...]
    tm = meta.shape[0]
    r_iota = lax.broadcasted_iota(jnp.int32, (tm, GROUP_R), 1)
    lp = [meta[:, TOP_K + k:TOP_K + k + 1].astype(jnp.int32) for k in range(TOP_K)]
    gk = [meta[:, k:k + 1] for k in range(TOP_K)]
    gsel = jnp.where(r_iota == lp[0], gk[0], jnp.where(r_iota == lp[1], gk[1], jnp.where(
        r_iota == lp[2], gk[2], jnp.where(r_iota == lp[3], gk[3], 0.0))))
    sel01 = jnp.where(gsel != 0.0, 1.0, 0.0).astype(BF16)
    rg_row = jnp.sum(gsel, axis=0, keepdims=True)
    rg_col = jnp.transpose(jnp.broadcast_to(rg_row, (SUBLANES, GROUP_R)))[:, 0:1]
    og = (obuf[slot] * rg_col).astype(BF16)
    acc = x1_ref[...] + jnp.dot(sel01, og, preferred_element_type=F32)
    y = _rms(acc, gf_ref[...])

    @pl.when(i < n_ptiles)
    def _():
        yp_ref[...] = y

    @pl.when(i >= n_ptiles)
    def _():
        ys_ref[...] = y


def _combine(ctab, outs, x1, meta, g_final, tp, ts):
    n_pt, n_st = tp // ROUTE_T, ts // ROUTE_T
    n = n_pt + n_st
    nch = GROUP_R // SUBLANES
    return pl.pallas_call(
        functools.partial(_combine_kernel, n_pt),
        grid=(n,),
        in_specs=[pl.BlockSpec((1, 1, nch), lambda i: (i, 0, 0), memory_space=pltpu.SMEM),
                  pl.BlockSpec((1, 1, nch), lambda i: (jnp.minimum(i + 1, n - 1), 0, 0), memory_space=pltpu.SMEM),
                  pl.BlockSpec(memory_space=pl.ANY),
                  pl.BlockSpec((ROUTE_T, D_MODEL), lambda i: (i, 0)),
                  pl.BlockSpec((ROUTE_T, 2 * TOP_K), lambda i: (i, 0)),
                  pl.BlockSpec((1, D_MODEL), lambda i: (0, 0))],
        out_specs=(pl.BlockSpec((ROUTE_T, D_MODEL), lambda i: (jnp.minimum(i, n_pt - 1), 0)),
                   pl.BlockSpec((ROUTE_T, D_MODEL), lambda i: (jnp.maximum(i - n_pt, 0), 0))),
        out_shape=(jax.ShapeDtypeStruct((tp, D_MODEL), F32), jax.ShapeDtypeStruct((ts, D_MODEL), F32)),
        scratch_shapes=[pltpu.VMEM((2, GROUP_R, D_MODEL), F32), pltpu.SemaphoreType.DMA((2,))],
        compiler_params=_cparams(("arbitrary",), VMEM_LIMIT),
        name="moe_combine",
    )(ctab, ctab, outs, x1, meta, g_final)


def kernel(x_prompt, x_sample, cache_swa_k, cache_swa_v, state_mlstm_c, state_mlstm_n, state_mlstm_m,
           g_mix, w_in, b_igate, b_fgate, g_head, attn_sinks, w_out, g_ffn, w_router, b_router,
           w_gate_up, b_gate_up, w_down, b_down, g_final):
    assert w_in.shape[0] == 1, "single-layer problem"
    B, S, _ = x_prompt.shape
    Bd, Tn, _ = x_sample.shape
    tp, ts = B * S, Bd * Tn
    t_all = tp + ts
    xp = x_prompt.reshape(tp, D_MODEL)
    xs = x_sample.reshape(ts, D_MODEL)

    w = w_in[0]
    o = np.cumsum([0, DM, DM, DM, DM, N_HEADS, N_HEADS, DM, DKV, DKV])
    col = lambda a: w[:, int(o[a]):int(o[a + 1])]
    wgates = jnp.concatenate([col(4), col(5)], axis=1)
    w1 = jnp.concatenate([col(1), col(6), col(7), col(8), jnp.pad(wgates, ((0, 0), (0, LANES - 2 * N_HEADS)))],
                         axis=1).astype(BF16)
    wkt = jnp.concatenate([col(0), col(2), col(3), wgates], axis=1).T.astype(BF16)
    bg = jnp.concatenate([b_igate[0], b_fgate[0]]).astype(F32)
    bcol = jnp.pad(bg, (0, LANES - 2 * N_HEADS)).reshape(1, LANES)
    brow = bg.reshape(2 * N_HEADS, 1)

    km, qa, ka, va, qmt, vmt, omt, gcol, grow = _inproj(xp, xs, g_mix[0].reshape(1, D_MODEL), w1, wkt, bcol, brow)

    gh = g_head[0].astype(F32)
    sinks = attn_sinks[0].astype(F32)

    hm_p, ctp, m_p = _mlstm_prompt(km, qmt, vmt, omt, gcol, grow,
                                   jnp.broadcast_to(gh.reshape(DM, 1), (DM, LANES)), B, S)
    ha_p = _swa_prompt(sinks, qa, ka, va, B, S)

    hm_s, c_s, n_s, mt_s = _mlstm_sample(km, qmt, vmt, omt, gcol, grow, state_mlstm_c[0], state_mlstm_n[0],
                                         state_mlstm_m[0], gh.reshape(1, DM), tp, Tn)
    n_s = n_s.reshape(Bd, N_HEADS, HEAD_DIM)
    m_s = mt_s[:, :, :LANES // Tn].transpose(0, 2, 1).reshape(Bd, N_HEADS)
    ha_s, k_s, v_s = _swa_decode(sinks, qa, ka, va, cache_swa_k[0].reshape(Bd, WINDOW, DKV),
                                 cache_swa_v[0].reshape(Bd, WINDOW, DKV), tp, Tn)

    wo = w_out[0].astype(BF16)
    x1, xg, meta, cnt = _outproj_router(
        xp, xs, hm_p, ha_p, hm_s, ha_s, wo[:DM], wo[DM:], g_ffn[0].reshape(1, D_MODEL),
        w_router[0].T, b_router[0].reshape(N_EXPERTS, 1))

    i32 = jnp.int32
    n_tiles = t_all // ROUTE_T
    max_rows = t_all * TOP_K + n_tiles * N_EXPERTS * (SUBLANES - 1) + N_EXPERTS * (MOE_BM - 1)
    n_blocks = -(-max_rows // MOE_BM)
    cpad = (cnt[:, :, 0].astype(i32) + (SUBLANES - 1)) // SUBLANES * SUBLANES
    lstart = jnp.cumsum(cpad, axis=1) - cpad
    goff = jnp.cumsum(cpad, axis=0) - cpad
    padded = (jnp.sum(cpad, axis=0) + MOE_BM - 1) // MOE_BM * MOE_BM
    pad_end = jnp.cumsum(padded)
    seg_begin = (pad_end - padded)[None, :] + goff
    n_active = (pad_end[-1] // MOE_BM).astype(i32)
    blk = jnp.minimum(jnp.arange(n_blocks, dtype=i32), n_active - 1)
    block_exp = jnp.minimum(jnp.sum((pad_end[None, :] <= (blk * MOE_BM)[:, None]).astype(i32), axis=1),
                            N_EXPERTS - 1)
    e_ids = jnp.arange(N_EXPERTS, dtype=i32)
    nonempty = padded > 0
    nxt_e = jnp.min(jnp.where((e_ids[None, :] > e_ids[:, None]) & nonempty[None, :], e_ids[None, :], N_EXPERTS),
                    axis=1)
    nxt_e = jnp.where(nxt_e == N_EXPERTS, -1, nxt_e)
    ord_e = jnp.cumsum(nonempty.astype(i32)) - 1
    be_hot = block_exp[:, None] == e_ids[None, :]
    next_exp = jnp.sum(jnp.where(be_hot, nxt_e[None, :], 0), axis=1).astype(i32)
    wslot = (jnp.sum(jnp.where(be_hot, ord_e[None, :], 0), axis=1) % 2).astype(i32)

    seg_src = jnp.arange(n_tiles, dtype=i32)[:, None] * GROUP_R + lstart
    sb, sl, ss = seg_begin.reshape(-1), cpad.reshape(-1), seg_src.reshape(-1)
    rc = jnp.arange(n_blocks * MOE_BM // SUBLANES, dtype=i32)[:, None] * SUBLANES
    inseg = (sb[None, :] <= rc) & (rc < (sb + sl)[None, :])
    ctab_e = jnp.where(jnp.any(inseg, axis=1), jnp.sum(jnp.where(inseg, (ss - sb)[None, :] + rc, 0), axis=1),
                       GROUP_R - SUBLANES)
    lr = jnp.arange(GROUP_R // SUBLANES, dtype=i32)[None, :, None] * SUBLANES
    inl = (lstart[:, None, :] <= lr) & (lr < (lstart + cpad)[:, None, :])
    ctab_c = jnp.sum(jnp.where(inl, (seg_begin - lstart)[:, None, :] + lr, 0), axis=2)

    outs = _expert_ffn(block_exp, n_active.reshape(1), wslot, next_exp,
                       ctab_e.astype(i32).reshape(n_blocks, 1, MOE_BM // SUBLANES), xg, w_gate_up[0],
                       b_gate_up[0].reshape(N_EXPERTS, 1, 2 * D_FF), w_down[0],
                       b_down[0].reshape(N_EXPERTS, 1, D_MODEL))
    y_p, y_s = _combine(ctab_c.astype(i32).reshape(n_tiles, 1, GROUP_R // SUBLANES), outs, x1, meta,
                        g_final.reshape(1, D_MODEL), tp, ts)

    kv_tail = lambda a: jnp.concatenate([a[(b + 1) * S - WINDOW:(b + 1) * S] for b in range(B)], axis=0).reshape(
        1, B, WINDOW, N_KV, HEAD_DIM)
    c_e = ctp[:, :, :HEAD_DIM, :HEAD_DIM]
    c_o = ctp[:, :, HEAD_DIM:LANES, HEAD_DIM:]
    c_p = jnp.swapaxes(jnp.stack([c_e, c_o], axis=2), -1, -2).reshape(B, N_HEADS, HEAD_DIM, HEAD_DIM)
    n_p = (ctp[:, :, LANES:LANES + 2, :HEAD_DIM] + ctp[:, :, LANES:LANES + 2, HEAD_DIM:]).reshape(B, N_HEADS, HEAD_DIM)
    return (y_p.reshape(B, S, D_MODEL), y_s.reshape(Bd, Tn, D_MODEL),
            kv_tail(ka), kv_tail(va), c_p[None], n_p[None], m_p[:, :, 0][None],
            k_s.reshape(Bd, WINDOW, N_KV, HEAD_DIM)[None], v_s.reshape(Bd, WINDOW, N_KV, HEAD_DIM)[None],
            c_s[None], n_s[None], m_s[None])
```

```python
import functools

import jax
import jax.numpy as jnp
import numpy as np
from jax import lax
from jax.experimental import pallas as pl
from jax.experimental.pallas import tpu as pltpu

F32 = jnp.float32
BF16 = jnp.bfloat16
HIGHEST = lax.Precision.HIGHEST

D_MODEL = 1024
HEAD_DIM = 64
N_HEADS = 8
N_PAIRS = N_HEADS // 2
N_KV = 2
GROUP = N_HEADS // N_KV
WINDOW = 128
N_EXPERTS = 32
TOP_K = 4
D_FF = 1024
SWIGLU_LIMIT = 7.0
SWIGLU_ALPHA = 1.702
RMS_EPS = 1e-5
DM = N_HEADS * HEAD_DIM
DKV = N_KV * HEAD_DIM
NEG = -1e30

LANES = 128
SUBLANES = 8
VMEM_LIMIT = 56 * 1024 * 1024

TM = 512
ROUTE_T = 512
MLSTM_TL = 512
MLSTM_L = 128
SWA_BLOCKS = 8
MOE_BM = 256
X_SLOTS = 3
GROUP_R = -(-(TOP_K * ROUTE_T + N_EXPERTS * (SUBLANES - 1) + SUBLANES) // LANES) * LANES


def _cparams(sem, vmem=None):
    return pltpu.CompilerParams(dimension_semantics=sem, vmem_limit_bytes=vmem)


def _rms(x, g):
    return x * lax.rsqrt(jnp.mean(x * x, axis=-1, keepdims=True) + RMS_EPS) * g


def _log_sigmoid(z):
    return jnp.minimum(z, 0.0) - jnp.log(1.0 + jnp.exp(-jnp.abs(z)))


def _sigmoid(z):
    return 1.0 / (1.0 + jnp.exp(-z))


def _inproj_kernel(n_ptiles, xp_ref, xs_ref, g_ref, w1_ref, wt_ref, bcol_ref, brow_ref,
                   km_ref, qa_ref, ka_ref, va_ref, qmt_ref, vmt_ref, omt_ref, gcol_ref, grow_ref):
    i = pl.program_id(0)
    x = jnp.where(i < n_ptiles, xp_ref[...], xs_ref[...])
    h = _rms(x, g_ref[...]).astype(BF16)
    main = jnp.dot(h, w1_ref[...], preferred_element_type=F32)
    km_ref[...] = main[:, 0:DM] * (HEAD_DIM ** -0.5)
    qa_ref[...] = main[:, DM:2 * DM]
    ka_ref[...] = main[:, 2 * DM:2 * DM + DKV]
    va_ref[...] = main[:, 2 * DM + DKV:2 * DM + 2 * DKV]
    t = lax.dot_general(wt_ref[...], h, (((1,), (1,)), ((), ())), preferred_element_type=F32)
    qmt_ref[...] = t[0:DM]
    vmt_ref[...] = t[DM:2 * DM]
    omt_ref[...] = t[2 * DM:3 * DM]
    zc = main[:, 2 * DM + 2 * DKV:] + bcol_ref[...]
    lane = lax.broadcasted_iota(jnp.int32, zc.shape, 1)
    gcol_ref[...] = jnp.where(lane < N_HEADS, zc, _log_sigmoid(zc))
    zr = t[3 * DM:] + brow_ref[...]
    row = lax.broadcasted_iota(jnp.int32, zr.shape, 0)
    grow_ref[...] = jnp.where(row < N_HEADS, zr, _log_sigmoid(zr))


def _inproj(xp, xs, g_mix, w1, wkt, bcol, brow):
    tp, ts = xp.shape[0], xs.shape[0]
    n_pt, n_st = tp // TM, ts // TM
    t_all = tp + ts
    tok = lambda w: pl.BlockSpec((TM, w), lambda i: (i, 0))
    tr = lambda r: pl.BlockSpec((r, TM), lambda i: (0, i))
    full = lambda a: pl.BlockSpec(a.shape, lambda i: (0,) * a.ndim)
    out_shape = (
        jax.ShapeDtypeStruct((t_all, DM), F32), jax.ShapeDtypeStruct((t_all, DM), F32),
        jax.ShapeDtypeStruct((t_all, DKV), F32), jax.ShapeDtypeStruct((t_all, DKV), F32),
        jax.ShapeDtypeStruct((DM, t_all), F32), jax.ShapeDtypeStruct((DM, t_all), F32),
        jax.ShapeDtypeStruct((DM, t_all), F32),
        jax.ShapeDtypeStruct((t_all, LANES), F32), jax.ShapeDtypeStruct((2 * N_HEADS, t_all), F32),
    )
    return pl.pallas_call(
        functools.partial(_inproj_kernel, n_pt),
        grid=(n_pt + n_st,),
        in_specs=[
            pl.BlockSpec((TM, D_MODEL), lambda i: (jnp.minimum(i, n_pt - 1), 0)),
            pl.BlockSpec((TM, D_MODEL), lambda i: (jnp.maximum(i - n_pt, 0), 0)),
            full(g_mix), full(w1), full(wkt), full(bcol), full(brow),
        ],
        out_specs=(tok(DM), tok(DM), tok(DKV), tok(DKV), tr(DM), tr(DM), tr(DM),
                   tok(LANES), tr(2 * N_HEADS)),
        out_shape=out_shape,
        compiler_params=_cparams(("arbitrary",), VMEM_LIMIT),
        name="inproj",
    )(xp, xs, g_mix, w1, wkt, bcol, brow)


CT_ROWS = LANES + 2 * SUBLANES


def _cumsum_rows(x, n):
    row = lax.broadcasted_iota(jnp.int32, x.shape, 0)
    sh = 1
    while sh < n:
        x = x + jnp.where(row >= sh, pltpu.roll(x, sh, axis=0), 0.0)
        sh *= 2
    return x


def _mlstm_prompt_kernel(km_ref, qmt_ref, vmt_ref, omt_ref, gcol_ref, grow_ref, ghr_ref,
                         hm_ref, ct_ref, m_ref, ct_s, m_s):
    j = pl.program_id(1)
    L = MLSTM_L
    assert L == LANES

    @pl.when(j == 0)
    def _():
        ct_s[...] = jnp.zeros_like(ct_s)
        m_s[...] = jnp.zeros_like(m_s)

    si = lax.broadcasted_iota(jnp.int32, (L, L), 0)
    ti = lax.broadcasted_iota(jnp.int32, (L, L), 1)
    causal_t = si <= ti
    upper = jnp.where(causal_t, 1.0, 0.0)
    rows_c = lax.broadcasted_iota(jnp.int32, (CT_ROWS, 1), 0)
    rmask_e = (rows_c < HEAD_DIM) | (rows_c == LANES)
    rmask_o = ((rows_c >= HEAD_DIM) & (rows_c < LANES)) | (rows_c == LANES + 1)
    rows_e = lax.broadcasted_iota(jnp.int32, (LANES, 1), 0) < HEAD_DIM
    cols_e = lax.broadcasted_iota(jnp.int32, (1, LANES), 1) < HEAD_DIM
    bdt_mask = (rmask_e & cols_e) | (rmask_o & (~cols_e))
    ones_rows = jnp.where(lax.broadcasted_iota(jnp.int32, (CT_ROWS - LANES, L), 0) < 2, 1.0, 0.0)

    for c in range(MLSTM_TL // L):
        sl = slice(c * L, (c + 1) * L)
        grow = grow_ref[:, sl]
        i_row = grow[0:N_HEADS]
        b_row = jnp.dot(grow[N_HEADS:2 * N_HEADS], upper, precision=HIGHEST,
                        preferred_element_type=F32)
        gc = gcol_ref[sl, :]
        bc_all = _cumsum_rows(gc, L)
        for p in range(N_PAIRS):
            ls = slice(p * LANES, (p + 1) * LANES)
            k2 = km_ref[sl, ls]
            qt2 = qmt_ref[ls, sl]
            qt_e = jnp.where(rows_e, qt2, 0.0).astype(BF16)
            qt_o = jnp.where(rows_e, 0.0, qt2).astype(BF16)
            st2 = jnp.dot(k2.astype(BF16), jnp.concatenate([qt_e, qt_o], axis=1),
                          preferred_element_type=F32)
            ct = ct_s[p]
            rqt = jnp.dot(ct.astype(BF16), qt2.astype(BF16), preferred_element_type=F32)
            pts, mts, inters, wreps, decays, mnews = [], [], [], [], [], []
            for hh in range(2):
                h = 2 * p + hh
                cvec = jnp.broadcast_to(gc[:, h:h + 1] - bc_all[:, N_HEADS + h:N_HEADS + h + 1], (L, L))
                brow = b_row[h:h + 1, :]
                logdt = jnp.where(causal_t, cvec + brow, NEG)
                m_prev = m_s[h:h + 1, 0:1]
                m_inter = m_prev + brow
                m_t = jnp.maximum(m_inter, jnp.max(logdt, axis=0, keepdims=True))
                pts.append((st2[:, hh * L:(hh + 1) * L] * jnp.exp(logdt - m_t)).astype(BF16))
                mts.append(m_t)
                inters.append(jnp.exp(m_inter - m_t))
                m_new = m_t[:, L - 1:L]
                b_last = brow[:, L - 1:L]
                decays.append(jnp.exp(m_prev + b_last - m_new))
                wreps.append(jnp.exp(cvec + (b_last - m_new)))
                mnews.append(m_new)
            vext = jnp.concatenate([vmt_ref[ls, sl], ones_rows], axis=0)
            lhs = jnp.concatenate([jnp.where(rmask_e, vext, 0.0), jnp.where(rmask_o, vext, 0.0)],
                                  axis=1).astype(BF16)
            rt = jnp.dot(lhs, jnp.concatenate(pts, axis=0), preferred_element_type=F32)
            ndt = rt + jnp.where(rmask_e, inters[0], inters[1]) * rqt
            den = jnp.where(rows_e, ndt[LANES:LANES + 1], ndt[LANES + 1:LANES + 2])
            mt2 = jnp.where(rows_e, mts[0], mts[1])
            hvt = ndt[0:LANES] / jnp.maximum(jnp.abs(den), jnp.exp(-mt2))
            sq = hvt * hvt
            ms = jnp.where(rows_e, jnp.sum(sq[0:HEAD_DIM], axis=0, keepdims=True),
                           jnp.sum(sq[HEAD_DIM:LANES], axis=0, keepdims=True)) * (1.0 / HEAD_DIM)
            yt = hvt * lax.rsqrt(ms + RMS_EPS) * ghr_ref[ls, :] * _sigmoid(omt_ref[ls, sl])
            hm_ref[sl, ls] = jnp.transpose(yt).astype(hm_ref.dtype)
            kw = (k2 * jnp.where(cols_e, wreps[0], wreps[1])).astype(BF16)
            upd = jnp.dot(vext.astype(BF16), kw, preferred_element_type=F32)
            ct_s[p] = jnp.where(rmask_e, decays[0], decays[1]) * ct + jnp.where(bdt_mask, upd, 0.0)
            for hh in range(2):
                h = 2 * p + hh
                m_s[h:h + 1, :] = jnp.broadcast_to(mnews[hh], (1, LANES))

    @pl.when(j == pl.num_programs(1) - 1)
    def _():
        ct_ref[0] = ct_s[...]
        m_ref[0] = m_s[...]


def _mlstm_prompt(km, qmt, vmt, omt, gcol, grow, ghr, batch, seq):
    nt = seq // MLSTM_TL
    tokb = lambda w: pl.BlockSpec((MLSTM_TL, w), lambda b, j: (b * nt + j, 0))
    rowb = lambda r: pl.BlockSpec((r, MLSTM_TL), lambda b, j: (0, b * nt + j))
    return pl.pallas_call(
        _mlstm_prompt_kernel,
        grid=(batch, nt),
        in_specs=[tokb(DM), rowb(DM), rowb(DM), rowb(DM), tokb(LANES), rowb(2 * N_HEADS),
                  pl.BlockSpec((DM, LANES), lambda b, j: (0, 0))],
        out_specs=(tokb(DM),
                   pl.BlockSpec((1, N_PAIRS, CT_ROWS, LANES), lambda b, j: (b, 0, 0, 0)),
                   pl.BlockSpec((1, N_HEADS, LANES), lambda b, j: (b, 0, 0))),
        out_shape=(jax.ShapeDtypeStruct((batch * seq, DM), BF16),
                   jax.ShapeDtypeStruct((batch, N_PAIRS, CT_ROWS, LANES), F32),
                   jax.ShapeDtypeStruct((batch, N_HEADS, LANES), F32)),
        scratch_shapes=[pltpu.VMEM((N_PAIRS, CT_ROWS, LANES), F32), pltpu.VMEM((N_HEADS, LANES), F32)],
        compiler_params=_cparams(("arbitrary", "arbitrary"), VMEM_LIMIT),
        name="mlstm_prompt",
    )(km, qmt, vmt, omt, gcol, grow, ghr)


def _mlstm_sample_kernel(n_tok, km_ref, qmt_ref, vmt_ref, omt_ref, gcol_ref, grow_ref, c0_ref, n0_ref, m0_ref,
                         m0t_ref, gh_ref, hm_ref, c_ref, n_ref, mt_ref):
    L = LANES
    NB = L // n_tok
    ti = lax.broadcasted_iota(jnp.int32, (L, L), 0)
    si = lax.broadcasted_iota(jnp.int32, (L, L), 1)
    same = (ti // n_tok) == (si // n_tok)
    causal = same & (ti >= si)
    useg = jnp.where(same & (ti <= si), 1.0, 0.0)
    slast = jnp.where(same & (ti % n_tok == n_tok - 1), 1.0, 0.0)
    expand = jnp.where(ti // n_tok == si, 1.0, 0.0)
    expand_t = jnp.where(ti == si // n_tok, 1.0, 0.0)
    pick = jnp.where((ti // n_tok == si) & (ti % n_tok == n_tok - 1), 1.0, 0.0)
    hdot = lambda a, b: jnp.dot(a, b, precision=HIGHEST, preferred_element_type=F32)

    lane128 = lax.broadcasted_iota(jnp.int32, (L, LANES), 1)
    even128 = lane128 < HEAD_DIM
    lane256 = lax.broadcasted_iota(jnp.int32, (1, 2 * LANES), 1)
    cols_e = (lane256 < HEAD_DIM) | (lane256 == LANES)
    cols_o = ((lane256 >= HEAD_DIM) & (lane256 < LANES)) | (lane256 == LANES + 1)
    rows_e = lax.broadcasted_iota(jnp.int32, (LANES, 1), 0) < HEAD_DIM
    ones_cols = jnp.where(lane128 < 2, 1.0, 0.0)
    bo_r = lax.broadcasted_iota(jnp.int32, (LANES, LANES), 0) // HEAD_DIM
    bo_c = lax.broadcasted_iota(jnp.int32, (LANES, LANES), 1) // HEAD_DIM
    block_ones = jnp.where(bo_r == bo_c, 1.0, 0.0)
    W = NB * LANES
    rb = lax.broadcasted_iota(jnp.int32, (L, W), 0)
    cb = lax.broadcasted_iota(jnp.int32, (L, W), 1)
    own_block = (rb // n_tok) == (cb // LANES)
    bd_tiled = (rb // HEAD_DIM) == ((cb % LANES) // HEAD_DIM)

    grow = grow_ref[...]
    i_row = grow[0:N_HEADS]
    b_row = hdot(grow[N_HEADS:2 * N_HEADS], useg)
    b_last = hdot(b_row, slast)
    a_row = b_last - b_row + i_row
    pos = lax.broadcasted_iota(jnp.int32, a_row.shape, 1) % n_tok
    pm = a_row
    sh = 1
    while sh < n_tok:
        pm = jnp.where(pos >= sh, jnp.maximum(pm, pltpu.roll(pm, sh, axis=1)), pm)
        sh *= 2
    m_carry = hdot(jnp.concatenate([m0t_ref[0], b_row], axis=1), jnp.concatenate([expand_t, slast], axis=0))
    m_new_row = jnp.maximum(m_carry, hdot(pm, slast))
    decay_row = jnp.exp(m_carry - m_new_row)
    w_row = jnp.exp(a_row - m_new_row)
    mt_ref[0] = hdot(m_new_row, pick)
    decay_bh = hdot(decay_row, pick)
    decay_hb = jnp.transpose(decay_bh)[0:NB]

    bc_all = gcol_ref[...]
    rowpos = lax.broadcasted_iota(jnp.int32, bc_all.shape, 0) % n_tok
    sh = 1
    while sh < n_tok:
        bc_all = bc_all + jnp.where(rowpos >= sh, pltpu.roll(bc_all, sh, axis=0), 0.0)
        sh *= 2
    pad_rows = lambda a: jnp.concatenate([a, jnp.zeros((L - NB, a.shape[1]), F32)], axis=0)
    m0_col = hdot(expand, pad_rows(m0_ref[...]))
    rowv_all = i_row - b_row

    for p in range(N_PAIRS):
        ls = slice(p * LANES, (p + 1) * LANES)
        q2f = jnp.transpose(qmt_ref[ls, :])
        q2 = q2f.astype(BF16)
        kt2 = jnp.transpose(km_ref[:, ls])
        v2 = jnp.transpose(vmt_ref[ls, :])
        vext = jnp.concatenate([v2, ones_cols], axis=1)
        kt_e = jnp.where(rows_e, kt2, 0.0).astype(BF16)
        kt_o = jnp.where(rows_e, 0.0, kt2).astype(BF16)
        s2 = jnp.dot(q2, jnp.concatenate([kt_e, kt_o], axis=1), preferred_element_type=F32)
        ps, mts, inters = [], [], []
        for hh in range(2):
            h = 2 * p + hh
            bcol = bc_all[:, N_HEADS + h:N_HEADS + h + 1]
            logd = jnp.where(causal, bcol + rowv_all[h:h + 1, :], NEG)
            m_inter = m0_col[:, h:h + 1] + bcol
            m_t = jnp.maximum(m_inter, jnp.max(logd, axis=1, keepdims=True))
            ps.append((s2[:, hh * L:(hh + 1) * L] * jnp.exp(logd - m_t)).astype(BF16))
            mts.append(m_t)
            inters.append(jnp.exp(m_inter - m_t))
        vstack = jnp.concatenate([jnp.where(cols_e, vext, 0.0), jnp.where(cols_o, vext, 0.0)],
                                 axis=0).astype(BF16)
        r = jnp.dot(jnp.concatenate(ps, axis=1), vstack, preferred_element_type=F32)
        zero = jnp.zeros((HEAD_DIM, HEAD_DIM), F32)
        cstack = jnp.concatenate(
            [jnp.concatenate([jnp.concatenate([c0_ref[b, 2 * p], zero], axis=1),
                              jnp.concatenate([zero, c0_ref[b, 2 * p + 1]], axis=1)], axis=0)
             for b in range(NB)], axis=1)
        rq_all = jnp.where(own_block, jnp.dot(q2, cstack.astype(BF16), preferred_element_type=F32), 0.0)
        rq = rq_all[:, 0:LANES]
        for b in range(1, NB):
            rq = rq + rq_all[:, b * LANES:(b + 1) * LANES]
        n_rows = hdot(expand, pad_rows(n0_ref[:, ls]))
        qn = hdot(q2f * n_rows, block_ones)
        inter2 = jnp.where(even128, inters[0], inters[1])
        num = r[:, 0:LANES] + inter2 * rq
        den = jnp.where(even128, r[:, LANES:LANES + 1], r[:, LANES + 1:LANES + 2]) + inter2 * qn
        mt2 = jnp.where(even128, mts[0], mts[1])
        hv = num / jnp.maximum(jnp.abs(den), jnp.exp(-mt2))
        ms = hdot(hv * hv, block_ones) * (1.0 / HEAD_DIM)
        y = hv * lax.rsqrt(ms + RMS_EPS) * gh_ref[:, ls] * _sigmoid(jnp.transpose(omt_ref[ls, :]))
        hm_ref[:, ls] = y.astype(hm_ref.dtype)
        kw = kt2 * jnp.where(rows_e, w_row[2 * p:2 * p + 1, :], w_row[2 * p + 1:2 * p + 2, :])
        vbd = jnp.where(own_block, jnp.concatenate([v2] * NB, axis=1), 0.0).astype(BF16)
        upd = jnp.dot(kw.astype(BF16), vbd, preferred_element_type=F32)
        upd = jnp.where(bd_tiled, upd, 0.0)
        for b in range(NB):
            bs = slice(b * LANES, (b + 1) * LANES)
            dec_b = jnp.where(rows_e, decay_hb[b:b + 1, 2 * p:2 * p + 1], decay_hb[b:b + 1, 2 * p + 1:2 * p + 2])
            cnew = dec_b * cstack[:, bs] + upd[:, bs]
            c_ref[b, 2 * p] = cnew[0:HEAD_DIM, 0:HEAD_DIM]
            c_ref[b, 2 * p + 1] = cnew[HEAD_DIM:LANES, HEAD_DIM:LANES]
        nsum = jnp.transpose(hdot(kw, expand))[0:NB]
        dec_n = jnp.where(lax.broadcasted_iota(jnp.int32, (NB, LANES), 1) < HEAD_DIM,
                          decay_hb[:, 2 * p:2 * p + 1], decay_hb[:, 2 * p + 1:2 * p + 2])
        n_ref[:, ls] = dec_n * n0_ref[:, ls] + nsum


def _mlstm_sample(km, qmt, vmt, omt, gcol, grow, c0, n0, m0, gh, row0, n_tok):
    nb = c0.shape[0]
    g_nb = LANES // n_tok
    n_g = nb // g_nb
    blk0 = row0 // LANES
    tokb = lambda w: pl.BlockSpec((LANES, w), lambda i: (blk0 + i, 0))
    rowb = lambda r: pl.BlockSpec((r, LANES), lambda i: (0, blk0 + i))
    m0t = jnp.pad(m0.reshape(n_g, g_nb, N_HEADS).transpose(0, 2, 1),
                  ((0, 0), (0, 0), (0, LANES - g_nb)))
    return pl.pallas_call(
        functools.partial(_mlstm_sample_kernel, n_tok),
        grid=(n_g,),
        in_specs=[tokb(DM), rowb(DM), rowb(DM), rowb(DM), tokb(LANES), rowb(2 * N_HEADS),
                  pl.BlockSpec((g_nb, N_HEADS, HEAD_DIM, HEAD_DIM), lambda i: (i, 0, 0, 0)),
                  pl.BlockSpec((g_nb, DM), lambda i: (i, 0)),
                  pl.BlockSpec((g_nb, N_HEADS), lambda i: (i, 0)),
                  pl.BlockSpec((1, N_HEADS, LANES), lambda i: (i, 0, 0)),
                  pl.BlockSpec((1, DM), lambda i: (0, 0))],
        out_specs=(pl.BlockSpec((LANES, DM), lambda i: (i, 0)),
                   pl.BlockSpec((g_nb, N_HEADS, HEAD_DIM, HEAD_DIM), lambda i: (i, 0, 0, 0)),
                   pl.BlockSpec((g_nb, DM), lambda i: (i, 0)),
                   pl.BlockSpec((1, N_HEADS, LANES), lambda i: (i, 0, 0))),
        out_shape=(jax.ShapeDtypeStruct((nb * n_tok, DM), BF16), jax.ShapeDtypeStruct(c0.shape, F32),
                   jax.ShapeDtypeStruct((nb, DM), F32), jax.ShapeDtypeStruct((n_g, N_HEADS, LANES), F32)),
        compiler_params=_cparams(("arbitrary",), VMEM_LIMIT),
        name="mlstm_sample",
    )(km, qmt, vmt, omt, gcol, grow, c0, n0.reshape(nb, DM), m0, m0t, gh)


def _alibi_slope(h):
    return float(np.float32(2.0 ** (-8.0 * (h + 1) / N_HEADS)))


def _dup_halves(x):
    lane = lax.broadcasted_iota(jnp.int32, x.shape, 1)
    xr = pltpu.roll(x, HEAD_DIM, axis=1)
    lo = lane < HEAD_DIM
    return jnp.where(lo, x, xr), jnp.where(lo, xr, x)


def _stack_group_queries(q, g):
    lane = lax.broadcasted_iota(jnp.int32, (q.shape[0], LANES), 1)
    parts = []
    for hh in range(GROUP):
        h = GROUP * g + hh
        blk = q[:, (h // 2) * LANES:(h // 2 + 1) * LANES]
        keep = (lane < HEAD_DIM) if h % 2 == 0 else (lane >= HEAD_DIM)
        parts.append(jnp.where(keep, blk, 0.0))
    return jnp.concatenate(parts, axis=0).astype(BF16)


def _swa_prompt_kernel(sink_ref, q_ref, kp_ref, ko_ref, vp_ref, vo_ref, o_ref):
    j = pl.program_id(1)
    R = WINDOW
    qi = lax.broadcasted_iota(jnp.int32, (R, R), 0)
    kj = lax.broadcasted_iota(jnp.int32, (R, R), 1)
    own = kj <= qi
    distf = jnp.where(own, qi - kj, qi - kj + R).astype(F32)
    lane = lax.broadcasted_iota(jnp.int32, (R, LANES), 1)
    for u in range(SWA_BLOCKS):
        rs = slice(u * R, (u + 1) * R)
        q = q_ref[rs, :]
        k_prev = kp_ref[...] if u == 0 else ko_ref[(u - 1) * R:u * R, :]
        v_prev = vp_ref[...] if u == 0 else vo_ref[(u - 1) * R:u * R, :]
        kd = _dup_halves(jnp.concatenate([k_prev, ko_ref[rs, :]], axis=0))
        vd = _dup_halves(jnp.concatenate([v_prev, vo_ref[rs, :]], axis=0))
        valid = (own | (j > 0)) if u == 0 else None
        outs = []
        for g in range(N_KV):
            qs = _stack_group_queries(q, g)
            s = lax.dot_general(qs, kd[g].astype(BF16), (((1,), (1,)), ((), ())),
                                preferred_element_type=F32) * (HEAD_DIM ** -0.5)
            ps = []
            for hh in range(GROUP):
                h = GROUP * g + hh
                sink = sink_ref[h]
                rows = slice(hh * R, (hh + 1) * R)
                sh = jnp.where(own, s[rows, R:], s[rows, :R]) - _alibi_slope(h) * distf
                if valid is not None:
                    sh = jnp.where(valid, sh, NEG)
                mx = jnp.maximum(jnp.max(sh, axis=1, keepdims=True), sink)
                p = jnp.exp(sh - mx)
                p = p / (jnp.sum(p, axis=1, keepdims=True) + jnp.exp(sink - mx))
                ps.append(jnp.concatenate([jnp.where(own, 0.0, p), jnp.where(own, p, 0.0)], axis=1).astype(BF16))
            o = jnp.dot(jnp.concatenate(ps, axis=0), vd[g].astype(BF16), preferred_element_type=F32)
            for pp in range(GROUP // 2):
                outs.append(jnp.where(lane < HEAD_DIM, o[(2 * pp) * R:(2 * pp + 1) * R],
                                      o[(2 * pp + 1) * R:(2 * pp + 2) * R]))
        o_ref[rs, :] = jnp.concatenate(outs, axis=1).astype(o_ref.dtype)


def _swa_prompt(sinks, qa, ka, va, batch, seq):
    nb = seq // (WINDOW * SWA_BLOCKS)
    own = lambda w: pl.BlockSpec((WINDOW * SWA_BLOCKS, w), lambda b, j: (b * nb + j, 0))
    prev = lambda w: pl.BlockSpec(
        (WINDOW, w), lambda b, j: (b * nb * SWA_BLOCKS + jnp.maximum(j * SWA_BLOCKS - 1, 0), 0))
    return pl.pallas_call(
        _swa_prompt_kernel,
        grid=(batch, nb),
        in_specs=[pl.BlockSpec(memory_space=pltpu.SMEM), own(DM), prev(DKV), own(DKV), prev(DKV), own(DKV)],
        out_specs=own(DM),
        out_shape=jax.ShapeDtypeStruct((batch * seq, DM), BF16),
        compiler_params=_cparams(("arbitrary", "arbitrary"), VMEM_LIMIT),
        name="swa_prompt",
    )(sinks, qa, ka, ka, va, va)


def _swa_decode_kernel(n_tok, sink_ref, q_ref, kn_ref, vn_ref, kc_ref, vc_ref, o_ref, ko_ref, vo_ref):
    W = WINDOW
    L = LANES
    NB = L // n_tok
    for b in range(NB):
        ko_ref[b, 0:W - n_tok, :] = kc_ref[b, n_tok:W, :]
        ko_ref[b, W - n_tok:W, :] = kn_ref[b * n_tok:(b + 1) * n_tok, :]
        vo_ref[b, 0:W - n_tok, :] = vc_ref[b, n_tok:W, :]
        vo_ref[b, W - n_tok:W, :] = vn_ref[b * n_tok:(b + 1) * n_tok, :]
    q = q_ref[...]
    knd, vnd = _dup_halves(kn_ref[...]), _dup_halves(vn_ref[...])
    kcd = _dup_halves(kc_ref[...].reshape(NB * W, DKV))
    vcd = _dup_halves(vc_ref[...].reshape(NB * W, DKV))
    ri = lax.broadcasted_iota(jnp.int32, (L, L), 0)
    ci = lax.broadcasted_iota(jnp.int32, (L, L), 1)
    t_q = ri % n_tok
    dist_n = t_q - ci % n_tok
    valid_n = ((ri // n_tok) == (ci // n_tok)) & (dist_n >= 0)
    dist_c = t_q + W - ci
    valid_c = dist_c < WINDOW
    dnf = dist_n.astype(F32)
    dcf = dist_c.astype(F32)
    own1 = (lax.broadcasted_iota(jnp.int32, (L, NB * W), 0) // n_tok
            == lax.broadcasted_iota(jnp.int32, (L, NB * W), 1) // W)
    own4 = ((lax.broadcasted_iota(jnp.int32, (GROUP * L, NB * W), 0) % L) // n_tok
            == lax.broadcasted_iota(jnp.int32, (GROUP * L, NB * W), 1) // W)
    lane = lax.broadcasted_iota(jnp.int32, (L, LANES), 1)
    nt = (((1,), (1,)), ((), ()))
    outs = []
    for g in range(N_KV):
        qs = _stack_group_queries(q, g)
        sn = lax.dot_general(qs, knd[g].astype(BF16), nt, preferred_element_type=F32) * (HEAD_DIM ** -0.5)
        sc_all = lax.dot_general(qs, kcd[g].astype(BF16), nt, preferred_element_type=F32) * (HEAD_DIM ** -0.5)
        pcs, pns = [], []
        for hh in range(GROUP):
            h = GROUP * g + hh
            sink = sink_ref[h]
            slope = _alibi_slope(h)
            rows = slice(hh * L, (hh + 1) * L)
            blk = jnp.where(own1, sc_all[rows], 0.0)
            sc = blk[:, 0:W]
            for b in range(1, NB):
                sc = sc + blk[:, b * W:(b + 1) * W]
            shc = jnp.where(valid_c, sc - slope * dcf, NEG)
            shn = jnp.where(valid_n, sn[rows] - slope * dnf, NEG)
            mx = jnp.maximum(jnp.maximum(jnp.max(shc, axis=1, keepdims=True),
                                         jnp.max(shn, axis=1, keepdims=True)), sink)
            pc = jnp.exp(shc - mx)
            pn = jnp.exp(shn - mx)
            inv = 1.0 / (jnp.sum(pc, axis=1, keepdims=True) + jnp.sum(pn, axis=1, keepdims=True)
                         + jnp.exp(sink - mx))
            pcs.append(pc * inv)
            pns.append((pn * inv).astype(BF16))
        pc4 = jnp.concatenate(pcs, axis=0)
        p_bd = jnp.where(own4, jnp.concatenate([pc4] * NB, axis=1), 0.0).astype(BF16)
        o = (jnp.dot(p_bd, vcd[g].astype(BF16), preferred_element_type=F32)
             + jnp.dot(jnp.concatenate(pns, axis=0), vnd[g].astype(BF16), preferred_element_type=F32))
        for pp in range(GROUP // 2):
            outs.append(jnp.where(lane < HEAD_DIM, o[(2 * pp) * L:(2 * pp + 1) * L],
                                  o[(2 * pp + 1) * L:(2 * pp + 2) * L]))
    o_ref[...] = jnp.concatenate(outs, axis=1)


def _swa_decode(sinks, qa, ka, va, kc, vc, row0, n_tok):
    nb = kc.shape[0]
    g_nb = LANES // n_tok
    blk0 = row0 // LANES
    tokb = lambda w: pl.BlockSpec((LANES, w), lambda i: (blk0 + i, 0))
    cache = pl.BlockSpec((g_nb, WINDOW, DKV), lambda i: (i, 0, 0))
    return pl.pallas_call(
        functools.partial(_swa_decode_kernel, n_tok),
        grid=(nb // g_nb,),
        in_specs=[pl.BlockSpec(memory_space=pltpu.SMEM), tokb(DM), tokb(DKV), tokb(DKV), cache, cache],
        out_specs=(pl.BlockSpec((LANES, DM), lambda i: (i, 0)), cache, cache),
        out_shape=(jax.ShapeDtypeStruct((nb * n_tok, DM), F32),
                   jax.ShapeDtypeStruct(kc.shape, F32), jax.ShapeDtypeStruct(vc.shape, F32)),
        compiler_params=_cparams(("arbitrary",), VMEM_LIMIT),
        name="swa_decode",
    )(sinks, qa, ka, va, kc, vc)


def _outproj_router_kernel(n_ptiles, xp_ref, xs_ref, hmp_ref, hap_ref, hms_ref, has_ref, wom_ref, woa_ref,
                           g_ref, wrt_ref, br_ref,
                           x1_ref, xg_ref, meta_ref, cnt_ref):
    i = pl.program_id(0)
    is_p = i < n_ptiles
    x = jnp.where(is_p, xp_ref[...], xs_ref[...])
    hm = jnp.where(is_p, hmp_ref[...], hms_ref[...].astype(BF16))
    ha = jnp.where(is_p, hap_ref[...], has_ref[...].astype(BF16))
    x1 = (x + jnp.dot(hm, wom_ref[...], preferred_element_type=F32)
          + jnp.dot(ha, woa_ref[...], preferred_element_type=F32))
    x1_ref[...] = x1
    h2 = _rms(x1, g_ref[...])
    logits = lax.dot_general(wrt_ref[...], h2, (((1,), (1,)), ((), ())), precision=HIGHEST,
                             preferred_element_type=F32) + br_ref[...]
    eidx = lax.broadcasted_iota(jnp.int32, logits.shape, 0).astype(F32)
    work = logits
    vals, hots = [], []
    for _ in range(TOP_K):
        mv = jnp.max(work, axis=0, keepdims=True)
        sel = jnp.min(jnp.where(work == mv, eidx, float(N_EXPERTS)), axis=0, keepdims=True)
        hot = eidx == sel
        vals.append(mv)
        hots.append(hot)
        work = jnp.where(hot, -jnp.inf, work)
    es = [jnp.exp(v - vals[0]) for v in vals]
    tot = es[0] + es[1] + es[2] + es[3]
    gates = [e / tot for e in es]
    hot_all = jnp.where(hots[0] | hots[1] | hots[2] | hots[3], 1.0, 0.0)
    tm = logits.shape[1]
    su = (lax.broadcasted_iota(jnp.int32, (tm, tm), 0) < lax.broadcasted_iota(jnp.int32, (tm, tm), 1))
    cum = jnp.dot(hot_all.astype(BF16), su.astype(BF16), preferred_element_type=F32)
    cnt = jnp.sum(hot_all, axis=1, keepdims=True)
    cpad = (((cnt.astype(jnp.int32) + (SUBLANES - 1)) // SUBLANES) * SUBLANES).astype(F32)
    lower = (lax.broadcasted_iota(jnp.int32, (N_EXPERTS, N_EXPERTS), 0)
             > lax.broadcasted_iota(jnp.int32, (N_EXPERTS, N_EXPERTS), 1)).astype(F32)
    lstart = jnp.dot(lower, jnp.broadcast_to(cpad, (N_EXPERTS, LANES)), precision=HIGHEST,
                     preferred_element_type=F32)[:, 0:1]
    base = lstart + cum
    lpos = [jnp.sum(jnp.where(hot, base, 0.0), axis=0, keepdims=True) for hot in hots]
    lpi = [p.astype(jnp.int32) for p in lpos]
    r_iota = lax.broadcasted_iota(jnp.int32, (GROUP_R, tm), 0)
    sel01 = jnp.where(r_iota == lpi[0], 1.0, jnp.where(r_iota == lpi[1], 1.0, jnp.where(
        r_iota == lpi[2], 1.0, jnp.where(r_iota == lpi[3], 1.0, 0.0)))).astype(BF16)
    xg_ref[...] = jnp.dot(sel01, h2.astype(BF16), preferred_element_type=F32)
    meta_ref[...] = jnp.transpose(jnp.concatenate(gates + lpos, axis=0))
    cnt_ref[0] = jnp.broadcast_to(cnt, (N_EXPERTS, LANES))


def _outproj_router(xp, xs, hmp, hap, hms, has, wom, woa, g_ffn, wrt, br):
    tp, ts = xp.shape[0], xs.shape[0]
    n_pt, n_st = tp // ROUTE_T, ts // ROUTE_T
    t_all = tp + ts
    pblk = lambda w: pl.BlockSpec((ROUTE_T, w), lambda i: (jnp.minimum(i, n_pt - 1), 0))
    sblk = lambda w: pl.BlockSpec((ROUTE_T, w), lambda i: (jnp.maximum(i - n_pt, 0), 0))
    full = lambda a: pl.BlockSpec(a.shape, lambda i: (0,) * a.ndim)
    return pl.pallas_call(
        functools.partial(_outproj_router_kernel, n_pt),
        grid=(n_pt + n_st,),
        in_specs=[pblk(D_MODEL), sblk(D_MODEL), pblk(DM), pblk(DM), sblk(DM), sblk(DM),
                  full(wom), full(woa), full(g_ffn), full(wrt), full(br)],
        out_specs=(pl.BlockSpec((ROUTE_T, D_MODEL), lambda i: (i, 0)),
                   pl.BlockSpec((GROUP_R, D_MODEL), lambda i: (i, 0)),
                   pl.BlockSpec((ROUTE_T, 2 * TOP_K), lambda i: (i, 0)),
                   pl.BlockSpec((1, N_EXPERTS, LANES), lambda i: (i, 0, 0))),
        out_shape=(jax.ShapeDtypeStruct((t_all, D_MODEL), F32),
                   jax.ShapeDtypeStruct(((n_pt + n_st) * GROUP_R, D_MODEL), F32),
                   jax.ShapeDtypeStruct((t_all, 2 * TOP_K), F32),
                   jax.ShapeDtypeStruct((n_pt + n_st, N_EXPERTS, LANES), F32)),
        compiler_params=_cparams(("arbitrary",), VMEM_LIMIT),
        name="outproj_router",
    )(xp, xs, hmp, hap, hms, has, wom, woa, g_ffn, wrt, br)


def _expert_kernel(be_ref, na_ref, slot_ref, nxt_ref, ctab_ref, ctab1_ref, ctab2_ref, xg_ref, wgu_ref, bgu_ref,
                   wd_ref, bd_ref, o_ref, xbuf, wgu_f, wd_f, wgu_s, wd_s, xsem, wsem):
    i = pl.program_id(0)
    na = na_ref[0]

    def x_copies(tab_ref, slot):
        return [pltpu.make_async_copy(xg_ref.at[pl.ds(pl.multiple_of(tab_ref[0, 0, c], SUBLANES), SUBLANES)],
                                      xbuf.at[slot, pl.ds(c * SUBLANES, SUBLANES)], xsem.at[slot])
                for c in range(MOE_BM // SUBLANES)]

    def w_copies(e, slot):
        return [pltpu.make_async_copy(wgu_ref.at[e], wgu_f.at[slot], wsem.at[slot]),
                pltpu.make_async_copy(wd_ref.at[e], wd_f.at[slot], wsem.at[slot])]

    @pl.when(i == 0)
    def _():
        for cp in x_copies(ctab_ref, 0) + x_copies(ctab1_ref, 1) + w_copies(be_ref[0], 0):
            cp.start()

    @pl.when(i == na)
    def _():
        for cp in x_copies(ctab_ref, lax.rem(i, X_SLOTS)) + x_copies(ctab1_ref, lax.rem(i + 1, X_SLOTS)):
            cp.wait()

    @pl.when(i < na)
    def _():
        changed = (i == 0) | (be_ref[i] != be_ref[jnp.maximum(i - 1, 0)])
        wslot = slot_ref[i]
        xslot = lax.rem(i, X_SLOTS)

        @pl.when(changed)
        def _():
            for cp in w_copies(be_ref[i], wslot):
                cp.wait()

            @pl.when(nxt_ref[i] >= 0)
            def _():
                for cp in w_copies(nxt_ref[i], 1 - wslot):
                    cp.start(priority=1)

            wgu_s[...] = wgu_f[wslot].astype(BF16)
            wd_s[...] = wd_f[wslot].astype(BF16)

        for cp in x_copies(ctab_ref, xslot):
            cp.wait()

        x = xbuf[xslot].astype(BF16)
        gu = jnp.dot(x, wgu_s[...], preferred_element_type=F32) + bgu_ref[...]
        for cp in x_copies(ctab2_ref, lax.rem(i + 2, X_SLOTS)):
            cp.start()
        gate = jnp.minimum(gu[:, :D_FF], SWIGLU_LIMIT)
        up = jnp.clip(gu[:, D_FF:], -SWIGLU_LIMIT, SWIGLU_LIMIT)
        act = (up + 1.0) * (gate * _sigmoid(gate * SWIGLU_ALPHA))
        o_ref[...] = jnp.dot(act.astype(BF16), wd_s[...], preferred_element_type=F32) + bd_ref[...]


def _expert_ffn(block_exp, n_active, wslot, next_exp, ctab, xg, wgu, bgu, wd, bd):
    n_blocks = ctab.shape[0]
    n_rows = n_blocks * MOE_BM
    nch = MOE_BM // SUBLANES
    tab = lambda d: pl.BlockSpec((1, 1, nch), lambda i, *_: (jnp.minimum(i + d, n_blocks - 1), 0, 0),
                                 memory_space=pltpu.SMEM)
    blk_e = lambda i, be: be[jnp.minimum(i, n_blocks - 1)]
    grid_spec = pltpu.PrefetchScalarGridSpec(
        num_scalar_prefetch=4,
        grid=(n_blocks + 1,),
        in_specs=[tab(0), tab(1), tab(2),
                  pl.BlockSpec(memory_space=pl.ANY),
                  pl.BlockSpec(memory_space=pl.ANY),
                  pl.BlockSpec((None, 1, 2 * D_FF), lambda i, be, *_: (blk_e(i, be), 0, 0)),
                  pl.BlockSpec(memory_space=pl.ANY),
                  pl.BlockSpec((None, 1, D_MODEL), lambda i, be, *_: (blk_e(i, be), 0, 0))],
        out_specs=pl.BlockSpec((MOE_BM, D_MODEL), lambda i, be, na, *_: (jnp.minimum(i, na[0] - 1), 0)),
        scratch_shapes=[pltpu.VMEM((X_SLOTS, MOE_BM, D_MODEL), F32),
                        pltpu.VMEM((2, D_MODEL, 2 * D_FF), F32), pltpu.VMEM((2, D_FF, D_MODEL), F32),
                        pltpu.VMEM((D_MODEL, 2 * D_FF), BF16), pltpu.VMEM((D_FF, D_MODEL), BF16),
                        pltpu.SemaphoreType.DMA((X_SLOTS,)), pltpu.SemaphoreType.DMA((2,))],
    )
    return pl.pallas_call(
        _expert_kernel,
        grid_spec=grid_spec,
        out_shape=jax.ShapeDtypeStruct((n_rows, D_MODEL), F32),
        compiler_params=_cparams(("arbitrary",), VMEM_LIMIT),
        name="moe_experts",
    )(block_exp, n_active, wslot, next_exp, ctab, ctab, ctab, xg, wgu, bgu, wd, bd)


def _combine_kernel(n_ptiles, ctab_ref, ctab_next_ref, outs_ref, x1_ref, meta_ref, gf_ref, yp_ref, ys_ref,
                    obuf, sem):
    i = pl.program_id(0)
    n = pl.num_programs(0)
    slot = lax.rem(i, 2)

    def copies(tab_ref, s):
        return [pltpu.make_async_copy(outs_ref.at[pl.ds(pl.multiple_of(tab_ref[0, 0, c], SUBLANES), SUBLANES)],
                                      obuf.at[s, pl.ds(c * SUBLANES, SUBLANES)], sem.at[s])
                for c in range(GROUP_R // SUBLANES)]

    @pl.when(i == 0)
    def _():
        for cp in copies(ctab_ref, 0):
            cp.start()

    @pl.when(i + 1 < n)
    def _():
        for cp in copies(ctab_next_ref, 1 - slot):
            cp.start()

    for cp in copies(ctab_ref, slot):
        cp.wait()

    meta = meta_ref[...]
    tm = meta.shape[0]
    r_iota = lax.broadcasted_iota(jnp.int32, (tm, GROUP_R), 1)
    lp = [meta[:, TOP_K + k:TOP_K + k + 1].astype(jnp.int32) for k in range(TOP_K)]
    gk = [meta[:, k:k + 1] for k in range(TOP_K)]
    gsel = jnp.where(r_iota == lp[0], gk[0], jnp.where(r_iota == lp[1], gk[1], jnp.where(
        r_iota == lp[2], gk[2], jnp.where(r_iota == lp[3], gk[3], 0.0))))
    sel01 = jnp.where(gsel != 0.0, 1.0, 0.0).astype(BF16)
    rg_row = jnp.sum(gsel, axis=0, keepdims=True)
    rg_col = jnp.transpose(jnp.broadcast_to(rg_row, (SUBLANES, GROUP_R)))[:, 0:1]
    og = (obuf[slot] * rg_col).astype(BF16)
    acc = x1_ref[...] + jnp.dot(sel01, og, preferred_element_type=F32)
    y = _rms(acc, gf_ref[...])

    @pl.when(i < n_ptiles)
    def _():
        yp_ref[...] = y

    @pl.when(i >= n_ptiles)
    def _():
        ys_ref[...] = y


def _combine(ctab, outs, x1, meta, g_final, tp, ts):
    n_pt, n_st = tp // ROUTE_T, ts // ROUTE_T
    n = n_pt + n_st
    nch = GROUP_R // SUBLANES
    return pl.pallas_call(
        functools.partial(_combine_kernel, n_pt),
        grid=(n,),
        in_specs=[pl.BlockSpec((1, 1, nch), lambda i: (i, 0, 0), memory_space=pltpu.SMEM),
                  pl.BlockSpec((1, 1, nch), lambda i: (jnp.minimum(i + 1, n - 1), 0, 0), memory_space=pltpu.SMEM),
                  pl.BlockSpec(memory_space=pl.ANY),
                  pl.BlockSpec((ROUTE_T, D_MODEL), lambda i: (i, 0)),
                  pl.BlockSpec((ROUTE_T, 2 * TOP_K), lambda i: (i, 0)),
                  pl.BlockSpec((1, D_MODEL), lambda i: (0, 0))],
        out_specs=(pl.BlockSpec((ROUTE_T, D_MODEL), lambda i: (jnp.minimum(i, n_pt - 1), 0)),
                   pl.BlockSpec((ROUTE_T, D_MODEL), lambda i: (jnp.maximum(i - n_pt, 0), 0))),
        out_shape=(jax.ShapeDtypeStruct((tp, D_MODEL), F32), jax.ShapeDtypeStruct((ts, D_MODEL), F32)),
        scratch_shapes=[pltpu.VMEM((2, GROUP_R, D_MODEL), F32), pltpu.SemaphoreType.DMA((2,))],
        compiler_params=_cparams(("arbitrary",), VMEM_LIMIT),
        name="moe_combine",
    )(ctab, ctab, outs, x1, meta, g_final)


def kernel(x_prompt, x_sample, cache_swa_k, cache_swa_v, state_mlstm_c, state_mlstm_n, state_mlstm_m,
           g_mix, w_in, b_igate, b_fgate, g_head, attn_sinks, w_out, g_ffn, w_router, b_router,
           w_gate_up, b_gate_up, w_down, b_down, g_final):
    assert w_in.shape[0] == 1, "single-layer problem"
    B, S, _ = x_prompt.shape
    Bd, Tn, _ = x_sample.shape
    tp, ts = B * S, Bd * Tn
    t_all = tp + ts
    xp = x_prompt.reshape(tp, D_MODEL)
    xs = x_sample.reshape(ts, D_MODEL)

    w = w_in[0]
    o = np.cumsum([0, DM, DM, DM, DM, N_HEADS, N_HEADS, DM, DKV, DKV])
    col = lambda a: w[:, int(o[a]):int(o[a + 1])]
    wgates = jnp.concatenate([col(4), col(5)], axis=1)
    w1 = jnp.concatenate([col(1), col(6), col(7), col(8), jnp.pad(wgates, ((0, 0), (0, LANES - 2 * N_HEADS)))],
                         axis=1).astype(BF16)
    wkt = jnp.concatenate([col(0), col(2), col(3), wgates], axis=1).T.astype(BF16)
    bg = jnp.concatenate([b_igate[0], b_fgate[0]]).astype(F32)
    bcol = jnp.pad(bg, (0, LANES - 2 * N_HEADS)).reshape(1, LANES)
    brow = bg.reshape(2 * N_HEADS, 1)

    km, qa, ka, va, qmt, vmt, omt, gcol, grow = _inproj(xp, xs, g_mix[0].reshape(1, D_MODEL), w1, wkt, bcol, brow)

    gh = g_head[0].astype(F32)
    sinks = attn_sinks[0].astype(F32)

    hm_p, ctp, m_p = _mlstm_prompt(km, qmt, vmt, omt, gcol, grow,
                                   jnp.broadcast_to(gh.reshape(DM, 1), (DM, LANES)), B, S)
    ha_p = _swa_prompt(sinks, qa, ka, va, B, S)

    hm_s, c_s, n_s, mt_s = _mlstm_sample(km, qmt, vmt, omt, gcol, grow, state_mlstm_c[0], state_mlstm_n[0],
                                         state_mlstm_m[0], gh.reshape(1, DM), tp, Tn)
    n_s = n_s.reshape(Bd, N_HEADS, HEAD_DIM)
    m_s = mt_s[:, :, :LANES // Tn].transpose(0, 2, 1).reshape(Bd, N_HEADS)
    ha_s, k_s, v_s = _swa_decode(sinks, qa, ka, va, cache_swa_k[0].reshape(Bd, WINDOW, DKV),
                                 cache_swa_v[0].reshape(Bd, WINDOW, DKV), tp, Tn)

    wo = w_out[0].astype(BF16)
    x1, xg, meta, cnt = _outproj_router(
        xp, xs, hm_p, ha_p, hm_s, ha_s, wo[:DM], wo[DM:], g_ffn[0].reshape(1, D_MODEL),
        w_router[0].T, b_router[0].reshape(N_EXPERTS, 1))

    i32 = jnp.int32
    n_tiles = t_all // ROUTE_T
    max_rows = t_all * TOP_K + n_tiles * N_EXPERTS * (SUBLANES - 1) + N_EXPERTS * (MOE_BM - 1)
    n_blocks = -(-max_rows // MOE_BM)
    cpad = (cnt[:, :, 0].astype(i32) + (SUBLANES - 1)) // SUBLANES * SUBLANES
    lstart = jnp.cumsum(cpad, axis=1) - cpad
    goff = jnp.cumsum(cpad, axis=0) - cpad
    padded = (jnp.sum(cpad, axis=0) + MOE_BM - 1) // MOE_BM * MOE_BM
    pad_end = jnp.cumsum(padded)
    seg_begin = (pad_end - padded)[None, :] + goff
    n_active = (pad_end[-1] // MOE_BM).astype(i32)
    blk = jnp.minimum(jnp.arange(n_blocks, dtype=i32), n_active - 1)
    block_exp = jnp.minimum(jnp.sum((pad_end[None, :] <= (blk * MOE_BM)[:, None]).astype(i32), axis=1),
                            N_EXPERTS - 1)
    e_ids = jnp.arange(N_EXPERTS, dtype=i32)
    nonempty = padded > 0
    nxt_e = jnp.min(jnp.where((e_ids[None, :] > e_ids[:, None]) & nonempty[None, :], e_ids[None, :], N_EXPERTS),
                    axis=1)
    nxt_e = jnp.where(nxt_e == N_EXPERTS, -1, nxt_e)
    ord_e = jnp.cumsum(nonempty.astype(i32)) - 1
    be_hot = block_exp[:, None] == e_ids[None, :]
    next_exp = jnp.sum(jnp.where(be_hot, nxt_e[None, :], 0), axis=1).astype(i32)
    wslot = (jnp.sum(jnp.where(be_hot, ord_e[None, :], 0), axis=1) % 2).astype(i32)

    seg_src = jnp.arange(n_tiles, dtype=i32)[:, None] * GROUP_R + lstart
    sb, sl, ss = seg_begin.reshape(-1), cpad.reshape(-1), seg_src.reshape(-1)
    rc = jnp.arange(n_blocks * MOE_BM // SUBLANES, dtype=i32)[:, None] * SUBLANES
    inseg = (sb[None, :] <= rc) & (rc < (sb + sl)[None, :])
    ctab_e = jnp.where(jnp.any(inseg, axis=1), jnp.sum(jnp.where(inseg, (ss - sb)[None, :] + rc, 0), axis=1),
                       GROUP_R - SUBLANES)
    lr = jnp.arange(GROUP_R // SUBLANES, dtype=i32)[None, :, None] * SUBLANES
    inl = (lstart[:, None, :] <= lr) & (lr < (lstart + cpad)[:, None, :])
    ctab_c = jnp.sum(jnp.where(inl, (seg_begin - lstart)[:, None, :] + lr, 0), axis=2)

    outs = _expert_ffn(block_exp, n_active.reshape(1), wslot, next_exp,
                       ctab_e.astype(i32).reshape(n_blocks, 1, MOE_BM // SUBLANES), xg, w_gate_up[0],
                       b_gate_up[0].reshape(N_EXPERTS, 1, 2 * D_FF), w_down[0],
                       b_down[0].reshape(N_EXPERTS, 1, D_MODEL))
    y_p, y_s = _combine(ctab_c.astype(i32).reshape(n_tiles, 1, GROUP_R // SUBLANES), outs, x1, meta,
                        g_final.reshape(1, D_MODEL), tp, ts)

    kv_tail = lambda a: jnp.concatenate([a[(b + 1) * S - WINDOW:(b + 1) * S] for b in range(B)], axis=0).reshape(
        1, B, WINDOW, N_KV, HEAD_DIM)
    c_e = ctp[:, :, :HEAD_DIM, :HEAD_DIM]
    c_o = ctp[:, :, HEAD_DIM:LANES, HEAD_DIM:]
    c_p = jnp.swapaxes(jnp.stack([c_e, c_o], axis=2), -1, -2).reshape(B, N_HEADS, HEAD_DIM, HEAD_DIM)
    n_p = (ctp[:, :, LANES:LANES + 2, :HEAD_DIM] + ctp[:, :, LANES:LANES + 2, HEAD_DIM:]).reshape(B, N_HEADS, HEAD_DIM)
    return (y_p.reshape(B, S, D_MODEL), y_s.reshape(Bd, Tn, D_MODEL),
            kv_tail(ka), kv_tail(va), c_p[None], n_p[None], m_p[:, :, 0][None],
            k_s.reshape(Bd, WINDOW, N_KV, HEAD_DIM)[None], v_s.reshape(Bd, WINDOW, N_KV, HEAD_DIM)[None],
            c_s[None], n_s[None], m_s[None])
```

```python
import functools

import jax
import jax.numpy as jnp
import numpy as np
from jax import lax
from jax.experimental import pallas as pl
from jax.experimental.pallas import tpu as pltpu

F32 = jnp.float32
BF16 = jnp.bfloat16
HIGHEST = lax.Precision.HIGHEST

D_MODEL = 1024
HEAD_DIM = 64
N_HEADS = 8
N_PAIRS = N_HEADS // 2
N_KV = 2
GROUP = N_HEADS // N_KV
WINDOW = 128
N_EXPERTS = 32
TOP_K = 4
D_FF = 1024
SWIGLU_LIMIT = 7.0
SWIGLU_ALPHA = 1.702
RMS_EPS = 1e-5
DM = N_HEADS * HEAD_DIM
DKV = N_KV * HEAD_DIM
NEG = -1e30

LANES = 128
SUBLANES = 8
VMEM_LIMIT = 56 * 1024 * 1024

TM = 512
ROUTE_T = 512
MLSTM_TL = 512
MLSTM_L = 128
SWA_BLOCKS = 8
MOE_BM = 256
X_SLOTS = 3
GROUP_R = -(-(TOP_K * ROUTE_T + N_EXPERTS * (SUBLANES - 1) + SUBLANES) // LANES) * LANES


def _cparams(sem, vmem=None):
    return pltpu.CompilerParams(dimension_semantics=sem, vmem_limit_bytes=vmem)


def _rms(x, g):
    return x * lax.rsqrt(jnp.mean(x * x, axis=-1, keepdims=True) + RMS_EPS) * g


def _log_sigmoid(z):
    return jnp.minimum(z, 0.0) - jnp.log(1.0 + jnp.exp(-jnp.abs(z)))


def _sigmoid(z):
    return 1.0 / (1.0 + jnp.exp(-z))


def _inproj_kernel(n_ptiles, xp_ref, xs_ref, g_ref, w1_ref, wt_ref, bcol_ref, brow_ref,
                   km_ref, qa_ref, ka_ref, va_ref, qmt_ref, vmt_ref, omt_ref, gcol_ref, grow_ref):
    i = pl.program_id(0)
    x = jnp.where(i < n_ptiles, xp_ref[...], xs_ref[...])
    h = _rms(x, g_ref[...]).astype(BF16)
    main = jnp.dot(h, w1_ref[...], preferred_element_type=F32)
    km_ref[...] = main[:, 0:DM] * (HEAD_DIM ** -0.5)
    qa_ref[...] = main[:, DM:2 * DM]
    ka_ref[...] = main[:, 2 * DM:2 * DM + DKV]
    va_ref[...] = main[:, 2 * DM + DKV:2 * DM + 2 * DKV]
    t = lax.dot_general(wt_ref[...], h, (((1,), (1,)), ((), ())), preferred_element_type=F32)
    qmt_ref[...] = t[0:DM]
    vmt_ref[...] = t[DM:2 * DM]
    omt_ref[...] = t[2 * DM:3 * DM]
    zc = main[:, 2 * DM + 2 * DKV:] + bcol_ref[...]
    lane = lax.broadcasted_iota(jnp.int32, zc.shape, 1)
    gcol_ref[...] = jnp.where(lane < N_HEADS, zc, _log_sigmoid(zc))
    zr = t[3 * DM:] + brow_ref[...]
    row = lax.broadcasted_iota(jnp.int32, zr.shape, 0)
    grow_ref[...] = jnp.where(row < N_HEADS, zr, _log_sigmoid(zr))


def _inproj(xp, xs, g_mix, w1, wkt, bcol, brow):
    tp, ts = xp.shape[0], xs.shape[0]
    n_pt, n_st = tp // TM, ts // TM
    t_all = tp + ts
    tok = lambda w: pl.BlockSpec((TM, w), lambda i: (i, 0))
    tr = lambda r: pl.BlockSpec((r, TM), lambda i: (0, i))
    full = lambda a: pl.BlockSpec(a.shape, lambda i: (0,) * a.ndim)
    out_shape = (
        jax.ShapeDtypeStruct((t_all, DM), F32), jax.ShapeDtypeStruct((t_all, DM), F32),
        jax.ShapeDtypeStruct((t_all, DKV), F32), jax.ShapeDtypeStruct((t_all, DKV), F32),
        jax.ShapeDtypeStruct((DM, t_all), F32), jax.ShapeDtypeStruct((DM, t_all), F32),
        jax.ShapeDtypeStruct((DM, t_all), F32),
        jax.ShapeDtypeStruct((t_all, LANES), F32), jax.ShapeDtypeStruct((2 * N_HEADS, t_all), F32),
    )
    return pl.pallas_call(
        functools.partial(_inproj_kernel, n_pt),
        grid=(n_pt + n_st,),
        in_specs=[
            pl.BlockSpec((TM, D_MODEL), lambda i: (jnp.minimum(i, n_pt - 1), 0)),
            pl.BlockSpec((TM, D_MODEL), lambda i: (jnp.maximum(i - n_pt, 0), 0)),
            full(g_mix), full(w1), full(wkt), full(bcol), full(brow),
        ],
        out_specs=(tok(DM), tok(DM), tok(DKV), tok(DKV), tr(DM), tr(DM), tr(DM),
                   tok(LANES), tr(2 * N_HEADS)),
        out_shape=out_shape,
        compiler_params=_cparams(("arbitrary",), VMEM_LIMIT),
        name="inproj",
    )(xp, xs, g_mix, w1, wkt, bcol, brow)


CT_ROWS = LANES + 2 * SUBLANES


def _cumsum_rows(x, n):
    row = lax.broadcasted_iota(jnp.int32, x.shape, 0)
    sh = 1
    while sh < n:
        x = x + jnp.where(row >= sh, pltpu.roll(x, sh, axis=0), 0.0)
        sh *= 2
    return x


def _mlstm_prompt_kernel(km_ref, qmt_ref, vmt_ref, omt_ref, gcol_ref, grow_ref, ghr_ref,
                         hm_ref, ct_ref, m_ref, ct_s, m_s):
    j = pl.program_id(1)
    L = MLSTM_L
    assert L == LANES

    @pl.when(j == 0)
    def _():
        ct_s[...] = jnp.zeros_like(ct_s)
        m_s[...] = jnp.zeros_like(m_s)

    si = lax.broadcasted_iota(jnp.int32, (L, L), 0)
    ti = lax.broadcasted_iota(jnp.int32, (L, L), 1)
    causal_t = si <= ti
    upper = jnp.where(causal_t, 1.0, 0.0)
    rows_c = lax.broadcasted_iota(jnp.int32, (CT_ROWS, 1), 0)
    rmask_e = (rows_c < HEAD_DIM) | (rows_c == LANES)
    rmask_o = ((rows_c >= HEAD_DIM) & (rows_c < LANES)) | (rows_c == LANES + 1)
    rows_e = lax.broadcasted_iota(jnp.int32, (LANES, 1), 0) < HEAD_DIM
    cols_e = lax.broadcasted_iota(jnp.int32, (1, LANES), 1) < HEAD_DIM
    bdt_mask = (rmask_e & cols_e) | (rmask_o & (~cols_e))
    ones_rows = jnp.where(lax.broadcasted_iota(jnp.int32, (CT_ROWS - LANES, L), 0) < 2, 1.0, 0.0)

    for c in range(MLSTM_TL // L):
        sl = slice(c * L, (c + 1) * L)
        grow = grow_ref[:, sl]
        i_row = grow[0:N_HEADS]
        b_row = jnp.dot(grow[N_HEADS:2 * N_HEADS], upper, precision=HIGHEST,
                        preferred_element_type=F32)
        gc = gcol_ref[sl, :]
        bc_all = _cumsum_rows(gc, L)
        for p in range(N_PAIRS):
            ls = slice(p * LANES, (p + 1) * LANES)
            k2 = km_ref[sl, ls]
            qt2 = qmt_ref[ls, sl]
            qt_e = jnp.where(rows_e, qt2, 0.0).astype(BF16)
            qt_o = jnp.where(rows_e, 0.0, qt2).astype(BF16)
            st2 = jnp.dot(k2.astype(BF16), jnp.concatenate([qt_e, qt_o], axis=1),
                          preferred_element_type=F32)
            ct = ct_s[p]
            rqt = jnp.dot(ct.astype(BF16), qt2.astype(BF16), preferred_element_type=F32)
            pts, mts, inters, wreps, decays, mnews = [], [], [], [], [], []
            for hh in range(2):
                h = 2 * p + hh
                cvec = jnp.broadcast_to(gc[:, h:h + 1] - bc_all[:, N_HEADS + h:N_HEADS + h + 1], (L, L))
                brow = b_row[h:h + 1, :]
                logdt = jnp.where(causal_t, cvec + brow, NEG)
                m_prev = m_s[h:h + 1, 0:1]
                m_inter = m_prev + brow
                m_t = jnp.maximum(m_inter, jnp.max(logdt, axis=0, keepdims=True))
                pts.append((st2[:, hh * L:(hh + 1) * L] * jnp.exp(logdt - m_t)).astype(BF16))
                mts.append(m_t)
                inters.append(jnp.exp(m_inter - m_t))
                m_new = m_t[:, L - 1:L]
                b_last = brow[:, L - 1:L]
                decays.append(jnp.exp(m_prev + b_last - m_new))
                wreps.append(jnp.exp(cvec + (b_last - m_new)))
                mnews.append(m_new)
            vext = jnp.concatenate([vmt_ref[ls, sl], ones_rows], axis=0)
            lhs = jnp.concatenate([jnp.where(rmask_e, vext, 0.0), jnp.where(rmask_o, vext, 0.0)],
                                  axis=1).astype(BF16)
            rt = jnp.dot(lhs, jnp.concatenate(pts, axis=0), preferred_element_type=F32)
            ndt = rt + jnp.where(rmask_e, inters[0], inters[1]) * rqt
            den = jnp.where(rows_e, ndt[LANES:LANES + 1], ndt[LANES + 1:LANES + 2])
            mt2 = jnp.where(rows_e, mts[0], mts[1])
            hvt = ndt[0:LANES] / jnp.maximum(jnp.abs(den), jnp.exp(-mt2))
            sq = hvt * hvt
            ms = jnp.where(rows_e, jnp.sum(sq[0:HEAD_DIM], axis=0, keepdims=True),
                           jnp.sum(sq[HEAD_DIM:LANES], axis=0, keepdims=True)) * (1.0 / HEAD_DIM)
            yt = hvt * lax.rsqrt(ms + RMS_EPS) * ghr_ref[ls, :] * _sigmoid(omt_ref[ls, sl])
            hm_ref[sl, ls] = jnp.transpose(yt).astype(hm_ref.dtype)
            kw = (k2 * jnp.where(cols_e, wreps[0], wreps[1])).astype(BF16)
            upd = jnp.dot(vext.astype(BF16), kw, preferred_element_type=F32)
            ct_s[p] = jnp.where(rmask_e, decays[0], decays[1]) * ct + jnp.where(bdt_mask, upd, 0.0)
            for hh in range(2):
                h = 2 * p + hh
                m_s[h:h + 1, :] = jnp.broadcast_to(mnews[hh], (1, LANES))

    @pl.when(j == pl.num_programs(1) - 1)
    def _():
        ct_ref[0] = ct_s[...]
        m_ref[0] = m_s[...]


def _mlstm_prompt(km, qmt, vmt, omt, gcol, grow, ghr, batch, seq):
    nt = seq // MLSTM_TL
    tokb = lambda w: pl.BlockSpec((MLSTM_TL, w), lambda b, j: (b * nt + j, 0))
    rowb = lambda r: pl.BlockSpec((r, MLSTM_TL), lambda b, j: (0, b * nt + j))
    return pl.pallas_call(
        _mlstm_prompt_kernel,
        grid=(batch, nt),
        in_specs=[tokb(DM), rowb(DM), rowb(DM), rowb(DM), tokb(LANES), rowb(2 * N_HEADS),
                  pl.BlockSpec((DM, LANES), lambda b, j: (0, 0))],
        out_specs=(tokb(DM),
                   pl.BlockSpec((1, N_PAIRS, CT_ROWS, LANES), lambda b, j: (b, 0, 0, 0)),
                   pl.BlockSpec((1, N_HEADS, LANES), lambda b, j: (b, 0, 0))),
        out_shape=(jax.ShapeDtypeStruct((batch * seq, DM), BF16),
                   jax.ShapeDtypeStruct((batch, N_PAIRS, CT_ROWS, LANES), F32),
                   jax.ShapeDtypeStruct((batch, N_HEADS, LANES), F32)),
        scratch_shapes=[pltpu.VMEM((N_PAIRS, CT_ROWS, LANES), F32), pltpu.VMEM((N_HEADS, LANES), F32)],
        compiler_params=_cparams(("arbitrary", "arbitrary"), VMEM_LIMIT),
        name="mlstm_prompt",
    )(km, qmt, vmt, omt, gcol, grow, ghr)


def _mlstm_sample_kernel(n_tok, km_ref, qmt_ref, vmt_ref, omt_ref, gcol_ref, grow_ref, c0_ref, n0_ref, m0_ref,
                         m0t_ref, gh_ref, hm_ref, c_ref, n_ref, mt_ref):
    L = LANES
    NB = L // n_tok
    ti = lax.broadcasted_iota(jnp.int32, (L, L), 0)
    si = lax.broadcasted_iota(jnp.int32, (L, L), 1)
    same = (ti // n_tok) == (si // n_tok)
    causal = same & (ti >= si)
    useg = jnp.where(same & (ti <= si), 1.0, 0.0)
    slast = jnp.where(same & (ti % n_tok == n_tok - 1), 1.0, 0.0)
    expand = jnp.where(ti // n_tok == si, 1.0, 0.0)
    expand_t = jnp.where(ti == si // n_tok, 1.0, 0.0)
    pick = jnp.where((ti // n_tok == si) & (ti % n_tok == n_tok - 1), 1.0, 0.0)
    hdot = lambda a, b: jnp.dot(a, b, precision=HIGHEST, preferred_element_type=F32)

    lane128 = lax.broadcasted_iota(jnp.int32, (L, LANES), 1)
    even128 = lane128 < HEAD_DIM
    lane256 = lax.broadcasted_iota(jnp.int32, (1, 2 * LANES), 1)
    cols_e = (lane256 < HEAD_DIM) | (lane256 == LANES)
    cols_o = ((lane256 >= HEAD_DIM) & (lane256 < LANES)) | (lane256 == LANES + 1)
    rows_e = lax.broadcasted_iota(jnp.int32, (LANES, 1), 0) < HEAD_DIM
    ones_cols = jnp.where(lane128 < 2, 1.0, 0.0)
    bo_r = lax.broadcasted_iota(jnp.int32, (LANES, LANES), 0) // HEAD_DIM
    bo_c = lax.broadcasted_iota(jnp.int32, (LANES, LANES), 1) // HEAD_DIM
    block_ones = jnp.where(bo_r == bo_c, 1.0, 0.0)
    W = NB * LANES
    rb = lax.broadcasted_iota(jnp.int32, (L, W), 0)
    cb = lax.broadcasted_iota(jnp.int32, (L, W), 1)
    own_block = (rb // n_tok) == (cb // LANES)
    bd_tiled = (rb // HEAD_DIM) == ((cb % LANES) // HEAD_DIM)

    grow = grow_ref[...]
    i_row = grow[0:N_HEADS]
    b_row = hdot(grow[N_HEADS:2 * N_HEADS], useg)
    b_last = hdot(b_row, slast)
    a_row = b_last - b_row + i_row
    pos = lax.broadcasted_iota(jnp.int32, a_row.shape, 1) % n_tok
    pm = a_row
    sh = 1
    while sh < n_tok:
        pm = jnp.where(pos >= sh, jnp.maximum(pm, pltpu.roll(pm, sh, axis=1)), pm)
        sh *= 2
    m_carry = hdot(jnp.concatenate([m0t_ref[0], b_row], axis=1), jnp.concatenate([expand_t, slast], axis=0))
    m_new_row = jnp.maximum(m_carry, hdot(pm, slast))
    decay_row = jnp.exp(m_carry - m_new_row)
    w_row = jnp.exp(a_row - m_new_row)
    mt_ref[0] = hdot(m_new_row, pick)
    decay_bh = hdot(decay_row, pick)
    decay_hb = jnp.transpose(decay_bh)[0:NB]

    bc_all = gcol_ref[...]
    rowpos = lax.broadcasted_iota(jnp.int32, bc_all.shape, 0) % n_tok
    sh = 1
    while sh < n_tok:
        bc_all = bc_all + jnp.where(rowpos >= sh, pltpu.roll(bc_all, sh, axis=0), 0.0)
        sh *= 2
    pad_rows = lambda a: jnp.concatenate([a, jnp.zeros((L - NB, a.shape[1]), F32)], axis=0)
    m0_col = hdot(expand, pad_rows(m0_ref[...]))
    rowv_all = i_row - b_row

    for p in range(N_PAIRS):
        ls = slice(p * LANES, (p + 1) * LANES)
        q2f = jnp.transpose(qmt_ref[ls, :])
        q2 = q2f.astype(BF16)
        kt2 = jnp.transpose(km_ref[:, ls])
        v2 = jnp.transpose(vmt_ref[ls, :])
        vext = jnp.concatenate([v2, ones_cols], axis=1)
        kt_e = jnp.where(rows_e, kt2, 0.0).astype(BF16)
        kt_o = jnp.where(rows_e, 0.0, kt2).astype(BF16)
        s2 = jnp.dot(q2, jnp.concatenate([kt_e, kt_o], axis=1), preferred_element_type=F32)
        ps, mts, inters = [], [], []
        for hh in range(2):
            h = 2 * p + hh
            bcol = bc_all[:, N_HEADS + h:N_HEADS + h + 1]
            logd = jnp.where(causal, bcol + rowv_all[h:h + 1, :], NEG)
            m_inter = m0_col[:, h:h + 1] + bcol
            m_t = jnp.maximum(m_inter, jnp.max(logd, axis=1, keepdims=True))
            ps.append((s2[:, hh * L:(hh + 1) * L] * jnp.exp(logd - m_t)).astype(BF16))
            mts.append(m_t)
            inters.append(jnp.exp(m_inter - m_t))
        vstack = jnp.concatenate([jnp.where(cols_e, vext, 0.0), jnp.where(cols_o, vext, 0.0)],
                                 axis=0).astype(BF16)
        r = jnp.dot(jnp.concatenate(ps, axis=1), vstack, preferred_element_type=F32)
        zero = jnp.zeros((HEAD_DIM, HEAD_DIM), F32)
        cstack = jnp.concatenate(
            [jnp.concatenate([jnp.concatenate([c0_ref[b, 2 * p], zero], axis=1),
                              jnp.concatenate([zero, c0_ref[b, 2 * p + 1]], axis=1)], axis=0)
             for b in range(NB)], axis=1)
        rq_all = jnp.where(own_block, jnp.dot(q2, cstack.astype(BF16), preferred_element_type=F32), 0.0)
        rq = rq_all[:, 0:LANES]
        for b in range(1, NB):
            rq = rq + rq_all[:, b * LANES:(b + 1) * LANES]
        n_rows = hdot(expand, pad_rows(n0_ref[:, ls]))
        qn = hdot(q2f * n_rows, block_ones)
        inter2 = jnp.where(even128, inters[0], inters[1])
        num = r[:, 0:LANES] + inter2 * rq
        den = jnp.where(even128, r[:, LANES:LANES + 1], r[:, LANES + 1:LANES + 2]) + inter2 * qn
        mt2 = jnp.where(even128, mts[0], mts[1])
        hv = num / jnp.maximum(jnp.abs(den), jnp.exp(-mt2))
        ms = hdot(hv * hv, block_ones) * (1.0 / HEAD_DIM)
        y = hv * lax.rsqrt(ms + RMS_EPS) * gh_ref[:, ls] * _sigmoid(jnp.transpose(omt_ref[ls, :]))
        hm_ref[:, ls] = y.astype(hm_ref.dtype)
        kw = kt2 * jnp.where(rows_e, w_row[2 * p:2 * p + 1, :], w_row[2 * p + 1:2 * p + 2, :])
        vbd = jnp.where(own_block, jnp.concatenate([v2] * NB, axis=1), 0.0).astype(BF16)
        upd = jnp.dot(kw.astype(BF16), vbd, preferred_element_type=F32)
        upd = jnp.where(bd_tiled, upd, 0.0)
        for b in range(NB):
            bs = slice(b * LANES, (b + 1) * LANES)
            dec_b = jnp.where(rows_e, decay_hb[b:b + 1, 2 * p:2 * p + 1], decay_hb[b:b + 1, 2 * p + 1:2 * p + 2])
            cnew = dec_b * cstack[:, bs] + upd[:, bs]
            c_ref[b, 2 * p] = cnew[0:HEAD_DIM, 0:HEAD_DIM]
            c_ref[b, 2 * p + 1] = cnew[HEAD_DIM:LANES, HEAD_DIM:LANES]
        nsum = jnp.transpose(hdot(kw, expand))[0:NB]
        dec_n = jnp.where(lax.broadcasted_iota(jnp.int32, (NB, LANES), 1) < HEAD_DIM,
                          decay_hb[:, 2 * p:2 * p + 1], decay_hb[:, 2 * p + 1:2 * p + 2])
        n_ref[:, ls] = dec_n * n0_ref[:, ls] + nsum


def _mlstm_sample(km, qmt, vmt, omt, gcol, grow, c0, n0, m0, gh, row0, n_tok):
    nb = c0.shape[0]
    g_nb = LANES // n_tok
    n_g = nb // g_nb
    blk0 = row0 // LANES
    tokb = lambda w: pl.BlockSpec((LANES, w), lambda i: (blk0 + i, 0))
    rowb = lambda r: pl.BlockSpec((r, LANES), lambda i: (0, blk0 + i))
    m0t = jnp.pad(m0.reshape(n_g, g_nb, N_HEADS).transpose(0, 2, 1),
                  ((0, 0), (0, 0), (0, LANES - g_nb)))
    return pl.pallas_call(
        functools.partial(_mlstm_sample_kernel, n_tok),
        grid=(n_g,),
        in_specs=[tokb(DM), rowb(DM), rowb(DM), rowb(DM), tokb(LANES), rowb(2 * N_HEADS),
                  pl.BlockSpec((g_nb, N_HEADS, HEAD_DIM, HEAD_DIM), lambda i: (i, 0, 0, 0)),
                  pl.BlockSpec((g_nb, DM), lambda i: (i, 0)),
                  pl.BlockSpec((g_nb, N_HEADS), lambda i: (i, 0)),
                  pl.BlockSpec((1, N_HEADS, LANES), lambda i: (i, 0, 0)),
                  pl.BlockSpec((1, DM), lambda i: (0, 0))],
        out_specs=(pl.BlockSpec((LANES, DM), lambda i: (i, 0)),
                   pl.BlockSpec((g_nb, N_HEADS, HEAD_DIM, HEAD_DIM), lambda i: (i, 0, 0, 0)),
                   pl.BlockSpec((g_nb, DM), lambda i: (i, 0)),
                   pl.BlockSpec((1, N_HEADS, LANES), lambda i: (i, 0, 0))),
        out_shape=(jax.ShapeDtypeStruct((nb * n_tok, DM), BF16), jax.ShapeDtypeStruct(c0.shape, F32),
                   jax.ShapeDtypeStruct((nb, DM), F32), jax.ShapeDtypeStruct((n_g, N_HEADS, LANES), F32)),
        compiler_params=_cparams(("arbitrary",), VMEM_LIMIT),
        name="mlstm_sample",
    )(km, qmt, vmt, omt, gcol, grow, c0, n0.reshape(nb, DM), m0, m0t, gh)


def _alibi_slope(h):
    return float(np.float32(2.0 ** (-8.0 * (h + 1) / N_HEADS)))


def _dup_halves(x):
    lane = lax.broadcasted_iota(jnp.int32, x.shape, 1)
    xr = pltpu.roll(x, HEAD_DIM, axis=1)
    lo = lane < HEAD_DIM
    return jnp.where(lo, x, xr), jnp.where(lo, xr, x)


def _stack_group_queries(q, g):
    lane = lax.broadcasted_iota(jnp.int32, (q.shape[0], LANES), 1)
    parts = []
    for hh in range(GROUP):
        h = GROUP * g + hh
        blk = q[:, (h // 2) * LANES:(h // 2 + 1) * LANES]
        keep = (lane < HEAD_DIM) if h % 2 == 0 else (lane >= HEAD_DIM)
        parts.append(jnp.where(keep, blk, 0.0))
    return jnp.concatenate(parts, axis=0).astype(BF16)


def _swa_prompt_kernel(sink_ref, q_ref, kp_ref, ko_ref, vp_ref, vo_ref, o_ref):
    j = pl.program_id(1)
    R = WINDOW
    qi = lax.broadcasted_iota(jnp.int32, (R, R), 0)
    kj = lax.broadcasted_iota(jnp.int32, (R, R), 1)
    own = kj <= qi
    distf = jnp.where(own, qi - kj, qi - kj + R).astype(F32)
    lane = lax.broadcasted_iota(jnp.int32, (R, LANES), 1)
    for u in range(SWA_BLOCKS):
        rs = slice(u * R, (u + 1) * R)
        q = q_ref[rs, :]
        k_prev = kp_ref[...] if u == 0 else ko_ref[(u - 1) * R:u * R, :]
        v_prev = vp_ref[...] if u == 0 else vo_ref[(u - 1) * R:u * R, :]
        kd = _dup_halves(jnp.concatenate([k_prev, ko_ref[rs, :]], axis=0))
        vd = _dup_halves(jnp.concatenate([v_prev, vo_ref[rs, :]], axis=0))
        valid = (own | (j > 0)) if u == 0 else None
        outs = []
        for g in range(N_KV):
            qs = _stack_group_queries(q, g)
            s = lax.dot_general(qs, kd[g].astype(BF16), (((1,), (1,)), ((), ())),
                                preferred_element_type=F32) * (HEAD_DIM ** -0.5)
            ps = []
            for hh in range(GROUP):
                h = GROUP * g + hh
                sink = sink_ref[h]
                rows = slice(hh * R, (hh + 1) * R)
                sh = jnp.where(own, s[rows, R:], s[rows, :R]) - _alibi_slope(h) * distf
                if valid is not None:
                    sh = jnp.where(valid, sh, NEG)
                mx = jnp.maximum(jnp.max(sh, axis=1, keepdims=True), sink)
                p = jnp.exp(sh - mx)
                p = p / (jnp.sum(p, axis=1, keepdims=True) + jnp.exp(sink - mx))
                ps.append(jnp.concatenate([jnp.where(own, 0.0, p), jnp.where(own, p, 0.0)], axis=1).astype(BF16))
            o = jnp.dot(jnp.concatenate(ps, axis=0), vd[g].astype(BF16), preferred_element_type=F32)
            for pp in range(GROUP // 2):
                outs.append(jnp.where(lane < HEAD_DIM, o[(2 * pp) * R:(2 * pp + 1) * R],
                                      o[(2 * pp + 1) * R:(2 * pp + 2) * R]))
        o_ref[rs, :] = jnp.concatenate(outs, axis=1).astype(o_ref.dtype)


def _swa_prompt(sinks, qa, ka, va, batch, seq):
    nb = seq // (WINDOW * SWA_BLOCKS)
    own = lambda w: pl.BlockSpec((WINDOW * SWA_BLOCKS, w), lambda b, j: (b * nb + j, 0))
    prev = lambda w: pl.BlockSpec(
        (WINDOW, w), lambda b, j: (b * nb * SWA_BLOCKS + jnp.maximum(j * SWA_BLOCKS - 1, 0), 0))
    return pl.pallas_call(
        _swa_prompt_kernel,
        grid=(batch, nb),
        in_specs=[pl.BlockSpec(memory_space=pltpu.SMEM), own(DM), prev(DKV), own(DKV), prev(DKV), own(DKV)],
        out_specs=own(DM),
        out_shape=jax.ShapeDtypeStruct((batch * seq, DM), BF16),
        compiler_params=_cparams(("arbitrary", "arbitrary"), VMEM_LIMIT),
        name="swa_prompt",
    )(sinks, qa, ka, ka, va, va)


def _swa_decode_kernel(n_tok, sink_ref, q_ref, kn_ref, vn_ref, kc_ref, vc_ref, o_ref, ko_ref, vo_ref):
    W = WINDOW
    L = LANES
    NB = L // n_tok
    for b in range(NB):
        ko_ref[b, 0:W - n_tok, :] = kc_ref[b, n_tok:W, :]
        ko_ref[b, W - n_tok:W, :] = kn_ref[b * n_tok:(b + 1) * n_tok, :]
        vo_ref[b, 0:W - n_tok, :] = vc_ref[b, n_tok:W, :]
        vo_ref[b, W - n_tok:W, :] = vn_ref[b * n_tok:(b + 1) * n_tok, :]
    q = q_ref[...]
    knd, vnd = _dup_halves(kn_ref[...]), _dup_halves(vn_ref[...])
    kcd = _dup_halves(kc_ref[...].reshape(NB * W, DKV))
    vcd = _dup_halves(vc_ref[...].reshape(NB * W, DKV))
    ri = lax.broadcasted_iota(jnp.int32, (L, L), 0)
    ci = lax.broadcasted_iota(jnp.int32, (L, L), 1)
    t_q = ri % n_tok
    dist_n = t_q - ci % n_tok
    valid_n = ((ri // n_tok) == (ci // n_tok)) & (dist_n >= 0)
    dist_c = t_q + W - ci
    valid_c = dist_c < WINDOW
    dnf = dist_n.astype(F32)
    dcf = dist_c.astype(F32)
    own1 = (lax.broadcasted_iota(jnp.int32, (L, NB * W), 0) // n_tok
            == lax.broadcasted_iota(jnp.int32, (L, NB * W), 1) // W)
    own4 = ((lax.broadcasted_iota(jnp.int32, (GROUP * L, NB * W), 0) % L) // n_tok
            == lax.broadcasted_iota(jnp.int32, (GROUP * L, NB * W), 1) // W)
    lane = lax.broadcasted_iota(jnp.int32, (L, LANES), 1)
    nt = (((1,), (1,)), ((), ()))
    outs = []
    for g in range(N_KV):
        qs = _stack_group_queries(q, g)
        sn = lax.dot_general(qs, knd[g].astype(BF16), nt, preferred_element_type=F32) * (HEAD_DIM ** -0.5)
        sc_all = lax.dot_general(qs, kcd[g].astype(BF16), nt, preferred_element_type=F32) * (HEAD_DIM ** -0.5)
        pcs, pns = [], []
        for hh in range(GROUP):
            h = GROUP * g + hh
            sink = sink_ref[h]
            slope = _alibi_slope(h)
            rows = slice(hh * L, (hh + 1) * L)
            blk = jnp.where(own1, sc_all[rows], 0.0)
            sc = blk[:, 0:W]
            for b in range(1, NB):
                sc = sc + blk[:, b * W:(b + 1) * W]
            shc = jnp.where(valid_c, sc - slope * dcf, NEG)
            shn = jnp.where(valid_n, sn[rows] - slope * dnf, NEG)
            mx = jnp.maximum(jnp.maximum(jnp.max(shc, axis=1, keepdims=True),
                                         jnp.max(shn, axis=1, keepdims=True)), sink)
            pc = jnp.exp(shc - mx)
            pn = jnp.exp(shn - mx)
            inv = 1.0 / (jnp.sum(pc, axis=1, keepdims=True) + jnp.sum(pn, axis=1, keepdims=True)
                         + jnp.exp(sink - mx))
            pcs.append(pc * inv)
            pns.append((pn * inv).astype(BF16))
        pc4 = jnp.concatenate(pcs, axis=0)
        p_bd = jnp.where(own4, jnp.concatenate([pc4] * NB, axis=1), 0.0).astype(BF16)
        o = (jnp.dot(p_bd, vcd[g].astype(BF16), preferred_element_type=F32)
             + jnp.dot(jnp.concatenate(pns, axis=0), vnd[g].astype(BF16), preferred_element_type=F32))
        for pp in range(GROUP // 2):
            outs.append(jnp.where(lane < HEAD_DIM, o[(2 * pp) * L:(2 * pp + 1) * L],
                                  o[(2 * pp + 1) * L:(2 * pp + 2) * L]))
    o_ref[...] = jnp.concatenate(outs, axis=1)


def _swa_decode(sinks, qa, ka, va, kc, vc, row0, n_tok):
    nb = kc.shape[0]
    g_nb = LANES // n_tok
    blk0 = row0 // LANES
    tokb = lambda w: pl.BlockSpec((LANES, w), lambda i: (blk0 + i, 0))
    cache = pl.BlockSpec((g_nb, WINDOW, DKV), lambda i: (i, 0, 0))
    return pl.pallas_call(
        functools.partial(_swa_decode_kernel, n_tok),
        grid=(nb // g_nb,),
        in_specs=[pl.BlockSpec(memory_space=pltpu.SMEM), tokb(DM), tokb(DKV), tokb(DKV), cache, cache],
        out_specs=(pl.BlockSpec((LANES, DM), lambda i: (i, 0)), cache, cache),
        out_shape=(jax.ShapeDtypeStruct((nb * n_tok, DM), F32),
                   jax.ShapeDtypeStruct(kc.shape, F32), jax.ShapeDtypeStruct(vc.shape, F32)),
        compiler_params=_cparams(("arbitrary",), VMEM_LIMIT),
        name="swa_decode",
    )(sinks, qa, ka, va, kc, vc)


def _outproj_router_kernel(n_ptiles, xp_ref, xs_ref, hmp_ref, hap_ref, hms_ref, has_ref, wom_ref, woa_ref,
                           g_ref, wrt_ref, br_ref,
                           x1_ref, xg_ref, meta_ref, cnt_ref):
    i = pl.program_id(0)
    is_p = i < n_ptiles
    x = jnp.where(is_p, xp_ref[...], xs_ref[...])
    hm = jnp.where(is_p, hmp_ref[...], hms_ref[...].astype(BF16))
    ha = jnp.where(is_p, hap_ref[...], has_ref[...].astype(BF16))
    x1 = (x + jnp.dot(hm, wom_ref[...], preferred_element_type=F32)
          + jnp.dot(ha, woa_ref[...], preferred_element_type=F32))
    x1_ref[...] = x1
    h2 = _rms(x1, g_ref[...])
    logits = lax.dot_general(wrt_ref[...], h2, (((1,), (1,)), ((), ())), precision=HIGHEST,
                             preferred_element_type=F32) + br_ref[...]
    eidx = lax.broadcasted_iota(jnp.int32, logits.shape, 0).astype(F32)
    work = logits
    vals, hots = [], []
    for _ in range(TOP_K):
        mv = jnp.max(work, axis=0, keepdims=True)
        sel = jnp.min(jnp.where(work == mv, eidx, float(N_EXPERTS)), axis=0, keepdims=True)
        hot = eidx == sel
        vals.append(mv)
        hots.append(hot)
        work = jnp.where(hot, -jnp.inf, work)
    es = [jnp.exp(v - vals[0]) for v in vals]
    tot = es[0] + es[1] + es[2] + es[3]
    gates = [e / tot for e in es]
    hot_all = jnp.where(hots[0] | hots[1] | hots[2] | hots[3], 1.0, 0.0)
    tm = logits.shape[1]
    su = (lax.broadcasted_iota(jnp.int32, (tm, tm), 0) < lax.broadcasted_iota(jnp.int32, (tm, tm), 1))
    cum = jnp.dot(hot_all.astype(BF16), su.astype(BF16), preferred_element_type=F32)
    cnt = jnp.sum(hot_all, axis=1, keepdims=True)
    cpad = (((cnt.astype(jnp.int32) + (SUBLANES - 1)) // SUBLANES) * SUBLANES).astype(F32)
    lower = (lax.broadcasted_iota(jnp.int32, (N_EXPERTS, N_EXPERTS), 0)
             > lax.broadcasted_iota(jnp.int32, (N_EXPERTS, N_EXPERTS), 1)).astype(F32)
    lstart = jnp.dot(lower, jnp.broadcast_to(cpad, (N_EXPERTS, LANES)), precision=HIGHEST,
                     preferred_element_type=F32)[:, 0:1]
    base = lstart + cum
    lpos = [jnp.sum(jnp.where(hot, base, 0.0), axis=0, keepdims=True) for hot in hots]
    lpi = [p.astype(jnp.int32) for p in lpos]
    r_iota = lax.broadcasted_iota(jnp.int32, (GROUP_R, tm), 0)
    sel01 = jnp.where(r_iota == lpi[0], 1.0, jnp.where(r_iota == lpi[1], 1.0, jnp.where(
        r_iota == lpi[2], 1.0, jnp.where(r_iota == lpi[3], 1.0, 0.0)))).astype(BF16)
    xg_ref[...] = jnp.dot(sel01, h2.astype(BF16), preferred_element_type=F32)
    meta_ref[...] = jnp.transpose(jnp.concatenate(gates + lpos, axis=0))
    cnt_ref[0] = jnp.broadcast_to(cnt, (N_EXPERTS, LANES))


def _outproj_router(xp, xs, hmp, hap, hms, has, wom, woa, g_ffn, wrt, br):
    tp, ts = xp.shape[0], xs.shape[0]
    n_pt, n_st = tp // ROUTE_T, ts // ROUTE_T
    t_all = tp + ts
    pblk = lambda w: pl.BlockSpec((ROUTE_T, w), lambda i: (jnp.minimum(i, n_pt - 1), 0))
    sblk = lambda w: pl.BlockSpec((ROUTE_T, w), lambda i: (jnp.maximum(i - n_pt, 0), 0))
    full = lambda a: pl.BlockSpec(a.shape, lambda i: (0,) * a.ndim)
    return pl.pallas_call(
        functools.partial(_outproj_router_kernel, n_pt),
        grid=(n_pt + n_st,),
        in_specs=[pblk(D_MODEL), sblk(D_MODEL), pblk(DM), pblk(DM), sblk(DM), sblk(DM),
                  full(wom), full(woa), full(g_ffn), full(wrt), full(br)],
        out_specs=(pl.BlockSpec((ROUTE_T, D_MODEL), lambda i: (i, 0)),
                   pl.BlockSpec((GROUP_R, D_MODEL), lambda i: (i, 0)),
                   pl.BlockSpec((ROUTE_T, 2 * TOP_K), lambda i: (i, 0)),
                   pl.BlockSpec((1, N_EXPERTS, LANES), lambda i: (i, 0, 0))),
        out_shape=(jax.ShapeDtypeStruct((t_all, D_MODEL), F32),
                   jax.ShapeDtypeStruct(((n_pt + n_st) * GROUP_R, D_MODEL), F32),
                   jax.ShapeDtypeStruct((t_all, 2 * TOP_K), F32),
                   jax.ShapeDtypeStruct((n_pt + n_st, N_EXPERTS, LANES), F32)),
        compiler_params=_cparams(("arbitrary",), VMEM_LIMIT),
        name="outproj_router",
    )(xp, xs, hmp, hap, hms, has, wom, woa, g_ffn, wrt, br)


def _expert_kernel(be_ref, na_ref, slot_ref, nxt_ref, ctab_ref, ctab1_ref, ctab2_ref, xg_ref, wgu_ref, bgu_ref,
                   wd_ref, bd_ref, o_ref, xbuf, wgu_f, wd_f, wgu_s, wd_s, xsem, wsem):
    i = pl.program_id(0)
    na = na_ref[0]

    def x_copies(tab_ref, slot):
        return [pltpu.make_async_copy(xg_ref.at[pl.ds(pl.multiple_of(tab_ref[0, 0, c], SUBLANES), SUBLANES)],
                                      xbuf.at[slot, pl.ds(c * SUBLANES, SUBLANES)], xsem.at[slot])
                for c in range(MOE_BM // SUBLANES)]

    def w_copies(e, slot):
        return [pltpu.make_async_copy(wgu_ref.at[e], wgu_f.at[slot], wsem.at[slot]),
                pltpu.make_async_copy(wd_ref.at[e], wd_f.at[slot], wsem.at[slot])]

    @pl.when(i == 0)
    def _():
        for cp in x_copies(ctab_ref, 0) + x_copies(ctab1_ref, 1) + w_copies(be_ref[0], 0):
            cp.start()

    @pl.when(i == na)
    def _():
        for cp in x_copies(ctab_ref, lax.rem(i, X_SLOTS)) + x_copies(ctab1_ref, lax.rem(i + 1, X_SLOTS)):
            cp.wait()

    @pl.when(i < na)
    def _():
        changed = (i == 0) | (be_ref[i] != be_ref[jnp.maximum(i - 1, 0)])
        wslot = slot_ref[i]
        xslot = lax.rem(i, X_SLOTS)

        @pl.when(changed)
        def _():
            for cp in w_copies(be_ref[i], wslot):
                cp.wait()

            @pl.when(nxt_ref[i] >= 0)
            def _():
                for cp in w_copies(nxt_ref[i], 1 - wslot):
                    cp.start(priority=1)

            wgu_s[...] = wgu_f[wslot].astype(BF16)
            wd_s[...] = wd_f[wslot].astype(BF16)

        for cp in x_copies(ctab_ref, xslot):
            cp.wait()

        x = xbuf[xslot].astype(BF16)
        gu = jnp.dot(x, wgu_s[...], preferred_element_type=F32) + bgu_ref[...]
        for cp in x_copies(ctab2_ref, lax.rem(i + 2, X_SLOTS)):
            cp.start()
        gate = jnp.minimum(gu[:, :D_FF], SWIGLU_LIMIT)
        up = jnp.clip(gu[:, D_FF:], -SWIGLU_LIMIT, SWIGLU_LIMIT)
        act = (up + 1.0) * (gate * _sigmoid(gate * SWIGLU_ALPHA))
        o_ref[...] = jnp.dot(act.astype(BF16), wd_s[...], preferred_element_type=F32) + bd_ref[...]


def _expert_ffn(block_exp, n_active, wslot, next_exp, ctab, xg, wgu, bgu, wd, bd):
    n_blocks = ctab.shape[0]
    n_rows = n_blocks * MOE_BM
    nch = MOE_BM // SUBLANES
    tab = lambda d: pl.BlockSpec((1, 1, nch), lambda i, *_: (jnp.minimum(i + d, n_blocks - 1), 0, 0),
                                 memory_space=pltpu.SMEM)
    blk_e = lambda i, be: be[jnp.minimum(i, n_blocks - 1)]
    grid_spec = pltpu.PrefetchScalarGridSpec(
        num_scalar_prefetch=4,
        grid=(n_blocks + 1,),
        in_specs=[tab(0), tab(1), tab(2),
                  pl.BlockSpec(memory_space=pl.ANY),
                  pl.BlockSpec(memory_space=pl.ANY),
                  pl.BlockSpec((None, 1, 2 * D_FF), lambda i, be, *_: (blk_e(i, be), 0, 0)),
                  pl.BlockSpec(memory_space=pl.ANY),
                  pl.BlockSpec((None, 1, D_MODEL), lambda i, be, *_: (blk_e(i, be), 0, 0))],
        out_specs=pl.BlockSpec((MOE_BM, D_MODEL), lambda i, be, na, *_: (jnp.minimum(i, na[0] - 1), 0)),
        scratch_shapes=[pltpu.VMEM((X_SLOTS, MOE_BM, D_MODEL), F32),
                        pltpu.VMEM((2, D_MODEL, 2 * D_FF), F32), pltpu.VMEM((2, D_FF, D_MODEL), F32),
                        pltpu.VMEM((D_MODEL, 2 * D_FF), BF16), pltpu.VMEM((D_FF, D_MODEL), BF16),
                        pltpu.SemaphoreType.DMA((X_SLOTS,)), pltpu.SemaphoreType.DMA((2,))],
    )
    return pl.pallas_call(
        _expert_kernel,
        grid_spec=grid_spec,
        out_shape=jax.ShapeDtypeStruct((n_rows, D_MODEL), F32),
        compiler_params=_cparams(("arbitrary",), VMEM_LIMIT),
        name="moe_experts",
    )(block_exp, n_active, wslot, next_exp, ctab, ctab, ctab, xg, wgu, bgu, wd, bd)


def _combine_kernel(n_ptiles, n_tiles, ctab_ref, ctab1_ref, ctab2_ref, outs_ref, x1_ref, meta_ref, gf_ref,
                    yp_ref, ys_ref, obuf, sem):
    i = pl.program_id(0)

    def copies(tab_ref, s):
        return [pltpu.make_async_copy(outs_ref.at[pl.ds(pl.multiple_of(tab_ref[0, 0, c], SUBLANES), SUBLANES)],
                                      obuf.at[s, pl.ds(c * SUBLANES, SUBLANES)], sem.at[s])
                for c in range(GROUP_R // SUBLANES)]

    @pl.when(i == 0)
    def _():
        for cp in copies(ctab_ref, 0) + copies(ctab1_ref, 1):
            cp.start()

    @pl.when(i == n_tiles)
    def _():
        for cp in copies(ctab_ref, lax.rem(i, X_SLOTS)) + copies(ctab1_ref, lax.rem(i + 1, X_SLOTS)):
            cp.wait()

    @pl.when(i < n_tiles)
    def _():
        _combine_tile(i, n_ptiles, copies, ctab_ref, ctab2_ref, x1_ref, meta_ref, gf_ref, yp_ref, ys_ref, obuf)


def _combine_tile(i, n_ptiles, copies, ctab_ref, ctab2_ref, x1_ref, meta_ref, gf_ref, yp_ref, ys_ref, obuf):
    slot = lax.rem(i, X_SLOTS)
    for cp in copies(ctab_ref, slot):
        cp.wait()
    for cp in copies(ctab2_ref, lax.rem(i + 2, X_SLOTS)):
        cp.start()

    meta = meta_ref[...]
    tm = meta.shape[0]
    r_iota = lax.broadcasted_iota(jnp.int32, (tm, GROUP_R), 1)
    lp = [meta[:, TOP_K + k:TOP_K + k + 1].astype(jnp.int32) for k in range(TOP_K)]
    gk = [meta[:, k:k + 1] for k in range(TOP_K)]
    gsel = jnp.where(r_iota == lp[0], gk[0], jnp.where(r_iota == lp[1], gk[1], jnp.where(
        r_iota == lp[2], gk[2], jnp.where(r_iota == lp[3], gk[3], 0.0))))
    sel01 = jnp.where(gsel != 0.0, 1.0, 0.0).astype(BF16)
    rg_row = jnp.sum(gsel, axis=0, keepdims=True)
    rg_col = jnp.transpose(jnp.broadcast_to(rg_row, (SUBLANES, GROUP_R)))[:, 0:1]
    og = (obuf[slot] * rg_col).astype(BF16)
    acc = x1_ref[...] + jnp.dot(sel01, og, preferred_element_type=F32)
    y = _rms(acc, gf_ref[...])

    @pl.when(i < n_ptiles)
    def _():
        yp_ref[...] = y

    @pl.when(i >= n_ptiles)
    def _():
        ys_ref[...] = y


def _combine(ctab, outs, x1, meta, g_final, tp, ts):
    n_pt, n_st = tp // ROUTE_T, ts // ROUTE_T
    n = n_pt + n_st
    nch = GROUP_R // SUBLANES
    last = lambda i: jnp.minimum(i, n - 1)
    tab = lambda d: pl.BlockSpec((1, 1, nch), lambda i: (last(i + d), 0, 0), memory_space=pltpu.SMEM)
    return pl.pallas_call(
        functools.partial(_combine_kernel, n_pt, n),
        grid=(n + 1,),
        in_specs=[tab(0), tab(1), tab(2),
                  pl.BlockSpec(memory_space=pl.ANY),
                  pl.BlockSpec((ROUTE_T, D_MODEL), lambda i: (last(i), 0)),
                  pl.BlockSpec((ROUTE_T, 2 * TOP_K), lambda i: (last(i), 0)),
                  pl.BlockSpec((1, D_MODEL), lambda i: (0, 0))],
        out_specs=(pl.BlockSpec((ROUTE_T, D_MODEL), lambda i: (jnp.minimum(i, n_pt - 1), 0)),
                   pl.BlockSpec((ROUTE_T, D_MODEL), lambda i: (jnp.maximum(last(i) - n_pt, 0), 0))),
        out_shape=(jax.ShapeDtypeStruct((tp, D_MODEL), F32), jax.ShapeDtypeStruct((ts, D_MODEL), F32)),
        scratch_shapes=[pltpu.VMEM((X_SLOTS, GROUP_R, D_MODEL), F32), pltpu.SemaphoreType.DMA((X_SLOTS,))],
        compiler_params=_cparams(("arbitrary",), VMEM_LIMIT),
        name="moe_combine",
    )(ctab, ctab, ctab, outs, x1, meta, g_final)


def kernel(x_prompt, x_sample, cache_swa_k, cache_swa_v, state_mlstm_c, state_mlstm_n, state_mlstm_m,
           g_mix, w_in, b_igate, b_fgate, g_head, attn_sinks, w_out, g_ffn, w_router, b_router,
           w_gate_up, b_gate_up, w_down, b_down, g_final):
    assert w_in.shape[0] == 1, "single-layer problem"
    B, S, _ = x_prompt.shape
    Bd, Tn, _ = x_sample.shape
    tp, ts = B * S, Bd * Tn
    t_all = tp + ts
    xp = x_prompt.reshape(tp, D_MODEL)
    xs = x_sample.reshape(ts, D_MODEL)

    w = w_in[0]
    o = np.cumsum([0, DM, DM, DM, DM, N_HEADS, N_HEADS, DM, DKV, DKV])
    col = lambda a: w[:, int(o[a]):int(o[a + 1])]
    wgates = jnp.concatenate([col(4), col(5)], axis=1)
    w1 = jnp.concatenate([col(1), col(6), col(7), col(8), jnp.pad(wgates, ((0, 0), (0, LANES - 2 * N_HEADS)))],
                         axis=1).astype(BF16)
    wkt = jnp.concatenate([col(0), col(2), col(3), wgates], axis=1).T.astype(BF16)
    bg = jnp.concatenate([b_igate[0], b_fgate[0]]).astype(F32)
    bcol = jnp.pad(bg, (0, LANES - 2 * N_HEADS)).reshape(1, LANES)
    brow = bg.reshape(2 * N_HEADS, 1)

    km, qa, ka, va, qmt, vmt, omt, gcol, grow = _inproj(xp, xs, g_mix[0].reshape(1, D_MODEL), w1, wkt, bcol, brow)

    gh = g_head[0].astype(F32)
    sinks = attn_sinks[0].astype(F32)

    hm_p, ctp, m_p = _mlstm_prompt(km, qmt, vmt, omt, gcol, grow,
                                   jnp.broadcast_to(gh.reshape(DM, 1), (DM, LANES)), B, S)
    ha_p = _swa_prompt(sinks, qa, ka, va, B, S)

    hm_s, c_s, n_s, mt_s = _mlstm_sample(km, qmt, vmt, omt, gcol, grow, state_mlstm_c[0], state_mlstm_n[0],
                                         state_mlstm_m[0], gh.reshape(1, DM), tp, Tn)
    n_s = n_s.reshape(Bd, N_HEADS, HEAD_DIM)
    m_s = mt_s[:, :, :LANES // Tn].transpose(0, 2, 1).reshape(Bd, N_HEADS)
    ha_s, k_s, v_s = _swa_decode(sinks, qa, ka, va, cache_swa_k[0].reshape(Bd, WINDOW, DKV),
                                 cache_swa_v[0].reshape(Bd, WINDOW, DKV), tp, Tn)

    wo = w_out[0].astype(BF16)
    x1, xg, meta, cnt = _outproj_router(
        xp, xs, hm_p, ha_p, hm_s, ha_s, wo[:DM], wo[DM:], g_ffn[0].reshape(1, D_MODEL),
        w_router[0].T, b_router[0].reshape(N_EXPERTS, 1))

    i32 = jnp.int32
    n_tiles = t_all // ROUTE_T
    max_rows = t_all * TOP_K + n_tiles * N_EXPERTS * (SUBLANES - 1) + N_EXPERTS * (MOE_BM - 1)
    n_blocks = -(-max_rows // MOE_BM)
    cpad = (cnt[:, :, 0].astype(i32) + (SUBLANES - 1)) // SUBLANES * SUBLANES
    lstart = jnp.cumsum(cpad, axis=1) - cpad
    goff = jnp.cumsum(cpad, axis=0) - cpad
    padded = (jnp.sum(cpad, axis=0) + MOE_BM - 1) // MOE_BM * MOE_BM
    pad_end = jnp.cumsum(padded)
    seg_begin = (pad_end - padded)[None, :] + goff
    n_active = (pad_end[-1] // MOE_BM).astype(i32)
    blk = jnp.minimum(jnp.arange(n_blocks, dtype=i32), n_active - 1)
    block_exp = jnp.minimum(jnp.sum((pad_end[None, :] <= (blk * MOE_BM)[:, None]).astype(i32), axis=1),
                            N_EXPERTS - 1)
    e_ids = jnp.arange(N_EXPERTS, dtype=i32)
    nonempty = padded > 0
    nxt_e = jnp.min(jnp.where((e_ids[None, :] > e_ids[:, None]) & nonempty[None, :], e_ids[None, :], N_EXPERTS),
                    axis=1)
    nxt_e = jnp.where(nxt_e == N_EXPERTS, -1, nxt_e)
    ord_e = jnp.cumsum(nonempty.astype(i32)) - 1
    be_hot = block_exp[:, None] == e_ids[None, :]
    next_exp = jnp.sum(jnp.where(be_hot, nxt_e[None, :], 0), axis=1).astype(i32)
    wslot = (jnp.sum(jnp.where(be_hot, ord_e[None, :], 0), axis=1) % 2).astype(i32)

    seg_src = jnp.arange(n_tiles, dtype=i32)[:, None] * GROUP_R + lstart
    sb, sl, ss = seg_begin.reshape(-1), cpad.reshape(-1), seg_src.reshape(-1)
    rc = jnp.arange(n_blocks * MOE_BM // SUBLANES, dtype=i32)[:, None] * SUBLANES
    inseg = (sb[None, :] <= rc) & (rc < (sb + sl)[None, :])
    ctab_e = jnp.where(jnp.any(inseg, axis=1), jnp.sum(jnp.where(inseg, (ss - sb)[None, :] + rc, 0), axis=1),
                       GROUP_R - SUBLANES)
    lr = jnp.arange(GROUP_R // SUBLANES, dtype=i32)[None, :, None] * SUBLANES
    inl = (lstart[:, None, :] <= lr) & (lr < (lstart + cpad)[:, None, :])
    ctab_c = jnp.sum(jnp.where(inl, (seg_begin - lstart)[:, None, :] + lr, 0), axis=2)

    outs = _expert_ffn(block_exp, n_active.reshape(1), wslot, next_exp,
                       ctab_e.astype(i32).reshape(n_blocks, 1, MOE_BM // SUBLANES), xg, w_gate_up[0],
                       b_gate_up[0].reshape(N_EXPERTS, 1, 2 * D_FF), w_down[0],
                       b_down[0].reshape(N_EXPERTS, 1, D_MODEL))
    y_p, y_s = _combine(ctab_c.astype(i32).reshape(n_tiles, 1, GROUP_R // SUBLANES), outs, x1, meta,
                        g_final.reshape(1, D_MODEL), tp, ts)

    kv_tail = lambda a: jnp.concatenate([a[(b + 1) * S - WINDOW:(b + 1) * S] for b in range(B)], axis=0).reshape(
        1, B, WINDOW, N_KV, HEAD_DIM)
    c_e = ctp[:, :, :HEAD_DIM, :HEAD_DIM]
    c_o = ctp[:, :, HEAD_DIM:LANES, HEAD_DIM:]
    c_p = jnp.swapaxes(jnp.stack([c_e, c_o], axis=2), -1, -2).reshape(B, N_HEADS, HEAD_DIM, HEAD_DIM)
    n_p = (ctp[:, :, LANES:LANES + 2, :HEAD_DIM] + ctp[:, :, LANES:LANES + 2, HEAD_DIM:]).reshape(B, N_HEADS, HEAD_DIM)
    return (y_p.reshape(B, S, D_MODEL), y_s.reshape(Bd, Tn, D_MODEL),
            kv_tail(ka), kv_tail(va), c_p[None], n_p[None], m_p[:, :, 0][None],
            k_s.reshape(Bd, WINDOW, N_KV, HEAD_DIM)[None], v_s.reshape(Bd, WINDOW, N_KV, HEAD_DIM)[None],
            c_s[None], n_s[None], m_s[None])
```

```python
import functools

import jax
import jax.numpy as jnp
import numpy as np
from jax import lax
from jax.experimental import pallas as pl
from jax.experimental.pallas import tpu as pltpu

F32 = jnp.float32
BF16 = jnp.bfloat16
HIGHEST = lax.Precision.HIGHEST

D_MODEL = 1024
HEAD_DIM = 64
N_HEADS = 8
N_PAIRS = N_HEADS // 2
N_KV = 2
GROUP = N_HEADS // N_KV
WINDOW = 128
N_EXPERTS = 32
TOP_K = 4
D_FF = 1024
SWIGLU_LIMIT = 7.0
SWIGLU_ALPHA = 1.702
RMS_EPS = 1e-5
DM = N_HEADS * HEAD_DIM
DKV = N_KV * HEAD_DIM
NEG = -1e30

LANES = 128
SUBLANES = 8
VMEM_LIMIT = 56 * 1024 * 1024

TM = 512
ROUTE_T = 512
MLSTM_TL = 512
MLSTM_L = 128
SWA_BLOCKS = 8
MOE_BM = 256
X_SLOTS = 3
GROUP_R = -(-(TOP_K * ROUTE_T + N_EXPERTS * (SUBLANES - 1) + SUBLANES) // LANES) * LANES
SORT_CHUNKS = 3


def _cparams(sem, vmem=None):
    return pltpu.CompilerParams(dimension_semantics=sem, vmem_limit_bytes=vmem)


def _rms(x, g):
    return x * lax.rsqrt(jnp.mean(x * x, axis=-1, keepdims=True) + RMS_EPS) * g


def _log_sigmoid(z):
    return jnp.minimum(z, 0.0) - jnp.log(1.0 + jnp.exp(-jnp.abs(z)))


def _sigmoid(z):
    return 1.0 / (1.0 + jnp.exp(-z))


def _inproj_kernel(n_ptiles, xp_ref, xs_ref, g_ref, w1_ref, wt_ref, bcol_ref, brow_ref,
                   km_ref, qa_ref, ka_ref, va_ref, qmt_ref, vmt_ref, omt_ref, gcol_ref, grow_ref):
    i = pl.program_id(0)
    x = jnp.where(i < n_ptiles, xp_ref[...], xs_ref[...])
    h = _rms(x, g_ref[...]).astype(BF16)
    main = jnp.dot(h, w1_ref[...], preferred_element_type=F32)
    km_ref[...] = main[:, 0:DM] * (HEAD_DIM ** -0.5)
    qa_ref[...] = main[:, DM:2 * DM]
    ka_ref[...] = main[:, 2 * DM:2 * DM + DKV]
    va_ref[...] = main[:, 2 * DM + DKV:2 * DM + 2 * DKV]
    t = lax.dot_general(wt_ref[...], h, (((1,), (1,)), ((), ())), preferred_element_type=F32)
    qmt_ref[...] = t[0:DM]
    vmt_ref[...] = t[DM:2 * DM]
    omt_ref[...] = t[2 * DM:3 * DM]
    zc = main[:, 2 * DM + 2 * DKV:] + bcol_ref[...]
    lane = lax.broadcasted_iota(jnp.int32, zc.shape, 1)
    gcol_ref[...] = jnp.where(lane < N_HEADS, zc, _log_sigmoid(zc))
    zr = t[3 * DM:] + brow_ref[...]
    row = lax.broadcasted_iota(jnp.int32, zr.shape, 0)
    grow_ref[...] = jnp.where(row < N_HEADS, zr, _log_sigmoid(zr))


def _inproj(xp, xs, g_mix, w1, wkt, bcol, brow):
    tp, ts = xp.shape[0], xs.shape[0]
    n_pt, n_st = tp // TM, ts // TM
    t_all = tp + ts
    tok = lambda w: pl.BlockSpec((TM, w), lambda i: (i, 0))
    tr = lambda r: pl.BlockSpec((r, TM), lambda i: (0, i))
    full = lambda a: pl.BlockSpec(a.shape, lambda i: (0,) * a.ndim)
    out_shape = (
        jax.ShapeDtypeStruct((t_all, DM), F32), jax.ShapeDtypeStruct((t_all, DM), F32),
        jax.ShapeDtypeStruct((t_all, DKV), F32), jax.ShapeDtypeStruct((t_all, DKV), F32),
        jax.ShapeDtypeStruct((DM, t_all), F32), jax.ShapeDtypeStruct((DM, t_all), F32),
        jax.ShapeDtypeStruct((DM, t_all), F32),
        jax.ShapeDtypeStruct((t_all, LANES), F32), jax.ShapeDtypeStruct((2 * N_HEADS, t_all), F32),
    )
    return pl.pallas_call(
        functools.partial(_inproj_kernel, n_pt),
        grid=(n_pt + n_st,),
        in_specs=[
            pl.BlockSpec((TM, D_MODEL), lambda i: (jnp.minimum(i, n_pt - 1), 0)),
            pl.BlockSpec((TM, D_MODEL), lambda i: (jnp.maximum(i - n_pt, 0), 0)),
            full(g_mix), full(w1), full(wkt), full(bcol), full(brow),
        ],
        out_specs=(tok(DM), tok(DM), tok(DKV), tok(DKV), tr(DM), tr(DM), tr(DM),
                   tok(LANES), tr(2 * N_HEADS)),
        out_shape=out_shape,
        compiler_params=_cparams(("arbitrary",), VMEM_LIMIT),
        name="inproj",
    )(xp, xs, g_mix, w1, wkt, bcol, brow)


CT_ROWS = LANES + 2 * SUBLANES


def _cumsum_rows(x, n):
    row = lax.broadcasted_iota(jnp.int32, x.shape, 0)
    sh = 1
    while sh < n:
        x = x + jnp.where(row >= sh, pltpu.roll(x, sh, axis=0), 0.0)
        sh *= 2
    return x


def _mlstm_prompt_kernel(km_ref, qmt_ref, vmt_ref, omt_ref, gcol_ref, grow_ref, ghr_ref,
                         hm_ref, ct_ref, m_ref, ct_s, m_s):
    j = pl.program_id(1)
    L = MLSTM_L
    assert L == LANES

    @pl.when(j == 0)
    def _():
        ct_s[...] = jnp.zeros_like(ct_s)
        m_s[...] = jnp.zeros_like(m_s)

    si = lax.broadcasted_iota(jnp.int32, (L, L), 0)
    ti = lax.broadcasted_iota(jnp.int32, (L, L), 1)
    causal_t = si <= ti
    upper = jnp.where(causal_t, 1.0, 0.0)
    rows_c = lax.broadcasted_iota(jnp.int32, (CT_ROWS, 1), 0)
    rmask_e = (rows_c < HEAD_DIM) | (rows_c == LANES)
    rmask_o = ((rows_c >= HEAD_DIM) & (rows_c < LANES)) | (rows_c == LANES + 1)
    rows_e = lax.broadcasted_iota(jnp.int32, (LANES, 1), 0) < HEAD_DIM
    cols_e = lax.broadcasted_iota(jnp.int32, (1, LANES), 1) < HEAD_DIM
    bdt_mask = (rmask_e & cols_e) | (rmask_o & (~cols_e))
    ones_rows = jnp.where(lax.broadcasted_iota(jnp.int32, (CT_ROWS - LANES, L), 0) < 2, 1.0, 0.0)

    for c in range(MLSTM_TL // L):
        sl = slice(c * L, (c + 1) * L)
        grow = grow_ref[:, sl]
        i_row = grow[0:N_HEADS]
        b_row = jnp.dot(grow[N_HEADS:2 * N_HEADS], upper, precision=HIGHEST,
                        preferred_element_type=F32)
        gc = gcol_ref[sl, :]
        bc_all = _cumsum_rows(gc, L)
        for p in range(N_PAIRS):
            ls = slice(p * LANES, (p + 1) * LANES)
            k2 = km_ref[sl, ls]
            qt2 = qmt_ref[ls, sl]
            qt_e = jnp.where(rows_e, qt2, 0.0).astype(BF16)
            qt_o = jnp.where(rows_e, 0.0, qt2).astype(BF16)
            st2 = jnp.dot(k2.astype(BF16), jnp.concatenate([qt_e, qt_o], axis=1),
                          preferred_element_type=F32)
            ct = ct_s[p]
            rqt = jnp.dot(ct.astype(BF16), qt2.astype(BF16), preferred_element_type=F32)
            pts, mts, inters, wreps, decays, mnews = [], [], [], [], [], []
            for hh in range(2):
                h = 2 * p + hh
                cvec = jnp.broadcast_to(gc[:, h:h + 1] - bc_all[:, N_HEADS + h:N_HEADS + h + 1], (L, L))
                brow = b_row[h:h + 1, :]
                logdt = jnp.where(causal_t, cvec + brow, NEG)
                m_prev = m_s[h:h + 1, 0:1]
                m_inter = m_prev + brow
                m_t = jnp.maximum(m_inter, jnp.max(logdt, axis=0, keepdims=True))
                pts.append((st2[:, hh * L:(hh + 1) * L] * jnp.exp(logdt - m_t)).astype(BF16))
                mts.append(m_t)
                inters.append(jnp.exp(m_inter - m_t))
                m_new = m_t[:, L - 1:L]
                b_last = brow[:, L - 1:L]
                decays.append(jnp.exp(m_prev + b_last - m_new))
                wreps.append(jnp.exp(cvec + (b_last - m_new)))
                mnews.append(m_new)
            vext = jnp.concatenate([vmt_ref[ls, sl], ones_rows], axis=0)
            lhs = jnp.concatenate([jnp.where(rmask_e, vext, 0.0), jnp.where(rmask_o, vext, 0.0)],
                                  axis=1).astype(BF16)
            rt = jnp.dot(lhs, jnp.concatenate(pts, axis=0), preferred_element_type=F32)
            ndt = rt + jnp.where(rmask_e, inters[0], inters[1]) * rqt
            den = jnp.where(rows_e, ndt[LANES:LANES + 1], ndt[LANES + 1:LANES + 2])
            mt2 = jnp.where(rows_e, mts[0], mts[1])
            hvt = ndt[0:LANES] / jnp.maximum(jnp.abs(den), jnp.exp(-mt2))
            sq = hvt * hvt
            ms = jnp.where(rows_e, jnp.sum(sq[0:HEAD_DIM], axis=0, keepdims=True),
                           jnp.sum(sq[HEAD_DIM:LANES], axis=0, keepdims=True)) * (1.0 / HEAD_DIM)
            yt = hvt * lax.rsqrt(ms + RMS_EPS) * ghr_ref[ls, :] * _sigmoid(omt_ref[ls, sl])
            hm_ref[sl, ls] = jnp.transpose(yt).astype(hm_ref.dtype)
            kw = (k2 * jnp.where(cols_e, wreps[0], wreps[1])).astype(BF16)
            upd = jnp.dot(vext.astype(BF16), kw, preferred_element_type=F32)
            ct_s[p] = jnp.where(rmask_e, decays[0], decays[1]) * ct + jnp.where(bdt_mask, upd, 0.0)
            for hh in range(2):
                h = 2 * p + hh
                m_s[h:h + 1, :] = jnp.broadcast_to(mnews[hh], (1, LANES))

    @pl.when(j == pl.num_programs(1) - 1)
    def _():
        ct_ref[0] = ct_s[...]
        m_ref[0] = m_s[...]


def _mlstm_prompt(km, qmt, vmt, omt, gcol, grow, ghr, batch, seq):
    nt = seq // MLSTM_TL
    tokb = lambda w: pl.BlockSpec((MLSTM_TL, w), lambda b, j: (b * nt + j, 0))
    rowb = lambda r: pl.BlockSpec((r, MLSTM_TL), lambda b, j: (0, b * nt + j))
    return pl.pallas_call(
        _mlstm_prompt_kernel,
        grid=(batch, nt),
        in_specs=[tokb(DM), rowb(DM), rowb(DM), rowb(DM), tokb(LANES), rowb(2 * N_HEADS),
                  pl.BlockSpec((DM, LANES), lambda b, j: (0, 0))],
        out_specs=(tokb(DM),
                   pl.BlockSpec((1, N_PAIRS, CT_ROWS, LANES), lambda b, j: (b, 0, 0, 0)),
                   pl.BlockSpec((1, N_HEADS, LANES), lambda b, j: (b, 0, 0))),
        out_shape=(jax.ShapeDtypeStruct((batch * seq, DM), BF16),
                   jax.ShapeDtypeStruct((batch, N_PAIRS, CT_ROWS, LANES), F32),
                   jax.ShapeDtypeStruct((batch, N_HEADS, LANES), F32)),
        scratch_shapes=[pltpu.VMEM((N_PAIRS, CT_ROWS, LANES), F32), pltpu.VMEM((N_HEADS, LANES), F32)],
        compiler_params=_cparams(("arbitrary", "arbitrary"), VMEM_LIMIT),
        name="mlstm_prompt",
    )(km, qmt, vmt, omt, gcol, grow, ghr)


def _mlstm_sample_kernel(n_tok, km_ref, qmt_ref, vmt_ref, omt_ref, gcol_ref, grow_ref, c0_ref, n0_ref, m0_ref,
                         m0t_ref, gh_ref, hm_ref, c_ref, n_ref, mt_ref):
    L = LANES
    NB = L // n_tok
    ti = lax.broadcasted_iota(jnp.int32, (L, L), 0)
    si = lax.broadcasted_iota(jnp.int32, (L, L), 1)
    same = (ti // n_tok) == (si // n_tok)
    causal = same & (ti >= si)
    useg = jnp.where(same & (ti <= si), 1.0, 0.0)
    slast = jnp.where(same & (ti % n_tok == n_tok - 1), 1.0, 0.0)
    expand = jnp.where(ti // n_tok == si, 1.0, 0.0)
    expand_t = jnp.where(ti == si // n_tok, 1.0, 0.0)
    pick = jnp.where((ti // n_tok == si) & (ti % n_tok == n_tok - 1), 1.0, 0.0)
    hdot = lambda a, b: jnp.dot(a, b, precision=HIGHEST, preferred_element_type=F32)

    lane128 = lax.broadcasted_iota(jnp.int32, (L, LANES), 1)
    even128 = lane128 < HEAD_DIM
    lane256 = lax.broadcasted_iota(jnp.int32, (1, 2 * LANES), 1)
    cols_e = (lane256 < HEAD_DIM) | (lane256 == LANES)
    cols_o = ((lane256 >= HEAD_DIM) & (lane256 < LANES)) | (lane256 == LANES + 1)
    rows_e = lax.broadcasted_iota(jnp.int32, (LANES, 1), 0) < HEAD_DIM
    ones_cols = jnp.where(lane128 < 2, 1.0, 0.0)
    bo_r = lax.broadcasted_iota(jnp.int32, (LANES, LANES), 0) // HEAD_DIM
    bo_c = lax.broadcasted_iota(jnp.int32, (LANES, LANES), 1) // HEAD_DIM
    block_ones = jnp.where(bo_r == bo_c, 1.0, 0.0)
    W = NB * LANES
    rb = lax.broadcasted_iota(jnp.int32, (L, W), 0)
    cb = lax.broadcasted_iota(jnp.int32, (L, W), 1)
    own_block = (rb // n_tok) == (cb // LANES)
    bd_tiled = (rb // HEAD_DIM) == ((cb % LANES) // HEAD_DIM)

    grow = grow_ref[...]
    i_row = grow[0:N_HEADS]
    b_row = hdot(grow[N_HEADS:2 * N_HEADS], useg)
    b_last = hdot(b_row, slast)
    a_row = b_last - b_row + i_row
    pos = lax.broadcasted_iota(jnp.int32, a_row.shape, 1) % n_tok
    pm = a_row
    sh = 1
    while sh < n_tok:
        pm = jnp.where(pos >= sh, jnp.maximum(pm, pltpu.roll(pm, sh, axis=1)), pm)
        sh *= 2
    m_carry = hdot(jnp.concatenate([m0t_ref[0], b_row], axis=1), jnp.concatenate([expand_t, slast], axis=0))
    m_new_row = jnp.maximum(m_carry, hdot(pm, slast))
    decay_row = jnp.exp(m_carry - m_new_row)
    w_row = jnp.exp(a_row - m_new_row)
    mt_ref[0] = hdot(m_new_row, pick)
    decay_bh = hdot(decay_row, pick)
    decay_hb = jnp.transpose(decay_bh)[0:NB]

    bc_all = gcol_ref[...]
    rowpos = lax.broadcasted_iota(jnp.int32, bc_all.shape, 0) % n_tok
    sh = 1
    while sh < n_tok:
        bc_all = bc_all + jnp.where(rowpos >= sh, pltpu.roll(bc_all, sh, axis=0), 0.0)
        sh *= 2
    pad_rows = lambda a: jnp.concatenate([a, jnp.zeros((L - NB, a.shape[1]), F32)], axis=0)
    m0_col = hdot(expand, pad_rows(m0_ref[...]))
    rowv_all = i_row - b_row

    for p in range(N_PAIRS):
        ls = slice(p * LANES, (p + 1) * LANES)
        q2f = jnp.transpose(qmt_ref[ls, :])
        q2 = q2f.astype(BF16)
        kt2 = jnp.transpose(km_ref[:, ls])
        v2 = jnp.transpose(vmt_ref[ls, :])
        vext = jnp.concatenate([v2, ones_cols], axis=1)
        kt_e = jnp.where(rows_e, kt2, 0.0).astype(BF16)
        kt_o = jnp.where(rows_e, 0.0, kt2).astype(BF16)
        s2 = jnp.dot(q2, jnp.concatenate([kt_e, kt_o], axis=1), preferred_element_type=F32)
        ps, mts, inters = [], [], []
        for hh in range(2):
            h = 2 * p + hh
            bcol = bc_all[:, N_HEADS + h:N_HEADS + h + 1]
            logd = jnp.where(causal, bcol + rowv_all[h:h + 1, :], NEG)
            m_inter = m0_col[:, h:h + 1] + bcol
            m_t = jnp.maximum(m_inter, jnp.max(logd, axis=1, keepdims=True))
            ps.append((s2[:, hh * L:(hh + 1) * L] * jnp.exp(logd - m_t)).astype(BF16))
            mts.append(m_t)
            inters.append(jnp.exp(m_inter - m_t))
        vstack = jnp.concatenate([jnp.where(cols_e, vext, 0.0), jnp.where(cols_o, vext, 0.0)],
                                 axis=0).astype(BF16)
        r = jnp.dot(jnp.concatenate(ps, axis=1), vstack, preferred_element_type=F32)
        zero = jnp.zeros((HEAD_DIM, HEAD_DIM), F32)
        cstack = jnp.concatenate(
            [jnp.concatenate([jnp.concatenate([c0_ref[b, 2 * p], zero], axis=1),
                              jnp.concatenate([zero, c0_ref[b, 2 * p + 1]], axis=1)], axis=0)
             for b in range(NB)], axis=1)
        rq_all = jnp.where(own_block, jnp.dot(q2, cstack.astype(BF16), preferred_element_type=F32), 0.0)
        rq = rq_all[:, 0:LANES]
        for b in range(1, NB):
            rq = rq + rq_all[:, b * LANES:(b + 1) * LANES]
        n_rows = hdot(expand, pad_rows(n0_ref[:, ls]))
        qn = hdot(q2f * n_rows, block_ones)
        inter2 = jnp.where(even128, inters[0], inters[1])
        num = r[:, 0:LANES] + inter2 * rq
        den = jnp.where(even128, r[:, LANES:LANES + 1], r[:, LANES + 1:LANES + 2]) + inter2 * qn
        mt2 = jnp.where(even128, mts[0], mts[1])
        hv = num / jnp.maximum(jnp.abs(den), jnp.exp(-mt2))
        ms = hdot(hv * hv, block_ones) * (1.0 / HEAD_DIM)
        y = hv * lax.rsqrt(ms + RMS_EPS) * gh_ref[:, ls] * _sigmoid(jnp.transpose(omt_ref[ls, :]))
        hm_ref[:, ls] = y.astype(hm_ref.dtype)
        kw = kt2 * jnp.where(rows_e, w_row[2 * p:2 * p + 1, :], w_row[2 * p + 1:2 * p + 2, :])
        vbd = jnp.where(own_block, jnp.concatenate([v2] * NB, axis=1), 0.0).astype(BF16)
        upd = jnp.dot(kw.astype(BF16), vbd, preferred_element_type=F32)
        upd = jnp.where(bd_tiled, upd, 0.0)
        for b in range(NB):
            bs = slice(b * LANES, (b + 1) * LANES)
            dec_b = jnp.where(rows_e, decay_hb[b:b + 1, 2 * p:2 * p + 1], decay_hb[b:b + 1, 2 * p + 1:2 * p + 2])
            cnew = dec_b * cstack[:, bs] + upd[:, bs]
            c_ref[b, 2 * p] = cnew[0:HEAD_DIM, 0:HEAD_DIM]
            c_ref[b, 2 * p + 1] = cnew[HEAD_DIM:LANES, HEAD_DIM:LANES]
        nsum = jnp.transpose(hdot(kw, expand))[0:NB]
        dec_n = jnp.where(lax.broadcasted_iota(jnp.int32, (NB, LANES), 1) < HEAD_DIM,
                          decay_hb[:, 2 * p:2 * p + 1], decay_hb[:, 2 * p + 1:2 * p + 2])
        n_ref[:, ls] = dec_n * n0_ref[:, ls] + nsum


def _mlstm_sample(km, qmt, vmt, omt, gcol, grow, c0, n0, m0, gh, row0, n_tok):
    nb = c0.shape[0]
    g_nb = LANES // n_tok
    n_g = nb // g_nb
    blk0 = row0 // LANES
    tokb = lambda w: pl.BlockSpec((LANES, w), lambda i: (blk0 + i, 0))
    rowb = lambda r: pl.BlockSpec((r, LANES), lambda i: (0, blk0 + i))
    m0t = jnp.pad(m0.reshape(n_g, g_nb, N_HEADS).transpose(0, 2, 1),
                  ((0, 0), (0, 0), (0, LANES - g_nb)))
    return pl.pallas_call(
        functools.partial(_mlstm_sample_kernel, n_tok),
        grid=(n_g,),
        in_specs=[tokb(DM), rowb(DM), rowb(DM), rowb(DM), tokb(LANES), rowb(2 * N_HEADS),
                  pl.BlockSpec((g_nb, N_HEADS, HEAD_DIM, HEAD_DIM), lambda i: (i, 0, 0, 0)),
                  pl.BlockSpec((g_nb, DM), lambda i: (i, 0)),
                  pl.BlockSpec((g_nb, N_HEADS), lambda i: (i, 0)),
                  pl.BlockSpec((1, N_HEADS, LANES), lambda i: (i, 0, 0)),
                  pl.BlockSpec((1, DM), lambda i: (0, 0))],
        out_specs=(pl.BlockSpec((LANES, DM), lambda i: (i, 0)),
                   pl.BlockSpec((g_nb, N_HEADS, HEAD_DIM, HEAD_DIM), lambda i: (i, 0, 0, 0)),
                   pl.BlockSpec((g_nb, DM), lambda i: (i, 0)),
                   pl.BlockSpec((1, N_HEADS, LANES), lambda i: (i, 0, 0))),
        out_shape=(jax.ShapeDtypeStruct((nb * n_tok, DM), BF16), jax.ShapeDtypeStruct(c0.shape, F32),
                   jax.ShapeDtypeStruct((nb, DM), F32), jax.ShapeDtypeStruct((n_g, N_HEADS, LANES), F32)),
        compiler_params=_cparams(("arbitrary",), VMEM_LIMIT),
        name="mlstm_sample",
    )(km, qmt, vmt, omt, gcol, grow, c0, n0.reshape(nb, DM), m0, m0t, gh)


def _alibi_slope(h):
    return float(np.float32(2.0 ** (-8.0 * (h + 1) / N_HEADS)))


def _dup_halves(x):
    lane = lax.broadcasted_iota(jnp.int32, x.shape, 1)
    xr = pltpu.roll(x, HEAD_DIM, axis=1)
    lo = lane < HEAD_DIM
    return jnp.where(lo, x, xr), jnp.where(lo, xr, x)


def _stack_group_queries(q, g):
    lane = lax.broadcasted_iota(jnp.int32, (q.shape[0], LANES), 1)
    parts = []
    for hh in range(GROUP):
        h = GROUP * g + hh
        blk = q[:, (h // 2) * LANES:(h // 2 + 1) * LANES]
        keep = (lane < HEAD_DIM) if h % 2 == 0 else (lane >= HEAD_DIM)
        parts.append(jnp.where(keep, blk, 0.0))
    return jnp.concatenate(parts, axis=0).astype(BF16)


def _swa_prompt_kernel(sink_ref, q_ref, kp_ref, ko_ref, vp_ref, vo_ref, o_ref):
    j = pl.program_id(1)
    R = WINDOW
    qi = lax.broadcasted_iota(jnp.int32, (R, R), 0)
    kj = lax.broadcasted_iota(jnp.int32, (R, R), 1)
    own = kj <= qi
    distf = jnp.where(own, qi - kj, qi - kj + R).astype(F32)
    lane = lax.broadcasted_iota(jnp.int32, (R, LANES), 1)
    for u in range(SWA_BLOCKS):
        rs = slice(u * R, (u + 1) * R)
        q = q_ref[rs, :]
        k_prev = kp_ref[...] if u == 0 else ko_ref[(u - 1) * R:u * R, :]
        v_prev = vp_ref[...] if u == 0 else vo_ref[(u - 1) * R:u * R, :]
        kd = _dup_halves(jnp.concatenate([k_prev, ko_ref[rs, :]], axis=0))
        vd = _dup_halves(jnp.concatenate([v_prev, vo_ref[rs, :]], axis=0))
        valid = (own | (j > 0)) if u == 0 else None
        outs = []
        for g in range(N_KV):
            qs = _stack_group_queries(q, g)
            s = lax.dot_general(qs, kd[g].astype(BF16), (((1,), (1,)), ((), ())),
                                preferred_element_type=F32) * (HEAD_DIM ** -0.5)
            ps = []
            for hh in range(GROUP):
                h = GROUP * g + hh
                sink = sink_ref[h]
                rows = slice(hh * R, (hh + 1) * R)
                sh = jnp.where(own, s[rows, R:], s[rows, :R]) - _alibi_slope(h) * distf
                if valid is not None:
                    sh = jnp.where(valid, sh, NEG)
                mx = jnp.maximum(jnp.max(sh, axis=1, keepdims=True), sink)
                p = jnp.exp(sh - mx)
                p = p / (jnp.sum(p, axis=1, keepdims=True) + jnp.exp(sink - mx))
                ps.append(jnp.concatenate([jnp.where(own, 0.0, p), jnp.where(own, p, 0.0)], axis=1).astype(BF16))
            o = jnp.dot(jnp.concatenate(ps, axis=0), vd[g].astype(BF16), preferred_element_type=F32)
            for pp in range(GROUP // 2):
                outs.append(jnp.where(lane < HEAD_DIM, o[(2 * pp) * R:(2 * pp + 1) * R],
                                      o[(2 * pp + 1) * R:(2 * pp + 2) * R]))
        o_ref[rs, :] = jnp.concatenate(outs, axis=1).astype(o_ref.dtype)


def _swa_prompt(sinks, qa, ka, va, batch, seq):
    nb = seq // (WINDOW * SWA_BLOCKS)
    own = lambda w: pl.BlockSpec((WINDOW * SWA_BLOCKS, w), lambda b, j: (b * nb + j, 0))
    prev = lambda w: pl.BlockSpec(
        (WINDOW, w), lambda b, j: (b * nb * SWA_BLOCKS + jnp.maximum(j * SWA_BLOCKS - 1, 0), 0))
    return pl.pallas_call(
        _swa_prompt_kernel,
        grid=(batch, nb),
        in_specs=[pl.BlockSpec(memory_space=pltpu.SMEM), own(DM), prev(DKV), own(DKV), prev(DKV), own(DKV)],
        out_specs=own(DM),
        out_shape=jax.ShapeDtypeStruct((batch * seq, DM), BF16),
        compiler_params=_cparams(("arbitrary", "arbitrary"), VMEM_LIMIT),
        name="swa_prompt",
    )(sinks, qa, ka, ka, va, va)


def _swa_decode_kernel(n_tok, sink_ref, q_ref, kn_ref, vn_ref, kc_ref, vc_ref, o_ref, ko_ref, vo_ref):
    W = WINDOW
    L = LANES
    NB = L // n_tok
    for b in range(NB):
        ko_ref[b, 0:W - n_tok, :] = kc_ref[b, n_tok:W, :]
        ko_ref[b, W - n_tok:W, :] = kn_ref[b * n_tok:(b + 1) * n_tok, :]
        vo_ref[b, 0:W - n_tok, :] = vc_ref[b, n_tok:W, :]
        vo_ref[b, W - n_tok:W, :] = vn_ref[b * n_tok:(b + 1) * n_tok, :]
    q = q_ref[...]
    knd, vnd = _dup_halves(kn_ref[...]), _dup_halves(vn_ref[...])
    kcd = _dup_halves(kc_ref[...].reshape(NB * W, DKV))
    vcd = _dup_halves(vc_ref[...].reshape(NB * W, DKV))
    ri = lax.broadcasted_iota(jnp.int32, (L, L), 0)
    ci = lax.broadcasted_iota(jnp.int32, (L, L), 1)
    t_q = ri % n_tok
    dist_n = t_q - ci % n_tok
    valid_n = ((ri // n_tok) == (ci // n_tok)) & (dist_n >= 0)
    dist_c = t_q + W - ci
    valid_c = dist_c < WINDOW
    dnf = dist_n.astype(F32)
    dcf = dist_c.astype(F32)
    own1 = (lax.broadcasted_iota(jnp.int32, (L, NB * W), 0) // n_tok
            == lax.broadcasted_iota(jnp.int32, (L, NB * W), 1) // W)
    own4 = ((lax.broadcasted_iota(jnp.int32, (GROUP * L, NB * W), 0) % L) // n_tok
            == lax.broadcasted_iota(jnp.int32, (GROUP * L, NB * W), 1) // W)
    lane = lax.broadcasted_iota(jnp.int32, (L, LANES), 1)
    nt = (((1,), (1,)), ((), ()))
    outs = []
    for g in range(N_KV):
        qs = _stack_group_queries(q, g)
        sn = lax.dot_general(qs, knd[g].astype(BF16), nt, preferred_element_type=F32) * (HEAD_DIM ** -0.5)
        sc_all = lax.dot_general(qs, kcd[g].astype(BF16), nt, preferred_element_type=F32) * (HEAD_DIM ** -0.5)
        pcs, pns = [], []
        for hh in range(GROUP):
            h = GROUP * g + hh
            sink = sink_ref[h]
            slope = _alibi_slope(h)
            rows = slice(hh * L, (hh + 1) * L)
            blk = jnp.where(own1, sc_all[rows], 0.0)
            sc = blk[:, 0:W]
            for b in range(1, NB):
                sc = sc + blk[:, b * W:(b + 1) * W]
            shc = jnp.where(valid_c, sc - slope * dcf, NEG)
            shn = jnp.where(valid_n, sn[rows] - slope * dnf, NEG)
            mx = jnp.maximum(jnp.maximum(jnp.max(shc, axis=1, keepdims=True),
                                         jnp.max(shn, axis=1, keepdims=True)), sink)
            pc = jnp.exp(shc - mx)
            pn = jnp.exp(shn - mx)
            inv = 1.0 / (jnp.sum(pc, axis=1, keepdims=True) + jnp.sum(pn, axis=1, keepdims=True)
                         + jnp.exp(sink - mx))
            pcs.append(pc * inv)
            pns.append((pn * inv).astype(BF16))
        pc4 = jnp.concatenate(pcs, axis=0)
        p_bd = jnp.where(own4, jnp.concatenate([pc4] * NB, axis=1), 0.0).astype(BF16)
        o = (jnp.dot(p_bd, vcd[g].astype(BF16), preferred_element_type=F32)
             + jnp.dot(jnp.concatenate(pns, axis=0), vnd[g].astype(BF16), preferred_element_type=F32))
        for pp in range(GROUP // 2):
            outs.append(jnp.where(lane < HEAD_DIM, o[(2 * pp) * L:(2 * pp + 1) * L],
                                  o[(2 * pp + 1) * L:(2 * pp + 2) * L]))
    o_ref[...] = jnp.concatenate(outs, axis=1)


def _swa_decode(sinks, qa, ka, va, kc, vc, row0, n_tok):
    nb = kc.shape[0]
    g_nb = LANES // n_tok
    blk0 = row0 // LANES
    tokb = lambda w: pl.BlockSpec((LANES, w), lambda i: (blk0 + i, 0))
    cache = pl.BlockSpec((g_nb, WINDOW, DKV), lambda i: (i, 0, 0))
    return pl.pallas_call(
        functools.partial(_swa_decode_kernel, n_tok),
        grid=(nb // g_nb,),
        in_specs=[pl.BlockSpec(memory_space=pltpu.SMEM), tokb(DM), tokb(DKV), tokb(DKV), cache, cache],
        out_specs=(pl.BlockSpec((LANES, DM), lambda i: (i, 0)), cache, cache),
        out_shape=(jax.ShapeDtypeStruct((nb * n_tok, DM), F32),
                   jax.ShapeDtypeStruct(kc.shape, F32), jax.ShapeDtypeStruct(vc.shape, F32)),
        compiler_params=_cparams(("arbitrary",), VMEM_LIMIT),
        name="swa_decode",
    )(sinks, qa, ka, va, kc, vc)


def _outproj_router_kernel(n_ptiles, xp_ref, xs_ref, hmp_ref, hap_ref, hms_ref, has_ref, wom_ref, woa_ref,
                           g_ref, wrt_ref, br_ref,
                           x1_ref, xg_ref, meta_ref, cnt_ref):
    i = pl.program_id(0)
    is_p = i < n_ptiles
    x = jnp.where(is_p, xp_ref[...], xs_ref[...])
    hm = jnp.where(is_p, hmp_ref[...], hms_ref[...].astype(BF16))
    ha = jnp.where(is_p, hap_ref[...], has_ref[...].astype(BF16))
    x1 = (x + jnp.dot(hm, wom_ref[...], preferred_element_type=F32)
          + jnp.dot(ha, woa_ref[...], preferred_element_type=F32))
    x1_ref[...] = x1
    h2 = _rms(x1, g_ref[...])
    nt = (((1,), (1,)), ((), ()))
    h2_hi = h2.astype(BF16)
    h2_lo = (h2 - h2_hi.astype(F32)).astype(BF16)
    wr = wrt_ref[...]
    wr_hi = wr.astype(BF16)
    wr_lo = (wr - wr_hi.astype(F32)).astype(BF16)
    logits = (lax.dot_general(wr_hi, h2_hi, nt, preferred_element_type=F32)
              + lax.dot_general(wr_lo, h2_hi, nt, preferred_element_type=F32)
              + lax.dot_general(wr_hi, h2_lo, nt, preferred_element_type=F32)) + br_ref[...]
    eidx = lax.broadcasted_iota(jnp.int32, logits.shape, 0).astype(F32)
    work = logits
    vals, hots = [], []
    for _ in range(TOP_K):
        mv = jnp.max(work, axis=0, keepdims=True)
        sel = jnp.min(jnp.where(work == mv, eidx, float(N_EXPERTS)), axis=0, keepdims=True)
        hot = eidx == sel
        vals.append(mv)
        hots.append(hot)
        work = jnp.where(hot, -jnp.inf, work)
    es = [jnp.exp(v - vals[0]) for v in vals]
    tot = es[0] + es[1] + es[2] + es[3]
    gates = [e / tot for e in es]
    hot_all = jnp.where(hots[0] | hots[1] | hots[2] | hots[3], 1.0, 0.0)
    tm = logits.shape[1]
    su = (lax.broadcasted_iota(jnp.int32, (tm, tm), 0) < lax.broadcasted_iota(jnp.int32, (tm, tm), 1))
    cum = jnp.dot(hot_all.astype(BF16), su.astype(BF16), preferred_element_type=F32)
    cnt = jnp.sum(hot_all, axis=1, keepdims=True)
    cpad = (((cnt.astype(jnp.int32) + (SUBLANES - 1)) // SUBLANES) * SUBLANES).astype(F32)
    lower = (lax.broadcasted_iota(jnp.int32, (N_EXPERTS, N_EXPERTS), 0)
             > lax.broadcasted_iota(jnp.int32, (N_EXPERTS, N_EXPERTS), 1)).astype(F32)
    lstart = jnp.dot(lower, jnp.broadcast_to(cpad, (N_EXPERTS, LANES)), precision=HIGHEST,
                     preferred_element_type=F32)[:, 0:1]
    base = lstart + cum
    lpos = [jnp.sum(jnp.where(hot, base, 0.0), axis=0, keepdims=True) for hot in hots]
    lpi = [p.astype(jnp.int32) for p in lpos]
    h2b = h2.astype(BF16)
    rows_per = GROUP_R // SORT_CHUNKS
    for j in range(SORT_CHUNKS):
        r_iota = lax.broadcasted_iota(jnp.int32, (rows_per, tm), 0) + j * rows_per
        sel01 = jnp.where(r_iota == lpi[0], 1.0, jnp.where(r_iota == lpi[1], 1.0, jnp.where(
            r_iota == lpi[2], 1.0, jnp.where(r_iota == lpi[3], 1.0, 0.0)))).astype(BF16)
        xg_ref[j * rows_per:(j + 1) * rows_per, :] = jnp.dot(sel01, h2b, preferred_element_type=F32)
    meta_ref[...] = jnp.transpose(jnp.concatenate(gates + lpos, axis=0))
    cnt_ref[0] = jnp.broadcast_to(cnt, (N_EXPERTS, LANES))


def _outproj_router(xp, xs, hmp, hap, hms, has, wom, woa, g_ffn, wrt, br):
    tp, ts = xp.shape[0], xs.shape[0]
    n_pt, n_st = tp // ROUTE_T, ts // ROUTE_T
    t_all = tp + ts
    pblk = lambda w: pl.BlockSpec((ROUTE_T, w), lambda i: (jnp.minimum(i, n_pt - 1), 0))
    sblk = lambda w: pl.BlockSpec((ROUTE_T, w), lambda i: (jnp.maximum(i - n_pt, 0), 0))
    full = lambda a: pl.BlockSpec(a.shape, lambda i: (0,) * a.ndim)
    return pl.pallas_call(
        functools.partial(_outproj_router_kernel, n_pt),
        grid=(n_pt + n_st,),
        in_specs=[pblk(D_MODEL), sblk(D_MODEL), pblk(DM), pblk(DM), sblk(DM), sblk(DM),
                  full(wom), full(woa), full(g_ffn), full(wrt), full(br)],
        out_specs=(pl.BlockSpec((ROUTE_T, D_MODEL), lambda i: (i, 0)),
                   pl.BlockSpec((GROUP_R, D_MODEL), lambda i: (i, 0)),
                   pl.BlockSpec((ROUTE_T, 2 * TOP_K), lambda i: (i, 0)),
                   pl.BlockSpec((1, N_EXPERTS, LANES), lambda i: (i, 0, 0))),
        out_shape=(jax.ShapeDtypeStruct((t_all, D_MODEL), F32),
                   jax.ShapeDtypeStruct(((n_pt + n_st) * GROUP_R, D_MODEL), F32),
                   jax.ShapeDtypeStruct((t_all, 2 * TOP_K), F32),
                   jax.ShapeDtypeStruct((n_pt + n_st, N_EXPERTS, LANES), F32)),
        compiler_params=_cparams(("arbitrary",), VMEM_LIMIT),
        name="outproj_router",
    )(xp, xs, hmp, hap, hms, has, wom, woa, g_ffn, wrt, br)


def _expert_kernel(be_ref, na_ref, slot_ref, nxt_ref, ctab_ref, ctab1_ref, ctab2_ref, xg_ref, wgu_ref, bgu_ref,
                   wd_ref, bd_ref, o_ref, xbuf, wgu_f, wd_f, wgu_s, wd_s, xsem, wsem):
    i = pl.program_id(0)
    na = na_ref[0]

    def x_copies(tab_ref, slot):
        return [pltpu.make_async_copy(xg_ref.at[pl.ds(pl.multiple_of(tab_ref[0, 0, c], SUBLANES), SUBLANES)],
                                      xbuf.at[slot, pl.ds(c * SUBLANES, SUBLANES)], xsem.at[slot])
                for c in range(MOE_BM // SUBLANES)]

    def w_copies(e, slot):
        return [pltpu.make_async_copy(wgu_ref.at[e], wgu_f.at[slot], wsem.at[slot]),
                pltpu.make_async_copy(wd_ref.at[e], wd_f.at[slot], wsem.at[slot])]

    @pl.when(i == 0)
    def _():
        for cp in x_copies(ctab_ref, 0) + x_copies(ctab1_ref, 1) + w_copies(be_ref[0], 0):
            cp.start()

    @pl.when(i == na)
    def _():
        for cp in x_copies(ctab_ref, lax.rem(i, X_SLOTS)) + x_copies(ctab1_ref, lax.rem(i + 1, X_SLOTS)):
            cp.wait()

    @pl.when(i < na)
    def _():
        changed = (i == 0) | (be_ref[i] != be_ref[jnp.maximum(i - 1, 0)])
        wslot = slot_ref[i]
        xslot = lax.rem(i, X_SLOTS)

        @pl.when(changed)
        def _():
            for cp in w_copies(be_ref[i], wslot):
                cp.wait()

            @pl.when(nxt_ref[i] >= 0)
            def _():
                for cp in w_copies(nxt_ref[i], 1 - wslot):
                    cp.start(priority=1)

            wgu_s[...] = wgu_f[wslot].astype(BF16)
            wd_s[...] = wd_f[wslot].astype(BF16)

        for cp in x_copies(ctab_ref, xslot):
            cp.wait()

        x = xbuf[xslot].astype(BF16)
        gu = jnp.dot(x, wgu_s[...], preferred_element_type=F32) + bgu_ref[...]
        for cp in x_copies(ctab2_ref, lax.rem(i + 2, X_SLOTS)):
            cp.start()
        gate = jnp.minimum(gu[:, :D_FF], SWIGLU_LIMIT)
        up = jnp.clip(gu[:, D_FF:], -SWIGLU_LIMIT, SWIGLU_LIMIT)
        act = (up + 1.0) * (gate * _sigmoid(gate * SWIGLU_ALPHA))
        o_ref[...] = jnp.dot(act.astype(BF16), wd_s[...], preferred_element_type=F32) + bd_ref[...]


def _expert_ffn(block_exp, n_active, wslot, next_exp, ctab, xg, wgu, bgu, wd, bd):
    n_blocks = ctab.shape[0]
    n_rows = n_blocks * MOE_BM
    nch = MOE_BM // SUBLANES
    tab = lambda d: pl.BlockSpec((1, 1, nch), lambda i, *_: (jnp.minimum(i + d, n_blocks - 1), 0, 0),
                                 memory_space=pltpu.SMEM)
    blk_e = lambda i, be: be[jnp.minimum(i, n_blocks - 1)]
    grid_spec = pltpu.PrefetchScalarGridSpec(
        num_scalar_prefetch=4,
        grid=(n_blocks + 1,),
        in_specs=[tab(0), tab(1), tab(2),
                  pl.BlockSpec(memory_space=pl.ANY),
                  pl.BlockSpec(memory_space=pl.ANY),
                  pl.BlockSpec((None, 1, 2 * D_FF), lambda i, be, *_: (blk_e(i, be), 0, 0)),
                  pl.BlockSpec(memory_space=pl.ANY),
                  pl.BlockSpec((None, 1, D_MODEL), lambda i, be, *_: (blk_e(i, be), 0, 0))],
        out_specs=pl.BlockSpec((MOE_BM, D_MODEL), lambda i, be, na, *_: (jnp.minimum(i, na[0] - 1), 0)),
        scratch_shapes=[pltpu.VMEM((X_SLOTS, MOE_BM, D_MODEL), F32),
                        pltpu.VMEM((2, D_MODEL, 2 * D_FF), F32), pltpu.VMEM((2, D_FF, D_MODEL), F32),
                        pltpu.VMEM((D_MODEL, 2 * D_FF), BF16), pltpu.VMEM((D_FF, D_MODEL), BF16),
                        pltpu.SemaphoreType.DMA((X_SLOTS,)), pltpu.SemaphoreType.DMA((2,))],
    )
    return pl.pallas_call(
        _expert_kernel,
        grid_spec=grid_spec,
        out_shape=jax.ShapeDtypeStruct((n_rows, D_MODEL), F32),
        compiler_params=_cparams(("arbitrary",), VMEM_LIMIT),
        name="moe_experts",
    )(block_exp, n_active, wslot, next_exp, ctab, ctab, ctab, xg, wgu, bgu, wd, bd)


def _combine_kernel(n_ptiles, n_tiles, ctab_ref, ctab1_ref, ctab2_ref, outs_ref, x1_ref, meta_ref, gf_ref,
                    yp_ref, ys_ref, obuf, sem):
    i = pl.program_id(0)

    def copies(tab_ref, s):
        return [pltpu.make_async_copy(outs_ref.at[pl.ds(pl.multiple_of(tab_ref[0, 0, c], SUBLANES), SUBLANES)],
                                      obuf.at[s, pl.ds(c * SUBLANES, SUBLANES)], sem.at[s])
                for c in range(GROUP_R // SUBLANES)]

    @pl.when(i == 0)
    def _():
        for cp in copies(ctab_ref, 0) + copies(ctab1_ref, 1):
            cp.start()

    @pl.when(i == n_tiles)
    def _():
        for cp in copies(ctab_ref, lax.rem(i, X_SLOTS)) + copies(ctab1_ref, lax.rem(i + 1, X_SLOTS)):
            cp.wait()

    @pl.when(i < n_tiles)
    def _():
        _combine_tile(i, n_ptiles, copies, ctab_ref, ctab2_ref, x1_ref, meta_ref, gf_ref, yp_ref, ys_ref, obuf)


def _combine_tile(i, n_ptiles, copies, ctab_ref, ctab2_ref, x1_ref, meta_ref, gf_ref, yp_ref, ys_ref, obuf):
    slot = lax.rem(i, X_SLOTS)
    for cp in copies(ctab_ref, slot):
        cp.wait()
    for cp in copies(ctab2_ref, lax.rem(i + 2, X_SLOTS)):
        cp.start()

    meta = meta_ref[...]
    tm = meta.shape[0]
    r_iota = lax.broadcasted_iota(jnp.int32, (tm, GROUP_R), 1)
    lp = [meta[:, TOP_K + k:TOP_K + k + 1].astype(jnp.int32) for k in range(TOP_K)]
    gk = [meta[:, k:k + 1] for k in range(TOP_K)]
    gsel = jnp.where(r_iota == lp[0], gk[0], jnp.where(r_iota == lp[1], gk[1], jnp.where(
        r_iota == lp[2], gk[2], jnp.where(r_iota == lp[3], gk[3], 0.0))))
    sel01 = jnp.where(gsel != 0.0, 1.0, 0.0).astype(BF16)
    rg_row = jnp.sum(gsel, axis=0, keepdims=True)
    rg_col = jnp.transpose(jnp.broadcast_to(rg_row, (SUBLANES, GROUP_R)))[:, 0:1]
    og = (obuf[slot] * rg_col).astype(BF16)
    acc = x1_ref[...] + jnp.dot(sel01, og, preferred_element_type=F32)
    y = _rms(acc, gf_ref[...])

    @pl.when(i < n_ptiles)
    def _():
        yp_ref[...] = y

    @pl.when(i >= n_ptiles)
    def _():
        ys_ref[...] = y


def _combine(ctab, outs, x1, meta, g_final, tp, ts):
    n_pt, n_st = tp // ROUTE_T, ts // ROUTE_T
    n = n_pt + n_st
    nch = GROUP_R // SUBLANES
    last = lambda i: jnp.minimum(i, n - 1)
    tab = lambda d: pl.BlockSpec((1, 1, nch), lambda i: (last(i + d), 0, 0), memory_space=pltpu.SMEM)
    return pl.pallas_call(
        functools.partial(_combine_kernel, n_pt, n),
        grid=(n + 1,),
        in_specs=[tab(0), tab(1), tab(2),
                  pl.BlockSpec(memory_space=pl.ANY),
                  pl.BlockSpec((ROUTE_T, D_MODEL), lambda i: (last(i), 0)),
                  pl.BlockSpec((ROUTE_T, 2 * TOP_K), lambda i: (last(i), 0)),
                  pl.BlockSpec((1, D_MODEL), lambda i: (0, 0))],
        out_specs=(pl.BlockSpec((ROUTE_T, D_MODEL), lambda i: (jnp.minimum(i, n_pt - 1), 0)),
                   pl.BlockSpec((ROUTE_T, D_MODEL), lambda i: (jnp.maximum(last(i) - n_pt, 0), 0))),
        out_shape=(jax.ShapeDtypeStruct((tp, D_MODEL), F32), jax.ShapeDtypeStruct((ts, D_MODEL), F32)),
        scratch_shapes=[pltpu.VMEM((X_SLOTS, GROUP_R, D_MODEL), F32), pltpu.SemaphoreType.DMA((X_SLOTS,))],
        compiler_params=_cparams(("arbitrary",), VMEM_LIMIT),
        name="moe_combine",
    )(ctab, ctab, ctab, outs, x1, meta, g_final)


def kernel(x_prompt, x_sample, cache_swa_k, cache_swa_v, state_mlstm_c, state_mlstm_n, state_mlstm_m,
           g_mix, w_in, b_igate, b_fgate, g_head, attn_sinks, w_out, g_ffn, w_router, b_router,
           w_gate_up, b_gate_up, w_down, b_down, g_final):
    assert w_in.shape[0] == 1, "single-layer problem"
    B, S, _ = x_prompt.shape
    Bd, Tn, _ = x_sample.shape
    tp, ts = B * S, Bd * Tn
    t_all = tp + ts
    xp = x_prompt.reshape(tp, D_MODEL)
    xs = x_sample.reshape(ts, D_MODEL)

    w = w_in[0]
    o = np.cumsum([0, DM, DM, DM, DM, N_HEADS, N_HEADS, DM, DKV, DKV])
    col = lambda a: w[:, int(o[a]):int(o[a + 1])]
    wgates = jnp.concatenate([col(4), col(5)], axis=1)
    w1 = jnp.concatenate([col(1), col(6), col(7), col(8), jnp.pad(wgates, ((0, 0), (0, LANES - 2 * N_HEADS)))],
                         axis=1).astype(BF16)
    wkt = jnp.concatenate([col(0), col(2), col(3), wgates], axis=1).T.astype(BF16)
    bg = jnp.concatenate([b_igate[0], b_fgate[0]]).astype(F32)
    bcol = jnp.pad(bg, (0, LANES - 2 * N_HEADS)).reshape(1, LANES)
    brow = bg.reshape(2 * N_HEADS, 1)

    km, qa, ka, va, qmt, vmt, omt, gcol, grow = _inproj(xp, xs, g_mix[0].reshape(1, D_MODEL), w1, wkt, bcol, brow)

    gh = g_head[0].astype(F32)
    sinks = attn_sinks[0].astype(F32)

    hm_p, ctp, m_p = _mlstm_prompt(km, qmt, vmt, omt, gcol, grow,
                                   jnp.broadcast_to(gh.reshape(DM, 1), (DM, LANES)), B, S)
    ha_p = _swa_prompt(sinks, qa, ka, va, B, S)

    hm_s, c_s, n_s, mt_s = _mlstm_sample(km, qmt, vmt, omt, gcol, grow, state_mlstm_c[0], state_mlstm_n[0],
                                         state_mlstm_m[0], gh.reshape(1, DM), tp, Tn)
    n_s = n_s.reshape(Bd, N_HEADS, HEAD_DIM)
    m_s = mt_s[:, :, :LANES // Tn].transpose(0, 2, 1).reshape(Bd, N_HEADS)
    ha_s, k_s, v_s = _swa_decode(sinks, qa, ka, va, cache_swa_k[0].reshape(Bd, WINDOW, DKV),
                                 cache_swa_v[0].reshape(Bd, WINDOW, DKV), tp, Tn)

    wo = w_out[0].astype(BF16)
    x1, xg, meta, cnt = _outproj_router(
        xp, xs, hm_p, ha_p, hm_s, ha_s, wo[:DM], wo[DM:], g_ffn[0].reshape(1, D_MODEL),
        w_router[0].T, b_router[0].reshape(N_EXPERTS, 1))

    i32 = jnp.int32
    n_tiles = t_all // ROUTE_T
    max_rows = t_all * TOP_K + n_tiles * N_EXPERTS * (SUBLANES - 1) + N_EXPERTS * (MOE_BM - 1)
    n_blocks = -(-max_rows // MOE_BM)
    cpad = (cnt[:, :, 0].astype(i32) + (SUBLANES - 1)) // SUBLANES * SUBLANES
    lstart = jnp.cumsum(cpad, axis=1) - cpad
    goff = jnp.cumsum(cpad, axis=0) - cpad
    padded = (jnp.sum(cpad, axis=0) + MOE_BM - 1) // MOE_BM * MOE_BM
    pad_end = jnp.cumsum(padded)
    seg_begin = (pad_end - padded)[None, :] + goff
    n_active = (pad_end[-1] // MOE_BM).astype(i32)
    blk = jnp.minimum(jnp.arange(n_blocks, dtype=i32), n_active - 1)
    block_exp = jnp.minimum(jnp.sum((pad_end[None, :] <= (blk * MOE_BM)[:, None]).astype(i32), axis=1),
                            N_EXPERTS - 1)
    e_ids = jnp.arange(N_EXPERTS, dtype=i32)
    nonempty = padded > 0
    nxt_e = jnp.min(jnp.where((e_ids[None, :] > e_ids[:, None]) & nonempty[None, :], e_ids[None, :], N_EXPERTS),
                    axis=1)
    nxt_e = jnp.where(nxt_e == N_EXPERTS, -1, nxt_e)
    ord_e = jnp.cumsum(nonempty.astype(i32)) - 1
    be_hot = block_exp[:, None] == e_ids[None, :]
    next_exp = jnp.sum(jnp.where(be_hot, nxt_e[None, :], 0), axis=1).astype(i32)
    wslot = (jnp.sum(jnp.where(be_hot, ord_e[None, :], 0), axis=1) % 2).astype(i32)

    seg_src = jnp.arange(n_tiles, dtype=i32)[:, None] * GROUP_R + lstart
    sb, sl, ss = seg_begin.reshape(-1), cpad.reshape(-1), seg_src.reshape(-1)
    rc = jnp.arange(n_blocks * MOE_BM // SUBLANES, dtype=i32)[:, None] * SUBLANES
    inseg = (sb[None, :] <= rc) & (rc < (sb + sl)[None, :])
    ctab_e = jnp.where(jnp.any(inseg, axis=1), jnp.sum(jnp.where(inseg, (ss - sb)[None, :] + rc, 0), axis=1),
                       GROUP_R - SUBLANES)
    lr = jnp.arange(GROUP_R // SUBLANES, dtype=i32)[None, :, None] * SUBLANES
    inl = (lstart[:, None, :] <= lr) & (lr < (lstart + cpad)[:, None, :])
    ctab_c = jnp.sum(jnp.where(inl, (seg_begin - lstart)[:, None, :] + lr, 0), axis=2)

    outs = _expert_ffn(block_exp, n_active.reshape(1), wslot, next_exp,
                       ctab_e.astype(i32).reshape(n_blocks, 1, MOE_BM // SUBLANES), xg, w_gate_up[0],
                       b_gate_up[0].reshape(N_EXPERTS, 1, 2 * D_FF), w_down[0],
                       b_down[0].reshape(N_EXPERTS, 1, D_MODEL))
    y_p, y_s = _combine(ctab_c.astype(i32).reshape(n_tiles, 1, GROUP_R // SUBLANES), outs, x1, meta,
                        g_final.reshape(1, D_MODEL), tp, ts)

    kv_tail = lambda a: jnp.concatenate([a[(b + 1) * S - WINDOW:(b + 1) * S] for b in range(B)], axis=0).reshape(
        1, B, WINDOW, N_KV, HEAD_DIM)
    c_e = ctp[:, :, :HEAD_DIM, :HEAD_DIM]
    c_o = ctp[:, :, HEAD_DIM:LANES, HEAD_DIM:]
    c_p = jnp.swapaxes(jnp.stack([c_e, c_o], axis=2), -1, -2).reshape(B, N_HEADS, HEAD_DIM, HEAD_DIM)
    n_p = (ctp[:, :, LANES:LANES + 2, :HEAD_DIM] + ctp[:, :, LANES:LANES + 2, HEAD_DIM:]).reshape(B, N_HEADS, HEAD_DIM)
    return (y_p.reshape(B, S, D_MODEL), y_s.reshape(Bd, Tn, D_MODEL),
            kv_tail(ka), kv_tail(va), c_p[None], n_p[None], m_p[:, :, 0][None],
            k_s.reshape(Bd, WINDOW, N_KV, HEAD_DIM)[None], v_s.reshape(Bd, WINDOW, N_KV, HEAD_DIM)[None],
            c_s[None], n_s[None], m_s[None])
```

```python
import functools

import jax
import jax.numpy as jnp
import numpy as np
from jax import lax
from jax.experimental import pallas as pl
from jax.experimental.pallas import tpu as pltpu

F32 = jnp.float32
BF16 = jnp.bfloat16
HIGHEST = lax.Precision.HIGHEST

D_MODEL = 1024
HEAD_DIM = 64
N_HEADS = 8
N_PAIRS = N_HEADS // 2
N_KV = 2
GROUP = N_HEADS // N_KV
WINDOW = 128
N_EXPERTS = 32
TOP_K = 4
D_FF = 1024
SWIGLU_LIMIT = 7.0
SWIGLU_ALPHA = 1.702
RMS_EPS = 1e-5
DM = N_HEADS * HEAD_DIM
DKV = N_KV * HEAD_DIM
NEG = -1e30

LANES = 128
SUBLANES = 8
VMEM_LIMIT = 56 * 1024 * 1024

TM = 512
ROUTE_T = 512
MLSTM_TL = 1024
MLSTM_L = 128
SWA_BLOCKS = 8
MOE_BM = 256
X_SLOTS = 3
GROUP_R = -(-(TOP_K * ROUTE_T + N_EXPERTS * (SUBLANES - 1) + SUBLANES) // LANES) * LANES
SORT_CHUNKS = 3


def _cparams(sem, vmem=None):
    return pltpu.CompilerParams(dimension_semantics=sem, vmem_limit_bytes=vmem)


def _rms(x, g):
    return x * lax.rsqrt(jnp.mean(x * x, axis=-1, keepdims=True) + RMS_EPS) * g


def _log_sigmoid(z):
    return jnp.minimum(z, 0.0) - jnp.log(1.0 + jnp.exp(-jnp.abs(z)))


def _sigmoid(z):
    return 1.0 / (1.0 + jnp.exp(-z))


def _inproj_kernel(n_ptiles, xp_ref, xs_ref, g_ref, w1_ref, wt_ref, bcol_ref, brow_ref,
                   km_ref, qa_ref, ka_ref, va_ref, qmt_ref, vmt_ref, omt_ref, gcol_ref, grow_ref):
    i = pl.program_id(0)
    x = jnp.where(i < n_ptiles, xp_ref[...], xs_ref[...])
    h = _rms(x, g_ref[...]).astype(BF16)
    main = jnp.dot(h, w1_ref[...], preferred_element_type=F32)
    km_ref[...] = main[:, 0:DM] * (HEAD_DIM ** -0.5)
    qa_ref[...] = main[:, DM:2 * DM]
    ka_ref[...] = main[:, 2 * DM:2 * DM + DKV]
    va_ref[...] = main[:, 2 * DM + DKV:2 * DM + 2 * DKV]
    t = lax.dot_general(wt_ref[...], h, (((1,), (1,)), ((), ())), preferred_element_type=F32)
    qmt_ref[...] = t[0:DM]
    vmt_ref[...] = t[DM:2 * DM]
    omt_ref[...] = t[2 * DM:3 * DM]
    zc = main[:, 2 * DM + 2 * DKV:] + bcol_ref[...]
    lane = lax.broadcasted_iota(jnp.int32, zc.shape, 1)
    gcol_ref[...] = jnp.where(lane < N_HEADS, zc, _log_sigmoid(zc))
    zr = t[3 * DM:] + brow_ref[...]
    row = lax.broadcasted_iota(jnp.int32, zr.shape, 0)
    grow_ref[...] = jnp.where(row < N_HEADS, zr, _log_sigmoid(zr))


def _inproj(xp, xs, g_mix, w1, wkt, bcol, brow):
    tp, ts = xp.shape[0], xs.shape[0]
    n_pt, n_st = tp // TM, ts // TM
    t_all = tp + ts
    tok = lambda w: pl.BlockSpec((TM, w), lambda i: (i, 0))
    tr = lambda r: pl.BlockSpec((r, TM), lambda i: (0, i))
    full = lambda a: pl.BlockSpec(a.shape, lambda i: (0,) * a.ndim)
    out_shape = (
        jax.ShapeDtypeStruct((t_all, DM), F32), jax.ShapeDtypeStruct((t_all, DM), F32),
        jax.ShapeDtypeStruct((t_all, DKV), F32), jax.ShapeDtypeStruct((t_all, DKV), F32),
        jax.ShapeDtypeStruct((DM, t_all), F32), jax.ShapeDtypeStruct((DM, t_all), F32),
        jax.ShapeDtypeStruct((DM, t_all), F32),
        jax.ShapeDtypeStruct((t_all, LANES), F32), jax.ShapeDtypeStruct((2 * N_HEADS, t_all), F32),
    )
    return pl.pallas_call(
        functools.partial(_inproj_kernel, n_pt),
        grid=(n_pt + n_st,),
        in_specs=[
            pl.BlockSpec((TM, D_MODEL), lambda i: (jnp.minimum(i, n_pt - 1), 0)),
            pl.BlockSpec((TM, D_MODEL), lambda i: (jnp.maximum(i - n_pt, 0), 0)),
            full(g_mix), full(w1), full(wkt), full(bcol), full(brow),
        ],
        out_specs=(tok(DM), tok(DM), tok(DKV), tok(DKV), tr(DM), tr(DM), tr(DM),
                   tok(LANES), tr(2 * N_HEADS)),
        out_shape=out_shape,
        compiler_params=_cparams(("arbitrary",), VMEM_LIMIT),
        name="inproj",
    )(xp, xs, g_mix, w1, wkt, bcol, brow)


CT_ROWS = LANES + 2 * SUBLANES


def _cumsum_rows(x, n):
    row = lax.broadcasted_iota(jnp.int32, x.shape, 0)
    sh = 1
    while sh < n:
        x = x + jnp.where(row >= sh, pltpu.roll(x, sh, axis=0), 0.0)
        sh *= 2
    return x


def _mlstm_prompt_kernel(km_ref, qmt_ref, vmt_ref, omt_ref, gcol_ref, grow_ref, ghr_ref,
                         hm_ref, ct_ref, m_ref, ct_s, m_s):
    j = pl.program_id(1)
    L = MLSTM_L
    assert L == LANES

    @pl.when(j == 0)
    def _():
        ct_s[...] = jnp.zeros_like(ct_s)
        m_s[...] = jnp.zeros_like(m_s)

    si = lax.broadcasted_iota(jnp.int32, (L, L), 0)
    ti = lax.broadcasted_iota(jnp.int32, (L, L), 1)
    causal_t = si <= ti
    upper = jnp.where(causal_t, 1.0, 0.0)
    rows_c = lax.broadcasted_iota(jnp.int32, (CT_ROWS, 1), 0)
    rmask_e = (rows_c < HEAD_DIM) | (rows_c == LANES)
    rmask_o = ((rows_c >= HEAD_DIM) & (rows_c < LANES)) | (rows_c == LANES + 1)
    rows_e = lax.broadcasted_iota(jnp.int32, (LANES, 1), 0) < HEAD_DIM
    cols_e = lax.broadcasted_iota(jnp.int32, (1, LANES), 1) < HEAD_DIM
    bdt_mask = (rmask_e & cols_e) | (rmask_o & (~cols_e))
    ones_rows = jnp.where(lax.broadcasted_iota(jnp.int32, (CT_ROWS - LANES, L), 0) < 2, 1.0, 0.0)

    for c in range(MLSTM_TL // L):
        sl = slice(c * L, (c + 1) * L)
        grow = grow_ref[:, sl]
        i_row = grow[0:N_HEADS]
        b_row = jnp.dot(grow[N_HEADS:2 * N_HEADS], upper, precision=HIGHEST,
                        preferred_element_type=F32)
        gc = gcol_ref[sl, :]
        bc_all = _cumsum_rows(gc, L)
        for p in range(N_PAIRS):
            ls = slice(p * LANES, (p + 1) * LANES)
            k2 = km_ref[sl, ls]
            qt2 = qmt_ref[ls, sl]
            qt_e = jnp.where(rows_e, qt2, 0.0).astype(BF16)
            qt_o = jnp.where(rows_e, 0.0, qt2).astype(BF16)
            st2 = jnp.dot(k2.astype(BF16), jnp.concatenate([qt_e, qt_o], axis=1),
                          preferred_element_type=F32)
            ct = ct_s[p]
            rqt = jnp.dot(ct.astype(BF16), qt2.astype(BF16), preferred_element_type=F32)
            pts, mts, inters, wreps, decays, mnews = [], [], [], [], [], []
            for hh in range(2):
                h = 2 * p + hh
                cvec = jnp.broadcast_to(gc[:, h:h + 1] - bc_all[:, N_HEADS + h:N_HEADS + h + 1], (L, L))
                brow = b_row[h:h + 1, :]
                logdt = jnp.where(causal_t, cvec + brow, NEG)
                m_prev = m_s[h:h + 1, 0:1]
                m_inter = m_prev + brow
                m_t = jnp.maximum(m_inter, jnp.max(logdt, axis=0, keepdims=True))
                pts.append((st2[:, hh * L:(hh + 1) * L] * jnp.exp(logdt - m_t)).astype(BF16))
                mts.append(m_t)
                inters.append(jnp.exp(m_inter - m_t))
                m_new = m_t[:, L - 1:L]
                b_last = brow[:, L - 1:L]
                decays.append(jnp.exp(m_prev + b_last - m_new))
                wreps.append(jnp.exp(cvec + (b_last - m_new)))
                mnews.append(m_new)
            vext = jnp.concatenate([vmt_ref[ls, sl], ones_rows], axis=0)
            lhs = jnp.concatenate([jnp.where(rmask_e, vext, 0.0), jnp.where(rmask_o, vext, 0.0)],
                                  axis=1).astype(BF16)
            rt = jnp.dot(lhs, jnp.concatenate(pts, axis=0), preferred_element_type=F32)
            ndt = rt + jnp.where(rmask_e, inters[0], inters[1]) * rqt
            den = jnp.where(rows_e, ndt[LANES:LANES + 1], ndt[LANES + 1:LANES + 2])
            mt2 = jnp.where(rows_e, mts[0], mts[1])
            hvt = ndt[0:LANES] / jnp.maximum(jnp.abs(den), jnp.exp(-mt2))
            sq = hvt * hvt
            ms = jnp.where(rows_e, jnp.sum(sq[0:HEAD_DIM], axis=0, keepdims=True),
                           jnp.sum(sq[HEAD_DIM:LANES], axis=0, keepdims=True)) * (1.0 / HEAD_DIM)
            yt = hvt * lax.rsqrt(ms + RMS_EPS) * ghr_ref[ls, :] * _sigmoid(omt_ref[ls, sl])
            hm_ref[sl, ls] = jnp.transpose(yt).astype(hm_ref.dtype)
            kw = (k2 * jnp.where(cols_e, wreps[0], wreps[1])).astype(BF16)
            upd = jnp.dot(vext.astype(BF16), kw, preferred_element_type=F32)
            ct_s[p] = jnp.where(rmask_e, decays[0], decays[1]) * ct + jnp.where(bdt_mask, upd, 0.0)
            for hh in range(2):
                h = 2 * p + hh
                m_s[h:h + 1, :] = jnp.broadcast_to(mnews[hh], (1, LANES))

    @pl.when(j == pl.num_programs(1) - 1)
    def _():
        ct_ref[0] = ct_s[...]
        m_ref[0] = m_s[...]


def _mlstm_prompt(km, qmt, vmt, omt, gcol, grow, ghr, batch, seq):
    nt = seq // MLSTM_TL
    tokb = lambda w: pl.BlockSpec((MLSTM_TL, w), lambda b, j: (b * nt + j, 0))
    rowb = lambda r: pl.BlockSpec((r, MLSTM_TL), lambda b, j: (0, b * nt + j))
    return pl.pallas_call(
        _mlstm_prompt_kernel,
        grid=(batch, nt),
        in_specs=[tokb(DM), rowb(DM), rowb(DM), rowb(DM), tokb(LANES), rowb(2 * N_HEADS),
                  pl.BlockSpec((DM, LANES), lambda b, j: (0, 0))],
        out_specs=(tokb(DM),
                   pl.BlockSpec((1, N_PAIRS, CT_ROWS, LANES), lambda b, j: (b, 0, 0, 0)),
                   pl.BlockSpec((1, N_HEADS, LANES), lambda b, j: (b, 0, 0))),
        out_shape=(jax.ShapeDtypeStruct((batch * seq, DM), BF16),
                   jax.ShapeDtypeStruct((batch, N_PAIRS, CT_ROWS, LANES), F32),
                   jax.ShapeDtypeStruct((batch, N_HEADS, LANES), F32)),
        scratch_shapes=[pltpu.VMEM((N_PAIRS, CT_ROWS, LANES), F32), pltpu.VMEM((N_HEADS, LANES), F32)],
        compiler_params=_cparams(("arbitrary", "arbitrary"), VMEM_LIMIT),
        name="mlstm_prompt",
    )(km, qmt, vmt, omt, gcol, grow, ghr)


def _mlstm_sample_kernel(n_tok, km_ref, qmt_ref, vmt_ref, omt_ref, gcol_ref, grow_ref, c0_ref, n0_ref, m0_ref,
                         m0t_ref, gh_ref, hm_ref, c_ref, n_ref, mt_ref):
    L = LANES
    NB = L // n_tok
    ti = lax.broadcasted_iota(jnp.int32, (L, L), 0)
    si = lax.broadcasted_iota(jnp.int32, (L, L), 1)
    same = (ti // n_tok) == (si // n_tok)
    causal = same & (ti >= si)
    useg = jnp.where(same & (ti <= si), 1.0, 0.0)
    slast = jnp.where(same & (ti % n_tok == n_tok - 1), 1.0, 0.0)
    expand = jnp.where(ti // n_tok == si, 1.0, 0.0)
    expand_t = jnp.where(ti == si // n_tok, 1.0, 0.0)
    pick = jnp.where((ti // n_tok == si) & (ti % n_tok == n_tok - 1), 1.0, 0.0)
    hdot = lambda a, b: jnp.dot(a, b, precision=HIGHEST, preferred_element_type=F32)

    lane128 = lax.broadcasted_iota(jnp.int32, (L, LANES), 1)
    even128 = lane128 < HEAD_DIM
    lane256 = lax.broadcasted_iota(jnp.int32, (1, 2 * LANES), 1)
    cols_e = (lane256 < HEAD_DIM) | (lane256 == LANES)
    cols_o = ((lane256 >= HEAD_DIM) & (lane256 < LANES)) | (lane256 == LANES + 1)
    rows_e = lax.broadcasted_iota(jnp.int32, (LANES, 1), 0) < HEAD_DIM
    ones_cols = jnp.where(lane128 < 2, 1.0, 0.0)
    bo_r = lax.broadcasted_iota(jnp.int32, (LANES, LANES), 0) // HEAD_DIM
    bo_c = lax.broadcasted_iota(jnp.int32, (LANES, LANES), 1) // HEAD_DIM
    block_ones = jnp.where(bo_r == bo_c, 1.0, 0.0)
    W = NB * LANES
    rb = lax.broadcasted_iota(jnp.int32, (L, W), 0)
    cb = lax.broadcasted_iota(jnp.int32, (L, W), 1)
    own_block = (rb // n_tok) == (cb // LANES)
    bd_tiled = (rb // HEAD_DIM) == ((cb % LANES) // HEAD_DIM)

    grow = grow_ref[...]
    i_row = grow[0:N_HEADS]
    b_row = hdot(grow[N_HEADS:2 * N_HEADS], useg)
    b_last = hdot(b_row, slast)
    a_row = b_last - b_row + i_row
    pos = lax.broadcasted_iota(jnp.int32, a_row.shape, 1) % n_tok
    pm = a_row
    sh = 1
    while sh < n_tok:
        pm = jnp.where(pos >= sh, jnp.maximum(pm, pltpu.roll(pm, sh, axis=1)), pm)
        sh *= 2
    m_carry = hdot(jnp.concatenate([m0t_ref[0], b_row], axis=1), jnp.concatenate([expand_t, slast], axis=0))
    m_new_row = jnp.maximum(m_carry, hdot(pm, slast))
    decay_row = jnp.exp(m_carry - m_new_row)
    w_row = jnp.exp(a_row - m_new_row)
    mt_ref[0] = hdot(m_new_row, pick)
    decay_bh = hdot(decay_row, pick)
    decay_hb = jnp.transpose(decay_bh)[0:NB]

    bc_all = gcol_ref[...]
    rowpos = lax.broadcasted_iota(jnp.int32, bc_all.shape, 0) % n_tok
    sh = 1
    while sh < n_tok:
        bc_all = bc_all + jnp.where(rowpos >= sh, pltpu.roll(bc_all, sh, axis=0), 0.0)
        sh *= 2
    pad_rows = lambda a: jnp.concatenate([a, jnp.zeros((L - NB, a.shape[1]), F32)], axis=0)
    m0_col = hdot(expand, pad_rows(m0_ref[...]))
    rowv_all = i_row - b_row

    for p in range(N_PAIRS):
        ls = slice(p * LANES, (p + 1) * LANES)
        q2f = jnp.transpose(qmt_ref[ls, :])
        q2 = q2f.astype(BF16)
        kt2 = jnp.transpose(km_ref[:, ls])
        v2 = jnp.transpose(vmt_ref[ls, :])
        vext = jnp.concatenate([v2, ones_cols], axis=1)
        kt_e = jnp.where(rows_e, kt2, 0.0).astype(BF16)
        kt_o = jnp.where(rows_e, 0.0, kt2).astype(BF16)
        s2 = jnp.dot(q2, jnp.concatenate([kt_e, kt_o], axis=1), preferred_element_type=F32)
        ps, mts, inters = [], [], []
        for hh in range(2):
            h = 2 * p + hh
            bcol = bc_all[:, N_HEADS + h:N_HEADS + h + 1]
            logd = jnp.where(causal, bcol + rowv_all[h:h + 1, :], NEG)
            m_inter = m0_col[:, h:h + 1] + bcol
            m_t = jnp.maximum(m_inter, jnp.max(logd, axis=1, keepdims=True))
            ps.append((s2[:, hh * L:(hh + 1) * L] * jnp.exp(logd - m_t)).astype(BF16))
            mts.append(m_t)
            inters.append(jnp.exp(m_inter - m_t))
        vstack = jnp.concatenate([jnp.where(cols_e, vext, 0.0), jnp.where(cols_o, vext, 0.0)],
                                 axis=0).astype(BF16)
        r = jnp.dot(jnp.concatenate(ps, axis=1), vstack, preferred_element_type=F32)
        zero = jnp.zeros((HEAD_DIM, HEAD_DIM), F32)
        cstack = jnp.concatenate(
            [jnp.concatenate([jnp.concatenate([c0_ref[b, 2 * p], zero], axis=1),
                              jnp.concatenate([zero, c0_ref[b, 2 * p + 1]], axis=1)], axis=0)
             for b in range(NB)], axis=1)
        rq_all = jnp.where(own_block, jnp.dot(q2, cstack.astype(BF16), preferred_element_type=F32), 0.0)
        rq = rq_all[:, 0:LANES]
        for b in range(1, NB):
            rq = rq + rq_all[:, b * LANES:(b + 1) * LANES]
        n_rows = hdot(expand, pad_rows(n0_ref[:, ls]))
        qn = hdot(q2f * n_rows, block_ones)
        inter2 = jnp.where(even128, inters[0], inters[1])
        num = r[:, 0:LANES] + inter2 * rq
        den = jnp.where(even128, r[:, LANES:LANES + 1], r[:, LANES + 1:LANES + 2]) + inter2 * qn
        mt2 = jnp.where(even128, mts[0], mts[1])
        hv = num / jnp.maximum(jnp.abs(den), jnp.exp(-mt2))
        ms = hdot(hv * hv, block_ones) * (1.0 / HEAD_DIM)
        y = hv * lax.rsqrt(ms + RMS_EPS) * gh_ref[:, ls] * _sigmoid(jnp.transpose(omt_ref[ls, :]))
        hm_ref[:, ls] = y.astype(hm_ref.dtype)
        kw = kt2 * jnp.where(rows_e, w_row[2 * p:2 * p + 1, :], w_row[2 * p + 1:2 * p + 2, :])
        vbd = jnp.where(own_block, jnp.concatenate([v2] * NB, axis=1), 0.0).astype(BF16)
        upd = jnp.dot(kw.astype(BF16), vbd, preferred_element_type=F32)
        upd = jnp.where(bd_tiled, upd, 0.0)
        for b in range(NB):
            bs = slice(b * LANES, (b + 1) * LANES)
            dec_b = jnp.where(rows_e, decay_hb[b:b + 1, 2 * p:2 * p + 1], decay_hb[b:b + 1, 2 * p + 1:2 * p + 2])
            cnew = dec_b * cstack[:, bs] + upd[:, bs]
            c_ref[b, 2 * p] = cnew[0:HEAD_DIM, 0:HEAD_DIM]
            c_ref[b, 2 * p + 1] = cnew[HEAD_DIM:LANES, HEAD_DIM:LANES]
        nsum = jnp.transpose(hdot(kw, expand))[0:NB]
        dec_n = jnp.where(lax.broadcasted_iota(jnp.int32, (NB, LANES), 1) < HEAD_DIM,
                          decay_hb[:, 2 * p:2 * p + 1], decay_hb[:, 2 * p + 1:2 * p + 2])
        n_ref[:, ls] = dec_n * n0_ref[:, ls] + nsum


def _mlstm_sample(km, qmt, vmt, omt, gcol, grow, c0, n0, m0, gh, row0, n_tok):
    nb = c0.shape[0]
    g_nb = LANES // n_tok
    n_g = nb // g_nb
    blk0 = row0 // LANES
    tokb = lambda w: pl.BlockSpec((LANES, w), lambda i: (blk0 + i, 0))
    rowb = lambda r: pl.BlockSpec((r, LANES), lambda i: (0, blk0 + i))
    m0t = jnp.pad(m0.reshape(n_g, g_nb, N_HEADS).transpose(0, 2, 1),
                  ((0, 0), (0, 0), (0, LANES - g_nb)))
    return pl.pallas_call(
        functools.partial(_mlstm_sample_kernel, n_tok),
        grid=(n_g,),
        in_specs=[tokb(DM), rowb(DM), rowb(DM), rowb(DM), tokb(LANES), rowb(2 * N_HEADS),
                  pl.BlockSpec((g_nb, N_HEADS, HEAD_DIM, HEAD_DIM), lambda i: (i, 0, 0, 0)),
                  pl.BlockSpec((g_nb, DM), lambda i: (i, 0)),
                  pl.BlockSpec((g_nb, N_HEADS), lambda i: (i, 0)),
                  pl.BlockSpec((1, N_HEADS, LANES), lambda i: (i, 0, 0)),
                  pl.BlockSpec((1, DM), lambda i: (0, 0))],
        out_specs=(pl.BlockSpec((LANES, DM), lambda i: (i, 0)),
                   pl.BlockSpec((g_nb, N_HEADS, HEAD_DIM, HEAD_DIM), lambda i: (i, 0, 0, 0)),
                   pl.BlockSpec((g_nb, DM), lambda i: (i, 0)),
                   pl.BlockSpec((1, N_HEADS, LANES), lambda i: (i, 0, 0))),
        out_shape=(jax.ShapeDtypeStruct((nb * n_tok, DM), BF16), jax.ShapeDtypeStruct(c0.shape, F32),
                   jax.ShapeDtypeStruct((nb, DM), F32), jax.ShapeDtypeStruct((n_g, N_HEADS, LANES), F32)),
        compiler_params=_cparams(("arbitrary",), VMEM_LIMIT),
        name="mlstm_sample",
    )(km, qmt, vmt, omt, gcol, grow, c0, n0.reshape(nb, DM), m0, m0t, gh)


def _alibi_slope(h):
    return float(np.float32(2.0 ** (-8.0 * (h + 1) / N_HEADS)))


def _dup_halves(x):
    lane = lax.broadcasted_iota(jnp.int32, x.shape, 1)
    xr = pltpu.roll(x, HEAD_DIM, axis=1)
    lo = lane < HEAD_DIM
    return jnp.where(lo, x, xr), jnp.where(lo, xr, x)


def _stack_group_queries(q, g):
    lane = lax.broadcasted_iota(jnp.int32, (q.shape[0], LANES), 1)
    parts = []
    for hh in range(GROUP):
        h = GROUP * g + hh
        blk = q[:, (h // 2) * LANES:(h // 2 + 1) * LANES]
        keep = (lane < HEAD_DIM) if h % 2 == 0 else (lane >= HEAD_DIM)
        parts.append(jnp.where(keep, blk, 0.0))
    return jnp.concatenate(parts, axis=0).astype(BF16)


def _swa_prompt_kernel(sink_ref, q_ref, kp_ref, ko_ref, vp_ref, vo_ref, o_ref):
    j = pl.program_id(1)
    R = WINDOW
    qi = lax.broadcasted_iota(jnp.int32, (R, R), 0)
    kj = lax.broadcasted_iota(jnp.int32, (R, R), 1)
    own = kj <= qi
    distf = jnp.where(own, qi - kj, qi - kj + R).astype(F32)
    lane = lax.broadcasted_iota(jnp.int32, (R, LANES), 1)
    for u in range(SWA_BLOCKS):
        rs = slice(u * R, (u + 1) * R)
        q = q_ref[rs, :]
        k_prev = kp_ref[...] if u == 0 else ko_ref[(u - 1) * R:u * R, :]
        v_prev = vp_ref[...] if u == 0 else vo_ref[(u - 1) * R:u * R, :]
        kd = _dup_halves(jnp.concatenate([k_prev, ko_ref[rs, :]], axis=0))
        vd = _dup_halves(jnp.concatenate([v_prev, vo_ref[rs, :]], axis=0))
        valid = (own | (j > 0)) if u == 0 else None
        outs = []
        for g in range(N_KV):
            qs = _stack_group_queries(q, g)
            s = lax.dot_general(qs, kd[g].astype(BF16), (((1,), (1,)), ((), ())),
                                preferred_element_type=F32) * (HEAD_DIM ** -0.5)
            ps = []
            for hh in range(GROUP):
                h = GROUP * g + hh
                sink = sink_ref[h]
                rows = slice(hh * R, (hh + 1) * R)
                sh = jnp.where(own, s[rows, R:], s[rows, :R]) - _alibi_slope(h) * distf
                if valid is not None:
                    sh = jnp.where(valid, sh, NEG)
                mx = jnp.maximum(jnp.max(sh, axis=1, keepdims=True), sink)
                p = jnp.exp(sh - mx)
                p = p / (jnp.sum(p, axis=1, keepdims=True) + jnp.exp(sink - mx))
                ps.append(jnp.concatenate([jnp.where(own, 0.0, p), jnp.where(own, p, 0.0)], axis=1).astype(BF16))
            o = jnp.dot(jnp.concatenate(ps, axis=0), vd[g].astype(BF16), preferred_element_type=F32)
            for pp in range(GROUP // 2):
                outs.append(jnp.where(lane < HEAD_DIM, o[(2 * pp) * R:(2 * pp + 1) * R],
                                      o[(2 * pp + 1) * R:(2 * pp + 2) * R]))
        o_ref[rs, :] = jnp.concatenate(outs, axis=1).astype(o_ref.dtype)


def _swa_prompt(sinks, qa, ka, va, batch, seq):
    nb = seq // (WINDOW * SWA_BLOCKS)
    own = lambda w: pl.BlockSpec((WINDOW * SWA_BLOCKS, w), lambda b, j: (b * nb + j, 0))
    prev = lambda w: pl.BlockSpec(
        (WINDOW, w), lambda b, j: (b * nb * SWA_BLOCKS + jnp.maximum(j * SWA_BLOCKS - 1, 0), 0))
    return pl.pallas_call(
        _swa_prompt_kernel,
        grid=(batch, nb),
        in_specs=[pl.BlockSpec(memory_space=pltpu.SMEM), own(DM), prev(DKV), own(DKV), prev(DKV), own(DKV)],
        out_specs=own(DM),
        out_shape=jax.ShapeDtypeStruct((batch * seq, DM), BF16),
        compiler_params=_cparams(("arbitrary", "arbitrary"), VMEM_LIMIT),
        name="swa_prompt",
    )(sinks, qa, ka, ka, va, va)


def _swa_decode_kernel(n_tok, sink_ref, q_ref, kn_ref, vn_ref, kc_ref, vc_ref, o_ref, ko_ref, vo_ref):
    W = WINDOW
    L = LANES
    NB = L // n_tok
    for b in range(NB):
        ko_ref[b, 0:W - n_tok, :] = kc_ref[b, n_tok:W, :]
        ko_ref[b, W - n_tok:W, :] = kn_ref[b * n_tok:(b + 1) * n_tok, :]
        vo_ref[b, 0:W - n_tok, :] = vc_ref[b, n_tok:W, :]
        vo_ref[b, W - n_tok:W, :] = vn_ref[b * n_tok:(b + 1) * n_tok, :]
    q = q_ref[...]
    knd, vnd = _dup_halves(kn_ref[...]), _dup_halves(vn_ref[...])
    kcd = _dup_halves(kc_ref[...].reshape(NB * W, DKV))
    vcd = _dup_halves(vc_ref[...].reshape(NB * W, DKV))
    ri = lax.broadcasted_iota(jnp.int32, (L, L), 0)
    ci = lax.broadcasted_iota(jnp.int32, (L, L), 1)
    t_q = ri % n_tok
    dist_n = t_q - ci % n_tok
    valid_n = ((ri // n_tok) == (ci // n_tok)) & (dist_n >= 0)
    dist_c = t_q + W - ci
    valid_c = dist_c < WINDOW
    dnf = dist_n.astype(F32)
    dcf = dist_c.astype(F32)
    own1 = (lax.broadcasted_iota(jnp.int32, (L, NB * W), 0) // n_tok
            == lax.broadcasted_iota(jnp.int32, (L, NB * W), 1) // W)
    own4 = ((lax.broadcasted_iota(jnp.int32, (GROUP * L, NB * W), 0) % L) // n_tok
            == lax.broadcasted_iota(jnp.int32, (GROUP * L, NB * W), 1) // W)
    lane = lax.broadcasted_iota(jnp.int32, (L, LANES), 1)
    nt = (((1,), (1,)), ((), ()))
    outs = []
    for g in range(N_KV):
        qs = _stack_group_queries(q, g)
        sn = lax.dot_general(qs, knd[g].astype(BF16), nt, preferred_element_type=F32) * (HEAD_DIM ** -0.5)
        sc_all = lax.dot_general(qs, kcd[g].astype(BF16), nt, preferred_element_type=F32) * (HEAD_DIM ** -0.5)
        pcs, pns = [], []
        for hh in range(GROUP):
            h = GROUP * g + hh
            sink = sink_ref[h]
            slope = _alibi_slope(h)
            rows = slice(hh * L, (hh + 1) * L)
            blk = jnp.where(own1, sc_all[rows], 0.0)
            sc = blk[:, 0:W]
            for b in range(1, NB):
                sc = sc + blk[:, b * W:(b + 1) * W]
            shc = jnp.where(valid_c, sc - slope * dcf, NEG)
            shn = jnp.where(valid_n, sn[rows] - slope * dnf, NEG)
            mx = jnp.maximum(jnp.maximum(jnp.max(shc, axis=1, keepdims=True),
                                         jnp.max(shn, axis=1, keepdims=True)), sink)
            pc = jnp.exp(shc - mx)
            pn = jnp.exp(shn - mx)
            inv = 1.0 / (jnp.sum(pc, axis=1, keepdims=True) + jnp.sum(pn, axis=1, keepdims=True)
                         + jnp.exp(sink - mx))
            pcs.append(pc * inv)
            pns.append((pn * inv).astype(BF16))
        pc4 = jnp.concatenate(pcs, axis=0)
        p_bd = jnp.where(own4, jnp.concatenate([pc4] * NB, axis=1), 0.0).astype(BF16)
        o = (jnp.dot(p_bd, vcd[g].astype(BF16), preferred_element_type=F32)
             + jnp.dot(jnp.concatenate(pns, axis=0), vnd[g].astype(BF16), preferred_element_type=F32))
        for pp in range(GROUP // 2):
            outs.append(jnp.where(lane < HEAD_DIM, o[(2 * pp) * L:(2 * pp + 1) * L],
                                  o[(2 * pp + 1) * L:(2 * pp + 2) * L]))
    o_ref[...] = jnp.concatenate(outs, axis=1)


def _swa_decode(sinks, qa, ka, va, kc, vc, row0, n_tok):
    nb = kc.shape[0]
    g_nb = LANES // n_tok
    blk0 = row0 // LANES
    tokb = lambda w: pl.BlockSpec((LANES, w), lambda i: (blk0 + i, 0))
    cache = pl.BlockSpec((g_nb, WINDOW, DKV), lambda i: (i, 0, 0))
    return pl.pallas_call(
        functools.partial(_swa_decode_kernel, n_tok),
        grid=(nb // g_nb,),
        in_specs=[pl.BlockSpec(memory_space=pltpu.SMEM), tokb(DM), tokb(DKV), tokb(DKV), cache, cache],
        out_specs=(pl.BlockSpec((LANES, DM), lambda i: (i, 0)), cache, cache),
        out_shape=(jax.ShapeDtypeStruct((nb * n_tok, DM), F32),
                   jax.ShapeDtypeStruct(kc.shape, F32), jax.ShapeDtypeStruct(vc.shape, F32)),
        compiler_params=_cparams(("arbitrary",), VMEM_LIMIT),
        name="swa_decode",
    )(sinks, qa, ka, va, kc, vc)


def _outproj_router_kernel(n_ptiles, xp_ref, xs_ref, hmp_ref, hap_ref, hms_ref, has_ref, wom_ref, woa_ref,
                           g_ref, wrt_ref, br_ref,
                           x1_ref, xg_ref, meta_ref, cnt_ref):
    i = pl.program_id(0)
    is_p = i < n_ptiles
    x = jnp.where(is_p, xp_ref[...], xs_ref[...])
    hm = jnp.where(is_p, hmp_ref[...], hms_ref[...].astype(BF16))
    ha = jnp.where(is_p, hap_ref[...], has_ref[...].astype(BF16))
    x1 = (x + jnp.dot(hm, wom_ref[...], preferred_element_type=F32)
          + jnp.dot(ha, woa_ref[...], preferred_element_type=F32))
    x1_ref[...] = x1
    h2 = _rms(x1, g_ref[...])
    nt = (((1,), (1,)), ((), ()))
    h2_hi = h2.astype(BF16)
    h2_lo = (h2 - h2_hi.astype(F32)).astype(BF16)
    wr = wrt_ref[...]
    wr_hi = wr.astype(BF16)
    wr_lo = (wr - wr_hi.astype(F32)).astype(BF16)
    logits = (lax.dot_general(wr_hi, h2_hi, nt, preferred_element_type=F32)
              + lax.dot_general(wr_lo, h2_hi, nt, preferred_element_type=F32)
              + lax.dot_general(wr_hi, h2_lo, nt, preferred_element_type=F32)) + br_ref[...]
    eidx = lax.broadcasted_iota(jnp.int32, logits.shape, 0).astype(F32)
    work = logits
    vals, hots = [], []
    for _ in range(TOP_K):
        mv = jnp.max(work, axis=0, keepdims=True)
        sel = jnp.min(jnp.where(work == mv, eidx, float(N_EXPERTS)), axis=0, keepdims=True)
        hot = eidx == sel
        vals.append(mv)
        hots.append(hot)
        work = jnp.where(hot, -jnp.inf, work)
    es = [jnp.exp(v - vals[0]) for v in vals]
    tot = es[0] + es[1] + es[2] + es[3]
    gates = [e / tot for e in es]
    hot_all = jnp.where(hots[0] | hots[1] | hots[2] | hots[3], 1.0, 0.0)
    tm = logits.shape[1]
    su = (lax.broadcasted_iota(jnp.int32, (tm, tm), 0) < lax.broadcasted_iota(jnp.int32, (tm, tm), 1))
    cum = jnp.dot(hot_all.astype(BF16), su.astype(BF16), preferred_element_type=F32)
    cnt = jnp.sum(hot_all, axis=1, keepdims=True)
    cpad = (((cnt.astype(jnp.int32) + (SUBLANES - 1)) // SUBLANES) * SUBLANES).astype(F32)
    lower = (lax.broadcasted_iota(jnp.int32, (N_EXPERTS, N_EXPERTS), 0)
             > lax.broadcasted_iota(jnp.int32, (N_EXPERTS, N_EXPERTS), 1)).astype(F32)
    lstart = jnp.dot(lower, jnp.broadcast_to(cpad, (N_EXPERTS, LANES)), precision=HIGHEST,
                     preferred_element_type=F32)[:, 0:1]
    base = lstart + cum
    lpos = [jnp.sum(jnp.where(hot, base, 0.0), axis=0, keepdims=True) for hot in hots]
    lpi = [p.astype(jnp.int32) for p in lpos]
    h2b = h2.astype(BF16)
    rows_per = GROUP_R // SORT_CHUNKS
    for j in range(SORT_CHUNKS):
        r_iota = lax.broadcasted_iota(jnp.int32, (rows_per, tm), 0) + j * rows_per
        sel01 = jnp.where(r_iota == lpi[0], 1.0, jnp.where(r_iota == lpi[1], 1.0, jnp.where(
            r_iota == lpi[2], 1.0, jnp.where(r_iota == lpi[3], 1.0, 0.0)))).astype(BF16)
        xg_ref[j * rows_per:(j + 1) * rows_per, :] = jnp.dot(sel01, h2b, preferred_element_type=F32)
    meta_ref[...] = jnp.transpose(jnp.concatenate(gates + lpos, axis=0))
    cnt_ref[0] = jnp.broadcast_to(cnt, (N_EXPERTS, LANES))


def _outproj_router(xp, xs, hmp, hap, hms, has, wom, woa, g_ffn, wrt, br):
    tp, ts = xp.shape[0], xs.shape[0]
    n_pt, n_st = tp // ROUTE_T, ts // ROUTE_T
    t_all = tp + ts
    pblk = lambda w: pl.BlockSpec((ROUTE_T, w), lambda i: (jnp.minimum(i, n_pt - 1), 0))
    sblk = lambda w: pl.BlockSpec((ROUTE_T, w), lambda i: (jnp.maximum(i - n_pt, 0), 0))
    full = lambda a: pl.BlockSpec(a.shape, lambda i: (0,) * a.ndim)
    return pl.pallas_call(
        functools.partial(_outproj_router_kernel, n_pt),
        grid=(n_pt + n_st,),
        in_specs=[pblk(D_MODEL), sblk(D_MODEL), pblk(DM), pblk(DM), sblk(DM), sblk(DM),
                  full(wom), full(woa), full(g_ffn), full(wrt), full(br)],
        out_specs=(pl.BlockSpec((ROUTE_T, D_MODEL), lambda i: (i, 0)),
                   pl.BlockSpec((GROUP_R, D_MODEL), lambda i: (i, 0)),
                   pl.BlockSpec((ROUTE_T, 2 * TOP_K), lambda i: (i, 0)),
                   pl.BlockSpec((1, N_EXPERTS, LANES), lambda i: (i, 0, 0))),
        out_shape=(jax.ShapeDtypeStruct((t_all, D_MODEL), F32),
                   jax.ShapeDtypeStruct(((n_pt + n_st) * GROUP_R, D_MODEL), F32),
                   jax.ShapeDtypeStruct((t_all, 2 * TOP_K), F32),
                   jax.ShapeDtypeStruct((n_pt + n_st, N_EXPERTS, LANES), F32)),
        compiler_params=_cparams(("arbitrary",), VMEM_LIMIT),
        name="outproj_router",
    )(xp, xs, hmp, hap, hms, has, wom, woa, g_ffn, wrt, br)


def _expert_kernel(be_ref, na_ref, slot_ref, nxt_ref, ctab_ref, ctab1_ref, ctab2_ref, xg_ref, wgu_ref, bgu_ref,
                   wd_ref, bd_ref, o_ref, xbuf, wgu_f, wd_f, wgu_s, wd_s, xsem, wsem):
    i = pl.program_id(0)
    na = na_ref[0]

    def x_copies(tab_ref, slot):
        return [pltpu.make_async_copy(xg_ref.at[pl.ds(pl.multiple_of(tab_ref[0, 0, c], SUBLANES), SUBLANES)],
                                      xbuf.at[slot, pl.ds(c * SUBLANES, SUBLANES)], xsem.at[slot])
                for c in range(MOE_BM // SUBLANES)]

    def w_copies(e, slot):
        return [pltpu.make_async_copy(wgu_ref.at[e], wgu_f.at[slot], wsem.at[slot]),
                pltpu.make_async_copy(wd_ref.at[e], wd_f.at[slot], wsem.at[slot])]

    @pl.when(i == 0)
    def _():
        for cp in x_copies(ctab_ref, 0) + x_copies(ctab1_ref, 1) + w_copies(be_ref[0], 0):
            cp.start()

    @pl.when(i == na)
    def _():
        for cp in x_copies(ctab_ref, lax.rem(i, X_SLOTS)) + x_copies(ctab1_ref, lax.rem(i + 1, X_SLOTS)):
            cp.wait()

    @pl.when(i < na)
    def _():
        changed = (i == 0) | (be_ref[i] != be_ref[jnp.maximum(i - 1, 0)])
        wslot = slot_ref[i]
        xslot = lax.rem(i, X_SLOTS)

        @pl.when(changed)
        def _():
            for cp in w_copies(be_ref[i], wslot):
                cp.wait()

            @pl.when(nxt_ref[i] >= 0)
            def _():
                for cp in w_copies(nxt_ref[i], 1 - wslot):
                    cp.start(priority=1)

            wgu_s[...] = wgu_f[wslot].astype(BF16)
            wd_s[...] = wd_f[wslot].astype(BF16)

        for cp in x_copies(ctab_ref, xslot):
            cp.wait()

        x = xbuf[xslot].astype(BF16)
        gu = jnp.dot(x, wgu_s[...], preferred_element_type=F32) + bgu_ref[...]
        for cp in x_copies(ctab2_ref, lax.rem(i + 2, X_SLOTS)):
            cp.start()
        gate = jnp.minimum(gu[:, :D_FF], SWIGLU_LIMIT)
        up = jnp.clip(gu[:, D_FF:], -SWIGLU_LIMIT, SWIGLU_LIMIT)
        act = (up + 1.0) * (gate * _sigmoid(gate * SWIGLU_ALPHA))
        o_ref[...] = jnp.dot(act.astype(BF16), wd_s[...], preferred_element_type=F32) + bd_ref[...]


def _expert_ffn(block_exp, n_active, wslot, next_exp, ctab, xg, wgu, bgu, wd, bd):
    n_blocks = ctab.shape[0]
    n_rows = n_blocks * MOE_BM
    nch = MOE_BM // SUBLANES
    tab = lambda d: pl.BlockSpec((1, 1, nch), lambda i, *_: (jnp.minimum(i + d, n_blocks - 1), 0, 0),
                                 memory_space=pltpu.SMEM)
    blk_e = lambda i, be: be[jnp.minimum(i, n_blocks - 1)]
    grid_spec = pltpu.PrefetchScalarGridSpec(
        num_scalar_prefetch=4,
        grid=(n_blocks + 1,),
        in_specs=[tab(0), tab(1), tab(2),
                  pl.BlockSpec(memory_space=pl.ANY),
                  pl.BlockSpec(memory_space=pl.ANY),
                  pl.BlockSpec((None, 1, 2 * D_FF), lambda i, be, *_: (blk_e(i, be), 0, 0)),
                  pl.BlockSpec(memory_space=pl.ANY),
                  pl.BlockSpec((None, 1, D_MODEL), lambda i, be, *_: (blk_e(i, be), 0, 0))],
        out_specs=pl.BlockSpec((MOE_BM, D_MODEL), lambda i, be, na, *_: (jnp.minimum(i, na[0] - 1), 0)),
        scratch_shapes=[pltpu.VMEM((X_SLOTS, MOE_BM, D_MODEL), F32),
                        pltpu.VMEM((2, D_MODEL, 2 * D_FF), F32), pltpu.VMEM((2, D_FF, D_MODEL), F32),
                        pltpu.VMEM((D_MODEL, 2 * D_FF), BF16), pltpu.VMEM((D_FF, D_MODEL), BF16),
                        pltpu.SemaphoreType.DMA((X_SLOTS,)), pltpu.SemaphoreType.DMA((2,))],
    )
    return pl.pallas_call(
        _expert_kernel,
        grid_spec=grid_spec,
        out_shape=jax.ShapeDtypeStruct((n_rows, D_MODEL), F32),
        compiler_params=_cparams(("arbitrary",), VMEM_LIMIT),
        name="moe_experts",
    )(block_exp, n_active, wslot, next_exp, ctab, ctab, ctab, xg, wgu, bgu, wd, bd)


def _combine_kernel(n_ptiles, n_tiles, ctab_ref, ctab1_ref, ctab2_ref, outs_ref, x1_ref, meta_ref, gf_ref,
                    yp_ref, ys_ref, obuf, sem):
    i = pl.program_id(0)

    def copies(tab_ref, s):
        return [pltpu.make_async_copy(outs_ref.at[pl.ds(pl.multiple_of(tab_ref[0, 0, c], SUBLANES), SUBLANES)],
                                      obuf.at[s, pl.ds(c * SUBLANES, SUBLANES)], sem.at[s])
                for c in range(GROUP_R // SUBLANES)]

    @pl.when(i == 0)
    def _():
        for cp in copies(ctab_ref, 0) + copies(ctab1_ref, 1):
            cp.start()

    @pl.when(i == n_tiles)
    def _():
        for cp in copies(ctab_ref, lax.rem(i, X_SLOTS)) + copies(ctab1_ref, lax.rem(i + 1, X_SLOTS)):
            cp.wait()

    @pl.when(i < n_tiles)
    def _():
        _combine_tile(i, n_ptiles, copies, ctab_ref, ctab2_ref, x1_ref, meta_ref, gf_ref, yp_ref, ys_ref, obuf)


def _combine_tile(i, n_ptiles, copies, ctab_ref, ctab2_ref, x1_ref, meta_ref, gf_ref, yp_ref, ys_ref, obuf):
    slot = lax.rem(i, X_SLOTS)
    for cp in copies(ctab_ref, slot):
        cp.wait()
    for cp in copies(ctab2_ref, lax.rem(i + 2, X_SLOTS)):
        cp.start()

    meta = meta_ref[...]
    tm = meta.shape[0]
    r_iota = lax.broadcasted_iota(jnp.int32, (tm, GROUP_R), 1)
    lp = [meta[:, TOP_K + k:TOP_K + k + 1].astype(jnp.int32) for k in range(TOP_K)]
    gk = [meta[:, k:k + 1] for k in range(TOP_K)]
    gsel = jnp.where(r_iota == lp[0], gk[0], jnp.where(r_iota == lp[1], gk[1], jnp.where(
        r_iota == lp[2], gk[2], jnp.where(r_iota == lp[3], gk[3], 0.0))))
    sel01 = jnp.where(gsel != 0.0, 1.0, 0.0).astype(BF16)
    rg_row = jnp.sum(gsel, axis=0, keepdims=True)
    rg_col = jnp.transpose(jnp.broadcast_to(rg_row, (SUBLANES, GROUP_R)))[:, 0:1]
    og = (obuf[slot] * rg_col).astype(BF16)
    acc = x1_ref[...] + jnp.dot(sel01, og, preferred_element_type=F32)
    y = _rms(acc, gf_ref[...])

    @pl.when(i < n_ptiles)
    def _():
        yp_ref[...] = y

    @pl.when(i >= n_ptiles)
    def _():
        ys_ref[...] = y


def _combine(ctab, outs, x1, meta, g_final, tp, ts):
    n_pt, n_st = tp // ROUTE_T, ts // ROUTE_T
    n = n_pt + n_st
    nch = GROUP_R // SUBLANES
    last = lambda i: jnp.minimum(i, n - 1)
    tab = lambda d: pl.BlockSpec((1, 1, nch), lambda i: (last(i + d), 0, 0), memory_space=pltpu.SMEM)
    return pl.pallas_call(
        functools.partial(_combine_kernel, n_pt, n),
        grid=(n + 1,),
        in_specs=[tab(0), tab(1), tab(2),
                  pl.BlockSpec(memory_space=pl.ANY),
                  pl.BlockSpec((ROUTE_T, D_MODEL), lambda i: (last(i), 0)),
                  pl.BlockSpec((ROUTE_T, 2 * TOP_K), lambda i: (last(i), 0)),
                  pl.BlockSpec((1, D_MODEL), lambda i: (0, 0))],
        out_specs=(pl.BlockSpec((ROUTE_T, D_MODEL), lambda i: (jnp.minimum(i, n_pt - 1), 0)),
                   pl.BlockSpec((ROUTE_T, D_MODEL), lambda i: (jnp.maximum(last(i) - n_pt, 0), 0))),
        out_shape=(jax.ShapeDtypeStruct((tp, D_MODEL), F32), jax.ShapeDtypeStruct((ts, D_MODEL), F32)),
        scratch_shapes=[pltpu.VMEM((X_SLOTS, GROUP_R, D_MODEL), F32), pltpu.SemaphoreType.DMA((X_SLOTS,))],
        compiler_params=_cparams(("arbitrary",), VMEM_LIMIT),
        name="moe_combine",
    )(ctab, ctab, ctab, outs, x1, meta, g_final)


def kernel(x_prompt, x_sample, cache_swa_k, cache_swa_v, state_mlstm_c, state_mlstm_n, state_mlstm_m,
           g_mix, w_in, b_igate, b_fgate, g_head, attn_sinks, w_out, g_ffn, w_router, b_router,
           w_gate_up, b_gate_up, w_down, b_down, g_final):
    assert w_in.shape[0] == 1, "single-layer problem"
    B, S, _ = x_prompt.shape
    Bd, Tn, _ = x_sample.shape
    tp, ts = B * S, Bd * Tn
    t_all = tp + ts
    xp = x_prompt.reshape(tp, D_MODEL)
    xs = x_sample.reshape(ts, D_MODEL)

    w = w_in[0]
    o = np.cumsum([0, DM, DM, DM, DM, N_HEADS, N_HEADS, DM, DKV, DKV])
    col = lambda a: w[:, int(o[a]):int(o[a + 1])]
    wgates = jnp.concatenate([col(4), col(5)], axis=1)
    w1 = jnp.concatenate([col(1), col(6), col(7), col(8), jnp.pad(wgates, ((0, 0), (0, LANES - 2 * N_HEADS)))],
                         axis=1).astype(BF16)
    wkt = jnp.concatenate([col(0), col(2), col(3), wgates], axis=1).T.astype(BF16)
    bg = jnp.concatenate([b_igate[0], b_fgate[0]]).astype(F32)
    bcol = jnp.pad(bg, (0, LANES - 2 * N_HEADS)).reshape(1, LANES)
    brow = bg.reshape(2 * N_HEADS, 1)

    km, qa, ka, va, qmt, vmt, omt, gcol, grow = _inproj(xp, xs, g_mix[0].reshape(1, D_MODEL), w1, wkt, bcol, brow)

    gh = g_head[0].astype(F32)
    sinks = attn_sinks[0].astype(F32)

    hm_p, ctp, m_p = _mlstm_prompt(km, qmt, vmt, omt, gcol, grow,
                                   jnp.broadcast_to(gh.reshape(DM, 1), (DM, LANES)), B, S)
    ha_p = _swa_prompt(sinks, qa, ka, va, B, S)

    hm_s, c_s, n_s, mt_s = _mlstm_sample(km, qmt, vmt, omt, gcol, grow, state_mlstm_c[0], state_mlstm_n[0],
                                         state_mlstm_m[0], gh.reshape(1, DM), tp, Tn)
    n_s = n_s.reshape(Bd, N_HEADS, HEAD_DIM)
    m_s = mt_s[:, :, :LANES // Tn].transpose(0, 2, 1).reshape(Bd, N_HEADS)
    ha_s, k_s, v_s = _swa_decode(sinks, qa, ka, va, cache_swa_k[0].reshape(Bd, WINDOW, DKV),
                                 cache_swa_v[0].reshape(Bd, WINDOW, DKV), tp, Tn)

    wo = w_out[0].astype(BF16)
    x1, xg, meta, cnt = _outproj_router(
        xp, xs, hm_p, ha_p, hm_s, ha_s, wo[:DM], wo[DM:], g_ffn[0].reshape(1, D_MODEL),
        w_router[0].T, b_router[0].reshape(N_EXPERTS, 1))

    i32 = jnp.int32
    n_tiles = t_all // ROUTE_T
    max_rows = t_all * TOP_K + n_tiles * N_EXPERTS * (SUBLANES - 1) + N_EXPERTS * (MOE_BM - 1)
    n_blocks = -(-max_rows // MOE_BM)
    cpad = (cnt[:, :, 0].astype(i32) + (SUBLANES - 1)) // SUBLANES * SUBLANES
    lstart = jnp.cumsum(cpad, axis=1) - cpad
    goff = jnp.cumsum(cpad, axis=0) - cpad
    padded = (jnp.sum(cpad, axis=0) + MOE_BM - 1) // MOE_BM * MOE_BM
    pad_end = jnp.cumsum(padded)
    seg_begin = (pad_end - padded)[None, :] + goff
    n_active = (pad_end[-1] // MOE_BM).astype(i32)
    blk = jnp.minimum(jnp.arange(n_blocks, dtype=i32), n_active - 1)
    block_exp = jnp.minimum(jnp.sum((pad_end[None, :] <= (blk * MOE_BM)[:, None]).astype(i32), axis=1),
                            N_EXPERTS - 1)
    e_ids = jnp.arange(N_EXPERTS, dtype=i32)
    nonempty = padded > 0
    nxt_e = jnp.min(jnp.where((e_ids[None, :] > e_ids[:, None]) & nonempty[None, :], e_ids[None, :], N_EXPERTS),
                    axis=1)
    nxt_e = jnp.where(nxt_e == N_EXPERTS, -1, nxt_e)
    ord_e = jnp.cumsum(nonempty.astype(i32)) - 1
    be_hot = block_exp[:, None] == e_ids[None, :]
    next_exp = jnp.sum(jnp.where(be_hot, nxt_e[None, :], 0), axis=1).astype(i32)
    wslot = (jnp.sum(jnp.where(be_hot, ord_e[None, :], 0), axis=1) % 2).astype(i32)

    seg_src = jnp.arange(n_tiles, dtype=i32)[:, None] * GROUP_R + lstart
    sb, sl, ss = seg_begin.reshape(-1), cpad.reshape(-1), seg_src.reshape(-1)
    rc = jnp.arange(n_blocks * MOE_BM // SUBLANES, dtype=i32)[:, None] * SUBLANES
    inseg = (sb[None, :] <= rc) & (rc < (sb + sl)[None, :])
    ctab_e = jnp.where(jnp.any(inseg, axis=1), jnp.sum(jnp.where(inseg, (ss - sb)[None, :] + rc, 0), axis=1),
                       GROUP_R - SUBLANES)
    lr = jnp.arange(GROUP_R // SUBLANES, dtype=i32)[None, :, None] * SUBLANES
    inl = (lstart[:, None, :] <= lr) & (lr < (lstart + cpad)[:, None, :])
    ctab_c = jnp.sum(jnp.where(inl, (seg_begin - lstart)[:, None, :] + lr, 0), axis=2)

    outs = _expert_ffn(block_exp, n_active.reshape(1), wslot, next_exp,
                       ctab_e.astype(i32).reshape(n_blocks, 1, MOE_BM // SUBLANES), xg, w_gate_up[0],
                       b_gate_up[0].reshape(N_EXPERTS, 1, 2 * D_FF), w_down[0],
                       b_down[0].reshape(N_EXPERTS, 1, D_MODEL))
    y_p, y_s = _combine(ctab_c.astype(i32).reshape(n_tiles, 1, GROUP_R // SUBLANES), outs, x1, meta,
                        g_final.reshape(1, D_MODEL), tp, ts)

    kv_tail = lambda a: jnp.concatenate([a[(b + 1) * S - WINDOW:(b + 1) * S] for b in range(B)], axis=0).reshape(
        1, B, WINDOW, N_KV, HEAD_DIM)
    c_e = ctp[:, :, :HEAD_DIM, :HEAD_DIM]
    c_o = ctp[:, :, HEAD_DIM:LANES, HEAD_DIM:]
    c_p = jnp.swapaxes(jnp.stack([c_e, c_o], axis=2), -1, -2).reshape(B, N_HEADS, HEAD_DIM, HEAD_DIM)
    n_p = (ctp[:, :, LANES:LANES + 2, :HEAD_DIM] + ctp[:, :, LANES:LANES + 2, HEAD_DIM:]).reshape(B, N_HEADS, HEAD_DIM)
    return (y_p.reshape(B, S, D_MODEL), y_s.reshape(Bd, Tn, D_MODEL),
            kv_tail(ka), kv_tail(va), c_p[None], n_p[None], m_p[:, :, 0][None],
            k_s.reshape(Bd, WINDOW, N_KV, HEAD_DIM)[None], v_s.reshape(Bd, WINDOW, N_KV, HEAD_DIM)[None],
            c_s[None], n_s[None], m_s[None])
```

```python
import functools

import jax
import jax.numpy as jnp
import numpy as np
from jax import lax
from jax.experimental import pallas as pl
from jax.experimental.pallas import tpu as pltpu

F32 = jnp.float32
BF16 = jnp.bfloat16
HIGHEST = lax.Precision.HIGHEST

D_MODEL = 1024
HEAD_DIM = 64
N_HEADS = 8
N_PAIRS = N_HEADS // 2
N_KV = 2
GROUP = N_HEADS // N_KV
WINDOW = 128
N_EXPERTS = 32
TOP_K = 4
D_FF = 1024
SWIGLU_LIMIT = 7.0
SWIGLU_ALPHA = 1.702
RMS_EPS = 1e-5
DM = N_HEADS * HEAD_DIM
DKV = N_KV * HEAD_DIM
NEG = -1e30

LANES = 128
SUBLANES = 8
VMEM_LIMIT = 56 * 1024 * 1024

TM = 512
ROUTE_T = 512
MLSTM_TL = 512
MLSTM_L = 128
SWA_BLOCKS = 8
MOE_BM = 256
X_SLOTS = 3
GROUP_R = -(-(TOP_K * ROUTE_T + N_EXPERTS * (SUBLANES - 1) + SUBLANES) // LANES) * LANES
SORT_CHUNKS = 3


def _cparams(sem, vmem=None):
    return pltpu.CompilerParams(dimension_semantics=sem, vmem_limit_bytes=vmem)


def _rms(x, g):
    return x * lax.rsqrt(jnp.mean(x * x, axis=-1, keepdims=True) + RMS_EPS) * g


def _log_sigmoid(z):
    return jnp.minimum(z, 0.0) - jnp.log(1.0 + jnp.exp(-jnp.abs(z)))


def _sigmoid(z):
    return 1.0 / (1.0 + jnp.exp(-z))


def _inproj_kernel(n_ptiles, xp_ref, xs_ref, g_ref, w1_ref, wt_ref, bcol_ref, brow_ref,
                   km_ref, qa_ref, ka_ref, va_ref, qmt_ref, vmt_ref, omt_ref, gcol_ref, grow_ref):
    i = pl.program_id(0)
    x = jnp.where(i < n_ptiles, xp_ref[...], xs_ref[...])
    h = _rms(x, g_ref[...]).astype(BF16)
    main = jnp.dot(h, w1_ref[...], preferred_element_type=F32)
    km_ref[...] = main[:, 0:DM] * (HEAD_DIM ** -0.5)
    qa_ref[...] = main[:, DM:2 * DM]
    ka_ref[...] = main[:, 2 * DM:2 * DM + DKV]
    va_ref[...] = main[:, 2 * DM + DKV:2 * DM + 2 * DKV]
    t = lax.dot_general(wt_ref[...], h, (((1,), (1,)), ((), ())), preferred_element_type=F32)
    qmt_ref[...] = t[0:DM]
    vmt_ref[...] = t[DM:2 * DM]
    omt_ref[...] = t[2 * DM:3 * DM]
    zc = main[:, 2 * DM + 2 * DKV:] + bcol_ref[...]
    lane = lax.broadcasted_iota(jnp.int32, zc.shape, 1)
    gcol_ref[...] = jnp.where(lane < N_HEADS, zc, _log_sigmoid(zc))
    zr = t[3 * DM:] + brow_ref[...]
    row = lax.broadcasted_iota(jnp.int32, zr.shape, 0)
    grow_ref[...] = jnp.where(row < N_HEADS, zr, _log_sigmoid(zr))


def _inproj(xp, xs, g_mix, w1, wkt, bcol, brow):
    tp, ts = xp.shape[0], xs.shape[0]
    n_pt, n_st = tp // TM, ts // TM
    t_all = tp + ts
    tok = lambda w: pl.BlockSpec((TM, w), lambda i: (i, 0))
    tr = lambda r: pl.BlockSpec((r, TM), lambda i: (0, i))
    full = lambda a: pl.BlockSpec(a.shape, lambda i: (0,) * a.ndim)
    out_shape = (
        jax.ShapeDtypeStruct((t_all, DM), F32), jax.ShapeDtypeStruct((t_all, DM), F32),
        jax.ShapeDtypeStruct((t_all, DKV), F32), jax.ShapeDtypeStruct((t_all, DKV), F32),
        jax.ShapeDtypeStruct((DM, t_all), F32), jax.ShapeDtypeStruct((DM, t_all), F32),
        jax.ShapeDtypeStruct((DM, t_all), F32),
        jax.ShapeDtypeStruct((t_all, LANES), F32), jax.ShapeDtypeStruct((2 * N_HEADS, t_all), F32),
    )
    return pl.pallas_call(
        functools.partial(_inproj_kernel, n_pt),
        grid=(n_pt + n_st,),
        in_specs=[
            pl.BlockSpec((TM, D_MODEL), lambda i: (jnp.minimum(i, n_pt - 1), 0)),
            pl.BlockSpec((TM, D_MODEL), lambda i: (jnp.maximum(i - n_pt, 0), 0)),
            full(g_mix), full(w1), full(wkt), full(bcol), full(brow),
        ],
        out_specs=(tok(DM), tok(DM), tok(DKV), tok(DKV), tr(DM), tr(DM), tr(DM),
                   tok(LANES), tr(2 * N_HEADS)),
        out_shape=out_shape,
        compiler_params=_cparams(("arbitrary",), VMEM_LIMIT),
        name="inproj",
    )(xp, xs, g_mix, w1, wkt, bcol, brow)


CT_ROWS = LANES + 2 * SUBLANES


def _cumsum_rows(x, n):
    row = lax.broadcasted_iota(jnp.int32, x.shape, 0)
    sh = 1
    while sh < n:
        x = x + jnp.where(row >= sh, pltpu.roll(x, sh, axis=0), 0.0)
        sh *= 2
    return x


def _mlstm_prompt_kernel(km_ref, qmt_ref, vmt_ref, omt_ref, gcol_ref, grow_ref, ghr_ref,
                         hm_ref, ct_ref, m_ref, ct_s, m_s):
    j = pl.program_id(1)
    L = MLSTM_L
    assert L == LANES

    @pl.when(j == 0)
    def _():
        ct_s[...] = jnp.zeros_like(ct_s)
        m_s[...] = jnp.zeros_like(m_s)

    si = lax.broadcasted_iota(jnp.int32, (L, L), 0)
    ti = lax.broadcasted_iota(jnp.int32, (L, L), 1)
    causal_t = si <= ti
    upper = jnp.where(causal_t, 1.0, 0.0)
    rows_c = lax.broadcasted_iota(jnp.int32, (CT_ROWS, 1), 0)
    rmask_e = (rows_c < HEAD_DIM) | (rows_c == LANES)
    rmask_o = ((rows_c >= HEAD_DIM) & (rows_c < LANES)) | (rows_c == LANES + 1)
    rows_e = lax.broadcasted_iota(jnp.int32, (LANES, 1), 0) < HEAD_DIM
    cols_e = lax.broadcasted_iota(jnp.int32, (1, LANES), 1) < HEAD_DIM
    bdt_mask = (rmask_e & cols_e) | (rmask_o & (~cols_e))
    ones_rows = jnp.where(lax.broadcasted_iota(jnp.int32, (CT_ROWS - LANES, L), 0) < 2, 1.0, 0.0)

    n_chunks = MLSTM_TL // L
    b_rows = jnp.dot(jnp.concatenate([grow_ref[N_HEADS:2 * N_HEADS, c * L:(c + 1) * L] for c in range(n_chunks)],
                                     axis=0), upper, precision=HIGHEST, preferred_element_type=F32)
    for c in range(n_chunks):
        sl = slice(c * L, (c + 1) * L)
        i_row = grow_ref[0:N_HEADS, sl]
        b_row = b_rows[c * N_HEADS:(c + 1) * N_HEADS]
        gc = gcol_ref[sl, :]
        bc_all = _cumsum_rows(gc, L)
        for p in range(N_PAIRS):
            ls = slice(p * LANES, (p + 1) * LANES)
            k2 = km_ref[sl, ls]
            qt2 = qmt_ref[ls, sl]
            qt_e = jnp.where(rows_e, qt2, 0.0).astype(BF16)
            qt_o = jnp.where(rows_e, 0.0, qt2).astype(BF16)
            st2 = jnp.dot(k2.astype(BF16), jnp.concatenate([qt_e, qt_o], axis=1),
                          preferred_element_type=F32)
            ct = ct_s[p]
            rqt = jnp.dot(ct.astype(BF16), qt2.astype(BF16), preferred_element_type=F32)
            pts, mts, inters, wreps, decays, mnews = [], [], [], [], [], []
            for hh in range(2):
                h = 2 * p + hh
                cvec = jnp.broadcast_to(gc[:, h:h + 1] - bc_all[:, N_HEADS + h:N_HEADS + h + 1], (L, L))
                brow = b_row[h:h + 1, :]
                logdt = jnp.where(causal_t, cvec + brow, NEG)
                m_prev = m_s[h:h + 1, 0:1]
                m_inter = m_prev + brow
                m_t = jnp.maximum(m_inter, jnp.max(logdt, axis=0, keepdims=True))
                pts.append((st2[:, hh * L:(hh + 1) * L] * jnp.exp(logdt - m_t)).astype(BF16))
                mts.append(m_t)
                inters.append(jnp.exp(m_inter - m_t))
                m_new = m_t[:, L - 1:L]
                b_last = brow[:, L - 1:L]
                decays.append(jnp.exp(m_prev + b_last - m_new))
                wreps.append(jnp.exp(cvec + (b_last - m_new)))
                mnews.append(m_new)
            vext = jnp.concatenate([vmt_ref[ls, sl], ones_rows], axis=0)
            lhs = jnp.concatenate([jnp.where(rmask_e, vext, 0.0), jnp.where(rmask_o, vext, 0.0)],
                                  axis=1).astype(BF16)
            rt = jnp.dot(lhs, jnp.concatenate(pts, axis=0), preferred_element_type=F32)
            ndt = rt + jnp.where(rmask_e, inters[0], inters[1]) * rqt
            den = jnp.where(rows_e, ndt[LANES:LANES + 1], ndt[LANES + 1:LANES + 2])
            mt2 = jnp.where(rows_e, mts[0], mts[1])
            hvt = ndt[0:LANES] / jnp.maximum(jnp.abs(den), jnp.exp(-mt2))
            sq = hvt * hvt
            ms = jnp.where(rows_e, jnp.sum(sq[0:HEAD_DIM], axis=0, keepdims=True),
                           jnp.sum(sq[HEAD_DIM:LANES], axis=0, keepdims=True)) * (1.0 / HEAD_DIM)
            yt = hvt * lax.rsqrt(ms + RMS_EPS) * ghr_ref[ls, :] * _sigmoid(omt_ref[ls, sl])
            hm_ref[sl, ls] = jnp.transpose(yt).astype(hm_ref.dtype)
            kw = (k2 * jnp.where(cols_e, wreps[0], wreps[1])).astype(BF16)
            upd = jnp.dot(vext.astype(BF16), kw, preferred_element_type=F32)
            ct_s[p] = jnp.where(rmask_e, decays[0], decays[1]) * ct + jnp.where(bdt_mask, upd, 0.0)
            for hh in range(2):
                h = 2 * p + hh
                m_s[h:h + 1, :] = jnp.broadcast_to(mnews[hh], (1, LANES))

    @pl.when(j == pl.num_programs(1) - 1)
    def _():
        ct_ref[0] = ct_s[...]
        m_ref[0] = m_s[...]


def _mlstm_prompt(km, qmt, vmt, omt, gcol, grow, ghr, batch, seq):
    nt = seq // MLSTM_TL
    tokb = lambda w: pl.BlockSpec((MLSTM_TL, w), lambda b, j: (b * nt + j, 0))
    rowb = lambda r: pl.BlockSpec((r, MLSTM_TL), lambda b, j: (0, b * nt + j))
    return pl.pallas_call(
        _mlstm_prompt_kernel,
        grid=(batch, nt),
        in_specs=[tokb(DM), rowb(DM), rowb(DM), rowb(DM), tokb(LANES), rowb(2 * N_HEADS),
                  pl.BlockSpec((DM, LANES), lambda b, j: (0, 0))],
        out_specs=(tokb(DM),
                   pl.BlockSpec((1, N_PAIRS, CT_ROWS, LANES), lambda b, j: (b, 0, 0, 0)),
                   pl.BlockSpec((1, N_HEADS, LANES), lambda b, j: (b, 0, 0))),
        out_shape=(jax.ShapeDtypeStruct((batch * seq, DM), BF16),
                   jax.ShapeDtypeStruct((batch, N_PAIRS, CT_ROWS, LANES), F32),
                   jax.ShapeDtypeStruct((batch, N_HEADS, LANES), F32)),
        scratch_shapes=[pltpu.VMEM((N_PAIRS, CT_ROWS, LANES), F32), pltpu.VMEM((N_HEADS, LANES), F32)],
        compiler_params=_cparams(("arbitrary", "arbitrary"), VMEM_LIMIT),
        name="mlstm_prompt",
    )(km, qmt, vmt, omt, gcol, grow, ghr)


def _mlstm_sample_kernel(n_tok, km_ref, qmt_ref, vmt_ref, omt_ref, gcol_ref, grow_ref, c0_ref, n0_ref, m0_ref,
                         m0t_ref, gh_ref, hm_ref, c_ref, n_ref, mt_ref):
    L = LANES
    NB = L // n_tok
    ti = lax.broadcasted_iota(jnp.int32, (L, L), 0)
    si = lax.broadcasted_iota(jnp.int32, (L, L), 1)
    same = (ti // n_tok) == (si // n_tok)
    causal = same & (ti >= si)
    useg = jnp.where(same & (ti <= si), 1.0, 0.0)
    slast = jnp.where(same & (ti % n_tok == n_tok - 1), 1.0, 0.0)
    expand = jnp.where(ti // n_tok == si, 1.0, 0.0)
    expand_t = jnp.where(ti == si // n_tok, 1.0, 0.0)
    pick = jnp.where((ti // n_tok == si) & (ti % n_tok == n_tok - 1), 1.0, 0.0)
    hdot = lambda a, b: jnp.dot(a, b, precision=HIGHEST, preferred_element_type=F32)

    lane128 = lax.broadcasted_iota(jnp.int32, (L, LANES), 1)
    even128 = lane128 < HEAD_DIM
    lane256 = lax.broadcasted_iota(jnp.int32, (1, 2 * LANES), 1)
    cols_e = (lane256 < HEAD_DIM) | (lane256 == LANES)
    cols_o = ((lane256 >= HEAD_DIM) & (lane256 < LANES)) | (lane256 == LANES + 1)
    rows_e = lax.broadcasted_iota(jnp.int32, (LANES, 1), 0) < HEAD_DIM
    ones_cols = jnp.where(lane128 < 2, 1.0, 0.0)
    bo_r = lax.broadcasted_iota(jnp.int32, (LANES, LANES), 0) // HEAD_DIM
    bo_c = lax.broadcasted_iota(jnp.int32, (LANES, LANES), 1) // HEAD_DIM
    block_ones = jnp.where(bo_r == bo_c, 1.0, 0.0)
    W = NB * LANES
    rb = lax.broadcasted_iota(jnp.int32, (L, W), 0)
    cb = lax.broadcasted_iota(jnp.int32, (L, W), 1)
    own_block = (rb // n_tok) == (cb // LANES)
    bd_tiled = (rb // HEAD_DIM) == ((cb % LANES) // HEAD_DIM)

    grow = grow_ref[...]
    i_row = grow[0:N_HEADS]
    b_row = hdot(grow[N_HEADS:2 * N_HEADS], useg)
    b_last = hdot(b_row, slast)
    a_row = b_last - b_row + i_row
    pos = lax.broadcasted_iota(jnp.int32, a_row.shape, 1) % n_tok
    pm = a_row
    sh = 1
    while sh < n_tok:
        pm = jnp.where(pos >= sh, jnp.maximum(pm, pltpu.roll(pm, sh, axis=1)), pm)
        sh *= 2
    m_carry = hdot(jnp.concatenate([m0t_ref[0], b_row], axis=1), jnp.concatenate([expand_t, slast], axis=0))
    m_new_row = jnp.maximum(m_carry, hdot(pm, slast))
    decay_row = jnp.exp(m_carry - m_new_row)
    w_row = jnp.exp(a_row - m_new_row)
    mt_ref[0] = hdot(m_new_row, pick)
    decay_bh = hdot(decay_row, pick)
    decay_hb = jnp.transpose(decay_bh)[0:NB]

    bc_all = gcol_ref[...]
    rowpos = lax.broadcasted_iota(jnp.int32, bc_all.shape, 0) % n_tok
    sh = 1
    while sh < n_tok:
        bc_all = bc_all + jnp.where(rowpos >= sh, pltpu.roll(bc_all, sh, axis=0), 0.0)
        sh *= 2
    pad_rows = lambda a: jnp.concatenate([a, jnp.zeros((L - NB, a.shape[1]), F32)], axis=0)
    m0_col = hdot(expand, pad_rows(m0_ref[...]))
    rowv_all = i_row - b_row

    for p in range(N_PAIRS):
        ls = slice(p * LANES, (p + 1) * LANES)
        q2f = jnp.transpose(qmt_ref[ls, :])
        q2 = q2f.astype(BF16)
        kt2 = jnp.transpose(km_ref[:, ls])
        v2 = jnp.transpose(vmt_ref[ls, :])
        vext = jnp.concatenate([v2, ones_cols], axis=1)
        kt_e = jnp.where(rows_e, kt2, 0.0).astype(BF16)
        kt_o = jnp.where(rows_e, 0.0, kt2).astype(BF16)
        s2 = jnp.dot(q2, jnp.concatenate([kt_e, kt_o], axis=1), preferred_element_type=F32)
        ps, mts, inters = [], [], []
        for hh in range(2):
            h = 2 * p + hh
            bcol = bc_all[:, N_HEADS + h:N_HEADS + h + 1]
            logd = jnp.where(causal, bcol + rowv_all[h:h + 1, :], NEG)
            m_inter = m0_col[:, h:h + 1] + bcol
            m_t = jnp.maximum(m_inter, jnp.max(logd, axis=1, keepdims=True))
            ps.append((s2[:, hh * L:(hh + 1) * L] * jnp.exp(logd - m_t)).astype(BF16))
            mts.append(m_t)
            inters.append(jnp.exp(m_inter - m_t))
        vstack = jnp.concatenate([jnp.where(cols_e, vext, 0.0), jnp.where(cols_o, vext, 0.0)],
                                 axis=0).astype(BF16)
        r = jnp.dot(jnp.concatenate(ps, axis=1), vstack, preferred_element_type=F32)
        zero = jnp.zeros((HEAD_DIM, HEAD_DIM), F32)
        cstack = jnp.concatenate(
            [jnp.concatenate([jnp.concatenate([c0_ref[b, 2 * p], zero], axis=1),
                              jnp.concatenate([zero, c0_ref[b, 2 * p + 1]], axis=1)], axis=0)
             for b in range(NB)], axis=1)
        rq_all = jnp.where(own_block, jnp.dot(q2, cstack.astype(BF16), preferred_element_type=F32), 0.0)
        rq = rq_all[:, 0:LANES]
        for b in range(1, NB):
            rq = rq + rq_all[:, b * LANES:(b + 1) * LANES]
        n_rows = hdot(expand, pad_rows(n0_ref[:, ls]))
        qn = hdot(q2f * n_rows, block_ones)
        inter2 = jnp.where(even128, inters[0], inters[1])
        num = r[:, 0:LANES] + inter2 * rq
        den = jnp.where(even128, r[:, LANES:LANES + 1], r[:, LANES + 1:LANES + 2]) + inter2 * qn
        mt2 = jnp.where(even128, mts[0], mts[1])
        hv = num / jnp.maximum(jnp.abs(den), jnp.exp(-mt2))
        ms = hdot(hv * hv, block_ones) * (1.0 / HEAD_DIM)
        y = hv * lax.rsqrt(ms + RMS_EPS) * gh_ref[:, ls] * _sigmoid(jnp.transpose(omt_ref[ls, :]))
        hm_ref[:, ls] = y.astype(hm_ref.dtype)
        kw = kt2 * jnp.where(rows_e, w_row[2 * p:2 * p + 1, :], w_row[2 * p + 1:2 * p + 2, :])
        vbd = jnp.where(own_block, jnp.concatenate([v2] * NB, axis=1), 0.0).astype(BF16)
        upd = jnp.dot(kw.astype(BF16), vbd, preferred_element_type=F32)
        upd = jnp.where(bd_tiled, upd, 0.0)
        for b in range(NB):
            bs = slice(b * LANES, (b + 1) * LANES)
            dec_b = jnp.where(rows_e, decay_hb[b:b + 1, 2 * p:2 * p + 1], decay_hb[b:b + 1, 2 * p + 1:2 * p + 2])
            cnew = dec_b * cstack[:, bs] + upd[:, bs]
            c_ref[b, 2 * p] = cnew[0:HEAD_DIM, 0:HEAD_DIM]
            c_ref[b, 2 * p + 1] = cnew[HEAD_DIM:LANES, HEAD_DIM:LANES]
        nsum = jnp.transpose(hdot(kw, expand))[0:NB]
        dec_n = jnp.where(lax.broadcasted_iota(jnp.int32, (NB, LANES), 1) < HEAD_DIM,
                          decay_hb[:, 2 * p:2 * p + 1], decay_hb[:, 2 * p + 1:2 * p + 2])
        n_ref[:, ls] = dec_n * n0_ref[:, ls] + nsum


def _mlstm_sample(km, qmt, vmt, omt, gcol, grow, c0, n0, m0, gh, row0, n_tok):
    nb = c0.shape[0]
    g_nb = LANES // n_tok
    n_g = nb // g_nb
    blk0 = row0 // LANES
    tokb = lambda w: pl.BlockSpec((LANES, w), lambda i: (blk0 + i, 0))
    rowb = lambda r: pl.BlockSpec((r, LANES), lambda i: (0, blk0 + i))
    m0t = jnp.pad(m0.reshape(n_g, g_nb, N_HEADS).transpose(0, 2, 1),
                  ((0, 0), (0, 0), (0, LANES - g_nb)))
    return pl.pallas_call(
        functools.partial(_mlstm_sample_kernel, n_tok),
        grid=(n_g,),
        in_specs=[tokb(DM), rowb(DM), rowb(DM), rowb(DM), tokb(LANES), rowb(2 * N_HEADS),
                  pl.BlockSpec((g_nb, N_HEADS, HEAD_DIM, HEAD_DIM), lambda i: (i, 0, 0, 0)),
                  pl.BlockSpec((g_nb, DM), lambda i: (i, 0)),
                  pl.BlockSpec((g_nb, N_HEADS), lambda i: (i, 0)),
                  pl.BlockSpec((1, N_HEADS, LANES), lambda i: (i, 0, 0)),
                  pl.BlockSpec((1, DM), lambda i: (0, 0))],
        out_specs=(pl.BlockSpec((LANES, DM), lambda i: (i, 0)),
                   pl.BlockSpec((g_nb, N_HEADS, HEAD_DIM, HEAD_DIM), lambda i: (i, 0, 0, 0)),
                   pl.BlockSpec((g_nb, DM), lambda i: (i, 0)),
                   pl.BlockSpec((1, N_HEADS, LANES), lambda i: (i, 0, 0))),
        out_shape=(jax.ShapeDtypeStruct((nb * n_tok, DM), BF16), jax.ShapeDtypeStruct(c0.shape, F32),
                   jax.ShapeDtypeStruct((nb, DM), F32), jax.ShapeDtypeStruct((n_g, N_HEADS, LANES), F32)),
        compiler_params=_cparams(("arbitrary",), VMEM_LIMIT),
        name="mlstm_sample",
    )(km, qmt, vmt, omt, gcol, grow, c0, n0.reshape(nb, DM), m0, m0t, gh)


def _alibi_slope(h):
    return float(np.float32(2.0 ** (-8.0 * (h + 1) / N_HEADS)))


def _dup_halves(x):
    lane = lax.broadcasted_iota(jnp.int32, x.shape, 1)
    xr = pltpu.roll(x, HEAD_DIM, axis=1)
    lo = lane < HEAD_DIM
    return jnp.where(lo, x, xr), jnp.where(lo, xr, x)


def _stack_group_queries(q, g):
    lane = lax.broadcasted_iota(jnp.int32, (q.shape[0], LANES), 1)
    parts = []
    for hh in range(GROUP):
        h = GROUP * g + hh
        blk = q[:, (h // 2) * LANES:(h // 2 + 1) * LANES]
        keep = (lane < HEAD_DIM) if h % 2 == 0 else (lane >= HEAD_DIM)
        parts.append(jnp.where(keep, blk, 0.0))
    return jnp.concatenate(parts, axis=0).astype(BF16)


def _swa_prompt_kernel(sink_ref, q_ref, kp_ref, ko_ref, vp_ref, vo_ref, o_ref):
    j = pl.program_id(1)
    R = WINDOW
    qi = lax.broadcasted_iota(jnp.int32, (R, R), 0)
    kj = lax.broadcasted_iota(jnp.int32, (R, R), 1)
    own = kj <= qi
    distf = jnp.where(own, qi - kj, qi - kj + R).astype(F32)
    lane = lax.broadcasted_iota(jnp.int32, (R, LANES), 1)
    for u in range(SWA_BLOCKS):
        rs = slice(u * R, (u + 1) * R)
        q = q_ref[rs, :]
        k_prev = kp_ref[...] if u == 0 else ko_ref[(u - 1) * R:u * R, :]
        v_prev = vp_ref[...] if u == 0 else vo_ref[(u - 1) * R:u * R, :]
        kd = _dup_halves(jnp.concatenate([k_prev, ko_ref[rs, :]], axis=0))
        vd = _dup_halves(jnp.concatenate([v_prev, vo_ref[rs, :]], axis=0))
        valid = (own | (j > 0)) if u == 0 else None
        outs = []
        for g in range(N_KV):
            qs = _stack_group_queries(q, g)
            s = lax.dot_general(qs, kd[g].astype(BF16), (((1,), (1,)), ((), ())),
                                preferred_element_type=F32) * (HEAD_DIM ** -0.5)
            ps = []
            for hh in range(GROUP):
                h = GROUP * g + hh
                sink = sink_ref[h]
                rows = slice(hh * R, (hh + 1) * R)
                sh = jnp.where(own, s[rows, R:], s[rows, :R]) - _alibi_slope(h) * distf
                if valid is not None:
                    sh = jnp.where(valid, sh, NEG)
                mx = jnp.maximum(jnp.max(sh, axis=1, keepdims=True), sink)
                p = jnp.exp(sh - mx)
                p = p / (jnp.sum(p, axis=1, keepdims=True) + jnp.exp(sink - mx))
                ps.append(jnp.concatenate([jnp.where(own, 0.0, p), jnp.where(own, p, 0.0)], axis=1).astype(BF16))
            o = jnp.dot(jnp.concatenate(ps, axis=0), vd[g].astype(BF16), preferred_element_type=F32)
            for pp in range(GROUP // 2):
                outs.append(jnp.where(lane < HEAD_DIM, o[(2 * pp) * R:(2 * pp + 1) * R],
                                      o[(2 * pp + 1) * R:(2 * pp + 2) * R]))
        o_ref[rs, :] = jnp.concatenate(outs, axis=1).astype(o_ref.dtype)


def _swa_prompt(sinks, qa, ka, va, batch, seq):
    nb = seq // (WINDOW * SWA_BLOCKS)
    own = lambda w: pl.BlockSpec((WINDOW * SWA_BLOCKS, w), lambda b, j: (b * nb + j, 0))
    prev = lambda w: pl.BlockSpec(
        (WINDOW, w), lambda b, j: (b * nb * SWA_BLOCKS + jnp.maximum(j * SWA_BLOCKS - 1, 0), 0))
    return pl.pallas_call(
        _swa_prompt_kernel,
        grid=(batch, nb),
        in_specs=[pl.BlockSpec(memory_space=pltpu.SMEM), own(DM), prev(DKV), own(DKV), prev(DKV), own(DKV)],
        out_specs=own(DM),
        out_shape=jax.ShapeDtypeStruct((batch * seq, DM), BF16),
        compiler_params=_cparams(("arbitrary", "arbitrary"), VMEM_LIMIT),
        name="swa_prompt",
    )(sinks, qa, ka, ka, va, va)


def _swa_decode_kernel(n_tok, sink_ref, q_ref, kn_ref, vn_ref, kc_ref, vc_ref, o_ref, ko_ref, vo_ref):
    W = WINDOW
    L = LANES
    NB = L // n_tok
    for b in range(NB):
        ko_ref[b, 0:W - n_tok, :] = kc_ref[b, n_tok:W, :]
        ko_ref[b, W - n_tok:W, :] = kn_ref[b * n_tok:(b + 1) * n_tok, :]
        vo_ref[b, 0:W - n_tok, :] = vc_ref[b, n_tok:W, :]
        vo_ref[b, W - n_tok:W, :] = vn_ref[b * n_tok:(b + 1) * n_tok, :]
    q = q_ref[...]
    knd, vnd = _dup_halves(kn_ref[...]), _dup_halves(vn_ref[...])
    kcd = _dup_halves(kc_ref[...].reshape(NB * W, DKV))
    vcd = _dup_halves(vc_ref[...].reshape(NB * W, DKV))
    ri = lax.broadcasted_iota(jnp.int32, (L, L), 0)
    ci = lax.broadcasted_iota(jnp.int32, (L, L), 1)
    t_q = ri % n_tok
    dist_n = t_q - ci % n_tok
    valid_n = ((ri // n_tok) == (ci // n_tok)) & (dist_n >= 0)
    dist_c = t_q + W - ci
    valid_c = dist_c < WINDOW
    dnf = dist_n.astype(F32)
    dcf = dist_c.astype(F32)
    own1 = (lax.broadcasted_iota(jnp.int32, (L, NB * W), 0) // n_tok
            == lax.broadcasted_iota(jnp.int32, (L, NB * W), 1) // W)
    own4 = ((lax.broadcasted_iota(jnp.int32, (GROUP * L, NB * W), 0) % L) // n_tok
            == lax.broadcasted_iota(jnp.int32, (GROUP * L, NB * W), 1) // W)
    lane = lax.broadcasted_iota(jnp.int32, (L, LANES), 1)
    nt = (((1,), (1,)), ((), ()))
    outs = []
    for g in range(N_KV):
        qs = _stack_group_queries(q, g)
        sn = lax.dot_general(qs, knd[g].astype(BF16), nt, preferred_element_type=F32) * (HEAD_DIM ** -0.5)
        sc_all = lax.dot_general(qs, kcd[g].astype(BF16), nt, preferred_element_type=F32) * (HEAD_DIM ** -0.5)
        pcs, pns = [], []
        for hh in range(GROUP):
            h = GROUP * g + hh
            sink = sink_ref[h]
            slope = _alibi_slope(h)
            rows = slice(hh * L, (hh + 1) * L)
            blk = jnp.where(own1, sc_all[rows], 0.0)
            sc = blk[:, 0:W]
            for b in range(1, NB):
                sc = sc + blk[:, b * W:(b + 1) * W]
            shc = jnp.where(valid_c, sc - slope * dcf, NEG)
            shn = jnp.where(valid_n, sn[rows] - slope * dnf, NEG)
            mx = jnp.maximum(jnp.maximum(jnp.max(shc, axis=1, keepdims=True),
                                         jnp.max(shn, axis=1, keepdims=True)), sink)
            pc = jnp.exp(shc - mx)
            pn = jnp.exp(shn - mx)
            inv = 1.0 / (jnp.sum(pc, axis=1, keepdims=True) + jnp.sum(pn, axis=1, keepdims=True)
                         + jnp.exp(sink - mx))
            pcs.append(pc * inv)
            pns.append((pn * inv).astype(BF16))
        pc4 = jnp.concatenate(pcs, axis=0)
        p_bd = jnp.where(own4, jnp.concatenate([pc4] * NB, axis=1), 0.0).astype(BF16)
        o = (jnp.dot(p_bd, vcd[g].astype(BF16), preferred_element_type=F32)
             + jnp.dot(jnp.concatenate(pns, axis=0), vnd[g].astype(BF16), preferred_element_type=F32))
        for pp in range(GROUP // 2):
            outs.append(jnp.where(lane < HEAD_DIM, o[(2 * pp) * L:(2 * pp + 1) * L],
                                  o[(2 * pp + 1) * L:(2 * pp + 2) * L]))
    o_ref[...] = jnp.concatenate(outs, axis=1)


def _swa_decode(sinks, qa, ka, va, kc, vc, row0, n_tok):
    nb = kc.shape[0]
    g_nb = LANES // n_tok
    blk0 = row0 // LANES
    tokb = lambda w: pl.BlockSpec((LANES, w), lambda i: (blk0 + i, 0))
    cache = pl.BlockSpec((g_nb, WINDOW, DKV), lambda i: (i, 0, 0))
    return pl.pallas_call(
        functools.partial(_swa_decode_kernel, n_tok),
        grid=(nb // g_nb,),
        in_specs=[pl.BlockSpec(memory_space=pltpu.SMEM), tokb(DM), tokb(DKV), tokb(DKV), cache, cache],
        out_specs=(pl.BlockSpec((LANES, DM), lambda i: (i, 0)), cache, cache),
        out_shape=(jax.ShapeDtypeStruct((nb * n_tok, DM), F32),
                   jax.ShapeDtypeStruct(kc.shape, F32), jax.ShapeDtypeStruct(vc.shape, F32)),
        compiler_params=_cparams(("arbitrary",), VMEM_LIMIT),
        name="swa_decode",
    )(sinks, qa, ka, va, kc, vc)


def _outproj_router_kernel(n_ptiles, xp_ref, xs_ref, hmp_ref, hap_ref, hms_ref, has_ref, wom_ref, woa_ref,
                           g_ref, wrt_ref, br_ref,
                           x1_ref, xg_ref, meta_ref, cnt_ref):
    i = pl.program_id(0)
    is_p = i < n_ptiles
    x = jnp.where(is_p, xp_ref[...], xs_ref[...])
    hm = jnp.where(is_p, hmp_ref[...], hms_ref[...].astype(BF16))
    ha = jnp.where(is_p, hap_ref[...], has_ref[...].astype(BF16))
    x1 = (x + jnp.dot(hm, wom_ref[...], preferred_element_type=F32)
          + jnp.dot(ha, woa_ref[...], preferred_element_type=F32))
    x1_ref[...] = x1
    h2 = _rms(x1, g_ref[...])
    nt = (((1,), (1,)), ((), ()))
    h2_hi = h2.astype(BF16)
    h2_lo = (h2 - h2_hi.astype(F32)).astype(BF16)
    wr = wrt_ref[...]
    wr_hi = wr.astype(BF16)
    wr_lo = (wr - wr_hi.astype(F32)).astype(BF16)
    logits = (lax.dot_general(wr_hi, h2_hi, nt, preferred_element_type=F32)
              + lax.dot_general(wr_lo, h2_hi, nt, preferred_element_type=F32)
              + lax.dot_general(wr_hi, h2_lo, nt, preferred_element_type=F32)) + br_ref[...]
    eidx = lax.broadcasted_iota(jnp.int32, logits.shape, 0).astype(F32)
    work = logits
    vals, hots = [], []
    for _ in range(TOP_K):
        mv = jnp.max(work, axis=0, keepdims=True)
        sel = jnp.min(jnp.where(work == mv, eidx, float(N_EXPERTS)), axis=0, keepdims=True)
        hot = eidx == sel
        vals.append(mv)
        hots.append(hot)
        work = jnp.where(hot, -jnp.inf, work)
    es = [jnp.exp(v - vals[0]) for v in vals]
    tot = es[0] + es[1] + es[2] + es[3]
    gates = [e / tot for e in es]
    hot_all = jnp.where(hots[0] | hots[1] | hots[2] | hots[3], 1.0, 0.0)
    tm = logits.shape[1]
    su = (lax.broadcasted_iota(jnp.int32, (tm, tm), 0) < lax.broadcasted_iota(jnp.int32, (tm, tm), 1))
    cum = jnp.dot(hot_all.astype(BF16), su.astype(BF16), preferred_element_type=F32)
    cnt = jnp.sum(hot_all, axis=1, keepdims=True)
    cpad = (((cnt.astype(jnp.int32) + (SUBLANES - 1)) // SUBLANES) * SUBLANES).astype(F32)
    lower = (lax.broadcasted_iota(jnp.int32, (N_EXPERTS, N_EXPERTS), 0)
             > lax.broadcasted_iota(jnp.int32, (N_EXPERTS, N_EXPERTS), 1)).astype(F32)
    lstart = jnp.dot(lower, jnp.broadcast_to(cpad, (N_EXPERTS, LANES)), precision=HIGHEST,
                     preferred_element_type=F32)[:, 0:1]
    base = lstart + cum
    lpos = [jnp.sum(jnp.where(hot, base, 0.0), axis=0, keepdims=True) for hot in hots]
    lpi = [p.astype(jnp.int32) for p in lpos]
    h2b = h2.astype(BF16)
    rows_per = GROUP_R // SORT_CHUNKS
    for j in range(SORT_CHUNKS):
        r_iota = lax.broadcasted_iota(jnp.int32, (rows_per, tm), 0) + j * rows_per
        sel01 = jnp.where(r_iota == lpi[0], 1.0, jnp.where(r_iota == lpi[1], 1.0, jnp.where(
            r_iota == lpi[2], 1.0, jnp.where(r_iota == lpi[3], 1.0, 0.0)))).astype(BF16)
        xg_ref[j * rows_per:(j + 1) * rows_per, :] = jnp.dot(sel01, h2b, preferred_element_type=F32)
    meta_ref[...] = jnp.transpose(jnp.concatenate(gates + lpos, axis=0))
    cnt_ref[0] = jnp.broadcast_to(cnt, (N_EXPERTS, LANES))


def _outproj_router(xp, xs, hmp, hap, hms, has, wom, woa, g_ffn, wrt, br):
    tp, ts = xp.shape[0], xs.shape[0]
    n_pt, n_st = tp // ROUTE_T, ts // ROUTE_T
    t_all = tp + ts
    pblk = lambda w: pl.BlockSpec((ROUTE_T, w), lambda i: (jnp.minimum(i, n_pt - 1), 0))
    sblk = lambda w: pl.BlockSpec((ROUTE_T, w), lambda i: (jnp.maximum(i - n_pt, 0), 0))
    full = lambda a: pl.BlockSpec(a.shape, lambda i: (0,) * a.ndim)
    return pl.pallas_call(
        functools.partial(_outproj_router_kernel, n_pt),
        grid=(n_pt + n_st,),
        in_specs=[pblk(D_MODEL), sblk(D_MODEL), pblk(DM), pblk(DM), sblk(DM), sblk(DM),
                  full(wom), full(woa), full(g_ffn), full(wrt), full(br)],
        out_specs=(pl.BlockSpec((ROUTE_T, D_MODEL), lambda i: (i, 0)),
                   pl.BlockSpec((GROUP_R, D_MODEL), lambda i: (i, 0)),
                   pl.BlockSpec((ROUTE_T, 2 * TOP_K), lambda i: (i, 0)),
                   pl.BlockSpec((1, N_EXPERTS, LANES), lambda i: (i, 0, 0))),
        out_shape=(jax.ShapeDtypeStruct((t_all, D_MODEL), F32),
                   jax.ShapeDtypeStruct(((n_pt + n_st) * GROUP_R, D_MODEL), F32),
                   jax.ShapeDtypeStruct((t_all, 2 * TOP_K), F32),
                   jax.ShapeDtypeStruct((n_pt + n_st, N_EXPERTS, LANES), F32)),
        compiler_params=_cparams(("arbitrary",), VMEM_LIMIT),
        name="outproj_router",
    )(xp, xs, hmp, hap, hms, has, wom, woa, g_ffn, wrt, br)


def _expert_kernel(be_ref, na_ref, slot_ref, nxt_ref, ctab_ref, ctab1_ref, ctab2_ref, xg_ref, wgu_ref, bgu_ref,
                   wd_ref, bd_ref, o_ref, xbuf, wgu_f, wd_f, wgu_s, wd_s, xsem, wsem):
    i = pl.program_id(0)
    na = na_ref[0]

    def x_copies(tab_ref, slot):
        return [pltpu.make_async_copy(xg_ref.at[pl.ds(pl.multiple_of(tab_ref[0, 0, c], SUBLANES), SUBLANES)],
                                      xbuf.at[slot, pl.ds(c * SUBLANES, SUBLANES)], xsem.at[slot])
                for c in range(MOE_BM // SUBLANES)]

    def w_copies(e, slot):
        return [pltpu.make_async_copy(wgu_ref.at[e], wgu_f.at[slot], wsem.at[slot]),
                pltpu.make_async_copy(wd_ref.at[e], wd_f.at[slot], wsem.at[slot])]

    @pl.when(i == 0)
    def _():
        for cp in x_copies(ctab_ref, 0) + x_copies(ctab1_ref, 1) + w_copies(be_ref[0], 0):
            cp.start()

    @pl.when(i == na)
    def _():
        for cp in x_copies(ctab_ref, lax.rem(i, X_SLOTS)) + x_copies(ctab1_ref, lax.rem(i + 1, X_SLOTS)):
            cp.wait()

    @pl.when(i < na)
    def _():
        changed = (i == 0) | (be_ref[i] != be_ref[jnp.maximum(i - 1, 0)])
        wslot = slot_ref[i]
        xslot = lax.rem(i, X_SLOTS)

        @pl.when(changed)
        def _():
            for cp in w_copies(be_ref[i], wslot):
                cp.wait()

            @pl.when(nxt_ref[i] >= 0)
            def _():
                for cp in w_copies(nxt_ref[i], 1 - wslot):
                    cp.start(priority=1)

            wgu_s[...] = wgu_f[wslot].astype(BF16)
            wd_s[...] = wd_f[wslot].astype(BF16)

        for cp in x_copies(ctab_ref, xslot):
            cp.wait()

        x = xbuf[xslot].astype(BF16)
        gu = jnp.dot(x, wgu_s[...], preferred_element_type=F32) + bgu_ref[...]
        for cp in x_copies(ctab2_ref, lax.rem(i + 2, X_SLOTS)):
            cp.start()
        gate = jnp.minimum(gu[:, :D_FF], SWIGLU_LIMIT)
        up = jnp.clip(gu[:, D_FF:], -SWIGLU_LIMIT, SWIGLU_LIMIT)
        act = (up + 1.0) * (gate * _sigmoid(gate * SWIGLU_ALPHA))
        o_ref[...] = jnp.dot(act.astype(BF16), wd_s[...], preferred_element_type=F32) + bd_ref[...]


def _expert_ffn(block_exp, n_active, wslot, next_exp, ctab, xg, wgu, bgu, wd, bd):
    n_blocks = ctab.shape[0]
    n_rows = n_blocks * MOE_BM
    nch = MOE_BM // SUBLANES
    tab = lambda d: pl.BlockSpec((1, 1, nch), lambda i, *_: (jnp.minimum(i + d, n_blocks - 1), 0, 0),
                                 memory_space=pltpu.SMEM)
    blk_e = lambda i, be: be[jnp.minimum(i, n_blocks - 1)]
    grid_spec = pltpu.PrefetchScalarGridSpec(
        num_scalar_prefetch=4,
        grid=(n_blocks + 1,),
        in_specs=[tab(0), tab(1), tab(2),
                  pl.BlockSpec(memory_space=pl.ANY),
                  pl.BlockSpec(memory_space=pl.ANY),
                  pl.BlockSpec((None, 1, 2 * D_FF), lambda i, be, *_: (blk_e(i, be), 0, 0)),
                  pl.BlockSpec(memory_space=pl.ANY),
                  pl.BlockSpec((None, 1, D_MODEL), lambda i, be, *_: (blk_e(i, be), 0, 0))],
        out_specs=pl.BlockSpec((MOE_BM, D_MODEL), lambda i, be, na, *_: (jnp.minimum(i, na[0] - 1), 0)),
        scratch_shapes=[pltpu.VMEM((X_SLOTS, MOE_BM, D_MODEL), F32),
                        pltpu.VMEM((2, D_MODEL, 2 * D_FF), F32), pltpu.VMEM((2, D_FF, D_MODEL), F32),
                        pltpu.VMEM((D_MODEL, 2 * D_FF), BF16), pltpu.VMEM((D_FF, D_MODEL), BF16),
                        pltpu.SemaphoreType.DMA((X_SLOTS,)), pltpu.SemaphoreType.DMA((2,))],
    )
    return pl.pallas_call(
        _expert_kernel,
        grid_spec=grid_spec,
        out_shape=jax.ShapeDtypeStruct((n_rows, D_MODEL), F32),
        compiler_params=_cparams(("arbitrary",), VMEM_LIMIT),
        name="moe_experts",
    )(block_exp, n_active, wslot, next_exp, ctab, ctab, ctab, xg, wgu, bgu, wd, bd)


def _combine_kernel(n_ptiles, n_tiles, ctab_ref, ctab1_ref, ctab2_ref, outs_ref, x1_ref, meta_ref, gf_ref,
                    yp_ref, ys_ref, obuf, sem):
    i = pl.program_id(0)

    def copies(tab_ref, s):
        return [pltpu.make_async_copy(outs_ref.at[pl.ds(pl.multiple_of(tab_ref[0, 0, c], SUBLANES), SUBLANES)],
                                      obuf.at[s, pl.ds(c * SUBLANES, SUBLANES)], sem.at[s])
                for c in range(GROUP_R // SUBLANES)]

    @pl.when(i == 0)
    def _():
        for cp in copies(ctab_ref, 0) + copies(ctab1_ref, 1):
            cp.start()

    @pl.when(i == n_tiles)
    def _():
        for cp in copies(ctab_ref, lax.rem(i, X_SLOTS)) + copies(ctab1_ref, lax.rem(i + 1, X_SLOTS)):
            cp.wait()

    @pl.when(i < n_tiles)
    def _():
        _combine_tile(i, n_ptiles, copies, ctab_ref, ctab2_ref, x1_ref, meta_ref, gf_ref, yp_ref, ys_ref, obuf)


def _combine_tile(i, n_ptiles, copies, ctab_ref, ctab2_ref, x1_ref, meta_ref, gf_ref, yp_ref, ys_ref, obuf):
    slot = lax.rem(i, X_SLOTS)
    for cp in copies(ctab_ref, slot):
        cp.wait()
    for cp in copies(ctab2_ref, lax.rem(i + 2, X_SLOTS)):
        cp.start()

    meta = meta_ref[...]
    tm = meta.shape[0]
    r_iota = lax.broadcasted_iota(jnp.int32, (tm, GROUP_R), 1)
    lp = [meta[:, TOP_K + k:TOP_K + k + 1].astype(jnp.int32) for k in range(TOP_K)]
    gk = [meta[:, k:k + 1] for k in range(TOP_K)]
    gsel = jnp.where(r_iota == lp[0], gk[0], jnp.where(r_iota == lp[1], gk[1], jnp.where(
        r_iota == lp[2], gk[2], jnp.where(r_iota == lp[3], gk[3], 0.0))))
    sel01 = jnp.where(gsel != 0.0, 1.0, 0.0).astype(BF16)
    rg_row = jnp.sum(gsel, axis=0, keepdims=True)
    rg_col = jnp.transpose(jnp.broadcast_to(rg_row, (SUBLANES, GROUP_R)))[:, 0:1]
    og = (obuf[slot] * rg_col).astype(BF16)
    acc = x1_ref[...] + jnp.dot(sel01, og, preferred_element_type=F32)
    y = _rms(acc, gf_ref[...])

    @pl.when(i < n_ptiles)
    def _():
        yp_ref[...] = y

    @pl.when(i >= n_ptiles)
    def _():
        ys_ref[...] = y


def _combine(ctab, outs, x1, meta, g_final, tp, ts):
    n_pt, n_st = tp // ROUTE_T, ts // ROUTE_T
    n = n_pt + n_st
    nch = GROUP_R // SUBLANES
    last = lambda i: jnp.minimum(i, n - 1)
    tab = lambda d: pl.BlockSpec((1, 1, nch), lambda i: (last(i + d), 0, 0), memory_space=pltpu.SMEM)
    return pl.pallas_call(
        functools.partial(_combine_kernel, n_pt, n),
        grid=(n + 1,),
        in_specs=[tab(0), tab(1), tab(2),
                  pl.BlockSpec(memory_space=pl.ANY),
                  pl.BlockSpec((ROUTE_T, D_MODEL), lambda i: (last(i), 0)),
                  pl.BlockSpec((ROUTE_T, 2 * TOP_K), lambda i: (last(i), 0)),
                  pl.BlockSpec((1, D_MODEL), lambda i: (0, 0))],
        out_specs=(pl.BlockSpec((ROUTE_T, D_MODEL), lambda i: (jnp.minimum(i, n_pt - 1), 0)),
                   pl.BlockSpec((ROUTE_T, D_MODEL), lambda i: (jnp.maximum(last(i) - n_pt, 0), 0))),
        out_shape=(jax.ShapeDtypeStruct((tp, D_MODEL), F32), jax.ShapeDtypeStruct((ts, D_MODEL), F32)),
        scratch_shapes=[pltpu.VMEM((X_SLOTS, GROUP_R, D_MODEL), F32), pltpu.SemaphoreType.DMA((X_SLOTS,))],
        compiler_params=_cparams(("arbitrary",), VMEM_LIMIT),
        name="moe_combine",
    )(ctab, ctab, ctab, outs, x1, meta, g_final)


def kernel(x_prompt, x_sample, cache_swa_k, cache_swa_v, state_mlstm_c, state_mlstm_n, state_mlstm_m,
           g_mix, w_in, b_igate, b_fgate, g_head, attn_sinks, w_out, g_ffn, w_router, b_router,
           w_gate_up, b_gate_up, w_down, b_down, g_final):
    assert w_in.shape[0] == 1, "single-layer problem"
    B, S, _ = x_prompt.shape
    Bd, Tn, _ = x_sample.shape
    tp, ts = B * S, Bd * Tn
    t_all = tp + ts
    xp = x_prompt.reshape(tp, D_MODEL)
    xs = x_sample.reshape(ts, D_MODEL)

    w = w_in[0]
    o = np.cumsum([0, DM, DM, DM, DM, N_HEADS, N_HEADS, DM, DKV, DKV])
    col = lambda a: w[:, int(o[a]):int(o[a + 1])]
    wgates = jnp.concatenate([col(4), col(5)], axis=1)
    w1 = jnp.concatenate([col(1), col(6), col(7), col(8), jnp.pad(wgates, ((0, 0), (0, LANES - 2 * N_HEADS)))],
                         axis=1).astype(BF16)
    wkt = jnp.concatenate([col(0), col(2), col(3), wgates], axis=1).T.astype(BF16)
    bg = jnp.concatenate([b_igate[0], b_fgate[0]]).astype(F32)
    bcol = jnp.pad(bg, (0, LANES - 2 * N_HEADS)).reshape(1, LANES)
    brow = bg.reshape(2 * N_HEADS, 1)

    km, qa, ka, va, qmt, vmt, omt, gcol, grow = _inproj(xp, xs, g_mix[0].reshape(1, D_MODEL), w1, wkt, bcol, brow)

    gh = g_head[0].astype(F32)
    sinks = attn_sinks[0].astype(F32)

    hm_p, ctp, m_p = _mlstm_prompt(km, qmt, vmt, omt, gcol, grow,
                                   jnp.broadcast_to(gh.reshape(DM, 1), (DM, LANES)), B, S)
    ha_p = _swa_prompt(sinks, qa, ka, va, B, S)

    hm_s, c_s, n_s, mt_s = _mlstm_sample(km, qmt, vmt, omt, gcol, grow, state_mlstm_c[0], state_mlstm_n[0],
                                         state_mlstm_m[0], gh.reshape(1, DM), tp, Tn)
    n_s = n_s.reshape(Bd, N_HEADS, HEAD_DIM)
    m_s = mt_s[:, :, :LANES // Tn].transpose(0, 2, 1).reshape(Bd, N_HEADS)
    ha_s, k_s, v_s = _swa_decode(sinks, qa, ka, va, cache_swa_k[0].reshape(Bd, WINDOW, DKV),
                                 cache_swa_v[0].reshape(Bd, WINDOW, DKV), tp, Tn)

    wo = w_out[0].astype(BF16)
    x1, xg, meta, cnt = _outproj_router(
        xp, xs, hm_p, ha_p, hm_s, ha_s, wo[:DM], wo[DM:], g_ffn[0].reshape(1, D_MODEL),
        w_router[0].T, b_router[0].reshape(N_EXPERTS, 1))

    i32 = jnp.int32
    n_tiles = t_all // ROUTE_T
    max_rows = t_all * TOP_K + n_tiles * N_EXPERTS * (SUBLANES - 1) + N_EXPERTS * (MOE_BM - 1)
    n_blocks = -(-max_rows // MOE_BM)
    cpad = (cnt[:, :, 0].astype(i32) + (SUBLANES - 1)) // SUBLANES * SUBLANES
    lstart = jnp.cumsum(cpad, axis=1) - cpad
    goff = jnp.cumsum(cpad, axis=0) - cpad
    padded = (jnp.sum(cpad, axis=0) + MOE_BM - 1) // MOE_BM * MOE_BM
    pad_end = jnp.cumsum(padded)
    seg_begin = (pad_end - padded)[None, :] + goff
    n_active = (pad_end[-1] // MOE_BM).astype(i32)
    blk = jnp.minimum(jnp.arange(n_blocks, dtype=i32), n_active - 1)
    block_exp = jnp.minimum(jnp.sum((pad_end[None, :] <= (blk * MOE_BM)[:, None]).astype(i32), axis=1),
                            N_EXPERTS - 1)
    e_ids = jnp.arange(N_EXPERTS, dtype=i32)
    nonempty = padded > 0
    nxt_e = jnp.min(jnp.where((e_ids[None, :] > e_ids[:, None]) & nonempty[None, :], e_ids[None, :], N_EXPERTS),
                    axis=1)
    nxt_e = jnp.where(nxt_e == N_EXPERTS, -1, nxt_e)
    ord_e = jnp.cumsum(nonempty.astype(i32)) - 1
    be_hot = block_exp[:, None] == e_ids[None, :]
    next_exp = jnp.sum(jnp.where(be_hot, nxt_e[None, :], 0), axis=1).astype(i32)
    wslot = (jnp.sum(jnp.where(be_hot, ord_e[None, :], 0), axis=1) % 2).astype(i32)

    seg_src = jnp.arange(n_tiles, dtype=i32)[:, None] * GROUP_R + lstart
    sb, sl, ss = seg_begin.reshape(-1), cpad.reshape(-1), seg_src.reshape(-1)
    rc = jnp.arange(n_blocks * MOE_BM // SUBLANES, dtype=i32)[:, None] * SUBLANES
    inseg = (sb[None, :] <= rc) & (rc < (sb + sl)[None, :])
    ctab_e = jnp.where(jnp.any(inseg, axis=1), jnp.sum(jnp.where(inseg, (ss - sb)[None, :] + rc, 0), axis=1),
                       GROUP_R - SUBLANES)
    lr = jnp.arange(GROUP_R // SUBLANES, dtype=i32)[None, :, None] * SUBLANES
    inl = (lstart[:, None, :] <= lr) & (lr < (lstart + cpad)[:, None, :])
    ctab_c = jnp.sum(jnp.where(inl, (seg_begin - lstart)[:, None, :] + lr, 0), axis=2)

    outs = _expert_ffn(block_exp, n_active.reshape(1), wslot, next_exp,
                       ctab_e.astype(i32).reshape(n_blocks, 1, MOE_BM // SUBLANES), xg, w_gate_up[0],
                       b_gate_up[0].reshape(N_EXPERTS, 1, 2 * D_FF), w_down[0],
                       b_down[0].reshape(N_EXPERTS, 1, D_MODEL))
    y_p, y_s = _combine(ctab_c.astype(i32).reshape(n_tiles, 1, GROUP_R // SUBLANES), outs, x1, meta,
                        g_final.reshape(1, D_MODEL), tp, ts)

    kv_tail = lambda a: jnp.concatenate([a[(b + 1) * S - WINDOW:(b + 1) * S] for b in range(B)], axis=0).reshape(
        1, B, WINDOW, N_KV, HEAD_DIM)
    c_e = ctp[:, :, :HEAD_DIM, :HEAD_DIM]
    c_o = ctp[:, :, HEAD_DIM:LANES, HEAD_DIM:]
    c_p = jnp.swapaxes(jnp.stack([c_e, c_o], axis=2), -1, -2).reshape(B, N_HEADS, HEAD_DIM, HEAD_DIM)
    n_p = (ctp[:, :, LANES:LANES + 2, :HEAD_DIM] + ctp[:, :, LANES:LANES + 2, HEAD_DIM:]).reshape(B, N_HEADS, HEAD_DIM)
    return (y_p.reshape(B, S, D_MODEL), y_s.reshape(Bd, Tn, D_MODEL),
            kv_tail(ka), kv_tail(va), c_p[None], n_p[None], m_p[:, :, 0][None],
            k_s.reshape(Bd, WINDOW, N_KV, HEAD_DIM)[None], v_s.reshape(Bd, WINDOW, N_KV, HEAD_DIM)[None],
            c_s[None], n_s[None], m_s[None])
```

```python
import functools

import jax
import jax.numpy as jnp
import numpy as np
from jax import lax
from jax.experimental import pallas as pl
from jax.experimental.pallas import tpu as pltpu

F32 = jnp.float32
BF16 = jnp.bfloat16
HIGHEST = lax.Precision.HIGHEST

D_MODEL = 1024
HEAD_DIM = 64
N_HEADS = 8
N_PAIRS = N_HEADS // 2
N_KV = 2
GROUP = N_HEADS // N_KV
WINDOW = 128
N_EXPERTS = 32
TOP_K = 4
D_FF = 1024
SWIGLU_LIMIT = 7.0
SWIGLU_ALPHA = 1.702
RMS_EPS = 1e-5
DM = N_HEADS * HEAD_DIM
DKV = N_KV * HEAD_DIM
NEG = -1e30

LANES = 128
SUBLANES = 8
VMEM_LIMIT = 56 * 1024 * 1024

TM = 512
ROUTE_T = 512
MLSTM_TL = 512
MLSTM_L = 128
SWA_BLOCKS = 8
MOE_BM = 256
X_SLOTS = 3
GROUP_R = -(-(TOP_K * ROUTE_T + N_EXPERTS * (SUBLANES - 1) + SUBLANES) // LANES) * LANES
SORT_CHUNKS = 3


def _cparams(sem, vmem=None):
    return pltpu.CompilerParams(dimension_semantics=sem, vmem_limit_bytes=vmem)


def _rms(x, g):
    return x * lax.rsqrt(jnp.mean(x * x, axis=-1, keepdims=True) + RMS_EPS) * g


def _log_sigmoid(z):
    return jnp.minimum(z, 0.0) - jnp.log(1.0 + jnp.exp(-jnp.abs(z)))


def _sigmoid(z):
    return 1.0 / (1.0 + jnp.exp(-z))


def _inproj_kernel(n_ptiles, xp_ref, xs_ref, g_ref, w1_ref, wt_ref, bcol_ref, brow_ref,
                   km_ref, qa_ref, ka_ref, va_ref, qmt_ref, vmt_ref, omt_ref, gcol_ref, grow_ref):
    i = pl.program_id(0)
    x = jnp.where(i < n_ptiles, xp_ref[...], xs_ref[...])
    h = _rms(x, g_ref[...]).astype(BF16)
    main = jnp.dot(h, w1_ref[...], preferred_element_type=F32)
    km_ref[...] = main[:, 0:DM] * (HEAD_DIM ** -0.5)
    qa_ref[...] = main[:, DM:2 * DM]
    ka_ref[...] = main[:, 2 * DM:2 * DM + DKV]
    va_ref[...] = main[:, 2 * DM + DKV:2 * DM + 2 * DKV]
    t = lax.dot_general(wt_ref[...], h, (((1,), (1,)), ((), ())), preferred_element_type=F32)
    qmt_ref[...] = t[0:DM]
    vmt_ref[...] = t[DM:2 * DM]
    omt_ref[...] = t[2 * DM:3 * DM]
    zc = main[:, 2 * DM + 2 * DKV:] + bcol_ref[...]
    lane = lax.broadcasted_iota(jnp.int32, zc.shape, 1)
    gcol_ref[...] = jnp.where(lane < N_HEADS, zc, _log_sigmoid(zc))
    zr = t[3 * DM:] + brow_ref[...]
    row = lax.broadcasted_iota(jnp.int32, zr.shape, 0)
    grow_ref[...] = jnp.where(row < N_HEADS, zr, _log_sigmoid(zr))


def _inproj(xp, xs, g_mix, w1, wkt, bcol, brow):
    tp, ts = xp.shape[0], xs.shape[0]
    n_pt, n_st = tp // TM, ts // TM
    t_all = tp + ts
    tok = lambda w: pl.BlockSpec((TM, w), lambda i: (i, 0))
    tr = lambda r: pl.BlockSpec((r, TM), lambda i: (0, i))
    full = lambda a: pl.BlockSpec(a.shape, lambda i: (0,) * a.ndim)
    out_shape = (
        jax.ShapeDtypeStruct((t_all, DM), F32), jax.ShapeDtypeStruct((t_all, DM), F32),
        jax.ShapeDtypeStruct((t_all, DKV), F32), jax.ShapeDtypeStruct((t_all, DKV), F32),
        jax.ShapeDtypeStruct((DM, t_all), F32), jax.ShapeDtypeStruct((DM, t_all), F32),
        jax.ShapeDtypeStruct((DM, t_all), F32),
        jax.ShapeDtypeStruct((t_all, LANES), F32), jax.ShapeDtypeStruct((2 * N_HEADS, t_all), F32),
    )
    return pl.pallas_call(
        functools.partial(_inproj_kernel, n_pt),
        grid=(n_pt + n_st,),
        in_specs=[
            pl.BlockSpec((TM, D_MODEL), lambda i: (jnp.minimum(i, n_pt - 1), 0)),
            pl.BlockSpec((TM, D_MODEL), lambda i: (jnp.maximum(i - n_pt, 0), 0)),
            full(g_mix), full(w1), full(wkt), full(bcol), full(brow),
        ],
        out_specs=(tok(DM), tok(DM), tok(DKV), tok(DKV), tr(DM), tr(DM), tr(DM),
                   tok(LANES), tr(2 * N_HEADS)),
        out_shape=out_shape,
        compiler_params=_cparams(("arbitrary",), VMEM_LIMIT),
        name="inproj",
    )(xp, xs, g_mix, w1, wkt, bcol, brow)


CT_ROWS = LANES + 2 * SUBLANES


def _cumsum_rows(x, n):
    row = lax.broadcasted_iota(jnp.int32, x.shape, 0)
    sh = 1
    while sh < n:
        x = x + jnp.where(row >= sh, pltpu.roll(x, sh, axis=0), 0.0)
        sh *= 2
    return x


def _mlstm_prompt_kernel(km_ref, qmt_ref, vmt_ref, omt_ref, gcol_ref, grow_ref, ghr_ref,
                         hm_ref, ct_ref, m_ref, ct_s, m_s):
    j = pl.program_id(1)
    L = MLSTM_L
    assert L == LANES

    @pl.when(j == 0)
    def _():
        ct_s[...] = jnp.zeros_like(ct_s)
        m_s[...] = jnp.zeros_like(m_s)

    si = lax.broadcasted_iota(jnp.int32, (L, L), 0)
    ti = lax.broadcasted_iota(jnp.int32, (L, L), 1)
    causal_t = si <= ti
    upper = jnp.where(causal_t, 1.0, 0.0)
    rows_c = lax.broadcasted_iota(jnp.int32, (CT_ROWS, 1), 0)
    rmask_e = (rows_c < HEAD_DIM) | (rows_c == LANES)
    rmask_o = ((rows_c >= HEAD_DIM) & (rows_c < LANES)) | (rows_c == LANES + 1)
    rows_e = lax.broadcasted_iota(jnp.int32, (LANES, 1), 0) < HEAD_DIM
    cols_e = lax.broadcasted_iota(jnp.int32, (1, LANES), 1) < HEAD_DIM
    bdt_mask = (rmask_e & cols_e) | (rmask_o & (~cols_e))
    ones_rows = jnp.where(lax.broadcasted_iota(jnp.int32, (CT_ROWS - LANES, L), 0) < 2, 1.0, 0.0)

    n_chunks = MLSTM_TL // L
    b_rows = jnp.dot(jnp.concatenate([grow_ref[N_HEADS:2 * N_HEADS, c * L:(c + 1) * L] for c in range(n_chunks)],
                                     axis=0), upper, precision=HIGHEST, preferred_element_type=F32)
    for c in range(n_chunks):
        sl = slice(c * L, (c + 1) * L)
        i_row = grow_ref[0:N_HEADS, sl]
        b_row = b_rows[c * N_HEADS:(c + 1) * N_HEADS]
        gc = gcol_ref[sl, :]
        bc_all = _cumsum_rows(gc, L)
        for p in range(N_PAIRS):
            ls = slice(p * LANES, (p + 1) * LANES)
            k2 = km_ref[sl, ls]
            qt2 = qmt_ref[ls, sl]
            qt_e = jnp.where(rows_e, qt2, 0.0).astype(BF16)
            qt_o = jnp.where(rows_e, 0.0, qt2).astype(BF16)
            st2 = jnp.dot(k2.astype(BF16), jnp.concatenate([qt_e, qt_o], axis=1),
                          preferred_element_type=F32)
            ct = ct_s[p]
            rqt = jnp.dot(ct.astype(BF16), qt2.astype(BF16), preferred_element_type=F32)
            pts, mts, inters, wreps, decays, mnews = [], [], [], [], [], []
            for hh in range(2):
                h = 2 * p + hh
                cvec = jnp.broadcast_to(gc[:, h:h + 1] - bc_all[:, N_HEADS + h:N_HEADS + h + 1], (L, L))
                brow = b_row[h:h + 1, :]
                logdt = jnp.where(causal_t, cvec + brow, NEG)
                m_prev = m_s[h:h + 1, 0:1]
                m_inter = m_prev + brow
                m_t = jnp.maximum(m_inter, jnp.max(logdt, axis=0, keepdims=True))
                pts.append((st2[:, hh * L:(hh + 1) * L] * jnp.exp(logdt - m_t)).astype(BF16))
                mts.append(m_t)
                inters.append(jnp.exp(m_inter - m_t))
                m_new = m_t[:, L - 1:L]
                b_last = brow[:, L - 1:L]
                decays.append(jnp.exp(m_prev + b_last - m_new))
                wreps.append(jnp.exp(cvec + (b_last - m_new)))
                mnews.append(m_new)
            vext = jnp.concatenate([vmt_ref[ls, sl], ones_rows], axis=0)
            lhs = jnp.concatenate([jnp.where(rmask_e, vext, 0.0), jnp.where(rmask_o, vext, 0.0)],
                                  axis=1).astype(BF16)
            rt = jnp.dot(lhs, jnp.concatenate(pts, axis=0), preferred_element_type=F32)
            ndt = rt + jnp.where(rmask_e, inters[0], inters[1]) * rqt
            den = jnp.where(rows_e, ndt[LANES:LANES + 1], ndt[LANES + 1:LANES + 2])
            mt2 = jnp.where(rows_e, mts[0], mts[1])
            hvt = ndt[0:LANES] / jnp.maximum(jnp.abs(den), jnp.exp(-mt2))
            sq = hvt * hvt
            ms = jnp.where(rows_e, jnp.sum(sq[0:HEAD_DIM], axis=0, keepdims=True),
                           jnp.sum(sq[HEAD_DIM:LANES], axis=0, keepdims=True)) * (1.0 / HEAD_DIM)
            yt = hvt * lax.rsqrt(ms + RMS_EPS) * ghr_ref[ls, :] * _sigmoid(omt_ref[ls, sl])
            hm_ref[sl, ls] = jnp.transpose(yt).astype(hm_ref.dtype)
            kw = (k2 * jnp.where(cols_e, wreps[0], wreps[1])).astype(BF16)
            upd = jnp.dot(vext.astype(BF16), kw, preferred_element_type=F32)
            ct_s[p] = jnp.where(rmask_e, decays[0], decays[1]) * ct + jnp.where(bdt_mask, upd, 0.0)
            for hh in range(2):
                h = 2 * p + hh
                m_s[h:h + 1, :] = jnp.broadcast_to(mnews[hh], (1, LANES))

    @pl.when(j == pl.num_programs(1) - 1)
    def _():
        ct_ref[0] = ct_s[...]
        m_ref[0] = m_s[...]


def _mlstm_prompt(km, qmt, vmt, omt, gcol, grow, ghr, batch, seq):
    nt = seq // MLSTM_TL
    tokb = lambda w: pl.BlockSpec((MLSTM_TL, w), lambda b, j: (b * nt + j, 0))
    rowb = lambda r: pl.BlockSpec((r, MLSTM_TL), lambda b, j: (0, b * nt + j))
    return pl.pallas_call(
        _mlstm_prompt_kernel,
        grid=(batch, nt),
        in_specs=[tokb(DM), rowb(DM), rowb(DM), rowb(DM), tokb(LANES), rowb(2 * N_HEADS),
                  pl.BlockSpec((DM, LANES), lambda b, j: (0, 0))],
        out_specs=(tokb(DM),
                   pl.BlockSpec((1, N_PAIRS, CT_ROWS, LANES), lambda b, j: (b, 0, 0, 0)),
                   pl.BlockSpec((1, N_HEADS, LANES), lambda b, j: (b, 0, 0))),
        out_shape=(jax.ShapeDtypeStruct((batch * seq, DM), BF16),
                   jax.ShapeDtypeStruct((batch, N_PAIRS, CT_ROWS, LANES), F32),
                   jax.ShapeDtypeStruct((batch, N_HEADS, LANES), F32)),
        scratch_shapes=[pltpu.VMEM((N_PAIRS, CT_ROWS, LANES), F32), pltpu.VMEM((N_HEADS, LANES), F32)],
        compiler_params=_cparams(("arbitrary", "arbitrary"), VMEM_LIMIT),
        name="mlstm_prompt",
    )(km, qmt, vmt, omt, gcol, grow, ghr)


def _mlstm_sample_kernel(n_tok, km_ref, qmt_ref, vmt_ref, omt_ref, gcol_ref, grow_ref, c0_ref, n0_ref, m0_ref,
                         m0t_ref, gh_ref, hm_ref, c_ref, n_ref, mt_ref):
    L = LANES
    NB = L // n_tok
    ti = lax.broadcasted_iota(jnp.int32, (L, L), 0)
    si = lax.broadcasted_iota(jnp.int32, (L, L), 1)
    same = (ti // n_tok) == (si // n_tok)
    causal = same & (ti >= si)
    useg = jnp.where(same & (ti <= si), 1.0, 0.0)
    slast = jnp.where(same & (ti % n_tok == n_tok - 1), 1.0, 0.0)
    expand = jnp.where(ti // n_tok == si, 1.0, 0.0)
    expand_t = jnp.where(ti == si // n_tok, 1.0, 0.0)
    pick = jnp.where((ti // n_tok == si) & (ti % n_tok == n_tok - 1), 1.0, 0.0)
    hdot = lambda a, b: jnp.dot(a, b, precision=HIGHEST, preferred_element_type=F32)

    lane128 = lax.broadcasted_iota(jnp.int32, (L, LANES), 1)
    even128 = lane128 < HEAD_DIM
    lane256 = lax.broadcasted_iota(jnp.int32, (1, 2 * LANES), 1)
    cols_e = (lane256 < HEAD_DIM) | (lane256 == LANES)
    cols_o = ((lane256 >= HEAD_DIM) & (lane256 < LANES)) | (lane256 == LANES + 1)
    rows_e = lax.broadcasted_iota(jnp.int32, (LANES, 1), 0) < HEAD_DIM
    ones_cols = jnp.where(lane128 < 2, 1.0, 0.0)
    bo_r = lax.broadcasted_iota(jnp.int32, (LANES, LANES), 0) // HEAD_DIM
    bo_c = lax.broadcasted_iota(jnp.int32, (LANES, LANES), 1) // HEAD_DIM
    block_ones = jnp.where(bo_r == bo_c, 1.0, 0.0)
    W = NB * LANES
    rb = lax.broadcasted_iota(jnp.int32, (L, W), 0)
    cb = lax.broadcasted_iota(jnp.int32, (L, W), 1)
    own_block = (rb // n_tok) == (cb // LANES)
    bd_tiled = (rb // HEAD_DIM) == ((cb % LANES) // HEAD_DIM)

    grow = grow_ref[...]
    i_row = grow[0:N_HEADS]
    b_row = hdot(grow[N_HEADS:2 * N_HEADS], useg)
    b_last = hdot(b_row, slast)
    a_row = b_last - b_row + i_row
    pos = lax.broadcasted_iota(jnp.int32, a_row.shape, 1) % n_tok
    pm = a_row
    sh = 1
    while sh < n_tok:
        pm = jnp.where(pos >= sh, jnp.maximum(pm, pltpu.roll(pm, sh, axis=1)), pm)
        sh *= 2
    m_carry = hdot(jnp.concatenate([m0t_ref[0], b_row], axis=1), jnp.concatenate([expand_t, slast], axis=0))
    m_new_row = jnp.maximum(m_carry, hdot(pm, slast))
    decay_row = jnp.exp(m_carry - m_new_row)
    w_row = jnp.exp(a_row - m_new_row)
    mt_ref[0] = hdot(m_new_row, pick)
    decay_bh = hdot(decay_row, pick)
    decay_hb = jnp.transpose(decay_bh)[0:NB]

    bc_all = gcol_ref[...]
    rowpos = lax.broadcasted_iota(jnp.int32, bc_all.shape, 0) % n_tok
    sh = 1
    while sh < n_tok:
        bc_all = bc_all + jnp.where(rowpos >= sh, pltpu.roll(bc_all, sh, axis=0), 0.0)
        sh *= 2
    pad_rows = lambda a: jnp.concatenate([a, jnp.zeros((L - NB, a.shape[1]), F32)], axis=0)
    m0_col = hdot(expand, pad_rows(m0_ref[...]))
    rowv_all = i_row - b_row

    for p in range(N_PAIRS):
        ls = slice(p * LANES, (p + 1) * LANES)
        q2f = jnp.transpose(qmt_ref[ls, :])
        q2 = q2f.astype(BF16)
        kt2 = jnp.transpose(km_ref[:, ls])
        v2 = jnp.transpose(vmt_ref[ls, :])
        vext = jnp.concatenate([v2, ones_cols], axis=1)
        kt_e = jnp.where(rows_e, kt2, 0.0).astype(BF16)
        kt_o = jnp.where(rows_e, 0.0, kt2).astype(BF16)
        s2 = jnp.dot(q2, jnp.concatenate([kt_e, kt_o], axis=1), preferred_element_type=F32)
        ps, mts, inters = [], [], []
        for hh in range(2):
            h = 2 * p + hh
            bcol = bc_all[:, N_HEADS + h:N_HEADS + h + 1]
            logd = jnp.where(causal, bcol + rowv_all[h:h + 1, :], NEG)
            m_inter = m0_col[:, h:h + 1] + bcol
            m_t = jnp.maximum(m_inter, jnp.max(logd, axis=1, keepdims=True))
            ps.append((s2[:, hh * L:(hh + 1) * L] * jnp.exp(logd - m_t)).astype(BF16))
            mts.append(m_t)
            inters.append(jnp.exp(m_inter - m_t))
        vstack = jnp.concatenate([jnp.where(cols_e, vext, 0.0), jnp.where(cols_o, vext, 0.0)],
                                 axis=0).astype(BF16)
        r = jnp.dot(jnp.concatenate(ps, axis=1), vstack, preferred_element_type=F32)
        zero = jnp.zeros((HEAD_DIM, HEAD_DIM), F32)
        cstack = jnp.concatenate(
            [jnp.concatenate([jnp.concatenate([c0_ref[b, 2 * p], zero], axis=1),
                              jnp.concatenate([zero, c0_ref[b, 2 * p + 1]], axis=1)], axis=0)
             for b in range(NB)], axis=1)
        rq_all = jnp.where(own_block, jnp.dot(q2, cstack.astype(BF16), preferred_element_type=F32), 0.0)
        rq = rq_all[:, 0:LANES]
        for b in range(1, NB):
            rq = rq + rq_all[:, b * LANES:(b + 1) * LANES]
        n_rows = hdot(expand, pad_rows(n0_ref[:, ls]))
        qn = hdot(q2f * n_rows, block_ones)
        inter2 = jnp.where(even128, inters[0], inters[1])
        num = r[:, 0:LANES] + inter2 * rq
        den = jnp.where(even128, r[:, LANES:LANES + 1], r[:, LANES + 1:LANES + 2]) + inter2 * qn
        mt2 = jnp.where(even128, mts[0], mts[1])
        hv = num / jnp.maximum(jnp.abs(den), jnp.exp(-mt2))
        ms = hdot(hv * hv, block_ones) * (1.0 / HEAD_DIM)
        y = hv * lax.rsqrt(ms + RMS_EPS) * gh_ref[:, ls] * _sigmoid(jnp.transpose(omt_ref[ls, :]))
        hm_ref[:, ls] = y.astype(hm_ref.dtype)
        kw = kt2 * jnp.where(rows_e, w_row[2 * p:2 * p + 1, :], w_row[2 * p + 1:2 * p + 2, :])
        vbd = jnp.where(own_block, jnp.concatenate([v2] * NB, axis=1), 0.0).astype(BF16)
        upd = jnp.dot(kw.astype(BF16), vbd, preferred_element_type=F32)
        upd = jnp.where(bd_tiled, upd, 0.0)
        for b in range(NB):
            bs = slice(b * LANES, (b + 1) * LANES)
            dec_b = jnp.where(rows_e, decay_hb[b:b + 1, 2 * p:2 * p + 1], decay_hb[b:b + 1, 2 * p + 1:2 * p + 2])
            cnew = dec_b * cstack[:, bs] + upd[:, bs]
            c_ref[b, 2 * p] = cnew[0:HEAD_DIM, 0:HEAD_DIM]
            c_ref[b, 2 * p + 1] = cnew[HEAD_DIM:LANES, HEAD_DIM:LANES]
        nsum = jnp.transpose(hdot(kw, expand))[0:NB]
        dec_n = jnp.where(lax.broadcasted_iota(jnp.int32, (NB, LANES), 1) < HEAD_DIM,
                          decay_hb[:, 2 * p:2 * p + 1], decay_hb[:, 2 * p + 1:2 * p + 2])
        n_ref[:, ls] = dec_n * n0_ref[:, ls] + nsum


def _mlstm_sample(km, qmt, vmt, omt, gcol, grow, c0, n0, m0, gh, row0, n_tok):
    nb = c0.shape[0]
    g_nb = LANES // n_tok
    n_g = nb // g_nb
    blk0 = row0 // LANES
    tokb = lambda w: pl.BlockSpec((LANES, w), lambda i: (blk0 + i, 0))
    rowb = lambda r: pl.BlockSpec((r, LANES), lambda i: (0, blk0 + i))
    m0t = jnp.pad(m0.reshape(n_g, g_nb, N_HEADS).transpose(0, 2, 1),
                  ((0, 0), (0, 0), (0, LANES - g_nb)))
    return pl.pallas_call(
        functools.partial(_mlstm_sample_kernel, n_tok),
        grid=(n_g,),
        in_specs=[tokb(DM), rowb(DM), rowb(DM), rowb(DM), tokb(LANES), rowb(2 * N_HEADS),
                  pl.BlockSpec((g_nb, N_HEADS, HEAD_DIM, HEAD_DIM), lambda i: (i, 0, 0, 0)),
                  pl.BlockSpec((g_nb, DM), lambda i: (i, 0)),
                  pl.BlockSpec((g_nb, N_HEADS), lambda i: (i, 0)),
                  pl.BlockSpec((1, N_HEADS, LANES), lambda i: (i, 0, 0)),
                  pl.BlockSpec((1, DM), lambda i: (0, 0))],
        out_specs=(pl.BlockSpec((LANES, DM), lambda i: (i, 0)),
                   pl.BlockSpec((g_nb, N_HEADS, HEAD_DIM, HEAD_DIM), lambda i: (i, 0, 0, 0)),
                   pl.BlockSpec((g_nb, DM), lambda i: (i, 0)),
                   pl.BlockSpec((1, N_HEADS, LANES), lambda i: (i, 0, 0))),
        out_shape=(jax.ShapeDtypeStruct((nb * n_tok, DM), BF16), jax.ShapeDtypeStruct(c0.shape, F32),
                   jax.ShapeDtypeStruct((nb, DM), F32), jax.ShapeDtypeStruct((n_g, N_HEADS, LANES), F32)),
        compiler_params=_cparams(("arbitrary",), VMEM_LIMIT),
        name="mlstm_sample",
    )(km, qmt, vmt, omt, gcol, grow, c0, n0.reshape(nb, DM), m0, m0t, gh)


def _alibi_slope(h):
    return float(np.float32(2.0 ** (-8.0 * (h + 1) / N_HEADS)))


def _dup_halves(x):
    lane = lax.broadcasted_iota(jnp.int32, x.shape, 1)
    xr = pltpu.roll(x, HEAD_DIM, axis=1)
    lo = lane < HEAD_DIM
    return jnp.where(lo, x, xr), jnp.where(lo, xr, x)


def _stack_group_queries(q, g):
    lane = lax.broadcasted_iota(jnp.int32, (q.shape[0], LANES), 1)
    parts = []
    for hh in range(GROUP):
        h = GROUP * g + hh
        blk = q[:, (h // 2) * LANES:(h // 2 + 1) * LANES]
        keep = (lane < HEAD_DIM) if h % 2 == 0 else (lane >= HEAD_DIM)
        parts.append(jnp.where(keep, blk, 0.0))
    return jnp.concatenate(parts, axis=0).astype(BF16)


def _swa_prompt_kernel(sink_ref, q_ref, kp_ref, ko_ref, vp_ref, vo_ref, o_ref):
    j = pl.program_id(1)
    R = WINDOW
    qi = lax.broadcasted_iota(jnp.int32, (R, R), 0)
    kj = lax.broadcasted_iota(jnp.int32, (R, R), 1)
    own = kj <= qi
    distf = jnp.where(own, qi - kj, qi - kj + R).astype(F32)
    lane = lax.broadcasted_iota(jnp.int32, (R, LANES), 1)
    for u in range(SWA_BLOCKS):
        rs = slice(u * R, (u + 1) * R)
        q = q_ref[rs, :]
        k_prev = kp_ref[...] if u == 0 else ko_ref[(u - 1) * R:u * R, :]
        v_prev = vp_ref[...] if u == 0 else vo_ref[(u - 1) * R:u * R, :]
        kd = _dup_halves(jnp.concatenate([k_prev, ko_ref[rs, :]], axis=0))
        vd = _dup_halves(jnp.concatenate([v_prev, vo_ref[rs, :]], axis=0))
        valid = (own | (j > 0)) if u == 0 else None
        outs = []
        for g in range(N_KV):
            qs = _stack_group_queries(q, g)
            s = lax.dot_general(qs, kd[g].astype(BF16), (((1,), (1,)), ((), ())),
                                preferred_element_type=F32) * (HEAD_DIM ** -0.5)
            ps = []
            for hh in range(GROUP):
                h = GROUP * g + hh
                sink = sink_ref[h]
                rows = slice(hh * R, (hh + 1) * R)
                sh = jnp.where(own, s[rows, R:], s[rows, :R]) - _alibi_slope(h) * distf
                if valid is not None:
                    sh = jnp.where(valid, sh, NEG)
                mx = jnp.maximum(jnp.max(sh, axis=1, keepdims=True), sink)
                p = jnp.exp(sh - mx)
                p = p / (jnp.sum(p, axis=1, keepdims=True) + jnp.exp(sink - mx))
                ps.append(jnp.concatenate([jnp.where(own, 0.0, p), jnp.where(own, p, 0.0)], axis=1).astype(BF16))
            o = jnp.dot(jnp.concatenate(ps, axis=0), vd[g].astype(BF16), preferred_element_type=F32)
            for pp in range(GROUP // 2):
                outs.append(jnp.where(lane < HEAD_DIM, o[(2 * pp) * R:(2 * pp + 1) * R],
                                      o[(2 * pp + 1) * R:(2 * pp + 2) * R]))
        o_ref[rs, :] = jnp.concatenate(outs, axis=1).astype(o_ref.dtype)


def _swa_prompt(sinks, qa, ka, va, batch, seq):
    nb = seq // (WINDOW * SWA_BLOCKS)
    own = lambda w: pl.BlockSpec((WINDOW * SWA_BLOCKS, w), lambda b, j: (b * nb + j, 0))
    prev = lambda w: pl.BlockSpec(
        (WINDOW, w), lambda b, j: (b * nb * SWA_BLOCKS + jnp.maximum(j * SWA_BLOCKS - 1, 0), 0))
    return pl.pallas_call(
        _swa_prompt_kernel,
        grid=(batch, nb),
        in_specs=[pl.BlockSpec(memory_space=pltpu.SMEM), own(DM), prev(DKV), own(DKV), prev(DKV), own(DKV)],
        out_specs=own(DM),
        out_shape=jax.ShapeDtypeStruct((batch * seq, DM), BF16),
        compiler_params=_cparams(("arbitrary", "arbitrary"), VMEM_LIMIT),
        name="swa_prompt",
    )(sinks, qa, ka, ka, va, va)


def _swa_decode_kernel(n_tok, sink_ref, q_ref, kn_ref, vn_ref, kc_ref, vc_ref, o_ref, ko_ref, vo_ref):
    W = WINDOW
    L = LANES
    NB = L // n_tok
    for b in range(NB):
        ko_ref[b, 0:W - n_tok, :] = kc_ref[b, n_tok:W, :]
        ko_ref[b, W - n_tok:W, :] = kn_ref[b * n_tok:(b + 1) * n_tok, :]
        vo_ref[b, 0:W - n_tok, :] = vc_ref[b, n_tok:W, :]
        vo_ref[b, W - n_tok:W, :] = vn_ref[b * n_tok:(b + 1) * n_tok, :]
    q = q_ref[...]
    knd, vnd = _dup_halves(kn_ref[...]), _dup_halves(vn_ref[...])
    kcd = _dup_halves(kc_ref[...].reshape(NB * W, DKV))
    vcd = _dup_halves(vc_ref[...].reshape(NB * W, DKV))
    ri = lax.broadcasted_iota(jnp.int32, (L, L), 0)
    ci = lax.broadcasted_iota(jnp.int32, (L, L), 1)
    t_q = ri % n_tok
    dist_n = t_q - ci % n_tok
    valid_n = ((ri // n_tok) == (ci // n_tok)) & (dist_n >= 0)
    dist_c = t_q + W - ci
    valid_c = dist_c < WINDOW
    dnf = dist_n.astype(F32)
    dcf = dist_c.astype(F32)
    own1 = (lax.broadcasted_iota(jnp.int32, (L, NB * W), 0) // n_tok
            == lax.broadcasted_iota(jnp.int32, (L, NB * W), 1) // W)
    own4 = ((lax.broadcasted_iota(jnp.int32, (GROUP * L, NB * W), 0) % L) // n_tok
            == lax.broadcasted_iota(jnp.int32, (GROUP * L, NB * W), 1) // W)
    lane = lax.broadcasted_iota(jnp.int32, (L, LANES), 1)
    nt = (((1,), (1,)), ((), ()))
    outs = []
    for g in range(N_KV):
        qs = _stack_group_queries(q, g)
        sn = lax.dot_general(qs, knd[g].astype(BF16), nt, preferred_element_type=F32) * (HEAD_DIM ** -0.5)
        sc_all = lax.dot_general(qs, kcd[g].astype(BF16), nt, preferred_element_type=F32) * (HEAD_DIM ** -0.5)
        pcs, pns = [], []
        for hh in range(GROUP):
            h = GROUP * g + hh
            sink = sink_ref[h]
            slope = _alibi_slope(h)
            rows = slice(hh * L, (hh + 1) * L)
            blk = jnp.where(own1, sc_all[rows], 0.0)
            sc = blk[:, 0:W]
            for b in range(1, NB):
                sc = sc + blk[:, b * W:(b + 1) * W]
            shc = jnp.where(valid_c, sc - slope * dcf, NEG)
            shn = jnp.where(valid_n, sn[rows] - slope * dnf, NEG)
            mx = jnp.maximum(jnp.maximum(jnp.max(shc, axis=1, keepdims=True),
                                         jnp.max(shn, axis=1, keepdims=True)), sink)
            pc = jnp.exp(shc - mx)
            pn = jnp.exp(shn - mx)
            inv = 1.0 / (jnp.sum(pc, axis=1, keepdims=True) + jnp.sum(pn, axis=1, keepdims=True)
                         + jnp.exp(sink - mx))
            pcs.append(pc * inv)
            pns.append((pn * inv).astype(BF16))
        pc4 = jnp.concatenate(pcs, axis=0)
        p_bd = jnp.where(own4, jnp.concatenate([pc4] * NB, axis=1), 0.0).astype(BF16)
        o = (jnp.dot(p_bd, vcd[g].astype(BF16), preferred_element_type=F32)
             + jnp.dot(jnp.concatenate(pns, axis=0), vnd[g].astype(BF16), preferred_element_type=F32))
        for pp in range(GROUP // 2):
            outs.append(jnp.where(lane < HEAD_DIM, o[(2 * pp) * L:(2 * pp + 1) * L],
                                  o[(2 * pp + 1) * L:(2 * pp + 2) * L]))
    o_ref[...] = jnp.concatenate(outs, axis=1)


def _swa_decode(sinks, qa, ka, va, kc, vc, row0, n_tok):
    nb = kc.shape[0]
    g_nb = LANES // n_tok
    blk0 = row0 // LANES
    tokb = lambda w: pl.BlockSpec((LANES, w), lambda i: (blk0 + i, 0))
    cache = pl.BlockSpec((g_nb, WINDOW, DKV), lambda i: (i, 0, 0))
    return pl.pallas_call(
        functools.partial(_swa_decode_kernel, n_tok),
        grid=(nb // g_nb,),
        in_specs=[pl.BlockSpec(memory_space=pltpu.SMEM), tokb(DM), tokb(DKV), tokb(DKV), cache, cache],
        out_specs=(pl.BlockSpec((LANES, DM), lambda i: (i, 0)), cache, cache),
        out_shape=(jax.ShapeDtypeStruct((nb * n_tok, DM), F32),
                   jax.ShapeDtypeStruct(kc.shape, F32), jax.ShapeDtypeStruct(vc.shape, F32)),
        compiler_params=_cparams(("arbitrary",), VMEM_LIMIT),
        name="swa_decode",
    )(sinks, qa, ka, va, kc, vc)


def _outproj_router_kernel(n_ptiles, xp_ref, xs_ref, hmp_ref, hap_ref, hms_ref, has_ref, wom_ref, woa_ref,
                           g_ref, wrt_ref, br_ref,
                           x1_ref, xg_ref, meta_ref, cnt_ref):
    i = pl.program_id(0)
    is_p = i < n_ptiles
    x = jnp.where(is_p, xp_ref[...], xs_ref[...])
    hm = jnp.where(is_p, hmp_ref[...], hms_ref[...].astype(BF16))
    ha = jnp.where(is_p, hap_ref[...], has_ref[...].astype(BF16))
    x1 = (x + jnp.dot(hm, wom_ref[...], preferred_element_type=F32)
          + jnp.dot(ha, woa_ref[...], preferred_element_type=F32))
    x1_ref[...] = x1
    h2 = _rms(x1, g_ref[...])
    nt = (((1,), (1,)), ((), ()))
    h2_hi = h2.astype(BF16)
    h2_lo = (h2 - h2_hi.astype(F32)).astype(BF16)
    wr = wrt_ref[...]
    wr_hi = wr.astype(BF16)
    wr_lo = (wr - wr_hi.astype(F32)).astype(BF16)
    r_hi = lax.dot_general(jnp.concatenate([wr_hi, wr_lo], axis=0), h2_hi, nt, preferred_element_type=F32)
    logits = (r_hi[0:N_EXPERTS] + r_hi[N_EXPERTS:]
              + lax.dot_general(wr_hi, h2_lo, nt, preferred_element_type=F32)) + br_ref[...]
    eidx = lax.broadcasted_iota(jnp.int32, logits.shape, 0).astype(F32)
    work = logits
    vals, hots = [], []
    for _ in range(TOP_K):
        mv = jnp.max(work, axis=0, keepdims=True)
        sel = jnp.min(jnp.where(work == mv, eidx, float(N_EXPERTS)), axis=0, keepdims=True)
        hot = eidx == sel
        vals.append(mv)
        hots.append(hot)
        work = jnp.where(hot, -jnp.inf, work)
    es = [jnp.exp(v - vals[0]) for v in vals]
    tot = es[0] + es[1] + es[2] + es[3]
    gates = [e / tot for e in es]
    hot_all = jnp.where(hots[0] | hots[1] | hots[2] | hots[3], 1.0, 0.0)
    tm = logits.shape[1]
    su = (lax.broadcasted_iota(jnp.int32, (tm, tm), 0) < lax.broadcasted_iota(jnp.int32, (tm, tm), 1))
    cum = jnp.dot(hot_all.astype(BF16), su.astype(BF16), preferred_element_type=F32)
    cnt = jnp.sum(hot_all, axis=1, keepdims=True)
    cpad = (((cnt.astype(jnp.int32) + (SUBLANES - 1)) // SUBLANES) * SUBLANES).astype(F32)
    lower = (lax.broadcasted_iota(jnp.int32, (N_EXPERTS, N_EXPERTS), 0)
             > lax.broadcasted_iota(jnp.int32, (N_EXPERTS, N_EXPERTS), 1)).astype(F32)
    lstart = jnp.dot(lower, jnp.broadcast_to(cpad, (N_EXPERTS, LANES)), precision=HIGHEST,
                     preferred_element_type=F32)[:, 0:1]
    base = lstart + cum
    lpos = [jnp.sum(jnp.where(hot, base, 0.0), axis=0, keepdims=True) for hot in hots]
    lpi = [p.astype(jnp.int32) for p in lpos]
    h2b = h2.astype(BF16)
    rows_per = GROUP_R // SORT_CHUNKS
    for j in range(SORT_CHUNKS):
        r_iota = lax.broadcasted_iota(jnp.int32, (rows_per, tm), 0) + j * rows_per
        sel01 = jnp.where(r_iota == lpi[0], 1.0, jnp.where(r_iota == lpi[1], 1.0, jnp.where(
            r_iota == lpi[2], 1.0, jnp.where(r_iota == lpi[3], 1.0, 0.0)))).astype(BF16)
        xg_ref[j * rows_per:(j + 1) * rows_per, :] = jnp.dot(sel01, h2b, preferred_element_type=F32)
    meta_ref[...] = jnp.transpose(jnp.concatenate(gates + lpos, axis=0))
    cnt_ref[0] = jnp.broadcast_to(cnt, (N_EXPERTS, LANES))


def _outproj_router(xp, xs, hmp, hap, hms, has, wom, woa, g_ffn, wrt, br):
    tp, ts = xp.shape[0], xs.shape[0]
    n_pt, n_st = tp // ROUTE_T, ts // ROUTE_T
    t_all = tp + ts
    pblk = lambda w: pl.BlockSpec((ROUTE_T, w), lambda i: (jnp.minimum(i, n_pt - 1), 0))
    sblk = lambda w: pl.BlockSpec((ROUTE_T, w), lambda i: (jnp.maximum(i - n_pt, 0), 0))
    full = lambda a: pl.BlockSpec(a.shape, lambda i: (0,) * a.ndim)
    return pl.pallas_call(
        functools.partial(_outproj_router_kernel, n_pt),
        grid=(n_pt + n_st,),
        in_specs=[pblk(D_MODEL), sblk(D_MODEL), pblk(DM), pblk(DM), sblk(DM), sblk(DM),
                  full(wom), full(woa), full(g_ffn), full(wrt), full(br)],
        out_specs=(pl.BlockSpec((ROUTE_T, D_MODEL), lambda i: (i, 0)),
                   pl.BlockSpec((GROUP_R, D_MODEL), lambda i: (i, 0)),
                   pl.BlockSpec((ROUTE_T, 2 * TOP_K), lambda i: (i, 0)),
                   pl.BlockSpec((1, N_EXPERTS, LANES), lambda i: (i, 0, 0))),
        out_shape=(jax.ShapeDtypeStruct((t_all, D_MODEL), F32),
                   jax.ShapeDtypeStruct(((n_pt + n_st) * GROUP_R, D_MODEL), F32),
                   jax.ShapeDtypeStruct((t_all, 2 * TOP_K), F32),
                   jax.ShapeDtypeStruct((n_pt + n_st, N_EXPERTS, LANES), F32)),
        compiler_params=_cparams(("arbitrary",), VMEM_LIMIT),
        name="outproj_router",
    )(xp, xs, hmp, hap, hms, has, wom, woa, g_ffn, wrt, br)


def _expert_kernel(be_ref, na_ref, slot_ref, nxt_ref, ctab_ref, ctab1_ref, ctab2_ref, xg_ref, wgu_ref, bgu_ref,
                   wd_ref, bd_ref, o_ref, xbuf, wgu_f, wd_f, wgu_s, wd_s, xsem, wsem):
    i = pl.program_id(0)
    na = na_ref[0]

    def x_copies(tab_ref, slot):
        return [pltpu.make_async_copy(xg_ref.at[pl.ds(pl.multiple_of(tab_ref[0, 0, c], SUBLANES), SUBLANES)],
                                      xbuf.at[slot, pl.ds(c * SUBLANES, SUBLANES)], xsem.at[slot])
                for c in range(MOE_BM // SUBLANES)]

    def w_copies(e, slot):
        return [pltpu.make_async_copy(wgu_ref.at[e], wgu_f.at[slot], wsem.at[slot]),
                pltpu.make_async_copy(wd_ref.at[e], wd_f.at[slot], wsem.at[slot])]

    @pl.when(i == 0)
    def _():
        for cp in x_copies(ctab_ref, 0) + x_copies(ctab1_ref, 1) + w_copies(be_ref[0], 0):
            cp.start()

    @pl.when(i == na)
    def _():
        for cp in x_copies(ctab_ref, lax.rem(i, X_SLOTS)) + x_copies(ctab1_ref, lax.rem(i + 1, X_SLOTS)):
            cp.wait()

    @pl.when(i < na)
    def _():
        changed = (i == 0) | (be_ref[i] != be_ref[jnp.maximum(i - 1, 0)])
        wslot = slot_ref[i]
        xslot = lax.rem(i, X_SLOTS)

        @pl.when(changed)
        def _():
            for cp in w_copies(be_ref[i], wslot):
                cp.wait()

            @pl.when(nxt_ref[i] >= 0)
            def _():
                for cp in w_copies(nxt_ref[i], 1 - wslot):
                    cp.start(priority=1)

            wgu_s[...] = wgu_f[wslot].astype(BF16)
            wd_s[...] = wd_f[wslot].astype(BF16)

        for cp in x_copies(ctab_ref, xslot):
            cp.wait()

        x = xbuf[xslot].astype(BF16)
        gu = jnp.dot(x, wgu_s[...], preferred_element_type=F32) + bgu_ref[...]
        for cp in x_copies(ctab2_ref, lax.rem(i + 2, X_SLOTS)):
            cp.start()
        gate = jnp.minimum(gu[:, :D_FF], SWIGLU_LIMIT)
        up = jnp.clip(gu[:, D_FF:], -SWIGLU_LIMIT, SWIGLU_LIMIT)
        act = (up + 1.0) * (gate * _sigmoid(gate * SWIGLU_ALPHA))
        o_ref[...] = jnp.dot(act.astype(BF16), wd_s[...], preferred_element_type=F32) + bd_ref[...]


def _expert_ffn(block_exp, n_active, wslot, next_exp, ctab, xg, wgu, bgu, wd, bd):
    n_blocks = ctab.shape[0]
    n_rows = n_blocks * MOE_BM
    nch = MOE_BM // SUBLANES
    tab = lambda d: pl.BlockSpec((1, 1, nch), lambda i, *_: (jnp.minimum(i + d, n_blocks - 1), 0, 0),
                                 memory_space=pltpu.SMEM)
    blk_e = lambda i, be: be[jnp.minimum(i, n_blocks - 1)]
    grid_spec = pltpu.PrefetchScalarGridSpec(
        num_scalar_prefetch=4,
        grid=(n_blocks + 1,),
        in_specs=[tab(0), tab(1), tab(2),
                  pl.BlockSpec(memory_space=pl.ANY),
                  pl.BlockSpec(memory_space=pl.ANY),
                  pl.BlockSpec((None, 1, 2 * D_FF), lambda i, be, *_: (blk_e(i, be), 0, 0)),
                  pl.BlockSpec(memory_space=pl.ANY),
                  pl.BlockSpec((None, 1, D_MODEL), lambda i, be, *_: (blk_e(i, be), 0, 0))],
        out_specs=pl.BlockSpec((MOE_BM, D_MODEL), lambda i, be, na, *_: (jnp.minimum(i, na[0] - 1), 0)),
        scratch_shapes=[pltpu.VMEM((X_SLOTS, MOE_BM, D_MODEL), F32),
                        pltpu.VMEM((2, D_MODEL, 2 * D_FF), F32), pltpu.VMEM((2, D_FF, D_MODEL), F32),
                        pltpu.VMEM((D_MODEL, 2 * D_FF), BF16), pltpu.VMEM((D_FF, D_MODEL), BF16),
                        pltpu.SemaphoreType.DMA((X_SLOTS,)), pltpu.SemaphoreType.DMA((2,))],
    )
    return pl.pallas_call(
        _expert_kernel,
        grid_spec=grid_spec,
        out_shape=jax.ShapeDtypeStruct((n_rows, D_MODEL), F32),
        compiler_params=_cparams(("arbitrary",), VMEM_LIMIT),
        name="moe_experts",
    )(block_exp, n_active, wslot, next_exp, ctab, ctab, ctab, xg, wgu, bgu, wd, bd)


def _combine_kernel(n_ptiles, n_tiles, ctab_ref, ctab1_ref, ctab2_ref, outs_ref, x1_ref, meta_ref, gf_ref,
                    yp_ref, ys_ref, obuf, sem):
    i = pl.program_id(0)

    def copies(tab_ref, s):
        return [pltpu.make_async_copy(outs_ref.at[pl.ds(pl.multiple_of(tab_ref[0, 0, c], SUBLANES), SUBLANES)],
                                      obuf.at[s, pl.ds(c * SUBLANES, SUBLANES)], sem.at[s])
                for c in range(GROUP_R // SUBLANES)]

    @pl.when(i == 0)
    def _():
        for cp in copies(ctab_ref, 0) + copies(ctab1_ref, 1):
            cp.start()

    @pl.when(i == n_tiles)
    def _():
        for cp in copies(ctab_ref, lax.rem(i, X_SLOTS)) + copies(ctab1_ref, lax.rem(i + 1, X_SLOTS)):
            cp.wait()

    @pl.when(i < n_tiles)
    def _():
        _combine_tile(i, n_ptiles, copies, ctab_ref, ctab2_ref, x1_ref, meta_ref, gf_ref, yp_ref, ys_ref, obuf)


def _combine_tile(i, n_ptiles, copies, ctab_ref, ctab2_ref, x1_ref, meta_ref, gf_ref, yp_ref, ys_ref, obuf):
    slot = lax.rem(i, X_SLOTS)
    for cp in copies(ctab_ref, slot):
        cp.wait()
    for cp in copies(ctab2_ref, lax.rem(i + 2, X_SLOTS)):
        cp.start()

    meta = meta_ref[...]
    tm = meta.shape[0]
    r_iota = lax.broadcasted_iota(jnp.int32, (tm, GROUP_R), 1)
    lp = [meta[:, TOP_K + k:TOP_K + k + 1].astype(jnp.int32) for k in range(TOP_K)]
    gk = [meta[:, k:k + 1] for k in range(TOP_K)]
    gsel = jnp.where(r_iota == lp[0], gk[0], jnp.where(r_iota == lp[1], gk[1], jnp.where(
        r_iota == lp[2], gk[2], jnp.where(r_iota == lp[3], gk[3], 0.0))))
    sel01 = jnp.where(gsel != 0.0, 1.0, 0.0).astype(BF16)
    rg_row = jnp.sum(gsel, axis=0, keepdims=True)
    rg_col = jnp.transpose(jnp.broadcast_to(rg_row, (SUBLANES, GROUP_R)))[:, 0:1]
    og = (obuf[slot] * rg_col).astype(BF16)
    acc = x1_ref[...] + jnp.dot(sel01, og, preferred_element_type=F32)
    y = _rms(acc, gf_ref[...])

    @pl.when(i < n_ptiles)
    def _():
        yp_ref[...] = y

    @pl.when(i >= n_ptiles)
    def _():
        ys_ref[...] = y


def _combine(ctab, outs, x1, meta, g_final, tp, ts):
    n_pt, n_st = tp // ROUTE_T, ts // ROUTE_T
    n = n_pt + n_st
    nch = GROUP_R // SUBLANES
    last = lambda i: jnp.minimum(i, n - 1)
    tab = lambda d: pl.BlockSpec((1, 1, nch), lambda i: (last(i + d), 0, 0), memory_space=pltpu.SMEM)
    return pl.pallas_call(
        functools.partial(_combine_kernel, n_pt, n),
        grid=(n + 1,),
        in_specs=[tab(0), tab(1), tab(2),
                  pl.BlockSpec(memory_space=pl.ANY),
                  pl.BlockSpec((ROUTE_T, D_MODEL), lambda i: (last(i), 0)),
                  pl.BlockSpec((ROUTE_T, 2 * TOP_K), lambda i: (last(i), 0)),
                  pl.BlockSpec((1, D_MODEL), lambda i: (0, 0))],
        out_specs=(pl.BlockSpec((ROUTE_T, D_MODEL), lambda i: (jnp.minimum(i, n_pt - 1), 0)),
                   pl.BlockSpec((ROUTE_T, D_MODEL), lambda i: (jnp.maximum(last(i) - n_pt, 0), 0))),
        out_shape=(jax.ShapeDtypeStruct((tp, D_MODEL), F32), jax.ShapeDtypeStruct((ts, D_MODEL), F32)),
        scratch_shapes=[pltpu.VMEM((X_SLOTS, GROUP_R, D_MODEL), F32), pltpu.SemaphoreType.DMA((X_SLOTS,))],
        compiler_params=_cparams(("arbitrary",), VMEM_LIMIT),
        name="moe_combine",
    )(ctab, ctab, ctab, outs, x1, meta, g_final)


def kernel(x_prompt, x_sample, cache_swa_k, cache_swa_v, state_mlstm_c, state_mlstm_n, state_mlstm_m,
           g_mix, w_in, b_igate, b_fgate, g_head, attn_sinks, w_out, g_ffn, w_router, b_router,
           w_gate_up, b_gate_up, w_down, b_down, g_final):
    assert w_in.shape[0] == 1, "single-layer problem"
    B, S, _ = x_prompt.shape
    Bd, Tn, _ = x_sample.shape
    tp, ts = B * S, Bd * Tn
    t_all = tp + ts
    xp = x_prompt.reshape(tp, D_MODEL)
    xs = x_sample.reshape(ts, D_MODEL)

    w = w_in[0]
    o = np.cumsum([0, DM, DM, DM, DM, N_HEADS, N_HEADS, DM, DKV, DKV])
    col = lambda a: w[:, int(o[a]):int(o[a + 1])]
    wgates = jnp.concatenate([col(4), col(5)], axis=1)
    w1 = jnp.concatenate([col(1), col(6), col(7), col(8), jnp.pad(wgates, ((0, 0), (0, LANES - 2 * N_HEADS)))],
                         axis=1).astype(BF16)
    wkt = jnp.concatenate([col(0), col(2), col(3), wgates], axis=1).T.astype(BF16)
    bg = jnp.concatenate([b_igate[0], b_fgate[0]]).astype(F32)
    bcol = jnp.pad(bg, (0, LANES - 2 * N_HEADS)).reshape(1, LANES)
    brow = bg.reshape(2 * N_HEADS, 1)

    km, qa, ka, va, qmt, vmt, omt, gcol, grow = _inproj(xp, xs, g_mix[0].reshape(1, D_MODEL), w1, wkt, bcol, brow)

    gh = g_head[0].astype(F32)
    sinks = attn_sinks[0].astype(F32)

    hm_p, ctp, m_p = _mlstm_prompt(km, qmt, vmt, omt, gcol, grow,
                                   jnp.broadcast_to(gh.reshape(DM, 1), (DM, LANES)), B, S)
    ha_p = _swa_prompt(sinks, qa, ka, va, B, S)

    hm_s, c_s, n_s, mt_s = _mlstm_sample(km, qmt, vmt, omt, gcol, grow, state_mlstm_c[0], state_mlstm_n[0],
                                         state_mlstm_m[0], gh.reshape(1, DM), tp, Tn)
    n_s = n_s.reshape(Bd, N_HEADS, HEAD_DIM)
    m_s = mt_s[:, :, :LANES // Tn].transpose(0, 2, 1).reshape(Bd, N_HEADS)
    ha_s, k_s, v_s = _swa_decode(sinks, qa, ka, va, cache_swa_k[0].reshape(Bd, WINDOW, DKV),
                                 cache_swa_v[0].reshape(Bd, WINDOW, DKV), tp, Tn)

    wo = w_out[0].astype(BF16)
    x1, xg, meta, cnt = _outproj_router(
        xp, xs, hm_p, ha_p, hm_s, ha_s, wo[:DM], wo[DM:], g_ffn[0].reshape(1, D_MODEL),
        w_router[0].T, b_router[0].reshape(N_EXPERTS, 1))

    i32 = jnp.int32
    n_tiles = t_all // ROUTE_T
    max_rows = t_all * TOP_K + n_tiles * N_EXPERTS * (SUBLANES - 1) + N_EXPERTS * (MOE_BM - 1)
    n_blocks = -(-max_rows // MOE_BM)
    cpad = (cnt[:, :, 0].astype(i32) + (SUBLANES - 1)) // SUBLANES * SUBLANES
    lstart = jnp.cumsum(cpad, axis=1) - cpad
    goff = jnp.cumsum(cpad, axis=0) - cpad
    padded = (jnp.sum(cpad, axis=0) + MOE_BM - 1) // MOE_BM * MOE_BM
    pad_end = jnp.cumsum(padded)
    seg_begin = (pad_end - padded)[None, :] + goff
    n_active = (pad_end[-1] // MOE_BM).astype(i32)
    blk = jnp.minimum(jnp.arange(n_blocks, dtype=i32), n_active - 1)
    block_exp = jnp.minimum(jnp.sum((pad_end[None, :] <= (blk * MOE_BM)[:, None]).astype(i32), axis=1),
                            N_EXPERTS - 1)
    e_ids = jnp.arange(N_EXPERTS, dtype=i32)
    nonempty = padded > 0
    nxt_e = jnp.min(jnp.where((e_ids[None, :] > e_ids[:, None]) & nonempty[None, :], e_ids[None, :], N_EXPERTS),
                    axis=1)
    nxt_e = jnp.where(nxt_e == N_EXPERTS, -1, nxt_e)
    ord_e = jnp.cumsum(nonempty.astype(i32)) - 1
    be_hot = block_exp[:, None] == e_ids[None, :]
    next_exp = jnp.sum(jnp.where(be_hot, nxt_e[None, :], 0), axis=1).astype(i32)
    wslot = (jnp.sum(jnp.where(be_hot, ord_e[None, :], 0), axis=1) % 2).astype(i32)

    seg_src = jnp.arange(n_tiles, dtype=i32)[:, None] * GROUP_R + lstart
    sb, sl, ss = seg_begin.reshape(-1), cpad.reshape(-1), seg_src.reshape(-1)
    rc = jnp.arange(n_blocks * MOE_BM // SUBLANES, dtype=i32)[:, None] * SUBLANES
    inseg = (sb[None, :] <= rc) & (rc < (sb + sl)[None, :])
    ctab_e = jnp.where(jnp.any(inseg, axis=1), jnp.sum(jnp.where(inseg, (ss - sb)[None, :] + rc, 0), axis=1),
                       GROUP_R - SUBLANES)
    lr = jnp.arange(GROUP_R // SUBLANES, dtype=i32)[None, :, None] * SUBLANES
    inl = (lstart[:, None, :] <= lr) & (lr < (lstart + cpad)[:, None, :])
    ctab_c = jnp.sum(jnp.where(inl, (seg_begin - lstart)[:, None, :] + lr, 0), axis=2)

    outs = _expert_ffn(block_exp, n_active.reshape(1), wslot, next_exp,
                       ctab_e.astype(i32).reshape(n_blocks, 1, MOE_BM // SUBLANES), xg, w_gate_up[0],
                       b_gate_up[0].reshape(N_EXPERTS, 1, 2 * D_FF), w_down[0],
                       b_down[0].reshape(N_EXPERTS, 1, D_MODEL))
    y_p, y_s = _combine(ctab_c.astype(i32).reshape(n_tiles, 1, GROUP_R // SUBLANES), outs, x1, meta,
                        g_final.reshape(1, D_MODEL), tp, ts)

    kv_tail = lambda a: jnp.concatenate([a[(b + 1) * S - WINDOW:(b + 1) * S] for b in range(B)], axis=0).reshape(
        1, B, WINDOW, N_KV, HEAD_DIM)
    c_e = ctp[:, :, :HEAD_DIM, :HEAD_DIM]
    c_o = ctp[:, :, HEAD_DIM:LANES, HEAD_DIM:]
    c_p = jnp.swapaxes(jnp.stack([c_e, c_o], axis=2), -1, -2).reshape(B, N_HEADS, HEAD_DIM, HEAD_DIM)
    n_p = (ctp[:, :, LANES:LANES + 2, :HEAD_DIM] + ctp[:, :, LANES:LANES + 2, HEAD_DIM:]).reshape(B, N_HEADS, HEAD_DIM)
    return (y_p.reshape(B, S, D_MODEL), y_s.reshape(Bd, Tn, D_MODEL),
            kv_tail(ka), kv_tail(va), c_p[None], n_p[None], m_p[:, :, 0][None],
            k_s.reshape(Bd, WINDOW, N_KV, HEAD_DIM)[None], v_s.reshape(Bd, WINDOW, N_KV, HEAD_DIM)[None],
            c_s[None], n_s[None], m_s[None])
```

```python
import functools

import jax
import jax.numpy as jnp
import numpy as np
from jax import lax
from jax.experimental import pallas as pl
from jax.experimental.pallas import tpu as pltpu

F32 = jnp.float32
BF16 = jnp.bfloat16
HIGHEST = lax.Precision.HIGHEST

D_MODEL = 1024
HEAD_DIM = 64
N_HEADS = 8
N_PAIRS = N_HEADS // 2
N_KV = 2
GROUP = N_HEADS // N_KV
WINDOW = 128
N_EXPERTS = 32
TOP_K = 4
D_FF = 1024
SWIGLU_LIMIT = 7.0
SWIGLU_ALPHA = 1.702
RMS_EPS = 1e-5
DM = N_HEADS * HEAD_DIM
DKV = N_KV * HEAD_DIM
NEG = -1e30

LANES = 128
SUBLANES = 8
VMEM_LIMIT = 56 * 1024 * 1024

TM = 512
ROUTE_T = 512
MLSTM_TL = 512
MLSTM_L = 128
SWA_BLOCKS = 8
MOE_BM = 256
X_SLOTS = 3
GROUP_R = -(-(TOP_K * ROUTE_T + N_EXPERTS * (SUBLANES - 1) + SUBLANES) // LANES) * LANES
SORT_CHUNKS = 3


def _cparams(sem, vmem=None):
    return pltpu.CompilerParams(dimension_semantics=sem, vmem_limit_bytes=vmem)


def _rms(x, g):
    return x * lax.rsqrt(jnp.mean(x * x, axis=-1, keepdims=True) + RMS_EPS) * g


def _log_sigmoid(z):
    return jnp.minimum(z, 0.0) - jnp.log(1.0 + jnp.exp(-jnp.abs(z)))


def _sigmoid(z):
    return 1.0 / (1.0 + jnp.exp(-z))


def _inproj_kernel(n_ptiles, xp_ref, xs_ref, g_ref, w1_ref, wt_ref, bcol_ref, brow_ref,
                   km_ref, qa_ref, ka_ref, va_ref, qmt_ref, vmt_ref, omt_ref, gcol_ref, grow_ref):
    i = pl.program_id(0)
    x = jnp.where(i < n_ptiles, xp_ref[...], xs_ref[...])
    h = _rms(x, g_ref[...]).astype(BF16)
    main = jnp.dot(h, w1_ref[...], preferred_element_type=F32)
    km_ref[...] = main[:, 0:DM] * (HEAD_DIM ** -0.5)
    qa_ref[...] = main[:, DM:2 * DM]
    ka_ref[...] = main[:, 2 * DM:2 * DM + DKV]
    va_ref[...] = main[:, 2 * DM + DKV:2 * DM + 2 * DKV]
    t = lax.dot_general(wt_ref[...], h, (((1,), (1,)), ((), ())), preferred_element_type=F32)
    qmt_ref[...] = t[0:DM]
    vmt_ref[...] = t[DM:2 * DM]
    omt_ref[...] = t[2 * DM:3 * DM]
    zc = main[:, 2 * DM + 2 * DKV:] + bcol_ref[...]
    lane = lax.broadcasted_iota(jnp.int32, zc.shape, 1)
    gcol_ref[...] = jnp.where(lane < N_HEADS, zc, _log_sigmoid(zc))
    zr = t[3 * DM:] + brow_ref[...]
    row = lax.broadcasted_iota(jnp.int32, zr.shape, 0)
    grow_ref[...] = jnp.where(row < N_HEADS, zr, _log_sigmoid(zr))


def _inproj(xp, xs, g_mix, w1, wkt, bcol, brow):
    tp, ts = xp.shape[0], xs.shape[0]
    n_pt, n_st = tp // TM, ts // TM
    t_all = tp + ts
    tok = lambda w: pl.BlockSpec((TM, w), lambda i: (i, 0))
    tr = lambda r: pl.BlockSpec((r, TM), lambda i: (0, i))
    full = lambda a: pl.BlockSpec(a.shape, lambda i: (0,) * a.ndim)
    out_shape = (
        jax.ShapeDtypeStruct((t_all, DM), F32), jax.ShapeDtypeStruct((t_all, DM), F32),
        jax.ShapeDtypeStruct((t_all, DKV), F32), jax.ShapeDtypeStruct((t_all, DKV), F32),
        jax.ShapeDtypeStruct((DM, t_all), F32), jax.ShapeDtypeStruct((DM, t_all), F32),
        jax.ShapeDtypeStruct((DM, t_all), F32),
        jax.ShapeDtypeStruct((t_all, LANES), F32), jax.ShapeDtypeStruct((2 * N_HEADS, t_all), F32),
    )
    return pl.pallas_call(
        functools.partial(_inproj_kernel, n_pt),
        grid=(n_pt + n_st,),
        in_specs=[
            pl.BlockSpec((TM, D_MODEL), lambda i: (jnp.minimum(i, n_pt - 1), 0)),
            pl.BlockSpec((TM, D_MODEL), lambda i: (jnp.maximum(i - n_pt, 0), 0)),
            full(g_mix), full(w1), full(wkt), full(bcol), full(brow),
        ],
        out_specs=(tok(DM), tok(DM), tok(DKV), tok(DKV), tr(DM), tr(DM), tr(DM),
                   tok(LANES), tr(2 * N_HEADS)),
        out_shape=out_shape,
        compiler_params=_cparams(("arbitrary",), VMEM_LIMIT),
        name="inproj",
    )(xp, xs, g_mix, w1, wkt, bcol, brow)


CT_ROWS = LANES + 2 * SUBLANES


def _cumsum_rows(x, n):
    row = lax.broadcasted_iota(jnp.int32, x.shape, 0)
    sh = 1
    while sh < n:
        x = x + jnp.where(row >= sh, pltpu.roll(x, sh, axis=0), 0.0)
        sh *= 2
    return x


def _mlstm_prompt_kernel(km_ref, qmt_ref, vmt_ref, omt_ref, gcol_ref, grow_ref, ghr_ref,
                         hm_ref, ct_ref, m_ref, ct_s, m_s):
    j = pl.program_id(1)
    L = MLSTM_L
    assert L == LANES

    @pl.when(j == 0)
    def _():
        ct_s[...] = jnp.zeros_like(ct_s)
        m_s[...] = jnp.zeros_like(m_s)

    si = lax.broadcasted_iota(jnp.int32, (L, L), 0)
    ti = lax.broadcasted_iota(jnp.int32, (L, L), 1)
    causal_t = si <= ti
    upper = jnp.where(causal_t, 1.0, 0.0)
    rows_c = lax.broadcasted_iota(jnp.int32, (CT_ROWS, 1), 0)
    rmask_e = (rows_c < HEAD_DIM) | (rows_c == LANES)
    rmask_o = ((rows_c >= HEAD_DIM) & (rows_c < LANES)) | (rows_c == LANES + 1)
    rows_e = lax.broadcasted_iota(jnp.int32, (LANES, 1), 0) < HEAD_DIM
    cols_e = lax.broadcasted_iota(jnp.int32, (1, LANES), 1) < HEAD_DIM
    bdt_mask = (rmask_e & cols_e) | (rmask_o & (~cols_e))
    ones_rows = jnp.where(lax.broadcasted_iota(jnp.int32, (CT_ROWS - LANES, L), 0) < 2, 1.0, 0.0)

    n_chunks = MLSTM_TL // L
    b_rows = jnp.dot(jnp.concatenate([grow_ref[N_HEADS:2 * N_HEADS, c * L:(c + 1) * L] for c in range(n_chunks)],
                                     axis=0), upper, precision=HIGHEST, preferred_element_type=F32)
    for c in range(n_chunks):
        sl = slice(c * L, (c + 1) * L)
        i_row = grow_ref[0:N_HEADS, sl]
        b_row = b_rows[c * N_HEADS:(c + 1) * N_HEADS]
        gc = gcol_ref[sl, :]
        bc_all = _cumsum_rows(gc, L)
        for p in range(N_PAIRS):
            ls = slice(p * LANES, (p + 1) * LANES)
            k2 = km_ref[sl, ls]
            qt2 = qmt_ref[ls, sl]
            qt_e = jnp.where(rows_e, qt2, 0.0).astype(BF16)
            qt_o = jnp.where(rows_e, 0.0, qt2).astype(BF16)
            st2 = jnp.dot(k2.astype(BF16), jnp.concatenate([qt_e, qt_o], axis=1),
                          preferred_element_type=F32)
            ct = ct_s[p]
            rqt = jnp.dot(ct.astype(BF16), qt2.astype(BF16), preferred_element_type=F32)
            pts, mts, inters, wreps, decays, mnews = [], [], [], [], [], []
            for hh in range(2):
                h = 2 * p + hh
                cvec = jnp.broadcast_to(gc[:, h:h + 1] - bc_all[:, N_HEADS + h:N_HEADS + h + 1], (L, L))
                brow = b_row[h:h + 1, :]
                logdt = jnp.where(causal_t, cvec + brow, NEG)
                m_prev = m_s[h:h + 1, 0:1]
                m_inter = m_prev + brow
                m_t = jnp.maximum(m_inter, jnp.max(logdt, axis=0, keepdims=True))
                pts.append((st2[:, hh * L:(hh + 1) * L] * jnp.exp(logdt - m_t)).astype(BF16))
                mts.append(m_t)
                inters.append(jnp.exp(m_inter - m_t))
                m_new = m_t[:, L - 1:L]
                b_last = brow[:, L - 1:L]
                decays.append(jnp.exp(m_prev + b_last - m_new))
                wreps.append(jnp.exp(cvec + (b_last - m_new)))
                mnews.append(m_new)
            vext = jnp.concatenate([vmt_ref[ls, sl], ones_rows], axis=0)
            lhs = jnp.concatenate([jnp.where(rmask_e, vext, 0.0), jnp.where(rmask_o, vext, 0.0)],
                                  axis=1).astype(BF16)
            rt = jnp.dot(lhs, jnp.concatenate(pts, axis=0), preferred_element_type=F32)
            ndt = rt + jnp.where(rmask_e, inters[0], inters[1]) * rqt
            den = jnp.where(rows_e, ndt[LANES:LANES + 1], ndt[LANES + 1:LANES + 2])
            mt2 = jnp.where(rows_e, mts[0], mts[1])
            hvt = ndt[0:LANES] / jnp.maximum(jnp.abs(den), jnp.exp(-mt2))
            sq = hvt * hvt
            ms = jnp.where(rows_e, jnp.sum(sq[0:HEAD_DIM], axis=0, keepdims=True),
                           jnp.sum(sq[HEAD_DIM:LANES], axis=0, keepdims=True)) * (1.0 / HEAD_DIM)
            yt = hvt * lax.rsqrt(ms + RMS_EPS) * ghr_ref[ls, :] * _sigmoid(omt_ref[ls, sl])
            hm_ref[sl, ls] = jnp.transpose(yt).astype(hm_ref.dtype)
            kw = (k2 * jnp.where(cols_e, wreps[0], wreps[1])).astype(BF16)
            upd = jnp.dot(vext.astype(BF16), kw, preferred_element_type=F32)
            ct_s[p] = jnp.where(rmask_e, decays[0], decays[1]) * ct + jnp.where(bdt_mask, upd, 0.0)
            for hh in range(2):
                h = 2 * p + hh
                m_s[h:h + 1, :] = jnp.broadcast_to(mnews[hh], (1, LANES))

    @pl.when(j == pl.num_programs(1) - 1)
    def _():
        ct_ref[0] = ct_s[...]
        m_ref[0] = m_s[...]


def _mlstm_prompt(km, qmt, vmt, omt, gcol, grow, ghr, batch, seq):
    nt = seq // MLSTM_TL
    tokb = lambda w: pl.BlockSpec((MLSTM_TL, w), lambda b, j: (b * nt + j, 0))
    rowb = lambda r: pl.BlockSpec((r, MLSTM_TL), lambda b, j: (0, b * nt + j))
    return pl.pallas_call(
        _mlstm_prompt_kernel,
        grid=(batch, nt),
        in_specs=[tokb(DM), rowb(DM), rowb(DM), rowb(DM), tokb(LANES), rowb(2 * N_HEADS),
                  pl.BlockSpec((DM, LANES), lambda b, j: (0, 0))],
        out_specs=(tokb(DM),
                   pl.BlockSpec((1, N_PAIRS, CT_ROWS, LANES), lambda b, j: (b, 0, 0, 0)),
                   pl.BlockSpec((1, N_HEADS, LANES), lambda b, j: (b, 0, 0))),
        out_shape=(jax.ShapeDtypeStruct((batch * seq, DM), BF16),
                   jax.ShapeDtypeStruct((batch, N_PAIRS, CT_ROWS, LANES), F32),
                   jax.ShapeDtypeStruct((batch, N_HEADS, LANES), F32)),
        scratch_shapes=[pltpu.VMEM((N_PAIRS, CT_ROWS, LANES), F32), pltpu.VMEM((N_HEADS, LANES), F32)],
        compiler_params=_cparams(("arbitrary", "arbitrary"), VMEM_LIMIT),
        name="mlstm_prompt",
    )(km, qmt, vmt, omt, gcol, grow, ghr)


def _mlstm_sample_kernel(n_tok, km_ref, qmt_ref, vmt_ref, omt_ref, gcol_ref, grow_ref, c0_ref, n0_ref, m0_ref,
                         m0t_ref, gh_ref, hm_ref, c_ref, n_ref, mt_ref):
    L = LANES
    NB = L // n_tok
    ti = lax.broadcasted_iota(jnp.int32, (L, L), 0)
    si = lax.broadcasted_iota(jnp.int32, (L, L), 1)
    same = (ti // n_tok) == (si // n_tok)
    causal = same & (ti >= si)
    useg = jnp.where(same & (ti <= si), 1.0, 0.0)
    slast = jnp.where(same & (ti % n_tok == n_tok - 1), 1.0, 0.0)
    expand = jnp.where(ti // n_tok == si, 1.0, 0.0)
    expand_t = jnp.where(ti == si // n_tok, 1.0, 0.0)
    pick = jnp.where((ti // n_tok == si) & (ti % n_tok == n_tok - 1), 1.0, 0.0)
    hdot = lambda a, b: jnp.dot(a, b, precision=HIGHEST, preferred_element_type=F32)

    lane128 = lax.broadcasted_iota(jnp.int32, (L, LANES), 1)
    even128 = lane128 < HEAD_DIM
    lane256 = lax.broadcasted_iota(jnp.int32, (1, 2 * LANES), 1)
    cols_e = (lane256 < HEAD_DIM) | (lane256 == LANES)
    cols_o = ((lane256 >= HEAD_DIM) & (lane256 < LANES)) | (lane256 == LANES + 1)
    rows_e = lax.broadcasted_iota(jnp.int32, (LANES, 1), 0) < HEAD_DIM
    ones_cols = jnp.where(lane128 < 2, 1.0, 0.0)
    bo_r = lax.broadcasted_iota(jnp.int32, (LANES, LANES), 0) // HEAD_DIM
    bo_c = lax.broadcasted_iota(jnp.int32, (LANES, LANES), 1) // HEAD_DIM
    block_ones = jnp.where(bo_r == bo_c, 1.0, 0.0)
    W = NB * LANES
    rb = lax.broadcasted_iota(jnp.int32, (L, W), 0)
    cb = lax.broadcasted_iota(jnp.int32, (L, W), 1)
    own_block = (rb // n_tok) == (cb // LANES)
    bd_tiled = (rb // HEAD_DIM) == ((cb % LANES) // HEAD_DIM)

    grow = grow_ref[...]
    i_row = grow[0:N_HEADS]
    b_row = hdot(grow[N_HEADS:2 * N_HEADS], useg)
    b_last = hdot(b_row, slast)
    a_row = b_last - b_row + i_row
    pos = lax.broadcasted_iota(jnp.int32, a_row.shape, 1) % n_tok
    pm = a_row
    sh = 1
    while sh < n_tok:
        pm = jnp.where(pos >= sh, jnp.maximum(pm, pltpu.roll(pm, sh, axis=1)), pm)
        sh *= 2
    m_carry = hdot(jnp.concatenate([m0t_ref[0], b_row], axis=1), jnp.concatenate([expand_t, slast], axis=0))
    m_new_row = jnp.maximum(m_carry, hdot(pm, slast))
    decay_row = jnp.exp(m_carry - m_new_row)
    w_row = jnp.exp(a_row - m_new_row)
    mt_ref[0] = hdot(m_new_row, pick)
    decay_bh = hdot(decay_row, pick)
    decay_hb = jnp.transpose(decay_bh)[0:NB]

    bc_all = gcol_ref[...]
    rowpos = lax.broadcasted_iota(jnp.int32, bc_all.shape, 0) % n_tok
    sh = 1
    while sh < n_tok:
        bc_all = bc_all + jnp.where(rowpos >= sh, pltpu.roll(bc_all, sh, axis=0), 0.0)
        sh *= 2
    pad_rows = lambda a: jnp.concatenate([a, jnp.zeros((L - NB, a.shape[1]), F32)], axis=0)
    m0_col = hdot(expand, pad_rows(m0_ref[...]))
    rowv_all = i_row - b_row

    for p in range(N_PAIRS):
        ls = slice(p * LANES, (p + 1) * LANES)
        q2f = jnp.transpose(qmt_ref[ls, :])
        q2 = q2f.astype(BF16)
        kt2 = jnp.transpose(km_ref[:, ls])
        v2 = jnp.transpose(vmt_ref[ls, :])
        vext = jnp.concatenate([v2, ones_cols], axis=1)
        kt_e = jnp.where(rows_e, kt2, 0.0).astype(BF16)
        kt_o = jnp.where(rows_e, 0.0, kt2).astype(BF16)
        s2 = jnp.dot(q2, jnp.concatenate([kt_e, kt_o], axis=1), preferred_element_type=F32)
        ps, mts, inters = [], [], []
        for hh in range(2):
            h = 2 * p + hh
            bcol = bc_all[:, N_HEADS + h:N_HEADS + h + 1]
            logd = jnp.where(causal, bcol + rowv_all[h:h + 1, :], NEG)
            m_inter = m0_col[:, h:h + 1] + bcol
            m_t = jnp.maximum(m_inter, jnp.max(logd, axis=1, keepdims=True))
            ps.append((s2[:, hh * L:(hh + 1) * L] * jnp.exp(logd - m_t)).astype(BF16))
            mts.append(m_t)
            inters.append(jnp.exp(m_inter - m_t))
        vstack = jnp.concatenate([jnp.where(cols_e, vext, 0.0), jnp.where(cols_o, vext, 0.0)],
                                 axis=0).astype(BF16)
        r = jnp.dot(jnp.concatenate(ps, axis=1), vstack, preferred_element_type=F32)
        zero = jnp.zeros((HEAD_DIM, HEAD_DIM), F32)
        cstack = jnp.concatenate(
            [jnp.concatenate([jnp.concatenate([c0_ref[b, 2 * p], zero], axis=1),
                              jnp.concatenate([zero, c0_ref[b, 2 * p + 1]], axis=1)], axis=0)
             for b in range(NB)], axis=1)
        rq_all = jnp.where(own_block, jnp.dot(q2, cstack.astype(BF16), preferred_element_type=F32), 0.0)
        rq = rq_all[:, 0:LANES]
        for b in range(1, NB):
            rq = rq + rq_all[:, b * LANES:(b + 1) * LANES]
        n_rows = hdot(expand, pad_rows(n0_ref[:, ls]))
        qn = hdot(q2f * n_rows, block_ones)
        inter2 = jnp.where(even128, inters[0], inters[1])
        num = r[:, 0:LANES] + inter2 * rq
        den = jnp.where(even128, r[:, LANES:LANES + 1], r[:, LANES + 1:LANES + 2]) + inter2 * qn
        mt2 = jnp.where(even128, mts[0], mts[1])
        hv = num / jnp.maximum(jnp.abs(den), jnp.exp(-mt2))
        ms = hdot(hv * hv, block_ones) * (1.0 / HEAD_DIM)
        y = hv * lax.rsqrt(ms + RMS_EPS) * gh_ref[:, ls] * _sigmoid(jnp.transpose(omt_ref[ls, :]))
        hm_ref[:, ls] = y.astype(hm_ref.dtype)
        kw = kt2 * jnp.where(rows_e, w_row[2 * p:2 * p + 1, :], w_row[2 * p + 1:2 * p + 2, :])
        vbd = jnp.where(own_block, jnp.concatenate([v2] * NB, axis=1), 0.0).astype(BF16)
        upd = jnp.dot(kw.astype(BF16), vbd, preferred_element_type=F32)
        upd = jnp.where(bd_tiled, upd, 0.0)
        for b in range(NB):
            bs = slice(b * LANES, (b + 1) * LANES)
            dec_b = jnp.where(rows_e, decay_hb[b:b + 1, 2 * p:2 * p + 1], decay_hb[b:b + 1, 2 * p + 1:2 * p + 2])
            cnew = dec_b * cstack[:, bs] + upd[:, bs]
            c_ref[b, 2 * p] = cnew[0:HEAD_DIM, 0:HEAD_DIM]
            c_ref[b, 2 * p + 1] = cnew[HEAD_DIM:LANES, HEAD_DIM:LANES]
        nsum = jnp.transpose(hdot(kw, expand))[0:NB]
        dec_n = jnp.where(lax.broadcasted_iota(jnp.int32, (NB, LANES), 1) < HEAD_DIM,
                          decay_hb[:, 2 * p:2 * p + 1], decay_hb[:, 2 * p + 1:2 * p + 2])
        n_ref[:, ls] = dec_n * n0_ref[:, ls] + nsum


def _mlstm_sample(km, qmt, vmt, omt, gcol, grow, c0, n0, m0, gh, row0, n_tok):
    nb = c0.shape[0]
    g_nb = LANES // n_tok
    n_g = nb // g_nb
    blk0 = row0 // LANES
    tokb = lambda w: pl.BlockSpec((LANES, w), lambda i: (blk0 + i, 0))
    rowb = lambda r: pl.BlockSpec((r, LANES), lambda i: (0, blk0 + i))
    m0t = jnp.pad(m0.reshape(n_g, g_nb, N_HEADS).transpose(0, 2, 1),
                  ((0, 0), (0, 0), (0, LANES - g_nb)))
    return pl.pallas_call(
        functools.partial(_mlstm_sample_kernel, n_tok),
        grid=(n_g,),
        in_specs=[tokb(DM), rowb(DM), rowb(DM), rowb(DM), tokb(LANES), rowb(2 * N_HEADS),
                  pl.BlockSpec((g_nb, N_HEADS, HEAD_DIM, HEAD_DIM), lambda i: (i, 0, 0, 0)),
                  pl.BlockSpec((g_nb, DM), lambda i: (i, 0)),
                  pl.BlockSpec((g_nb, N_HEADS), lambda i: (i, 0)),
                  pl.BlockSpec((1, N_HEADS, LANES), lambda i: (i, 0, 0)),
                  pl.BlockSpec((1, DM), lambda i: (0, 0))],
        out_specs=(pl.BlockSpec((LANES, DM), lambda i: (i, 0)),
                   pl.BlockSpec((g_nb, N_HEADS, HEAD_DIM, HEAD_DIM), lambda i: (i, 0, 0, 0)),
                   pl.BlockSpec((g_nb, DM), lambda i: (i, 0)),
                   pl.BlockSpec((1, N_HEADS, LANES), lambda i: (i, 0, 0))),
        out_shape=(jax.ShapeDtypeStruct((nb * n_tok, DM), BF16), jax.ShapeDtypeStruct(c0.shape, F32),
                   jax.ShapeDtypeStruct((nb, DM), F32), jax.ShapeDtypeStruct((n_g, N_HEADS, LANES), F32)),
        compiler_params=_cparams(("arbitrary",), VMEM_LIMIT),
        name="mlstm_sample",
    )(km, qmt, vmt, omt, gcol, grow, c0, n0.reshape(nb, DM), m0, m0t, gh)


def _alibi_slope(h):
    return float(np.float32(2.0 ** (-8.0 * (h + 1) / N_HEADS)))


def _dup_halves(x):
    lane = lax.broadcasted_iota(jnp.int32, x.shape, 1)
    xr = pltpu.roll(x, HEAD_DIM, axis=1)
    lo = lane < HEAD_DIM
    return jnp.where(lo, x, xr), jnp.where(lo, xr, x)


def _stack_group_queries(q, g):
    lane = lax.broadcasted_iota(jnp.int32, (q.shape[0], LANES), 1)
    parts = []
    for hh in range(GROUP):
        h = GROUP * g + hh
        blk = q[:, (h // 2) * LANES:(h // 2 + 1) * LANES]
        keep = (lane < HEAD_DIM) if h % 2 == 0 else (lane >= HEAD_DIM)
        parts.append(jnp.where(keep, blk, 0.0))
    return jnp.concatenate(parts, axis=0).astype(BF16)


def _swa_prompt_kernel(sink_ref, q_ref, kp_ref, ko_ref, vp_ref, vo_ref, o_ref):
    j = pl.program_id(1)
    R = WINDOW
    qi = lax.broadcasted_iota(jnp.int32, (R, R), 0)
    kj = lax.broadcasted_iota(jnp.int32, (R, R), 1)
    own = kj <= qi
    distf = jnp.where(own, qi - kj, qi - kj + R).astype(F32)
    lane = lax.broadcasted_iota(jnp.int32, (R, LANES), 1)
    for u in range(SWA_BLOCKS):
        rs = slice(u * R, (u + 1) * R)
        q = q_ref[rs, :]
        k_prev = kp_ref[...] if u == 0 else ko_ref[(u - 1) * R:u * R, :]
        v_prev = vp_ref[...] if u == 0 else vo_ref[(u - 1) * R:u * R, :]
        kd = _dup_halves(jnp.concatenate([k_prev, ko_ref[rs, :]], axis=0))
        vd = _dup_halves(jnp.concatenate([v_prev, vo_ref[rs, :]], axis=0))
        valid = (own | (j > 0)) if u == 0 else None
        outs = []
        for g in range(N_KV):
            qs = _stack_group_queries(q, g)
            s = lax.dot_general(qs, kd[g].astype(BF16), (((1,), (1,)), ((), ())),
                                preferred_element_type=F32) * (HEAD_DIM ** -0.5)
            ps = []
            for hh in range(GROUP):
                h = GROUP * g + hh
                sink = sink_ref[h]
                rows = slice(hh * R, (hh + 1) * R)
                sh = jnp.where(own, s[rows, R:], s[rows, :R]) - _alibi_slope(h) * distf
                if valid is not None:
                    sh = jnp.where(valid, sh, NEG)
                mx = jnp.maximum(jnp.max(sh, axis=1, keepdims=True), sink)
                p = jnp.exp(sh - mx)
                p = p / (jnp.sum(p, axis=1, keepdims=True) + jnp.exp(sink - mx))
                ps.append(jnp.concatenate([jnp.where(own, 0.0, p), jnp.where(own, p, 0.0)], axis=1).astype(BF16))
            o = jnp.dot(jnp.concatenate(ps, axis=0), vd[g].astype(BF16), preferred_element_type=F32)
            for pp in range(GROUP // 2):
                outs.append(jnp.where(lane < HEAD_DIM, o[(2 * pp) * R:(2 * pp + 1) * R],
                                      o[(2 * pp + 1) * R:(2 * pp + 2) * R]))
        o_ref[rs, :] = jnp.concatenate(outs, axis=1).astype(o_ref.dtype)


def _swa_prompt(sinks, qa, ka, va, batch, seq):
    nb = seq // (WINDOW * SWA_BLOCKS)
    own = lambda w: pl.BlockSpec((WINDOW * SWA_BLOCKS, w), lambda b, j: (b * nb + j, 0))
    prev = lambda w: pl.BlockSpec(
        (WINDOW, w), lambda b, j: (b * nb * SWA_BLOCKS + jnp.maximum(j * SWA_BLOCKS - 1, 0), 0))
    return pl.pallas_call(
        _swa_prompt_kernel,
        grid=(batch, nb),
        in_specs=[pl.BlockSpec(memory_space=pltpu.SMEM), own(DM), prev(DKV), own(DKV), prev(DKV), own(DKV)],
        out_specs=own(DM),
        out_shape=jax.ShapeDtypeStruct((batch * seq, DM), BF16),
        compiler_params=_cparams(("arbitrary", "arbitrary"), VMEM_LIMIT),
        name="swa_prompt",
    )(sinks, qa, ka, ka, va, va)


def _swa_decode_kernel(n_tok, sink_ref, q_ref, kn_ref, vn_ref, kc_ref, vc_ref, o_ref, ko_ref, vo_ref):
    W = WINDOW
    L = LANES
    NB = L // n_tok
    for b in range(NB):
        ko_ref[b, 0:W - n_tok, :] = kc_ref[b, n_tok:W, :]
        ko_ref[b, W - n_tok:W, :] = kn_ref[b * n_tok:(b + 1) * n_tok, :]
        vo_ref[b, 0:W - n_tok, :] = vc_ref[b, n_tok:W, :]
        vo_ref[b, W - n_tok:W, :] = vn_ref[b * n_tok:(b + 1) * n_tok, :]
    q = q_ref[...]
    knd, vnd = _dup_halves(kn_ref[...]), _dup_halves(vn_ref[...])
    kcd = _dup_halves(kc_ref[...].reshape(NB * W, DKV))
    vcd = _dup_halves(vc_ref[...].reshape(NB * W, DKV))
    ri = lax.broadcasted_iota(jnp.int32, (L, L), 0)
    ci = lax.broadcasted_iota(jnp.int32, (L, L), 1)
    t_q = ri % n_tok
    dist_n = t_q - ci % n_tok
    valid_n = ((ri // n_tok) == (ci // n_tok)) & (dist_n >= 0)
    dist_c = t_q + W - ci
    valid_c = dist_c < WINDOW
    dnf = dist_n.astype(F32)
    dcf = dist_c.astype(F32)
    own1 = (lax.broadcasted_iota(jnp.int32, (L, NB * W), 0) // n_tok
            == lax.broadcasted_iota(jnp.int32, (L, NB * W), 1) // W)
    own4 = ((lax.broadcasted_iota(jnp.int32, (GROUP * L, NB * W), 0) % L) // n_tok
            == lax.broadcasted_iota(jnp.int32, (GROUP * L, NB * W), 1) // W)
    lane = lax.broadcasted_iota(jnp.int32, (L, LANES), 1)
    nt = (((1,), (1,)), ((), ()))
    outs = []
    for g in range(N_KV):
        qs = _stack_group_queries(q, g)
        sn = lax.dot_general(qs, knd[g].astype(BF16), nt, preferred_element_type=F32) * (HEAD_DIM ** -0.5)
        sc_all = lax.dot_general(qs, kcd[g].astype(BF16), nt, preferred_element_type=F32) * (HEAD_DIM ** -0.5)
        pcs, pns = [], []
        for hh in range(GROUP):
            h = GROUP * g + hh
            sink = sink_ref[h]
            slope = _alibi_slope(h)
            rows = slice(hh * L, (hh + 1) * L)
            blk = jnp.where(own1, sc_all[rows], 0.0)
            sc = blk[:, 0:W]
            for b in range(1, NB):
                sc = sc + blk[:, b * W:(b + 1) * W]
            shc = jnp.where(valid_c, sc - slope * dcf, NEG)
            shn = jnp.where(valid_n, sn[rows] - slope * dnf, NEG)
            mx = jnp.maximum(jnp.maximum(jnp.max(shc, axis=1, keepdims=True),
                                         jnp.max(shn, axis=1, keepdims=True)), sink)
            pc = jnp.exp(shc - mx)
            pn = jnp.exp(shn - mx)
            inv = 1.0 / (jnp.sum(pc, axis=1, keepdims=True) + jnp.sum(pn, axis=1, keepdims=True)
                         + jnp.exp(sink - mx))
            pcs.append(pc * inv)
            pns.append((pn * inv).astype(BF16))
        pc4 = jnp.concatenate(pcs, axis=0)
        p_bd = jnp.where(own4, jnp.concatenate([pc4] * NB, axis=1), 0.0).astype(BF16)
        o = (jnp.dot(p_bd, vcd[g].astype(BF16), preferred_element_type=F32)
             + jnp.dot(jnp.concatenate(pns, axis=0), vnd[g].astype(BF16), preferred_element_type=F32))
        for pp in range(GROUP // 2):
            outs.append(jnp.where(lane < HEAD_DIM, o[(2 * pp) * L:(2 * pp + 1) * L],
                                  o[(2 * pp + 1) * L:(2 * pp + 2) * L]))
    o_ref[...] = jnp.concatenate(outs, axis=1)


def _swa_decode(sinks, qa, ka, va, kc, vc, row0, n_tok):
    nb = kc.shape[0]
    g_nb = LANES // n_tok
    blk0 = row0 // LANES
    tokb = lambda w: pl.BlockSpec((LANES, w), lambda i: (blk0 + i, 0))
    cache = pl.BlockSpec((g_nb, WINDOW, DKV), lambda i: (i, 0, 0))
    return pl.pallas_call(
        functools.partial(_swa_decode_kernel, n_tok),
        grid=(nb // g_nb,),
        in_specs=[pl.BlockSpec(memory_space=pltpu.SMEM), tokb(DM), tokb(DKV), tokb(DKV), cache, cache],
        out_specs=(pl.BlockSpec((LANES, DM), lambda i: (i, 0)), cache, cache),
        out_shape=(jax.ShapeDtypeStruct((nb * n_tok, DM), F32),
                   jax.ShapeDtypeStruct(kc.shape, F32), jax.ShapeDtypeStruct(vc.shape, F32)),
        compiler_params=_cparams(("arbitrary",), VMEM_LIMIT),
        name="swa_decode",
    )(sinks, qa, ka, va, kc, vc)


def _outproj_router_kernel(n_ptiles, xp_ref, xs_ref, hmp_ref, hap_ref, hms_ref, has_ref, wom_ref, woa_ref,
                           g_ref, wrt_ref, br_ref,
                           x1_ref, xg_ref, meta_ref, cnt_ref):
    i = pl.program_id(0)
    is_p = i < n_ptiles
    x = jnp.where(is_p, xp_ref[...], xs_ref[...])
    hm = jnp.where(is_p, hmp_ref[...], hms_ref[...].astype(BF16))
    ha = jnp.where(is_p, hap_ref[...], has_ref[...].astype(BF16))
    x1 = (x + jnp.dot(hm, wom_ref[...], preferred_element_type=F32)
          + jnp.dot(ha, woa_ref[...], preferred_element_type=F32))
    x1_ref[...] = x1
    h2 = _rms(x1, g_ref[...])
    nt = (((1,), (1,)), ((), ()))
    h2_hi = h2.astype(BF16)
    h2_lo = (h2 - h2_hi.astype(F32)).astype(BF16)
    wr = wrt_ref[...]
    wr_hi = wr.astype(BF16)
    wr_lo = (wr - wr_hi.astype(F32)).astype(BF16)
    r_hi = lax.dot_general(jnp.concatenate([wr_hi, wr_lo], axis=0), h2_hi, nt, preferred_element_type=F32)
    logits = (r_hi[0:N_EXPERTS] + r_hi[N_EXPERTS:]
              + lax.dot_general(wr_hi, h2_lo, nt, preferred_element_type=F32)) + br_ref[...]
    eidx = lax.broadcasted_iota(jnp.int32, logits.shape, 0).astype(F32)
    work = logits
    vals, hots = [], []
    for _ in range(TOP_K):
        mv = jnp.max(work, axis=0, keepdims=True)
        sel = jnp.min(jnp.where(work == mv, eidx, float(N_EXPERTS)), axis=0, keepdims=True)
        hot = eidx == sel
        vals.append(mv)
        hots.append(hot)
        work = jnp.where(hot, -jnp.inf, work)
    es = [jnp.exp(v - vals[0]) for v in vals]
    tot = es[0] + es[1] + es[2] + es[3]
    gates = [e / tot for e in es]
    hot_all = jnp.where(hots[0] | hots[1] | hots[2] | hots[3], 1.0, 0.0)
    tm = logits.shape[1]
    su = (lax.broadcasted_iota(jnp.int32, (tm, tm), 0) < lax.broadcasted_iota(jnp.int32, (tm, tm), 1))
    cum = jnp.dot(hot_all.astype(BF16), su.astype(BF16), preferred_element_type=F32)
    cnt = jnp.sum(hot_all, axis=1, keepdims=True)
    cpad = (((cnt.astype(jnp.int32) + (SUBLANES - 1)) // SUBLANES) * SUBLANES).astype(F32)
    lower = (lax.broadcasted_iota(jnp.int32, (N_EXPERTS, N_EXPERTS), 0)
             > lax.broadcasted_iota(jnp.int32, (N_EXPERTS, N_EXPERTS), 1)).astype(F32)
    lstart = jnp.dot(lower, jnp.broadcast_to(cpad, (N_EXPERTS, LANES)), precision=HIGHEST,
                     preferred_element_type=F32)[:, 0:1]
    base = lstart + cum
    lpos = [jnp.sum(jnp.where(hot, base, 0.0), axis=0, keepdims=True) for hot in hots]
    lpi = [p.astype(jnp.int32) for p in lpos]
    h2b = h2.astype(BF16)
    rows_per = GROUP_R // SORT_CHUNKS
    for j in range(SORT_CHUNKS):
        r_iota = lax.broadcasted_iota(jnp.int32, (rows_per, tm), 0) + j * rows_per
        sel01 = jnp.where(r_iota == lpi[0], 1.0, jnp.where(r_iota == lpi[1], 1.0, jnp.where(
            r_iota == lpi[2], 1.0, jnp.where(r_iota == lpi[3], 1.0, 0.0)))).astype(BF16)
        xg_ref[j * rows_per:(j + 1) * rows_per, :] = jnp.dot(sel01, h2b, preferred_element_type=F32)
    meta_ref[...] = jnp.transpose(jnp.concatenate(gates + lpos, axis=0))
    cnt_ref[0] = jnp.broadcast_to(cnt, (N_EXPERTS, LANES))


def _outproj_router(xp, xs, hmp, hap, hms, has, wom, woa, g_ffn, wrt, br):
    tp, ts = xp.shape[0], xs.shape[0]
    n_pt, n_st = tp // ROUTE_T, ts // ROUTE_T
    t_all = tp + ts
    pblk = lambda w: pl.BlockSpec((ROUTE_T, w), lambda i: (jnp.minimum(i, n_pt - 1), 0))
    sblk = lambda w: pl.BlockSpec((ROUTE_T, w), lambda i: (jnp.maximum(i - n_pt, 0), 0))
    full = lambda a: pl.BlockSpec(a.shape, lambda i: (0,) * a.ndim)
    return pl.pallas_call(
        functools.partial(_outproj_router_kernel, n_pt),
        grid=(n_pt + n_st,),
        in_specs=[pblk(D_MODEL), sblk(D_MODEL), pblk(DM), pblk(DM), sblk(DM), sblk(DM),
                  full(wom), full(woa), full(g_ffn), full(wrt), full(br)],
        out_specs=(pl.BlockSpec((ROUTE_T, D_MODEL), lambda i: (i, 0)),
                   pl.BlockSpec((GROUP_R, D_MODEL), lambda i: (i, 0)),
                   pl.BlockSpec((ROUTE_T, 2 * TOP_K), lambda i: (i, 0)),
                   pl.BlockSpec((1, N_EXPERTS, LANES), lambda i: (i, 0, 0))),
        out_shape=(jax.ShapeDtypeStruct((t_all, D_MODEL), F32),
                   jax.ShapeDtypeStruct(((n_pt + n_st) * GROUP_R, D_MODEL), F32),
                   jax.ShapeDtypeStruct((t_all, 2 * TOP_K), F32),
                   jax.ShapeDtypeStruct((n_pt + n_st, N_EXPERTS, LANES), F32)),
        compiler_params=_cparams(("arbitrary",), VMEM_LIMIT),
        name="outproj_router",
    )(xp, xs, hmp, hap, hms, has, wom, woa, g_ffn, wrt, br)


def _expert_kernel(be_ref, na_ref, slot_ref, nxt_ref, ctab_ref, ctab1_ref, ctab2_ref, xg_ref, wgu_ref, bgu_ref,
                   wd_ref, bd_ref, o_ref, xbuf, wgu_f, wd_f, wgu_s, wd_s, xsem, wsem):
    i = pl.program_id(0)
    na = na_ref[0]

    def x_copies(tab_ref, slot):
        return [pltpu.make_async_copy(xg_ref.at[pl.ds(pl.multiple_of(tab_ref[0, 0, c], SUBLANES), SUBLANES)],
                                      xbuf.at[slot, pl.ds(c * SUBLANES, SUBLANES)], xsem.at[slot])
                for c in range(MOE_BM // SUBLANES)]

    def w_copies(e, slot):
        return [pltpu.make_async_copy(wgu_ref.at[e], wgu_f.at[slot], wsem.at[slot]),
                pltpu.make_async_copy(wd_ref.at[e], wd_f.at[slot], wsem.at[slot])]

    @pl.when(i == 0)
    def _():
        for cp in x_copies(ctab_ref, 0) + x_copies(ctab1_ref, 1) + w_copies(be_ref[0], 0):
            cp.start()

    @pl.when(i == na)
    def _():
        for cp in x_copies(ctab_ref, lax.rem(i, X_SLOTS)) + x_copies(ctab1_ref, lax.rem(i + 1, X_SLOTS)):
            cp.wait()

    @pl.when(i < na)
    def _():
        changed = (i == 0) | (be_ref[i] != be_ref[jnp.maximum(i - 1, 0)])
        wslot = slot_ref[i]
        xslot = lax.rem(i, X_SLOTS)

        @pl.when(changed)
        def _():
            for cp in w_copies(be_ref[i], wslot):
                cp.wait()

            @pl.when(nxt_ref[i] >= 0)
            def _():
                for cp in w_copies(nxt_ref[i], 1 - wslot):
                    cp.start(priority=1)

            wgu_s[...] = wgu_f[wslot].astype(BF16)
            wd_s[...] = wd_f[wslot].astype(BF16)

        for cp in x_copies(ctab_ref, xslot):
            cp.wait()

        x = xbuf[xslot].astype(BF16)
        gu = jnp.dot(x, wgu_s[...], preferred_element_type=F32) + bgu_ref[...]
        for cp in x_copies(ctab2_ref, lax.rem(i + 2, X_SLOTS)):
            cp.start()
        gate = jnp.minimum(gu[:, :D_FF], SWIGLU_LIMIT)
        up = jnp.clip(gu[:, D_FF:], -SWIGLU_LIMIT, SWIGLU_LIMIT)
        act = (up + 1.0) * (gate * _sigmoid(gate * SWIGLU_ALPHA))
        o_ref[...] = jnp.dot(act.astype(BF16), wd_s[...], preferred_element_type=F32) + bd_ref[...]


def _expert_ffn(block_exp, n_active, wslot, next_exp, ctab, xg, wgu, bgu, wd, bd):
    n_blocks = ctab.shape[0]
    n_rows = n_blocks * MOE_BM
    nch = MOE_BM // SUBLANES
    tab = lambda d: pl.BlockSpec((1, 1, nch), lambda i, *_: (jnp.minimum(i + d, n_blocks - 1), 0, 0),
                                 memory_space=pltpu.SMEM)
    blk_e = lambda i, be: be[jnp.minimum(i, n_blocks - 1)]
    grid_spec = pltpu.PrefetchScalarGridSpec(
        num_scalar_prefetch=4,
        grid=(n_blocks + 1,),
        in_specs=[tab(0), tab(1), tab(2),
                  pl.BlockSpec(memory_space=pl.ANY),
                  pl.BlockSpec(memory_space=pl.ANY),
                  pl.BlockSpec((None, 1, 2 * D_FF), lambda i, be, *_: (blk_e(i, be), 0, 0)),
                  pl.BlockSpec(memory_space=pl.ANY),
                  pl.BlockSpec((None, 1, D_MODEL), lambda i, be, *_: (blk_e(i, be), 0, 0))],
        out_specs=pl.BlockSpec((MOE_BM, D_MODEL), lambda i, be, na, *_: (jnp.minimum(i, na[0] - 1), 0)),
        scratch_shapes=[pltpu.VMEM((X_SLOTS, MOE_BM, D_MODEL), F32),
                        pltpu.VMEM((2, D_MODEL, 2 * D_FF), F32), pltpu.VMEM((2, D_FF, D_MODEL), F32),
                        pltpu.VMEM((D_MODEL, 2 * D_FF), BF16), pltpu.VMEM((D_FF, D_MODEL), BF16),
                        pltpu.SemaphoreType.DMA((X_SLOTS,)), pltpu.SemaphoreType.DMA((2,))],
    )
    return pl.pallas_call(
        _expert_kernel,
        grid_spec=grid_spec,
        out_shape=jax.ShapeDtypeStruct((n_rows, D_MODEL), F32),
        compiler_params=_cparams(("arbitrary",), VMEM_LIMIT),
        name="moe_experts",
    )(block_exp, n_active, wslot, next_exp, ctab, ctab, ctab, xg, wgu, bgu, wd, bd)


def _combine_kernel(n_ptiles, n_tiles, ctab_ref, ctab1_ref, ctab2_ref, outs_ref, x1_ref, meta_ref, gf_ref,
                    yp_ref, ys_ref, obuf, sem):
    i = pl.program_id(0)

    def copies(tab_ref, s):
        return [pltpu.make_async_copy(outs_ref.at[pl.ds(pl.multiple_of(tab_ref[0, 0, c], SUBLANES), SUBLANES)],
                                      obuf.at[s, pl.ds(c * SUBLANES, SUBLANES)], sem.at[s])
                for c in range(GROUP_R // SUBLANES)]

    @pl.when(i == 0)
    def _():
        for cp in copies(ctab_ref, 0) + copies(ctab1_ref, 1):
            cp.start()

    @pl.when(i == n_tiles)
    def _():
        for cp in copies(ctab_ref, lax.rem(i, X_SLOTS)) + copies(ctab1_ref, lax.rem(i + 1, X_SLOTS)):
            cp.wait()

    @pl.when(i < n_tiles)
    def _():
        _combine_tile(i, n_ptiles, copies, ctab_ref, ctab2_ref, x1_ref, meta_ref, gf_ref, yp_ref, ys_ref, obuf)


def _combine_tile(i, n_ptiles, copies, ctab_ref, ctab2_ref, x1_ref, meta_ref, gf_ref, yp_ref, ys_ref, obuf):
    slot = lax.rem(i, X_SLOTS)
    for cp in copies(ctab_ref, slot):
        cp.wait()
    for c, cp in enumerate(copies(ctab2_ref, lax.rem(i + 2, X_SLOTS))):
        cp.start(priority=c % 2)

    meta = meta_ref[...]
    tm = meta.shape[0]
    r_iota = lax.broadcasted_iota(jnp.int32, (tm, GROUP_R), 1)
    lp = [meta[:, TOP_K + k:TOP_K + k + 1].astype(jnp.int32) for k in range(TOP_K)]
    gk = [meta[:, k:k + 1] for k in range(TOP_K)]
    gsel = jnp.where(r_iota == lp[0], gk[0], jnp.where(r_iota == lp[1], gk[1], jnp.where(
        r_iota == lp[2], gk[2], jnp.where(r_iota == lp[3], gk[3], 0.0))))
    sel01 = jnp.where(gsel != 0.0, 1.0, 0.0).astype(BF16)
    rg_row = jnp.sum(gsel, axis=0, keepdims=True)
    rg_col = jnp.transpose(jnp.broadcast_to(rg_row, (SUBLANES, GROUP_R)))[:, 0:1]
    og = (obuf[slot] * rg_col).astype(BF16)
    acc = x1_ref[...] + jnp.dot(sel01, og, preferred_element_type=F32)
    y = _rms(acc, gf_ref[...])

    @pl.when(i < n_ptiles)
    def _():
        yp_ref[...] = y

    @pl.when(i >= n_ptiles)
    def _():
        ys_ref[...] = y


def _combine(ctab, outs, x1, meta, g_final, tp, ts):
    n_pt, n_st = tp // ROUTE_T, ts // ROUTE_T
    n = n_pt + n_st
    nch = GROUP_R // SUBLANES
    last = lambda i: jnp.minimum(i, n - 1)
    tab = lambda d: pl.BlockSpec((1, 1, nch), lambda i: (last(i + d), 0, 0), memory_space=pltpu.SMEM)
    return pl.pallas_call(
        functools.partial(_combine_kernel, n_pt, n),
        grid=(n + 1,),
        in_specs=[tab(0), tab(1), tab(2),
                  pl.BlockSpec(memory_space=pl.ANY),
                  pl.BlockSpec((ROUTE_T, D_MODEL), lambda i: (last(i), 0)),
                  pl.BlockSpec((ROUTE_T, 2 * TOP_K), lambda i: (last(i), 0)),
                  pl.BlockSpec((1, D_MODEL), lambda i: (0, 0))],
        out_specs=(pl.BlockSpec((ROUTE_T, D_MODEL), lambda i: (jnp.minimum(i, n_pt - 1), 0)),
                   pl.BlockSpec((ROUTE_T, D_MODEL), lambda i: (jnp.maximum(last(i) - n_pt, 0), 0))),
        out_shape=(jax.ShapeDtypeStruct((tp, D_MODEL), F32), jax.ShapeDtypeStruct((ts, D_MODEL), F32)),
        scratch_shapes=[pltpu.VMEM((X_SLOTS, GROUP_R, D_MODEL), F32), pltpu.SemaphoreType.DMA((X_SLOTS,))],
        compiler_params=_cparams(("arbitrary",), VMEM_LIMIT),
        name="moe_combine",
    )(ctab, ctab, ctab, outs, x1, meta, g_final)


def kernel(x_prompt, x_sample, cache_swa_k, cache_swa_v, state_mlstm_c, state_mlstm_n, state_mlstm_m,
           g_mix, w_in, b_igate, b_fgate, g_head, attn_sinks, w_out, g_ffn, w_router, b_router,
           w_gate_up, b_gate_up, w_down, b_down, g_final):
    assert w_in.shape[0] == 1, "single-layer problem"
    B, S, _ = x_prompt.shape
    Bd, Tn, _ = x_sample.shape
    tp, ts = B * S, Bd * Tn
    t_all = tp + ts
    xp = x_prompt.reshape(tp, D_MODEL)
    xs = x_sample.reshape(ts, D_MODEL)

    w = w_in[0]
    o = np.cumsum([0, DM, DM, DM, DM, N_HEADS, N_HEADS, DM, DKV, DKV])
    col = lambda a: w[:, int(o[a]):int(o[a + 1])]
    wgates = jnp.concatenate([col(4), col(5)], axis=1)
    w1 = jnp.concatenate([col(1), col(6), col(7), col(8), jnp.pad(wgates, ((0, 0), (0, LANES - 2 * N_HEADS)))],
                         axis=1).astype(BF16)
    wkt = jnp.concatenate([col(0), col(2), col(3), wgates], axis=1).T.astype(BF16)
    bg = jnp.concatenate([b_igate[0], b_fgate[0]]).astype(F32)
    bcol = jnp.pad(bg, (0, LANES - 2 * N_HEADS)).reshape(1, LANES)
    brow = bg.reshape(2 * N_HEADS, 1)

    km, qa, ka, va, qmt, vmt, omt, gcol, grow = _inproj(xp, xs, g_mix[0].reshape(1, D_MODEL), w1, wkt, bcol, brow)

    gh = g_head[0].astype(F32)
    sinks = attn_sinks[0].astype(F32)

    hm_p, ctp, m_p = _mlstm_prompt(km, qmt, vmt, omt, gcol, grow,
                                   jnp.broadcast_to(gh.reshape(DM, 1), (DM, LANES)), B, S)
    ha_p = _swa_prompt(sinks, qa, ka, va, B, S)

    hm_s, c_s, n_s, mt_s = _mlstm_sample(km, qmt, vmt, omt, gcol, grow, state_mlstm_c[0], state_mlstm_n[0],
                                         state_mlstm_m[0], gh.reshape(1, DM), tp, Tn)
    n_s = n_s.reshape(Bd, N_HEADS, HEAD_DIM)
    m_s = mt_s[:, :, :LANES // Tn].transpose(0, 2, 1).reshape(Bd, N_HEADS)
    ha_s, k_s, v_s = _swa_decode(sinks, qa, ka, va, cache_swa_k[0].reshape(Bd, WINDOW, DKV),
                                 cache_swa_v[0].reshape(Bd, WINDOW, DKV), tp, Tn)

    wo = w_out[0].astype(BF16)
    x1, xg, meta, cnt = _outproj_router(
        xp, xs, hm_p, ha_p, hm_s, ha_s, wo[:DM], wo[DM:], g_ffn[0].reshape(1, D_MODEL),
        w_router[0].T, b_router[0].reshape(N_EXPERTS, 1))

    i32 = jnp.int32
    n_tiles = t_all // ROUTE_T
    max_rows = t_all * TOP_K + n_tiles * N_EXPERTS * (SUBLANES - 1) + N_EXPERTS * (MOE_BM - 1)
    n_blocks = -(-max_rows // MOE_BM)
    cpad = (cnt[:, :, 0].astype(i32) + (SUBLANES - 1)) // SUBLANES * SUBLANES
    lstart = jnp.cumsum(cpad, axis=1) - cpad
    goff = jnp.cumsum(cpad, axis=0) - cpad
    padded = (jnp.sum(cpad, axis=0) + MOE_BM - 1) // MOE_BM * MOE_BM
    pad_end = jnp.cumsum(padded)
    seg_begin = (pad_end - padded)[None, :] + goff
    n_active = (pad_end[-1] // MOE_BM).astype(i32)
    blk = jnp.minimum(jnp.arange(n_blocks, dtype=i32), n_active - 1)
    block_exp = jnp.minimum(jnp.sum((pad_end[None, :] <= (blk * MOE_BM)[:, None]).astype(i32), axis=1),
                            N_EXPERTS - 1)
    e_ids = jnp.arange(N_EXPERTS, dtype=i32)
    nonempty = padded > 0
    nxt_e = jnp.min(jnp.where((e_ids[None, :] > e_ids[:, None]) & nonempty[None, :], e_ids[None, :], N_EXPERTS),
                    axis=1)
    nxt_e = jnp.where(nxt_e == N_EXPERTS, -1, nxt_e)
    ord_e = jnp.cumsum(nonempty.astype(i32)) - 1
    be_hot = block_exp[:, None] == e_ids[None, :]
    next_exp = jnp.sum(jnp.where(be_hot, nxt_e[None, :], 0), axis=1).astype(i32)
    wslot = (jnp.sum(jnp.where(be_hot, ord_e[None, :], 0), axis=1) % 2).astype(i32)

    seg_src = jnp.arange(n_tiles, dtype=i32)[:, None] * GROUP_R + lstart
    sb, sl, ss = seg_begin.reshape(-1), cpad.reshape(-1), seg_src.reshape(-1)
    rc = jnp.arange(n_blocks * MOE_BM // SUBLANES, dtype=i32)[:, None] * SUBLANES
    inseg = (sb[None, :] <= rc) & (rc < (sb + sl)[None, :])
    ctab_e = jnp.where(jnp.any(inseg, axis=1), jnp.sum(jnp.where(inseg, (ss - sb)[None, :] + rc, 0), axis=1),
                       GROUP_R - SUBLANES)
    lr = jnp.arange(GROUP_R // SUBLANES, dtype=i32)[None, :, None] * SUBLANES
    inl = (lstart[:, None, :] <= lr) & (lr < (lstart + cpad)[:, None, :])
    ctab_c = jnp.sum(jnp.where(inl, (seg_begin - lstart)[:, None, :] + lr, 0), axis=2)

    outs = _expert_ffn(block_exp, n_active.reshape(1), wslot, next_exp,
                       ctab_e.astype(i32).reshape(n_blocks, 1, MOE_BM // SUBLANES), xg, w_gate_up[0],
                       b_gate_up[0].reshape(N_EXPERTS, 1, 2 * D_FF), w_down[0],
                       b_down[0].reshape(N_EXPERTS, 1, D_MODEL))
    y_p, y_s = _combine(ctab_c.astype(i32).reshape(n_tiles, 1, GROUP_R // SUBLANES), outs, x1, meta,
                        g_final.reshape(1, D_MODEL), tp, ts)

    kv_tail = lambda a: jnp.concatenate([a[(b + 1) * S - WINDOW:(b + 1) * S] for b in range(B)], axis=0).reshape(
        1, B, WINDOW, N_KV, HEAD_DIM)
    c_e = ctp[:, :, :HEAD_DIM, :HEAD_DIM]
    c_o = ctp[:, :, HEAD_DIM:LANES, HEAD_DIM:]
    c_p = jnp.swapaxes(jnp.stack([c_e, c_o], axis=2), -1, -2).reshape(B, N_HEADS, HEAD_DIM, HEAD_DIM)
    n_p = (ctp[:, :, LANES:LANES + 2, :HEAD_DIM] + ctp[:, :, LANES:LANES + 2, HEAD_DIM:]).reshape(B, N_HEADS, HEAD_DIM)
    return (y_p.reshape(B, S, D_MODEL), y_s.reshape(Bd, Tn, D_MODEL),
            kv_tail(ka), kv_tail(va), c_p[None], n_p[None], m_p[:, :, 0][None],
            k_s.reshape(Bd, WINDOW, N_KV, HEAD_DIM)[None], v_s.reshape(Bd, WINDOW, N_KV, HEAD_DIM)[None],
            c_s[None], n_s[None], m_s[None])
```
